```python
import math
import jax, jax.numpy as jnp
from jax import lax
import numpy as np

D_MODEL = 1024
BATCH = 16
SEQ = 2048
DEPTH = 4

CHUNK = 64
Q_BLOCK = 128
N_HEADS = 8
HEAD_DIM = 128
D_ATT = N_HEADS * HEAD_DIM
D_RNN = D_MODEL
N_RNN_BLOCKS = 8
RNN_BLOCK = D_RNN // N_RNN_BLOCKS
CONV_WIDTH = 4
RG_C = 8.0
D_FF = -(-8 * D_MODEL // (3 * 256)) * 256
D_PLE = 256
DN_ALPHA = float((2 * DEPTH) ** 0.25)
DN_BETA = float((8 * DEPTH) ** -0.25)
LN_EPS = 1e-5

COLS = [D_ATT, D_ATT, D_ATT, N_HEADS, D_RNN, D_RNN, D_MODEL, D_MODEL]
N_IN = sum(COLS)
SPLITS = list(np.cumsum(COLS)[:-1].tolist())

kernel_name = "hybrid_fox_rglru_deepnorm_encoder"


def layer_norm(x, g, b):
    xf = x.astype(jnp.float32)
    mu = jnp.mean(xf, axis=-1, keepdims=True)
    var = jnp.mean(jnp.square(xf - mu), axis=-1, keepdims=True)
    y = (xf - mu) * lax.rsqrt(var + LN_EPS)
    return (y * g.astype(jnp.float32) + b.astype(jnp.float32)).astype(x.dtype)


def forgetting_attention(q, k, v, logf):
    S = q.shape[1]
    scale = 1.0 / math.sqrt(HEAD_DIM)
    c = jnp.cumsum(logf, axis=1).transpose(0, 2, 1)
    outs = []
    for blk in range(S // Q_BLOCK):
        t0, t1 = blk * Q_BLOCK, (blk + 1) * Q_BLOCK
        qb, kb, vb = q[:, t0:t1], k[:, :t1], v[:, :t1]
        s = jnp.einsum('bqhd,bkhd->bhqk', qb, kb).astype(jnp.float32) * scale
        s = s + c[:, :, t0:t1, None] - c[:, :, None, :t1]
        q_pos = t0 + jnp.arange(Q_BLOCK)[:, None]
        k_pos = jnp.arange(t1)[None, :]
        s = jnp.where(k_pos <= q_pos, s, -jnp.inf)
        pr = jax.nn.softmax(s, axis=-1)
        outs.append(jnp.einsum('bhqk,bkhd->bqhd', pr.astype(vb.dtype), vb))
    return jnp.concatenate(outs, axis=1)


def causal_depthwise_conv(x, w, b):
    y = lax.conv_general_dilated(
        x, w[:, None, :].astype(x.dtype), window_strides=(1,),
        padding=[(CONV_WIDTH - 1, 0)],
        dimension_numbers=('NWC', 'WIO', 'NWC'),
        feature_group_count=x.shape[-1])
    return y + b


def block_diag_linear(x, w, b):
    B, S, _ = x.shape
    xr = x.reshape(B, S, N_RNN_BLOCKS, RNN_BLOCK)
    return jnp.einsum('bsnc,ncd->bsnd', xr, w).reshape(B, S, D_RNN) + b


def _lin_rec_combine(e1, e2):
    a1, b1 = e1
    a2, b2 = e2
    return a1 * a2, a2 * b1 + b2


def rg_lru_branch(rx, ry, conv_w, conv_b, w_a, b_a, w_x, b_x, lam):
    xc = causal_depthwise_conv(rx, conv_w, conv_b)
    r = jax.nn.sigmoid(block_diag_linear(xc, w_a, b_a).astype(jnp.float32))
    i = jax.nn.sigmoid(block_diag_linear(xc, w_x, b_x).astype(jnp.float32))
    log_a = -RG_C * jax.nn.softplus(-lam.astype(jnp.float32)) * r
    a = jnp.exp(log_a)
    mult = jnp.sqrt(-jnp.expm1(2.0 * log_a))
    u = mult * (i * xc.astype(jnp.float32))
    _, h = lax.associative_scan(_lin_rec_combine, (a, u), axis=1)
    return h.astype(rx.dtype) * jax.nn.gelu(ry)


def hybrid_mixer(u, w_in, b_forget, conv_w, conv_b, w_a, b_a, w_x, b_x, lam,
                 w_br_att, w_br_rnn, b_merge, w_out):
    B, S, _ = u.shape
    z = u @ w_in
    q, k, v, f_logit, rx, ry, ga, gb = jnp.split(z, SPLITS, axis=-1)
    q = q.reshape(B, S, N_HEADS, HEAD_DIM)
    k = k.reshape(B, S, N_HEADS, HEAD_DIM)
    v = v.reshape(B, S, N_HEADS, HEAD_DIM)
    logf = jax.nn.log_sigmoid((f_logit + b_forget).astype(jnp.float32))
    att = forgetting_attention(q, k, v, logf).reshape(B, S, D_ATT)
    rnn = rg_lru_branch(rx, ry, conv_w, conv_b, w_a, b_a, w_x, b_x, lam)
    ya = att @ w_br_att
    yb = rnn @ w_br_rnn
    merged = jax.nn.sigmoid(ga + b_merge[0]) * ya + jax.nn.sigmoid(gb + b_merge[1]) * yb
    return merged @ w_out


def swiglu(x, w_in, w_out):
    hg, hu = jnp.split(x @ w_in, 2, axis=-1)
    return (jax.nn.silu(hg) * hu) @ w_out


def _fwd_setup_inputs(seed: int = 0) -> dict:
    key = jax.random.key(seed)
    ks = jax.random.split(key, 32)
    f32 = jnp.float32
    L, D = DEPTH, D_MODEL

    def nrm(k, shape, scale):
        return jax.random.normal(k, shape, f32) * scale

    u_lam = jax.random.uniform(ks[10], (L, D_RNN), f32, 0.9, 0.999)
    a0 = u_lam ** (1.0 / RG_C)
    rg_lambda = jnp.log(a0) - jnp.log1p(-a0)

    return {
        "x": nrm(ks[0], (BATCH, SEQ, D), 1.0),
        "p": nrm(ks[1], (DEPTH, BATCH, SEQ, D_PLE), 1.0),
        "ln_in_g": 1.0 + nrm(ks[2], (D,), 0.02),
        "ln_in_b": nrm(ks[3], (D,), 0.02),
        "w_in": nrm(ks[4], (L, D, N_IN), D ** -0.5),
        "b_forget": jax.random.uniform(ks[5], (L, N_HEADS), f32, 1.0, 6.0),
        "conv_w": nrm(ks[6], (L, CONV_WIDTH, D_RNN), CONV_WIDTH ** -0.5),
        "conv_b": nrm(ks[7], (L, D_RNN), 0.02),
        "rg_w_a": nrm(ks[8], (L, N_RNN_BLOCKS, RNN_BLOCK, RNN_BLOCK), RNN_BLOCK ** -0.5),
        "rg_b_a": nrm(ks[9], (L, D_RNN), 0.02),
        "rg_w_x": nrm(ks[11], (L, N_RNN_BLOCKS, RNN_BLOCK, RNN_BLOCK), RNN_BLOCK ** -0.5),
        "rg_b_x": nrm(ks[12], (L, D_RNN), 0.02),
        "rg_lambda": rg_lambda,
        "w_branch_att": nrm(ks[13], (L, D_ATT, D), D_ATT ** -0.5),
        "w_branch_rnn": nrm(ks[14], (L, D_RNN, D), D_RNN ** -0.5),
        "b_merge": nrm(ks[15], (L, 2, D), 0.02),
        "w_out": nrm(ks[16], (L, D, D), D ** -0.5 * DN_BETA),
        "ln_mix_g": 1.0 + nrm(ks[17], (L, D), 0.02),
        "ln_mix_b": nrm(ks[18], (L, D), 0.02),
        "w_ffn_in": nrm(ks[19], (L, D, 2 * D_FF), D ** -0.5),
        "w_ffn_out": nrm(ks[20], (L, D_FF, D), D_FF ** -0.5 * DN_BETA),
        "ln_ffn_g": 1.0 + nrm(ks[21], (L, D), 0.02),
        "ln_ffn_b": nrm(ks[22], (L, D), 0.02),
        "w_ple": nrm(ks[23], (L, D_PLE, D), D_PLE ** -0.5 * DN_BETA),
        "w_ple_gate": nrm(ks[24], (L, D, D), D ** -0.5),
        "b_ple_gate": nrm(ks[25], (L, D), 0.02),
        "ln_ple_g": 1.0 + nrm(ks[26], (L, D), 0.02),
        "ln_ple_b": nrm(ks[27], (L, D), 0.02),
    }


def _fwd_reference(x, p, ln_in_g, ln_in_b, w_in, b_forget, conv_w, conv_b, rg_w_a, rg_b_a,
              rg_w_x, rg_b_x, rg_lambda, w_branch_att, w_branch_rnn, b_merge, w_out,
              ln_mix_g, ln_mix_b, w_ffn_in, w_ffn_out, ln_ffn_g, ln_ffn_b,
              w_ple, w_ple_gate, b_ple_gate, ln_ple_g, ln_ple_b):
    h = layer_norm(x, ln_in_g, ln_in_b)
    for l in range(DEPTH):
        m = hybrid_mixer(h, w_in[l], b_forget[l], conv_w[l], conv_b[l],
                         rg_w_a[l], rg_b_a[l], rg_w_x[l], rg_b_x[l], rg_lambda[l],
                         w_branch_att[l], w_branch_rnn[l], b_merge[l], w_out[l])
        h = layer_norm(DN_ALPHA * h + m, ln_mix_g[l], ln_mix_b[l])
        f = swiglu(h, w_ffn_in[l], w_ffn_out[l])
        h = layer_norm(DN_ALPHA * h + f, ln_ffn_g[l], ln_ffn_b[l])
        e = jax.nn.sigmoid(h @ w_ple_gate[l] + b_ple_gate[l]) * (p[l] @ w_ple[l])
        h = layer_norm(DN_ALPHA * h + e, ln_ple_g[l], ln_ple_b[l])
    return h


import jax as _jax
import jax.numpy as _jnp

TWIN_FORMAT = 'train_step'
FWD_PARAMS = ['x', 'p', 'ln_in_g', 'ln_in_b', 'w_in', 'b_forget', 'conv_w', 'conv_b', 'rg_w_a', 'rg_b_a', 'rg_w_x', 'rg_b_x', 'rg_lambda', 'w_branch_att', 'w_branch_rnn', 'b_merge', 'w_out', 'ln_mix_g', 'ln_mix_b', 'w_ffn_in', 'w_ffn_out', 'ln_ffn_g', 'ln_ffn_b', 'w_ple', 'w_ple_gate', 'b_ple_gate', 'ln_ple_g', 'ln_ple_b']
TWIN_WEIGHTS = ['ln_in_g', 'ln_in_b', 'w_in', 'b_forget', 'conv_w', 'conv_b', 'rg_w_a', 'rg_b_a', 'rg_w_x', 'rg_b_x', 'rg_lambda', 'w_branch_att', 'w_branch_rnn', 'b_merge', 'w_out', 'ln_mix_g', 'ln_mix_b', 'w_ffn_in', 'w_ffn_out', 'ln_ffn_g', 'ln_ffn_b', 'w_ple', 'w_ple_gate', 'b_ple_gate', 'ln_ple_g', 'ln_ple_b']
TWIN_DIFF_INPUT = 'x'
TWIN_INPUTS = ['x', 'p', 'ln_in_g', 'ln_in_b', 'w_in', 'b_forget', 'conv_w', 'conv_b', 'rg_w_a', 'rg_b_a', 'rg_w_x', 'rg_b_x', 'rg_lambda', 'w_branch_att', 'w_branch_rnn', 'b_merge', 'w_out', 'ln_mix_g', 'ln_mix_b', 'w_ffn_in', 'w_ffn_out', 'ln_ffn_g', 'ln_ffn_b', 'w_ple', 'w_ple_gate', 'b_ple_gate', 'ln_ple_g', 'ln_ple_b', 'loss_target', 'm_ln_in_g', 'm_ln_in_b', 'm_w_in', 'm_b_forget', 'm_conv_w', 'm_conv_b', 'm_rg_w_a', 'm_rg_b_a', 'm_rg_w_x', 'm_rg_b_x', 'm_rg_lambda', 'm_w_branch_att', 'm_w_branch_rnn', 'm_b_merge', 'm_w_out', 'm_ln_mix_g', 'm_ln_mix_b', 'm_w_ffn_in', 'm_w_ffn_out', 'm_ln_ffn_g', 'm_ln_ffn_b', 'm_w_ple', 'm_w_ple_gate', 'm_b_ple_gate', 'm_ln_ple_g', 'm_ln_ple_b', 'v_ln_in_g', 'v_ln_in_b', 'v_w_in', 'v_b_forget', 'v_conv_w', 'v_conv_b', 'v_rg_w_a', 'v_rg_b_a', 'v_rg_w_x', 'v_rg_b_x', 'v_rg_lambda', 'v_w_branch_att', 'v_w_branch_rnn', 'v_b_merge', 'v_w_out', 'v_ln_mix_g', 'v_ln_mix_b', 'v_w_ffn_in', 'v_w_ffn_out', 'v_ln_ffn_g', 'v_ln_ffn_b', 'v_w_ple', 'v_w_ple_gate', 'v_b_ple_gate', 'v_ln_ple_g', 'v_ln_ple_b']
TWIN_OUTPUTS = ['loss', 'grad_x', 'grad_ln_in_g', 'grad_ln_in_b', 'grad_w_in', 'grad_b_forget', 'grad_conv_w', 'grad_conv_b', 'grad_rg_w_a', 'grad_rg_b_a', 'grad_rg_w_x', 'grad_rg_b_x', 'grad_rg_lambda', 'grad_w_branch_att', 'grad_w_branch_rnn', 'grad_b_merge', 'grad_w_out', 'grad_ln_mix_g', 'grad_ln_mix_b', 'grad_w_ffn_in', 'grad_w_ffn_out', 'grad_ln_ffn_g', 'grad_ln_ffn_b', 'grad_w_ple', 'grad_w_ple_gate', 'grad_b_ple_gate', 'grad_ln_ple_g', 'grad_ln_ple_b', 'delta_ln_in_g', 'delta_ln_in_b', 'delta_w_in', 'delta_b_forget', 'delta_conv_w', 'delta_conv_b', 'delta_rg_w_a', 'delta_rg_b_a', 'delta_rg_w_x', 'delta_rg_b_x', 'delta_rg_lambda', 'delta_w_branch_att', 'delta_w_branch_rnn', 'delta_b_merge', 'delta_w_out', 'delta_ln_mix_g', 'delta_ln_mix_b', 'delta_w_ffn_in', 'delta_w_ffn_out', 'delta_ln_ffn_g', 'delta_ln_ffn_b', 'delta_w_ple', 'delta_w_ple_gate', 'delta_b_ple_gate', 'delta_ln_ple_g', 'delta_ln_ple_b', 'new_m_ln_in_g', 'new_m_ln_in_b', 'new_m_w_in', 'new_m_b_forget', 'new_m_conv_w', 'new_m_conv_b', 'new_m_rg_w_a', 'new_m_rg_b_a', 'new_m_rg_w_x', 'new_m_rg_b_x', 'new_m_rg_lambda', 'new_m_w_branch_att', 'new_m_w_branch_rnn', 'new_m_b_merge', 'new_m_w_out', 'new_m_ln_mix_g', 'new_m_ln_mix_b', 'new_m_w_ffn_in', 'new_m_w_ffn_out', 'new_m_ln_ffn_g', 'new_m_ln_ffn_b', 'new_m_w_ple', 'new_m_w_ple_gate', 'new_m_b_ple_gate', 'new_m_ln_ple_g', 'new_m_ln_ple_b', 'new_v_ln_in_g', 'new_v_ln_in_b', 'new_v_w_in', 'new_v_b_forget', 'new_v_conv_w', 'new_v_conv_b', 'new_v_rg_w_a', 'new_v_rg_b_a', 'new_v_rg_w_x', 'new_v_rg_b_x', 'new_v_rg_lambda', 'new_v_w_branch_att', 'new_v_w_branch_rnn', 'new_v_b_merge', 'new_v_w_out', 'new_v_ln_mix_g', 'new_v_ln_mix_b', 'new_v_w_ffn_in', 'new_v_w_ffn_out', 'new_v_ln_ffn_g', 'new_v_ln_ffn_b', 'new_v_w_ple', 'new_v_w_ple_gate', 'new_v_b_ple_gate', 'new_v_ln_ple_g', 'new_v_ln_ple_b']
TWIN_LEAF_KINDS = {'loss': 'loss', 'grad_x': 'grad_x', 'grad_ln_in_g': 'grad_w', 'grad_ln_in_b': 'grad_w', 'grad_w_in': 'grad_w', 'grad_b_forget': 'grad_w', 'grad_conv_w': 'grad_w', 'grad_conv_b': 'grad_w', 'grad_rg_w_a': 'grad_w', 'grad_rg_b_a': 'grad_w', 'grad_rg_w_x': 'grad_w', 'grad_rg_b_x': 'grad_w', 'grad_rg_lambda': 'grad_w', 'grad_w_branch_att': 'grad_w', 'grad_w_branch_rnn': 'grad_w', 'grad_b_merge': 'grad_w', 'grad_w_out': 'grad_w', 'grad_ln_mix_g': 'grad_w', 'grad_ln_mix_b': 'grad_w', 'grad_w_ffn_in': 'grad_w', 'grad_w_ffn_out': 'grad_w', 'grad_ln_ffn_g': 'grad_w', 'grad_ln_ffn_b': 'grad_w', 'grad_w_ple': 'grad_w', 'grad_w_ple_gate': 'grad_w', 'grad_b_ple_gate': 'grad_w', 'grad_ln_ple_g': 'grad_w', 'grad_ln_ple_b': 'grad_w', 'delta_ln_in_g': 'delta_w', 'delta_ln_in_b': 'delta_w', 'delta_w_in': 'delta_w', 'delta_b_forget': 'delta_w', 'delta_conv_w': 'delta_w', 'delta_conv_b': 'delta_w', 'delta_rg_w_a': 'delta_w', 'delta_rg_b_a': 'delta_w', 'delta_rg_w_x': 'delta_w', 'delta_rg_b_x': 'delta_w', 'delta_rg_lambda': 'delta_w', 'delta_w_branch_att': 'delta_w', 'delta_w_branch_rnn': 'delta_w', 'delta_b_merge': 'delta_w', 'delta_w_out': 'delta_w', 'delta_ln_mix_g': 'delta_w', 'delta_ln_mix_b': 'delta_w', 'delta_w_ffn_in': 'delta_w', 'delta_w_ffn_out': 'delta_w', 'delta_ln_ffn_g': 'delta_w', 'delta_ln_ffn_b': 'delta_w', 'delta_w_ple': 'delta_w', 'delta_w_ple_gate': 'delta_w', 'delta_b_ple_gate': 'delta_w', 'delta_ln_ple_g': 'delta_w', 'delta_ln_ple_b': 'delta_w', 'new_m_ln_in_g': 'new_m', 'new_m_ln_in_b': 'new_m', 'new_m_w_in': 'new_m', 'new_m_b_forget': 'new_m', 'new_m_conv_w': 'new_m', 'new_m_conv_b': 'new_m', 'new_m_rg_w_a': 'new_m', 'new_m_rg_b_a': 'new_m', 'new_m_rg_w_x': 'new_m', 'new_m_rg_b_x': 'new_m', 'new_m_rg_lambda': 'new_m', 'new_m_w_branch_att': 'new_m', 'new_m_w_branch_rnn': 'new_m', 'new_m_b_merge': 'new_m', 'new_m_w_out': 'new_m', 'new_m_ln_mix_g': 'new_m', 'new_m_ln_mix_b': 'new_m', 'new_m_w_ffn_in': 'new_m', 'new_m_w_ffn_out': 'new_m', 'new_m_ln_ffn_g': 'new_m', 'new_m_ln_ffn_b': 'new_m', 'new_m_w_ple': 'new_m', 'new_m_w_ple_gate': 'new_m', 'new_m_b_ple_gate': 'new_m', 'new_m_ln_ple_g': 'new_m', 'new_m_ln_ple_b': 'new_m', 'new_v_ln_in_g': 'new_v', 'new_v_ln_in_b': 'new_v', 'new_v_w_in': 'new_v', 'new_v_b_forget': 'new_v', 'new_v_conv_w': 'new_v', 'new_v_conv_b': 'new_v', 'new_v_rg_w_a': 'new_v', 'new_v_rg_b_a': 'new_v', 'new_v_rg_w_x': 'new_v', 'new_v_rg_b_x': 'new_v', 'new_v_rg_lambda': 'new_v', 'new_v_w_branch_att': 'new_v', 'new_v_w_branch_rnn': 'new_v', 'new_v_b_merge': 'new_v', 'new_v_w_out': 'new_v', 'new_v_ln_mix_g': 'new_v', 'new_v_ln_mix_b': 'new_v', 'new_v_w_ffn_in': 'new_v', 'new_v_w_ffn_out': 'new_v', 'new_v_ln_ffn_g': 'new_v', 'new_v_ln_ffn_b': 'new_v', 'new_v_w_ple': 'new_v', 'new_v_w_ple_gate': 'new_v', 'new_v_b_ple_gate': 'new_v', 'new_v_ln_ple_g': 'new_v', 'new_v_ln_ple_b': 'new_v'}


def _forward(args):
    return _fwd_reference(*[args[k] for k in FWD_PARAMS])


def _output_shape():
    out = _jax.eval_shape(lambda: _forward(_fwd_setup_inputs(0)))
    return out.shape, out.dtype

N_MICROBATCH = 1
ADAM_LR = 0.001
ADAM_B1 = 0.9
ADAM_B2 = 0.999
ADAM_EPS = 1e-08
ADAM_WD = 0.01
ADAM_STEP = 10
PER_EXAMPLE_BATCH_AXIS = {'x': 0, 'p': 1, 'loss_target': 0}
SHARED_INPUTS = []
_WEIGHT_DTYPES = {'ln_in_g': _jnp.float32, 'ln_in_b': _jnp.float32, 'w_in': _jnp.float32, 'b_forget': _jnp.float32, 'conv_w': _jnp.float32, 'conv_b': _jnp.float32, 'rg_w_a': _jnp.float32, 'rg_b_a': _jnp.float32, 'rg_w_x': _jnp.float32, 'rg_b_x': _jnp.float32, 'rg_lambda': _jnp.float32, 'w_branch_att': _jnp.float32, 'w_branch_rnn': _jnp.float32, 'b_merge': _jnp.float32, 'w_out': _jnp.float32, 'ln_mix_g': _jnp.float32, 'ln_mix_b': _jnp.float32, 'w_ffn_in': _jnp.float32, 'w_ffn_out': _jnp.float32, 'ln_ffn_g': _jnp.float32, 'ln_ffn_b': _jnp.float32, 'w_ple': _jnp.float32, 'w_ple_gate': _jnp.float32, 'b_ple_gate': _jnp.float32, 'ln_ple_g': _jnp.float32, 'ln_ple_b': _jnp.float32}
MOMENT_SCALE = {'ln_in_g': 8.239604e-01, 'ln_in_b': 5.244807e-01, 'w_in': 7.416320e-03, 'b_forget': 3.688551e-02, 'conv_w': 1.348266e-02, 'conv_b': 1.551851e-01, 'rg_w_a': 3.923947e-03, 'rg_b_a': 3.432128e-03, 'rg_w_x': 7.264117e-03, 'rg_b_x': 4.660830e-03, 'rg_lambda': 6.845165e-03, 'w_branch_att': 7.769762e-03, 'w_branch_rnn': 1.363715e-02, 'b_merge': 4.017017e-03, 'w_out': 3.378408e-02, 'ln_mix_g': 8.950993e-01, 'ln_mix_b': 4.812614e-01, 'w_ffn_in': 1.607476e-02, 'w_ffn_out': 6.241057e-02, 'ln_ffn_g': 9.149481e-01, 'ln_ffn_b': 4.871027e-01, 'w_ple': 5.701568e-02, 'w_ple_gate': 9.363234e-03, 'b_ple_gate': 1.091049e-02, 'ln_ple_g': 1.612641e+01, 'ln_ple_b': 1.090250e+00}


def _to_microbatches(a, axis):
    t = _jnp.moveaxis(a, axis, 0)
    t = t.reshape((N_MICROBATCH, t.shape[0] // N_MICROBATCH) + t.shape[1:])
    return _jnp.moveaxis(t, 1, axis + 1)


def setup_inputs(seed: int = 0) -> dict:
    inp = _fwd_setup_inputs(seed)
    key = _jax.random.fold_in(_jax.random.key(seed), 7919)
    shape, _ = _output_shape()
    out = dict(inp)
    out["loss_target"] = _jax.random.normal(_jax.random.fold_in(key, 0), shape, _jnp.float32)
    for i, name in enumerate(TWIN_WEIGHTS):
        w = inp[name].astype(_jnp.float32)
        if MOMENT_SCALE is None:
            s = _jnp.sqrt(_jnp.mean(_jnp.square(w)) + 1e-30)
        else:
            s = MOMENT_SCALE[name]
        km, kv = _jax.random.split(_jax.random.fold_in(key, i + 1))
        out[name] = w
        out["m_" + name] = s * _jax.random.normal(km, w.shape, _jnp.float32)
        out["v_" + name] = (s * s) * _jax.random.uniform(kv, w.shape, _jnp.float32, 0.5, 1.5)
    if N_MICROBATCH > 1:
        for name, axis in PER_EXAMPLE_BATCH_AXIS.items():
            out[name] = _to_microbatches(out[name], axis)
    return {'x': out['x'], 'p': out['p'], 'ln_in_g': out['ln_in_g'], 'ln_in_b': out['ln_in_b'], 'w_in': out['w_in'], 'b_forget': out['b_forget'], 'conv_w': out['conv_w'], 'conv_b': out['conv_b'], 'rg_w_a': out['rg_w_a'], 'rg_b_a': out['rg_b_a'], 'rg_w_x': out['rg_w_x'], 'rg_b_x': out['rg_b_x'], 'rg_lambda': out['rg_lambda'], 'w_branch_att': out['w_branch_att'], 'w_branch_rnn': out['w_branch_rnn'], 'b_merge': out['b_merge'], 'w_out': out['w_out'], 'ln_mix_g': out['ln_mix_g'], 'ln_mix_b': out['ln_mix_b'], 'w_ffn_in': out['w_ffn_in'], 'w_ffn_out': out['w_ffn_out'], 'ln_ffn_g': out['ln_ffn_g'], 'ln_ffn_b': out['ln_ffn_b'], 'w_ple': out['w_ple'], 'w_ple_gate': out['w_ple_gate'], 'b_ple_gate': out['b_ple_gate'], 'ln_ple_g': out['ln_ple_g'], 'ln_ple_b': out['ln_ple_b'], 'loss_target': out['loss_target'], 'm_ln_in_g': out['m_ln_in_g'], 'm_ln_in_b': out['m_ln_in_b'], 'm_w_in': out['m_w_in'], 'm_b_forget': out['m_b_forget'], 'm_conv_w': out['m_conv_w'], 'm_conv_b': out['m_conv_b'], 'm_rg_w_a': out['m_rg_w_a'], 'm_rg_b_a': out['m_rg_b_a'], 'm_rg_w_x': out['m_rg_w_x'], 'm_rg_b_x': out['m_rg_b_x'], 'm_rg_lambda': out['m_rg_lambda'], 'm_w_branch_att': out['m_w_branch_att'], 'm_w_branch_rnn': out['m_w_branch_rnn'], 'm_b_merge': out['m_b_merge'], 'm_w_out': out['m_w_out'], 'm_ln_mix_g': out['m_ln_mix_g'], 'm_ln_mix_b': out['m_ln_mix_b'], 'm_w_ffn_in': out['m_w_ffn_in'], 'm_w_ffn_out': out['m_w_ffn_out'], 'm_ln_ffn_g': out['m_ln_ffn_g'], 'm_ln_ffn_b': out['m_ln_ffn_b'], 'm_w_ple': out['m_w_ple'], 'm_w_ple_gate': out['m_w_ple_gate'], 'm_b_ple_gate': out['m_b_ple_gate'], 'm_ln_ple_g': out['m_ln_ple_g'], 'm_ln_ple_b': out['m_ln_ple_b'], 'v_ln_in_g': out['v_ln_in_g'], 'v_ln_in_b': out['v_ln_in_b'], 'v_w_in': out['v_w_in'], 'v_b_forget': out['v_b_forget'], 'v_conv_w': out['v_conv_w'], 'v_conv_b': out['v_conv_b'], 'v_rg_w_a': out['v_rg_w_a'], 'v_rg_b_a': out['v_rg_b_a'], 'v_rg_w_x': out['v_rg_w_x'], 'v_rg_b_x': out['v_rg_b_x'], 'v_rg_lambda': out['v_rg_lambda'], 'v_w_branch_att': out['v_w_branch_att'], 'v_w_branch_rnn': out['v_w_branch_rnn'], 'v_b_merge': out['v_b_merge'], 'v_w_out': out['v_w_out'], 'v_ln_mix_g': out['v_ln_mix_g'], 'v_ln_mix_b': out['v_ln_mix_b'], 'v_w_ffn_in': out['v_w_ffn_in'], 'v_w_ffn_out': out['v_w_ffn_out'], 'v_ln_ffn_g': out['v_ln_ffn_g'], 'v_ln_ffn_b': out['v_ln_ffn_b'], 'v_w_ple': out['v_w_ple'], 'v_w_ple_gate': out['v_w_ple_gate'], 'v_b_ple_gate': out['v_b_ple_gate'], 'v_ln_ple_g': out['v_ln_ple_g'], 'v_ln_ple_b': out['v_ln_ple_b']}


def _loss(weights, diff, rest, loss_target):
    with _jax.named_scope("forward"):
        args = {**rest, TWIN_DIFF_INPUT: diff, **{k: w.astype(_WEIGHT_DTYPES[k]) for k, w in weights.items()}}
        y = _forward(args)
    with _jax.named_scope("loss_head"):
        err = _jnp.square(y.astype(_jnp.float32) - loss_target)
        return 0.5 * _jnp.sum(_jnp.mean(err, axis=-1)) if err.ndim else 0.5 * err


def _adamw(w, g, m, v):
    m = ADAM_B1 * m + (1.0 - ADAM_B1) * g
    v = ADAM_B2 * v + (1.0 - ADAM_B2) * _jnp.square(g)
    m_hat = m / (1.0 - ADAM_B1 ** ADAM_STEP)
    v_hat = v / (1.0 - ADAM_B2 ** ADAM_STEP)
    delta = -ADAM_LR * (m_hat / (_jnp.sqrt(v_hat) + ADAM_EPS) + ADAM_WD * w)
    return delta, m, v


def reference(x, p, ln_in_g, ln_in_b, w_in, b_forget, conv_w, conv_b, rg_w_a, rg_b_a, rg_w_x, rg_b_x, rg_lambda, w_branch_att, w_branch_rnn, b_merge, w_out, ln_mix_g, ln_mix_b, w_ffn_in, w_ffn_out, ln_ffn_g, ln_ffn_b, w_ple, w_ple_gate, b_ple_gate, ln_ple_g, ln_ple_b, loss_target, m_ln_in_g, m_ln_in_b, m_w_in, m_b_forget, m_conv_w, m_conv_b, m_rg_w_a, m_rg_b_a, m_rg_w_x, m_rg_b_x, m_rg_lambda, m_w_branch_att, m_w_branch_rnn, m_b_merge, m_w_out, m_ln_mix_g, m_ln_mix_b, m_w_ffn_in, m_w_ffn_out, m_ln_ffn_g, m_ln_ffn_b, m_w_ple, m_w_ple_gate, m_b_ple_gate, m_ln_ple_g, m_ln_ple_b, v_ln_in_g, v_ln_in_b, v_w_in, v_b_forget, v_conv_w, v_conv_b, v_rg_w_a, v_rg_b_a, v_rg_w_x, v_rg_b_x, v_rg_lambda, v_w_branch_att, v_w_branch_rnn, v_b_merge, v_w_out, v_ln_mix_g, v_ln_mix_b, v_w_ffn_in, v_w_ffn_out, v_ln_ffn_g, v_ln_ffn_b, v_w_ple, v_w_ple_gate, v_b_ple_gate, v_ln_ple_g, v_ln_ple_b):
    given = dict(x=x, p=p, ln_in_g=ln_in_g, ln_in_b=ln_in_b, w_in=w_in, b_forget=b_forget, conv_w=conv_w, conv_b=conv_b, rg_w_a=rg_w_a, rg_b_a=rg_b_a, rg_w_x=rg_w_x, rg_b_x=rg_b_x, rg_lambda=rg_lambda, w_branch_att=w_branch_att, w_branch_rnn=w_branch_rnn, b_merge=b_merge, w_out=w_out, ln_mix_g=ln_mix_g, ln_mix_b=ln_mix_b, w_ffn_in=w_ffn_in, w_ffn_out=w_ffn_out, ln_ffn_g=ln_ffn_g, ln_ffn_b=ln_ffn_b, w_ple=w_ple, w_ple_gate=w_ple_gate, b_ple_gate=b_ple_gate, ln_ple_g=ln_ple_g, ln_ple_b=ln_ple_b, loss_target=loss_target, m_ln_in_g=m_ln_in_g, m_ln_in_b=m_ln_in_b, m_w_in=m_w_in, m_b_forget=m_b_forget, m_conv_w=m_conv_w, m_conv_b=m_conv_b, m_rg_w_a=m_rg_w_a, m_rg_b_a=m_rg_b_a, m_rg_w_x=m_rg_w_x, m_rg_b_x=m_rg_b_x, m_rg_lambda=m_rg_lambda, m_w_branch_att=m_w_branch_att, m_w_branch_rnn=m_w_branch_rnn, m_b_merge=m_b_merge, m_w_out=m_w_out, m_ln_mix_g=m_ln_mix_g, m_ln_mix_b=m_ln_mix_b, m_w_ffn_in=m_w_ffn_in, m_w_ffn_out=m_w_ffn_out, m_ln_ffn_g=m_ln_ffn_g, m_ln_ffn_b=m_ln_ffn_b, m_w_ple=m_w_ple, m_w_ple_gate=m_w_ple_gate, m_b_ple_gate=m_b_ple_gate, m_ln_ple_g=m_ln_ple_g, m_ln_ple_b=m_ln_ple_b, v_ln_in_g=v_ln_in_g, v_ln_in_b=v_ln_in_b, v_w_in=v_w_in, v_b_forget=v_b_forget, v_conv_w=v_conv_w, v_conv_b=v_conv_b, v_rg_w_a=v_rg_w_a, v_rg_b_a=v_rg_b_a, v_rg_w_x=v_rg_w_x, v_rg_b_x=v_rg_b_x, v_rg_lambda=v_rg_lambda, v_w_branch_att=v_w_branch_att, v_w_branch_rnn=v_w_branch_rnn, v_b_merge=v_b_merge, v_w_out=v_w_out, v_ln_mix_g=v_ln_mix_g, v_ln_mix_b=v_ln_mix_b, v_w_ffn_in=v_w_ffn_in, v_w_ffn_out=v_w_ffn_out, v_ln_ffn_g=v_ln_ffn_g, v_ln_ffn_b=v_ln_ffn_b, v_w_ple=v_w_ple, v_w_ple_gate=v_w_ple_gate, v_b_ple_gate=v_b_ple_gate, v_ln_ple_g=v_ln_ple_g, v_ln_ple_b=v_ln_ple_b)
    weights = {n: given[n] for n in TWIN_WEIGHTS}
    shared = {n: given[n] for n in SHARED_INPUTS}
    per_example = {n: given[n] for n in ['x', 'p']}
    grad_fn = _jax.value_and_grad(_loss, argnums=(0, 1))

    def one_microbatch(ex, loss_target):
        ex = dict(ex)
        diff = ex.pop(TWIN_DIFF_INPUT)
        return grad_fn(weights, diff, {**shared, **ex}, loss_target)

    if N_MICROBATCH == 1:
        loss, (grad_w, grad_x) = one_microbatch(per_example, given["loss_target"])
    else:
        def body(carry, xs):
            loss_sum, grad_sum = carry
            l_k, (gw_k, gx_k) = one_microbatch(xs[0], xs[1])
            with _jax.named_scope("update"):
                return (loss_sum + l_k, _jax.tree.map(_jnp.add, grad_sum, gw_k)), gx_k

        init = (_jnp.zeros((), _jnp.float32), _jax.tree.map(_jnp.zeros_like, weights))
        (loss, grad_w), grad_x = _jax.lax.scan(body, init, (per_example, given["loss_target"]))
    with _jax.named_scope("update"):
        delta_w, new_m, new_v = {}, {}, {}
        for n in TWIN_WEIGHTS:
            delta_w[n], new_m[n], new_v[n] = _adamw(weights[n], grad_w[n], given["m_" + n], given["v_" + n])
    return (loss, grad_x, *[grad_w[n] for n in TWIN_WEIGHTS], *[delta_w[n] for n in TWIN_WEIGHTS],
            *[new_m[n] for n in TWIN_WEIGHTS], *[new_v[n] for n in TWIN_WEIGHTS])
```

```python
import functools
import math

import jax
import jax.numpy as jnp
from jax import lax
from jax.experimental import pallas as pl
from jax.experimental.pallas import tpu as pltpu

F32 = jnp.float32
BF16 = jnp.bfloat16

N_DEV = 8
D_MODEL = 1024
N_HEADS = 8
HEAD_DIM = 128
N_BLK = 8
BLK = 128
CONV_W = 4
D_PLE = 256
FF_SH = 704
N_FF = 4
IN_SH = 897
N_IN = 7176
DEPTH = 4
RG_C = 8.0
ALPHA = float((2 * DEPTH) ** 0.25)
LN_EPS = 1e-5
SCALE = 1.0 / math.sqrt(HEAD_DIM)
NEG = -1e30
ADAM_LR, ADAM_B1, ADAM_B2, ADAM_EPS, ADAM_WD, ADAM_STEP = 0.001, 0.9, 0.999, 1e-08, 0.01, 10
OFF_Q, OFF_K, OFF_V, OFF_RX, OFF_RY, OFF_GA, OFF_GB = (i * D_MODEL for i in range(7))
Z7 = 7 * D_MODEL
V7X_VMEM_LIMIT = 48 * 1024 * 1024

WEIGHTS = ['ln_in_g', 'ln_in_b', 'w_in', 'b_forget', 'conv_w', 'conv_b', 'rg_w_a', 'rg_b_a', 'rg_w_x', 'rg_b_x',
           'rg_lambda', 'w_branch_att', 'w_branch_rnn', 'b_merge', 'w_out', 'ln_mix_g', 'ln_mix_b', 'w_ffn_in',
           'w_ffn_out', 'ln_ffn_g', 'ln_ffn_b', 'w_ple', 'w_ple_gate', 'b_ple_gate', 'ln_ple_g', 'ln_ple_b']
SHARDED_BF16 = ['w_in', 'w_branch_att', 'w_branch_rnn', 'w_out', 'w_ffn_in', 'w_ffn_out', 'w_ple', 'w_ple_gate']
SHARDED_F32 = ['conv_w', 'b_merge']
REPLICATED = [n for n in WEIGHTS if n not in SHARDED_BF16 and n not in SHARDED_F32]

NN = ((1,), (0,))
NT = ((1,), (1,))
TN = ((0,), (0,))


def _pcall(body, **kw):
    return pl.pallas_call(body, **kw)


def _tile(n, pref, mult=8):
    if n <= pref:
        return n
    t = (pref // mult) * mult
    while t >= mult:
        if n % t == 0:
            return t
        t -= mult
    return n


def _cparams(sem):
    return pltpu.CompilerParams(dimension_semantics=sem, vmem_limit_bytes=V7X_VMEM_LIMIT)


def _mm(name, a, b, *, grid, a_spec, b_spec, o_spec, out_shape, contract):
    nk = grid[-1]
    acc_shape = tuple(d for d in o_spec.block_shape if d is not None)

    def body(a_ref, b_ref, o_ref, acc_ref):
        k = pl.program_id(len(grid) - 1)
        part = lax.dot_general(a_ref[...].astype(BF16), b_ref[...].astype(BF16), (contract, ((), ())),
                               preferred_element_type=F32)

        @pl.when(k == 0)
        def _():
            acc_ref[...] = part

        @pl.when(k > 0)
        def _():
            acc_ref[...] += part

        @pl.when(k == nk - 1)
        def _():
            o_ref[...] = acc_ref[...].astype(o_ref.dtype)

    sem = ("parallel",) * (len(grid) - 1) + ("arbitrary",)
    return _pcall(body, name=name, grid=grid, in_specs=[a_spec, b_spec], out_specs=o_spec, out_shape=out_shape,
                  scratch_shapes=[pltpu.VMEM(acc_shape, F32)], compiler_params=_cparams(sem))(a, b)


def _mm_nn(name, a, b, *, a_off=0, out_dtype=F32, tm=512, tn=1024, tk=1024):
    m = a.shape[0]
    k, n = b.shape
    tm, tn, tk = _tile(m, tm), _tile(n, tn, 128), _tile(k, tk, 128)
    ko = a_off // tk
    return _mm(name, a, b, grid=(m // tm, n // tn, k // tk),
               a_spec=pl.BlockSpec((tm, tk), lambda i, j, kk: (i, kk + ko)),
               b_spec=pl.BlockSpec((tk, tn), lambda i, j, kk: (kk, j)),
               o_spec=pl.BlockSpec((tm, tn), lambda i, j, kk: (i, j)),
               out_shape=jax.ShapeDtypeStruct((m, n), out_dtype), contract=NN)


def _mm_nt(name, a, b, *, out_dtype=F32, tm=512, tn=1024, tk=1024):
    m, k = a.shape
    n = b.shape[0]
    tm, tn, tk = _tile(m, tm), _tile(n, tn, 128), _tile(k, tk, 128)
    return _mm(name, a, b, grid=(m // tm, n // tn, k // tk),
               a_spec=pl.BlockSpec((tm, tk), lambda i, j, kk: (i, kk)),
               b_spec=pl.BlockSpec((tn, tk), lambda i, j, kk: (j, kk)),
               o_spec=pl.BlockSpec((tm, tn), lambda i, j, kk: (i, j)),
               out_shape=jax.ShapeDtypeStruct((m, n), out_dtype), contract=NT)


def _mm_tn(name, a, b, *, a_off=0, m=None, out_dtype=F32, tm=512, tn=1024, tk=512):
    t, n = b.shape
    m = a.shape[1] if m is None else m
    tm, tn, tk = _tile(m, tm, 128), _tile(n, tn, 128), _tile(t, tk)
    mo = a_off // tm
    return _mm(name, a, b, grid=(m // tm, n // tn, t // tk),
               a_spec=pl.BlockSpec((tk, tm), lambda i, j, kk: (kk, i + mo)),
               b_spec=pl.BlockSpec((tk, tn), lambda i, j, kk: (kk, j)),
               o_spec=pl.BlockSpec((tm, tn), lambda i, j, kk: (i, j)),
               out_shape=jax.ShapeDtypeStruct((m, n), out_dtype), contract=TN)


def _rowwise(name, fn, rows, params, out_rows, out_reds, tm=256):
    rows = [r if isinstance(r, tuple) else (r, 0, r.shape[1]) for r in rows]
    t = rows[0][0].shape[0]
    tm = _tile(t, tm)
    in_specs = []
    for _, off, w in rows:
        in_specs.append(pl.BlockSpec((tm, w), functools.partial(lambda i, cb: (i, cb), cb=off // w)))
    for p in params:
        in_specs.append(pl.BlockSpec((1, p.shape[1]), lambda i: (0, 0)))
    out_specs = [pl.BlockSpec((tm, w), lambda i: (i, 0)) for w, _ in out_rows]
    out_specs += [pl.BlockSpec((1, w), lambda i: (0, 0)) for w in out_reds]
    out_shape = [jax.ShapeDtypeStruct((t, w), dt) for w, dt in out_rows]
    out_shape += [jax.ShapeDtypeStruct((1, w), F32) for w in out_reds]
    nr, npar, nor = len(rows), len(params), len(out_rows)

    def body(*refs):
        ins, outs = refs[:nr + npar], refs[nr + npar:]
        vals = [r[...].astype(F32) for r in ins[:nr]]
        vals += [jnp.broadcast_to(r[...], (tm, r.shape[1])) for r in ins[nr:]]
        res = fn(*vals)
        step = pl.program_id(0)
        for o, v in zip(outs[:nor], res[:nor]):
            o[...] = v.astype(o.dtype)
        for o, v in zip(outs[nor:], res[nor:]):
            _accumulate(o, v, step)

    res = _pcall(body, name=name, grid=(t // tm,), in_specs=in_specs, out_specs=out_specs, out_shape=out_shape,
                 compiler_params=_cparams(("arbitrary",)))(*[r[0] for r in rows], *params)
    return res


def _accumulate(o_ref, v, step):
    @pl.when(step == 0)
    def _():
        o_ref[...] = v

    @pl.when(step > 0)
    def _():
        o_ref[...] += v


def _colsum(v):
    return jnp.sum(v, axis=0, keepdims=True)


def _vjp_rowwise(name, fn, rows, params, cots, n_row_grads, tm=256):
    nr, npar, nc = len(rows), len(params), len(cots)

    def bwd(*vals):
        prim, par, ct = vals[:nr], vals[nr + nc:], vals[nr:nr + nc]
        _, pull = jax.vjp(fn, *prim, *par)
        grads = pull(tuple(ct) if nc > 1 else ct[0])
        return tuple(grads[:n_row_grads]) + tuple(_colsum(g) for g in grads[nr:])

    widths = [(r[2] if isinstance(r, tuple) else r.shape[1], F32) for r in rows[:n_row_grads]]
    return _rowwise(name, bwd, list(rows) + list(cots), params, widths, [p.shape[1] for p in params], tm=tm)


def _ln(s, g, b):
    mu = jnp.mean(s, axis=-1, keepdims=True)
    var = jnp.mean(jnp.square(s - mu), axis=-1, keepdims=True)
    return (s - mu) * lax.rsqrt(var + LN_EPS) * g + b


def _softplus(x):
    return jnp.maximum(x, 0.0) + jnp.log1p(jnp.exp(-jnp.abs(x)))


def _expm1(x):
    series = x * (1.0 + x * (1.0 / 2 + x * (1.0 / 6 + x * (1.0 / 24 + x * (1.0 / 120 + x * (1.0 / 720))))))
    return jnp.where(jnp.abs(x) < 0.25, series, jnp.exp(x) - 1.0)


def _f_resid_ln(h, branch, g, b):
    return _ln(ALPHA * h + branch, g, b)


def _f_ple(h, gp, pe, bpg, g, b):
    return _ln(ALPHA * h + jax.nn.sigmoid(gp + bpg) * pe, g, b)


def _f_merge(ga, gb, ya, yb, bm0, bm1):
    return jax.nn.sigmoid(ga + bm0) * ya + jax.nn.sigmoid(gb + bm1) * yb


def _f_rnn_out(hs, ry):
    return hs * jax.nn.gelu(ry, approximate=True)


def _f_logf(fl, bf):
    return -_softplus(-(fl + bf))


def _f_gate(xc, ra, ia, lam, ba, bx):
    r = jax.nn.sigmoid(ra + ba)
    i = jax.nn.sigmoid(ia + bx)
    log_a = -RG_C * _softplus(-lam) * r
    a = jnp.exp(log_a)
    mult = jnp.sqrt(-_expm1(2.0 * log_a))
    return a, mult * (i * xc)


def _f_act(hg, hu):
    return jax.nn.silu(hg) * hu


def _scores(q, k, cq, ck, i, j, tq):
    s = lax.dot_general(q.astype(BF16), k.astype(BF16), (NT, ((), ())), preferred_element_type=F32) * SCALE
    s = s + cq - ck
    row = lax.broadcasted_iota(jnp.int32, (tq, tq), 0)
    col = lax.broadcasted_iota(jnp.int32, (tq, tq), 1)
    return jnp.where((j < i) | (col <= row), s, NEG)


def _dscores(p, do, o, v):
    dob = do.astype(BF16)
    delta = jnp.sum(dob.astype(F32) * o, axis=1, keepdims=True)
    dp = lax.dot_general(dob, v.astype(BF16), (NT, ((), ())), preferred_element_type=F32)
    return p * (dp - delta)


def _attn_fwd(z, cq, ck, bsz, seq):
    t = bsz * seq
    tq = _tile(seq, 256)
    nq = seq // tq

    def body(q_ref, k_ref, v_ref, cq_ref, ck_ref, o_ref, lse_ref, m_sc, l_sc, acc_sc):
        i, j = pl.program_id(2), pl.program_id(3)

        @pl.when(j == 0)
        def _():
            m_sc[...] = jnp.full_like(m_sc, NEG)
            l_sc[...] = jnp.zeros_like(l_sc)
            acc_sc[...] = jnp.zeros_like(acc_sc)

        @pl.when(j <= i)
        def _():
            s = _scores(q_ref[...], k_ref[...], cq_ref[...], ck_ref[...], i, j, tq)
            m_new = jnp.maximum(m_sc[...], jnp.max(s, axis=1, keepdims=True))
            alpha = jnp.exp(m_sc[...] - m_new)
            p = jnp.exp(s - m_new)
            l_sc[...] = alpha * l_sc[...] + jnp.sum(p, axis=1, keepdims=True)
            p_hi = p.astype(BF16)
            p_lo = (p - p_hi.astype(F32)).astype(BF16)
            vb = v_ref[...].astype(BF16)
            pv = lax.dot_general(p_hi, vb, (NN, ((), ())), preferred_element_type=F32)
            pv = pv + lax.dot_general(p_lo, vb, (NN, ((), ())), preferred_element_type=F32)
            acc_sc[...] = alpha * acc_sc[...] + pv
            m_sc[...] = m_new

        @pl.when(j == nq - 1)
        def _():
            o_ref[...] = acc_sc[...] / l_sc[...]
            lse_ref[...] = m_sc[...] + jnp.log(l_sc[...])

    blk = (tq, HEAD_DIM)
    in_specs = [
        pl.BlockSpec(blk, lambda b, h, i, j: (b * nq + i, h)),
        pl.BlockSpec(blk, lambda b, h, i, j: (b * nq + jnp.minimum(j, i), N_HEADS + h)),
        pl.BlockSpec(blk, lambda b, h, i, j: (b * nq + jnp.minimum(j, i), 2 * N_HEADS + h)),
        pl.BlockSpec((None, None, tq, 1), lambda b, h, i, j: (b, h, i, 0)),
        pl.BlockSpec((None, None, 1, tq), lambda b, h, i, j: (b, h, 0, jnp.minimum(j, i))),
    ]
    out_specs = [pl.BlockSpec(blk, lambda b, h, i, j: (b * nq + i, h)),
                 pl.BlockSpec((None, None, tq, 1), lambda b, h, i, j: (b, h, i, 0))]
    out_shape = [jax.ShapeDtypeStruct((t, D_MODEL), F32), jax.ShapeDtypeStruct((bsz, N_HEADS, seq, 1), F32)]
    return _pcall(body, name="attn_fwd", grid=(bsz, N_HEADS, nq, nq), in_specs=in_specs, out_specs=out_specs,
                  out_shape=out_shape,
                  scratch_shapes=[pltpu.VMEM((tq, 1), F32), pltpu.VMEM((tq, 1), F32), pltpu.VMEM(blk, F32)],
                  compiler_params=_cparams(("parallel", "parallel", "parallel", "arbitrary")))(z, z, z, cq, ck)


def _attn_bwd_kv(z, att, datt, lse, cq, ck, bsz, seq):
    t = bsz * seq
    tq = _tile(seq, 256)
    nq = seq // tq

    def body(q_ref, k_ref, v_ref, o_ref, do_ref, lse_ref, cq_ref, ck_ref, dk_ref, dv_ref, dck_ref,
             dk_sc, dv_sc, dc_sc):
        j, i = pl.program_id(2), pl.program_id(3)

        @pl.when(i == 0)
        def _():
            dk_sc[...] = jnp.zeros_like(dk_sc)
            dv_sc[...] = jnp.zeros_like(dv_sc)
            dc_sc[...] = jnp.zeros_like(dc_sc)

        @pl.when(i >= j)
        def _():
            q, do = q_ref[...], do_ref[...]
            s = _scores(q, k_ref[...], cq_ref[...], ck_ref[...], i, j, tq)
            p = jnp.exp(s - lse_ref[...])
            ds = _dscores(p, do, o_ref[...], v_ref[...])
            dv_sc[...] += lax.dot_general(p.astype(BF16), do.astype(BF16), (TN, ((), ())),
                                          preferred_element_type=F32)
            dk_sc[...] += lax.dot_general((ds * SCALE).astype(BF16), q.astype(BF16), (TN, ((), ())),
                                          preferred_element_type=F32)
            dc_sc[...] -= jnp.sum(ds, axis=0, keepdims=True)

        @pl.when(i == nq - 1)
        def _():
            dk_ref[...] = dk_sc[...]
            dv_ref[...] = dv_sc[...]
            dck_ref[...] = dc_sc[...]

    blk = (tq, HEAD_DIM)
    qi = lambda b, h, j, i: (b * nq + jnp.maximum(i, j), h)
    col = lambda b, h, j, i: (b, h, jnp.maximum(i, j), 0)
    in_specs = [
        pl.BlockSpec(blk, qi),
        pl.BlockSpec(blk, lambda b, h, j, i: (b * nq + j, N_HEADS + h)),
        pl.BlockSpec(blk, lambda b, h, j, i: (b * nq + j, 2 * N_HEADS + h)),
        pl.BlockSpec(blk, qi),
        pl.BlockSpec(blk, qi),
        pl.BlockSpec((None, None, tq, 1), col),
        pl.BlockSpec((None, None, tq, 1), col),
        pl.BlockSpec((None, None, 1, tq), lambda b, h, j, i: (b, h, 0, j)),
    ]
    kv_out = pl.BlockSpec(blk, lambda b, h, j, i: (b * nq + j, h))
    out_specs = [kv_out, kv_out, pl.BlockSpec((None, None, 1, tq), lambda b, h, j, i: (b, h, 0, j))]
    out_shape = [jax.ShapeDtypeStruct((t, D_MODEL), F32), jax.ShapeDtypeStruct((t, D_MODEL), F32),
                 jax.ShapeDtypeStruct((bsz, N_HEADS, 1, seq), F32)]
    return _pcall(body, name="attn_bwd_kv", grid=(bsz, N_HEADS, nq, nq), in_specs=in_specs, out_specs=out_specs,
                  out_shape=out_shape,
                  scratch_shapes=[pltpu.VMEM(blk, F32), pltpu.VMEM(blk, F32), pltpu.VMEM((1, tq), F32)],
                  compiler_params=_cparams(("parallel", "parallel", "parallel", "arbitrary")))(
                      z, z, z, att, datt, lse, cq, ck)


def _attn_bwd_q(z, att, datt, lse, cq, ck, bsz, seq):
    t = bsz * seq
    tq = _tile(seq, 256)
    nq = seq // tq

    def body(q_ref, k_ref, v_ref, o_ref, do_ref, lse_ref, cq_ref, ck_ref, dq_ref, dcq_ref, dq_sc, dc_sc):
        i, j = pl.program_id(2), pl.program_id(3)

        @pl.when(j == 0)
        def _():
            dq_sc[...] = jnp.zeros_like(dq_sc)
            dc_sc[...] = jnp.zeros_like(dc_sc)

        @pl.when(j <= i)
        def _():
            k = k_ref[...]
            s = _scores(q_ref[...], k, cq_ref[...], ck_ref[...], i, j, tq)
            p = jnp.exp(s - lse_ref[...])
            ds = _dscores(p, do_ref[...], o_ref[...], v_ref[...])
            dq_sc[...] += lax.dot_general((ds * SCALE).astype(BF16), k.astype(BF16), (NN, ((), ())),
                                          preferred_element_type=F32)
            dc_sc[...] += jnp.sum(ds, axis=1, keepdims=True)

        @pl.when(j == nq - 1)
        def _():
            dq_ref[...] = dq_sc[...]
            dcq_ref[...] = dc_sc[...]

    blk = (tq, HEAD_DIM)
    qi = lambda b, h, i, j: (b * nq + i, h)
    col = lambda b, h, i, j: (b, h, i, 0)
    in_specs = [
        pl.BlockSpec(blk, qi),
        pl.BlockSpec(blk, lambda b, h, i, j: (b * nq + jnp.minimum(j, i), N_HEADS + h)),
        pl.BlockSpec(blk, lambda b, h, i, j: (b * nq + jnp.minimum(j, i), 2 * N_HEADS + h)),
        pl.BlockSpec(blk, qi),
        pl.BlockSpec(blk, qi),
        pl.BlockSpec((None, None, tq, 1), col),
        pl.BlockSpec((None, None, tq, 1), col),
        pl.BlockSpec((None, None, 1, tq), lambda b, h, i, j: (b, h, 0, jnp.minimum(j, i))),
    ]
    return _pcall(body, name="attn_bwd_q", grid=(bsz, N_HEADS, nq, nq), in_specs=in_specs,
                  out_specs=[pl.BlockSpec(blk, qi), pl.BlockSpec((None, None, tq, 1), col)],
                  out_shape=[jax.ShapeDtypeStruct((t, D_MODEL), F32),
                             jax.ShapeDtypeStruct((bsz, N_HEADS, seq, 1), F32)],
                  scratch_shapes=[pltpu.VMEM(blk, F32), pltpu.VMEM((tq, 1), F32)],
                  compiler_params=_cparams(("parallel", "parallel", "parallel", "arbitrary")))(
                      z, z, z, att, datt, lse, cq, ck)


def _scan(name, a, u, bsz, seq, *, reverse, with_prev=False, tb=256):
    c = u.shape[1]
    tb = _tile(seq, tb)
    nb = seq // tb
    has_a = a is not None

    def body(*refs):
        if has_a:
            a_ref, u_ref = refs[0], refs[1]
            rest = refs[2:]
        else:
            u_ref = refs[0]
            rest = refs[1:]
        outs = rest[:2] if with_prev else rest[:1]
        carry_sc, afirst_sc = rest[-2], rest[-1]
        step = pl.program_id(1)

        @pl.when(step == 0)
        def _():
            carry_sc[...] = jnp.zeros_like(carry_sc)
            afirst_sc[...] = jnp.zeros_like(afirst_sc)

        row = lax.broadcasted_iota(jnp.int32, (tb, c), 0)
        uu = u_ref[...]
        if has_a:
            aa = a_ref[...]
            if reverse:
                coef = jnp.where(row < tb - 1, pltpu.roll(aa, tb - 1, 0), afirst_sc[...])
            else:
                coef = aa
        k = 1
        while k < tb:
            shift = tb - k if reverse else k
            keep = (row < tb - k) if reverse else (row >= k)
            uu_sh = jnp.where(keep, pltpu.roll(uu, shift, 0), 0.0)
            if has_a:
                uu = coef * uu_sh + uu
                coef = coef * jnp.where(keep, pltpu.roll(coef, shift, 0), 1.0)
            else:
                uu = uu + uu_sh
            k *= 2
        carry = carry_sc[...]
        h = uu + coef * carry if has_a else uu + carry
        outs[0][...] = h
        if with_prev:
            outs[1][...] = jnp.where(row >= 1, pltpu.roll(h, 1, 0), carry)
        if reverse:
            carry_sc[...] = outs[0][0:1, :]
            if has_a:
                afirst_sc[...] = a_ref[0:1, :]
        else:
            carry_sc[...] = outs[0][tb - 1:tb, :]

    if reverse:
        imap = lambda b, s: (b * nb + nb - 1 - s, 0)
    else:
        imap = lambda b, s: (b * nb + s, 0)
    spec = pl.BlockSpec((tb, c), imap)
    n_in = 2 if has_a else 1
    n_out = 2 if with_prev else 1
    res = _pcall(body, name=name, grid=(bsz, nb), in_specs=[spec] * n_in, out_specs=[spec] * n_out,
                 out_shape=[jax.ShapeDtypeStruct(u.shape, F32)] * n_out,
                 scratch_shapes=[pltpu.VMEM((1, c), F32), pltpu.VMEM((1, c), F32)],
                 compiler_params=_cparams(("parallel", "arbitrary")))(*([a, u] if has_a else [u]))
    return res if with_prev else res[0]


def _conv_fwd(z, w, b, bsz, seq, tb=256):
    c = D_MODEL
    t = bsz * seq
    tb = _tile(seq, tb)
    nb = seq // tb

    def body(x_ref, w_ref, b_ref, o_ref, tail_sc):
        step = pl.program_id(1)

        @pl.when(step == 0)
        def _():
            tail_sc[...] = jnp.zeros_like(tail_sc)

        x = x_ref[...]
        row8 = lax.broadcasted_iota(jnp.int32, (8, c), 0)
        tail = tail_sc[...]
        acc = w_ref[CONV_W - 1:CONV_W, :] * x + b_ref[...]
        for sh in range(1, CONV_W):
            xs = pltpu.roll(x, sh, 0)
            top = jnp.where(row8 < sh, pltpu.roll(tail, sh, 0), xs[0:8, :])
            xs = jnp.concatenate([top, xs[8:, :]], axis=0) if tb > 8 else top
            acc = acc + w_ref[CONV_W - 1 - sh:CONV_W - sh, :] * xs
        o_ref[...] = acc
        tail_sc[...] = x_ref[tb - 8:tb, :]

    return _pcall(body, name="conv_fwd", grid=(bsz, nb),
                  in_specs=[pl.BlockSpec((tb, c), lambda bb, s: (bb * nb + s, OFF_RX // c)),
                            pl.BlockSpec((CONV_W, c), lambda bb, s: (0, 0)),
                            pl.BlockSpec((1, c), lambda bb, s: (0, 0))],
                  out_specs=pl.BlockSpec((tb, c), lambda bb, s: (bb * nb + s, 0)),
                  out_shape=jax.ShapeDtypeStruct((t, c), F32),
                  scratch_shapes=[pltpu.VMEM((8, c), F32)],
                  compiler_params=_cparams(("parallel", "arbitrary")))(z, w, b)


def _conv_bwd(z, dxc, w, bsz, seq, tb=256):
    c = D_MODEL
    t = bsz * seq
    tb = _tile(seq, tb)
    nb = seq // tb

    def body(x_ref, g_ref, w_ref, dx_ref, dw_ref, db_ref, head_sc):
        bb, step = pl.program_id(0), pl.program_id(1)

        @pl.when(step == 0)
        def _():
            head_sc[...] = jnp.zeros_like(head_sc)

        x, g = x_ref[...], g_ref[...]
        row8 = lax.broadcasted_iota(jnp.int32, (8, c), 0)
        head = head_sc[...]
        dx = w_ref[CONV_W - 1:CONV_W, :] * g
        dws = [None] * CONV_W
        dws[CONV_W - 1] = _colsum(g * x)
        for sh in range(1, CONV_W):
            gs = pltpu.roll(g, tb - sh, 0)
            bot = jnp.where(row8 >= 8 - sh, pltpu.roll(head, 8 - sh, 0), gs[tb - 8:tb, :])
            gs = jnp.concatenate([gs[:tb - 8, :], bot], axis=0) if tb > 8 else bot
            dx = dx + w_ref[CONV_W - 1 - sh:CONV_W - sh, :] * gs
            dws[CONV_W - 1 - sh] = _colsum(gs * x)
        dx_ref[...] = dx
        first = (bb == 0) & (step == 0)
        dw = jnp.concatenate(dws, axis=0)
        db = _colsum(g)

        @pl.when(first)
        def _():
            dw_ref[...] = dw
            db_ref[...] = db

        @pl.when(jnp.logical_not(first))
        def _():
            dw_ref[...] += dw
            db_ref[...] += db

        head_sc[...] = g_ref[0:8, :]

    rmap = lambda bb, s: (bb * nb + nb - 1 - s, 0)
    return _pcall(body, name="conv_bwd", grid=(bsz, nb),
                  in_specs=[pl.BlockSpec((tb, c), lambda bb, s: (bb * nb + nb - 1 - s, OFF_RX // c)),
                            pl.BlockSpec((tb, c), rmap),
                            pl.BlockSpec((CONV_W, c), lambda bb, s: (0, 0))],
                  out_specs=[pl.BlockSpec((tb, c), rmap),
                             pl.BlockSpec((CONV_W, c), lambda bb, s: (0, 0)),
                             pl.BlockSpec((1, c), lambda bb, s: (0, 0))],
                  out_shape=[jax.ShapeDtypeStruct((t, c), F32), jax.ShapeDtypeStruct((CONV_W, c), F32),
                             jax.ShapeDtypeStruct((1, c), F32)],
                  scratch_shapes=[pltpu.VMEM((8, c), F32)],
                  compiler_params=_cparams(("arbitrary", "arbitrary")))(z, dxc, w)


def _gate_fwd(xc, w_a, w_x, b_a, b_x, lam, tm=512):
    t = xc.shape[0]
    tm = _tile(t, tm)

    def body(xc_ref, wa_ref, wx_ref, ba_ref, bx_ref, lam_ref, a_ref, u_ref):
        xc_b = xc_ref[...]
        xb = xc_b.astype(BF16)
        ra = lax.dot_general(xb, wa_ref[...].astype(BF16), (NN, ((), ())), preferred_element_type=F32)
        ia = lax.dot_general(xb, wx_ref[...].astype(BF16), (NN, ((), ())), preferred_element_type=F32)
        a, u = _f_gate(xc_b, ra, ia, lam_ref[...], ba_ref[...], bx_ref[...])
        a_ref[...] = a
        u_ref[...] = u

    row = pl.BlockSpec((tm, BLK), lambda n, i: (i, n))
    wsp = pl.BlockSpec((None, BLK, BLK), lambda n, i: (n, 0, 0))
    vec = pl.BlockSpec((1, BLK), lambda n, i: (0, n))
    return _pcall(body, name="gate_fwd", grid=(N_BLK, t // tm), in_specs=[row, wsp, wsp, vec, vec, vec],
                  out_specs=[row, row], out_shape=[jax.ShapeDtypeStruct((t, D_MODEL), F32)] * 2,
                  compiler_params=_cparams(("parallel", "parallel")))(xc, w_a, w_x, b_a, b_x, lam)


def _gate_bwd(xc, w_a, w_x, b_a, b_x, lam, da, du, tm=512):
    t = xc.shape[0]
    tm = _tile(t, tm)

    def body(xc_ref, wa_ref, wx_ref, ba_ref, bx_ref, lam_ref, da_ref, du_ref,
             dxc_ref, dwa_ref, dwx_ref, dba_ref, dbx_ref, dlam_ref):
        step = pl.program_id(1)
        xc_b = xc_ref[...]
        xb = xc_b.astype(BF16)
        wa, wx = wa_ref[...].astype(BF16), wx_ref[...].astype(BF16)
        ra = lax.dot_general(xb, wa, (NN, ((), ())), preferred_element_type=F32)
        ia = lax.dot_general(xb, wx, (NN, ((), ())), preferred_element_type=F32)
        full = lambda r: jnp.broadcast_to(r[...], (tm, BLK))
        _, pull = jax.vjp(_f_gate, xc_b, ra, ia, full(lam_ref), full(ba_ref), full(bx_ref))
        dxc, dra, dia, dlam, dba, dbx = pull((da_ref[...], du_ref[...]))
        drb, dib = dra.astype(BF16), dia.astype(BF16)
        dxc = dxc + lax.dot_general(drb, wa, (NT, ((), ())), preferred_element_type=F32)
        dxc = dxc + lax.dot_general(dib, wx, (NT, ((), ())), preferred_element_type=F32)
        dxc_ref[...] = dxc
        _accumulate(dwa_ref, lax.dot_general(xb, drb, (TN, ((), ())), preferred_element_type=F32), step)
        _accumulate(dwx_ref, lax.dot_general(xb, dib, (TN, ((), ())), preferred_element_type=F32), step)
        _accumulate(dba_ref, _colsum(dba), step)
        _accumulate(dbx_ref, _colsum(dbx), step)
        _accumulate(dlam_ref, _colsum(dlam), step)

    row = pl.BlockSpec((tm, BLK), lambda n, i: (i, n))
    wsp = pl.BlockSpec((None, BLK, BLK), lambda n, i: (n, 0, 0))
    vec = pl.BlockSpec((1, BLK), lambda n, i: (0, n))
    wshape = jax.ShapeDtypeStruct((N_BLK, BLK, BLK), F32)
    vshape = jax.ShapeDtypeStruct((1, D_MODEL), F32)
    return _pcall(body, name="gate_bwd", grid=(N_BLK, t // tm),
                  in_specs=[row, wsp, wsp, vec, vec, vec, row, row],
                  out_specs=[row, wsp, wsp, vec, vec, vec],
                  out_shape=[jax.ShapeDtypeStruct((t, D_MODEL), F32), wshape, wshape, vshape, vshape, vshape],
                  compiler_params=_cparams(("parallel", "arbitrary")))(xc, w_a, w_x, b_a, b_x, lam, da, du)


def _act_fwd(hgu, tm=512):
    _, t, w = hgu.shape
    tm = _tile(t, tm)

    def body(hg_ref, hu_ref, o_ref):
        o_ref[...] = _f_act(hg_ref[...], hu_ref[...])

    spec = lambda off: pl.BlockSpec((None, tm, w), lambda s, i: (s + off, i, 0))
    return _pcall(body, name="act_fwd", grid=(N_FF, t // tm), in_specs=[spec(0), spec(N_FF)], out_specs=spec(0),
                  out_shape=jax.ShapeDtypeStruct((N_FF, t, w), F32),
                  compiler_params=_cparams(("parallel", "parallel")))(hgu, hgu)


def _act_bwd(hgu, dact, tm=512):
    _, t, w = hgu.shape
    tm = _tile(t, tm)

    def body(hg_ref, hu_ref, d_ref, o_ref):
        half = pl.program_id(0)
        _, pull = jax.vjp(_f_act, hg_ref[...], hu_ref[...])
        dhg, dhu = pull(d_ref[...])
        o_ref[...] = jnp.where(half == 0, dhg, dhu)

    spec = lambda off: pl.BlockSpec((None, tm, w), lambda hf, s, i: (s + off, i, 0))
    return _pcall(body, name="act_bwd", grid=(2, N_FF, t // tm), in_specs=[spec(0), spec(N_FF), spec(0)],
                  out_specs=pl.BlockSpec((None, tm, w), lambda hf, s, i: (hf * N_FF + s, i, 0)),
                  out_shape=jax.ShapeDtypeStruct((2 * N_FF, t, w), F32),
                  compiler_params=_cparams(("parallel", "parallel", "parallel")))(hgu, hgu, dact)


def _adamw(name, parts, w, m, v, tr=256):
    r, c = w.shape
    tr = _tile(r, tr)
    bc1 = 1.0 - ADAM_B1 ** ADAM_STEP
    bc2 = 1.0 - ADAM_B2 ** ADAM_STEP

    def body(p_ref, w_ref, m_ref, v_ref, g_ref, d_ref, nm_ref, nv_ref):
        g = p_ref[0].astype(F32)
        for s in range(1, parts.shape[0]):
            g = g + p_ref[s].astype(F32)
        nm = ADAM_B1 * m_ref[...] + (1.0 - ADAM_B1) * g
        nv = ADAM_B2 * v_ref[...] + (1.0 - ADAM_B2) * jnp.square(g)
        g_ref[...] = g
        nm_ref[...] = nm
        nv_ref[...] = nv
        d_ref[...] = -ADAM_LR * ((nm / bc1) / (jnp.sqrt(nv / bc2) + ADAM_EPS) + ADAM_WD * w_ref[...])

    spec = pl.BlockSpec((tr, c), lambda i: (i, 0))
    return _pcall(body, name=name, grid=(r // tr,),
                  in_specs=[pl.BlockSpec((parts.shape[0], tr, c), lambda i: (0, i, 0)), spec, spec, spec],
                  out_specs=[spec] * 4, out_shape=[jax.ShapeDtypeStruct((r, c), F32)] * 4,
                  compiler_params=_cparams(("parallel",)))(parts, w, m, v)


def _sum_parts(name, parts, tr=256):
    _, r, c = parts.shape
    tr = _tile(r, tr)

    def body(p_ref, o_ref):
        g = p_ref[0]
        for s in range(1, parts.shape[0]):
            g = g + p_ref[s]
        o_ref[...] = g

    return _pcall(body, name=name, grid=(r // tr,),
                  in_specs=[pl.BlockSpec((parts.shape[0], tr, c), lambda i: (0, i, 0))],
                  out_specs=pl.BlockSpec((tr, c), lambda i: (i, 0)),
                  out_shape=jax.ShapeDtypeStruct((r, c), F32), compiler_params=_cparams(("parallel",)))(parts)


def _peer(k):
    x, y, c = lax.axis_index("x"), lax.axis_index("y"), lax.axis_index("c")
    return (x ^ ((k >> 2) & 1), y ^ ((k >> 1) & 1), c ^ (k & 1))


def _my_id():
    return 4 * lax.axis_index("x") + 2 * lax.axis_index("y") + lax.axis_index("c")


def _exchange(name, arrays, *, gather):
    n = len(arrays)

    def body(*refs):
        ins, outs = refs[:n], refs[n:2 * n]
        send_sems, recv_sems, local_sems = refs[2 * n:]
        me = _my_id()
        copies = []
        for a in range(n):
            src_of = (lambda d, a=a: ins[a]) if gather else (lambda d, a=a: ins[a].at[d])
            local = pltpu.make_async_copy(src_of(me), outs[a].at[me], local_sems.at[a])
            local.start()
            copies.append(local)
            for k in range(1, N_DEV):
                peer_id = me ^ k
                rc = pltpu.make_async_remote_copy(
                    src_ref=src_of(peer_id), dst_ref=outs[a].at[me],
                    send_sem=send_sems.at[a * N_DEV + k], recv_sem=recv_sems.at[a * N_DEV + k],
                    device_id=_peer(k), device_id_type=pl.DeviceIdType.MESH)
                rc.start()
                copies.append(rc)
        for cp in copies:
            cp.wait()

    hbm = pl.BlockSpec(memory_space=pltpu.HBM)
    if gather:
        out_shape = [jax.ShapeDtypeStruct((N_DEV,) + a.shape, a.dtype) for a in arrays]
    else:
        out_shape = [jax.ShapeDtypeStruct(a.shape, a.dtype) for a in arrays]
    return _pcall(body, name=name, in_specs=[hbm] * n, out_specs=[hbm] * n, out_shape=out_shape,
                  scratch_shapes=[pltpu.SemaphoreType.DMA((n * N_DEV,)), pltpu.SemaphoreType.DMA((n * N_DEV,)),
                                  pltpu.SemaphoreType.DMA((n,))])(*arrays)


def _row(v):
    return v.reshape(1, -1)


def _time_major_heads(c, bsz, seq):
    return c.reshape(bsz, seq, BLK)[:, :, :N_HEADS].transpose(0, 2, 1)


def _layer_fwd(h, p_l, w, bsz, seq):
    t = bsz * seq
    z = _mm_nn("z_proj", h, w['w_in7'])
    fl = _mm_nn("f_proj", h, w['w_inf'])
    logf, = _rowwise("logf_fwd", lambda f, b: (_f_logf(f, b),), [fl], [w['b_forget']], [(BLK, F32)], [])
    c = _scan("cumsum_fwd", None, logf, bsz, seq, reverse=False)
    ct = _time_major_heads(c, bsz, seq)
    cq, ck = ct[..., None], ct[:, :, None, :]
    att, lse = _attn_fwd(z, cq, ck, bsz, seq)
    xc = _conv_fwd(z, w['conv_w'], w['conv_b'], bsz, seq)
    a, u = _gate_fwd(xc, w['rg_w_a'], w['rg_w_x'], w['rg_b_a'], w['rg_b_x'], w['rg_lambda'])
    hs, hprev = _scan("lru_fwd", a, u, bsz, seq, reverse=False, with_prev=True)
    rnn, = _rowwise("rnn_out_fwd", lambda s, y: (_f_rnn_out(s, y),), [hs, (z, OFF_RY, D_MODEL)], [],
                    [(D_MODEL, F32)], [])
    ya = _mm_nn("branch_att", att, w['w_branch_att'])
    yb = _mm_nn("branch_rnn", rnn, w['w_branch_rnn'])
    merged, = _rowwise("merge_fwd", lambda *v: (_f_merge(*v),),
                       [(z, OFF_GA, D_MODEL), (z, OFF_GB, D_MODEL), ya, yb], [w['b_merge0'], w['b_merge1']],
                       [(D_MODEL, F32)], [])
    mix = _mm_nn("mix_out", merged, w['w_out'])
    h1, = _rowwise("ln_mix_fwd", lambda *v: (_f_resid_ln(*v),), [h, mix], [w['ln_mix_g'], w['ln_mix_b']],
                   [(D_MODEL, F32)], [])
    tm = _tile(t, 512)
    hgu = _mm("ffn_in", h1, w['w_ffn_in'], grid=(t // tm, 2 * N_FF, 1),
              a_spec=pl.BlockSpec((tm, D_MODEL), lambda i, s, k: (i, 0)),
              b_spec=pl.BlockSpec((None, D_MODEL, FF_SH), lambda i, s, k: (s, 0, 0)),
              o_spec=pl.BlockSpec((None, tm, FF_SH), lambda i, s, k: (s, i, 0)),
              out_shape=jax.ShapeDtypeStruct((2 * N_FF, t, FF_SH), F32), contract=NN)
    act = _act_fwd(hgu)
    ffn = _mm("ffn_out", act, w['w_ffn_out'], grid=(t // tm, 1, N_FF),
              a_spec=pl.BlockSpec((None, tm, FF_SH), lambda i, j, s: (s, i, 0)),
              b_spec=pl.BlockSpec((None, FF_SH, D_MODEL), lambda i, j, s: (s, 0, 0)),
              o_spec=pl.BlockSpec((tm, D_MODEL), lambda i, j, s: (i, 0)),
              out_shape=jax.ShapeDtypeStruct((t, D_MODEL), F32), contract=NN)
    h2, = _rowwise("ln_ffn_fwd", lambda *v: (_f_resid_ln(*v),), [h1, ffn], [w['ln_ffn_g'], w['ln_ffn_b']],
                   [(D_MODEL, F32)], [])
    gp = _mm_nn("ple_gate", h2, w['w_ple_gate'])
    pe = _mm_nn("ple_proj", p_l, w['w_ple'])
    h3, = _rowwise("ln_ple_fwd", lambda *v: (_f_ple(*v),), [h2, gp, pe],
                   [w['b_ple_gate'], w['ln_ple_g'], w['ln_ple_b']], [(D_MODEL, F32)], [])
    saved = dict(h=h, z=z, fl=fl, cq=cq, ck=ck, att=att, lse=lse, xc=xc, a=a, hprev=hprev, hs=hs, rnn=rnn,
                 ya=ya, yb=yb, merged=merged, mix=mix, h1=h1, hgu=hgu, act=act, ffn=ffn, h2=h2, gp=gp, pe=pe)
    return h3, saved


def _layer_bwd(dh3, p_l, w, s, bsz, seq):
    t = bsz * seq
    g = {}
    dh2, dgp, dpe, g['b_ple_gate'], g['ln_ple_g'], g['ln_ple_b'] = _vjp_rowwise(
        "ln_ple_bwd", _f_ple, [s['h2'], s['gp'], s['pe']], [w['b_ple_gate'], w['ln_ple_g'], w['ln_ple_b']], [dh3], 3)
    g['w_ple_gate'] = _mm_tn("ple_gate_dw", s['h2'], dgp, out_dtype=BF16)
    g['w_ple'] = _mm_tn("ple_proj_dw", p_l, dpe, out_dtype=BF16)
    dh2b = _mm_nt("ple_gate_dx", dgp, w['w_ple_gate'])
    dh1, dffn, g['ln_ffn_g'], g['ln_ffn_b'] = _ln_resid_bwd(
        "ln_ffn_bwd", s['h1'], s['ffn'], w['ln_ffn_g'], w['ln_ffn_b'], dh2, dh2b)
    tm = _tile(t, 512)
    dact = _mm("ffn_out_dx", dffn, w['w_ffn_out'], grid=(t // tm, N_FF, 1),
               a_spec=pl.BlockSpec((tm, D_MODEL), lambda i, ss, k: (i, 0)),
               b_spec=pl.BlockSpec((None, FF_SH, D_MODEL), lambda i, ss, k: (ss, 0, 0)),
               o_spec=pl.BlockSpec((None, tm, FF_SH), lambda i, ss, k: (ss, i, 0)),
               out_shape=jax.ShapeDtypeStruct((N_FF, t, FF_SH), F32), contract=NT)
    tk = _tile(t, 512)
    g['w_ffn_out'] = _mm("ffn_out_dw", s['act'], dffn, grid=(N_FF, 1, t // tk),
                         a_spec=pl.BlockSpec((None, tk, FF_SH), lambda ss, j, k: (ss, k, 0)),
                         b_spec=pl.BlockSpec((tk, D_MODEL), lambda ss, j, k: (k, 0)),
                         o_spec=pl.BlockSpec((None, FF_SH, D_MODEL), lambda ss, j, k: (ss, 0, 0)),
                         out_shape=jax.ShapeDtypeStruct((N_FF, FF_SH, D_MODEL), BF16), contract=TN)
    dhgu = _act_bwd(s['hgu'], dact)
    g['w_ffn_in'] = _mm("ffn_in_dw", s['h1'], dhgu, grid=(2 * N_FF, 1, t // tk),
                        a_spec=pl.BlockSpec((tk, D_MODEL), lambda ss, j, k: (k, 0)),
                        b_spec=pl.BlockSpec((None, tk, FF_SH), lambda ss, j, k: (ss, k, 0)),
                        o_spec=pl.BlockSpec((None, D_MODEL, FF_SH), lambda ss, j, k: (ss, 0, 0)),
                        out_shape=jax.ShapeDtypeStruct((2 * N_FF, D_MODEL, FF_SH), BF16), contract=TN)
    dh1b = _mm("ffn_in_dx", dhgu, w['w_ffn_in'], grid=(t // tm, 1, 2 * N_FF),
               a_spec=pl.BlockSpec((None, tm, FF_SH), lambda i, j, ss: (ss, i, 0)),
               b_spec=pl.BlockSpec((None, D_MODEL, FF_SH), lambda i, j, ss: (ss, 0, 0)),
               o_spec=pl.BlockSpec((tm, D_MODEL), lambda i, j, ss: (i, 0)),
               out_shape=jax.ShapeDtypeStruct((t, D_MODEL), F32), contract=NT)
    dh, dmix, g['ln_mix_g'], g['ln_mix_b'] = _ln_resid_bwd(
        "ln_mix_bwd", s['h'], s['mix'], w['ln_mix_g'], w['ln_mix_b'], dh1, dh1b)
    g['w_out'] = _mm_tn("mix_out_dw", s['merged'], dmix, out_dtype=BF16)
    dmerged = _mm_nt("mix_out_dx", dmix, w['w_out'])
    z = s['z']
    dga, dgb, dya, dyb, dbm0, dbm1 = _vjp_rowwise(
        "merge_bwd", _f_merge, [(z, OFF_GA, D_MODEL), (z, OFF_GB, D_MODEL), s['ya'], s['yb']],
        [w['b_merge0'], w['b_merge1']], [dmerged], 4)
    g['b_merge'] = jnp.concatenate([dbm0, dbm1], axis=0)
    g['w_branch_att'] = _mm_tn("branch_att_dw", s['att'], dya, out_dtype=BF16)
    g['w_branch_rnn'] = _mm_tn("branch_rnn_dw", s['rnn'], dyb, out_dtype=BF16)
    datt = _mm_nt("branch_att_dx", dya, w['w_branch_att'])
    drnn = _mm_nt("branch_rnn_dx", dyb, w['w_branch_rnn'])
    dhs, dry = _vjp_rowwise("rnn_out_bwd", _f_rnn_out, [s['hs'], (z, OFF_RY, D_MODEL)], [], [drnn], 2)
    lam = _scan("lru_bwd", s['a'], dhs, bsz, seq, reverse=True)
    da, = _rowwise("lru_da", lambda l, hp: (l * hp,), [lam, s['hprev']], [], [(D_MODEL, F32)], [])
    dxc, g['rg_w_a'], g['rg_w_x'], g['rg_b_a'], g['rg_b_x'], g['rg_lambda'] = _gate_bwd(
        s['xc'], w['rg_w_a'], w['rg_w_x'], w['rg_b_a'], w['rg_b_x'], w['rg_lambda'], da, lam)
    drx, g['conv_w'], g['conv_b'] = _conv_bwd(z, dxc, w['conv_w'], bsz, seq)
    dk, dv, dck = _attn_bwd_kv(z, s['att'], datt, s['lse'], s['cq'], s['ck'], bsz, seq)
    dq, dcq = _attn_bwd_q(z, s['att'], datt, s['lse'], s['cq'], s['ck'], bsz, seq)
    dc = (dcq[:, :, :, 0] + dck[:, :, 0, :]).transpose(0, 2, 1)
    dc = jnp.pad(dc, ((0, 0), (0, 0), (0, BLK - N_HEADS))).reshape(t, BLK)
    dlogf = _scan("cumsum_bwd", None, dc, bsz, seq, reverse=True)
    dfl, g['b_forget'] = _vjp_rowwise("logf_bwd", _f_logf, [s['fl']], [w['b_forget']], [dlogf], 1)
    dz = jnp.concatenate([dq, dk, dv, drx, dry, dga, dgb], axis=1)
    g['w_in7'] = _mm_tn("z_proj_dw", s['h'], dz, out_dtype=BF16)
    g['w_inf'] = _mm_tn("f_proj_dw", s['h'], dfl, out_dtype=BF16)
    dhz = _mm_nt("z_proj_dx", dz, w['w_in7'])
    dhf = _mm_nt("f_proj_dx", dfl, w['w_inf'])
    dh_in, = _rowwise("dh_sum", lambda x0, x1, x2: (x0 + x1 + x2,), [dh, dhz, dhf], [], [(D_MODEL, F32)], [])
    return dh_in, g


def _ln_resid_bwd(name, h, branch, gam, bet, d0, d1):
    def bwd(hv, bv, d0v, d1v, gv, btv):
        _, pull = jax.vjp(_f_resid_ln, hv, bv, gv, btv)
        dh, db, dg, dbt = pull(d0v + d1v)
        return dh, db, _colsum(dg), _colsum(dbt)

    return _rowwise(name, bwd, [h, branch, d0, d1], [gam, bet], [(D_MODEL, F32), (D_MODEL, F32)],
                    [D_MODEL, D_MODEL])


def _local_step(x2, tgt, p3, layer_w, g_in, b_in, bsz, seq):
    depth = len(layer_w)
    h, = _rowwise("ln_in_fwd", lambda xv, gv, bv: (_ln(xv, gv, bv),), [x2], [g_in, b_in], [(D_MODEL, F32)], [])
    saved = []
    for l in range(depth):
        h, s = _layer_fwd(h, p3[l], layer_w[l], bsz, seq)
        saved.append(s)

    def loss_fn(y, tv):
        err = y - tv
        return err * (1.0 / D_MODEL), _colsum(jnp.square(err))

    dh, sq = _rowwise("loss", loss_fn, [h, tgt], [], [(D_MODEL, F32)], [D_MODEL])
    grads = [None] * depth
    for l in reversed(range(depth)):
        dh, grads[l] = _layer_bwd(dh, p3[l], layer_w[l], saved[l], bsz, seq)
    dx, dg_in, db_in = _vjp_rowwise("ln_in_bwd", _ln, [x2], [g_in, b_in], [dh], 1)
    return sq, dx, grads, dg_in, db_in


def _layer_weights(full, l):
    w = {}
    wt = full['w_in'][:, l].transpose(1, 0, 2).reshape(D_MODEL, N_IN)
    w['w_in7'] = jnp.concatenate([wt[:, :3 * D_MODEL], wt[:, 3 * D_MODEL + N_HEADS:]], axis=1)
    w['w_inf'] = jnp.pad(wt[:, 3 * D_MODEL:3 * D_MODEL + N_HEADS], ((0, 0), (0, BLK - N_HEADS)))
    for n in ['w_branch_att', 'w_branch_rnn', 'w_out', 'w_ple_gate']:
        w[n] = full[n][:, l].reshape(D_MODEL, D_MODEL)
    w['w_ffn_in'] = full['w_ffn_in'][:, l]
    w['w_ffn_out'] = full['w_ffn_out'][:, l].reshape(N_FF, FF_SH, D_MODEL)
    w['w_ple'] = full['w_ple'][:, l].transpose(1, 0, 2).reshape(D_PLE, D_MODEL)
    w['conv_w'] = full['conv_w'][:, l].transpose(1, 0, 2).reshape(CONV_W, D_MODEL)
    bm = full['b_merge'][:, l].transpose(1, 0, 2).reshape(2, D_MODEL)
    w['b_merge0'], w['b_merge1'] = bm[0:1], bm[1:2]
    return w


def _by_destination(name, gw):
    if name == 'w_in':
        g7, gf = gw['w_in7'], gw['w_inf']
        true = jnp.concatenate([g7[:, :3 * D_MODEL], gf[:, :N_HEADS], g7[:, 3 * D_MODEL:]], axis=1)
        return true.reshape(D_MODEL, N_DEV, IN_SH).transpose(1, 0, 2)
    g = gw[name]
    if name in ('w_branch_att', 'w_branch_rnn', 'w_out', 'w_ple_gate'):
        return g.reshape(N_DEV, D_MODEL // N_DEV, D_MODEL)
    if name == 'w_ffn_in':
        return g
    if name == 'w_ffn_out':
        return g.reshape(N_DEV, N_FF * FF_SH // N_DEV, D_MODEL)
    return g.reshape(g.shape[0], N_DEV, BLK).transpose(1, 0, 2)


def kernel(x, p, ln_in_g, ln_in_b, w_in, b_forget, conv_w, conv_b, rg_w_a, rg_b_a, rg_w_x, rg_b_x, rg_lambda, w_branch_att, w_branch_rnn, b_merge, w_out, ln_mix_g, ln_mix_b, w_ffn_in, w_ffn_out, ln_ffn_g, ln_ffn_b, w_ple, w_ple_gate, b_ple_gate, ln_ple_g, ln_ple_b, loss_target, m_ln_in_g, m_ln_in_b, m_w_in, m_b_forget, m_conv_w, m_conv_b, m_rg_w_a, m_rg_b_a, m_rg_w_x, m_rg_b_x, m_rg_lambda, m_w_branch_att, m_w_branch_rnn, m_b_merge, m_w_out, m_ln_mix_g, m_ln_mix_b, m_w_ffn_in, m_w_ffn_out, m_ln_ffn_g, m_ln_ffn_b, m_w_ple, m_w_ple_gate, m_b_ple_gate, m_ln_ple_g, m_ln_ple_b, v_ln_in_g, v_ln_in_b, v_w_in, v_b_forget, v_conv_w, v_conv_b, v_rg_w_a, v_rg_b_a, v_rg_w_x, v_rg_b_x, v_rg_lambda, v_w_branch_att, v_w_branch_rnn, v_b_merge, v_w_out, v_ln_mix_g, v_ln_mix_b, v_w_ffn_in, v_w_ffn_out, v_ln_ffn_g, v_ln_ffn_b, v_w_ple, v_w_ple_gate, v_b_ple_gate, v_ln_ple_g, v_ln_ple_b):
    env = dict(locals())
    wts = {n: env[n] for n in WEIGHTS}
    mom = {n: env['m_' + n] for n in WEIGHTS}
    var = {n: env['v_' + n] for n in WEIGHTS}
    bsz, seq, _ = x.shape
    depth = w_in.shape[0]
    t = bsz * seq
    x2, tgt = x.reshape(t, D_MODEL), loss_target.reshape(t, D_MODEL)
    p3 = p.reshape(depth, t, D_PLE)

    shard_names = SHARDED_BF16 + SHARDED_F32
    shards = [wts[n].astype(BF16) for n in SHARDED_BF16] + [wts[n] for n in SHARDED_F32]
    full = dict(zip(shard_names, _exchange("gather_weights", shards, gather=True)))
    layer_w = []
    for l in range(depth):
        w = _layer_weights(full, l)
        for n in ['conv_b', 'rg_b_a', 'rg_b_x', 'rg_lambda', 'ln_mix_g', 'ln_mix_b', 'ln_ffn_g', 'ln_ffn_b',
                  'b_ple_gate', 'ln_ple_g', 'ln_ple_b']:
            w[n] = _row(wts[n][l])
        w['b_forget'] = jnp.pad(_row(b_forget[l]), ((0, 0), (0, BLK - N_HEADS)))
        w['rg_w_a'], w['rg_w_x'] = rg_w_a[l], rg_w_x[l]
        layer_w.append(w)

    g_in, b_in = _row(ln_in_g), _row(ln_in_b)
    sq, dx, grads, dg_in, db_in = _local_step(x2, tgt, p3, layer_w, g_in, b_in, bsz, seq)
    loss = lax.psum(0.5 * jnp.sum(sq) / D_MODEL, ("x", "y", "c"))
    grad_x = dx.reshape(bsz, seq, D_MODEL)

    parts = [jnp.stack([_by_destination(n, grads[l]) for l in range(depth)], axis=1) for n in shard_names]
    recv = dict(zip(shard_names, _exchange("scatter_grads", parts, gather=False)))
    out = {}
    for n in shard_names:
        shp = wts[n].shape
        flat = lambda a: a.reshape(-1, shp[-1])
        res = _adamw("adamw_" + n, recv[n].reshape(N_DEV, -1, shp[-1]), flat(wts[n]), flat(mom[n]), flat(var[n]))
        out[n] = [r.reshape(shp) for r in res]

    def rep_grad(n):
        if n == 'ln_in_g':
            return dg_in.reshape(-1)
        if n == 'ln_in_b':
            return db_in.reshape(-1)
        return jnp.stack([grads[l][n].reshape(wts[n].shape[1:]) if n != 'b_forget'
                          else grads[l][n][0, :N_HEADS] for l in range(depth)]).reshape(-1)

    sizes = [int(wts[n].size) for n in REPLICATED]
    total = sum(sizes)
    unit = N_DEV * 8 * BLK
    padded = -(-total // unit) * unit
    pack = lambda vals: jnp.pad(jnp.concatenate([v.reshape(-1) for v in vals]), (0, padded - total))
    rows = padded // N_DEV // BLK
    gp, = _exchange("scatter_small", [pack([rep_grad(n) for n in REPLICATED]).reshape(N_DEV, rows, BLK)],
                    gather=False)
    g_slice = _sum_parts("sum_small", gp)
    g_all, = _exchange("gather_small", [g_slice], gather=True)
    one = lambda a: a.reshape(1, padded // BLK, BLK)
    res = _adamw("adamw_small", one(g_all), *[pack([d[n] for n in REPLICATED]).reshape(padded // BLK, BLK)
                                             for d in (wts, mom, var)])
    offs = [sum(sizes[:i]) for i in range(len(sizes))]
    for n, o, sz in zip(REPLICATED, offs, sizes):
        out[n] = [r.reshape(-1)[o:o + sz].reshape(wts[n].shape) for r in res]

    return (loss, grad_x, *[out[n][k] for k in range(4) for n in WEIGHTS])
```

```python
import functools
import math

import jax
import jax.numpy as jnp
from jax import lax
from jax.experimental import pallas as pl
from jax.experimental.pallas import tpu as pltpu

F32 = jnp.float32
BF16 = jnp.bfloat16

N_DEV = 8
D_MODEL = 1024
N_HEADS = 8
HEAD_DIM = 128
N_BLK = 8
BLK = 128
CONV_W = 4
D_PLE = 256
FF_SH = 704
N_FF = 4
IN_SH = 897
N_IN = 7176
DEPTH = 4
RG_C = 8.0
ALPHA = float((2 * DEPTH) ** 0.25)
LN_EPS = 1e-5
SCALE = 1.0 / math.sqrt(HEAD_DIM)
NEG = -1e30
ADAM_LR, ADAM_B1, ADAM_B2, ADAM_EPS, ADAM_WD, ADAM_STEP = 0.001, 0.9, 0.999, 1e-08, 0.01, 10
OFF_Q, OFF_K, OFF_V, OFF_RX, OFF_RY, OFF_GA, OFF_GB = (i * D_MODEL for i in range(7))
Z7 = 7 * D_MODEL
V7X_VMEM_LIMIT = 48 * 1024 * 1024

WEIGHTS = ['ln_in_g', 'ln_in_b', 'w_in', 'b_forget', 'conv_w', 'conv_b', 'rg_w_a', 'rg_b_a', 'rg_w_x', 'rg_b_x',
           'rg_lambda', 'w_branch_att', 'w_branch_rnn', 'b_merge', 'w_out', 'ln_mix_g', 'ln_mix_b', 'w_ffn_in',
           'w_ffn_out', 'ln_ffn_g', 'ln_ffn_b', 'w_ple', 'w_ple_gate', 'b_ple_gate', 'ln_ple_g', 'ln_ple_b']
SHARDED_BF16 = ['w_in', 'w_branch_att', 'w_branch_rnn', 'w_out', 'w_ffn_in', 'w_ffn_out', 'w_ple', 'w_ple_gate']
SHARDED_F32 = ['conv_w', 'b_merge']
REPLICATED = [n for n in WEIGHTS if n not in SHARDED_BF16 and n not in SHARDED_F32]

NN = ((1,), (0,))
NT = ((1,), (1,))
TN = ((0,), (0,))


def _pcall(body, **kw):
    return pl.pallas_call(body, **kw)


def _tile(n, pref, mult=8):
    if n <= pref:
        return n
    t = (pref // mult) * mult
    while t >= mult:
        if n % t == 0:
            return t
        t -= mult
    return n


def _cparams(sem):
    return pltpu.CompilerParams(dimension_semantics=sem, vmem_limit_bytes=V7X_VMEM_LIMIT)


def _mm(name, a, b, *, grid, a_spec, b_spec, o_spec, out_shape, contract):
    nk = grid[-1]
    acc_shape = tuple(d for d in o_spec.block_shape if d is not None)

    def body(a_ref, b_ref, o_ref, acc_ref):
        k = pl.program_id(len(grid) - 1)
        part = lax.dot_general(a_ref[...].astype(BF16), b_ref[...].astype(BF16), (contract, ((), ())),
                               preferred_element_type=F32)

        @pl.when(k == 0)
        def _():
            acc_ref[...] = part

        @pl.when(k > 0)
        def _():
            acc_ref[...] += part

        @pl.when(k == nk - 1)
        def _():
            o_ref[...] = acc_ref[...].astype(o_ref.dtype)

    sem = ("parallel",) * (len(grid) - 1) + ("arbitrary",)
    return _pcall(body, name=name, grid=grid, in_specs=[a_spec, b_spec], out_specs=o_spec, out_shape=out_shape,
                  scratch_shapes=[pltpu.VMEM(acc_shape, F32)], compiler_params=_cparams(sem))(a, b)


def _mm_nn(name, a, b, *, a_off=0, out_dtype=F32, tm=512, tn=1024, tk=1024):
    m = a.shape[0]
    k, n = b.shape
    tm, tn, tk = _tile(m, tm), _tile(n, tn, 128), _tile(k, tk, 128)
    ko = a_off // tk
    return _mm(name, a, b, grid=(m // tm, n // tn, k // tk),
               a_spec=pl.BlockSpec((tm, tk), lambda i, j, kk: (i, kk + ko)),
               b_spec=pl.BlockSpec((tk, tn), lambda i, j, kk: (kk, j)),
               o_spec=pl.BlockSpec((tm, tn), lambda i, j, kk: (i, j)),
               out_shape=jax.ShapeDtypeStruct((m, n), out_dtype), contract=NN)


def _mm_nt(name, a, b, *, out_dtype=F32, tm=512, tn=1024, tk=1024):
    m, k = a.shape
    n = b.shape[0]
    tm, tn, tk = _tile(m, tm), _tile(n, tn, 128), _tile(k, tk, 128)
    return _mm(name, a, b, grid=(m // tm, n // tn, k // tk),
               a_spec=pl.BlockSpec((tm, tk), lambda i, j, kk: (i, kk)),
               b_spec=pl.BlockSpec((tn, tk), lambda i, j, kk: (j, kk)),
               o_spec=pl.BlockSpec((tm, tn), lambda i, j, kk: (i, j)),
               out_shape=jax.ShapeDtypeStruct((m, n), out_dtype), contract=NT)


def _mm_tn(name, a, b, *, a_off=0, m=None, out_dtype=F32, tm=512, tn=1024, tk=512):
    t, n = b.shape
    m = a.shape[1] if m is None else m
    tm, tn, tk = _tile(m, tm, 128), _tile(n, tn, 128), _tile(t, tk)
    mo = a_off // tm
    return _mm(name, a, b, grid=(m // tm, n // tn, t // tk),
               a_spec=pl.BlockSpec((tk, tm), lambda i, j, kk: (kk, i + mo)),
               b_spec=pl.BlockSpec((tk, tn), lambda i, j, kk: (kk, j)),
               o_spec=pl.BlockSpec((tm, tn), lambda i, j, kk: (i, j)),
               out_shape=jax.ShapeDtypeStruct((m, n), out_dtype), contract=TN)


def _rowwise(name, fn, rows, params, out_rows, out_reds, tm=256):
    rows = [r if isinstance(r, tuple) else (r, 0, r.shape[1]) for r in rows]
    t = rows[0][0].shape[0]
    tm = _tile(t, tm)
    in_specs = []
    for _, off, w in rows:
        in_specs.append(pl.BlockSpec((tm, w), functools.partial(lambda i, cb: (i, cb), cb=off // w)))
    for p in params:
        in_specs.append(pl.BlockSpec((1, p.shape[1]), lambda i: (0, 0)))
    out_specs = [pl.BlockSpec((tm, w), lambda i: (i, 0)) for w, _ in out_rows]
    out_specs += [pl.BlockSpec((1, w), lambda i: (0, 0)) for w in out_reds]
    out_shape = [jax.ShapeDtypeStruct((t, w), dt) for w, dt in out_rows]
    out_shape += [jax.ShapeDtypeStruct((1, w), F32) for w in out_reds]
    nr, npar, nor = len(rows), len(params), len(out_rows)

    def body(*refs):
        ins, outs = refs[:nr + npar], refs[nr + npar:]
        vals = [r[...].astype(F32) for r in ins[:nr]]
        vals += [jnp.broadcast_to(r[...], (tm, r.shape[1])) for r in ins[nr:]]
        res = fn(*vals)
        step = pl.program_id(0)
        for o, v in zip(outs[:nor], res[:nor]):
            o[...] = v.astype(o.dtype)
        for o, v in zip(outs[nor:], res[nor:]):
            _accumulate(o, v, step)

    res = _pcall(body, name=name, grid=(t // tm,), in_specs=in_specs, out_specs=out_specs, out_shape=out_shape,
                 compiler_params=_cparams(("arbitrary",)))(*[r[0] for r in rows], *params)
    return res


def _accumulate(o_ref, v, step):
    @pl.when(step == 0)
    def _():
        o_ref[...] = v

    @pl.when(step > 0)
    def _():
        o_ref[...] += v


def _colsum(v):
    return jnp.sum(v, axis=0, keepdims=True)


def _vjp_rowwise(name, fn, rows, params, cots, n_row_grads, tm=256):
    nr, npar, nc = len(rows), len(params), len(cots)

    def bwd(*vals):
        prim, par, ct = vals[:nr], vals[nr + nc:], vals[nr:nr + nc]
        _, pull = jax.vjp(fn, *prim, *par)
        grads = pull(tuple(ct) if nc > 1 else ct[0])
        return tuple(grads[:n_row_grads]) + tuple(_colsum(g) for g in grads[nr:])

    widths = [(r[2] if isinstance(r, tuple) else r.shape[1], F32) for r in rows[:n_row_grads]]
    return _rowwise(name, bwd, list(rows) + list(cots), params, widths, [p.shape[1] for p in params], tm=tm)


def _ln(s, g, b):
    mu = jnp.mean(s, axis=-1, keepdims=True)
    var = jnp.mean(jnp.square(s - mu), axis=-1, keepdims=True)
    return (s - mu) * lax.rsqrt(var + LN_EPS) * g + b


def _softplus(x):
    return jnp.maximum(x, 0.0) + jnp.log1p(jnp.exp(-jnp.abs(x)))


def _expm1(x):
    series = x * (1.0 + x * (1.0 / 2 + x * (1.0 / 6 + x * (1.0 / 24 + x * (1.0 / 120 + x * (1.0 / 720))))))
    return jnp.where(jnp.abs(x) < 0.25, series, jnp.exp(x) - 1.0)


def _f_resid_ln(h, branch, g, b):
    return _ln(ALPHA * h + branch, g, b)


def _f_ple(h, gp, pe, bpg, g, b):
    return _ln(ALPHA * h + jax.nn.sigmoid(gp + bpg) * pe, g, b)


def _f_merge(ga, gb, ya, yb, bm0, bm1):
    return jax.nn.sigmoid(ga + bm0) * ya + jax.nn.sigmoid(gb + bm1) * yb


def _f_rnn_out(hs, ry):
    return hs * jax.nn.gelu(ry, approximate=True)


def _f_logf(fl, bf):
    return -_softplus(-(fl + bf))


def _f_gate(xc, ra, ia, lam, ba, bx):
    r = jax.nn.sigmoid(ra + ba)
    i = jax.nn.sigmoid(ia + bx)
    log_a = -RG_C * _softplus(-lam) * r
    a = jnp.exp(log_a)
    mult = jnp.sqrt(-_expm1(2.0 * log_a))
    return a, mult * (i * xc)


def _f_act(hg, hu):
    return jax.nn.silu(hg) * hu


ATT_BLOCK = 512


def _scores(q, k, cq, ck, diagonal):
    s = lax.dot_general(q, k, (NT, ((), ())), preferred_element_type=F32) * SCALE
    s = s + cq - ck
    if diagonal:
        row = lax.broadcasted_iota(jnp.int32, s.shape, 0)
        col = lax.broadcasted_iota(jnp.int32, s.shape, 1)
        s = jnp.where(col <= row, s, NEG)
    return s


def _dscores(p, do, o, v):
    dob = do.astype(BF16)
    delta = jnp.sum(dob.astype(F32) * o, axis=1, keepdims=True)
    dp = lax.dot_general(dob, v.astype(BF16), (NT, ((), ())), preferred_element_type=F32)
    return p * (dp - delta)


def _attn_fwd(z, cq, ck, bsz, seq):
    t = bsz * seq
    tq = _tile(seq, ATT_BLOCK)
    nq = seq // tq

    def body(q_ref, k_ref, v_ref, cq_ref, ck_ref, o_ref, lse_ref):
        for i in range(nq):
            rows = slice(i * tq, (i + 1) * tq)
            q = q_ref[rows, :].astype(BF16)
            cqi = cq_ref[rows, :]

            def step(j, carry, diagonal, q=q, cqi=cqi):
                m, l, acc = carry
                keys = pl.ds(pl.multiple_of(j * tq, tq), tq)
                s = _scores(q, k_ref[keys, :].astype(BF16), cqi, ck_ref[pl.ds(j, 1), :], diagonal)
                m_new = jnp.maximum(m, jnp.max(s, axis=1, keepdims=True))
                alpha = jnp.exp(m - m_new)
                p = jnp.exp(s - m_new)
                p_hi = p.astype(BF16)
                p_lo = (p - p_hi.astype(F32)).astype(BF16)
                vb = v_ref[keys, :].astype(BF16)
                pv = lax.dot_general(p_hi, vb, (NN, ((), ())), preferred_element_type=F32)
                pv = pv + lax.dot_general(p_lo, vb, (NN, ((), ())), preferred_element_type=F32)
                return m_new, alpha * l + jnp.sum(p, axis=1, keepdims=True), alpha * acc + pv

            carry = (jnp.full((tq, 1), NEG, F32), jnp.zeros((tq, 1), F32), jnp.zeros((tq, HEAD_DIM), F32))
            if i > 0:
                carry = lax.fori_loop(0, i, functools.partial(step, diagonal=False), carry)
            m, l, acc = step(i, carry, True)
            o_ref[rows, :] = acc / l
            lse_ref[rows, :] = m + jnp.log(l)

    head = (seq, HEAD_DIM)
    in_specs = [
        pl.BlockSpec(head, lambda b, h: (b, h)),
        pl.BlockSpec(head, lambda b, h: (b, N_HEADS + h)),
        pl.BlockSpec(head, lambda b, h: (b, 2 * N_HEADS + h)),
        pl.BlockSpec((None, None, seq, 1), lambda b, h: (b, h, 0, 0)),
        pl.BlockSpec((None, None, nq, tq), lambda b, h: (b, h, 0, 0)),
    ]
    out_specs = [pl.BlockSpec(head, lambda b, h: (b, h)),
                 pl.BlockSpec((None, None, seq, 1), lambda b, h: (b, h, 0, 0))]
    out_shape = [jax.ShapeDtypeStruct((t, D_MODEL), F32), jax.ShapeDtypeStruct((bsz, N_HEADS, seq, 1), F32)]
    return _pcall(body, name="attn_fwd", grid=(bsz, N_HEADS), in_specs=in_specs, out_specs=out_specs,
                  out_shape=out_shape, compiler_params=_cparams(("parallel", "parallel")))(
                      z, z, z, cq, ck.reshape(bsz, N_HEADS, nq, tq))


def _attn_bwd(z, att, datt, lse, cq, ck, bsz, seq):
    t = bsz * seq
    tq = _tile(seq, ATT_BLOCK)
    nq = seq // tq

    def body(q_ref, k_ref, v_ref, o_ref, do_ref, lse_ref, cq_ref, ck_ref,
             dq_ref, dk_ref, dv_ref, dcq_ref, dck_ref):
        dq_ref[...] = jnp.zeros_like(dq_ref)
        dcq_ref[...] = jnp.zeros_like(dcq_ref)
        for j in range(nq):
            keys = slice(j * tq, (j + 1) * tq)
            kb = k_ref[keys, :].astype(BF16)
            vb = v_ref[keys, :].astype(BF16)
            ckj = ck_ref[j:j + 1, :]

            def step(i, carry, diagonal, kb=kb, vb=vb, ckj=ckj):
                dk, dv, dc = carry
                rows = pl.ds(pl.multiple_of(i * tq, tq), tq)
                qb = q_ref[rows, :].astype(BF16)
                do = do_ref[rows, :]
                s = _scores(qb, kb, cq_ref[rows, :], ckj, diagonal)
                p = jnp.exp(s - lse_ref[rows, :])
                ds = _dscores(p, do, o_ref[rows, :], vb)
                dsb = (ds * SCALE).astype(BF16)
                dq_ref[rows, :] += lax.dot_general(dsb, kb, (NN, ((), ())), preferred_element_type=F32)
                dcq_ref[rows, :] += jnp.sum(ds, axis=1, keepdims=True)
                dv = dv + lax.dot_general(p.astype(BF16), do.astype(BF16), (TN, ((), ())),
                                          preferred_element_type=F32)
                dk = dk + lax.dot_general(dsb, qb, (TN, ((), ())), preferred_element_type=F32)
                return dk, dv, dc - jnp.sum(ds, axis=0, keepdims=True)

            zero = jnp.zeros((tq, HEAD_DIM), F32)
            carry = step(j, (zero, zero, jnp.zeros((1, tq), F32)), True)
            if j + 1 < nq:
                carry = lax.fori_loop(j + 1, nq, functools.partial(step, diagonal=False), carry)
            dk_ref[keys, :], dv_ref[keys, :], dck_ref[j:j + 1, :] = carry

    head = (seq, HEAD_DIM)
    hmap = lambda b, h: (b, h)
    col = pl.BlockSpec((None, None, seq, 1), lambda b, h: (b, h, 0, 0))
    row = pl.BlockSpec((None, None, nq, tq), lambda b, h: (b, h, 0, 0))
    in_specs = [pl.BlockSpec(head, hmap),
                pl.BlockSpec(head, lambda b, h: (b, N_HEADS + h)),
                pl.BlockSpec(head, lambda b, h: (b, 2 * N_HEADS + h)),
                pl.BlockSpec(head, hmap), pl.BlockSpec(head, hmap), col, col, row]
    big = jax.ShapeDtypeStruct((t, D_MODEL), F32)
    return _pcall(body, name="attn_bwd", grid=(bsz, N_HEADS), in_specs=in_specs,
                  out_specs=[pl.BlockSpec(head, hmap)] * 3 + [col, row],
                  out_shape=[big, big, big, jax.ShapeDtypeStruct((bsz, N_HEADS, seq, 1), F32),
                             jax.ShapeDtypeStruct((bsz, N_HEADS, nq, tq), F32)],
                  compiler_params=_cparams(("parallel", "parallel")))(
                      z, z, z, att, datt, lse, cq, ck.reshape(bsz, N_HEADS, nq, tq))


def _scan(name, a, u, bsz, seq, *, reverse, with_prev=False, tb=256):
    c = u.shape[1]
    tb = _tile(seq, tb)
    nb = seq // tb
    has_a = a is not None

    def body(*refs):
        if has_a:
            a_ref, u_ref = refs[0], refs[1]
            rest = refs[2:]
        else:
            u_ref = refs[0]
            rest = refs[1:]
        outs = rest[:2] if with_prev else rest[:1]
        carry_sc, afirst_sc = rest[-2], rest[-1]
        step = pl.program_id(1)

        @pl.when(step == 0)
        def _():
            carry_sc[...] = jnp.zeros_like(carry_sc)
            afirst_sc[...] = jnp.zeros_like(afirst_sc)

        row = lax.broadcasted_iota(jnp.int32, (tb, c), 0)
        uu = u_ref[...]
        if has_a:
            aa = a_ref[...]
            if reverse:
                coef = jnp.where(row < tb - 1, pltpu.roll(aa, tb - 1, 0), afirst_sc[...])
            else:
                coef = aa
        k = 1
        while k < tb:
            shift = tb - k if reverse else k
            keep = (row < tb - k) if reverse else (row >= k)
            uu_sh = jnp.where(keep, pltpu.roll(uu, shift, 0), 0.0)
            if has_a:
                uu = coef * uu_sh + uu
                coef = coef * jnp.where(keep, pltpu.roll(coef, shift, 0), 1.0)
            else:
                uu = uu + uu_sh
            k *= 2
        carry = carry_sc[...]
        h = uu + coef * carry if has_a else uu + carry
        outs[0][...] = h
        if with_prev:
            outs[1][...] = jnp.where(row >= 1, pltpu.roll(h, 1, 0), carry)
        if reverse:
            carry_sc[...] = outs[0][0:1, :]
            if has_a:
                afirst_sc[...] = a_ref[0:1, :]
        else:
            carry_sc[...] = outs[0][tb - 1:tb, :]

    if reverse:
        imap = lambda b, s: (b * nb + nb - 1 - s, 0)
    else:
        imap = lambda b, s: (b * nb + s, 0)
    spec = pl.BlockSpec((tb, c), imap)
    n_in = 2 if has_a else 1
    n_out = 2 if with_prev else 1
    res = _pcall(body, name=name, grid=(bsz, nb), in_specs=[spec] * n_in, out_specs=[spec] * n_out,
                 out_shape=[jax.ShapeDtypeStruct(u.shape, F32)] * n_out,
                 scratch_shapes=[pltpu.VMEM((1, c), F32), pltpu.VMEM((1, c), F32)],
                 compiler_params=_cparams(("parallel", "arbitrary")))(*([a, u] if has_a else [u]))
    return res if with_prev else res[0]


def _conv_fwd(z, w, b, bsz, seq, tb=256):
    c = D_MODEL
    t = bsz * seq
    tb = _tile(seq, tb)
    nb = seq // tb

    def body(x_ref, w_ref, b_ref, o_ref, tail_sc):
        step = pl.program_id(1)

        @pl.when(step == 0)
        def _():
            tail_sc[...] = jnp.zeros_like(tail_sc)

        x = x_ref[...]
        row8 = lax.broadcasted_iota(jnp.int32, (8, c), 0)
        tail = tail_sc[...]
        acc = w_ref[CONV_W - 1:CONV_W, :] * x + b_ref[...]
        for sh in range(1, CONV_W):
            xs = pltpu.roll(x, sh, 0)
            top = jnp.where(row8 < sh, pltpu.roll(tail, sh, 0), xs[0:8, :])
            xs = jnp.concatenate([top, xs[8:, :]], axis=0) if tb > 8 else top
            acc = acc + w_ref[CONV_W - 1 - sh:CONV_W - sh, :] * xs
        o_ref[...] = acc
        tail_sc[...] = x_ref[tb - 8:tb, :]

    return _pcall(body, name="conv_fwd", grid=(bsz, nb),
                  in_specs=[pl.BlockSpec((tb, c), lambda bb, s: (bb * nb + s, OFF_RX // c)),
                            pl.BlockSpec((CONV_W, c), lambda bb, s: (0, 0)),
                            pl.BlockSpec((1, c), lambda bb, s: (0, 0))],
                  out_specs=pl.BlockSpec((tb, c), lambda bb, s: (bb * nb + s, 0)),
                  out_shape=jax.ShapeDtypeStruct((t, c), F32),
                  scratch_shapes=[pltpu.VMEM((8, c), F32)],
                  compiler_params=_cparams(("parallel", "arbitrary")))(z, w, b)


def _conv_bwd(z, dxc, w, bsz, seq, tb=256):
    c = D_MODEL
    t = bsz * seq
    tb = _tile(seq, tb)
    nb = seq // tb

    def body(x_ref, g_ref, w_ref, dx_ref, dw_ref, db_ref, head_sc):
        bb, step = pl.program_id(0), pl.program_id(1)

        @pl.when(step == 0)
        def _():
            head_sc[...] = jnp.zeros_like(head_sc)

        x, g = x_ref[...], g_ref[...]
        row8 = lax.broadcasted_iota(jnp.int32, (8, c), 0)
        head = head_sc[...]
        dx = w_ref[CONV_W - 1:CONV_W, :] * g
        dws = [None] * CONV_W
        dws[CONV_W - 1] = _colsum(g * x)
        for sh in range(1, CONV_W):
            gs = pltpu.roll(g, tb - sh, 0)
            bot = jnp.where(row8 >= 8 - sh, pltpu.roll(head, 8 - sh, 0), gs[tb - 8:tb, :])
            gs = jnp.concatenate([gs[:tb - 8, :], bot], axis=0) if tb > 8 else bot
            dx = dx + w_ref[CONV_W - 1 - sh:CONV_W - sh, :] * gs
            dws[CONV_W - 1 - sh] = _colsum(gs * x)
        dx_ref[...] = dx
        first = (bb == 0) & (step == 0)
        dw = jnp.concatenate(dws, axis=0)
        db = _colsum(g)

        @pl.when(first)
        def _():
            dw_ref[...] = dw
            db_ref[...] = db

        @pl.when(jnp.logical_not(first))
        def _():
            dw_ref[...] += dw
            db_ref[...] += db

        head_sc[...] = g_ref[0:8, :]

    rmap = lambda bb, s: (bb * nb + nb - 1 - s, 0)
    return _pcall(body, name="conv_bwd", grid=(bsz, nb),
                  in_specs=[pl.BlockSpec((tb, c), lambda bb, s: (bb * nb + nb - 1 - s, OFF_RX // c)),
                            pl.BlockSpec((tb, c), rmap),
                            pl.BlockSpec((CONV_W, c), lambda bb, s: (0, 0))],
                  out_specs=[pl.BlockSpec((tb, c), rmap),
                             pl.BlockSpec((CONV_W, c), lambda bb, s: (0, 0)),
                             pl.BlockSpec((1, c), lambda bb, s: (0, 0))],
                  out_shape=[jax.ShapeDtypeStruct((t, c), F32), jax.ShapeDtypeStruct((CONV_W, c), F32),
                             jax.ShapeDtypeStruct((1, c), F32)],
                  scratch_shapes=[pltpu.VMEM((8, c), F32)],
                  compiler_params=_cparams(("arbitrary", "arbitrary")))(z, dxc, w)


def _gate_fwd(xc, w_a, w_x, b_a, b_x, lam, tm=512):
    t = xc.shape[0]
    tm = _tile(t, tm)

    def body(xc_ref, wa_ref, wx_ref, ba_ref, bx_ref, lam_ref, a_ref, u_ref):
        xc_b = xc_ref[...]
        xb = xc_b.astype(BF16)
        ra = lax.dot_general(xb, wa_ref[...].astype(BF16), (NN, ((), ())), preferred_element_type=F32)
        ia = lax.dot_general(xb, wx_ref[...].astype(BF16), (NN, ((), ())), preferred_element_type=F32)
        a, u = _f_gate(xc_b, ra, ia, lam_ref[...], ba_ref[...], bx_ref[...])
        a_ref[...] = a
        u_ref[...] = u

    row = pl.BlockSpec((tm, BLK), lambda n, i: (i, n))
    wsp = pl.BlockSpec((None, BLK, BLK), lambda n, i: (n, 0, 0))
    vec = pl.BlockSpec((1, BLK), lambda n, i: (0, n))
    return _pcall(body, name="gate_fwd", grid=(N_BLK, t // tm), in_specs=[row, wsp, wsp, vec, vec, vec],
                  out_specs=[row, row], out_shape=[jax.ShapeDtypeStruct((t, D_MODEL), F32)] * 2,
                  compiler_params=_cparams(("parallel", "parallel")))(xc, w_a, w_x, b_a, b_x, lam)


def _gate_bwd(xc, w_a, w_x, b_a, b_x, lam, da, du, tm=512):
    t = xc.shape[0]
    tm = _tile(t, tm)

    def body(xc_ref, wa_ref, wx_ref, ba_ref, bx_ref, lam_ref, da_ref, du_ref,
             dxc_ref, dwa_ref, dwx_ref, dba_ref, dbx_ref, dlam_ref):
        step = pl.program_id(1)
        xc_b = xc_ref[...]
        xb = xc_b.astype(BF16)
        wa, wx = wa_ref[...].astype(BF16), wx_ref[...].astype(BF16)
        ra = lax.dot_general(xb, wa, (NN, ((), ())), preferred_element_type=F32)
        ia = lax.dot_general(xb, wx, (NN, ((), ())), preferred_element_type=F32)
        full = lambda r: jnp.broadcast_to(r[...], (tm, BLK))
        _, pull = jax.vjp(_f_gate, xc_b, ra, ia, full(lam_ref), full(ba_ref), full(bx_ref))
        dxc, dra, dia, dlam, dba, dbx = pull((da_ref[...], du_ref[...]))
        drb, dib = dra.astype(BF16), dia.astype(BF16)
        dxc = dxc + lax.dot_general(drb, wa, (NT, ((), ())), preferred_element_type=F32)
        dxc = dxc + lax.dot_general(dib, wx, (NT, ((), ())), preferred_element_type=F32)
        dxc_ref[...] = dxc
        _accumulate(dwa_ref, lax.dot_general(xb, drb, (TN, ((), ())), preferred_element_type=F32), step)
        _accumulate(dwx_ref, lax.dot_general(xb, dib, (TN, ((), ())), preferred_element_type=F32), step)
        _accumulate(dba_ref, _colsum(dba), step)
        _accumulate(dbx_ref, _colsum(dbx), step)
        _accumulate(dlam_ref, _colsum(dlam), step)

    row = pl.BlockSpec((tm, BLK), lambda n, i: (i, n))
    wsp = pl.BlockSpec((None, BLK, BLK), lambda n, i: (n, 0, 0))
    vec = pl.BlockSpec((1, BLK), lambda n, i: (0, n))
    wshape = jax.ShapeDtypeStruct((N_BLK, BLK, BLK), F32)
    vshape = jax.ShapeDtypeStruct((1, D_MODEL), F32)
    return _pcall(body, name="gate_bwd", grid=(N_BLK, t // tm),
                  in_specs=[row, wsp, wsp, vec, vec, vec, row, row],
                  out_specs=[row, wsp, wsp, vec, vec, vec],
                  out_shape=[jax.ShapeDtypeStruct((t, D_MODEL), F32), wshape, wshape, vshape, vshape, vshape],
                  compiler_params=_cparams(("parallel", "arbitrary")))(xc, w_a, w_x, b_a, b_x, lam, da, du)


def _act_fwd(hgu, tm=512):
    _, t, w = hgu.shape
    tm = _tile(t, tm)

    def body(hg_ref, hu_ref, o_ref):
        o_ref[...] = _f_act(hg_ref[...], hu_ref[...])

    spec = lambda off: pl.BlockSpec((None, tm, w), lambda s, i: (s + off, i, 0))
    return _pcall(body, name="act_fwd", grid=(N_FF, t // tm), in_specs=[spec(0), spec(N_FF)], out_specs=spec(0),
                  out_shape=jax.ShapeDtypeStruct((N_FF, t, w), F32),
                  compiler_params=_cparams(("parallel", "parallel")))(hgu, hgu)


def _act_bwd(hgu, dact, tm=512):
    _, t, w = hgu.shape
    tm = _tile(t, tm)

    def body(hg_ref, hu_ref, d_ref, o_ref):
        half = pl.program_id(0)
        _, pull = jax.vjp(_f_act, hg_ref[...], hu_ref[...])
        dhg, dhu = pull(d_ref[...])
        o_ref[...] = jnp.where(half == 0, dhg, dhu)

    spec = lambda off: pl.BlockSpec((None, tm, w), lambda hf, s, i: (s + off, i, 0))
    return _pcall(body, name="act_bwd", grid=(2, N_FF, t // tm), in_specs=[spec(0), spec(N_FF), spec(0)],
                  out_specs=pl.BlockSpec((None, tm, w), lambda hf, s, i: (hf * N_FF + s, i, 0)),
                  out_shape=jax.ShapeDtypeStruct((2 * N_FF, t, w), F32),
                  compiler_params=_cparams(("parallel", "parallel", "parallel")))(hgu, hgu, dact)


def _adamw(name, parts, w, m, v, tr=256):
    r, c = w.shape
    tr = _tile(r, tr)
    bc1 = 1.0 - ADAM_B1 ** ADAM_STEP
    bc2 = 1.0 - ADAM_B2 ** ADAM_STEP

    def body(p_ref, w_ref, m_ref, v_ref, g_ref, d_ref, nm_ref, nv_ref):
        g = p_ref[0].astype(F32)
        for s in range(1, parts.shape[0]):
            g = g + p_ref[s].astype(F32)
        nm = ADAM_B1 * m_ref[...] + (1.0 - ADAM_B1) * g
        nv = ADAM_B2 * v_ref[...] + (1.0 - ADAM_B2) * jnp.square(g)
        g_ref[...] = g
        nm_ref[...] = nm
        nv_ref[...] = nv
        d_ref[...] = -ADAM_LR * ((nm / bc1) / (jnp.sqrt(nv / bc2) + ADAM_EPS) + ADAM_WD * w_ref[...])

    spec = pl.BlockSpec((tr, c), lambda i: (i, 0))
    return _pcall(body, name=name, grid=(r // tr,),
                  in_specs=[pl.BlockSpec((parts.shape[0], tr, c), lambda i: (0, i, 0)), spec, spec, spec],
                  out_specs=[spec] * 4, out_shape=[jax.ShapeDtypeStruct((r, c), F32)] * 4,
                  compiler_params=_cparams(("parallel",)))(parts, w, m, v)


def _sum_parts(name, parts, tr=256):
    _, r, c = parts.shape
    tr = _tile(r, tr)

    def body(p_ref, o_ref):
        g = p_ref[0]
        for s in range(1, parts.shape[0]):
            g = g + p_ref[s]
        o_ref[...] = g

    return _pcall(body, name=name, grid=(r // tr,),
                  in_specs=[pl.BlockSpec((parts.shape[0], tr, c), lambda i: (0, i, 0))],
                  out_specs=pl.BlockSpec((tr, c), lambda i: (i, 0)),
                  out_shape=jax.ShapeDtypeStruct((r, c), F32), compiler_params=_cparams(("parallel",)))(parts)


def _peer(k):
    x, y, c = lax.axis_index("x"), lax.axis_index("y"), lax.axis_index("c")
    return (x ^ ((k >> 2) & 1), y ^ ((k >> 1) & 1), c ^ (k & 1))


def _my_id():
    return 4 * lax.axis_index("x") + 2 * lax.axis_index("y") + lax.axis_index("c")


def _exchange(name, arrays, *, gather):
    n = len(arrays)

    def body(*refs):
        ins, outs = refs[:n], refs[n:2 * n]
        send_sems, recv_sems, local_sems = refs[2 * n:]
        me = _my_id()
        copies = []
        for a in range(n):
            src_of = (lambda d, a=a: ins[a]) if gather else (lambda d, a=a: ins[a].at[d])
            local = pltpu.make_async_copy(src_of(me), outs[a].at[me], local_sems.at[a])
            local.start()
            copies.append(local)
            for k in range(1, N_DEV):
                peer_id = me ^ k
                rc = pltpu.make_async_remote_copy(
                    src_ref=src_of(peer_id), dst_ref=outs[a].at[me],
                    send_sem=send_sems.at[a * N_DEV + k], recv_sem=recv_sems.at[a * N_DEV + k],
                    device_id=_peer(k), device_id_type=pl.DeviceIdType.MESH)
                rc.start()
                copies.append(rc)
        for cp in copies:
            cp.wait()

    hbm = pl.BlockSpec(memory_space=pltpu.HBM)
    if gather:
        out_shape = [jax.ShapeDtypeStruct((N_DEV,) + a.shape, a.dtype) for a in arrays]
    else:
        out_shape = [jax.ShapeDtypeStruct(a.shape, a.dtype) for a in arrays]
    return _pcall(body, name=name, in_specs=[hbm] * n, out_specs=[hbm] * n, out_shape=out_shape,
                  scratch_shapes=[pltpu.SemaphoreType.DMA((n * N_DEV,)), pltpu.SemaphoreType.DMA((n * N_DEV,)),
                                  pltpu.SemaphoreType.DMA((n,))])(*arrays)


def _row(v):
    return v.reshape(1, -1)


def _time_major_heads(c, bsz, seq):
    return c.reshape(bsz, seq, BLK)[:, :, :N_HEADS].transpose(0, 2, 1)


def _layer_fwd(h, p_l, w, bsz, seq):
    t = bsz * seq
    z = _mm_nn("z_proj", h, w['w_in7'])
    fl = _mm_nn("f_proj", h, w['w_inf'])
    logf, = _rowwise("logf_fwd", lambda f, b: (_f_logf(f, b),), [fl], [w['b_forget']], [(BLK, F32)], [])
    c = _scan("cumsum_fwd", None, logf, bsz, seq, reverse=False)
    ct = _time_major_heads(c, bsz, seq)
    cq, ck = ct[..., None], ct[:, :, None, :]
    att, lse = _attn_fwd(z, cq, ck, bsz, seq)
    xc = _conv_fwd(z, w['conv_w'], w['conv_b'], bsz, seq)
    a, u = _gate_fwd(xc, w['rg_w_a'], w['rg_w_x'], w['rg_b_a'], w['rg_b_x'], w['rg_lambda'])
    hs, hprev = _scan("lru_fwd", a, u, bsz, seq, reverse=False, with_prev=True)
    rnn, = _rowwise("rnn_out_fwd", lambda s, y: (_f_rnn_out(s, y),), [hs, (z, OFF_RY, D_MODEL)], [],
                    [(D_MODEL, F32)], [])
    ya = _mm_nn("branch_att", att, w['w_branch_att'])
    yb = _mm_nn("branch_rnn", rnn, w['w_branch_rnn'])
    merged, = _rowwise("merge_fwd", lambda *v: (_f_merge(*v),),
                       [(z, OFF_GA, D_MODEL), (z, OFF_GB, D_MODEL), ya, yb], [w['b_merge0'], w['b_merge1']],
                       [(D_MODEL, F32)], [])
    mix = _mm_nn("mix_out", merged, w['w_out'])
    h1, = _rowwise("ln_mix_fwd", lambda *v: (_f_resid_ln(*v),), [h, mix], [w['ln_mix_g'], w['ln_mix_b']],
                   [(D_MODEL, F32)], [])
    tm = _tile(t, 512)
    hgu = _mm("ffn_in", h1, w['w_ffn_in'], grid=(t // tm, 2 * N_FF, 1),
              a_spec=pl.BlockSpec((tm, D_MODEL), lambda i, s, k: (i, 0)),
              b_spec=pl.BlockSpec((None, D_MODEL, FF_SH), lambda i, s, k: (s, 0, 0)),
              o_spec=pl.BlockSpec((None, tm, FF_SH), lambda i, s, k: (s, i, 0)),
              out_shape=jax.ShapeDtypeStruct((2 * N_FF, t, FF_SH), F32), contract=NN)
    act = _act_fwd(hgu)
    ffn = _mm("ffn_out", act, w['w_ffn_out'], grid=(t // tm, 1, N_FF),
              a_spec=pl.BlockSpec((None, tm, FF_SH), lambda i, j, s: (s, i, 0)),
              b_spec=pl.BlockSpec((None, FF_SH, D_MODEL), lambda i, j, s: (s, 0, 0)),
              o_spec=pl.BlockSpec((tm, D_MODEL), lambda i, j, s: (i, 0)),
              out_shape=jax.ShapeDtypeStruct((t, D_MODEL), F32), contract=NN)
    h2, = _rowwise("ln_ffn_fwd", lambda *v: (_f_resid_ln(*v),), [h1, ffn], [w['ln_ffn_g'], w['ln_ffn_b']],
                   [(D_MODEL, F32)], [])
    gp = _mm_nn("ple_gate", h2, w['w_ple_gate'])
    pe = _mm_nn("ple_proj", p_l, w['w_ple'])
    h3, = _rowwise("ln_ple_fwd", lambda *v: (_f_ple(*v),), [h2, gp, pe],
                   [w['b_ple_gate'], w['ln_ple_g'], w['ln_ple_b']], [(D_MODEL, F32)], [])
    saved = dict(h=h, z=z, fl=fl, cq=cq, ck=ck, att=att, lse=lse, xc=xc, a=a, hprev=hprev, hs=hs, rnn=rnn,
                 ya=ya, yb=yb, merged=merged, mix=mix, h1=h1, hgu=hgu, act=act, ffn=ffn, h2=h2, gp=gp, pe=pe)
    return h3, saved


def _layer_bwd(dh3, p_l, w, s, bsz, seq):
    t = bsz * seq
    g = {}
    dh2, dgp, dpe, g['b_ple_gate'], g['ln_ple_g'], g['ln_ple_b'] = _vjp_rowwise(
        "ln_ple_bwd", _f_ple, [s['h2'], s['gp'], s['pe']], [w['b_ple_gate'], w['ln_ple_g'], w['ln_ple_b']], [dh3], 3)
    g['w_ple_gate'] = _mm_tn("ple_gate_dw", s['h2'], dgp, out_dtype=BF16)
    g['w_ple'] = _mm_tn("ple_proj_dw", p_l, dpe, out_dtype=BF16)
    dh2b = _mm_nt("ple_gate_dx", dgp, w['w_ple_gate'])
    dh1, dffn, g['ln_ffn_g'], g['ln_ffn_b'] = _ln_resid_bwd(
        "ln_ffn_bwd", s['h1'], s['ffn'], w['ln_ffn_g'], w['ln_ffn_b'], dh2, dh2b)
    tm = _tile(t, 512)
    dact = _mm("ffn_out_dx", dffn, w['w_ffn_out'], grid=(t // tm, N_FF, 1),
               a_spec=pl.BlockSpec((tm, D_MODEL), lambda i, ss, k: (i, 0)),
               b_spec=pl.BlockSpec((None, FF_SH, D_MODEL), lambda i, ss, k: (ss, 0, 0)),
               o_spec=pl.BlockSpec((None, tm, FF_SH), lambda i, ss, k: (ss, i, 0)),
               out_shape=jax.ShapeDtypeStruct((N_FF, t, FF_SH), F32), contract=NT)
    tk = _tile(t, 512)
    g['w_ffn_out'] = _mm("ffn_out_dw", s['act'], dffn, grid=(N_FF, 1, t // tk),
                         a_spec=pl.BlockSpec((None, tk, FF_SH), lambda ss, j, k: (ss, k, 0)),
                         b_spec=pl.BlockSpec((tk, D_MODEL), lambda ss, j, k: (k, 0)),
                         o_spec=pl.BlockSpec((None, FF_SH, D_MODEL), lambda ss, j, k: (ss, 0, 0)),
                         out_shape=jax.ShapeDtypeStruct((N_FF, FF_SH, D_MODEL), BF16), contract=TN)
    dhgu = _act_bwd(s['hgu'], dact)
    g['w_ffn_in'] = _mm("ffn_in_dw", s['h1'], dhgu, grid=(2 * N_FF, 1, t // tk),
                        a_spec=pl.BlockSpec((tk, D_MODEL), lambda ss, j, k: (k, 0)),
                        b_spec=pl.BlockSpec((None, tk, FF_SH), lambda ss, j, k: (ss, k, 0)),
                        o_spec=pl.BlockSpec((None, D_MODEL, FF_SH), lambda ss, j, k: (ss, 0, 0)),
                        out_shape=jax.ShapeDtypeStruct((2 * N_FF, D_MODEL, FF_SH), BF16), contract=TN)
    dh1b = _mm("ffn_in_dx", dhgu, w['w_ffn_in'], grid=(t // tm, 1, 2 * N_FF),
               a_spec=pl.BlockSpec((None, tm, FF_SH), lambda i, j, ss: (ss, i, 0)),
               b_spec=pl.BlockSpec((None, D_MODEL, FF_SH), lambda i, j, ss: (ss, 0, 0)),
               o_spec=pl.BlockSpec((tm, D_MODEL), lambda i, j, ss: (i, 0)),
               out_shape=jax.ShapeDtypeStruct((t, D_MODEL), F32), contract=NT)
    dh, dmix, g['ln_mix_g'], g['ln_mix_b'] = _ln_resid_bwd(
        "ln_mix_bwd", s['h'], s['mix'], w['ln_mix_g'], w['ln_mix_b'], dh1, dh1b)
    g['w_out'] = _mm_tn("mix_out_dw", s['merged'], dmix, out_dtype=BF16)
    dmerged = _mm_nt("mix_out_dx", dmix, w['w_out'])
    z = s['z']
    dga, dgb, dya, dyb, dbm0, dbm1 = _vjp_rowwise(
        "merge_bwd", _f_merge, [(z, OFF_GA, D_MODEL), (z, OFF_GB, D_MODEL), s['ya'], s['yb']],
        [w['b_merge0'], w['b_merge1']], [dmerged], 4)
    g['b_merge'] = jnp.concatenate([dbm0, dbm1], axis=0)
    g['w_branch_att'] = _mm_tn("branch_att_dw", s['att'], dya, out_dtype=BF16)
    g['w_branch_rnn'] = _mm_tn("branch_rnn_dw", s['rnn'], dyb, out_dtype=BF16)
    datt = _mm_nt("branch_att_dx", dya, w['w_branch_att'])
    drnn = _mm_nt("branch_rnn_dx", dyb, w['w_branch_rnn'])
    dhs, dry = _vjp_rowwise("rnn_out_bwd", _f_rnn_out, [s['hs'], (z, OFF_RY, D_MODEL)], [], [drnn], 2)
    lam = _scan("lru_bwd", s['a'], dhs, bsz, seq, reverse=True)
    da, = _rowwise("lru_da", lambda l, hp: (l * hp,), [lam, s['hprev']], [], [(D_MODEL, F32)], [])
    dxc, g['rg_w_a'], g['rg_w_x'], g['rg_b_a'], g['rg_b_x'], g['rg_lambda'] = _gate_bwd(
        s['xc'], w['rg_w_a'], w['rg_w_x'], w['rg_b_a'], w['rg_b_x'], w['rg_lambda'], da, lam)
    drx, g['conv_w'], g['conv_b'] = _conv_bwd(z, dxc, w['conv_w'], bsz, seq)
    dq, dk, dv, dcq, dck = _attn_bwd(z, s['att'], datt, s['lse'], s['cq'], s['ck'], bsz, seq)
    dc = (dcq[:, :, :, 0] + dck.reshape(bsz, N_HEADS, seq)).transpose(0, 2, 1)
    dc = jnp.pad(dc, ((0, 0), (0, 0), (0, BLK - N_HEADS))).reshape(t, BLK)
    dlogf = _scan("cumsum_bwd", None, dc, bsz, seq, reverse=True)
    dfl, g['b_forget'] = _vjp_rowwise("logf_bwd", _f_logf, [s['fl']], [w['b_forget']], [dlogf], 1)
    dz = jnp.concatenate([dq, dk, dv, drx, dry, dga, dgb], axis=1)
    g['w_in7'] = _mm_tn("z_proj_dw", s['h'], dz, out_dtype=BF16)
    g['w_inf'] = _mm_tn("f_proj_dw", s['h'], dfl, out_dtype=BF16)
    dhz = _mm_nt("z_proj_dx", dz, w['w_in7'])
    dhf = _mm_nt("f_proj_dx", dfl, w['w_inf'])
    dh_in, = _rowwise("dh_sum", lambda x0, x1, x2: (x0 + x1 + x2,), [dh, dhz, dhf], [], [(D_MODEL, F32)], [])
    return dh_in, g


def _ln_resid_bwd(name, h, branch, gam, bet, d0, d1):
    def bwd(hv, bv, d0v, d1v, gv, btv):
        _, pull = jax.vjp(_f_resid_ln, hv, bv, gv, btv)
        dh, db, dg, dbt = pull(d0v + d1v)
        return dh, db, _colsum(dg), _colsum(dbt)

    return _rowwise(name, bwd, [h, branch, d0, d1], [gam, bet], [(D_MODEL, F32), (D_MODEL, F32)],
                    [D_MODEL, D_MODEL])


def _local_step(x2, tgt, p3, layer_w, g_in, b_in, bsz, seq):
    depth = len(layer_w)
    h, = _rowwise("ln_in_fwd", lambda xv, gv, bv: (_ln(xv, gv, bv),), [x2], [g_in, b_in], [(D_MODEL, F32)], [])
    saved = []
    for l in range(depth):
        h, s = _layer_fwd(h, p3[l], layer_w[l], bsz, seq)
        saved.append(s)

    def loss_fn(y, tv):
        err = y - tv
        return err * (1.0 / D_MODEL), _colsum(jnp.square(err))

    dh, sq = _rowwise("loss", loss_fn, [h, tgt], [], [(D_MODEL, F32)], [D_MODEL])
    grads = [None] * depth
    for l in reversed(range(depth)):
        dh, grads[l] = _layer_bwd(dh, p3[l], layer_w[l], saved[l], bsz, seq)
    dx, dg_in, db_in = _vjp_rowwise("ln_in_bwd", _ln, [x2], [g_in, b_in], [dh], 1)
    return sq, dx, grads, dg_in, db_in


def _layer_weights(full, l):
    w = {}
    wt = full['w_in'][:, l].transpose(1, 0, 2).reshape(D_MODEL, N_IN)
    w['w_in7'] = jnp.concatenate([wt[:, :3 * D_MODEL], wt[:, 3 * D_MODEL + N_HEADS:]], axis=1)
    w['w_inf'] = jnp.pad(wt[:, 3 * D_MODEL:3 * D_MODEL + N_HEADS], ((0, 0), (0, BLK - N_HEADS)))
    for n in ['w_branch_att', 'w_branch_rnn', 'w_out', 'w_ple_gate']:
        w[n] = full[n][:, l].reshape(D_MODEL, D_MODEL)
    w['w_ffn_in'] = full['w_ffn_in'][:, l]
    w['w_ffn_out'] = full['w_ffn_out'][:, l].reshape(N_FF, FF_SH, D_MODEL)
    w['w_ple'] = full['w_ple'][:, l].transpose(1, 0, 2).reshape(D_PLE, D_MODEL)
    w['conv_w'] = full['conv_w'][:, l].transpose(1, 0, 2).reshape(CONV_W, D_MODEL)
    bm = full['b_merge'][:, l].transpose(1, 0, 2).reshape(2, D_MODEL)
    w['b_merge0'], w['b_merge1'] = bm[0:1], bm[1:2]
    return w


def _by_destination(name, gw):
    if name == 'w_in':
        g7, gf = gw['w_in7'], gw['w_inf']
        true = jnp.concatenate([g7[:, :3 * D_MODEL], gf[:, :N_HEADS], g7[:, 3 * D_MODEL:]], axis=1)
        return true.reshape(D_MODEL, N_DEV, IN_SH).transpose(1, 0, 2)
    g = gw[name]
    if name in ('w_branch_att', 'w_branch_rnn', 'w_out', 'w_ple_gate'):
        return g.reshape(N_DEV, D_MODEL // N_DEV, D_MODEL)
    if name == 'w_ffn_in':
        return g
    if name == 'w_ffn_out':
        return g.reshape(N_DEV, N_FF * FF_SH // N_DEV, D_MODEL)
    return g.reshape(g.shape[0], N_DEV, BLK).transpose(1, 0, 2)


def kernel(x, p, ln_in_g, ln_in_b, w_in, b_forget, conv_w, conv_b, rg_w_a, rg_b_a, rg_w_x, rg_b_x, rg_lambda, w_branch_att, w_branch_rnn, b_merge, w_out, ln_mix_g, ln_mix_b, w_ffn_in, w_ffn_out, ln_ffn_g, ln_ffn_b, w_ple, w_ple_gate, b_ple_gate, ln_ple_g, ln_ple_b, loss_target, m_ln_in_g, m_ln_in_b, m_w_in, m_b_forget, m_conv_w, m_conv_b, m_rg_w_a, m_rg_b_a, m_rg_w_x, m_rg_b_x, m_rg_lambda, m_w_branch_att, m_w_branch_rnn, m_b_merge, m_w_out, m_ln_mix_g, m_ln_mix_b, m_w_ffn_in, m_w_ffn_out, m_ln_ffn_g, m_ln_ffn_b, m_w_ple, m_w_ple_gate, m_b_ple_gate, m_ln_ple_g, m_ln_ple_b, v_ln_in_g, v_ln_in_b, v_w_in, v_b_forget, v_conv_w, v_conv_b, v_rg_w_a, v_rg_b_a, v_rg_w_x, v_rg_b_x, v_rg_lambda, v_w_branch_att, v_w_branch_rnn, v_b_merge, v_w_out, v_ln_mix_g, v_ln_mix_b, v_w_ffn_in, v_w_ffn_out, v_ln_ffn_g, v_ln_ffn_b, v_w_ple, v_w_ple_gate, v_b_ple_gate, v_ln_ple_g, v_ln_ple_b):
    env = dict(locals())
    wts = {n: env[n] for n in WEIGHTS}
    mom = {n: env['m_' + n] for n in WEIGHTS}
    var = {n: env['v_' + n] for n in WEIGHTS}
    bsz, seq, _ = x.shape
    depth = w_in.shape[0]
    t = bsz * seq
    x2, tgt = x.reshape(t, D_MODEL), loss_target.reshape(t, D_MODEL)
    p3 = p.reshape(depth, t, D_PLE)

    shard_names = SHARDED_BF16 + SHARDED_F32
    shards = [wts[n].astype(BF16) for n in SHARDED_BF16] + [wts[n] for n in SHARDED_F32]
    full = dict(zip(shard_names, _exchange("gather_weights", shards, gather=True)))
    layer_w = []
    for l in range(depth):
        w = _layer_weights(full, l)
        for n in ['conv_b', 'rg_b_a', 'rg_b_x', 'rg_lambda', 'ln_mix_g', 'ln_mix_b', 'ln_ffn_g', 'ln_ffn_b',
                  'b_ple_gate', 'ln_ple_g', 'ln_ple_b']:
            w[n] = _row(wts[n][l])
        w['b_forget'] = jnp.pad(_row(b_forget[l]), ((0, 0), (0, BLK - N_HEADS)))
        w['rg_w_a'], w['rg_w_x'] = rg_w_a[l], rg_w_x[l]
        layer_w.append(w)

    g_in, b_in = _row(ln_in_g), _row(ln_in_b)
    sq, dx, grads, dg_in, db_in = _local_step(x2, tgt, p3, layer_w, g_in, b_in, bsz, seq)
    loss = lax.psum(0.5 * jnp.sum(sq) / D_MODEL, ("x", "y", "c"))
    grad_x = dx.reshape(bsz, seq, D_MODEL)

    parts = [jnp.stack([_by_destination(n, grads[l]) for l in range(depth)], axis=1) for n in shard_names]
    recv = dict(zip(shard_names, _exchange("scatter_grads", parts, gather=False)))
    out = {}
    for n in shard_names:
        shp = wts[n].shape
        flat = lambda a: a.reshape(-1, shp[-1])
        res = _adamw("adamw_" + n, recv[n].reshape(N_DEV, -1, shp[-1]), flat(wts[n]), flat(mom[n]), flat(var[n]))
        out[n] = [r.reshape(shp) for r in res]

    def rep_grad(n):
        if n == 'ln_in_g':
            return dg_in.reshape(-1)
        if n == 'ln_in_b':
            return db_in.reshape(-1)
        return jnp.stack([grads[l][n].reshape(wts[n].shape[1:]) if n != 'b_forget'
                          else grads[l][n][0, :N_HEADS] for l in range(depth)]).reshape(-1)

    sizes = [int(wts[n].size) for n in REPLICATED]
    total = sum(sizes)
    unit = N_DEV * 8 * BLK
    padded = -(-total // unit) * unit
    pack = lambda vals: jnp.pad(jnp.concatenate([v.reshape(-1) for v in vals]), (0, padded - total))
    rows = padded // N_DEV // BLK
    gp, = _exchange("scatter_small", [pack([rep_grad(n) for n in REPLICATED]).reshape(N_DEV, rows, BLK)],
                    gather=False)
    g_slice = _sum_parts("sum_small", gp)
    g_all, = _exchange("gather_small", [g_slice], gather=True)
    one = lambda a: a.reshape(1, padded // BLK, BLK)
    res = _adamw("adamw_small", one(g_all), *[pack([d[n] for n in REPLICATED]).reshape(padded // BLK, BLK)
                                             for d in (wts, mom, var)])
    offs = [sum(sizes[:i]) for i in range(len(sizes))]
    for n, o, sz in zip(REPLICATED, offs, sizes):
        out[n] = [r.reshape(-1)[o:o + sz].reshape(wts[n].shape) for r in res]

    return (loss, grad_x, *[out[n][k] for k in range(4) for n in WEIGHTS])
```

```python
import functools
import math

import jax
import jax.numpy as jnp
from jax import lax
from jax.experimental import pallas as pl
from jax.experimental.pallas import tpu as pltpu

F32 = jnp.float32
BF16 = jnp.bfloat16

N_DEV = 8
D_MODEL = 1024
N_HEADS = 8
HEAD_DIM = 128
N_BLK = 8
BLK = 128
CONV_W = 4
D_PLE = 256
FF_SH = 704
N_FF = 4
IN_SH = 897
N_IN = 7176
DEPTH = 4
RG_C = 8.0
ALPHA = float((2 * DEPTH) ** 0.25)
LN_EPS = 1e-5
SCALE = 1.0 / math.sqrt(HEAD_DIM)
NEG = -1e30
ADAM_LR, ADAM_B1, ADAM_B2, ADAM_EPS, ADAM_WD, ADAM_STEP = 0.001, 0.9, 0.999, 1e-08, 0.01, 10
OFF_Q, OFF_K, OFF_V, OFF_RX, OFF_RY, OFF_GA, OFF_GB = (i * D_MODEL for i in range(7))
Z7 = 7 * D_MODEL
V7X_VMEM_LIMIT = 48 * 1024 * 1024

WEIGHTS = ['ln_in_g', 'ln_in_b', 'w_in', 'b_forget', 'conv_w', 'conv_b', 'rg_w_a', 'rg_b_a', 'rg_w_x', 'rg_b_x',
           'rg_lambda', 'w_branch_att', 'w_branch_rnn', 'b_merge', 'w_out', 'ln_mix_g', 'ln_mix_b', 'w_ffn_in',
           'w_ffn_out', 'ln_ffn_g', 'ln_ffn_b', 'w_ple', 'w_ple_gate', 'b_ple_gate', 'ln_ple_g', 'ln_ple_b']
SHARDED_BF16 = ['w_in', 'w_branch_att', 'w_branch_rnn', 'w_out', 'w_ffn_in', 'w_ffn_out', 'w_ple', 'w_ple_gate']
SHARDED_F32 = ['conv_w', 'b_merge']
REPLICATED = [n for n in WEIGHTS if n not in SHARDED_BF16 and n not in SHARDED_F32]

NN = ((1,), (0,))
NT = ((1,), (1,))
TN = ((0,), (0,))


class _Ride:
    def __init__(self, arrays, *, gather, index=None):
        self.arrays, self.gather, self.index = list(arrays), gather, index
        self.result = None

    def out_shapes(self):
        if not self.gather:
            return [jax.ShapeDtypeStruct(a.shape, a.dtype) for a in self.arrays]
        cut = 0 if self.index is None else 1
        return [jax.ShapeDtypeStruct((N_DEV,) + a.shape[cut:], a.dtype) for a in self.arrays]

    def scratch(self):
        n = len(self.arrays)
        return [pltpu.SemaphoreType.DMA((n * N_DEV,)), pltpu.SemaphoreType.DMA((n * N_DEV,)),
                pltpu.SemaphoreType.DMA((n,))]

    def copies(self, ins, outs, sems):
        send_sems, recv_sems, local_sems = sems
        me = _my_id()
        res = []
        for a in range(len(ins)):
            if self.gather:
                src_of = lambda d, a=a: ins[a] if self.index is None else ins[a].at[self.index]
            else:
                src_of = lambda d, a=a: ins[a].at[d]
            res.append(pltpu.make_async_copy(src_of(me), outs[a].at[me], local_sems.at[a]))
            for k in range(1, N_DEV):
                res.append(pltpu.make_async_remote_copy(
                    src_ref=src_of(me ^ k), dst_ref=outs[a].at[me],
                    send_sem=send_sems.at[a * N_DEV + k], recv_sem=recv_sems.at[a * N_DEV + k],
                    device_id=_peer(k), device_id_type=pl.DeviceIdType.MESH))
        return res


def _pcall(body, ride=None, **kw):
    if ride is None:
        return pl.pallas_call(body, **kw)
    n = len(ride.arrays)
    grid = kw['grid']
    single = not isinstance(kw['out_shape'], (list, tuple))
    out_specs = [kw['out_specs']] if single else list(kw['out_specs'])
    out_shape = [kw['out_shape']] if single else list(kw['out_shape'])
    in_specs = list(kw['in_specs'])
    scratch = list(kw.get('scratch_shapes', ()))
    n_in, n_out, n_sc = len(in_specs), len(out_shape), len(scratch)
    hbm = pl.BlockSpec(memory_space=pltpu.HBM)

    def wrapped(*refs):
        ins, xin = refs[:n_in], refs[n_in:n_in + n]
        outs, xout = refs[n_in + n:n_in + n + n_out], refs[n_in + n + n_out:n_in + 2 * n + n_out]
        sc, sems = refs[n_in + 2 * n + n_out:n_in + 2 * n + n_out + n_sc], refs[-3:]
        ids = [pl.program_id(ax) for ax in range(len(grid))]
        first = functools.reduce(jnp.logical_and, [i == 0 for i in ids])
        last = functools.reduce(jnp.logical_and, [i == g - 1 for i, g in zip(ids, grid)])

        @pl.when(first)
        def _():
            for cp in ride.copies(xin, xout, sems):
                cp.start()

        body(*ins, *outs, *sc)

        @pl.when(last)
        def _():
            for cp in ride.copies(xin, xout, sems):
                cp.wait()

    call = pl.pallas_call(wrapped, name=kw['name'], grid=grid, in_specs=in_specs + [hbm] * n,
                          out_specs=out_specs + [hbm] * n, out_shape=out_shape + ride.out_shapes(),
                          scratch_shapes=scratch + ride.scratch(), compiler_params=kw['compiler_params'])

    def run(*args):
        res = call(*args, *ride.arrays)
        ride.result = list(res[n_out:])
        return res[0] if single else list(res[:n_out])

    return run


def _tile(n, pref, mult=8):
    if n <= pref:
        return n
    t = (pref // mult) * mult
    while t >= mult:
        if n % t == 0:
            return t
        t -= mult
    return n


def _cparams(sem):
    return pltpu.CompilerParams(dimension_semantics=sem, vmem_limit_bytes=V7X_VMEM_LIMIT)


def _mm(name, a, b, *, grid, a_spec, b_spec, o_spec, out_shape, contract, ride=None):
    nk = grid[-1]
    acc_shape = tuple(d for d in o_spec.block_shape if d is not None)

    def body(a_ref, b_ref, o_ref, acc_ref):
        k = pl.program_id(len(grid) - 1)
        part = lax.dot_general(a_ref[...].astype(BF16), b_ref[...].astype(BF16), (contract, ((), ())),
                               preferred_element_type=F32)

        @pl.when(k == 0)
        def _():
            acc_ref[...] = part

        @pl.when(k > 0)
        def _():
            acc_ref[...] += part

        @pl.when(k == nk - 1)
        def _():
            o_ref[...] = acc_ref[...].astype(o_ref.dtype)

    sem = ("parallel",) * (len(grid) - 1) + ("arbitrary",)
    return _pcall(body, ride=ride, name=name, grid=grid, in_specs=[a_spec, b_spec], out_specs=o_spec,
                  out_shape=out_shape, scratch_shapes=[pltpu.VMEM(acc_shape, F32)],
                  compiler_params=_cparams(sem))(a, b)


def _mm_nn(name, a, b, *, a_off=0, out_dtype=F32, tm=512, tn=1024, tk=1024, ride=None):
    m = a.shape[0]
    k, n = b.shape
    tm, tn, tk = _tile(m, tm), _tile(n, tn, 128), _tile(k, tk, 128)
    ko = a_off // tk
    return _mm(name, a, b, grid=(m // tm, n // tn, k // tk),
               a_spec=pl.BlockSpec((tm, tk), lambda i, j, kk: (i, kk + ko)),
               b_spec=pl.BlockSpec((tk, tn), lambda i, j, kk: (kk, j)),
               o_spec=pl.BlockSpec((tm, tn), lambda i, j, kk: (i, j)),
               out_shape=jax.ShapeDtypeStruct((m, n), out_dtype), contract=NN, ride=ride)


def _mm_nt(name, a, b, *, out_dtype=F32, tm=512, tn=1024, tk=1024, ride=None):
    m, k = a.shape
    n = b.shape[0]
    tm, tn, tk = _tile(m, tm), _tile(n, tn, 128), _tile(k, tk, 128)
    return _mm(name, a, b, grid=(m // tm, n // tn, k // tk),
               a_spec=pl.BlockSpec((tm, tk), lambda i, j, kk: (i, kk)),
               b_spec=pl.BlockSpec((tn, tk), lambda i, j, kk: (j, kk)),
               o_spec=pl.BlockSpec((tm, tn), lambda i, j, kk: (i, j)),
               out_shape=jax.ShapeDtypeStruct((m, n), out_dtype), contract=NT, ride=ride)


def _mm_tn(name, a, b, *, a_off=0, m=None, out_dtype=F32, tm=512, tn=1024, tk=512, ride=None):
    t, n = b.shape
    m = a.shape[1] if m is None else m
    tm, tn, tk = _tile(m, tm, 128), _tile(n, tn, 128), _tile(t, tk)
    mo = a_off // tm
    return _mm(name, a, b, grid=(m // tm, n // tn, t // tk),
               a_spec=pl.BlockSpec((tk, tm), lambda i, j, kk: (kk, i + mo)),
               b_spec=pl.BlockSpec((tk, tn), lambda i, j, kk: (kk, j)),
               o_spec=pl.BlockSpec((tm, tn), lambda i, j, kk: (i, j)),
               out_shape=jax.ShapeDtypeStruct((m, n), out_dtype), contract=TN, ride=ride)


def _rowwise(name, fn, rows, params, out_rows, out_reds, tm=256):
    rows = [r if isinstance(r, tuple) else (r, 0, r.shape[1]) for r in rows]
    t = rows[0][0].shape[0]
    tm = _tile(t, tm)
    in_specs = []
    for _, off, w in rows:
        in_specs.append(pl.BlockSpec((tm, w), functools.partial(lambda i, cb: (i, cb), cb=off // w)))
    for p in params:
        in_specs.append(pl.BlockSpec((1, p.shape[1]), lambda i: (0, 0)))
    out_specs = [pl.BlockSpec((tm, w), lambda i: (i, 0)) for w, _ in out_rows]
    out_specs += [pl.BlockSpec((1, w), lambda i: (0, 0)) for w in out_reds]
    out_shape = [jax.ShapeDtypeStruct((t, w), dt) for w, dt in out_rows]
    out_shape += [jax.ShapeDtypeStruct((1, w), F32) for w in out_reds]
    nr, npar, nor = len(rows), len(params), len(out_rows)

    def body(*refs):
        ins, outs = refs[:nr + npar], refs[nr + npar:]
        vals = [r[...].astype(F32) for r in ins[:nr]]
        vals += [jnp.broadcast_to(r[...], (tm, r.shape[1])) for r in ins[nr:]]
        res = fn(*vals)
        step = pl.program_id(0)
        for o, v in zip(outs[:nor], res[:nor]):
            o[...] = v.astype(o.dtype)
        for o, v in zip(outs[nor:], res[nor:]):
            _accumulate(o, v, step)

    res = _pcall(body, name=name, grid=(t // tm,), in_specs=in_specs, out_specs=out_specs, out_shape=out_shape,
                 compiler_params=_cparams(("arbitrary",)))(*[r[0] for r in rows], *params)
    return res


def _accumulate(o_ref, v, step):
    @pl.when(step == 0)
    def _():
        o_ref[...] = v

    @pl.when(step > 0)
    def _():
        o_ref[...] += v


def _colsum(v):
    return jnp.sum(v, axis=0, keepdims=True)


def _vjp_rowwise(name, fn, rows, params, cots, n_row_grads, tm=256):
    nr, npar, nc = len(rows), len(params), len(cots)

    def bwd(*vals):
        prim, par, ct = vals[:nr], vals[nr + nc:], vals[nr:nr + nc]
        _, pull = jax.vjp(fn, *prim, *par)
        grads = pull(tuple(ct) if nc > 1 else ct[0])
        return tuple(grads[:n_row_grads]) + tuple(_colsum(g) for g in grads[nr:])

    widths = [(r[2] if isinstance(r, tuple) else r.shape[1], F32) for r in rows[:n_row_grads]]
    return _rowwise(name, bwd, list(rows) + list(cots), params, widths, [p.shape[1] for p in params], tm=tm)


def _ln(s, g, b):
    mu = jnp.mean(s, axis=-1, keepdims=True)
    var = jnp.mean(jnp.square(s - mu), axis=-1, keepdims=True)
    return (s - mu) * lax.rsqrt(var + LN_EPS) * g + b


def _softplus(x):
    return jnp.maximum(x, 0.0) + jnp.log1p(jnp.exp(-jnp.abs(x)))


def _expm1(x):
    series = x * (1.0 + x * (1.0 / 2 + x * (1.0 / 6 + x * (1.0 / 24 + x * (1.0 / 120 + x * (1.0 / 720))))))
    return jnp.where(jnp.abs(x) < 0.25, series, jnp.exp(x) - 1.0)


def _f_resid_ln(h, branch, g, b):
    return _ln(ALPHA * h + branch, g, b)


def _f_ple(h, gp, pe, bpg, g, b):
    return _ln(ALPHA * h + jax.nn.sigmoid(gp + bpg) * pe, g, b)


def _f_merge(ga, gb, ya, yb, bm0, bm1):
    return jax.nn.sigmoid(ga + bm0) * ya + jax.nn.sigmoid(gb + bm1) * yb


def _f_rnn_out(hs, ry):
    return hs * jax.nn.gelu(ry, approximate=True)


def _f_logf(fl, bf):
    return -_softplus(-(fl + bf))


def _f_gate(xc, ra, ia, lam, ba, bx):
    r = jax.nn.sigmoid(ra + ba)
    i = jax.nn.sigmoid(ia + bx)
    log_a = -RG_C * _softplus(-lam) * r
    a = jnp.exp(log_a)
    mult = jnp.sqrt(-_expm1(2.0 * log_a))
    return a, mult * (i * xc)


def _f_act(hg, hu):
    return jax.nn.silu(hg) * hu


ATT_BLOCK = 512


def _scores(q, k, cq, ck, diagonal):
    s = lax.dot_general(q, k, (NT, ((), ())), preferred_element_type=F32) * SCALE
    s = s + cq - ck
    if diagonal:
        row = lax.broadcasted_iota(jnp.int32, s.shape, 0)
        col = lax.broadcasted_iota(jnp.int32, s.shape, 1)
        s = jnp.where(col <= row, s, NEG)
    return s


def _dscores(p, do, o, v):
    dob = do.astype(BF16)
    delta = jnp.sum(dob.astype(F32) * o, axis=1, keepdims=True)
    dp = lax.dot_general(dob, v.astype(BF16), (NT, ((), ())), preferred_element_type=F32)
    return p * (dp - delta)


def _attn_fwd(z, cq, ck, bsz, seq, ride=None):
    t = bsz * seq
    tq = _tile(seq, ATT_BLOCK)
    nq = seq // tq

    def body(q_ref, k_ref, v_ref, cq_ref, ck_ref, o_ref, lse_ref):
        for i in range(nq):
            rows = slice(i * tq, (i + 1) * tq)
            q = q_ref[rows, :].astype(BF16)
            cqi = cq_ref[rows, :]

            def step(j, carry, diagonal, q=q, cqi=cqi):
                m, l, acc = carry
                keys = pl.ds(pl.multiple_of(j * tq, tq), tq)
                s = _scores(q, k_ref[keys, :].astype(BF16), cqi, ck_ref[pl.ds(j, 1), :], diagonal)
                m_new = jnp.maximum(m, jnp.max(s, axis=1, keepdims=True))
                alpha = jnp.exp(m - m_new)
                p = jnp.exp(s - m_new)
                p_hi = p.astype(BF16)
                p_lo = (p - p_hi.astype(F32)).astype(BF16)
                vb = v_ref[keys, :].astype(BF16)
                pv = lax.dot_general(p_hi, vb, (NN, ((), ())), preferred_element_type=F32)
                pv = pv + lax.dot_general(p_lo, vb, (NN, ((), ())), preferred_element_type=F32)
                return m_new, alpha * l + jnp.sum(p, axis=1, keepdims=True), alpha * acc + pv

            carry = (jnp.full((tq, 1), NEG, F32), jnp.zeros((tq, 1), F32), jnp.zeros((tq, HEAD_DIM), F32))
            if i > 0:
                carry = lax.fori_loop(0, i, functools.partial(step, diagonal=False), carry)
            m, l, acc = step(i, carry, True)
            o_ref[rows, :] = acc / l
            lse_ref[rows, :] = m + jnp.log(l)

    head = (seq, HEAD_DIM)
    in_specs = [
        pl.BlockSpec(head, lambda b, h: (b, h)),
        pl.BlockSpec(head, lambda b, h: (b, N_HEADS + h)),
        pl.BlockSpec(head, lambda b, h: (b, 2 * N_HEADS + h)),
        pl.BlockSpec((None, None, seq, 1), lambda b, h: (b, h, 0, 0)),
        pl.BlockSpec((None, None, nq, tq), lambda b, h: (b, h, 0, 0)),
    ]
    out_specs = [pl.BlockSpec(head, lambda b, h: (b, h)),
                 pl.BlockSpec((None, None, seq, 1), lambda b, h: (b, h, 0, 0))]
    out_shape = [jax.ShapeDtypeStruct((t, D_MODEL), F32), jax.ShapeDtypeStruct((bsz, N_HEADS, seq, 1), F32)]
    return _pcall(body, ride=ride, name="attn_fwd", grid=(bsz, N_HEADS), in_specs=in_specs, out_specs=out_specs,
                  out_shape=out_shape, compiler_params=_cparams(("parallel", "parallel")))(
                      z, z, z, cq, ck.reshape(bsz, N_HEADS, nq, tq))


def _attn_bwd(z, att, datt, lse, cq, ck, bsz, seq, ride=None):
    t = bsz * seq
    tq = _tile(seq, ATT_BLOCK)
    nq = seq // tq

    def body(q_ref, k_ref, v_ref, o_ref, do_ref, lse_ref, cq_ref, ck_ref,
             dq_ref, dk_ref, dv_ref, dcq_ref, dck_ref):
        dq_ref[...] = jnp.zeros_like(dq_ref)
        dcq_ref[...] = jnp.zeros_like(dcq_ref)
        for j in range(nq):
            keys = slice(j * tq, (j + 1) * tq)
            kb = k_ref[keys, :].astype(BF16)
            vb = v_ref[keys, :].astype(BF16)
            ckj = ck_ref[j:j + 1, :]

            def step(i, carry, diagonal, kb=kb, vb=vb, ckj=ckj):
                dk, dv, dc = carry
                rows = pl.ds(pl.multiple_of(i * tq, tq), tq)
                qb = q_ref[rows, :].astype(BF16)
                do = do_ref[rows, :]
                s = _scores(qb, kb, cq_ref[rows, :], ckj, diagonal)
                p = jnp.exp(s - lse_ref[rows, :])
                ds = _dscores(p, do, o_ref[rows, :], vb)
                dsb = (ds * SCALE).astype(BF16)
                dq_ref[rows, :] += lax.dot_general(dsb, kb, (NN, ((), ())), preferred_element_type=F32)
                dcq_ref[rows, :] += jnp.sum(ds, axis=1, keepdims=True)
                dv = dv + lax.dot_general(p.astype(BF16), do.astype(BF16), (TN, ((), ())),
                                          preferred_element_type=F32)
                dk = dk + lax.dot_general(dsb, qb, (TN, ((), ())), preferred_element_type=F32)
                return dk, dv, dc - jnp.sum(ds, axis=0, keepdims=True)

            zero = jnp.zeros((tq, HEAD_DIM), F32)
            carry = step(j, (zero, zero, jnp.zeros((1, tq), F32)), True)
            if j + 1 < nq:
                carry = lax.fori_loop(j + 1, nq, functools.partial(step, diagonal=False), carry)
            dk_ref[keys, :], dv_ref[keys, :], dck_ref[j:j + 1, :] = carry

    head = (seq, HEAD_DIM)
    hmap = lambda b, h: (b, h)
    col = pl.BlockSpec((None, None, seq, 1), lambda b, h: (b, h, 0, 0))
    row = pl.BlockSpec((None, None, nq, tq), lambda b, h: (b, h, 0, 0))
    in_specs = [pl.BlockSpec(head, hmap),
                pl.BlockSpec(head, lambda b, h: (b, N_HEADS + h)),
                pl.BlockSpec(head, lambda b, h: (b, 2 * N_HEADS + h)),
                pl.BlockSpec(head, hmap), pl.BlockSpec(head, hmap), col, col, row]
    big = jax.ShapeDtypeStruct((t, D_MODEL), F32)
    return _pcall(body, ride=ride, name="attn_bwd", grid=(bsz, N_HEADS), in_specs=in_specs,
                  out_specs=[pl.BlockSpec(head, hmap)] * 3 + [col, row],
                  out_shape=[big, big, big, jax.ShapeDtypeStruct((bsz, N_HEADS, seq, 1), F32),
                             jax.ShapeDtypeStruct((bsz, N_HEADS, nq, tq), F32)],
                  compiler_params=_cparams(("parallel", "parallel")))(
                      z, z, z, att, datt, lse, cq, ck.reshape(bsz, N_HEADS, nq, tq))


def _scan(name, a, u, bsz, seq, *, reverse, with_prev=False, tb=256):
    c = u.shape[1]
    tb = _tile(seq, tb)
    nb = seq // tb
    has_a = a is not None

    def body(*refs):
        if has_a:
            a_ref, u_ref = refs[0], refs[1]
            rest = refs[2:]
        else:
            u_ref = refs[0]
            rest = refs[1:]
        outs = rest[:2] if with_prev else rest[:1]
        carry_sc, afirst_sc = rest[-2], rest[-1]
        step = pl.program_id(1)

        @pl.when(step == 0)
        def _():
            carry_sc[...] = jnp.zeros_like(carry_sc)
            afirst_sc[...] = jnp.zeros_like(afirst_sc)

        row = lax.broadcasted_iota(jnp.int32, (tb, c), 0)
        uu = u_ref[...]
        if has_a:
            aa = a_ref[...]
            if reverse:
                coef = jnp.where(row < tb - 1, pltpu.roll(aa, tb - 1, 0), afirst_sc[...])
            else:
                coef = aa
        k = 1
        while k < tb:
            shift = tb - k if reverse else k
            keep = (row < tb - k) if reverse else (row >= k)
            uu_sh = jnp.where(keep, pltpu.roll(uu, shift, 0), 0.0)
            if has_a:
                uu = coef * uu_sh + uu
                coef = coef * jnp.where(keep, pltpu.roll(coef, shift, 0), 1.0)
            else:
                uu = uu + uu_sh
            k *= 2
        carry = carry_sc[...]
        h = uu + coef * carry if has_a else uu + carry
        outs[0][...] = h
        if with_prev:
            outs[1][...] = jnp.where(row >= 1, pltpu.roll(h, 1, 0), carry)
        if reverse:
            carry_sc[...] = outs[0][0:1, :]
            if has_a:
                afirst_sc[...] = a_ref[0:1, :]
        else:
            carry_sc[...] = outs[0][tb - 1:tb, :]

    if reverse:
        imap = lambda b, s: (b * nb + nb - 1 - s, 0)
    else:
        imap = lambda b, s: (b * nb + s, 0)
    spec = pl.BlockSpec((tb, c), imap)
    n_in = 2 if has_a else 1
    n_out = 2 if with_prev else 1
    res = _pcall(body, name=name, grid=(bsz, nb), in_specs=[spec] * n_in, out_specs=[spec] * n_out,
                 out_shape=[jax.ShapeDtypeStruct(u.shape, F32)] * n_out,
                 scratch_shapes=[pltpu.VMEM((1, c), F32), pltpu.VMEM((1, c), F32)],
                 compiler_params=_cparams(("parallel", "arbitrary")))(*([a, u] if has_a else [u]))
    return res if with_prev else res[0]


def _conv_fwd(z, w, b, bsz, seq, tb=256):
    c = D_MODEL
    t = bsz * seq
    tb = _tile(seq, tb)
    nb = seq // tb

    def body(x_ref, w_ref, b_ref, o_ref, tail_sc):
        step = pl.program_id(1)

        @pl.when(step == 0)
        def _():
            tail_sc[...] = jnp.zeros_like(tail_sc)

        x = x_ref[...]
        row8 = lax.broadcasted_iota(jnp.int32, (8, c), 0)
        tail = tail_sc[...]
        acc = w_ref[CONV_W - 1:CONV_W, :] * x + b_ref[...]
        for sh in range(1, CONV_W):
            xs = pltpu.roll(x, sh, 0)
            top = jnp.where(row8 < sh, pltpu.roll(tail, sh, 0), xs[0:8, :])
            xs = jnp.concatenate([top, xs[8:, :]], axis=0) if tb > 8 else top
            acc = acc + w_ref[CONV_W - 1 - sh:CONV_W - sh, :] * xs
        o_ref[...] = acc
        tail_sc[...] = x_ref[tb - 8:tb, :]

    return _pcall(body, name="conv_fwd", grid=(bsz, nb),
                  in_specs=[pl.BlockSpec((tb, c), lambda bb, s: (bb * nb + s, OFF_RX // c)),
                            pl.BlockSpec((CONV_W, c), lambda bb, s: (0, 0)),
                            pl.BlockSpec((1, c), lambda bb, s: (0, 0))],
                  out_specs=pl.BlockSpec((tb, c), lambda bb, s: (bb * nb + s, 0)),
                  out_shape=jax.ShapeDtypeStruct((t, c), F32),
                  scratch_shapes=[pltpu.VMEM((8, c), F32)],
                  compiler_params=_cparams(("parallel", "arbitrary")))(z, w, b)


def _conv_bwd(z, dxc, w, bsz, seq, tb=256):
    c = D_MODEL
    t = bsz * seq
    tb = _tile(seq, tb)
    nb = seq // tb

    def body(x_ref, g_ref, w_ref, dx_ref, dw_ref, db_ref, head_sc):
        bb, step = pl.program_id(0), pl.program_id(1)

        @pl.when(step == 0)
        def _():
            head_sc[...] = jnp.zeros_like(head_sc)

        x, g = x_ref[...], g_ref[...]
        row8 = lax.broadcasted_iota(jnp.int32, (8, c), 0)
        head = head_sc[...]
        dx = w_ref[CONV_W - 1:CONV_W, :] * g
        dws = [None] * CONV_W
        dws[CONV_W - 1] = _colsum(g * x)
        for sh in range(1, CONV_W):
            gs = pltpu.roll(g, tb - sh, 0)
            bot = jnp.where(row8 >= 8 - sh, pltpu.roll(head, 8 - sh, 0), gs[tb - 8:tb, :])
            gs = jnp.concatenate([gs[:tb - 8, :], bot], axis=0) if tb > 8 else bot
            dx = dx + w_ref[CONV_W - 1 - sh:CONV_W - sh, :] * gs
            dws[CONV_W - 1 - sh] = _colsum(gs * x)
        dx_ref[...] = dx
        first = (bb == 0) & (step == 0)
        dw = jnp.concatenate(dws, axis=0)
        db = _colsum(g)

        @pl.when(first)
        def _():
            dw_ref[...] = dw
            db_ref[...] = db

        @pl.when(jnp.logical_not(first))
        def _():
            dw_ref[...] += dw
            db_ref[...] += db

        head_sc[...] = g_ref[0:8, :]

    rmap = lambda bb, s: (bb * nb + nb - 1 - s, 0)
    return _pcall(body, name="conv_bwd", grid=(bsz, nb),
                  in_specs=[pl.BlockSpec((tb, c), lambda bb, s: (bb * nb + nb - 1 - s, OFF_RX // c)),
                            pl.BlockSpec((tb, c), rmap),
                            pl.BlockSpec((CONV_W, c), lambda bb, s: (0, 0))],
                  out_specs=[pl.BlockSpec((tb, c), rmap),
                             pl.BlockSpec((CONV_W, c), lambda bb, s: (0, 0)),
                             pl.BlockSpec((1, c), lambda bb, s: (0, 0))],
                  out_shape=[jax.ShapeDtypeStruct((t, c), F32), jax.ShapeDtypeStruct((CONV_W, c), F32),
                             jax.ShapeDtypeStruct((1, c), F32)],
                  scratch_shapes=[pltpu.VMEM((8, c), F32)],
                  compiler_params=_cparams(("arbitrary", "arbitrary")))(z, dxc, w)


def _gate_fwd(xc, w_a, w_x, b_a, b_x, lam, tm=512):
    t = xc.shape[0]
    tm = _tile(t, tm)

    def body(xc_ref, wa_ref, wx_ref, ba_ref, bx_ref, lam_ref, a_ref, u_ref):
        xc_b = xc_ref[...]
        xb = xc_b.astype(BF16)
        ra = lax.dot_general(xb, wa_ref[...].astype(BF16), (NN, ((), ())), preferred_element_type=F32)
        ia = lax.dot_general(xb, wx_ref[...].astype(BF16), (NN, ((), ())), preferred_element_type=F32)
        a, u = _f_gate(xc_b, ra, ia, lam_ref[...], ba_ref[...], bx_ref[...])
        a_ref[...] = a
        u_ref[...] = u

    row = pl.BlockSpec((tm, BLK), lambda n, i: (i, n))
    wsp = pl.BlockSpec((None, BLK, BLK), lambda n, i: (n, 0, 0))
    vec = pl.BlockSpec((1, BLK), lambda n, i: (0, n))
    return _pcall(body, name="gate_fwd", grid=(N_BLK, t // tm), in_specs=[row, wsp, wsp, vec, vec, vec],
                  out_specs=[row, row], out_shape=[jax.ShapeDtypeStruct((t, D_MODEL), F32)] * 2,
                  compiler_params=_cparams(("parallel", "parallel")))(xc, w_a, w_x, b_a, b_x, lam)


def _gate_bwd(xc, w_a, w_x, b_a, b_x, lam, da, du, tm=512):
    t = xc.shape[0]
    tm = _tile(t, tm)

    def body(xc_ref, wa_ref, wx_ref, ba_ref, bx_ref, lam_ref, da_ref, du_ref,
             dxc_ref, dwa_ref, dwx_ref, dba_ref, dbx_ref, dlam_ref):
        step = pl.program_id(1)
        xc_b = xc_ref[...]
        xb = xc_b.astype(BF16)
        wa, wx = wa_ref[...].astype(BF16), wx_ref[...].astype(BF16)
        ra = lax.dot_general(xb, wa, (NN, ((), ())), preferred_element_type=F32)
        ia = lax.dot_general(xb, wx, (NN, ((), ())), preferred_element_type=F32)
        full = lambda r: jnp.broadcast_to(r[...], (tm, BLK))
        _, pull = jax.vjp(_f_gate, xc_b, ra, ia, full(lam_ref), full(ba_ref), full(bx_ref))
        dxc, dra, dia, dlam, dba, dbx = pull((da_ref[...], du_ref[...]))
        drb, dib = dra.astype(BF16), dia.astype(BF16)
        dxc = dxc + lax.dot_general(drb, wa, (NT, ((), ())), preferred_element_type=F32)
        dxc = dxc + lax.dot_general(dib, wx, (NT, ((), ())), preferred_element_type=F32)
        dxc_ref[...] = dxc
        _accumulate(dwa_ref, lax.dot_general(xb, drb, (TN, ((), ())), preferred_element_type=F32), step)
        _accumulate(dwx_ref, lax.dot_general(xb, dib, (TN, ((), ())), preferred_element_type=F32), step)
        _accumulate(dba_ref, _colsum(dba), step)
        _accumulate(dbx_ref, _colsum(dbx), step)
        _accumulate(dlam_ref, _colsum(dlam), step)

    row = pl.BlockSpec((tm, BLK), lambda n, i: (i, n))
    wsp = pl.BlockSpec((None, BLK, BLK), lambda n, i: (n, 0, 0))
    vec = pl.BlockSpec((1, BLK), lambda n, i: (0, n))
    wshape = jax.ShapeDtypeStruct((N_BLK, BLK, BLK), F32)
    vshape = jax.ShapeDtypeStruct((1, D_MODEL), F32)
    return _pcall(body, name="gate_bwd", grid=(N_BLK, t // tm),
                  in_specs=[row, wsp, wsp, vec, vec, vec, row, row],
                  out_specs=[row, wsp, wsp, vec, vec, vec],
                  out_shape=[jax.ShapeDtypeStruct((t, D_MODEL), F32), wshape, wshape, vshape, vshape, vshape],
                  compiler_params=_cparams(("parallel", "arbitrary")))(xc, w_a, w_x, b_a, b_x, lam, da, du)


def _act_fwd(hgu, tm=512):
    _, t, w = hgu.shape
    tm = _tile(t, tm)

    def body(hg_ref, hu_ref, o_ref):
        o_ref[...] = _f_act(hg_ref[...], hu_ref[...])

    spec = lambda off: pl.BlockSpec((None, tm, w), lambda s, i: (s + off, i, 0))
    return _pcall(body, name="act_fwd", grid=(N_FF, t // tm), in_specs=[spec(0), spec(N_FF)], out_specs=spec(0),
                  out_shape=jax.ShapeDtypeStruct((N_FF, t, w), F32),
                  compiler_params=_cparams(("parallel", "parallel")))(hgu, hgu)


def _act_bwd(hgu, dact, tm=512):
    _, t, w = hgu.shape
    tm = _tile(t, tm)

    def body(hg_ref, hu_ref, d_ref, o_ref):
        half = pl.program_id(0)
        _, pull = jax.vjp(_f_act, hg_ref[...], hu_ref[...])
        dhg, dhu = pull(d_ref[...])
        o_ref[...] = jnp.where(half == 0, dhg, dhu)

    spec = lambda off: pl.BlockSpec((None, tm, w), lambda hf, s, i: (s + off, i, 0))
    return _pcall(body, name="act_bwd", grid=(2, N_FF, t // tm), in_specs=[spec(0), spec(N_FF), spec(0)],
                  out_specs=pl.BlockSpec((None, tm, w), lambda hf, s, i: (hf * N_FF + s, i, 0)),
                  out_shape=jax.ShapeDtypeStruct((2 * N_FF, t, w), F32),
                  compiler_params=_cparams(("parallel", "parallel", "parallel")))(hgu, hgu, dact)


def _adamw(name, parts, w, m, v, tr=128):
    ng = len(parts)
    n_src, r, c = parts[0].shape
    tr = _tile(r, tr)
    nb = r // tr
    bc1 = 1.0 - ADAM_B1 ** ADAM_STEP
    bc2 = 1.0 - ADAM_B2 ** ADAM_STEP

    def body(*refs):
        p_refs = refs[:ng]
        w_ref, m_ref, v_ref, g_ref, d_ref, nm_ref, nv_ref = refs[ng:]
        grp = pl.program_id(0)

        def update(p_ref):
            g = p_ref[0].astype(F32)
            for s in range(1, n_src):
                g = g + p_ref[s].astype(F32)
            nm = ADAM_B1 * m_ref[...] + (1.0 - ADAM_B1) * g
            nv = ADAM_B2 * v_ref[...] + (1.0 - ADAM_B2) * jnp.square(g)
            g_ref[...] = g
            nm_ref[...] = nm
            nv_ref[...] = nv
            d_ref[...] = -ADAM_LR * ((nm / bc1) / (jnp.sqrt(nv / bc2) + ADAM_EPS) + ADAM_WD * w_ref[...])

        for k in range(ng):
            pl.when(grp == k)(functools.partial(update, p_refs[k]))

    p_specs = [pl.BlockSpec((n_src, tr, c), functools.partial(lambda gi, i, k: (0, jnp.where(gi == k, i, 0), 0), k=k))
               for k in range(ng)]
    spec = pl.BlockSpec((tr, c), lambda gi, i: (gi * nb + i, 0))
    return _pcall(body, name=name, grid=(ng, nb), in_specs=p_specs + [spec, spec, spec],
                  out_specs=[spec] * 4, out_shape=[jax.ShapeDtypeStruct((ng * r, c), F32)] * 4,
                  compiler_params=_cparams(("parallel", "parallel")))(*parts, w, m, v)


def _sum_parts(name, parts, tr=256):
    _, r, c = parts.shape
    tr = _tile(r, tr)

    def body(p_ref, o_ref):
        g = p_ref[0]
        for s in range(1, parts.shape[0]):
            g = g + p_ref[s]
        o_ref[...] = g

    return _pcall(body, name=name, grid=(r // tr,),
                  in_specs=[pl.BlockSpec((parts.shape[0], tr, c), lambda i: (0, i, 0))],
                  out_specs=pl.BlockSpec((tr, c), lambda i: (i, 0)),
                  out_shape=jax.ShapeDtypeStruct((r, c), F32), compiler_params=_cparams(("parallel",)))(parts)


def _peer(k):
    x, y, c = lax.axis_index("x"), lax.axis_index("y"), lax.axis_index("c")
    return (x ^ ((k >> 2) & 1), y ^ ((k >> 1) & 1), c ^ (k & 1))


def _my_id():
    return 4 * lax.axis_index("x") + 2 * lax.axis_index("y") + lax.axis_index("c")


def _exchange(name, ride):
    n = len(ride.arrays)

    def body(*refs):
        copies = ride.copies(refs[:n], refs[n:2 * n], refs[2 * n:])
        for cp in copies:
            cp.start()
        for cp in copies:
            cp.wait()

    hbm = pl.BlockSpec(memory_space=pltpu.HBM)
    return _pcall(body, name=name, in_specs=[hbm] * n, out_specs=[hbm] * n, out_shape=ride.out_shapes(),
                  scratch_shapes=ride.scratch())(*ride.arrays)


def _row(v):
    return v.reshape(1, -1)


def _time_major_heads(c, bsz, seq):
    return c.reshape(bsz, seq, BLK)[:, :, :N_HEADS].transpose(0, 2, 1)


def _no_ride(*_):
    return None


def _layer_fwd(h, p_l, w, bsz, seq, ride_of=_no_ride):
    t = bsz * seq
    z = _mm_nn("z_proj", h, w['w_in7'], ride=ride_of('z_proj'))
    fl = _mm_nn("f_proj", h, w['w_inf'])
    logf, = _rowwise("logf_fwd", lambda f, b: (_f_logf(f, b),), [fl], [w['b_forget']], [(BLK, F32)], [])
    c = _scan("cumsum_fwd", None, logf, bsz, seq, reverse=False)
    ct = _time_major_heads(c, bsz, seq)
    cq, ck = ct[..., None], ct[:, :, None, :]
    att, lse = _attn_fwd(z, cq, ck, bsz, seq, ride=ride_of('attn_fwd'))
    xc = _conv_fwd(z, w['conv_w'], w['conv_b'], bsz, seq)
    a, u = _gate_fwd(xc, w['rg_w_a'], w['rg_w_x'], w['rg_b_a'], w['rg_b_x'], w['rg_lambda'])
    hs, hprev = _scan("lru_fwd", a, u, bsz, seq, reverse=False, with_prev=True)
    rnn, = _rowwise("rnn_out_fwd", lambda s, y: (_f_rnn_out(s, y),), [hs, (z, OFF_RY, D_MODEL)], [],
                    [(D_MODEL, F32)], [])
    ya = _mm_nn("branch_att", att, w['w_branch_att'])
    yb = _mm_nn("branch_rnn", rnn, w['w_branch_rnn'])
    merged, = _rowwise("merge_fwd", lambda *v: (_f_merge(*v),),
                       [(z, OFF_GA, D_MODEL), (z, OFF_GB, D_MODEL), ya, yb], [w['b_merge0'], w['b_merge1']],
                       [(D_MODEL, F32)], [])
    mix = _mm_nn("mix_out", merged, w['w_out'])
    h1, = _rowwise("ln_mix_fwd", lambda *v: (_f_resid_ln(*v),), [h, mix], [w['ln_mix_g'], w['ln_mix_b']],
                   [(D_MODEL, F32)], [])
    tm = _tile(t, 512)
    hgu = _mm("ffn_in", h1, w['w_ffn_in'], grid=(t // tm, 2 * N_FF, 1),
              a_spec=pl.BlockSpec((tm, D_MODEL), lambda i, s, k: (i, 0)),
              b_spec=pl.BlockSpec((None, D_MODEL, FF_SH), lambda i, s, k: (s, 0, 0)),
              o_spec=pl.BlockSpec((None, tm, FF_SH), lambda i, s, k: (s, i, 0)),
              out_shape=jax.ShapeDtypeStruct((2 * N_FF, t, FF_SH), F32), contract=NN, ride=ride_of('ffn_in'))
    act = _act_fwd(hgu)
    ffn = _mm("ffn_out", act, w['w_ffn_out'], grid=(t // tm, 1, N_FF),
              a_spec=pl.BlockSpec((None, tm, FF_SH), lambda i, j, s: (s, i, 0)),
              b_spec=pl.BlockSpec((None, FF_SH, D_MODEL), lambda i, j, s: (s, 0, 0)),
              o_spec=pl.BlockSpec((tm, D_MODEL), lambda i, j, s: (i, 0)),
              out_shape=jax.ShapeDtypeStruct((t, D_MODEL), F32), contract=NN)
    h2, = _rowwise("ln_ffn_fwd", lambda *v: (_f_resid_ln(*v),), [h1, ffn], [w['ln_ffn_g'], w['ln_ffn_b']],
                   [(D_MODEL, F32)], [])
    gp = _mm_nn("ple_gate", h2, w['w_ple_gate'])
    pe = _mm_nn("ple_proj", p_l, w['w_ple'])
    h3, = _rowwise("ln_ple_fwd", lambda *v: (_f_ple(*v),), [h2, gp, pe],
                   [w['b_ple_gate'], w['ln_ple_g'], w['ln_ple_b']], [(D_MODEL, F32)], [])
    saved = dict(h=h, z=z, fl=fl, cq=cq, ck=ck, att=att, lse=lse, xc=xc, a=a, hprev=hprev, hs=hs, rnn=rnn,
                 ya=ya, yb=yb, merged=merged, mix=mix, h1=h1, hgu=hgu, act=act, ffn=ffn, h2=h2, gp=gp, pe=pe)
    return h3, saved


def _layer_bwd(dh3, p_l, w, s, bsz, seq, ride_of=_no_ride):
    t = bsz * seq
    g = {}
    dh2, dgp, dpe, g['b_ple_gate'], g['ln_ple_g'], g['ln_ple_b'] = _vjp_rowwise(
        "ln_ple_bwd", _f_ple, [s['h2'], s['gp'], s['pe']], [w['b_ple_gate'], w['ln_ple_g'], w['ln_ple_b']], [dh3], 3)
    g['w_ple_gate'] = _mm_tn("ple_gate_dw", s['h2'], dgp, out_dtype=BF16)
    g['w_ple'] = _mm_tn("ple_proj_dw", p_l, dpe, out_dtype=BF16)
    dh2b = _mm_nt("ple_gate_dx", dgp, w['w_ple_gate'])
    dh1, dffn, g['ln_ffn_g'], g['ln_ffn_b'] = _ln_resid_bwd(
        "ln_ffn_bwd", s['h1'], s['ffn'], w['ln_ffn_g'], w['ln_ffn_b'], dh2, dh2b)
    tm = _tile(t, 512)
    dact = _mm("ffn_out_dx", dffn, w['w_ffn_out'], grid=(t // tm, N_FF, 1),
               a_spec=pl.BlockSpec((tm, D_MODEL), lambda i, ss, k: (i, 0)),
               b_spec=pl.BlockSpec((None, FF_SH, D_MODEL), lambda i, ss, k: (ss, 0, 0)),
               o_spec=pl.BlockSpec((None, tm, FF_SH), lambda i, ss, k: (ss, i, 0)),
               out_shape=jax.ShapeDtypeStruct((N_FF, t, FF_SH), F32), contract=NT)
    tk = _tile(t, 512)
    g['w_ffn_out'] = _mm("ffn_out_dw", s['act'], dffn, grid=(N_FF, 1, t // tk),
                         a_spec=pl.BlockSpec((None, tk, FF_SH), lambda ss, j, k: (ss, k, 0)),
                         b_spec=pl.BlockSpec((tk, D_MODEL), lambda ss, j, k: (k, 0)),
                         o_spec=pl.BlockSpec((None, FF_SH, D_MODEL), lambda ss, j, k: (ss, 0, 0)),
                         out_shape=jax.ShapeDtypeStruct((N_FF, FF_SH, D_MODEL), BF16), contract=TN)
    dhgu = _act_bwd(s['hgu'], dact)
    g['w_ffn_in'] = _mm("ffn_in_dw", s['h1'], dhgu, grid=(2 * N_FF, 1, t // tk),
                        a_spec=pl.BlockSpec((tk, D_MODEL), lambda ss, j, k: (k, 0)),
                        b_spec=pl.BlockSpec((None, tk, FF_SH), lambda ss, j, k: (ss, k, 0)),
                        o_spec=pl.BlockSpec((None, D_MODEL, FF_SH), lambda ss, j, k: (ss, 0, 0)),
                        out_shape=jax.ShapeDtypeStruct((2 * N_FF, D_MODEL, FF_SH), BF16), contract=TN,
                        ride=ride_of('ffn_in_dw', g))
    dh1b = _mm("ffn_in_dx", dhgu, w['w_ffn_in'], grid=(t // tm, 1, 2 * N_FF),
               a_spec=pl.BlockSpec((None, tm, FF_SH), lambda i, j, ss: (ss, i, 0)),
               b_spec=pl.BlockSpec((None, D_MODEL, FF_SH), lambda i, j, ss: (ss, 0, 0)),
               o_spec=pl.BlockSpec((tm, D_MODEL), lambda i, j, ss: (i, 0)),
               out_shape=jax.ShapeDtypeStruct((t, D_MODEL), F32), contract=NT)
    dh, dmix, g['ln_mix_g'], g['ln_mix_b'] = _ln_resid_bwd(
        "ln_mix_bwd", s['h'], s['mix'], w['ln_mix_g'], w['ln_mix_b'], dh1, dh1b)
    g['w_out'] = _mm_tn("mix_out_dw", s['merged'], dmix, out_dtype=BF16)
    dmerged = _mm_nt("mix_out_dx", dmix, w['w_out'])
    z = s['z']
    dga, dgb, dya, dyb, dbm0, dbm1 = _vjp_rowwise(
        "merge_bwd", _f_merge, [(z, OFF_GA, D_MODEL), (z, OFF_GB, D_MODEL), s['ya'], s['yb']],
        [w['b_merge0'], w['b_merge1']], [dmerged], 4)
    g['b_merge'] = jnp.concatenate([dbm0, dbm1], axis=0)
    g['w_branch_att'] = _mm_tn("branch_att_dw", s['att'], dya, out_dtype=BF16)
    g['w_branch_rnn'] = _mm_tn("branch_rnn_dw", s['rnn'], dyb, out_dtype=BF16)
    datt = _mm_nt("branch_att_dx", dya, w['w_branch_att'])
    drnn = _mm_nt("branch_rnn_dx", dyb, w['w_branch_rnn'])
    dhs, dry = _vjp_rowwise("rnn_out_bwd", _f_rnn_out, [s['hs'], (z, OFF_RY, D_MODEL)], [], [drnn], 2)
    lam = _scan("lru_bwd", s['a'], dhs, bsz, seq, reverse=True)
    da, = _rowwise("lru_da", lambda l, hp: (l * hp,), [lam, s['hprev']], [], [(D_MODEL, F32)], [])
    dxc, g['rg_w_a'], g['rg_w_x'], g['rg_b_a'], g['rg_b_x'], g['rg_lambda'] = _gate_bwd(
        s['xc'], w['rg_w_a'], w['rg_w_x'], w['rg_b_a'], w['rg_b_x'], w['rg_lambda'], da, lam)
    drx, g['conv_w'], g['conv_b'] = _conv_bwd(z, dxc, w['conv_w'], bsz, seq)
    dq, dk, dv, dcq, dck = _attn_bwd(z, s['att'], datt, s['lse'], s['cq'], s['ck'], bsz, seq,
                                     ride=ride_of('attn_bwd', g))
    dc = (dcq[:, :, :, 0] + dck.reshape(bsz, N_HEADS, seq)).transpose(0, 2, 1)
    dc = jnp.pad(dc, ((0, 0), (0, 0), (0, BLK - N_HEADS))).reshape(t, BLK)
    dlogf = _scan("cumsum_bwd", None, dc, bsz, seq, reverse=True)
    dfl, g['b_forget'] = _vjp_rowwise("logf_bwd", _f_logf, [s['fl']], [w['b_forget']], [dlogf], 1)
    dz = jnp.concatenate([dq, dk, dv, drx, dry, dga, dgb], axis=1)
    g['w_in7'] = _mm_tn("z_proj_dw", s['h'], dz, out_dtype=BF16)
    g['w_inf'] = _mm_tn("f_proj_dw", s['h'], dfl, out_dtype=BF16)
    dhz = _mm_nt("z_proj_dx", dz, w['w_in7'], ride=ride_of('z_proj_dx', g))
    dhf = _mm_nt("f_proj_dx", dfl, w['w_inf'])
    dh_in, = _rowwise("dh_sum", lambda x0, x1, x2: (x0 + x1 + x2,), [dh, dhz, dhf], [], [(D_MODEL, F32)], [])
    return dh_in, g


def _ln_resid_bwd(name, h, branch, gam, bet, d0, d1):
    def bwd(hv, bv, d0v, d1v, gv, btv):
        _, pull = jax.vjp(_f_resid_ln, hv, bv, gv, btv)
        dh, db, dg, dbt = pull(d0v + d1v)
        return dh, db, _colsum(dg), _colsum(dbt)

    return _rowwise(name, bwd, [h, branch, d0, d1], [gam, bet], [(D_MODEL, F32), (D_MODEL, F32)],
                    [D_MODEL, D_MODEL])


class _Schedule:
    FWD = {'z_proj': ['w_ffn_out', 'w_branch_att', 'w_branch_rnn', 'w_out', 'w_ple_gate', 'w_ple', 'conv_w',
                      'b_merge'],
           'attn_fwd': ['w_in'], 'ffn_in': ['w_ffn_in']}
    BWD = {'ffn_in_dw': ['w_ffn_out', 'w_ple_gate', 'w_ple'],
           'attn_bwd': ['w_ffn_in', 'w_out', 'w_branch_att', 'w_branch_rnn', 'conv_w', 'b_merge'],
           'z_proj_dx': ['w_in']}

    def __init__(self, shards, depth):
        self.shards, self.depth = shards, depth
        self.gathered = [{} for _ in range(depth)]
        self.received = [{} for _ in range(depth)]
        self.pending = []

    def gather_ride(self, layer, kernel_name):
        if layer + 1 >= self.depth:
            return None
        names = self.FWD[kernel_name]
        ride = _Ride([self.shards[n] for n in names], gather=True, index=layer + 1)
        self.pending.append((ride, names, self.gathered[layer + 1]))
        return ride

    def scatter_ride(self, layer, kernel_name, grads):
        names = self.BWD[kernel_name]
        ride = _Ride([_by_destination(n, grads) for n in names], gather=False)
        self.pending.append((ride, names, self.received[layer]))
        return ride

    def collect(self):
        for ride, names, dst in self.pending:
            dst.update(zip(names, ride.result))
        self.pending = []


def _local_step(x2, tgt, p3, weights_of, depth, g_in, b_in, bsz, seq, sched=None):
    h, = _rowwise("ln_in_fwd", lambda xv, gv, bv: (_ln(xv, gv, bv),), [x2], [g_in, b_in], [(D_MODEL, F32)], [])
    saved, layer_w = [], []
    for l in range(depth):
        layer_w.append(weights_of(l))
        ride_of = functools.partial(sched.gather_ride, l) if sched else _no_ride
        h, s = _layer_fwd(h, p3[l], layer_w[l], bsz, seq, ride_of)
        if sched:
            sched.collect()
        saved.append(s)

    def loss_fn(y, tv):
        err = y - tv
        return err * (1.0 / D_MODEL), _colsum(jnp.square(err))

    dh, sq = _rowwise("loss", loss_fn, [h, tgt], [], [(D_MODEL, F32)], [D_MODEL])
    grads = [None] * depth
    for l in reversed(range(depth)):
        ride_of = functools.partial(sched.scatter_ride, l) if sched else _no_ride
        dh, grads[l] = _layer_bwd(dh, p3[l], layer_w[l], saved[l], bsz, seq, ride_of)
        if sched:
            sched.collect()
    dx, dg_in, db_in = _vjp_rowwise("ln_in_bwd", _ln, [x2], [g_in, b_in], [dh], 1)
    return sq, dx, grads, dg_in, db_in


def _layer_weights(full):
    w = {}
    wt = full['w_in'].transpose(1, 0, 2).reshape(D_MODEL, N_IN)
    w['w_in7'] = jnp.concatenate([wt[:, :3 * D_MODEL], wt[:, 3 * D_MODEL + N_HEADS:]], axis=1)
    w['w_inf'] = jnp.pad(wt[:, 3 * D_MODEL:3 * D_MODEL + N_HEADS], ((0, 0), (0, BLK - N_HEADS)))
    for n in ['w_branch_att', 'w_branch_rnn', 'w_out', 'w_ple_gate']:
        w[n] = full[n].reshape(D_MODEL, D_MODEL)
    w['w_ffn_in'] = full['w_ffn_in']
    w['w_ffn_out'] = full['w_ffn_out'].reshape(N_FF, FF_SH, D_MODEL)
    w['w_ple'] = full['w_ple'].transpose(1, 0, 2).reshape(D_PLE, D_MODEL)
    w['conv_w'] = full['conv_w'].transpose(1, 0, 2).reshape(CONV_W, D_MODEL)
    bm = full['b_merge'].transpose(1, 0, 2).reshape(2, D_MODEL)
    w['b_merge0'], w['b_merge1'] = bm[0:1], bm[1:2]
    return w


def _by_destination(name, gw):
    if name == 'w_in':
        g7, gf = gw['w_in7'], gw['w_inf']
        true = jnp.concatenate([g7[:, :3 * D_MODEL], gf[:, :N_HEADS], g7[:, 3 * D_MODEL:]], axis=1)
        return true.reshape(D_MODEL, N_DEV, IN_SH).transpose(1, 0, 2)
    g = gw[name]
    if name in ('w_branch_att', 'w_branch_rnn', 'w_out', 'w_ple_gate'):
        return g.reshape(N_DEV, D_MODEL // N_DEV, D_MODEL)
    if name == 'w_ffn_in':
        return g
    if name == 'w_ffn_out':
        return g.reshape(N_DEV, N_FF * FF_SH // N_DEV, D_MODEL)
    return g.reshape(g.shape[0], N_DEV, BLK).transpose(1, 0, 2)


def kernel(x, p, ln_in_g, ln_in_b, w_in, b_forget, conv_w, conv_b, rg_w_a, rg_b_a, rg_w_x, rg_b_x, rg_lambda, w_branch_att, w_branch_rnn, b_merge, w_out, ln_mix_g, ln_mix_b, w_ffn_in, w_ffn_out, ln_ffn_g, ln_ffn_b, w_ple, w_ple_gate, b_ple_gate, ln_ple_g, ln_ple_b, loss_target, m_ln_in_g, m_ln_in_b, m_w_in, m_b_forget, m_conv_w, m_conv_b, m_rg_w_a, m_rg_b_a, m_rg_w_x, m_rg_b_x, m_rg_lambda, m_w_branch_att, m_w_branch_rnn, m_b_merge, m_w_out, m_ln_mix_g, m_ln_mix_b, m_w_ffn_in, m_w_ffn_out, m_ln_ffn_g, m_ln_ffn_b, m_w_ple, m_w_ple_gate, m_b_ple_gate, m_ln_ple_g, m_ln_ple_b, v_ln_in_g, v_ln_in_b, v_w_in, v_b_forget, v_conv_w, v_conv_b, v_rg_w_a, v_rg_b_a, v_rg_w_x, v_rg_b_x, v_rg_lambda, v_w_branch_att, v_w_branch_rnn, v_b_merge, v_w_out, v_ln_mix_g, v_ln_mix_b, v_w_ffn_in, v_w_ffn_out, v_ln_ffn_g, v_ln_ffn_b, v_w_ple, v_w_ple_gate, v_b_ple_gate, v_ln_ple_g, v_ln_ple_b):
    env = dict(locals())
    wts = {n: env[n] for n in WEIGHTS}
    mom = {n: env['m_' + n] for n in WEIGHTS}
    var = {n: env['v_' + n] for n in WEIGHTS}
    bsz, seq, _ = x.shape
    depth = w_in.shape[0]
    t = bsz * seq
    x2, tgt = x.reshape(t, D_MODEL), loss_target.reshape(t, D_MODEL)
    p3 = p.reshape(depth, t, D_PLE)

    shard_names = SHARDED_BF16 + SHARDED_F32
    shards = {n: wts[n].astype(BF16) for n in SHARDED_BF16}
    shards.update({n: wts[n] for n in SHARDED_F32})
    sched = _Schedule(shards, depth)
    first = _Ride([shards[n] for n in shard_names], gather=True, index=0)
    sched.gathered[0] = dict(zip(shard_names, _exchange("gather_layer0", first)))

    def weights_of(l):
        w = _layer_weights(sched.gathered[l])
        for n in ['conv_b', 'rg_b_a', 'rg_b_x', 'rg_lambda', 'ln_mix_g', 'ln_mix_b', 'ln_ffn_g', 'ln_ffn_b',
                  'b_ple_gate', 'ln_ple_g', 'ln_ple_b']:
            w[n] = _row(wts[n][l])
        w['b_forget'] = jnp.pad(_row(b_forget[l]), ((0, 0), (0, BLK - N_HEADS)))
        w['rg_w_a'], w['rg_w_x'] = rg_w_a[l], rg_w_x[l]
        return w

    g_in, b_in = _row(ln_in_g), _row(ln_in_b)
    sq, dx, grads, dg_in, db_in = _local_step(x2, tgt, p3, weights_of, depth, g_in, b_in, bsz, seq, sched)
    loss = lax.psum(0.5 * jnp.sum(sq) / D_MODEL, ("x", "y", "c"))
    grad_x = dx.reshape(bsz, seq, D_MODEL)

    out = {}
    for n in shard_names:
        shp = wts[n].shape
        flat = lambda a: a.reshape(-1, shp[-1])
        recv = [sched.received[l][n] for l in range(depth)]
        if n in SHARDED_F32:
            recv = [jnp.stack(recv, axis=1).reshape(N_DEV, -1, shp[-1])]
        res = _adamw("adamw_" + n, recv, flat(wts[n]), flat(mom[n]), flat(var[n]))
        out[n] = [r.reshape(shp) for r in res]

    def rep_grad(n):
        if n == 'ln_in_g':
            return dg_in.reshape(-1)
        if n == 'ln_in_b':
            return db_in.reshape(-1)
        return jnp.stack([grads[l][n].reshape(wts[n].shape[1:]) if n != 'b_forget'
                          else grads[l][n][0, :N_HEADS] for l in range(depth)]).reshape(-1)

    sizes = [int(wts[n].size) for n in REPLICATED]
    total = sum(sizes)
    unit = N_DEV * 8 * BLK
    padded = -(-total // unit) * unit
    pack = lambda vals: jnp.pad(jnp.concatenate([v.reshape(-1) for v in vals]), (0, padded - total))
    rows = padded // N_DEV // BLK
    gp, = _exchange("scatter_small", _Ride([pack([rep_grad(n) for n in REPLICATED]).reshape(N_DEV, rows, BLK)],
                                           gather=False))
    g_slice = _sum_parts("sum_small", gp)
    g_all, = _exchange("gather_small", _Ride([g_slice], gather=True))
    one = lambda a: a.reshape(1, padded // BLK, BLK)
    res = _adamw("adamw_small", [one(g_all)], *[pack([d[n] for n in REPLICATED]).reshape(padded // BLK, BLK)
                                               for d in (wts, mom, var)])
    offs = [sum(sizes[:i]) for i in range(len(sizes))]
    for n, o, sz in zip(REPLICATED, offs, sizes):
        out[n] = [r.reshape(-1)[o:o + sz].reshape(wts[n].shape) for r in res]

    return (loss, grad_x, *[out[n][k] for k in range(4) for n in WEIGHTS])
```

```python
import functools
import math

import jax
import jax.numpy as jnp
from jax import lax
from jax.experimental import pallas as pl
from jax.experimental.pallas import tpu as pltpu

F32 = jnp.float32
BF16 = jnp.bfloat16

N_DEV = 8
D_MODEL = 1024
N_HEADS = 8
HEAD_DIM = 128
N_BLK = 8
BLK = 128
CONV_W = 4
D_PLE = 256
FF_SH = 704
N_FF = 4
IN_SH = 897
N_IN = 7176
DEPTH = 4
RG_C = 8.0
ALPHA = float((2 * DEPTH) ** 0.25)
LN_EPS = 1e-5
SCALE = 1.0 / math.sqrt(HEAD_DIM)
NEG = -1e30
ADAM_LR, ADAM_B1, ADAM_B2, ADAM_EPS, ADAM_WD, ADAM_STEP = 0.001, 0.9, 0.999, 1e-08, 0.01, 10
QKV = 3 * D_MODEL
OFF_RX, OFF_RY, OFF_GA, OFF_GB = (i * D_MODEL for i in range(4))
V7X_VMEM_LIMIT = 48 * 1024 * 1024

WEIGHTS = ['ln_in_g', 'ln_in_b', 'w_in', 'b_forget', 'conv_w', 'conv_b', 'rg_w_a', 'rg_b_a', 'rg_w_x', 'rg_b_x',
           'rg_lambda', 'w_branch_att', 'w_branch_rnn', 'b_merge', 'w_out', 'ln_mix_g', 'ln_mix_b', 'w_ffn_in',
           'w_ffn_out', 'ln_ffn_g', 'ln_ffn_b', 'w_ple', 'w_ple_gate', 'b_ple_gate', 'ln_ple_g', 'ln_ple_b']
SHARDED_BF16 = ['w_in', 'w_branch_att', 'w_branch_rnn', 'w_out', 'w_ffn_in', 'w_ffn_out', 'w_ple', 'w_ple_gate']
SHARDED_F32 = ['conv_w', 'b_merge']
REPLICATED = [n for n in WEIGHTS if n not in SHARDED_BF16 and n not in SHARDED_F32]

NN = ((1,), (0,))
NT = ((1,), (1,))
TN = ((0,), (0,))


class _Ride:
    def __init__(self, arrays, *, gather, index=None):
        self.arrays, self.gather, self.index = list(arrays), gather, index
        self.result = None

    def out_shapes(self):
        if not self.gather:
            return [jax.ShapeDtypeStruct(a.shape, a.dtype) for a in self.arrays]
        cut = 0 if self.index is None else 1
        return [jax.ShapeDtypeStruct((N_DEV,) + a.shape[cut:], a.dtype) for a in self.arrays]

    def scratch(self):
        n = len(self.arrays)
        return [pltpu.SemaphoreType.DMA((n * N_DEV,)), pltpu.SemaphoreType.DMA((n * N_DEV,)),
                pltpu.SemaphoreType.DMA((n,))]

    def copies(self, ins, outs, sems):
        send_sems, recv_sems, local_sems = sems
        me = _my_id()
        res = []
        for a in range(len(ins)):
            if self.gather:
                src_of = lambda d, a=a: ins[a] if self.index is None else ins[a].at[self.index]
            else:
                src_of = lambda d, a=a: ins[a].at[d]
            res.append(pltpu.make_async_copy(src_of(me), outs[a].at[me], local_sems.at[a]))
            for k in range(1, N_DEV):
                res.append(pltpu.make_async_remote_copy(
                    src_ref=src_of(me ^ k), dst_ref=outs[a].at[me],
                    send_sem=send_sems.at[a * N_DEV + k], recv_sem=recv_sems.at[a * N_DEV + k],
                    device_id=_peer(k), device_id_type=pl.DeviceIdType.MESH))
        return res


def _pcall(body, ride=None, **kw):
    if ride is None:
        return pl.pallas_call(body, **kw)
    n = len(ride.arrays)
    grid = kw['grid']
    single = not isinstance(kw['out_shape'], (list, tuple))
    out_specs = [kw['out_specs']] if single else list(kw['out_specs'])
    out_shape = [kw['out_shape']] if single else list(kw['out_shape'])
    in_specs = list(kw['in_specs'])
    scratch = list(kw.get('scratch_shapes', ()))
    n_in, n_out, n_sc = len(in_specs), len(out_shape), len(scratch)
    hbm = pl.BlockSpec(memory_space=pltpu.HBM)

    def wrapped(*refs):
        ins, xin = refs[:n_in], refs[n_in:n_in + n]
        outs, xout = refs[n_in + n:n_in + n + n_out], refs[n_in + n + n_out:n_in + 2 * n + n_out]
        sc, sems = refs[n_in + 2 * n + n_out:n_in + 2 * n + n_out + n_sc], refs[-3:]
        ids = [pl.program_id(ax) for ax in range(len(grid))]
        first = functools.reduce(jnp.logical_and, [i == 0 for i in ids])
        last = functools.reduce(jnp.logical_and, [i == g - 1 for i, g in zip(ids, grid)])

        @pl.when(first)
        def _():
            for cp in ride.copies(xin, xout, sems):
                cp.start()

        body(*ins, *outs, *sc)

        @pl.when(last)
        def _():
            for cp in ride.copies(xin, xout, sems):
                cp.wait()

    call = pl.pallas_call(wrapped, name=kw['name'], grid=grid, in_specs=in_specs + [hbm] * n,
                          out_specs=out_specs + [hbm] * n, out_shape=out_shape + ride.out_shapes(),
                          scratch_shapes=scratch + ride.scratch(), compiler_params=kw['compiler_params'])

    def run(*args):
        res = call(*args, *ride.arrays)
        ride.result = list(res[n_out:])
        return res[0] if single else list(res[:n_out])

    return run


def _tile(n, pref, mult=8):
    if n <= pref:
        return n
    t = (pref // mult) * mult
    while t >= mult:
        if n % t == 0:
            return t
        t -= mult
    return n


def _cparams(sem):
    return pltpu.CompilerParams(dimension_semantics=sem, vmem_limit_bytes=V7X_VMEM_LIMIT)


def _mm(name, a, b, *, grid, a_spec, b_spec, o_spec, out_shape, contract, ride=None):
    nk = grid[-1]
    acc_shape = tuple(d for d in o_spec.block_shape if d is not None)

    def body(a_ref, b_ref, o_ref, acc_ref):
        k = pl.program_id(len(grid) - 1)
        part = lax.dot_general(a_ref[...].astype(BF16), b_ref[...].astype(BF16), (contract, ((), ())),
                               preferred_element_type=F32)

        @pl.when(k == 0)
        def _():
            acc_ref[...] = part

        @pl.when(k > 0)
        def _():
            acc_ref[...] += part

        @pl.when(k == nk - 1)
        def _():
            o_ref[...] = acc_ref[...].astype(o_ref.dtype)

    sem = ("parallel",) * (len(grid) - 1) + ("arbitrary",)
    return _pcall(body, ride=ride, name=name, grid=grid, in_specs=[a_spec, b_spec], out_specs=o_spec,
                  out_shape=out_shape, scratch_shapes=[pltpu.VMEM(acc_shape, F32)],
                  compiler_params=_cparams(sem))(a, b)


def _mm_nn(name, a, b, *, b_off=0, n=None, out_dtype=F32, tm=1024, tn=1024, tk=1024, ride=None):
    m, k = a.shape
    n = b.shape[1] if n is None else n
    tm, tn, tk = _tile(m, tm), _tile(n, tn, 128), _tile(k, tk, 128)
    no = b_off // tn
    return _mm(name, a, b, grid=(m // tm, n // tn, k // tk),
               a_spec=pl.BlockSpec((tm, tk), lambda i, j, kk: (i, kk)),
               b_spec=pl.BlockSpec((tk, tn), lambda i, j, kk: (kk, j + no)),
               o_spec=pl.BlockSpec((tm, tn), lambda i, j, kk: (i, j)),
               out_shape=jax.ShapeDtypeStruct((m, n), out_dtype), contract=NN, ride=ride)


def _mm_nt(name, a, b, *, out_dtype=F32, tm=1024, tn=1024, tk=1024, ride=None):
    m, k = a.shape
    n = b.shape[0]
    tm, tn, tk = _tile(m, tm), _tile(n, tn, 128), _tile(k, tk, 128)
    return _mm(name, a, b, grid=(m // tm, n // tn, k // tk),
               a_spec=pl.BlockSpec((tm, tk), lambda i, j, kk: (i, kk)),
               b_spec=pl.BlockSpec((tn, tk), lambda i, j, kk: (j, kk)),
               o_spec=pl.BlockSpec((tm, tn), lambda i, j, kk: (i, j)),
               out_shape=jax.ShapeDtypeStruct((m, n), out_dtype), contract=NT, ride=ride)


def _mm_tn(name, a, b, *, a_off=0, m=None, out_dtype=F32, tm=1024, tn=1024, tk=2048, ride=None):
    t, n = b.shape
    m = a.shape[1] if m is None else m
    tm, tn, tk = _tile(m, tm, 128), _tile(n, tn, 128), _tile(t, tk)
    mo = a_off // tm
    return _mm(name, a, b, grid=(m // tm, n // tn, t // tk),
               a_spec=pl.BlockSpec((tk, tm), lambda i, j, kk: (kk, i + mo)),
               b_spec=pl.BlockSpec((tk, tn), lambda i, j, kk: (kk, j)),
               o_spec=pl.BlockSpec((tm, tn), lambda i, j, kk: (i, j)),
               out_shape=jax.ShapeDtypeStruct((m, n), out_dtype), contract=TN, ride=ride)


def _rowwise(name, fn, rows, params, out_rows, out_reds, tm=256):
    rows = [r if isinstance(r, tuple) else (r, 0, r.shape[1]) for r in rows]
    t = rows[0][0].shape[0]
    tm = _tile(t, tm)
    in_specs = []
    for _, off, w in rows:
        in_specs.append(pl.BlockSpec((tm, w), functools.partial(lambda i, cb: (i, cb), cb=off // w)))
    for p in params:
        in_specs.append(pl.BlockSpec((1, p.shape[1]), lambda i: (0, 0)))
    out_specs = [pl.BlockSpec((tm, w), lambda i: (i, 0)) for w, _ in out_rows]
    out_specs += [pl.BlockSpec((1, w), lambda i: (0, 0)) for w in out_reds]
    out_shape = [jax.ShapeDtypeStruct((t, w), dt) for w, dt in out_rows]
    out_shape += [jax.ShapeDtypeStruct((1, w), F32) for w in out_reds]
    nr, npar, nor = len(rows), len(params), len(out_rows)

    def body(*refs):
        ins, outs = refs[:nr + npar], refs[nr + npar:]
        vals = [r[...].astype(F32) for r in ins[:nr]]
        vals += [jnp.broadcast_to(r[...], (tm, r.shape[1])) for r in ins[nr:]]
        res = fn(*vals)
        step = pl.program_id(0)
        for o, v in zip(outs[:nor], res[:nor]):
            o[...] = v.astype(o.dtype)
        for o, v in zip(outs[nor:], res[nor:]):
            _accumulate(o, v, step)

    res = _pcall(body, name=name, grid=(t // tm,), in_specs=in_specs, out_specs=out_specs, out_shape=out_shape,
                 compiler_params=_cparams(("arbitrary",)))(*[r[0] for r in rows], *params)
    return res


def _accumulate(o_ref, v, step):
    @pl.when(step == 0)
    def _():
        o_ref[...] = v

    @pl.when(step > 0)
    def _():
        o_ref[...] += v


def _colsum(v):
    return jnp.sum(v, axis=0, keepdims=True)


def _vjp_rowwise(name, fn, rows, params, cots, n_row_grads, tm=256, dtypes=None):
    nr, npar, nc = len(rows), len(params), len(cots)

    def bwd(*vals):
        prim, par, ct = vals[:nr], vals[nr + nc:], vals[nr:nr + nc]
        _, pull = jax.vjp(fn, *prim, *par)
        grads = pull(tuple(ct) if nc > 1 else ct[0])
        return tuple(grads[:n_row_grads]) + tuple(_colsum(g) for g in grads[nr:])

    dtypes = [F32] * n_row_grads if dtypes is None else dtypes
    widths = [(r[2] if isinstance(r, tuple) else r.shape[1], dt) for r, dt in zip(rows[:n_row_grads], dtypes)]
    return _rowwise(name, bwd, list(rows) + list(cots), params, widths, [p.shape[1] for p in params], tm=tm)


def _ln(s, g, b):
    mu = jnp.mean(s, axis=-1, keepdims=True)
    var = jnp.mean(jnp.square(s - mu), axis=-1, keepdims=True)
    return (s - mu) * lax.rsqrt(var + LN_EPS) * g + b


def _softplus(x):
    return jnp.maximum(x, 0.0) + jnp.log1p(jnp.exp(-jnp.abs(x)))


def _expm1(x):
    series = x * (1.0 + x * (1.0 / 2 + x * (1.0 / 6 + x * (1.0 / 24 + x * (1.0 / 120 + x * (1.0 / 720))))))
    return jnp.where(jnp.abs(x) < 0.25, series, jnp.exp(x) - 1.0)


def _f_resid_ln(h, branch, g, b):
    return _ln(ALPHA * h + branch, g, b)


def _f_ple(h, gp, pe, bpg, g, b):
    return _ln(ALPHA * h + jax.nn.sigmoid(gp + bpg) * pe, g, b)


def _f_merge(ga, gb, ya, yb, bm0, bm1):
    return jax.nn.sigmoid(ga + bm0) * ya + jax.nn.sigmoid(gb + bm1) * yb


def _f_rnn_out(hs, ry):
    return hs * jax.nn.gelu(ry, approximate=True)


def _f_logf(fl, bf):
    return -_softplus(-(fl + bf))


def _f_gate(xc, ra, ia, lam, ba, bx):
    r = jax.nn.sigmoid(ra + ba)
    i = jax.nn.sigmoid(ia + bx)
    log_a = -RG_C * _softplus(-lam) * r
    a = jnp.exp(log_a)
    mult = jnp.sqrt(-_expm1(2.0 * log_a))
    return a, mult * (i * xc)


def _f_act(hg, hu):
    return jax.nn.silu(hg) * hu


ATT_BLOCK = 512


def _scores(q, k, cq, ck, diagonal):
    s = lax.dot_general(q, k, (NT, ((), ())), preferred_element_type=F32) * SCALE
    s = s + cq - ck
    if diagonal:
        row = lax.broadcasted_iota(jnp.int32, s.shape, 0)
        col = lax.broadcasted_iota(jnp.int32, s.shape, 1)
        s = jnp.where(col <= row, s, NEG)
    return s


def _dscores(p, do, o, v):
    dob = do.astype(BF16)
    delta = jnp.sum(dob.astype(F32) * o, axis=1, keepdims=True)
    dp = lax.dot_general(dob, v.astype(BF16), (NT, ((), ())), preferred_element_type=F32)
    return p * (dp - delta)


def _attn_fwd(z, cq, ck, bsz, seq, ride=None):
    t = bsz * seq
    tq = _tile(seq, ATT_BLOCK)
    nq = seq // tq

    def body(q_ref, k_ref, v_ref, cq_ref, ck_ref, o_ref, ob_ref, lse_ref):
        for i in range(nq):
            rows = slice(i * tq, (i + 1) * tq)
            q = q_ref[rows, :].astype(BF16)
            cqi = cq_ref[rows, :]

            def step(j, carry, diagonal, q=q, cqi=cqi):
                m, l, acc = carry
                keys = pl.ds(pl.multiple_of(j * tq, tq), tq)
                s = _scores(q, k_ref[keys, :].astype(BF16), cqi, ck_ref[pl.ds(j, 1), :], diagonal)
                m_new = jnp.maximum(m, jnp.max(s, axis=1, keepdims=True))
                alpha = jnp.exp(m - m_new)
                p = jnp.exp(s - m_new)
                p_hi = p.astype(BF16)
                p_lo = (p - p_hi.astype(F32)).astype(BF16)
                vb = v_ref[keys, :].astype(BF16)
                pv = lax.dot_general(p_hi, vb, (NN, ((), ())), preferred_element_type=F32)
                pv = pv + lax.dot_general(p_lo, vb, (NN, ((), ())), preferred_element_type=F32)
                return m_new, alpha * l + jnp.sum(p, axis=1, keepdims=True), alpha * acc + pv

            carry = (jnp.full((tq, 1), NEG, F32), jnp.zeros((tq, 1), F32), jnp.zeros((tq, HEAD_DIM), F32))
            if i > 0:
                carry = lax.fori_loop(0, i, functools.partial(step, diagonal=False), carry)
            m, l, acc = step(i, carry, True)
            o = acc / l
            o_ref[rows, :] = o
            ob_ref[rows, :] = o.astype(BF16)
            lse_ref[rows, :] = m + jnp.log(l)

    head = (seq, HEAD_DIM)
    in_specs = [
        pl.BlockSpec(head, lambda b, h: (b, h)),
        pl.BlockSpec(head, lambda b, h: (b, N_HEADS + h)),
        pl.BlockSpec(head, lambda b, h: (b, 2 * N_HEADS + h)),
        pl.BlockSpec((None, None, seq, 1), lambda b, h: (b, h, 0, 0)),
        pl.BlockSpec((None, None, nq, tq), lambda b, h: (b, h, 0, 0)),
    ]
    out_specs = [pl.BlockSpec(head, lambda b, h: (b, h)), pl.BlockSpec(head, lambda b, h: (b, h)),
                 pl.BlockSpec((None, None, seq, 1), lambda b, h: (b, h, 0, 0))]
    out_shape = [jax.ShapeDtypeStruct((t, D_MODEL), F32), jax.ShapeDtypeStruct((t, D_MODEL), BF16),
                 jax.ShapeDtypeStruct((bsz, N_HEADS, seq, 1), F32)]
    return _pcall(body, ride=ride, name="attn_fwd", grid=(bsz, N_HEADS), in_specs=in_specs, out_specs=out_specs,
                  out_shape=out_shape, compiler_params=_cparams(("parallel", "parallel")))(
                      z, z, z, cq, ck.reshape(bsz, N_HEADS, nq, tq))


def _attn_bwd(z, att, datt, lse, cq, ck, bsz, seq, ride=None):
    t = bsz * seq
    tq = _tile(seq, ATT_BLOCK)
    nq = seq // tq

    def body(q_ref, k_ref, v_ref, o_ref, do_ref, lse_ref, cq_ref, ck_ref,
             dq_ref, dk_ref, dv_ref, dcq_ref, dck_ref, dq_sc):
        dq_sc[...] = jnp.zeros_like(dq_sc)
        dcq_ref[...] = jnp.zeros_like(dcq_ref)
        for j in range(nq):
            keys = slice(j * tq, (j + 1) * tq)
            kb = k_ref[keys, :].astype(BF16)
            vb = v_ref[keys, :].astype(BF16)
            ckj = ck_ref[j:j + 1, :]

            def step(i, carry, diagonal, kb=kb, vb=vb, ckj=ckj):
                dk, dv, dc = carry
                rows = pl.ds(pl.multiple_of(i * tq, tq), tq)
                qb = q_ref[rows, :].astype(BF16)
                do = do_ref[rows, :]
                s = _scores(qb, kb, cq_ref[rows, :], ckj, diagonal)
                p = jnp.exp(s - lse_ref[rows, :])
                ds = _dscores(p, do, o_ref[rows, :], vb)
                dsb = (ds * SCALE).astype(BF16)
                dq_sc[rows, :] += lax.dot_general(dsb, kb, (NN, ((), ())), preferred_element_type=F32)
                dcq_ref[rows, :] += jnp.sum(ds, axis=1, keepdims=True)
                dv = dv + lax.dot_general(p.astype(BF16), do.astype(BF16), (TN, ((), ())),
                                          preferred_element_type=F32)
                dk = dk + lax.dot_general(dsb, qb, (TN, ((), ())), preferred_element_type=F32)
                return dk, dv, dc - jnp.sum(ds, axis=0, keepdims=True)

            zero = jnp.zeros((tq, HEAD_DIM), F32)
            carry = step(j, (zero, zero, jnp.zeros((1, tq), F32)), True)
            if j + 1 < nq:
                carry = lax.fori_loop(j + 1, nq, functools.partial(step, diagonal=False), carry)
            dk, dv, dck_ref[j:j + 1, :] = carry
            dk_ref[keys, :] = dk.astype(BF16)
            dv_ref[keys, :] = dv.astype(BF16)
        dq_ref[...] = dq_sc[...].astype(BF16)

    head = (seq, HEAD_DIM)
    hmap = lambda b, h: (b, h)
    col = pl.BlockSpec((None, None, seq, 1), lambda b, h: (b, h, 0, 0))
    row = pl.BlockSpec((None, None, nq, tq), lambda b, h: (b, h, 0, 0))
    in_specs = [pl.BlockSpec(head, hmap),
                pl.BlockSpec(head, lambda b, h: (b, N_HEADS + h)),
                pl.BlockSpec(head, lambda b, h: (b, 2 * N_HEADS + h)),
                pl.BlockSpec(head, hmap), pl.BlockSpec(head, hmap), col, col, row]
    big = jax.ShapeDtypeStruct((t, D_MODEL), BF16)
    return _pcall(body, ride=ride, name="attn_bwd", grid=(bsz, N_HEADS), in_specs=in_specs,
                  out_specs=[pl.BlockSpec(head, hmap)] * 3 + [col, row],
                  out_shape=[big, big, big, jax.ShapeDtypeStruct((bsz, N_HEADS, seq, 1), F32),
                             jax.ShapeDtypeStruct((bsz, N_HEADS, nq, tq), F32)],
                  scratch_shapes=[pltpu.VMEM(head, F32)],
                  compiler_params=_cparams(("parallel", "parallel")))(
                      z, z, z, att, datt, lse, cq, ck.reshape(bsz, N_HEADS, nq, tq))


def _scan(name, a, u, bsz, seq, *, reverse, with_prev=False, tb=256):
    c = u.shape[1]
    tb = _tile(seq, tb)
    nb = seq // tb
    has_a = a is not None

    def body(*refs):
        if has_a:
            a_ref, u_ref = refs[0], refs[1]
            rest = refs[2:]
        else:
            u_ref = refs[0]
            rest = refs[1:]
        outs = rest[:2] if with_prev else rest[:1]
        carry_sc, afirst_sc = rest[-2], rest[-1]
        step = pl.program_id(1)

        @pl.when(step == 0)
        def _():
            carry_sc[...] = jnp.zeros_like(carry_sc)
            afirst_sc[...] = jnp.zeros_like(afirst_sc)

        row = lax.broadcasted_iota(jnp.int32, (tb, c), 0)
        uu = u_ref[...]
        if has_a:
            aa = a_ref[...]
            if reverse:
                coef = jnp.where(row < tb - 1, pltpu.roll(aa, tb - 1, 0), afirst_sc[...])
            else:
                coef = aa
        k = 1
        while k < tb:
            shift = tb - k if reverse else k
            keep = (row < tb - k) if reverse else (row >= k)
            uu_sh = jnp.where(keep, pltpu.roll(uu, shift, 0), 0.0)
            if has_a:
                uu = coef * uu_sh + uu
                coef = coef * jnp.where(keep, pltpu.roll(coef, shift, 0), 1.0)
            else:
                uu = uu + uu_sh
            k *= 2
        carry = carry_sc[...]
        h = uu + coef * carry if has_a else uu + carry
        outs[0][...] = h
        if with_prev:
            outs[1][...] = jnp.where(row >= 1, pltpu.roll(h, 1, 0), carry)
        if reverse:
            carry_sc[...] = outs[0][0:1, :]
            if has_a:
                afirst_sc[...] = a_ref[0:1, :]
        else:
            carry_sc[...] = outs[0][tb - 1:tb, :]

    if reverse:
        imap = lambda b, s: (b * nb + nb - 1 - s, 0)
    else:
        imap = lambda b, s: (b * nb + s, 0)
    spec = pl.BlockSpec((tb, c), imap)
    n_in = 2 if has_a else 1
    n_out = 2 if with_prev else 1
    res = _pcall(body, name=name, grid=(bsz, nb), in_specs=[spec] * n_in, out_specs=[spec] * n_out,
                 out_shape=[jax.ShapeDtypeStruct(u.shape, F32)] * n_out,
                 scratch_shapes=[pltpu.VMEM((1, c), F32), pltpu.VMEM((1, c), F32)],
                 compiler_params=_cparams(("parallel", "arbitrary")))(*([a, u] if has_a else [u]))
    return res if with_prev else res[0]


def _conv_fwd(z, w, b, bsz, seq, tb=256):
    c = D_MODEL
    t = bsz * seq
    tb = _tile(seq, tb)
    nb = seq // tb

    def body(x_ref, w_ref, b_ref, o_ref, tail_sc):
        step = pl.program_id(1)

        @pl.when(step == 0)
        def _():
            tail_sc[...] = jnp.zeros_like(tail_sc)

        x = x_ref[...]
        row8 = lax.broadcasted_iota(jnp.int32, (8, c), 0)
        tail = tail_sc[...]
        acc = w_ref[CONV_W - 1:CONV_W, :] * x + b_ref[...]
        for sh in range(1, CONV_W):
            xs = pltpu.roll(x, sh, 0)
            top = jnp.where(row8 < sh, pltpu.roll(tail, sh, 0), xs[0:8, :])
            xs = jnp.concatenate([top, xs[8:, :]], axis=0) if tb > 8 else top
            acc = acc + w_ref[CONV_W - 1 - sh:CONV_W - sh, :] * xs
        o_ref[...] = acc
        tail_sc[...] = x_ref[tb - 8:tb, :]

    return _pcall(body, name="conv_fwd", grid=(bsz, nb),
                  in_specs=[pl.BlockSpec((tb, c), lambda bb, s: (bb * nb + s, OFF_RX // c)),
                            pl.BlockSpec((CONV_W, c), lambda bb, s: (0, 0)),
                            pl.BlockSpec((1, c), lambda bb, s: (0, 0))],
                  out_specs=pl.BlockSpec((tb, c), lambda bb, s: (bb * nb + s, 0)),
                  out_shape=jax.ShapeDtypeStruct((t, c), F32),
                  scratch_shapes=[pltpu.VMEM((8, c), F32)],
                  compiler_params=_cparams(("parallel", "arbitrary")))(z, w, b)


def _conv_bwd(z, dxc, w, bsz, seq, tb=256):
    c = D_MODEL
    t = bsz * seq
    tb = _tile(seq, tb)
    nb = seq // tb

    def body(x_ref, g_ref, w_ref, dx_ref, dw_ref, db_ref, head_sc):
        bb, step = pl.program_id(0), pl.program_id(1)

        @pl.when(step == 0)
        def _():
            head_sc[...] = jnp.zeros_like(head_sc)

        x, g = x_ref[...], g_ref[...]
        row8 = lax.broadcasted_iota(jnp.int32, (8, c), 0)
        head = head_sc[...]
        dx = w_ref[CONV_W - 1:CONV_W, :] * g
        dws = [None] * CONV_W
        dws[CONV_W - 1] = _colsum(g * x)
        for sh in range(1, CONV_W):
            gs = pltpu.roll(g, tb - sh, 0)
            bot = jnp.where(row8 >= 8 - sh, pltpu.roll(head, 8 - sh, 0), gs[tb - 8:tb, :])
            gs = jnp.concatenate([gs[:tb - 8, :], bot], axis=0) if tb > 8 else bot
            dx = dx + w_ref[CONV_W - 1 - sh:CONV_W - sh, :] * gs
            dws[CONV_W - 1 - sh] = _colsum(gs * x)
        dx_ref[...] = dx.astype(dx_ref.dtype)
        first = (bb == 0) & (step == 0)
        dw = jnp.concatenate(dws, axis=0)
        db = _colsum(g)

        @pl.when(first)
        def _():
            dw_ref[...] = dw
            db_ref[...] = db

        @pl.when(jnp.logical_not(first))
        def _():
            dw_ref[...] += dw
            db_ref[...] += db

        head_sc[...] = g_ref[0:8, :]

    rmap = lambda bb, s: (bb * nb + nb - 1 - s, 0)
    return _pcall(body, name="conv_bwd", grid=(bsz, nb),
                  in_specs=[pl.BlockSpec((tb, c), lambda bb, s: (bb * nb + nb - 1 - s, OFF_RX // c)),
                            pl.BlockSpec((tb, c), rmap),
                            pl.BlockSpec((CONV_W, c), lambda bb, s: (0, 0))],
                  out_specs=[pl.BlockSpec((tb, c), rmap),
                             pl.BlockSpec((CONV_W, c), lambda bb, s: (0, 0)),
                             pl.BlockSpec((1, c), lambda bb, s: (0, 0))],
                  out_shape=[jax.ShapeDtypeStruct((t, c), BF16), jax.ShapeDtypeStruct((CONV_W, c), F32),
                             jax.ShapeDtypeStruct((1, c), F32)],
                  scratch_shapes=[pltpu.VMEM((8, c), F32)],
                  compiler_params=_cparams(("arbitrary", "arbitrary")))(z, dxc, w)


def _gate_fwd(xc, w_a, w_x, b_a, b_x, lam, tm=512):
    t = xc.shape[0]
    tm = _tile(t, tm)

    def body(xc_ref, wa_ref, wx_ref, ba_ref, bx_ref, lam_ref, a_ref, u_ref):
        xc_b = xc_ref[...]
        xb = xc_b.astype(BF16)
        ra = lax.dot_general(xb, wa_ref[...].astype(BF16), (NN, ((), ())), preferred_element_type=F32)
        ia = lax.dot_general(xb, wx_ref[...].astype(BF16), (NN, ((), ())), preferred_element_type=F32)
        a, u = _f_gate(xc_b, ra, ia, lam_ref[...], ba_ref[...], bx_ref[...])
        a_ref[...] = a
        u_ref[...] = u

    row = pl.BlockSpec((tm, BLK), lambda n, i: (i, n))
    wsp = pl.BlockSpec((None, BLK, BLK), lambda n, i: (n, 0, 0))
    vec = pl.BlockSpec((1, BLK), lambda n, i: (0, n))
    return _pcall(body, name="gate_fwd", grid=(N_BLK, t // tm), in_specs=[row, wsp, wsp, vec, vec, vec],
                  out_specs=[row, row], out_shape=[jax.ShapeDtypeStruct((t, D_MODEL), F32)] * 2,
                  compiler_params=_cparams(("parallel", "parallel")))(xc, w_a, w_x, b_a, b_x, lam)


def _gate_bwd(xc, w_a, w_x, b_a, b_x, lam, da, du, tm=512):
    t = xc.shape[0]
    tm = _tile(t, tm)

    def body(xc_ref, wa_ref, wx_ref, ba_ref, bx_ref, lam_ref, da_ref, du_ref,
             dxc_ref, dwa_ref, dwx_ref, dba_ref, dbx_ref, dlam_ref):
        step = pl.program_id(1)
        xc_b = xc_ref[...]
        xb = xc_b.astype(BF16)
        wa, wx = wa_ref[...].astype(BF16), wx_ref[...].astype(BF16)
        ra = lax.dot_general(xb, wa, (NN, ((), ())), preferred_element_type=F32)
        ia = lax.dot_general(xb, wx, (NN, ((), ())), preferred_element_type=F32)
        full = lambda r: jnp.broadcast_to(r[...], (tm, BLK))
        _, pull = jax.vjp(_f_gate, xc_b, ra, ia, full(lam_ref), full(ba_ref), full(bx_ref))
        dxc, dra, dia, dlam, dba, dbx = pull((da_ref[...], du_ref[...]))
        drb, dib = dra.astype(BF16), dia.astype(BF16)
        dxc = dxc + lax.dot_general(drb, wa, (NT, ((), ())), preferred_element_type=F32)
        dxc = dxc + lax.dot_general(dib, wx, (NT, ((), ())), preferred_element_type=F32)
        dxc_ref[...] = dxc
        _accumulate(dwa_ref, lax.dot_general(xb, drb, (TN, ((), ())), preferred_element_type=F32), step)
        _accumulate(dwx_ref, lax.dot_general(xb, dib, (TN, ((), ())), preferred_element_type=F32), step)
        _accumulate(dba_ref, _colsum(dba), step)
        _accumulate(dbx_ref, _colsum(dbx), step)
        _accumulate(dlam_ref, _colsum(dlam), step)

    row = pl.BlockSpec((tm, BLK), lambda n, i: (i, n))
    wsp = pl.BlockSpec((None, BLK, BLK), lambda n, i: (n, 0, 0))
    vec = pl.BlockSpec((1, BLK), lambda n, i: (0, n))
    wshape = jax.ShapeDtypeStruct((N_BLK, BLK, BLK), F32)
    vshape = jax.ShapeDtypeStruct((1, D_MODEL), F32)
    return _pcall(body, name="gate_bwd", grid=(N_BLK, t // tm),
                  in_specs=[row, wsp, wsp, vec, vec, vec, row, row],
                  out_specs=[row, wsp, wsp, vec, vec, vec],
                  out_shape=[jax.ShapeDtypeStruct((t, D_MODEL), F32), wshape, wshape, vshape, vshape, vshape],
                  compiler_params=_cparams(("parallel", "arbitrary")))(xc, w_a, w_x, b_a, b_x, lam, da, du)


def _act_fwd(hgu, tm=512):
    _, t, w = hgu.shape
    tm = _tile(t, tm)

    def body(hg_ref, hu_ref, o_ref):
        o_ref[...] = _f_act(hg_ref[...], hu_ref[...]).astype(o_ref.dtype)

    spec = lambda off: pl.BlockSpec((None, tm, w), lambda s, i: (s + off, i, 0))
    return _pcall(body, name="act_fwd", grid=(N_FF, t // tm), in_specs=[spec(0), spec(N_FF)], out_specs=spec(0),
                  out_shape=jax.ShapeDtypeStruct((N_FF, t, w), BF16),
                  compiler_params=_cparams(("parallel", "parallel")))(hgu, hgu)


def _act_bwd(hgu, dact, tm=512):
    _, t, w = hgu.shape
    tm = _tile(t, tm)

    def body(hg_ref, hu_ref, d_ref, o_ref):
        _, pull = jax.vjp(_f_act, hg_ref[...], hu_ref[...])
        dhg, dhu = pull(d_ref[...])
        o_ref[0] = dhg.astype(o_ref.dtype)
        o_ref[1] = dhu.astype(o_ref.dtype)

    spec = lambda off: pl.BlockSpec((None, tm, w), lambda s, i: (s + off, i, 0))
    res = _pcall(body, name="act_bwd", grid=(N_FF, t // tm), in_specs=[spec(0), spec(N_FF), spec(0)],
                 out_specs=pl.BlockSpec((2, None, tm, w), lambda s, i: (0, s, i, 0)),
                 out_shape=jax.ShapeDtypeStruct((2, N_FF, t, w), BF16),
                 compiler_params=_cparams(("parallel", "parallel")))(hgu, hgu, dact)
    return res.reshape(2 * N_FF, t, w)


def _adamw(name, parts, w, m, v, tr=128):
    ng = len(parts)
    n_src, r, c = parts[0].shape
    tr = _tile(r, tr)
    nb = r // tr
    bc1 = 1.0 - ADAM_B1 ** ADAM_STEP
    bc2 = 1.0 - ADAM_B2 ** ADAM_STEP

    def body(*refs):
        p_refs = refs[:ng]
        w_ref, m_ref, v_ref, g_ref, d_ref, nm_ref, nv_ref = refs[ng:]
        grp = pl.program_id(0)

        def update(p_ref):
            g = p_ref[0].astype(F32)
            for s in range(1, n_src):
                g = g + p_ref[s].astype(F32)
            nm = ADAM_B1 * m_ref[...] + (1.0 - ADAM_B1) * g
            nv = ADAM_B2 * v_ref[...] + (1.0 - ADAM_B2) * jnp.square(g)
            g_ref[...] = g
            nm_ref[...] = nm
            nv_ref[...] = nv
            d_ref[...] = -ADAM_LR * ((nm / bc1) / (jnp.sqrt(nv / bc2) + ADAM_EPS) + ADAM_WD * w_ref[...])

        for k in range(ng):
            pl.when(grp == k)(functools.partial(update, p_refs[k]))

    p_specs = [pl.BlockSpec((n_src, tr, c), functools.partial(lambda gi, i, k: (0, jnp.where(gi == k, i, 0), 0), k=k))
               for k in range(ng)]
    spec = pl.BlockSpec((tr, c), lambda gi, i: (gi * nb + i, 0))
    return _pcall(body, name=name, grid=(ng, nb), in_specs=p_specs + [spec, spec, spec],
                  out_specs=[spec] * 4, out_shape=[jax.ShapeDtypeStruct((ng * r, c), F32)] * 4,
                  compiler_params=_cparams(("parallel", "parallel")))(*parts, w, m, v)


def _sum_parts(name, parts, tr=256):
    _, r, c = parts.shape
    tr = _tile(r, tr)

    def body(p_ref, o_ref):
        g = p_ref[0]
        for s in range(1, parts.shape[0]):
            g = g + p_ref[s]
        o_ref[...] = g

    return _pcall(body, name=name, grid=(r // tr,),
                  in_specs=[pl.BlockSpec((parts.shape[0], tr, c), lambda i: (0, i, 0))],
                  out_specs=pl.BlockSpec((tr, c), lambda i: (i, 0)),
                  out_shape=jax.ShapeDtypeStruct((r, c), F32), compiler_params=_cparams(("parallel",)))(parts)


def _peer(k):
    x, y, c = lax.axis_index("x"), lax.axis_index("y"), lax.axis_index("c")
    return (x ^ ((k >> 2) & 1), y ^ ((k >> 1) & 1), c ^ (k & 1))


def _my_id():
    return 4 * lax.axis_index("x") + 2 * lax.axis_index("y") + lax.axis_index("c")


def _exchange(name, ride):
    n = len(ride.arrays)

    def body(*refs):
        copies = ride.copies(refs[:n], refs[n:2 * n], refs[2 * n:])
        for cp in copies:
            cp.start()
        for cp in copies:
            cp.wait()

    hbm = pl.BlockSpec(memory_space=pltpu.HBM)
    return _pcall(body, name=name, in_specs=[hbm] * n, out_specs=[hbm] * n, out_shape=ride.out_shapes(),
                  scratch_shapes=ride.scratch())(*ride.arrays)


def _row(v):
    return v.reshape(1, -1)


def _time_major_heads(c, bsz, seq):
    return c.reshape(bsz, seq, BLK)[:, :, :N_HEADS].transpose(0, 2, 1)


def _no_ride(*_):
    return None


TWICE = [(D_MODEL, F32), (D_MODEL, BF16)]


def _both(fn):
    def run(*v):
        y = fn(*v)
        return y, y
    return run


def _layer_fwd(h, hb, p_l, w, bsz, seq, ride_of=_no_ride):
    t = bsz * seq
    zq = _mm_nn("z_proj_qkv", hb, w['w_in7'], n=QKV, out_dtype=BF16, ride=ride_of('z_proj_qkv'))
    zr = _mm_nn("z_proj_rest", hb, w['w_in7'], b_off=QKV, n=4 * D_MODEL, ride=ride_of('z_proj_rest'))
    fl = _mm_nn("f_proj", hb, w['w_inf'])
    logf, = _rowwise("logf_fwd", lambda f, b: (_f_logf(f, b),), [fl], [w['b_forget']], [(BLK, F32)], [])
    c = _scan("cumsum_fwd", None, logf, bsz, seq, reverse=False)
    ct = _time_major_heads(c, bsz, seq)
    cq, ck = ct[..., None], ct[:, :, None, :]
    att, attb, lse = _attn_fwd(zq, cq, ck, bsz, seq, ride=ride_of('attn_fwd'))
    xc = _conv_fwd(zr, w['conv_w'], w['conv_b'], bsz, seq)
    a, u = _gate_fwd(xc, w['rg_w_a'], w['rg_w_x'], w['rg_b_a'], w['rg_b_x'], w['rg_lambda'])
    hs, hprev = _scan("lru_fwd", a, u, bsz, seq, reverse=False, with_prev=True)
    rnn, = _rowwise("rnn_out_fwd", lambda s, y: (_f_rnn_out(s, y),), [hs, (zr, OFF_RY, D_MODEL)], [],
                    [(D_MODEL, BF16)], [])
    ya = _mm_nn("branch_att", attb, w['w_branch_att'])
    yb = _mm_nn("branch_rnn", rnn, w['w_branch_rnn'])
    merged, = _rowwise("merge_fwd", lambda *v: (_f_merge(*v),),
                       [(zr, OFF_GA, D_MODEL), (zr, OFF_GB, D_MODEL), ya, yb], [w['b_merge0'], w['b_merge1']],
                       [(D_MODEL, BF16)], [])
    mix = _mm_nn("mix_out", merged, w['w_out'])
    h1, h1b = _rowwise("ln_mix_fwd", _both(_f_resid_ln), [h, mix], [w['ln_mix_g'], w['ln_mix_b']], TWICE, [])
    tm = _tile(t, 1024)
    hgu = _mm("ffn_in", h1b, w['w_ffn_in'], grid=(t // tm, 2 * N_FF, 1),
              a_spec=pl.BlockSpec((tm, D_MODEL), lambda i, s, k: (i, 0)),
              b_spec=pl.BlockSpec((None, D_MODEL, FF_SH), lambda i, s, k: (s, 0, 0)),
              o_spec=pl.BlockSpec((None, tm, FF_SH), lambda i, s, k: (s, i, 0)),
              out_shape=jax.ShapeDtypeStruct((2 * N_FF, t, FF_SH), F32), contract=NN, ride=ride_of('ffn_in'))
    act = _act_fwd(hgu)
    ffn = _mm("ffn_out", act, w['w_ffn_out'], grid=(t // tm, 1, N_FF),
              a_spec=pl.BlockSpec((None, tm, FF_SH), lambda i, j, s: (s, i, 0)),
              b_spec=pl.BlockSpec((None, FF_SH, D_MODEL), lambda i, j, s: (s, 0, 0)),
              o_spec=pl.BlockSpec((tm, D_MODEL), lambda i, j, s: (i, 0)),
              out_shape=jax.ShapeDtypeStruct((t, D_MODEL), F32), contract=NN, ride=ride_of('ffn_out'))
    h2, h2b = _rowwise("ln_ffn_fwd", _both(_f_resid_ln), [h1, ffn], [w['ln_ffn_g'], w['ln_ffn_b']], TWICE, [])
    gp = _mm_nn("ple_gate", h2b, w['w_ple_gate'])
    pe = _mm_nn("ple_proj", p_l, w['w_ple'])
    h3, h3b = _rowwise("ln_ple_fwd", _both(_f_ple), [h2, gp, pe],
                       [w['b_ple_gate'], w['ln_ple_g'], w['ln_ple_b']], TWICE, [])
    saved = dict(h=h, hb=hb, zq=zq, zr=zr, fl=fl, cq=cq, ck=ck, att=att, attb=attb, lse=lse, xc=xc, a=a,
                 hprev=hprev, hs=hs, rnn=rnn, ya=ya, yb=yb, merged=merged, mix=mix, h1=h1, h1b=h1b, hgu=hgu,
                 act=act, ffn=ffn, h2=h2, h2b=h2b, gp=gp, pe=pe)
    return h3, h3b, saved


def _layer_bwd(dh3, p_l, w, s, bsz, seq, ride_of=_no_ride):
    t = bsz * seq
    g = {}
    dh2, dgp, dpe, g['b_ple_gate'], g['ln_ple_g'], g['ln_ple_b'] = _vjp_rowwise(
        "ln_ple_bwd", _f_ple, [s['h2'], s['gp'], s['pe']], [w['b_ple_gate'], w['ln_ple_g'], w['ln_ple_b']], [dh3], 3,
        dtypes=[F32, BF16, BF16])
    g['w_ple_gate'] = _mm_tn("ple_gate_dw", s['h2b'], dgp, out_dtype=BF16)
    g['w_ple'] = _mm_tn("ple_proj_dw", p_l, dpe, out_dtype=BF16)
    dh2b = _mm_nt("ple_gate_dx", dgp, w['w_ple_gate'])
    dh1, dffn, g['ln_ffn_g'], g['ln_ffn_b'] = _ln_resid_bwd(
        "ln_ffn_bwd", s['h1'], s['ffn'], w['ln_ffn_g'], w['ln_ffn_b'], dh2, dh2b)
    tm = _tile(t, 1024)
    dact = _mm("ffn_out_dx", dffn, w['w_ffn_out'], grid=(t // tm, N_FF, 1),
               a_spec=pl.BlockSpec((tm, D_MODEL), lambda i, ss, k: (i, 0)),
               b_spec=pl.BlockSpec((None, FF_SH, D_MODEL), lambda i, ss, k: (ss, 0, 0)),
               o_spec=pl.BlockSpec((None, tm, FF_SH), lambda i, ss, k: (ss, i, 0)),
               out_shape=jax.ShapeDtypeStruct((N_FF, t, FF_SH), F32), contract=NT)
    tk = _tile(t, 2048)
    g['w_ffn_out'] = _mm("ffn_out_dw", s['act'], dffn, grid=(N_FF, 1, t // tk),
                         a_spec=pl.BlockSpec((None, tk, FF_SH), lambda ss, j, k: (ss, k, 0)),
                         b_spec=pl.BlockSpec((tk, D_MODEL), lambda ss, j, k: (k, 0)),
                         o_spec=pl.BlockSpec((None, FF_SH, D_MODEL), lambda ss, j, k: (ss, 0, 0)),
                         out_shape=jax.ShapeDtypeStruct((N_FF, FF_SH, D_MODEL), BF16), contract=TN)
    dhgu = _act_bwd(s['hgu'], dact)
    g['w_ffn_in'] = _mm("ffn_in_dw", s['h1b'], dhgu, grid=(2 * N_FF, 1, t // tk),
                        a_spec=pl.BlockSpec((tk, D_MODEL), lambda ss, j, k: (k, 0)),
                        b_spec=pl.BlockSpec((None, tk, FF_SH), lambda ss, j, k: (ss, k, 0)),
                        o_spec=pl.BlockSpec((None, D_MODEL, FF_SH), lambda ss, j, k: (ss, 0, 0)),
                        out_shape=jax.ShapeDtypeStruct((2 * N_FF, D_MODEL, FF_SH), BF16), contract=TN,
                        ride=ride_of('ffn_in_dw', g))
    dh1b = _mm("ffn_in_dx", dhgu, w['w_ffn_in'], grid=(t // tm, 1, 2 * N_FF),
               a_spec=pl.BlockSpec((None, tm, FF_SH), lambda i, j, ss: (ss, i, 0)),
               b_spec=pl.BlockSpec((None, D_MODEL, FF_SH), lambda i, j, ss: (ss, 0, 0)),
               o_spec=pl.BlockSpec((tm, D_MODEL), lambda i, j, ss: (i, 0)),
               out_shape=jax.ShapeDtypeStruct((t, D_MODEL), F32), contract=NT)
    dh, dmix, g['ln_mix_g'], g['ln_mix_b'] = _ln_resid_bwd(
        "ln_mix_bwd", s['h'], s['mix'], w['ln_mix_g'], w['ln_mix_b'], dh1, dh1b)
    g['w_out'] = _mm_tn("mix_out_dw", s['merged'], dmix, out_dtype=BF16)
    dmerged = _mm_nt("mix_out_dx", dmix, w['w_out'])
    z = s['zr']
    dga, dgb, dya, dyb, dbm0, dbm1 = _vjp_rowwise(
        "merge_bwd", _f_merge, [(z, OFF_GA, D_MODEL), (z, OFF_GB, D_MODEL), s['ya'], s['yb']],
        [w['b_merge0'], w['b_merge1']], [dmerged], 4, dtypes=[BF16] * 4)
    g['b_merge'] = jnp.concatenate([dbm0, dbm1], axis=0)
    g['w_branch_att'] = _mm_tn("branch_att_dw", s['attb'], dya, out_dtype=BF16)
    g['w_branch_rnn'] = _mm_tn("branch_rnn_dw", s['rnn'], dyb, out_dtype=BF16)
    datt = _mm_nt("branch_att_dx", dya, w['w_branch_att'], out_dtype=BF16)
    drnn = _mm_nt("branch_rnn_dx", dyb, w['w_branch_rnn'])
    dhs, dry = _vjp_rowwise("rnn_out_bwd", _f_rnn_out, [s['hs'], (z, OFF_RY, D_MODEL)], [], [drnn], 2,
                            dtypes=[F32, BF16])
    lam = _scan("lru_bwd", s['a'], dhs, bsz, seq, reverse=True)
    da, = _rowwise("lru_da", lambda l, hp: (l * hp,), [lam, s['hprev']], [], [(D_MODEL, F32)], [])
    dxc, g['rg_w_a'], g['rg_w_x'], g['rg_b_a'], g['rg_b_x'], g['rg_lambda'] = _gate_bwd(
        s['xc'], w['rg_w_a'], w['rg_w_x'], w['rg_b_a'], w['rg_b_x'], w['rg_lambda'], da, lam)
    drx, g['conv_w'], g['conv_b'] = _conv_bwd(z, dxc, w['conv_w'], bsz, seq)
    dq, dk, dv, dcq, dck = _attn_bwd(s['zq'], s['att'], datt, s['lse'], s['cq'], s['ck'], bsz, seq,
                                     ride=ride_of('attn_bwd', g))
    dc = (dcq[:, :, :, 0] + dck.reshape(bsz, N_HEADS, seq)).transpose(0, 2, 1)
    dc = jnp.pad(dc, ((0, 0), (0, 0), (0, BLK - N_HEADS))).reshape(t, BLK)
    dlogf = _scan("cumsum_bwd", None, dc, bsz, seq, reverse=True)
    dfl, g['b_forget'] = _vjp_rowwise("logf_bwd", _f_logf, [s['fl']], [w['b_forget']], [dlogf], 1, dtypes=[BF16])
    dz = jnp.concatenate([dq, dk, dv, drx, dry, dga, dgb], axis=1)
    g['w_in7'] = _mm_tn("z_proj_dw", s['hb'], dz, out_dtype=BF16)
    g['w_inf'] = _mm_tn("f_proj_dw", s['hb'], dfl, out_dtype=BF16)
    dhz = _mm_nt("z_proj_dx", dz, w['w_in7'], ride=ride_of('z_proj_dx', g))
    dhf = _mm_nt("f_proj_dx", dfl, w['w_inf'])
    dh_in, = _rowwise("dh_sum", lambda x0, x1, x2: (x0 + x1 + x2,), [dh, dhz, dhf], [], [(D_MODEL, F32)], [])
    return dh_in, g


def _ln_resid_bwd(name, h, branch, gam, bet, d0, d1):
    def bwd(hv, bv, d0v, d1v, gv, btv):
        _, pull = jax.vjp(_f_resid_ln, hv, bv, gv, btv)
        dh, db, dg, dbt = pull(d0v + d1v)
        return dh, db, _colsum(dg), _colsum(dbt)

    return _rowwise(name, bwd, [h, branch, d0, d1], [gam, bet], [(D_MODEL, F32), (D_MODEL, BF16)],
                    [D_MODEL, D_MODEL])


class _Schedule:
    FWD = {'z_proj_qkv': ['w_ffn_out'], 'z_proj_rest': ['w_branch_att', 'w_branch_rnn', 'w_out', 'w_ple_gate'],
           'attn_fwd': ['w_in'], 'ffn_in': ['w_ffn_in'], 'ffn_out': ['w_ple', 'conv_w', 'b_merge']}
    BWD = {'ffn_in_dw': ['w_ffn_out', 'w_ple_gate', 'w_ple'],
           'attn_bwd': ['w_ffn_in', 'w_out', 'w_branch_att', 'w_branch_rnn', 'conv_w', 'b_merge'],
           'z_proj_dx': ['w_in']}

    def __init__(self, shards, depth):
        self.shards, self.depth = shards, depth
        self.gathered = [{} for _ in range(depth)]
        self.received = [{} for _ in range(depth)]
        self.pending = []

    def gather_ride(self, layer, kernel_name):
        if layer + 1 >= self.depth:
            return None
        names = self.FWD[kernel_name]
        ride = _Ride([self.shards[n] for n in names], gather=True, index=layer + 1)
        self.pending.append((ride, names, self.gathered[layer + 1]))
        return ride

    def scatter_ride(self, layer, kernel_name, grads):
        names = self.BWD[kernel_name]
        ride = _Ride([_by_destination(n, grads) for n in names], gather=False)
        self.pending.append((ride, names, self.received[layer]))
        return ride

    def collect(self):
        for ride, names, dst in self.pending:
            dst.update(zip(names, ride.result))
        self.pending = []


def _local_step(x2, tgt, p3, weights_of, depth, g_in, b_in, bsz, seq, sched=None):
    h, hb = _rowwise("ln_in_fwd", _both(_ln), [x2], [g_in, b_in], TWICE, [])
    p3 = p3.astype(BF16)
    saved, layer_w = [], []
    for l in range(depth):
        layer_w.append(weights_of(l))
        ride_of = functools.partial(sched.gather_ride, l) if sched else _no_ride
        h, hb, s = _layer_fwd(h, hb, p3[l], layer_w[l], bsz, seq, ride_of)
        if sched:
            sched.collect()
        saved.append(s)

    def loss_fn(y, tv):
        err = y - tv
        return err * (1.0 / D_MODEL), _colsum(jnp.square(err))

    dh, sq = _rowwise("loss", loss_fn, [h, tgt], [], [(D_MODEL, F32)], [D_MODEL])
    grads = [None] * depth
    for l in reversed(range(depth)):
        ride_of = functools.partial(sched.scatter_ride, l) if sched else _no_ride
        dh, grads[l] = _layer_bwd(dh, p3[l], layer_w[l], saved[l], bsz, seq, ride_of)
        if sched:
            sched.collect()
    dx, dg_in, db_in = _vjp_rowwise("ln_in_bwd", _ln, [x2], [g_in, b_in], [dh], 1)
    return sq, dx, grads, dg_in, db_in


def _layer_weights(full):
    w = {}
    wt = full['w_in'].transpose(1, 0, 2).reshape(D_MODEL, N_IN)
    w['w_in7'] = jnp.concatenate([wt[:, :3 * D_MODEL], wt[:, 3 * D_MODEL + N_HEADS:]], axis=1)
    w['w_inf'] = jnp.pad(wt[:, 3 * D_MODEL:3 * D_MODEL + N_HEADS], ((0, 0), (0, BLK - N_HEADS)))
    for n in ['w_branch_att', 'w_branch_rnn', 'w_out', 'w_ple_gate']:
        w[n] = full[n].reshape(D_MODEL, D_MODEL)
    w['w_ffn_in'] = full['w_ffn_in']
    w['w_ffn_out'] = full['w_ffn_out'].reshape(N_FF, FF_SH, D_MODEL)
    w['w_ple'] = full['w_ple'].transpose(1, 0, 2).reshape(D_PLE, D_MODEL)
    w['conv_w'] = full['conv_w'].transpose(1, 0, 2).reshape(CONV_W, D_MODEL)
    bm = full['b_merge'].transpose(1, 0, 2).reshape(2, D_MODEL)
    w['b_merge0'], w['b_merge1'] = bm[0:1], bm[1:2]
    return w


def _by_destination(name, gw):
    if name == 'w_in':
        g7, gf = gw['w_in7'], gw['w_inf']
        true = jnp.concatenate([g7[:, :3 * D_MODEL], gf[:, :N_HEADS], g7[:, 3 * D_MODEL:]], axis=1)
        return true.reshape(D_MODEL, N_DEV, IN_SH).transpose(1, 0, 2)
    g = gw[name]
    if name in ('w_branch_att', 'w_branch_rnn', 'w_out', 'w_ple_gate'):
        return g.reshape(N_DEV, D_MODEL // N_DEV, D_MODEL)
    if name == 'w_ffn_in':
        return g
    if name == 'w_ffn_out':
        return g.reshape(N_DEV, N_FF * FF_SH // N_DEV, D_MODEL)
    return g.reshape(g.shape[0], N_DEV, BLK).transpose(1, 0, 2)


def kernel(x, p, ln_in_g, ln_in_b, w_in, b_forget, conv_w, conv_b, rg_w_a, rg_b_a, rg_w_x, rg_b_x, rg_lambda, w_branch_att, w_branch_rnn, b_merge, w_out, ln_mix_g, ln_mix_b, w_ffn_in, w_ffn_out, ln_ffn_g, ln_ffn_b, w_ple, w_ple_gate, b_ple_gate, ln_ple_g, ln_ple_b, loss_target, m_ln_in_g, m_ln_in_b, m_w_in, m_b_forget, m_conv_w, m_conv_b, m_rg_w_a, m_rg_b_a, m_rg_w_x, m_rg_b_x, m_rg_lambda, m_w_branch_att, m_w_branch_rnn, m_b_merge, m_w_out, m_ln_mix_g, m_ln_mix_b, m_w_ffn_in, m_w_ffn_out, m_ln_ffn_g, m_ln_ffn_b, m_w_ple, m_w_ple_gate, m_b_ple_gate, m_ln_ple_g, m_ln_ple_b, v_ln_in_g, v_ln_in_b, v_w_in, v_b_forget, v_conv_w, v_conv_b, v_rg_w_a, v_rg_b_a, v_rg_w_x, v_rg_b_x, v_rg_lambda, v_w_branch_att, v_w_branch_rnn, v_b_merge, v_w_out, v_ln_mix_g, v_ln_mix_b, v_w_ffn_in, v_w_ffn_out, v_ln_ffn_g, v_ln_ffn_b, v_w_ple, v_w_ple_gate, v_b_ple_gate, v_ln_ple_g, v_ln_ple_b):
    env = dict(locals())
    wts = {n: env[n] for n in WEIGHTS}
    mom = {n: env['m_' + n] for n in WEIGHTS}
    var = {n: env['v_' + n] for n in WEIGHTS}
    bsz, seq, _ = x.shape
    depth = w_in.shape[0]
    t = bsz * seq
    x2, tgt = x.reshape(t, D_MODEL), loss_target.reshape(t, D_MODEL)
    p3 = p.reshape(depth, t, D_PLE)

    shard_names = SHARDED_BF16 + SHARDED_F32
    shards = {n: wts[n].astype(BF16) for n in SHARDED_BF16}
    shards.update({n: wts[n] for n in SHARDED_F32})
    sched = _Schedule(shards, depth)
    first = _Ride([shards[n] for n in shard_names], gather=True, index=0)
    sched.gathered[0] = dict(zip(shard_names, _exchange("gather_layer0", first)))

    def weights_of(l):
        w = _layer_weights(sched.gathered[l])
        for n in ['conv_b', 'rg_b_a', 'rg_b_x', 'rg_lambda', 'ln_mix_g', 'ln_mix_b', 'ln_ffn_g', 'ln_ffn_b',
                  'b_ple_gate', 'ln_ple_g', 'ln_ple_b']:
            w[n] = _row(wts[n][l])
        w['b_forget'] = jnp.pad(_row(b_forget[l]), ((0, 0), (0, BLK - N_HEADS)))
        w['rg_w_a'], w['rg_w_x'] = rg_w_a[l], rg_w_x[l]
        return w

    g_in, b_in = _row(ln_in_g), _row(ln_in_b)
    sq, dx, grads, dg_in, db_in = _local_step(x2, tgt, p3, weights_of, depth, g_in, b_in, bsz, seq, sched)
    loss = lax.psum(0.5 * jnp.sum(sq) / D_MODEL, ("x", "y", "c"))
    grad_x = dx.reshape(bsz, seq, D_MODEL)

    out = {}
    for n in shard_names:
        shp = wts[n].shape
        flat = lambda a: a.reshape(-1, shp[-1])
        recv = [sched.received[l][n] for l in range(depth)]
        if n in SHARDED_F32:
            recv = [jnp.stack(recv, axis=1).reshape(N_DEV, -1, shp[-1])]
        res = _adamw("adamw_" + n, recv, flat(wts[n]), flat(mom[n]), flat(var[n]))
        out[n] = [r.reshape(shp) for r in res]

    def rep_grad(n):
        if n == 'ln_in_g':
            return dg_in.reshape(-1)
        if n == 'ln_in_b':
            return db_in.reshape(-1)
        return jnp.stack([grads[l][n].reshape(wts[n].shape[1:]) if n != 'b_forget'
                          else grads[l][n][0, :N_HEADS] for l in range(depth)]).reshape(-1)

    sizes = [int(wts[n].size) for n in REPLICATED]
    n_rows = [8 * (-(-sz // (8 * BLK))) for sz in sizes]
    total_rows = -(-sum(n_rows) // (N_DEV * 8)) * (N_DEV * 8)

    def as_rows(v, sz, nr):
        v = v.reshape(-1)
        return (jnp.pad(v, (0, nr * BLK - sz)) if nr * BLK != sz else v).reshape(nr, BLK)

    def pack(vals):
        parts = [as_rows(v, sz, nr) for v, sz, nr in zip(vals, sizes, n_rows)]
        parts.append(jnp.zeros((total_rows - sum(n_rows), BLK), F32))
        return jnp.concatenate(parts, axis=0)

    gp, = _exchange("scatter_small", _Ride([pack([rep_grad(n) for n in REPLICATED]).reshape(
        N_DEV, total_rows // N_DEV, BLK)], gather=False))
    g_slice = _sum_parts("sum_small", gp)
    g_all, = _exchange("gather_small", _Ride([g_slice], gather=True))
    res = _adamw("adamw_small", [g_all.reshape(1, total_rows, BLK)],
                 *[pack([d[n] for n in REPLICATED]) for d in (wts, mom, var)])
    starts = [sum(n_rows[:i]) for i in range(len(n_rows))]
    for n, r0, sz, nr in zip(REPLICATED, starts, sizes, n_rows):
        if nr * BLK == sz:
            out[n] = [r[r0:r0 + nr].reshape(wts[n].shape) for r in res]
        else:
            out[n] = [r[r0:r0 + nr].reshape(-1)[:sz].reshape(wts[n].shape) for r in res]

    return (loss, grad_x, *[out[n][k] for k in range(4) for n in WEIGHTS])
```

```python
import functools
import math

import jax
import jax.numpy as jnp
from jax import lax
from jax.experimental import pallas as pl
from jax.experimental.pallas import tpu as pltpu

F32 = jnp.float32
BF16 = jnp.bfloat16

N_DEV = 8
D_MODEL = 1024
N_HEADS = 8
HEAD_DIM = 128
N_BLK = 8
BLK = 128
CONV_W = 4
D_PLE = 256
FF_SH = 704
N_FF = 4
IN_SH = 897
N_IN = 7176
DEPTH = 4
RG_C = 8.0
ALPHA = float((2 * DEPTH) ** 0.25)
LN_EPS = 1e-5
SCALE = 1.0 / math.sqrt(HEAD_DIM)
NEG = -1e30
ADAM_LR, ADAM_B1, ADAM_B2, ADAM_EPS, ADAM_WD, ADAM_STEP = 0.001, 0.9, 0.999, 1e-08, 0.01, 10
QKV = 3 * D_MODEL
OFF_RX, OFF_RY, OFF_GA, OFF_GB = (i * D_MODEL for i in range(4))
V7X_VMEM_LIMIT = 48 * 1024 * 1024

WEIGHTS = ['ln_in_g', 'ln_in_b', 'w_in', 'b_forget', 'conv_w', 'conv_b', 'rg_w_a', 'rg_b_a', 'rg_w_x', 'rg_b_x',
           'rg_lambda', 'w_branch_att', 'w_branch_rnn', 'b_merge', 'w_out', 'ln_mix_g', 'ln_mix_b', 'w_ffn_in',
           'w_ffn_out', 'ln_ffn_g', 'ln_ffn_b', 'w_ple', 'w_ple_gate', 'b_ple_gate', 'ln_ple_g', 'ln_ple_b']
SHARDED_BF16 = ['w_in', 'w_branch_att', 'w_branch_rnn', 'w_out', 'w_ffn_in', 'w_ffn_out', 'w_ple', 'w_ple_gate']
SHARDED_F32 = ['conv_w', 'b_merge']
REPLICATED = [n for n in WEIGHTS if n not in SHARDED_BF16 and n not in SHARDED_F32]

NN = ((1,), (0,))
NT = ((1,), (1,))
TN = ((0,), (0,))


class _Ride:
    def __init__(self, arrays, *, gather, index=None):
        self.arrays, self.gather, self.index = list(arrays), gather, index
        self.result = None

    def out_shapes(self):
        if not self.gather:
            return [jax.ShapeDtypeStruct(a.shape, a.dtype) for a in self.arrays]
        cut = 0 if self.index is None else 1
        return [jax.ShapeDtypeStruct((N_DEV,) + a.shape[cut:], a.dtype) for a in self.arrays]

    def scratch(self):
        n = len(self.arrays)
        return [pltpu.SemaphoreType.DMA((n * N_DEV,)), pltpu.SemaphoreType.DMA((n * N_DEV,)),
                pltpu.SemaphoreType.DMA((n,))]

    def _copy(self, a, k, src, dst, sems, to=None):
        send_sems, recv_sems, _ = sems
        return pltpu.make_async_remote_copy(
            src_ref=src, dst_ref=dst, send_sem=send_sems.at[a * N_DEV + k], recv_sem=recv_sems.at[a * N_DEV + k],
            device_id=_peer(k if to is None else to), device_id_type=pl.DeviceIdType.MESH)

    def begin(self, ins, outs, sems):
        me = _my_id()
        started = []
        for a in range(len(ins)):
            if self.gather:
                src = ins[a] if self.index is None else ins[a].at[self.index]
                started.append(pltpu.make_async_copy(src, outs[a].at[me], sems[2].at[a]))
                started += [self._copy(a, k, src, outs[a].at[me], sems) for k in (1, 2, 4, 6)]
            else:
                started.append(pltpu.make_async_copy(ins[a].at[me], outs[a].at[me], sems[2].at[a]))
                started += [self._copy(a, k, ins[a].at[me ^ k], outs[a].at[me], sems) for k in range(1, N_DEV)]
        for cp in started:
            cp.start()

    def finish(self, ins, outs, sems):
        me = _my_id()
        for a in range(len(ins)):
            if self.gather:
                src = ins[a] if self.index is None else ins[a].at[self.index]
                passed = []
                for k in (2, 4, 6):
                    block = outs[a].at[me ^ k]
                    self._copy(a, k, src, block, sems).wait_recv()
                    passed.append(self._copy(a, k + 1, block, block, sems, to=1))
                    passed[-1].start()
                for k in (1, 2, 4, 6):
                    self._copy(a, k, src, outs[a].at[me], sems).wait_send()
                self._copy(a, 1, src, outs[a].at[me ^ 1], sems).wait_recv()
                for cp in passed:
                    cp.wait()
                pltpu.make_async_copy(src, outs[a].at[me], sems[2].at[a]).wait()
            else:
                pltpu.make_async_copy(ins[a].at[me], outs[a].at[me], sems[2].at[a]).wait()
                for k in range(1, N_DEV):
                    self._copy(a, k, ins[a].at[me ^ k], outs[a].at[me], sems).wait()


def _pcall(body, ride=None, **kw):
    if ride is None:
        return pl.pallas_call(body, **kw)
    n = len(ride.arrays)
    grid = kw['grid']
    single = not isinstance(kw['out_shape'], (list, tuple))
    out_specs = [kw['out_specs']] if single else list(kw['out_specs'])
    out_shape = [kw['out_shape']] if single else list(kw['out_shape'])
    in_specs = list(kw['in_specs'])
    scratch = list(kw.get('scratch_shapes', ()))
    n_in, n_out, n_sc = len(in_specs), len(out_shape), len(scratch)
    hbm = pl.BlockSpec(memory_space=pltpu.HBM)

    def wrapped(*refs):
        ins, xin = refs[:n_in], refs[n_in:n_in + n]
        outs, xout = refs[n_in + n:n_in + n + n_out], refs[n_in + n + n_out:n_in + 2 * n + n_out]
        sc, sems = refs[n_in + 2 * n + n_out:n_in + 2 * n + n_out + n_sc], refs[-3:]
        ids = [pl.program_id(ax) for ax in range(len(grid))]
        first = functools.reduce(jnp.logical_and, [i == 0 for i in ids])
        last = functools.reduce(jnp.logical_and, [i == g - 1 for i, g in zip(ids, grid)])

        pl.when(first)(lambda: ride.begin(xin, xout, sems))
        body(*ins, *outs, *sc)
        pl.when(last)(lambda: ride.finish(xin, xout, sems))

    call = pl.pallas_call(wrapped, name=kw['name'], grid=grid, in_specs=in_specs + [hbm] * n,
                          out_specs=out_specs + [hbm] * n, out_shape=out_shape + ride.out_shapes(),
                          scratch_shapes=scratch + ride.scratch(), compiler_params=kw['compiler_params'])

    def run(*args):
        res = call(*args, *ride.arrays)
        ride.result = list(res[n_out:])
        return res[0] if single else list(res[:n_out])

    return run


def _tile(n, pref, mult=8):
    if n <= pref:
        return n
    t = (pref // mult) * mult
    while t >= mult:
        if n % t == 0:
            return t
        t -= mult
    return n


def _cparams(sem):
    return pltpu.CompilerParams(dimension_semantics=sem, vmem_limit_bytes=V7X_VMEM_LIMIT)


def _mm(name, a, b, *, grid, a_spec, b_spec, o_spec, out_shape, contract, ride=None):
    nk = grid[-1]
    acc_shape = tuple(d for d in o_spec.block_shape if d is not None)

    def body(a_ref, b_ref, o_ref, acc_ref):
        k = pl.program_id(len(grid) - 1)
        part = lax.dot_general(a_ref[...].astype(BF16), b_ref[...].astype(BF16), (contract, ((), ())),
                               preferred_element_type=F32)

        @pl.when(k == 0)
        def _():
            acc_ref[...] = part

        @pl.when(k > 0)
        def _():
            acc_ref[...] += part

        @pl.when(k == nk - 1)
        def _():
            o_ref[...] = acc_ref[...].astype(o_ref.dtype)

    sem = ("parallel",) * (len(grid) - 1) + ("arbitrary",)
    return _pcall(body, ride=ride, name=name, grid=grid, in_specs=[a_spec, b_spec], out_specs=o_spec,
                  out_shape=out_shape, scratch_shapes=[pltpu.VMEM(acc_shape, F32)],
                  compiler_params=_cparams(sem))(a, b)


def _mm_nn(name, a, b, *, b_off=0, n=None, out_dtype=F32, tm=1024, tn=1024, tk=1024, ride=None):
    m, k = a.shape
    n = b.shape[1] if n is None else n
    tm, tn, tk = _tile(m, tm), _tile(n, tn, 128), _tile(k, tk, 128)
    no = b_off // tn
    return _mm(name, a, b, grid=(m // tm, n // tn, k // tk),
               a_spec=pl.BlockSpec((tm, tk), lambda i, j, kk: (i, kk)),
               b_spec=pl.BlockSpec((tk, tn), lambda i, j, kk: (kk, j + no)),
               o_spec=pl.BlockSpec((tm, tn), lambda i, j, kk: (i, j)),
               out_shape=jax.ShapeDtypeStruct((m, n), out_dtype), contract=NN, ride=ride)


def _mm_nt(name, a, b, *, out_dtype=F32, tm=1024, tn=1024, tk=1024, ride=None):
    m, k = a.shape
    n = b.shape[0]
    tm, tn, tk = _tile(m, tm), _tile(n, tn, 128), _tile(k, tk, 128)
    return _mm(name, a, b, grid=(m // tm, n // tn, k // tk),
               a_spec=pl.BlockSpec((tm, tk), lambda i, j, kk: (i, kk)),
               b_spec=pl.BlockSpec((tn, tk), lambda i, j, kk: (j, kk)),
               o_spec=pl.BlockSpec((tm, tn), lambda i, j, kk: (i, j)),
               out_shape=jax.ShapeDtypeStruct((m, n), out_dtype), contract=NT, ride=ride)


def _mm_tn(name, a, b, *, a_off=0, m=None, out_dtype=F32, tm=1024, tn=1024, tk=2048, ride=None):
    t, n = b.shape
    m = a.shape[1] if m is None else m
    tm, tn, tk = _tile(m, tm, 128), _tile(n, tn, 128), _tile(t, tk)
    mo = a_off // tm
    return _mm(name, a, b, grid=(m // tm, n // tn, t // tk),
               a_spec=pl.BlockSpec((tk, tm), lambda i, j, kk: (kk, i + mo)),
               b_spec=pl.BlockSpec((tk, tn), lambda i, j, kk: (kk, j)),
               o_spec=pl.BlockSpec((tm, tn), lambda i, j, kk: (i, j)),
               out_shape=jax.ShapeDtypeStruct((m, n), out_dtype), contract=TN, ride=ride)


def _rowwise(name, fn, rows, params, out_rows, out_reds, tm=256):
    rows = [r if isinstance(r, tuple) else (r, 0, r.shape[1]) for r in rows]
    t = rows[0][0].shape[0]
    tm = _tile(t, tm)
    in_specs = []
    for _, off, w in rows:
        in_specs.append(pl.BlockSpec((tm, w), functools.partial(lambda i, cb: (i, cb), cb=off // w)))
    for p in params:
        in_specs.append(pl.BlockSpec((1, p.shape[1]), lambda i: (0, 0)))
    out_specs = [pl.BlockSpec((tm, w), lambda i: (i, 0)) for w, _ in out_rows]
    out_specs += [pl.BlockSpec((1, w), lambda i: (0, 0)) for w in out_reds]
    out_shape = [jax.ShapeDtypeStruct((t, w), dt) for w, dt in out_rows]
    out_shape += [jax.ShapeDtypeStruct((1, w), F32) for w in out_reds]
    nr, npar, nor = len(rows), len(params), len(out_rows)

    def body(*refs):
        ins, outs = refs[:nr + npar], refs[nr + npar:]
        vals = [r[...].astype(F32) for r in ins[:nr]]
        vals += [jnp.broadcast_to(r[...], (tm, r.shape[1])) for r in ins[nr:]]
        res = fn(*vals)
        step = pl.program_id(0)
        for o, v in zip(outs[:nor], res[:nor]):
            o[...] = v.astype(o.dtype)
        for o, v in zip(outs[nor:], res[nor:]):
            _accumulate(o, v, step)

    res = _pcall(body, name=name, grid=(t // tm,), in_specs=in_specs, out_specs=out_specs, out_shape=out_shape,
                 compiler_params=_cparams(("arbitrary",)))(*[r[0] for r in rows], *params)
    return res


def _accumulate(o_ref, v, step):
    @pl.when(step == 0)
    def _():
        o_ref[...] = v

    @pl.when(step > 0)
    def _():
        o_ref[...] += v


def _colsum(v):
    return jnp.sum(v, axis=0, keepdims=True)


def _vjp_rowwise(name, fn, rows, params, cots, n_row_grads, tm=256, dtypes=None):
    nr, npar, nc = len(rows), len(params), len(cots)

    def bwd(*vals):
        prim, par, ct = vals[:nr], vals[nr + nc:], vals[nr:nr + nc]
        _, pull = jax.vjp(fn, *prim, *par)
        grads = pull(tuple(ct) if nc > 1 else ct[0])
        return tuple(grads[:n_row_grads]) + tuple(_colsum(g) for g in grads[nr:])

    dtypes = [F32] * n_row_grads if dtypes is None else dtypes
    widths = [(r[2] if isinstance(r, tuple) else r.shape[1], dt) for r, dt in zip(rows[:n_row_grads], dtypes)]
    return _rowwise(name, bwd, list(rows) + list(cots), params, widths, [p.shape[1] for p in params], tm=tm)


def _ln(s, g, b):
    mu = jnp.mean(s, axis=-1, keepdims=True)
    var = jnp.mean(jnp.square(s - mu), axis=-1, keepdims=True)
    return (s - mu) * lax.rsqrt(var + LN_EPS) * g + b


def _softplus(x):
    return jnp.maximum(x, 0.0) + jnp.log1p(jnp.exp(-jnp.abs(x)))


def _expm1(x):
    series = x * (1.0 + x * (1.0 / 2 + x * (1.0 / 6 + x * (1.0 / 24 + x * (1.0 / 120 + x * (1.0 / 720))))))
    return jnp.where(jnp.abs(x) < 0.25, series, jnp.exp(x) - 1.0)


def _f_resid_ln(h, branch, g, b):
    return _ln(ALPHA * h + branch, g, b)


def _f_ple(h, gp, pe, bpg, g, b):
    return _ln(ALPHA * h + jax.nn.sigmoid(gp + bpg) * pe, g, b)


def _f_merge(ga, gb, ya, yb, bm0, bm1):
    return jax.nn.sigmoid(ga + bm0) * ya + jax.nn.sigmoid(gb + bm1) * yb


def _f_rnn_out(hs, ry):
    return hs * jax.nn.gelu(ry, approximate=True)


def _f_logf(fl, bf):
    return -_softplus(-(fl + bf))


def _f_gate(xc, ra, ia, lam, ba, bx):
    r = jax.nn.sigmoid(ra + ba)
    i = jax.nn.sigmoid(ia + bx)
    log_a = -RG_C * _softplus(-lam) * r
    a = jnp.exp(log_a)
    mult = jnp.sqrt(-_expm1(2.0 * log_a))
    return a, mult * (i * xc)


def _f_act(hg, hu):
    return jax.nn.silu(hg) * hu


ATT_BLOCK = 512


def _scores(q, k, cq, ck, diagonal):
    s = lax.dot_general(q, k, (NT, ((), ())), preferred_element_type=F32) * SCALE
    s = s + cq - ck
    if diagonal:
        row = lax.broadcasted_iota(jnp.int32, s.shape, 0)
        col = lax.broadcasted_iota(jnp.int32, s.shape, 1)
        s = jnp.where(col <= row, s, NEG)
    return s


def _dscores(p, do, o, v):
    dob = do.astype(BF16)
    delta = jnp.sum(dob.astype(F32) * o, axis=1, keepdims=True)
    dp = lax.dot_general(dob, v.astype(BF16), (NT, ((), ())), preferred_element_type=F32)
    return p * (dp - delta)


def _attn_fwd(z, cq, ck, bsz, seq, ride=None):
    t = bsz * seq
    tq = _tile(seq, ATT_BLOCK)
    nq = seq // tq

    def body(q_ref, k_ref, v_ref, cq_ref, ck_ref, o_ref, ob_ref, lse_ref):
        for i in range(nq):
            rows = slice(i * tq, (i + 1) * tq)
            q = q_ref[rows, :].astype(BF16)
            cqi = cq_ref[rows, :]

            def step(j, carry, diagonal, q=q, cqi=cqi):
                m, l, acc = carry
                keys = pl.ds(pl.multiple_of(j * tq, tq), tq)
                s = _scores(q, k_ref[keys, :].astype(BF16), cqi, ck_ref[pl.ds(j, 1), :], diagonal)
                m_new = jnp.maximum(m, jnp.max(s, axis=1, keepdims=True))
                alpha = jnp.exp(m - m_new)
                p = jnp.exp(s - m_new)
                p_hi = p.astype(BF16)
                p_lo = (p - p_hi.astype(F32)).astype(BF16)
                vb = v_ref[keys, :].astype(BF16)
                pv = lax.dot_general(p_hi, vb, (NN, ((), ())), preferred_element_type=F32)
                pv = pv + lax.dot_general(p_lo, vb, (NN, ((), ())), preferred_element_type=F32)
                return m_new, alpha * l + jnp.sum(p, axis=1, keepdims=True), alpha * acc + pv

            carry = (jnp.full((tq, 1), NEG, F32), jnp.zeros((tq, 1), F32), jnp.zeros((tq, HEAD_DIM), F32))
            if i > 0:
                carry = lax.fori_loop(0, i, functools.partial(step, diagonal=False), carry)
            m, l, acc = step(i, carry, True)
            o = acc / l
            o_ref[rows, :] = o
            ob_ref[rows, :] = o.astype(BF16)
            lse_ref[rows, :] = m + jnp.log(l)

    head = (seq, HEAD_DIM)
    in_specs = [
        pl.BlockSpec(head, lambda b, h: (b, h)),
        pl.BlockSpec(head, lambda b, h: (b, N_HEADS + h)),
        pl.BlockSpec(head, lambda b, h: (b, 2 * N_HEADS + h)),
        pl.BlockSpec((None, None, seq, 1), lambda b, h: (b, h, 0, 0)),
        pl.BlockSpec((None, None, nq, tq), lambda b, h: (b, h, 0, 0)),
    ]
    out_specs = [pl.BlockSpec(head, lambda b, h: (b, h)), pl.BlockSpec(head, lambda b, h: (b, h)),
                 pl.BlockSpec((None, None, seq, 1), lambda b, h: (b, h, 0, 0))]
    out_shape = [jax.ShapeDtypeStruct((t, D_MODEL), F32), jax.ShapeDtypeStruct((t, D_MODEL), BF16),
                 jax.ShapeDtypeStruct((bsz, N_HEADS, seq, 1), F32)]
    return _pcall(body, ride=ride, name="attn_fwd", grid=(bsz, N_HEADS), in_specs=in_specs, out_specs=out_specs,
                  out_shape=out_shape, compiler_params=_cparams(("parallel", "parallel")))(
                      z, z, z, cq, ck.reshape(bsz, N_HEADS, nq, tq))


def _attn_bwd(z, att, datt, lse, cq, ck, bsz, seq, ride=None):
    t = bsz * seq
    tq = _tile(seq, ATT_BLOCK)
    nq = seq // tq

    def body(q_ref, k_ref, v_ref, o_ref, do_ref, lse_ref, cq_ref, ck_ref,
             dq_ref, dk_ref, dv_ref, dcq_ref, dck_ref, dq_sc):
        dq_sc[...] = jnp.zeros_like(dq_sc)
        dcq_ref[...] = jnp.zeros_like(dcq_ref)
        for j in range(nq):
            keys = slice(j * tq, (j + 1) * tq)
            kb = k_ref[keys, :].astype(BF16)
            vb = v_ref[keys, :].astype(BF16)
            ckj = ck_ref[j:j + 1, :]

            def step(i, carry, diagonal, kb=kb, vb=vb, ckj=ckj):
                dk, dv, dc = carry
                rows = pl.ds(pl.multiple_of(i * tq, tq), tq)
                qb = q_ref[rows, :].astype(BF16)
                do = do_ref[rows, :]
                s = _scores(qb, kb, cq_ref[rows, :], ckj, diagonal)
                p = jnp.exp(s - lse_ref[rows, :])
                ds = _dscores(p, do, o_ref[rows, :], vb)
                dsb = (ds * SCALE).astype(BF16)
                dq_sc[rows, :] += lax.dot_general(dsb, kb, (NN, ((), ())), preferred_element_type=F32)
                dcq_ref[rows, :] += jnp.sum(ds, axis=1, keepdims=True)
                dv = dv + lax.dot_general(p.astype(BF16), do.astype(BF16), (TN, ((), ())),
                                          preferred_element_type=F32)
                dk = dk + lax.dot_general(dsb, qb, (TN, ((), ())), preferred_element_type=F32)
                return dk, dv, dc - jnp.sum(ds, axis=0, keepdims=True)

            zero = jnp.zeros((tq, HEAD_DIM), F32)
            carry = step(j, (zero, zero, jnp.zeros((1, tq), F32)), True)
            if j + 1 < nq:
                carry = lax.fori_loop(j + 1, nq, functools.partial(step, diagonal=False), carry)
            dk, dv, dck_ref[j:j + 1, :] = carry
            dk_ref[keys, :] = dk.astype(BF16)
            dv_ref[keys, :] = dv.astype(BF16)
        dq_ref[...] = dq_sc[...].astype(BF16)

    head = (seq, HEAD_DIM)
    hmap = lambda b, h: (b, h)
    col = pl.BlockSpec((None, None, seq, 1), lambda b, h: (b, h, 0, 0))
    row = pl.BlockSpec((None, None, nq, tq), lambda b, h: (b, h, 0, 0))
    in_specs = [pl.BlockSpec(head, hmap),
                pl.BlockSpec(head, lambda b, h: (b, N_HEADS + h)),
                pl.BlockSpec(head, lambda b, h: (b, 2 * N_HEADS + h)),
                pl.BlockSpec(head, hmap), pl.BlockSpec(head, hmap), col, col, row]
    big = jax.ShapeDtypeStruct((t, D_MODEL), BF16)
    return _pcall(body, ride=ride, name="attn_bwd", grid=(bsz, N_HEADS), in_specs=in_specs,
                  out_specs=[pl.BlockSpec(head, hmap)] * 3 + [col, row],
                  out_shape=[big, big, big, jax.ShapeDtypeStruct((bsz, N_HEADS, seq, 1), F32),
                             jax.ShapeDtypeStruct((bsz, N_HEADS, nq, tq), F32)],
                  scratch_shapes=[pltpu.VMEM(head, F32)],
                  compiler_params=_cparams(("parallel", "parallel")))(
                      z, z, z, att, datt, lse, cq, ck.reshape(bsz, N_HEADS, nq, tq))


def _scan(name, a, u, bsz, seq, *, reverse, with_prev=False, tb=256):
    c = u.shape[1]
    tb = _tile(seq, tb)
    nb = seq // tb
    has_a = a is not None

    def body(*refs):
        if has_a:
            a_ref, u_ref = refs[0], refs[1]
            rest = refs[2:]
        else:
            u_ref = refs[0]
            rest = refs[1:]
        outs = rest[:2] if with_prev else rest[:1]
        carry_sc, afirst_sc = rest[-2], rest[-1]
        step = pl.program_id(1)

        @pl.when(step == 0)
        def _():
            carry_sc[...] = jnp.zeros_like(carry_sc)
            afirst_sc[...] = jnp.zeros_like(afirst_sc)

        row = lax.broadcasted_iota(jnp.int32, (tb, c), 0)
        uu = u_ref[...]
        if has_a:
            aa = a_ref[...]
            if reverse:
                coef = jnp.where(row < tb - 1, pltpu.roll(aa, tb - 1, 0), afirst_sc[...])
            else:
                coef = aa
        k = 1
        while k < tb:
            shift = tb - k if reverse else k
            keep = (row < tb - k) if reverse else (row >= k)
            uu_sh = jnp.where(keep, pltpu.roll(uu, shift, 0), 0.0)
            if has_a:
                uu = coef * uu_sh + uu
                coef = coef * jnp.where(keep, pltpu.roll(coef, shift, 0), 1.0)
            else:
                uu = uu + uu_sh
            k *= 2
        carry = carry_sc[...]
        h = uu + coef * carry if has_a else uu + carry
        outs[0][...] = h
        if with_prev:
            outs[1][...] = jnp.where(row >= 1, pltpu.roll(h, 1, 0), carry)
        if reverse:
            carry_sc[...] = outs[0][0:1, :]
            if has_a:
                afirst_sc[...] = a_ref[0:1, :]
        else:
            carry_sc[...] = outs[0][tb - 1:tb, :]

    if reverse:
        imap = lambda b, s: (b * nb + nb - 1 - s, 0)
    else:
        imap = lambda b, s: (b * nb + s, 0)
    spec = pl.BlockSpec((tb, c), imap)
    n_in = 2 if has_a else 1
    n_out = 2 if with_prev else 1
    res = _pcall(body, name=name, grid=(bsz, nb), in_specs=[spec] * n_in, out_specs=[spec] * n_out,
                 out_shape=[jax.ShapeDtypeStruct(u.shape, F32)] * n_out,
                 scratch_shapes=[pltpu.VMEM((1, c), F32), pltpu.VMEM((1, c), F32)],
                 compiler_params=_cparams(("parallel", "arbitrary")))(*([a, u] if has_a else [u]))
    return res if with_prev else res[0]


def _conv_fwd(z, w, b, bsz, seq, tb=256):
    c = D_MODEL
    t = bsz * seq
    tb = _tile(seq, tb)
    nb = seq // tb

    def body(x_ref, w_ref, b_ref, o_ref, tail_sc):
        step = pl.program_id(1)

        @pl.when(step == 0)
        def _():
            tail_sc[...] = jnp.zeros_like(tail_sc)

        x = x_ref[...]
        row8 = lax.broadcasted_iota(jnp.int32, (8, c), 0)
        tail = tail_sc[...]
        acc = w_ref[CONV_W - 1:CONV_W, :] * x + b_ref[...]
        for sh in range(1, CONV_W):
            xs = pltpu.roll(x, sh, 0)
            top = jnp.where(row8 < sh, pltpu.roll(tail, sh, 0), xs[0:8, :])
            xs = jnp.concatenate([top, xs[8:, :]], axis=0) if tb > 8 else top
            acc = acc + w_ref[CONV_W - 1 - sh:CONV_W - sh, :] * xs
        o_ref[...] = acc
        tail_sc[...] = x_ref[tb - 8:tb, :]

    return _pcall(body, name="conv_fwd", grid=(bsz, nb),
                  in_specs=[pl.BlockSpec((tb, c), lambda bb, s: (bb * nb + s, OFF_RX // c)),
                            pl.BlockSpec((CONV_W, c), lambda bb, s: (0, 0)),
                            pl.BlockSpec((1, c), lambda bb, s: (0, 0))],
                  out_specs=pl.BlockSpec((tb, c), lambda bb, s: (bb * nb + s, 0)),
                  out_shape=jax.ShapeDtypeStruct((t, c), F32),
                  scratch_shapes=[pltpu.VMEM((8, c), F32)],
                  compiler_params=_cparams(("parallel", "arbitrary")))(z, w, b)


def _conv_bwd(z, dxc, w, bsz, seq, tb=256):
    c = D_MODEL
    t = bsz * seq
    tb = _tile(seq, tb)
    nb = seq // tb

    def body(x_ref, g_ref, w_ref, dx_ref, dw_ref, db_ref, head_sc):
        bb, step = pl.program_id(0), pl.program_id(1)

        @pl.when(step == 0)
        def _():
            head_sc[...] = jnp.zeros_like(head_sc)

        x, g = x_ref[...], g_ref[...]
        row8 = lax.broadcasted_iota(jnp.int32, (8, c), 0)
        head = head_sc[...]
        dx = w_ref[CONV_W - 1:CONV_W, :] * g
        dws = [None] * CONV_W
        dws[CONV_W - 1] = _colsum(g * x)
        for sh in range(1, CONV_W):
            gs = pltpu.roll(g, tb - sh, 0)
            bot = jnp.where(row8 >= 8 - sh, pltpu.roll(head, 8 - sh, 0), gs[tb - 8:tb, :])
            gs = jnp.concatenate([gs[:tb - 8, :], bot], axis=0) if tb > 8 else bot
            dx = dx + w_ref[CONV_W - 1 - sh:CONV_W - sh, :] * gs
            dws[CONV_W - 1 - sh] = _colsum(gs * x)
        dx_ref[...] = dx.astype(dx_ref.dtype)
        first = (bb == 0) & (step == 0)
        dw = jnp.concatenate(dws, axis=0)
        db = _colsum(g)

        @pl.when(first)
        def _():
            dw_ref[...] = dw
            db_ref[...] = db

        @pl.when(jnp.logical_not(first))
        def _():
            dw_ref[...] += dw
            db_ref[...] += db

        head_sc[...] = g_ref[0:8, :]

    rmap = lambda bb, s: (bb * nb + nb - 1 - s, 0)
    return _pcall(body, name="conv_bwd", grid=(bsz, nb),
                  in_specs=[pl.BlockSpec((tb, c), lambda bb, s: (bb * nb + nb - 1 - s, OFF_RX // c)),
                            pl.BlockSpec((tb, c), rmap),
                            pl.BlockSpec((CONV_W, c), lambda bb, s: (0, 0))],
                  out_specs=[pl.BlockSpec((tb, c), rmap),
                             pl.BlockSpec((CONV_W, c), lambda bb, s: (0, 0)),
                             pl.BlockSpec((1, c), lambda bb, s: (0, 0))],
                  out_shape=[jax.ShapeDtypeStruct((t, c), BF16), jax.ShapeDtypeStruct((CONV_W, c), F32),
                             jax.ShapeDtypeStruct((1, c), F32)],
                  scratch_shapes=[pltpu.VMEM((8, c), F32)],
                  compiler_params=_cparams(("arbitrary", "arbitrary")))(z, dxc, w)


def _gate_fwd(xc, w_a, w_x, b_a, b_x, lam, tm=512):
    t = xc.shape[0]
    tm = _tile(t, tm)

    def body(xc_ref, wa_ref, wx_ref, ba_ref, bx_ref, lam_ref, a_ref, u_ref):
        xc_b = xc_ref[...]
        xb = xc_b.astype(BF16)
        ra = lax.dot_general(xb, wa_ref[...].astype(BF16), (NN, ((), ())), preferred_element_type=F32)
        ia = lax.dot_general(xb, wx_ref[...].astype(BF16), (NN, ((), ())), preferred_element_type=F32)
        a, u = _f_gate(xc_b, ra, ia, lam_ref[...], ba_ref[...], bx_ref[...])
        a_ref[...] = a
        u_ref[...] = u

    row = pl.BlockSpec((tm, BLK), lambda n, i: (i, n))
    wsp = pl.BlockSpec((None, BLK, BLK), lambda n, i: (n, 0, 0))
    vec = pl.BlockSpec((1, BLK), lambda n, i: (0, n))
    return _pcall(body, name="gate_fwd", grid=(N_BLK, t // tm), in_specs=[row, wsp, wsp, vec, vec, vec],
                  out_specs=[row, row], out_shape=[jax.ShapeDtypeStruct((t, D_MODEL), F32)] * 2,
                  compiler_params=_cparams(("parallel", "parallel")))(xc, w_a, w_x, b_a, b_x, lam)


def _gate_bwd(xc, w_a, w_x, b_a, b_x, lam, da, du, tm=512):
    t = xc.shape[0]
    tm = _tile(t, tm)

    def body(xc_ref, wa_ref, wx_ref, ba_ref, bx_ref, lam_ref, da_ref, du_ref,
             dxc_ref, dwa_ref, dwx_ref, dba_ref, dbx_ref, dlam_ref):
        step = pl.program_id(1)
        xc_b = xc_ref[...]
        xb = xc_b.astype(BF16)
        wa, wx = wa_ref[...].astype(BF16), wx_ref[...].astype(BF16)
        ra = lax.dot_general(xb, wa, (NN, ((), ())), preferred_element_type=F32)
        ia = lax.dot_general(xb, wx, (NN, ((), ())), preferred_element_type=F32)
        full = lambda r: jnp.broadcast_to(r[...], (tm, BLK))
        _, pull = jax.vjp(_f_gate, xc_b, ra, ia, full(lam_ref), full(ba_ref), full(bx_ref))
        dxc, dra, dia, dlam, dba, dbx = pull((da_ref[...], du_ref[...]))
        drb, dib = dra.astype(BF16), dia.astype(BF16)
        dxc = dxc + lax.dot_general(drb, wa, (NT, ((), ())), preferred_element_type=F32)
        dxc = dxc + lax.dot_general(dib, wx, (NT, ((), ())), preferred_element_type=F32)
        dxc_ref[...] = dxc
        _accumulate(dwa_ref, lax.dot_general(xb, drb, (TN, ((), ())), preferred_element_type=F32), step)
        _accumulate(dwx_ref, lax.dot_general(xb, dib, (TN, ((), ())), preferred_element_type=F32), step)
        _accumulate(dba_ref, _colsum(dba), step)
        _accumulate(dbx_ref, _colsum(dbx), step)
        _accumulate(dlam_ref, _colsum(dlam), step)

    row = pl.BlockSpec((tm, BLK), lambda n, i: (i, n))
    wsp = pl.BlockSpec((None, BLK, BLK), lambda n, i: (n, 0, 0))
    vec = pl.BlockSpec((1, BLK), lambda n, i: (0, n))
    wshape = jax.ShapeDtypeStruct((N_BLK, BLK, BLK), F32)
    vshape = jax.ShapeDtypeStruct((1, D_MODEL), F32)
    return _pcall(body, name="gate_bwd", grid=(N_BLK, t // tm),
                  in_specs=[row, wsp, wsp, vec, vec, vec, row, row],
                  out_specs=[row, wsp, wsp, vec, vec, vec],
                  out_shape=[jax.ShapeDtypeStruct((t, D_MODEL), F32), wshape, wshape, vshape, vshape, vshape],
                  compiler_params=_cparams(("parallel", "arbitrary")))(xc, w_a, w_x, b_a, b_x, lam, da, du)


def _act_fwd(hgu, tm=512):
    _, t, w = hgu.shape
    tm = _tile(t, tm)

    def body(hg_ref, hu_ref, o_ref):
        o_ref[...] = _f_act(hg_ref[...], hu_ref[...]).astype(o_ref.dtype)

    spec = lambda off: pl.BlockSpec((None, tm, w), lambda s, i: (s + off, i, 0))
    return _pcall(body, name="act_fwd", grid=(N_FF, t // tm), in_specs=[spec(0), spec(N_FF)], out_specs=spec(0),
                  out_shape=jax.ShapeDtypeStruct((N_FF, t, w), BF16),
                  compiler_params=_cparams(("parallel", "parallel")))(hgu, hgu)


def _act_bwd(hgu, dact, tm=512):
    _, t, w = hgu.shape
    tm = _tile(t, tm)

    def body(hg_ref, hu_ref, d_ref, o_ref):
        _, pull = jax.vjp(_f_act, hg_ref[...], hu_ref[...])
        dhg, dhu = pull(d_ref[...])
        o_ref[0] = dhg.astype(o_ref.dtype)
        o_ref[1] = dhu.astype(o_ref.dtype)

    spec = lambda off: pl.BlockSpec((None, tm, w), lambda s, i: (s + off, i, 0))
    res = _pcall(body, name="act_bwd", grid=(N_FF, t // tm), in_specs=[spec(0), spec(N_FF), spec(0)],
                 out_specs=pl.BlockSpec((2, None, tm, w), lambda s, i: (0, s, i, 0)),
                 out_shape=jax.ShapeDtypeStruct((2, N_FF, t, w), BF16),
                 compiler_params=_cparams(("parallel", "parallel")))(hgu, hgu, dact)
    return res.reshape(2 * N_FF, t, w)


def _adamw(name, parts, w, m, v, tr=128):
    ng = len(parts)
    n_src, r, c = parts[0].shape
    tr = _tile(r, tr)
    nb = r // tr
    bc1 = 1.0 - ADAM_B1 ** ADAM_STEP
    bc2 = 1.0 - ADAM_B2 ** ADAM_STEP

    def body(*refs):
        p_refs = refs[:ng]
        w_ref, m_ref, v_ref, g_ref, d_ref, nm_ref, nv_ref = refs[ng:]
        grp = pl.program_id(0)

        def update(p_ref):
            g = p_ref[0].astype(F32)
            for s in range(1, n_src):
                g = g + p_ref[s].astype(F32)
            nm = ADAM_B1 * m_ref[...] + (1.0 - ADAM_B1) * g
            nv = ADAM_B2 * v_ref[...] + (1.0 - ADAM_B2) * jnp.square(g)
            g_ref[...] = g
            nm_ref[...] = nm
            nv_ref[...] = nv
            d_ref[...] = -ADAM_LR * ((nm / bc1) / (jnp.sqrt(nv / bc2) + ADAM_EPS) + ADAM_WD * w_ref[...])

        for k in range(ng):
            pl.when(grp == k)(functools.partial(update, p_refs[k]))

    p_specs = [pl.BlockSpec((n_src, tr, c), functools.partial(lambda gi, i, k: (0, jnp.where(gi == k, i, 0), 0), k=k))
               for k in range(ng)]
    spec = pl.BlockSpec((tr, c), lambda gi, i: (gi * nb + i, 0))
    return _pcall(body, name=name, grid=(ng, nb), in_specs=p_specs + [spec, spec, spec],
                  out_specs=[spec] * 4, out_shape=[jax.ShapeDtypeStruct((ng * r, c), F32)] * 4,
                  compiler_params=_cparams(("parallel", "parallel")))(*parts, w, m, v)


def _sum_parts(name, parts, tr=256):
    _, r, c = parts.shape
    tr = _tile(r, tr)

    def body(p_ref, o_ref):
        g = p_ref[0]
        for s in range(1, parts.shape[0]):
            g = g + p_ref[s]
        o_ref[...] = g

    return _pcall(body, name=name, grid=(r // tr,),
                  in_specs=[pl.BlockSpec((parts.shape[0], tr, c), lambda i: (0, i, 0))],
                  out_specs=pl.BlockSpec((tr, c), lambda i: (i, 0)),
                  out_shape=jax.ShapeDtypeStruct((r, c), F32), compiler_params=_cparams(("parallel",)))(parts)


def _peer(k):
    x, y, c = lax.axis_index("x"), lax.axis_index("y"), lax.axis_index("c")
    return (x ^ ((k >> 2) & 1), y ^ ((k >> 1) & 1), c ^ (k & 1))


def _my_id():
    return 4 * lax.axis_index("x") + 2 * lax.axis_index("y") + lax.axis_index("c")


def _exchange(name, ride):
    n = len(ride.arrays)

    def body(*refs):
        ride.begin(refs[:n], refs[n:2 * n], refs[2 * n:])
        ride.finish(refs[:n], refs[n:2 * n], refs[2 * n:])

    hbm = pl.BlockSpec(memory_space=pltpu.HBM)
    return _pcall(body, name=name, in_specs=[hbm] * n, out_specs=[hbm] * n, out_shape=ride.out_shapes(),
                  scratch_shapes=ride.scratch())(*ride.arrays)


def _row(v):
    return v.reshape(1, -1)


def _time_major_heads(c, bsz, seq):
    return c.reshape(bsz, seq, BLK)[:, :, :N_HEADS].transpose(0, 2, 1)


def _no_ride(*_):
    return None


TWICE = [(D_MODEL, F32), (D_MODEL, BF16)]


def _both(fn):
    def run(*v):
        y = fn(*v)
        return y, y
    return run


def _layer_fwd(h, hb, p_l, w, bsz, seq, ride_of=_no_ride):
    t = bsz * seq
    zq = _mm_nn("z_proj_qkv", hb, w['w_in7'], n=QKV, out_dtype=BF16, ride=ride_of('z_proj_qkv'))
    zr = _mm_nn("z_proj_rest", hb, w['w_in7'], b_off=QKV, n=4 * D_MODEL, ride=ride_of('z_proj_rest'))
    fl = _mm_nn("f_proj", hb, w['w_inf'])
    logf, = _rowwise("logf_fwd", lambda f, b: (_f_logf(f, b),), [fl], [w['b_forget']], [(BLK, F32)], [])
    c = _scan("cumsum_fwd", None, logf, bsz, seq, reverse=False)
    ct = _time_major_heads(c, bsz, seq)
    cq, ck = ct[..., None], ct[:, :, None, :]
    att, attb, lse = _attn_fwd(zq, cq, ck, bsz, seq, ride=ride_of('attn_fwd'))
    xc = _conv_fwd(zr, w['conv_w'], w['conv_b'], bsz, seq)
    a, u = _gate_fwd(xc, w['rg_w_a'], w['rg_w_x'], w['rg_b_a'], w['rg_b_x'], w['rg_lambda'])
    hs, hprev = _scan("lru_fwd", a, u, bsz, seq, reverse=False, with_prev=True)
    rnn, = _rowwise("rnn_out_fwd", lambda s, y: (_f_rnn_out(s, y),), [hs, (zr, OFF_RY, D_MODEL)], [],
                    [(D_MODEL, BF16)], [])
    ya = _mm_nn("branch_att", attb, w['w_branch_att'])
    yb = _mm_nn("branch_rnn", rnn, w['w_branch_rnn'])
    merged, = _rowwise("merge_fwd", lambda *v: (_f_merge(*v),),
                       [(zr, OFF_GA, D_MODEL), (zr, OFF_GB, D_MODEL), ya, yb], [w['b_merge0'], w['b_merge1']],
                       [(D_MODEL, BF16)], [])
    mix = _mm_nn("mix_out", merged, w['w_out'])
    h1, h1b = _rowwise("ln_mix_fwd", _both(_f_resid_ln), [h, mix], [w['ln_mix_g'], w['ln_mix_b']], TWICE, [])
    tm = _tile(t, 1024)
    hgu = _mm("ffn_in", h1b, w['w_ffn_in'], grid=(t // tm, 2 * N_FF, 1),
              a_spec=pl.BlockSpec((tm, D_MODEL), lambda i, s, k: (i, 0)),
              b_spec=pl.BlockSpec((None, D_MODEL, FF_SH), lambda i, s, k: (s, 0, 0)),
              o_spec=pl.BlockSpec((None, tm, FF_SH), lambda i, s, k: (s, i, 0)),
              out_shape=jax.ShapeDtypeStruct((2 * N_FF, t, FF_SH), F32), contract=NN, ride=ride_of('ffn_in'))
    act = _act_fwd(hgu)
    ffn = _mm("ffn_out", act, w['w_ffn_out'], grid=(t // tm, 1, N_FF),
              a_spec=pl.BlockSpec((None, tm, FF_SH), lambda i, j, s: (s, i, 0)),
              b_spec=pl.BlockSpec((None, FF_SH, D_MODEL), lambda i, j, s: (s, 0, 0)),
              o_spec=pl.BlockSpec((tm, D_MODEL), lambda i, j, s: (i, 0)),
              out_shape=jax.ShapeDtypeStruct((t, D_MODEL), F32), contract=NN, ride=ride_of('ffn_out'))
    h2, h2b = _rowwise("ln_ffn_fwd", _both(_f_resid_ln), [h1, ffn], [w['ln_ffn_g'], w['ln_ffn_b']], TWICE, [])
    gp = _mm_nn("ple_gate", h2b, w['w_ple_gate'])
    pe = _mm_nn("ple_proj", p_l, w['w_ple'])
    h3, h3b = _rowwise("ln_ple_fwd", _both(_f_ple), [h2, gp, pe],
                       [w['b_ple_gate'], w['ln_ple_g'], w['ln_ple_b']], TWICE, [])
    saved = dict(h=h, hb=hb, zq=zq, zr=zr, fl=fl, cq=cq, ck=ck, att=att, attb=attb, lse=lse, xc=xc, a=a,
                 hprev=hprev, hs=hs, rnn=rnn, ya=ya, yb=yb, merged=merged, mix=mix, h1=h1, h1b=h1b, hgu=hgu,
                 act=act, ffn=ffn, h2=h2, h2b=h2b, gp=gp, pe=pe)
    return h3, h3b, saved


def _layer_bwd(dh3, p_l, w, s, bsz, seq, ride_of=_no_ride):
    t = bsz * seq
    g = {}
    dh2, dgp, dpe, g['b_ple_gate'], g['ln_ple_g'], g['ln_ple_b'] = _vjp_rowwise(
        "ln_ple_bwd", _f_ple, [s['h2'], s['gp'], s['pe']], [w['b_ple_gate'], w['ln_ple_g'], w['ln_ple_b']], [dh3], 3,
        dtypes=[F32, BF16, BF16])
    g['w_ple_gate'] = _mm_tn("ple_gate_dw", s['h2b'], dgp, out_dtype=BF16)
    g['w_ple'] = _mm_tn("ple_proj_dw", p_l, dpe, out_dtype=BF16)
    dh2b = _mm_nt("ple_gate_dx", dgp, w['w_ple_gate'])
    dh1, dffn, g['ln_ffn_g'], g['ln_ffn_b'] = _ln_resid_bwd(
        "ln_ffn_bwd", s['h1'], s['ffn'], w['ln_ffn_g'], w['ln_ffn_b'], dh2, dh2b)
    tm = _tile(t, 1024)
    dact = _mm("ffn_out_dx", dffn, w['w_ffn_out'], grid=(t // tm, N_FF, 1),
               a_spec=pl.BlockSpec((tm, D_MODEL), lambda i, ss, k: (i, 0)),
               b_spec=pl.BlockSpec((None, FF_SH, D_MODEL), lambda i, ss, k: (ss, 0, 0)),
               o_spec=pl.BlockSpec((None, tm, FF_SH), lambda i, ss, k: (ss, i, 0)),
               out_shape=jax.ShapeDtypeStruct((N_FF, t, FF_SH), F32), contract=NT)
    tk = _tile(t, 2048)
    g['w_ffn_out'] = _mm("ffn_out_dw", s['act'], dffn, grid=(N_FF, 1, t // tk),
                         a_spec=pl.BlockSpec((None, tk, FF_SH), lambda ss, j, k: (ss, k, 0)),
                         b_spec=pl.BlockSpec((tk, D_MODEL), lambda ss, j, k: (k, 0)),
                         o_spec=pl.BlockSpec((None, FF_SH, D_MODEL), lambda ss, j, k: (ss, 0, 0)),
                         out_shape=jax.ShapeDtypeStruct((N_FF, FF_SH, D_MODEL), BF16), contract=TN)
    dhgu = _act_bwd(s['hgu'], dact)
    g['w_ffn_in'] = _mm("ffn_in_dw", s['h1b'], dhgu, grid=(2 * N_FF, 1, t // tk),
                        a_spec=pl.BlockSpec((tk, D_MODEL), lambda ss, j, k: (k, 0)),
                        b_spec=pl.BlockSpec((None, tk, FF_SH), lambda ss, j, k: (ss, k, 0)),
                        o_spec=pl.BlockSpec((None, D_MODEL, FF_SH), lambda ss, j, k: (ss, 0, 0)),
                        out_shape=jax.ShapeDtypeStruct((2 * N_FF, D_MODEL, FF_SH), BF16), contract=TN,
                        ride=ride_of('ffn_in_dw', g))
    dh1b = _mm("ffn_in_dx", dhgu, w['w_ffn_in'], grid=(t // tm, 1, 2 * N_FF),
               a_spec=pl.BlockSpec((None, tm, FF_SH), lambda i, j, ss: (ss, i, 0)),
               b_spec=pl.BlockSpec((None, D_MODEL, FF_SH), lambda i, j, ss: (ss, 0, 0)),
               o_spec=pl.BlockSpec((tm, D_MODEL), lambda i, j, ss: (i, 0)),
               out_shape=jax.ShapeDtypeStruct((t, D_MODEL), F32), contract=NT)
    dh, dmix, g['ln_mix_g'], g['ln_mix_b'] = _ln_resid_bwd(
        "ln_mix_bwd", s['h'], s['mix'], w['ln_mix_g'], w['ln_mix_b'], dh1, dh1b)
    g['w_out'] = _mm_tn("mix_out_dw", s['merged'], dmix, out_dtype=BF16)
    dmerged = _mm_nt("mix_out_dx", dmix, w['w_out'])
    z = s['zr']
    dga, dgb, dya, dyb, dbm0, dbm1 = _vjp_rowwise(
        "merge_bwd", _f_merge, [(z, OFF_GA, D_MODEL), (z, OFF_GB, D_MODEL), s['ya'], s['yb']],
        [w['b_merge0'], w['b_merge1']], [dmerged], 4, dtypes=[BF16] * 4)
    g['b_merge'] = jnp.concatenate([dbm0, dbm1], axis=0)
    g['w_branch_att'] = _mm_tn("branch_att_dw", s['attb'], dya, out_dtype=BF16)
    g['w_branch_rnn'] = _mm_tn("branch_rnn_dw", s['rnn'], dyb, out_dtype=BF16)
    datt = _mm_nt("branch_att_dx", dya, w['w_branch_att'], out_dtype=BF16)
    drnn = _mm_nt("branch_rnn_dx", dyb, w['w_branch_rnn'])
    dhs, dry = _vjp_rowwise("rnn_out_bwd", _f_rnn_out, [s['hs'], (z, OFF_RY, D_MODEL)], [], [drnn], 2,
                            dtypes=[F32, BF16])
    lam = _scan("lru_bwd", s['a'], dhs, bsz, seq, reverse=True)
    da, = _rowwise("lru_da", lambda l, hp: (l * hp,), [lam, s['hprev']], [], [(D_MODEL, F32)], [])
    dxc, g['rg_w_a'], g['rg_w_x'], g['rg_b_a'], g['rg_b_x'], g['rg_lambda'] = _gate_bwd(
        s['xc'], w['rg_w_a'], w['rg_w_x'], w['rg_b_a'], w['rg_b_x'], w['rg_lambda'], da, lam)
    drx, g['conv_w'], g['conv_b'] = _conv_bwd(z, dxc, w['conv_w'], bsz, seq)
    dq, dk, dv, dcq, dck = _attn_bwd(s['zq'], s['att'], datt, s['lse'], s['cq'], s['ck'], bsz, seq,
                                     ride=ride_of('attn_bwd', g))
    dc = (dcq[:, :, :, 0] + dck.reshape(bsz, N_HEADS, seq)).transpose(0, 2, 1)
    dc = jnp.pad(dc, ((0, 0), (0, 0), (0, BLK - N_HEADS))).reshape(t, BLK)
    dlogf = _scan("cumsum_bwd", None, dc, bsz, seq, reverse=True)
    dfl, g['b_forget'] = _vjp_rowwise("logf_bwd", _f_logf, [s['fl']], [w['b_forget']], [dlogf], 1, dtypes=[BF16])
    dz = jnp.concatenate([dq, dk, dv, drx, dry, dga, dgb], axis=1)
    g['w_in7'] = _mm_tn("z_proj_dw", s['hb'], dz, out_dtype=BF16)
    g['w_inf'] = _mm_tn("f_proj_dw", s['hb'], dfl, out_dtype=BF16)
    dhz = _mm_nt("z_proj_dx", dz, w['w_in7'], ride=ride_of('z_proj_dx', g))
    dhf = _mm_nt("f_proj_dx", dfl, w['w_inf'])
    dh_in, = _rowwise("dh_sum", lambda x0, x1, x2: (x0 + x1 + x2,), [dh, dhz, dhf], [], [(D_MODEL, F32)], [])
    return dh_in, g


def _ln_resid_bwd(name, h, branch, gam, bet, d0, d1):
    def bwd(hv, bv, d0v, d1v, gv, btv):
        _, pull = jax.vjp(_f_resid_ln, hv, bv, gv, btv)
        dh, db, dg, dbt = pull(d0v + d1v)
        return dh, db, _colsum(dg), _colsum(dbt)

    return _rowwise(name, bwd, [h, branch, d0, d1], [gam, bet], [(D_MODEL, F32), (D_MODEL, BF16)],
                    [D_MODEL, D_MODEL])


class _Schedule:
    FWD = {'z_proj_qkv': ['w_ffn_out'], 'z_proj_rest': ['w_branch_att', 'w_branch_rnn', 'w_out', 'w_ple_gate'],
           'attn_fwd': ['w_in'], 'ffn_in': ['w_ffn_in'], 'ffn_out': ['w_ple', 'conv_w', 'b_merge']}
    BWD = {'ffn_in_dw': ['w_ffn_out', 'w_ple_gate', 'w_ple'],
           'attn_bwd': ['w_ffn_in', 'w_out', 'w_branch_att', 'w_branch_rnn', 'conv_w', 'b_merge'],
           'z_proj_dx': ['w_in']}

    def __init__(self, shards, depth):
        self.shards, self.depth = shards, depth
        self.gathered = [{} for _ in range(depth)]
        self.received = [{} for _ in range(depth)]
        self.pending = []

    def gather_ride(self, layer, kernel_name):
        if layer + 1 >= self.depth:
            return None
        names = self.FWD[kernel_name]
        ride = _Ride([self.shards[n] for n in names], gather=True, index=layer + 1)
        self.pending.append((ride, names, self.gathered[layer + 1]))
        return ride

    def scatter_ride(self, layer, kernel_name, grads):
        names = self.BWD[kernel_name]
        ride = _Ride([_by_destination(n, grads) for n in names], gather=False)
        self.pending.append((ride, names, self.received[layer]))
        return ride

    def collect(self):
        for ride, names, dst in self.pending:
            dst.update(zip(names, ride.result))
        self.pending = []


def _local_step(x2, tgt, p3, weights_of, depth, g_in, b_in, bsz, seq, sched=None):
    h, hb = _rowwise("ln_in_fwd", _both(_ln), [x2], [g_in, b_in], TWICE, [])
    p3 = p3.astype(BF16)
    saved, layer_w = [], []
    for l in range(depth):
        layer_w.append(weights_of(l))
        ride_of = functools.partial(sched.gather_ride, l) if sched else _no_ride
        h, hb, s = _layer_fwd(h, hb, p3[l], layer_w[l], bsz, seq, ride_of)
        if sched:
            sched.collect()
        saved.append(s)

    def loss_fn(y, tv):
        err = y - tv
        return err * (1.0 / D_MODEL), _colsum(jnp.square(err))

    dh, sq = _rowwise("loss", loss_fn, [h, tgt], [], [(D_MODEL, F32)], [D_MODEL])
    grads = [None] * depth
    for l in reversed(range(depth)):
        ride_of = functools.partial(sched.scatter_ride, l) if sched else _no_ride
        dh, grads[l] = _layer_bwd(dh, p3[l], layer_w[l], saved[l], bsz, seq, ride_of)
        if sched:
            sched.collect()
    dx, dg_in, db_in = _vjp_rowwise("ln_in_bwd", _ln, [x2], [g_in, b_in], [dh], 1)
    return sq, dx, grads, dg_in, db_in


def _layer_weights(full):
    w = {}
    wt = full['w_in'].transpose(1, 0, 2).reshape(D_MODEL, N_IN)
    w['w_in7'] = jnp.concatenate([wt[:, :3 * D_MODEL], wt[:, 3 * D_MODEL + N_HEADS:]], axis=1)
    w['w_inf'] = jnp.pad(wt[:, 3 * D_MODEL:3 * D_MODEL + N_HEADS], ((0, 0), (0, BLK - N_HEADS)))
    for n in ['w_branch_att', 'w_branch_rnn', 'w_out', 'w_ple_gate']:
        w[n] = full[n].reshape(D_MODEL, D_MODEL)
    w['w_ffn_in'] = full['w_ffn_in']
    w['w_ffn_out'] = full['w_ffn_out'].reshape(N_FF, FF_SH, D_MODEL)
    w['w_ple'] = full['w_ple'].transpose(1, 0, 2).reshape(D_PLE, D_MODEL)
    w['conv_w'] = full['conv_w'].transpose(1, 0, 2).reshape(CONV_W, D_MODEL)
    bm = full['b_merge'].transpose(1, 0, 2).reshape(2, D_MODEL)
    w['b_merge0'], w['b_merge1'] = bm[0:1], bm[1:2]
    return w


def _by_destination(name, gw):
    if name == 'w_in':
        g7, gf = gw['w_in7'], gw['w_inf']
        true = jnp.concatenate([g7[:, :3 * D_MODEL], gf[:, :N_HEADS], g7[:, 3 * D_MODEL:]], axis=1)
        return true.reshape(D_MODEL, N_DEV, IN_SH).transpose(1, 0, 2)
    g = gw[name]
    if name in ('w_branch_att', 'w_branch_rnn', 'w_out', 'w_ple_gate'):
        return g.reshape(N_DEV, D_MODEL // N_DEV, D_MODEL)
    if name == 'w_ffn_in':
        return g
    if name == 'w_ffn_out':
        return g.reshape(N_DEV, N_FF * FF_SH // N_DEV, D_MODEL)
    return g.reshape(g.shape[0], N_DEV, BLK).transpose(1, 0, 2)


def kernel(x, p, ln_in_g, ln_in_b, w_in, b_forget, conv_w, conv_b, rg_w_a, rg_b_a, rg_w_x, rg_b_x, rg_lambda, w_branch_att, w_branch_rnn, b_merge, w_out, ln_mix_g, ln_mix_b, w_ffn_in, w_ffn_out, ln_ffn_g, ln_ffn_b, w_ple, w_ple_gate, b_ple_gate, ln_ple_g, ln_ple_b, loss_target, m_ln_in_g, m_ln_in_b, m_w_in, m_b_forget, m_conv_w, m_conv_b, m_rg_w_a, m_rg_b_a, m_rg_w_x, m_rg_b_x, m_rg_lambda, m_w_branch_att, m_w_branch_rnn, m_b_merge, m_w_out, m_ln_mix_g, m_ln_mix_b, m_w_ffn_in, m_w_ffn_out, m_ln_ffn_g, m_ln_ffn_b, m_w_ple, m_w_ple_gate, m_b_ple_gate, m_ln_ple_g, m_ln_ple_b, v_ln_in_g, v_ln_in_b, v_w_in, v_b_forget, v_conv_w, v_conv_b, v_rg_w_a, v_rg_b_a, v_rg_w_x, v_rg_b_x, v_rg_lambda, v_w_branch_att, v_w_branch_rnn, v_b_merge, v_w_out, v_ln_mix_g, v_ln_mix_b, v_w_ffn_in, v_w_ffn_out, v_ln_ffn_g, v_ln_ffn_b, v_w_ple, v_w_ple_gate, v_b_ple_gate, v_ln_ple_g, v_ln_ple_b):
    env = dict(locals())
    wts = {n: env[n] for n in WEIGHTS}
    mom = {n: env['m_' + n] for n in WEIGHTS}
    var = {n: env['v_' + n] for n in WEIGHTS}
    bsz, seq, _ = x.shape
    depth = w_in.shape[0]
    t = bsz * seq
    x2, tgt = x.reshape(t, D_MODEL), loss_target.reshape(t, D_MODEL)
    p3 = p.reshape(depth, t, D_PLE)

    shard_names = SHARDED_BF16 + SHARDED_F32
    shards = {n: wts[n].astype(BF16) for n in SHARDED_BF16}
    shards.update({n: wts[n] for n in SHARDED_F32})
    sched = _Schedule(shards, depth)
    first = _Ride([shards[n] for n in shard_names], gather=True, index=0)
    sched.gathered[0] = dict(zip(shard_names, _exchange("gather_layer0", first)))

    def weights_of(l):
        w = _layer_weights(sched.gathered[l])
        for n in ['conv_b', 'rg_b_a', 'rg_b_x', 'rg_lambda', 'ln_mix_g', 'ln_mix_b', 'ln_ffn_g', 'ln_ffn_b',
                  'b_ple_gate', 'ln_ple_g', 'ln_ple_b']:
            w[n] = _row(wts[n][l])
        w['b_forget'] = jnp.pad(_row(b_forget[l]), ((0, 0), (0, BLK - N_HEADS)))
        w['rg_w_a'], w['rg_w_x'] = rg_w_a[l], rg_w_x[l]
        return w

    g_in, b_in = _row(ln_in_g), _row(ln_in_b)
    sq, dx, grads, dg_in, db_in = _local_step(x2, tgt, p3, weights_of, depth, g_in, b_in, bsz, seq, sched)
    loss = lax.psum(0.5 * jnp.sum(sq) / D_MODEL, ("x", "y", "c"))
    grad_x = dx.reshape(bsz, seq, D_MODEL)

    out = {}
    for n in shard_names:
        shp = wts[n].shape
        flat = lambda a: a.reshape(-1, shp[-1])
        recv = [sched.received[l][n] for l in range(depth)]
        if n in SHARDED_F32:
            recv = [jnp.stack(recv, axis=1).reshape(N_DEV, -1, shp[-1])]
        res = _adamw("adamw_" + n, recv, flat(wts[n]), flat(mom[n]), flat(var[n]))
        out[n] = [r.reshape(shp) for r in res]

    def rep_grad(n):
        if n == 'ln_in_g':
            return dg_in.reshape(-1)
        if n == 'ln_in_b':
            return db_in.reshape(-1)
        return jnp.stack([grads[l][n].reshape(wts[n].shape[1:]) if n != 'b_forget'
                          else grads[l][n][0, :N_HEADS] for l in range(depth)]).reshape(-1)

    sizes = [int(wts[n].size) for n in REPLICATED]
    n_rows = [8 * (-(-sz // (8 * BLK))) for sz in sizes]
    total_rows = -(-sum(n_rows) // (N_DEV * 8)) * (N_DEV * 8)

    def as_rows(v, sz, nr):
        v = v.reshape(-1)
        return (jnp.pad(v, (0, nr * BLK - sz)) if nr * BLK != sz else v).reshape(nr, BLK)

    def pack(vals):
        parts = [as_rows(v, sz, nr) for v, sz, nr in zip(vals, sizes, n_rows)]
        parts.append(jnp.zeros((total_rows - sum(n_rows), BLK), F32))
        return jnp.concatenate(parts, axis=0)

    gp, = _exchange("scatter_small", _Ride([pack([rep_grad(n) for n in REPLICATED]).reshape(
        N_DEV, total_rows // N_DEV, BLK)], gather=False))
    g_slice = _sum_parts("sum_small", gp)
    g_all, = _exchange("gather_small", _Ride([g_slice], gather=True))
    res = _adamw("adamw_small", [g_all.reshape(1, total_rows, BLK)],
                 *[pack([d[n] for n in REPLICATED]) for d in (wts, mom, var)])
    starts = [sum(n_rows[:i]) for i in range(len(n_rows))]
    for n, r0, sz, nr in zip(REPLICATED, starts, sizes, n_rows):
        if nr * BLK == sz:
            out[n] = [r[r0:r0 + nr].reshape(wts[n].shape) for r in res]
        else:
            out[n] = [r[r0:r0 + nr].reshape(-1)[:sz].reshape(wts[n].shape) for r in res]

    return (loss, grad_x, *[out[n][k] for k in range(4) for n in WEIGHTS])
```

```python
import functools
import math

import jax
import jax.numpy as jnp
from jax import lax
from jax.experimental import pallas as pl
from jax.experimental.pallas import tpu as pltpu

F32 = jnp.float32
BF16 = jnp.bfloat16

N_DEV = 8
D_MODEL = 1024
N_HEADS = 8
HEAD_DIM = 128
N_BLK = 8
BLK = 128
CONV_W = 4
D_PLE = 256
FF_SH = 704
N_FF = 4
IN_SH = 897
N_IN = 7176
DEPTH = 4
RG_C = 8.0
ALPHA = float((2 * DEPTH) ** 0.25)
LN_EPS = 1e-5
SCALE = 1.0 / math.sqrt(HEAD_DIM)
NEG = -1e30
ADAM_LR, ADAM_B1, ADAM_B2, ADAM_EPS, ADAM_WD, ADAM_STEP = 0.001, 0.9, 0.999, 1e-08, 0.01, 10
QKV = 3 * D_MODEL
OFF_RX, OFF_RY, OFF_GA, OFF_GB = (i * D_MODEL for i in range(4))
V7X_VMEM_LIMIT = 48 * 1024 * 1024

WEIGHTS = ['ln_in_g', 'ln_in_b', 'w_in', 'b_forget', 'conv_w', 'conv_b', 'rg_w_a', 'rg_b_a', 'rg_w_x', 'rg_b_x',
           'rg_lambda', 'w_branch_att', 'w_branch_rnn', 'b_merge', 'w_out', 'ln_mix_g', 'ln_mix_b', 'w_ffn_in',
           'w_ffn_out', 'ln_ffn_g', 'ln_ffn_b', 'w_ple', 'w_ple_gate', 'b_ple_gate', 'ln_ple_g', 'ln_ple_b']
SHARDED_BF16 = ['w_in', 'w_branch_att', 'w_branch_rnn', 'w_out', 'w_ffn_in', 'w_ffn_out', 'w_ple', 'w_ple_gate']
SHARDED_F32 = ['conv_w', 'b_merge']
REPLICATED = [n for n in WEIGHTS if n not in SHARDED_BF16 and n not in SHARDED_F32]

NN = ((1,), (0,))
NT = ((1,), (1,))
TN = ((0,), (0,))


class _Ride:
    def __init__(self, arrays, *, gather, index=None):
        self.arrays, self.gather, self.index = list(arrays), gather, index
        self.result = None

    def out_shapes(self):
        if not self.gather:
            return [jax.ShapeDtypeStruct(a.shape, a.dtype) for a in self.arrays]
        cut = 0 if self.index is None else 1
        return [jax.ShapeDtypeStruct((N_DEV,) + a.shape[cut:], a.dtype) for a in self.arrays]

    def scratch(self):
        n = len(self.arrays)
        return [pltpu.SemaphoreType.DMA((n * N_DEV,)), pltpu.SemaphoreType.DMA((n * N_DEV,)),
                pltpu.SemaphoreType.DMA((n,))]

    def _copy(self, a, k, src, dst, sems, to=None):
        send_sems, recv_sems, _ = sems
        return pltpu.make_async_remote_copy(
            src_ref=src, dst_ref=dst, send_sem=send_sems.at[a * N_DEV + k], recv_sem=recv_sems.at[a * N_DEV + k],
            device_id=_peer(k if to is None else to), device_id_type=pl.DeviceIdType.MESH)

    def begin(self, ins, outs, sems):
        me = _my_id()
        started = []
        for a in range(len(ins)):
            if self.gather:
                src = ins[a] if self.index is None else ins[a].at[self.index]
                started.append(pltpu.make_async_copy(src, outs[a].at[me], sems[2].at[a]))
                started += [self._copy(a, k, src, outs[a].at[me], sems) for k in (1, 2, 4, 6)]
            else:
                started.append(pltpu.make_async_copy(ins[a].at[me], outs[a].at[me], sems[2].at[a]))
                started += [self._copy(a, k, ins[a].at[me ^ k], outs[a].at[me], sems) for k in range(1, N_DEV)]
        for cp in started:
            cp.start()

    def finish(self, ins, outs, sems):
        me = _my_id()
        for a in range(len(ins)):
            if self.gather:
                src = ins[a] if self.index is None else ins[a].at[self.index]
                passed = []
                for k in (2, 4, 6):
                    block = outs[a].at[me ^ k]
                    self._copy(a, k, src, block, sems).wait_recv()
                    passed.append(self._copy(a, k + 1, block, block, sems, to=1))
                    passed[-1].start()
                for k in (1, 2, 4, 6):
                    self._copy(a, k, src, outs[a].at[me], sems).wait_send()
                self._copy(a, 1, src, outs[a].at[me ^ 1], sems).wait_recv()
                for cp in passed:
                    cp.wait()
                pltpu.make_async_copy(src, outs[a].at[me], sems[2].at[a]).wait()
            else:
                pltpu.make_async_copy(ins[a].at[me], outs[a].at[me], sems[2].at[a]).wait()
                for k in range(1, N_DEV):
                    self._copy(a, k, ins[a].at[me ^ k], outs[a].at[me], sems).wait()


def _pcall(body, ride=None, **kw):
    if ride is None:
        return pl.pallas_call(body, **kw)
    n = len(ride.arrays)
    grid = kw['grid']
    single = not isinstance(kw['out_shape'], (list, tuple))
    out_specs = [kw['out_specs']] if single else list(kw['out_specs'])
    out_shape = [kw['out_shape']] if single else list(kw['out_shape'])
    in_specs = list(kw['in_specs'])
    scratch = list(kw.get('scratch_shapes', ()))
    n_in, n_out, n_sc = len(in_specs), len(out_shape), len(scratch)
    hbm = pl.BlockSpec(memory_space=pltpu.HBM)

    def wrapped(*refs):
        ins, xin = refs[:n_in], refs[n_in:n_in + n]
        outs, xout = refs[n_in + n:n_in + n + n_out], refs[n_in + n + n_out:n_in + 2 * n + n_out]
        sc, sems = refs[n_in + 2 * n + n_out:n_in + 2 * n + n_out + n_sc], refs[-3:]
        ids = [pl.program_id(ax) for ax in range(len(grid))]
        first = functools.reduce(jnp.logical_and, [i == 0 for i in ids])
        last = functools.reduce(jnp.logical_and, [i == g - 1 for i, g in zip(ids, grid)])

        pl.when(first)(lambda: ride.begin(xin, xout, sems))
        body(*ins, *outs, *sc)
        pl.when(last)(lambda: ride.finish(xin, xout, sems))

    call = pl.pallas_call(wrapped, name=kw['name'], grid=grid, in_specs=in_specs + [hbm] * n,
                          out_specs=out_specs + [hbm] * n, out_shape=out_shape + ride.out_shapes(),
                          scratch_shapes=scratch + ride.scratch(), compiler_params=kw['compiler_params'])

    def run(*args):
        res = call(*args, *ride.arrays)
        ride.result = list(res[n_out:])
        return res[0] if single else list(res[:n_out])

    return run


def _tile(n, pref, mult=8):
    if n <= pref:
        return n
    t = (pref // mult) * mult
    while t >= mult:
        if n % t == 0:
            return t
        t -= mult
    return n


def _cparams(sem):
    return pltpu.CompilerParams(dimension_semantics=sem, vmem_limit_bytes=V7X_VMEM_LIMIT)


def _mm(name, a, b, *, grid, a_spec, b_spec, o_spec, out_shape, contract, ride=None):
    nk = grid[-1]
    acc_shape = tuple(d for d in o_spec.block_shape if d is not None)

    def body(a_ref, b_ref, o_ref, acc_ref):
        k = pl.program_id(len(grid) - 1)
        part = lax.dot_general(a_ref[...].astype(BF16), b_ref[...].astype(BF16), (contract, ((), ())),
                               preferred_element_type=F32)

        @pl.when(k == 0)
        def _():
            acc_ref[...] = part

        @pl.when(k > 0)
        def _():
            acc_ref[...] += part

        @pl.when(k == nk - 1)
        def _():
            o_ref[...] = acc_ref[...].astype(o_ref.dtype)

    sem = ("parallel",) * (len(grid) - 1) + ("arbitrary",)
    return _pcall(body, ride=ride, name=name, grid=grid, in_specs=[a_spec, b_spec], out_specs=o_spec,
                  out_shape=out_shape, scratch_shapes=[pltpu.VMEM(acc_shape, F32)],
                  compiler_params=_cparams(sem))(a, b)


def _mm_nn(name, a, b, *, b_off=0, n=None, out_dtype=F32, tm=1024, tn=1024, tk=1024, ride=None):
    m, k = a.shape
    n = b.shape[1] if n is None else n
    tm, tn, tk = _tile(m, tm), _tile(n, tn, 128), _tile(k, tk, 128)
    no = b_off // tn
    return _mm(name, a, b, grid=(m // tm, n // tn, k // tk),
               a_spec=pl.BlockSpec((tm, tk), lambda i, j, kk: (i, kk)),
               b_spec=pl.BlockSpec((tk, tn), lambda i, j, kk: (kk, j + no)),
               o_spec=pl.BlockSpec((tm, tn), lambda i, j, kk: (i, j)),
               out_shape=jax.ShapeDtypeStruct((m, n), out_dtype), contract=NN, ride=ride)


def _mm_nt(name, a, b, *, out_dtype=F32, tm=1024, tn=1024, tk=1024, ride=None):
    m, k = a.shape
    n = b.shape[0]
    tm, tn, tk = _tile(m, tm), _tile(n, tn, 128), _tile(k, tk, 128)
    return _mm(name, a, b, grid=(m // tm, n // tn, k // tk),
               a_spec=pl.BlockSpec((tm, tk), lambda i, j, kk: (i, kk)),
               b_spec=pl.BlockSpec((tn, tk), lambda i, j, kk: (j, kk)),
               o_spec=pl.BlockSpec((tm, tn), lambda i, j, kk: (i, j)),
               out_shape=jax.ShapeDtypeStruct((m, n), out_dtype), contract=NT, ride=ride)


def _mm_tn(name, a, b, *, a_off=0, m=None, out_dtype=F32, tm=1024, tn=1024, tk=2048, ride=None):
    t, n = b.shape
    m = a.shape[1] if m is None else m
    tm, tn, tk = _tile(m, tm, 128), _tile(n, tn, 128), _tile(t, tk)
    mo = a_off // tm
    return _mm(name, a, b, grid=(m // tm, n // tn, t // tk),
               a_spec=pl.BlockSpec((tk, tm), lambda i, j, kk: (kk, i + mo)),
               b_spec=pl.BlockSpec((tk, tn), lambda i, j, kk: (kk, j)),
               o_spec=pl.BlockSpec((tm, tn), lambda i, j, kk: (i, j)),
               out_shape=jax.ShapeDtypeStruct((m, n), out_dtype), contract=TN, ride=ride)


def _rowwise(name, fn, rows, params, out_rows, out_reds, tm=256):
    rows = [r if isinstance(r, tuple) else (r, 0, r.shape[1]) for r in rows]
    t = rows[0][0].shape[0]
    tm = _tile(t, tm)
    in_specs = []
    for _, off, w in rows:
        in_specs.append(pl.BlockSpec((tm, w), functools.partial(lambda i, cb: (i, cb), cb=off // w)))
    for p in params:
        in_specs.append(pl.BlockSpec((1, p.shape[1]), lambda i: (0, 0)))
    out_specs = [pl.BlockSpec((tm, w), lambda i: (i, 0)) for w, _ in out_rows]
    out_specs += [pl.BlockSpec((1, w), lambda i: (0, 0)) for w in out_reds]
    out_shape = [jax.ShapeDtypeStruct((t, w), dt) for w, dt in out_rows]
    out_shape += [jax.ShapeDtypeStruct((1, w), F32) for w in out_reds]
    nr, npar, nor = len(rows), len(params), len(out_rows)

    def body(*refs):
        ins, outs = refs[:nr + npar], refs[nr + npar:]
        vals = [r[...].astype(F32) for r in ins[:nr]]
        vals += [jnp.broadcast_to(r[...], (tm, r.shape[1])) for r in ins[nr:]]
        res = fn(*vals)
        step = pl.program_id(0)
        for o, v in zip(outs[:nor], res[:nor]):
            o[...] = v.astype(o.dtype)
        for o, v in zip(outs[nor:], res[nor:]):
            _accumulate(o, v, step)

    res = _pcall(body, name=name, grid=(t // tm,), in_specs=in_specs, out_specs=out_specs, out_shape=out_shape,
                 compiler_params=_cparams(("arbitrary",)))(*[r[0] for r in rows], *params)
    return res


def _accumulate(o_ref, v, step):
    @pl.when(step == 0)
    def _():
        o_ref[...] = v

    @pl.when(step > 0)
    def _():
        o_ref[...] += v


def _colsum(v):
    return jnp.sum(v, axis=0, keepdims=True)


def _vjp_rowwise(name, fn, rows, params, cots, n_row_grads, tm=256, dtypes=None):
    nr, npar, nc = len(rows), len(params), len(cots)

    def bwd(*vals):
        prim, par, ct = vals[:nr], vals[nr + nc:], vals[nr:nr + nc]
        _, pull = jax.vjp(fn, *prim, *par)
        grads = pull(tuple(ct) if nc > 1 else ct[0])
        return tuple(grads[:n_row_grads]) + tuple(_colsum(g) for g in grads[nr:])

    dtypes = [F32] * n_row_grads if dtypes is None else dtypes
    widths = [(r[2] if isinstance(r, tuple) else r.shape[1], dt) for r, dt in zip(rows[:n_row_grads], dtypes)]
    return _rowwise(name, bwd, list(rows) + list(cots), params, widths, [p.shape[1] for p in params], tm=tm)


def _ln(s, g, b):
    mu = jnp.mean(s, axis=-1, keepdims=True)
    var = jnp.mean(jnp.square(s - mu), axis=-1, keepdims=True)
    return (s - mu) * lax.rsqrt(var + LN_EPS) * g + b


def _softplus(x):
    return jnp.maximum(x, 0.0) + jnp.log1p(jnp.exp(-jnp.abs(x)))


def _expm1(x):
    series = x * (1.0 + x * (1.0 / 2 + x * (1.0 / 6 + x * (1.0 / 24 + x * (1.0 / 120 + x * (1.0 / 720))))))
    return jnp.where(jnp.abs(x) < 0.25, series, jnp.exp(x) - 1.0)


def _f_resid_ln(h, branch, g, b):
    return _ln(ALPHA * h + branch, g, b)


def _f_ple(h, gp, pe, bpg, g, b):
    return _ln(ALPHA * h + jax.nn.sigmoid(gp + bpg) * pe, g, b)


def _f_merge(ga, gb, ya, yb, bm0, bm1):
    return jax.nn.sigmoid(ga + bm0) * ya + jax.nn.sigmoid(gb + bm1) * yb


def _f_rnn_out(hs, ry):
    return hs * jax.nn.gelu(ry, approximate=True)


def _f_logf(fl, bf):
    return -_softplus(-(fl + bf))


def _f_gate(xc, ra, ia, lam, ba, bx):
    r = jax.nn.sigmoid(ra + ba)
    i = jax.nn.sigmoid(ia + bx)
    log_a = -RG_C * _softplus(-lam) * r
    a = jnp.exp(log_a)
    mult = jnp.sqrt(-_expm1(2.0 * log_a))
    return a, mult * (i * xc)


def _f_act(hg, hu):
    return jax.nn.silu(hg) * hu


ATT_BLOCK = 512


def _scores(q, k, cq, ck, diagonal):
    s = lax.dot_general(q, k, (NT, ((), ())), preferred_element_type=F32) * SCALE
    s = s + cq - ck
    if diagonal:
        row = lax.broadcasted_iota(jnp.int32, s.shape, 0)
        col = lax.broadcasted_iota(jnp.int32, s.shape, 1)
        s = jnp.where(col <= row, s, NEG)
    return s


def _dscores(p, do, o, v):
    dob = do.astype(BF16)
    delta = jnp.sum(dob.astype(F32) * o, axis=1, keepdims=True)
    dp = lax.dot_general(dob, v.astype(BF16), (NT, ((), ())), preferred_element_type=F32)
    return p * (dp - delta)


def _attn_fwd(z, cq, ck, bsz, seq, ride=None):
    t = bsz * seq
    tq = _tile(seq, ATT_BLOCK)
    nq = seq // tq

    def body(q_ref, k_ref, v_ref, cq_ref, ck_ref, o_ref, ob_ref, lse_ref):
        for i in range(nq):
            rows = slice(i * tq, (i + 1) * tq)
            q = q_ref[rows, :].astype(BF16)
            cqi = cq_ref[rows, :]

            def step(j, carry, diagonal, q=q, cqi=cqi):
                m, l, acc = carry
                keys = pl.ds(pl.multiple_of(j * tq, tq), tq)
                s = _scores(q, k_ref[keys, :].astype(BF16), cqi, ck_ref[pl.ds(j, 1), :], diagonal)
                m_new = jnp.maximum(m, jnp.max(s, axis=1, keepdims=True))
                alpha = jnp.exp(m - m_new)
                p = jnp.exp(s - m_new)
                p_hi = p.astype(BF16)
                p_lo = (p - p_hi.astype(F32)).astype(BF16)
                vb = v_ref[keys, :].astype(BF16)
                pv = lax.dot_general(p_hi, vb, (NN, ((), ())), preferred_element_type=F32)
                pv = pv + lax.dot_general(p_lo, vb, (NN, ((), ())), preferred_element_type=F32)
                return m_new, alpha * l + jnp.sum(p, axis=1, keepdims=True), alpha * acc + pv

            carry = (jnp.full((tq, 1), NEG, F32), jnp.zeros((tq, 1), F32), jnp.zeros((tq, HEAD_DIM), F32))
            if i > 0:
                carry = lax.fori_loop(0, i, functools.partial(step, diagonal=False), carry)
            m, l, acc = step(i, carry, True)
            o = acc / l
            o_ref[rows, :] = o
            ob_ref[rows, :] = o.astype(BF16)
            lse_ref[rows, :] = m + jnp.log(l)

    head = (seq, HEAD_DIM)
    in_specs = [
        pl.BlockSpec(head, lambda b, h: (b, h)),
        pl.BlockSpec(head, lambda b, h: (b, N_HEADS + h)),
        pl.BlockSpec(head, lambda b, h: (b, 2 * N_HEADS + h)),
        pl.BlockSpec((None, None, seq, 1), lambda b, h: (b, h, 0, 0)),
        pl.BlockSpec((None, None, nq, tq), lambda b, h: (b, h, 0, 0)),
    ]
    out_specs = [pl.BlockSpec(head, lambda b, h: (b, h)), pl.BlockSpec(head, lambda b, h: (b, h)),
                 pl.BlockSpec((None, None, seq, 1), lambda b, h: (b, h, 0, 0))]
    out_shape = [jax.ShapeDtypeStruct((t, D_MODEL), F32), jax.ShapeDtypeStruct((t, D_MODEL), BF16),
                 jax.ShapeDtypeStruct((bsz, N_HEADS, seq, 1), F32)]
    return _pcall(body, ride=ride, name="attn_fwd", grid=(bsz, N_HEADS), in_specs=in_specs, out_specs=out_specs,
                  out_shape=out_shape, compiler_params=_cparams(("parallel", "parallel")))(
                      z, z, z, cq, ck.reshape(bsz, N_HEADS, nq, tq))


def _attn_bwd(z, att, datt, lse, cq, ck, bsz, seq, ride=None):
    t = bsz * seq
    tq = _tile(seq, ATT_BLOCK)
    nq = seq // tq

    def body(q_ref, k_ref, v_ref, o_ref, do_ref, lse_ref, cq_ref, ck_ref,
             dq_ref, dk_ref, dv_ref, dcq_ref, dck_ref, dq_sc):
        dq_sc[...] = jnp.zeros_like(dq_sc)
        dcq_ref[...] = jnp.zeros_like(dcq_ref)
        for j in range(nq):
            keys = slice(j * tq, (j + 1) * tq)
            kb = k_ref[keys, :].astype(BF16)
            vb = v_ref[keys, :].astype(BF16)
            ckj = ck_ref[j:j + 1, :]

            def step(i, carry, diagonal, kb=kb, vb=vb, ckj=ckj):
                dk, dv, dc = carry
                rows = pl.ds(pl.multiple_of(i * tq, tq), tq)
                qb = q_ref[rows, :].astype(BF16)
                do = do_ref[rows, :]
                s = _scores(qb, kb, cq_ref[rows, :], ckj, diagonal)
                p = jnp.exp(s - lse_ref[rows, :])
                ds = _dscores(p, do, o_ref[rows, :], vb)
                dsb = (ds * SCALE).astype(BF16)
                dq_sc[rows, :] += lax.dot_general(dsb, kb, (NN, ((), ())), preferred_element_type=F32)
                dcq_ref[rows, :] += jnp.sum(ds, axis=1, keepdims=True)
                dv = dv + lax.dot_general(p.astype(BF16), do.astype(BF16), (TN, ((), ())),
                                          preferred_element_type=F32)
                dk = dk + lax.dot_general(dsb, qb, (TN, ((), ())), preferred_element_type=F32)
                return dk, dv, dc - jnp.sum(ds, axis=0, keepdims=True)

            zero = jnp.zeros((tq, HEAD_DIM), F32)
            carry = step(j, (zero, zero, jnp.zeros((1, tq), F32)), True)
            if j + 1 < nq:
                carry = lax.fori_loop(j + 1, nq, functools.partial(step, diagonal=False), carry)
            dk, dv, dck_ref[j:j + 1, :] = carry
            dk_ref[keys, :] = dk.astype(BF16)
            dv_ref[keys, :] = dv.astype(BF16)
        dq_ref[...] = dq_sc[...].astype(BF16)

    head = (seq, HEAD_DIM)
    hmap = lambda b, h: (b, h)
    col = pl.BlockSpec((None, None, seq, 1), lambda b, h: (b, h, 0, 0))
    row = pl.BlockSpec((None, None, nq, tq), lambda b, h: (b, h, 0, 0))
    in_specs = [pl.BlockSpec(head, hmap),
                pl.BlockSpec(head, lambda b, h: (b, N_HEADS + h)),
                pl.BlockSpec(head, lambda b, h: (b, 2 * N_HEADS + h)),
                pl.BlockSpec(head, hmap), pl.BlockSpec(head, hmap), col, col, row]
    big = jax.ShapeDtypeStruct((t, D_MODEL), BF16)
    return _pcall(body, ride=ride, name="attn_bwd", grid=(bsz, N_HEADS), in_specs=in_specs,
                  out_specs=[pl.BlockSpec(head, hmap)] * 3 + [col, row],
                  out_shape=[big, big, big, jax.ShapeDtypeStruct((bsz, N_HEADS, seq, 1), F32),
                             jax.ShapeDtypeStruct((bsz, N_HEADS, nq, tq), F32)],
                  scratch_shapes=[pltpu.VMEM(head, F32)],
                  compiler_params=_cparams(("parallel", "parallel")))(
                      z, z, z, att, datt, lse, cq, ck.reshape(bsz, N_HEADS, nq, tq))


def _scan(name, a, u, bsz, seq, *, reverse, with_prev=False, tb=256):
    c = u.shape[1]
    tb = _tile(seq, tb)
    nb = seq // tb
    has_a = a is not None

    def body(*refs):
        if has_a:
            a_ref, u_ref = refs[0], refs[1]
            rest = refs[2:]
        else:
            u_ref = refs[0]
            rest = refs[1:]
        outs = rest[:2] if with_prev else rest[:1]
        carry_sc, afirst_sc = rest[-2], rest[-1]
        step = pl.program_id(1)

        @pl.when(step == 0)
        def _():
            carry_sc[...] = jnp.zeros_like(carry_sc)
            afirst_sc[...] = jnp.zeros_like(afirst_sc)

        row = lax.broadcasted_iota(jnp.int32, (tb, c), 0)
        uu = u_ref[...]
        if has_a:
            aa = a_ref[...]
            if reverse:
                coef = jnp.where(row < tb - 1, pltpu.roll(aa, tb - 1, 0), afirst_sc[...])
            else:
                coef = aa
        k = 1
        while k < tb:
            shift = tb - k if reverse else k
            keep = (row < tb - k) if reverse else (row >= k)
            uu_sh = jnp.where(keep, pltpu.roll(uu, shift, 0), 0.0)
            if has_a:
                uu = coef * uu_sh + uu
                coef = coef * jnp.where(keep, pltpu.roll(coef, shift, 0), 1.0)
            else:
                uu = uu + uu_sh
            k *= 2
        carry = carry_sc[...]
        h = uu + coef * carry if has_a else uu + carry
        outs[0][...] = h
        if with_prev:
            outs[1][...] = jnp.where(row >= 1, pltpu.roll(h, 1, 0), carry)
        if reverse:
            carry_sc[...] = outs[0][0:1, :]
            if has_a:
                afirst_sc[...] = a_ref[0:1, :]
        else:
            carry_sc[...] = outs[0][tb - 1:tb, :]

    if reverse:
        imap = lambda b, s: (b * nb + nb - 1 - s, 0)
    else:
        imap = lambda b, s: (b * nb + s, 0)
    spec = pl.BlockSpec((tb, c), imap)
    n_in = 2 if has_a else 1
    n_out = 2 if with_prev else 1
    res = _pcall(body, name=name, grid=(bsz, nb), in_specs=[spec] * n_in, out_specs=[spec] * n_out,
                 out_shape=[jax.ShapeDtypeStruct(u.shape, F32)] * n_out,
                 scratch_shapes=[pltpu.VMEM((1, c), F32), pltpu.VMEM((1, c), F32)],
                 compiler_params=_cparams(("parallel", "arbitrary")))(*([a, u] if has_a else [u]))
    return res if with_prev else res[0]


def _conv_fwd(z, w, b, bsz, seq, tb=256):
    c = D_MODEL
    t = bsz * seq
    tb = _tile(seq, tb)
    nb = seq // tb

    def body(x_ref, w_ref, b_ref, o_ref, tail_sc):
        step = pl.program_id(1)

        @pl.when(step == 0)
        def _():
            tail_sc[...] = jnp.zeros_like(tail_sc)

        x = x_ref[...]
        row8 = lax.broadcasted_iota(jnp.int32, (8, c), 0)
        tail = tail_sc[...]
        acc = w_ref[CONV_W - 1:CONV_W, :] * x + b_ref[...]
        for sh in range(1, CONV_W):
            xs = pltpu.roll(x, sh, 0)
            top = jnp.where(row8 < sh, pltpu.roll(tail, sh, 0), xs[0:8, :])
            xs = jnp.concatenate([top, xs[8:, :]], axis=0) if tb > 8 else top
            acc = acc + w_ref[CONV_W - 1 - sh:CONV_W - sh, :] * xs
        o_ref[...] = acc
        tail_sc[...] = x_ref[tb - 8:tb, :]

    return _pcall(body, name="conv_fwd", grid=(bsz, nb),
                  in_specs=[pl.BlockSpec((tb, c), lambda bb, s: (bb * nb + s, OFF_RX // c)),
                            pl.BlockSpec((CONV_W, c), lambda bb, s: (0, 0)),
                            pl.BlockSpec((1, c), lambda bb, s: (0, 0))],
                  out_specs=pl.BlockSpec((tb, c), lambda bb, s: (bb * nb + s, 0)),
                  out_shape=jax.ShapeDtypeStruct((t, c), F32),
                  scratch_shapes=[pltpu.VMEM((8, c), F32)],
                  compiler_params=_cparams(("parallel", "arbitrary")))(z, w, b)


def _conv_bwd(z, dxc, w, bsz, seq, tb=256):
    c = D_MODEL
    t = bsz * seq
    tb = _tile(seq, tb)
    nb = seq // tb

    def body(x_ref, g_ref, w_ref, dx_ref, dw_ref, db_ref, head_sc):
        bb, step = pl.program_id(0), pl.program_id(1)

        @pl.when(step == 0)
        def _():
            head_sc[...] = jnp.zeros_like(head_sc)

        x, g = x_ref[...], g_ref[...]
        row8 = lax.broadcasted_iota(jnp.int32, (8, c), 0)
        head = head_sc[...]
        dx = w_ref[CONV_W - 1:CONV_W, :] * g
        dws = [None] * CONV_W
        dws[CONV_W - 1] = _colsum(g * x)
        for sh in range(1, CONV_W):
            gs = pltpu.roll(g, tb - sh, 0)
            bot = jnp.where(row8 >= 8 - sh, pltpu.roll(head, 8 - sh, 0), gs[tb - 8:tb, :])
            gs = jnp.concatenate([gs[:tb - 8, :], bot], axis=0) if tb > 8 else bot
            dx = dx + w_ref[CONV_W - 1 - sh:CONV_W - sh, :] * gs
            dws[CONV_W - 1 - sh] = _colsum(gs * x)
        dx_ref[...] = dx.astype(dx_ref.dtype)
        first = (bb == 0) & (step == 0)
        dw = jnp.concatenate(dws, axis=0)
        db = _colsum(g)

        @pl.when(first)
        def _():
            dw_ref[...] = dw
            db_ref[...] = db

        @pl.when(jnp.logical_not(first))
        def _():
            dw_ref[...] += dw
            db_ref[...] += db

        head_sc[...] = g_ref[0:8, :]

    rmap = lambda bb, s: (bb * nb + nb - 1 - s, 0)
    return _pcall(body, name="conv_bwd", grid=(bsz, nb),
                  in_specs=[pl.BlockSpec((tb, c), lambda bb, s: (bb * nb + nb - 1 - s, OFF_RX // c)),
                            pl.BlockSpec((tb, c), rmap),
                            pl.BlockSpec((CONV_W, c), lambda bb, s: (0, 0))],
                  out_specs=[pl.BlockSpec((tb, c), rmap),
                             pl.BlockSpec((CONV_W, c), lambda bb, s: (0, 0)),
                             pl.BlockSpec((1, c), lambda bb, s: (0, 0))],
                  out_shape=[jax.ShapeDtypeStruct((t, c), BF16), jax.ShapeDtypeStruct((CONV_W, c), F32),
                             jax.ShapeDtypeStruct((1, c), F32)],
                  scratch_shapes=[pltpu.VMEM((8, c), F32)],
                  compiler_params=_cparams(("arbitrary", "arbitrary")))(z, dxc, w)


def _gate_fwd(xc, w_a, w_x, b_a, b_x, lam, tm=512):
    t = xc.shape[0]
    tm = _tile(t, tm)

    def body(xc_ref, wa_ref, wx_ref, ba_ref, bx_ref, lam_ref, a_ref, u_ref):
        xc_b = xc_ref[...]
        xb = xc_b.astype(BF16)
        ra = lax.dot_general(xb, wa_ref[...].astype(BF16), (NN, ((), ())), preferred_element_type=F32)
        ia = lax.dot_general(xb, wx_ref[...].astype(BF16), (NN, ((), ())), preferred_element_type=F32)
        a, u = _f_gate(xc_b, ra, ia, lam_ref[...], ba_ref[...], bx_ref[...])
        a_ref[...] = a
        u_ref[...] = u

    row = pl.BlockSpec((tm, BLK), lambda n, i: (i, n))
    wsp = pl.BlockSpec((None, BLK, BLK), lambda n, i: (n, 0, 0))
    vec = pl.BlockSpec((1, BLK), lambda n, i: (0, n))
    return _pcall(body, name="gate_fwd", grid=(N_BLK, t // tm), in_specs=[row, wsp, wsp, vec, vec, vec],
                  out_specs=[row, row], out_shape=[jax.ShapeDtypeStruct((t, D_MODEL), F32)] * 2,
                  compiler_params=_cparams(("parallel", "parallel")))(xc, w_a, w_x, b_a, b_x, lam)


def _gate_bwd(xc, w_a, w_x, b_a, b_x, lam, da, du, tm=512, ride=None):
    t = xc.shape[0]
    tm = _tile(t, tm)

    def body(xc_ref, wa_ref, wx_ref, ba_ref, bx_ref, lam_ref, da_ref, du_ref,
             dxc_ref, dwa_ref, dwx_ref, dba_ref, dbx_ref, dlam_ref):
        step = pl.program_id(1)
        xc_b = xc_ref[...]
        xb = xc_b.astype(BF16)
        wa, wx = wa_ref[...].astype(BF16), wx_ref[...].astype(BF16)
        ra = lax.dot_general(xb, wa, (NN, ((), ())), preferred_element_type=F32)
        ia = lax.dot_general(xb, wx, (NN, ((), ())), preferred_element_type=F32)
        full = lambda r: jnp.broadcast_to(r[...], (tm, BLK))
        _, pull = jax.vjp(_f_gate, xc_b, ra, ia, full(lam_ref), full(ba_ref), full(bx_ref))
        dxc, dra, dia, dlam, dba, dbx = pull((da_ref[...], du_ref[...]))
        drb, dib = dra.astype(BF16), dia.astype(BF16)
        dxc = dxc + lax.dot_general(drb, wa, (NT, ((), ())), preferred_element_type=F32)
        dxc = dxc + lax.dot_general(dib, wx, (NT, ((), ())), preferred_element_type=F32)
        dxc_ref[...] = dxc
        _accumulate(dwa_ref, lax.dot_general(xb, drb, (TN, ((), ())), preferred_element_type=F32), step)
        _accumulate(dwx_ref, lax.dot_general(xb, dib, (TN, ((), ())), preferred_element_type=F32), step)
        _accumulate(dba_ref, _colsum(dba), step)
        _accumulate(dbx_ref, _colsum(dbx), step)
        _accumulate(dlam_ref, _colsum(dlam), step)

    row = pl.BlockSpec((tm, BLK), lambda n, i: (i, n))
    wsp = pl.BlockSpec((None, BLK, BLK), lambda n, i: (n, 0, 0))
    vec = pl.BlockSpec((1, BLK), lambda n, i: (0, n))
    wshape = jax.ShapeDtypeStruct((N_BLK, BLK, BLK), F32)
    vshape = jax.ShapeDtypeStruct((1, D_MODEL), F32)
    return _pcall(body, ride=ride, name="gate_bwd", grid=(N_BLK, t // tm),
                  in_specs=[row, wsp, wsp, vec, vec, vec, row, row],
                  out_specs=[row, wsp, wsp, vec, vec, vec],
                  out_shape=[jax.ShapeDtypeStruct((t, D_MODEL), F32), wshape, wshape, vshape, vshape, vshape],
                  compiler_params=_cparams(("parallel", "arbitrary")))(xc, w_a, w_x, b_a, b_x, lam, da, du)


def _act_fwd(hgu, tm=512):
    _, t, w = hgu.shape
    tm = _tile(t, tm)

    def body(hg_ref, hu_ref, o_ref):
        o_ref[...] = _f_act(hg_ref[...], hu_ref[...]).astype(o_ref.dtype)

    spec = lambda off: pl.BlockSpec((None, tm, w), lambda s, i: (s + off, i, 0))
    return _pcall(body, name="act_fwd", grid=(N_FF, t // tm), in_specs=[spec(0), spec(N_FF)], out_specs=spec(0),
                  out_shape=jax.ShapeDtypeStruct((N_FF, t, w), BF16),
                  compiler_params=_cparams(("parallel", "parallel")))(hgu, hgu)


def _act_bwd(hgu, dact, tm=512, ride=None):
    _, t, w = hgu.shape
    tm = _tile(t, tm)

    def body(hg_ref, hu_ref, d_ref, o_ref):
        _, pull = jax.vjp(_f_act, hg_ref[...], hu_ref[...])
        dhg, dhu = pull(d_ref[...])
        o_ref[0] = dhg.astype(o_ref.dtype)
        o_ref[1] = dhu.astype(o_ref.dtype)

    spec = lambda off: pl.BlockSpec((None, tm, w), lambda s, i: (s + off, i, 0))
    res = _pcall(body, ride=ride, name="act_bwd", grid=(N_FF, t // tm), in_specs=[spec(0), spec(N_FF), spec(0)],
                 out_specs=pl.BlockSpec((2, None, tm, w), lambda s, i: (0, s, i, 0)),
                 out_shape=jax.ShapeDtypeStruct((2, N_FF, t, w), BF16),
                 compiler_params=_cparams(("parallel", "parallel")))(hgu, hgu, dact)
    return res.reshape(2 * N_FF, t, w)


def _adamw(name, parts, w, m, v, tr=128):
    ng = len(parts)
    n_src, r, c = parts[0].shape
    tr = _tile(r, tr)
    nb = r // tr
    bc1 = 1.0 - ADAM_B1 ** ADAM_STEP
    bc2 = 1.0 - ADAM_B2 ** ADAM_STEP

    def body(*refs):
        p_refs = refs[:ng]
        w_ref, m_ref, v_ref, g_ref, d_ref, nm_ref, nv_ref = refs[ng:]
        grp = pl.program_id(0)

        def update(p_ref):
            g = p_ref[0].astype(F32)
            for s in range(1, n_src):
                g = g + p_ref[s].astype(F32)
            nm = ADAM_B1 * m_ref[...] + (1.0 - ADAM_B1) * g
            nv = ADAM_B2 * v_ref[...] + (1.0 - ADAM_B2) * jnp.square(g)
            g_ref[...] = g
            nm_ref[...] = nm
            nv_ref[...] = nv
            d_ref[...] = -ADAM_LR * ((nm / bc1) / (jnp.sqrt(nv / bc2) + ADAM_EPS) + ADAM_WD * w_ref[...])

        for k in range(ng):
            pl.when(grp == k)(functools.partial(update, p_refs[k]))

    p_specs = [pl.BlockSpec((n_src, tr, c), functools.partial(lambda gi, i, k: (0, jnp.where(gi == k, i, 0), 0), k=k))
               for k in range(ng)]
    spec = pl.BlockSpec((tr, c), lambda gi, i: (gi * nb + i, 0))
    return _pcall(body, name=name, grid=(ng, nb), in_specs=p_specs + [spec, spec, spec],
                  out_specs=[spec] * 4, out_shape=[jax.ShapeDtypeStruct((ng * r, c), F32)] * 4,
                  compiler_params=_cparams(("parallel", "parallel")))(*parts, w, m, v)


def _sum_parts(name, parts, tr=256):
    _, r, c = parts.shape
    tr = _tile(r, tr)

    def body(p_ref, o_ref):
        g = p_ref[0]
        for s in range(1, parts.shape[0]):
            g = g + p_ref[s]
        o_ref[...] = g

    return _pcall(body, name=name, grid=(r // tr,),
                  in_specs=[pl.BlockSpec((parts.shape[0], tr, c), lambda i: (0, i, 0))],
                  out_specs=pl.BlockSpec((tr, c), lambda i: (i, 0)),
                  out_shape=jax.ShapeDtypeStruct((r, c), F32), compiler_params=_cparams(("parallel",)))(parts)


def _peer(k):
    x, y, c = lax.axis_index("x"), lax.axis_index("y"), lax.axis_index("c")
    return (x ^ ((k >> 2) & 1), y ^ ((k >> 1) & 1), c ^ (k & 1))


def _my_id():
    return 4 * lax.axis_index("x") + 2 * lax.axis_index("y") + lax.axis_index("c")


def _exchange(name, ride):
    n = len(ride.arrays)

    def body(*refs):
        ride.begin(refs[:n], refs[n:2 * n], refs[2 * n:])
        ride.finish(refs[:n], refs[n:2 * n], refs[2 * n:])

    hbm = pl.BlockSpec(memory_space=pltpu.HBM)
    return _pcall(body, name=name, in_specs=[hbm] * n, out_specs=[hbm] * n, out_shape=ride.out_shapes(),
                  scratch_shapes=ride.scratch())(*ride.arrays)


def _row(v):
    return v.reshape(1, -1)


def _time_major_heads(c, bsz, seq):
    return c.reshape(bsz, seq, BLK)[:, :, :N_HEADS].transpose(0, 2, 1)


def _no_ride(*_):
    return None


TWICE = [(D_MODEL, F32), (D_MODEL, BF16)]


def _both(fn):
    def run(*v):
        y = fn(*v)
        return y, y
    return run


def _layer_fwd(h, hb, p_l, w, bsz, seq, ride_of=_no_ride):
    t = bsz * seq
    zq = _mm_nn("z_proj_qkv", hb, w['w_in7'], n=QKV, out_dtype=BF16, ride=ride_of('z_proj_qkv'))
    zr = _mm_nn("z_proj_rest", hb, w['w_in7'], b_off=QKV, n=4 * D_MODEL, ride=ride_of('z_proj_rest'))
    fl = _mm_nn("f_proj", hb, w['w_inf'])
    logf, = _rowwise("logf_fwd", lambda f, b: (_f_logf(f, b),), [fl], [w['b_forget']], [(BLK, F32)], [])
    c = _scan("cumsum_fwd", None, logf, bsz, seq, reverse=False)
    ct = _time_major_heads(c, bsz, seq)
    cq, ck = ct[..., None], ct[:, :, None, :]
    att, attb, lse = _attn_fwd(zq, cq, ck, bsz, seq, ride=ride_of('attn_fwd'))
    xc = _conv_fwd(zr, w['conv_w'], w['conv_b'], bsz, seq)
    a, u = _gate_fwd(xc, w['rg_w_a'], w['rg_w_x'], w['rg_b_a'], w['rg_b_x'], w['rg_lambda'])
    hs, hprev = _scan("lru_fwd", a, u, bsz, seq, reverse=False, with_prev=True)
    rnn, = _rowwise("rnn_out_fwd", lambda s, y: (_f_rnn_out(s, y),), [hs, (zr, OFF_RY, D_MODEL)], [],
                    [(D_MODEL, BF16)], [])
    ya = _mm_nn("branch_att", attb, w['w_branch_att'])
    yb = _mm_nn("branch_rnn", rnn, w['w_branch_rnn'])
    merged, = _rowwise("merge_fwd", lambda *v: (_f_merge(*v),),
                       [(zr, OFF_GA, D_MODEL), (zr, OFF_GB, D_MODEL), ya, yb], [w['b_merge0'], w['b_merge1']],
                       [(D_MODEL, BF16)], [])
    mix = _mm_nn("mix_out", merged, w['w_out'])
    h1, h1b = _rowwise("ln_mix_fwd", _both(_f_resid_ln), [h, mix], [w['ln_mix_g'], w['ln_mix_b']], TWICE, [])
    tm = _tile(t, 1024)
    hgu = _mm("ffn_in", h1b, w['w_ffn_in'], grid=(t // tm, 2 * N_FF, 1),
              a_spec=pl.BlockSpec((tm, D_MODEL), lambda i, s, k: (i, 0)),
              b_spec=pl.BlockSpec((None, D_MODEL, FF_SH), lambda i, s, k: (s, 0, 0)),
              o_spec=pl.BlockSpec((None, tm, FF_SH), lambda i, s, k: (s, i, 0)),
              out_shape=jax.ShapeDtypeStruct((2 * N_FF, t, FF_SH), F32), contract=NN, ride=ride_of('ffn_in'))
    act = _act_fwd(hgu)
    ffn = _mm("ffn_out", act, w['w_ffn_out'], grid=(t // tm, 1, N_FF),
              a_spec=pl.BlockSpec((None, tm, FF_SH), lambda i, j, s: (s, i, 0)),
              b_spec=pl.BlockSpec((None, FF_SH, D_MODEL), lambda i, j, s: (s, 0, 0)),
              o_spec=pl.BlockSpec((tm, D_MODEL), lambda i, j, s: (i, 0)),
              out_shape=jax.ShapeDtypeStruct((t, D_MODEL), F32), contract=NN, ride=ride_of('ffn_out'))
    h2, h2b = _rowwise("ln_ffn_fwd", _both(_f_resid_ln), [h1, ffn], [w['ln_ffn_g'], w['ln_ffn_b']], TWICE, [])
    gp = _mm_nn("ple_gate", h2b, w['w_ple_gate'])
    pe = _mm_nn("ple_proj", p_l, w['w_ple'])
    h3, h3b = _rowwise("ln_ple_fwd", _both(_f_ple), [h2, gp, pe],
                       [w['b_ple_gate'], w['ln_ple_g'], w['ln_ple_b']], TWICE, [])
    saved = dict(h=h, hb=hb, zq=zq, zr=zr, fl=fl, cq=cq, ck=ck, att=att, attb=attb, lse=lse, xc=xc, a=a,
                 hprev=hprev, hs=hs, rnn=rnn, ya=ya, yb=yb, merged=merged, mix=mix, h1=h1, h1b=h1b, hgu=hgu,
                 act=act, ffn=ffn, h2=h2, h2b=h2b, gp=gp, pe=pe)
    return h3, h3b, saved


def _layer_bwd(dh3, p_l, w, s, bsz, seq, ride_of=_no_ride):
    t = bsz * seq
    g = {}
    dh2, dgp, dpe, g['b_ple_gate'], g['ln_ple_g'], g['ln_ple_b'] = _vjp_rowwise(
        "ln_ple_bwd", _f_ple, [s['h2'], s['gp'], s['pe']], [w['b_ple_gate'], w['ln_ple_g'], w['ln_ple_b']], [dh3], 3,
        dtypes=[F32, BF16, BF16])
    g['w_ple_gate'] = _mm_tn("ple_gate_dw", s['h2b'], dgp, out_dtype=BF16)
    g['w_ple'] = _mm_tn("ple_proj_dw", p_l, dpe, out_dtype=BF16)
    dh2b = _mm_nt("ple_gate_dx", dgp, w['w_ple_gate'])
    dh1, dffn, g['ln_ffn_g'], g['ln_ffn_b'] = _ln_resid_bwd(
        "ln_ffn_bwd", s['h1'], s['ffn'], w['ln_ffn_g'], w['ln_ffn_b'], dh2, dh2b)
    tm = _tile(t, 1024)
    dact = _mm("ffn_out_dx", dffn, w['w_ffn_out'], grid=(t // tm, N_FF, 1),
               a_spec=pl.BlockSpec((tm, D_MODEL), lambda i, ss, k: (i, 0)),
               b_spec=pl.BlockSpec((None, FF_SH, D_MODEL), lambda i, ss, k: (ss, 0, 0)),
               o_spec=pl.BlockSpec((None, tm, FF_SH), lambda i, ss, k: (ss, i, 0)),
               out_shape=jax.ShapeDtypeStruct((N_FF, t, FF_SH), F32), contract=NT)
    tk = _tile(t, 2048)
    g['w_ffn_out'] = _mm("ffn_out_dw", s['act'], dffn, grid=(N_FF, 1, t // tk),
                         a_spec=pl.BlockSpec((None, tk, FF_SH), lambda ss, j, k: (ss, k, 0)),
                         b_spec=pl.BlockSpec((tk, D_MODEL), lambda ss, j, k: (k, 0)),
                         o_spec=pl.BlockSpec((None, FF_SH, D_MODEL), lambda ss, j, k: (ss, 0, 0)),
                         out_shape=jax.ShapeDtypeStruct((N_FF, FF_SH, D_MODEL), BF16), contract=TN)
    dhgu = _act_bwd(s['hgu'], dact, ride=ride_of('act_bwd', g))
    g['w_ffn_in'] = _mm("ffn_in_dw", s['h1b'], dhgu, grid=(2 * N_FF, 1, t // tk),
                        a_spec=pl.BlockSpec((tk, D_MODEL), lambda ss, j, k: (k, 0)),
                        b_spec=pl.BlockSpec((None, tk, FF_SH), lambda ss, j, k: (ss, k, 0)),
                        o_spec=pl.BlockSpec((None, D_MODEL, FF_SH), lambda ss, j, k: (ss, 0, 0)),
                        out_shape=jax.ShapeDtypeStruct((2 * N_FF, D_MODEL, FF_SH), BF16), contract=TN,
                        ride=ride_of('ffn_in_dw', g))
    dh1b = _mm("ffn_in_dx", dhgu, w['w_ffn_in'], grid=(t // tm, 1, 2 * N_FF),
               a_spec=pl.BlockSpec((None, tm, FF_SH), lambda i, j, ss: (ss, i, 0)),
               b_spec=pl.BlockSpec((None, D_MODEL, FF_SH), lambda i, j, ss: (ss, 0, 0)),
               o_spec=pl.BlockSpec((tm, D_MODEL), lambda i, j, ss: (i, 0)),
               out_shape=jax.ShapeDtypeStruct((t, D_MODEL), F32), contract=NT, ride=ride_of('ffn_in_dx', g))
    dh, dmix, g['ln_mix_g'], g['ln_mix_b'] = _ln_resid_bwd(
        "ln_mix_bwd", s['h'], s['mix'], w['ln_mix_g'], w['ln_mix_b'], dh1, dh1b)
    g['w_out'] = _mm_tn("mix_out_dw", s['merged'], dmix, out_dtype=BF16)
    dmerged = _mm_nt("mix_out_dx", dmix, w['w_out'])
    z = s['zr']
    dga, dgb, dya, dyb, dbm0, dbm1 = _vjp_rowwise(
        "merge_bwd", _f_merge, [(z, OFF_GA, D_MODEL), (z, OFF_GB, D_MODEL), s['ya'], s['yb']],
        [w['b_merge0'], w['b_merge1']], [dmerged], 4, dtypes=[BF16] * 4)
    g['b_merge'] = jnp.concatenate([dbm0, dbm1], axis=0)
    g['w_branch_att'] = _mm_tn("branch_att_dw", s['attb'], dya, out_dtype=BF16)
    g['w_branch_rnn'] = _mm_tn("branch_rnn_dw", s['rnn'], dyb, out_dtype=BF16)
    datt = _mm_nt("branch_att_dx", dya, w['w_branch_att'], out_dtype=BF16)
    drnn = _mm_nt("branch_rnn_dx", dyb, w['w_branch_rnn'])
    dhs, dry = _vjp_rowwise("rnn_out_bwd", _f_rnn_out, [s['hs'], (z, OFF_RY, D_MODEL)], [], [drnn], 2,
                            dtypes=[F32, BF16])
    lam = _scan("lru_bwd", s['a'], dhs, bsz, seq, reverse=True)
    da, = _rowwise("lru_da", lambda l, hp: (l * hp,), [lam, s['hprev']], [], [(D_MODEL, F32)], [])
    dxc, g['rg_w_a'], g['rg_w_x'], g['rg_b_a'], g['rg_b_x'], g['rg_lambda'] = _gate_bwd(
        s['xc'], w['rg_w_a'], w['rg_w_x'], w['rg_b_a'], w['rg_b_x'], w['rg_lambda'], da, lam,
        ride=ride_of('gate_bwd', g))
    drx, g['conv_w'], g['conv_b'] = _conv_bwd(z, dxc, w['conv_w'], bsz, seq)
    dq, dk, dv, dcq, dck = _attn_bwd(s['zq'], s['att'], datt, s['lse'], s['cq'], s['ck'], bsz, seq,
                                     ride=ride_of('attn_bwd', g))
    dc = (dcq[:, :, :, 0] + dck.reshape(bsz, N_HEADS, seq)).transpose(0, 2, 1)
    dc = jnp.pad(dc, ((0, 0), (0, 0), (0, BLK - N_HEADS))).reshape(t, BLK)
    dlogf = _scan("cumsum_bwd", None, dc, bsz, seq, reverse=True)
    dfl, g['b_forget'] = _vjp_rowwise("logf_bwd", _f_logf, [s['fl']], [w['b_forget']], [dlogf], 1, dtypes=[BF16])
    dz = jnp.concatenate([dq, dk, dv, drx, dry, dga, dgb], axis=1)
    g['w_in7'] = _mm_tn("z_proj_dw", s['hb'], dz, out_dtype=BF16)
    g['w_inf'] = _mm_tn("f_proj_dw", s['hb'], dfl, out_dtype=BF16)
    dhz = _mm_nt("z_proj_dx", dz, w['w_in7'], ride=ride_of('z_proj_dx', g))
    dhf = _mm_nt("f_proj_dx", dfl, w['w_inf'])
    dh_in, = _rowwise("dh_sum", lambda x0, x1, x2: (x0 + x1 + x2,), [dh, dhz, dhf], [], [(D_MODEL, F32)], [])
    return dh_in, g


def _ln_resid_bwd(name, h, branch, gam, bet, d0, d1):
    def bwd(hv, bv, d0v, d1v, gv, btv):
        _, pull = jax.vjp(_f_resid_ln, hv, bv, gv, btv)
        dh, db, dg, dbt = pull(d0v + d1v)
        return dh, db, _colsum(dg), _colsum(dbt)

    return _rowwise(name, bwd, [h, branch, d0, d1], [gam, bet], [(D_MODEL, F32), (D_MODEL, BF16)],
                    [D_MODEL, D_MODEL])


class _Schedule:
    FWD = {'z_proj_qkv': ['w_ffn_out'], 'z_proj_rest': ['w_branch_att', 'w_branch_rnn', 'w_out', 'w_ple_gate'],
           'attn_fwd': ['w_in'], 'ffn_in': ['w_ffn_in'], 'ffn_out': ['w_ple', 'conv_w', 'b_merge']}
    BWD = {'act_bwd': ['w_ffn_out'], 'ffn_in_dw': ['w_ple_gate', 'w_ple'],
           'gate_bwd': ['w_out', 'w_branch_att', 'w_branch_rnn'],
           'attn_bwd': ['w_ffn_in', 'conv_w', 'b_merge']}

    def __init__(self, shards, depth):
        self.shards, self.depth = shards, depth
        self.gathered = [{} for _ in range(depth)]
        self.received = [{} for _ in range(depth)]
        self.pending = []
        self.deferred = None

    def gather_ride(self, layer, kernel_name):
        if layer + 1 >= self.depth:
            return None
        names = self.FWD[kernel_name]
        ride = _Ride([self.shards[n] for n in names], gather=True, index=layer + 1)
        self.pending.append((ride, names, self.gathered[layer + 1]))
        return ride

    def _scatter(self, arrays, names, layer):
        ride = _Ride(arrays, gather=False)
        self.pending.append((ride, names, self.received[layer]))
        return ride

    def scatter_ride(self, layer, kernel_name, grads):
        if kernel_name == 'z_proj_dx':
            whole = _by_destination('w_in', grads)
            half = whole.shape[1] // 2
            self.deferred = (whole[:, half:], layer)
            return self._scatter([whole[:, :half]], ['w_in_a'], layer)
        if kernel_name == 'ffn_in_dx':
            if self.deferred is None:
                return None
            (late, from_layer), self.deferred = self.deferred, None
            return self._scatter([late], ['w_in_b'], from_layer)
        names = self.BWD[kernel_name]
        return self._scatter([_by_destination(n, grads) for n in names], names, layer)

    def flush(self):
        late, from_layer = self.deferred
        self.deferred = None
        self.received[from_layer]['w_in_b'], = _exchange("scatter_last", _Ride([late], gather=False))

    def collect(self):
        for ride, names, dst in self.pending:
            dst.update(zip(names, ride.result))
        self.pending = []


def _local_step(x2, tgt, p3, weights_of, depth, g_in, b_in, bsz, seq, sched=None):
    h, hb = _rowwise("ln_in_fwd", _both(_ln), [x2], [g_in, b_in], TWICE, [])
    p3 = p3.astype(BF16)
    saved, layer_w = [], []
    for l in range(depth):
        layer_w.append(weights_of(l))
        ride_of = functools.partial(sched.gather_ride, l) if sched else _no_ride
        h, hb, s = _layer_fwd(h, hb, p3[l], layer_w[l], bsz, seq, ride_of)
        if sched:
            sched.collect()
        saved.append(s)

    def loss_fn(y, tv):
        err = y - tv
        return err * (1.0 / D_MODEL), _colsum(jnp.square(err))

    dh, sq = _rowwise("loss", loss_fn, [h, tgt], [], [(D_MODEL, F32)], [D_MODEL])
    grads = [None] * depth
    for l in reversed(range(depth)):
        ride_of = functools.partial(sched.scatter_ride, l) if sched else _no_ride
        dh, grads[l] = _layer_bwd(dh, p3[l], layer_w[l], saved[l], bsz, seq, ride_of)
        if sched:
            sched.collect()
    dx, dg_in, db_in = _vjp_rowwise("ln_in_bwd", _ln, [x2], [g_in, b_in], [dh], 1)
    return sq, dx, grads, dg_in, db_in


def _layer_weights(full):
    w = {}
    wt = full['w_in'].transpose(1, 0, 2).reshape(D_MODEL, N_IN)
    w['w_in7'] = jnp.concatenate([wt[:, :3 * D_MODEL], wt[:, 3 * D_MODEL + N_HEADS:]], axis=1)
    w['w_inf'] = jnp.pad(wt[:, 3 * D_MODEL:3 * D_MODEL + N_HEADS], ((0, 0), (0, BLK - N_HEADS)))
    for n in ['w_branch_att', 'w_branch_rnn', 'w_out', 'w_ple_gate']:
        w[n] = full[n].reshape(D_MODEL, D_MODEL)
    w['w_ffn_in'] = full['w_ffn_in']
    w['w_ffn_out'] = full['w_ffn_out'].reshape(N_FF, FF_SH, D_MODEL)
    w['w_ple'] = full['w_ple'].transpose(1, 0, 2).reshape(D_PLE, D_MODEL)
    w['conv_w'] = full['conv_w'].transpose(1, 0, 2).reshape(CONV_W, D_MODEL)
    bm = full['b_merge'].transpose(1, 0, 2).reshape(2, D_MODEL)
    w['b_merge0'], w['b_merge1'] = bm[0:1], bm[1:2]
    return w


def _by_destination(name, gw):
    if name == 'w_in':
        g7, gf = gw['w_in7'], gw['w_inf']
        true = jnp.concatenate([g7[:, :3 * D_MODEL], gf[:, :N_HEADS], g7[:, 3 * D_MODEL:]], axis=1)
        return true.reshape(D_MODEL, N_DEV, IN_SH).transpose(1, 0, 2)
    g = gw[name]
    if name in ('w_branch_att', 'w_branch_rnn', 'w_out', 'w_ple_gate'):
        return g.reshape(N_DEV, D_MODEL // N_DEV, D_MODEL)
    if name == 'w_ffn_in':
        return g
    if name == 'w_ffn_out':
        return g.reshape(N_DEV, N_FF * FF_SH // N_DEV, D_MODEL)
    return g.reshape(g.shape[0], N_DEV, BLK).transpose(1, 0, 2)


def kernel(x, p, ln_in_g, ln_in_b, w_in, b_forget, conv_w, conv_b, rg_w_a, rg_b_a, rg_w_x, rg_b_x, rg_lambda, w_branch_att, w_branch_rnn, b_merge, w_out, ln_mix_g, ln_mix_b, w_ffn_in, w_ffn_out, ln_ffn_g, ln_ffn_b, w_ple, w_ple_gate, b_ple_gate, ln_ple_g, ln_ple_b, loss_target, m_ln_in_g, m_ln_in_b, m_w_in, m_b_forget, m_conv_w, m_conv_b, m_rg_w_a, m_rg_b_a, m_rg_w_x, m_rg_b_x, m_rg_lambda, m_w_branch_att, m_w_branch_rnn, m_b_merge, m_w_out, m_ln_mix_g, m_ln_mix_b, m_w_ffn_in, m_w_ffn_out, m_ln_ffn_g, m_ln_ffn_b, m_w_ple, m_w_ple_gate, m_b_ple_gate, m_ln_ple_g, m_ln_ple_b, v_ln_in_g, v_ln_in_b, v_w_in, v_b_forget, v_conv_w, v_conv_b, v_rg_w_a, v_rg_b_a, v_rg_w_x, v_rg_b_x, v_rg_lambda, v_w_branch_att, v_w_branch_rnn, v_b_merge, v_w_out, v_ln_mix_g, v_ln_mix_b, v_w_ffn_in, v_w_ffn_out, v_ln_ffn_g, v_ln_ffn_b, v_w_ple, v_w_ple_gate, v_b_ple_gate, v_ln_ple_g, v_ln_ple_b):
    env = dict(locals())
    wts = {n: env[n] for n in WEIGHTS}
    mom = {n: env['m_' + n] for n in WEIGHTS}
    var = {n: env['v_' + n] for n in WEIGHTS}
    bsz, seq, _ = x.shape
    depth = w_in.shape[0]
    t = bsz * seq
    x2, tgt = x.reshape(t, D_MODEL), loss_target.reshape(t, D_MODEL)
    p3 = p.reshape(depth, t, D_PLE)

    shard_names = SHARDED_BF16 + SHARDED_F32
    shards = {n: wts[n].astype(BF16) for n in SHARDED_BF16}
    shards.update({n: wts[n] for n in SHARDED_F32})
    sched = _Schedule(shards, depth)
    first = _Ride([shards[n] for n in shard_names], gather=True, index=0)
    sched.gathered[0] = dict(zip(shard_names, _exchange("gather_layer0", first)))

    def weights_of(l):
        w = _layer_weights(sched.gathered[l])
        for n in ['conv_b', 'rg_b_a', 'rg_b_x', 'rg_lambda', 'ln_mix_g', 'ln_mix_b', 'ln_ffn_g', 'ln_ffn_b',
                  'b_ple_gate', 'ln_ple_g', 'ln_ple_b']:
            w[n] = _row(wts[n][l])
        w['b_forget'] = jnp.pad(_row(b_forget[l]), ((0, 0), (0, BLK - N_HEADS)))
        w['rg_w_a'], w['rg_w_x'] = rg_w_a[l], rg_w_x[l]
        return w

    g_in, b_in = _row(ln_in_g), _row(ln_in_b)
    sq, dx, grads, dg_in, db_in = _local_step(x2, tgt, p3, weights_of, depth, g_in, b_in, bsz, seq, sched)
    loss = lax.psum(0.5 * jnp.sum(sq) / D_MODEL, ("x", "y", "c"))
    grad_x = dx.reshape(bsz, seq, D_MODEL)

    sched.flush()
    out = {}
    for n in shard_names:
        shp = wts[n].shape
        flat = lambda a: a.reshape(-1, shp[-1])
        if n == 'w_in':
            recv = [sched.received[l][half] for l in range(depth) for half in ('w_in_a', 'w_in_b')]
        else:
            recv = [sched.received[l][n] for l in range(depth)]
        if n in SHARDED_F32:
            recv = [jnp.stack(recv, axis=1).reshape(N_DEV, -1, shp[-1])]
        res = _adamw("adamw_" + n, recv, flat(wts[n]), flat(mom[n]), flat(var[n]))
        out[n] = [r.reshape(shp) for r in res]

    def rep_grad(n):
        if n == 'ln_in_g':
            return dg_in.reshape(-1)
        if n == 'ln_in_b':
            return db_in.reshape(-1)
        return jnp.stack([grads[l][n].reshape(wts[n].shape[1:]) if n != 'b_forget'
                          else grads[l][n][0, :N_HEADS] for l in range(depth)]).reshape(-1)

    sizes = [int(wts[n].size) for n in REPLICATED]
    n_rows = [8 * (-(-sz // (8 * BLK))) for sz in sizes]
    total_rows = -(-sum(n_rows) // (N_DEV * 8)) * (N_DEV * 8)

    def as_rows(v, sz, nr):
        v = v.reshape(-1)
        return (jnp.pad(v, (0, nr * BLK - sz)) if nr * BLK != sz else v).reshape(nr, BLK)

    def pack(vals):
        parts = [as_rows(v, sz, nr) for v, sz, nr in zip(vals, sizes, n_rows)]
        parts.append(jnp.zeros((total_rows - sum(n_rows), BLK), F32))
        return jnp.concatenate(parts, axis=0)

    gp, = _exchange("scatter_small", _Ride([pack([rep_grad(n) for n in REPLICATED]).reshape(
        N_DEV, total_rows // N_DEV, BLK)], gather=False))
    g_slice = _sum_parts("sum_small", gp)
    g_all, = _exchange("gather_small", _Ride([g_slice], gather=True))
    res = _adamw("adamw_small", [g_all.reshape(1, total_rows, BLK)],
                 *[pack([d[n] for n in REPLICATED]) for d in (wts, mom, var)])
    starts = [sum(n_rows[:i]) for i in range(len(n_rows))]
    for n, r0, sz, nr in zip(REPLICATED, starts, sizes, n_rows):
        if nr * BLK == sz:
            out[n] = [r[r0:r0 + nr].reshape(wts[n].shape) for r in res]
        else:
            out[n] = [r[r0:r0 + nr].reshape(-1)[:sz].reshape(wts[n].shape) for r in res]

    return (loss, grad_x, *[out[n][k] for k in range(4) for n in WEIGHTS])
```

```python
import functools
import math

import jax
import jax.numpy as jnp
from jax import lax
from jax.experimental import pallas as pl
from jax.experimental.pallas import tpu as pltpu

F32 = jnp.float32
BF16 = jnp.bfloat16

N_DEV = 8
D_MODEL = 1024
N_HEADS = 8
HEAD_DIM = 128
N_BLK = 8
BLK = 128
CONV_W = 4
D_PLE = 256
FF_SH = 704
N_FF = 4
IN_SH = 897
N_IN = 7176
DEPTH = 4
RG_C = 8.0
ALPHA = float((2 * DEPTH) ** 0.25)
LN_EPS = 1e-5
SCALE = 1.0 / math.sqrt(HEAD_DIM)
NEG = -1e30
ADAM_LR, ADAM_B1, ADAM_B2, ADAM_EPS, ADAM_WD, ADAM_STEP = 0.001, 0.9, 0.999, 1e-08, 0.01, 10
QKV = 3 * D_MODEL
OFF_RX, OFF_RY, OFF_GA, OFF_GB = (i * D_MODEL for i in range(4))
V7X_VMEM_LIMIT = 48 * 1024 * 1024

WEIGHTS = ['ln_in_g', 'ln_in_b', 'w_in', 'b_forget', 'conv_w', 'conv_b', 'rg_w_a', 'rg_b_a', 'rg_w_x', 'rg_b_x',
           'rg_lambda', 'w_branch_att', 'w_branch_rnn', 'b_merge', 'w_out', 'ln_mix_g', 'ln_mix_b', 'w_ffn_in',
           'w_ffn_out', 'ln_ffn_g', 'ln_ffn_b', 'w_ple', 'w_ple_gate', 'b_ple_gate', 'ln_ple_g', 'ln_ple_b']
SHARDED_BF16 = ['w_in', 'w_branch_att', 'w_branch_rnn', 'w_out', 'w_ffn_in', 'w_ffn_out', 'w_ple', 'w_ple_gate']
SHARDED_F32 = ['conv_w', 'b_merge']
REPLICATED = [n for n in WEIGHTS if n not in SHARDED_BF16 and n not in SHARDED_F32]

NN = ((1,), (0,))
NT = ((1,), (1,))
TN = ((0,), (0,))


class _Ride:
    def __init__(self, arrays, *, gather, index=None):
        self.arrays, self.gather, self.index = list(arrays), gather, index
        self.result = None

    def out_shapes(self):
        if not self.gather:
            return [jax.ShapeDtypeStruct(a.shape, a.dtype) for a in self.arrays]
        cut = 0 if self.index is None else 1
        return [jax.ShapeDtypeStruct((N_DEV,) + a.shape[cut:], a.dtype) for a in self.arrays]

    def scratch(self):
        n = len(self.arrays)
        return [pltpu.SemaphoreType.DMA((n * N_DEV,)), pltpu.SemaphoreType.DMA((n * N_DEV,)),
                pltpu.SemaphoreType.DMA((n,))]

    def _copy(self, a, k, src, dst, sems, to=None):
        send_sems, recv_sems, _ = sems
        return pltpu.make_async_remote_copy(
            src_ref=src, dst_ref=dst, send_sem=send_sems.at[a * N_DEV + k], recv_sem=recv_sems.at[a * N_DEV + k],
            device_id=_peer(k if to is None else to), device_id_type=pl.DeviceIdType.MESH)

    def begin(self, ins, outs, sems):
        me = _my_id()
        started = []
        for a in range(len(ins)):
            if self.gather:
                src = ins[a] if self.index is None else ins[a].at[self.index]
                started.append(pltpu.make_async_copy(src, outs[a].at[me], sems[2].at[a]))
                started += [self._copy(a, k, src, outs[a].at[me], sems) for k in (1, 2, 4, 6)]
            else:
                started.append(pltpu.make_async_copy(ins[a].at[me], outs[a].at[me], sems[2].at[a]))
                started += [self._copy(a, k, ins[a].at[me ^ k], outs[a].at[me], sems) for k in range(1, N_DEV)]
        for cp in started:
            cp.start()

    def finish(self, ins, outs, sems):
        me = _my_id()
        for a in range(len(ins)):
            if self.gather:
                src = ins[a] if self.index is None else ins[a].at[self.index]
                passed = []
                for k in (2, 4, 6):
                    block = outs[a].at[me ^ k]
                    self._copy(a, k, src, block, sems).wait_recv()
                    passed.append(self._copy(a, k + 1, block, block, sems, to=1))
                    passed[-1].start()
                for k in (1, 2, 4, 6):
                    self._copy(a, k, src, outs[a].at[me], sems).wait_send()
                self._copy(a, 1, src, outs[a].at[me ^ 1], sems).wait_recv()
                for cp in passed:
                    cp.wait()
                pltpu.make_async_copy(src, outs[a].at[me], sems[2].at[a]).wait()
            else:
                pltpu.make_async_copy(ins[a].at[me], outs[a].at[me], sems[2].at[a]).wait()
                for k in range(1, N_DEV):
                    self._copy(a, k, ins[a].at[me ^ k], outs[a].at[me], sems).wait()


def _pcall(body, ride=None, **kw):
    if ride is None:
        return pl.pallas_call(body, **kw)
    n = len(ride.arrays)
    grid = kw['grid']
    single = not isinstance(kw['out_shape'], (list, tuple))
    out_specs = [kw['out_specs']] if single else list(kw['out_specs'])
    out_shape = [kw['out_shape']] if single else list(kw['out_shape'])
    in_specs = list(kw['in_specs'])
    scratch = list(kw.get('scratch_shapes', ()))
    n_in, n_out, n_sc = len(in_specs), len(out_shape), len(scratch)
    hbm = pl.BlockSpec(memory_space=pltpu.HBM)

    def wrapped(*refs):
        ins, xin = refs[:n_in], refs[n_in:n_in + n]
        outs, xout = refs[n_in + n:n_in + n + n_out], refs[n_in + n + n_out:n_in + 2 * n + n_out]
        sc, sems = refs[n_in + 2 * n + n_out:n_in + 2 * n + n_out + n_sc], refs[-3:]
        ids = [pl.program_id(ax) for ax in range(len(grid))]
        first = functools.reduce(jnp.logical_and, [i == 0 for i in ids])
        last = functools.reduce(jnp.logical_and, [i == g - 1 for i, g in zip(ids, grid)])

        pl.when(first)(lambda: ride.begin(xin, xout, sems))
        body(*ins, *outs, *sc)
        pl.when(last)(lambda: ride.finish(xin, xout, sems))

    call = pl.pallas_call(wrapped, name=kw['name'], grid=grid, in_specs=in_specs + [hbm] * n,
                          out_specs=out_specs + [hbm] * n, out_shape=out_shape + ride.out_shapes(),
                          scratch_shapes=scratch + ride.scratch(), compiler_params=kw['compiler_params'])

    def run(*args):
        res = call(*args, *ride.arrays)
        ride.result = list(res[n_out:])
        return res[0] if single else list(res[:n_out])

    return run


def _tile(n, pref, mult=8):
    if n <= pref:
        return n
    t = (pref // mult) * mult
    while t >= mult:
        if n % t == 0:
            return t
        t -= mult
    return n


def _cparams(sem):
    return pltpu.CompilerParams(dimension_semantics=sem, vmem_limit_bytes=V7X_VMEM_LIMIT)


def _mm(name, a, b, *, grid, a_spec, b_spec, o_spec, out_shape, contract, ride=None, add=None):
    nk = grid[-1]
    in_out = out_shape.dtype == F32
    acc_shape = tuple(d for d in o_spec.block_shape if d is not None)

    def body(*refs):
        a_ref, b_ref = refs[0], refs[1]
        add_ref = refs[2] if add is not None else None
        o_ref = refs[3] if add is not None else refs[2]
        acc_ref = o_ref if (in_out or nk == 1) else refs[-1]
        k = pl.program_id(len(grid) - 1)
        part = lax.dot_general(a_ref[...].astype(BF16), b_ref[...].astype(BF16), (contract, ((), ())),
                               preferred_element_type=F32)
        if add_ref is not None:
            part = jnp.where(k == 0, part + add_ref[...], part) if nk > 1 else part + add_ref[...]
        if nk == 1:
            o_ref[...] = part.astype(o_ref.dtype)
            return

        @pl.when(k == 0)
        def _():
            acc_ref[...] = part

        @pl.when(k > 0)
        def _():
            acc_ref[...] += part

        if not in_out:
            @pl.when(k == nk - 1)
            def _():
                o_ref[...] = acc_ref[...].astype(o_ref.dtype)

    sem = ("parallel",) * (len(grid) - 1) + ("arbitrary",)
    scratch = [] if (in_out or nk == 1) else [pltpu.VMEM(acc_shape, F32)]
    in_specs, args = [a_spec, b_spec], [a, b]
    if add is not None:
        in_specs.append(o_spec)
        args.append(add)
    return _pcall(body, ride=ride, name=name, grid=grid, in_specs=in_specs, out_specs=o_spec,
                  out_shape=out_shape, scratch_shapes=scratch, compiler_params=_cparams(sem))(*args)


def _mm_nn(name, a, b, *, b_off=0, n=None, out_dtype=F32, tm=1024, tn=1024, tk=1024, ride=None):
    m, k = a.shape
    n = b.shape[1] if n is None else n
    tm, tn, tk = _tile(m, tm), _tile(n, tn, 128), _tile(k, tk, 128)
    no = b_off // tn
    return _mm(name, a, b, grid=(m // tm, n // tn, k // tk),
               a_spec=pl.BlockSpec((tm, tk), lambda i, j, kk: (i, kk)),
               b_spec=pl.BlockSpec((tk, tn), lambda i, j, kk: (kk, j + no)),
               o_spec=pl.BlockSpec((tm, tn), lambda i, j, kk: (i, j)),
               out_shape=jax.ShapeDtypeStruct((m, n), out_dtype), contract=NN, ride=ride)


def _mm_nt(name, a, b, *, out_dtype=F32, tm=1024, tn=1024, tk=1024, ride=None, add=None):
    m, k = a.shape
    n = b.shape[0]
    tm, tn, tk = _tile(m, tm), _tile(n, tn, 128), _tile(k, tk, 128)
    return _mm(name, a, b, grid=(m // tm, n // tn, k // tk),
               a_spec=pl.BlockSpec((tm, tk), lambda i, j, kk: (i, kk)),
               b_spec=pl.BlockSpec((tn, tk), lambda i, j, kk: (j, kk)),
               o_spec=pl.BlockSpec((tm, tn), lambda i, j, kk: (i, j)),
               out_shape=jax.ShapeDtypeStruct((m, n), out_dtype), contract=NT, ride=ride, add=add)


def _mm_tn(name, a, b, *, a_off=0, m=None, out_dtype=F32, tm=1024, tn=1024, tk=2048, ride=None):
    t, n = b.shape
    m = a.shape[1] if m is None else m
    tm, tn, tk = _tile(m, tm, 128), _tile(n, tn, 128), _tile(t, tk)
    mo = a_off // tm
    return _mm(name, a, b, grid=(m // tm, n // tn, t // tk),
               a_spec=pl.BlockSpec((tk, tm), lambda i, j, kk: (kk, i + mo)),
               b_spec=pl.BlockSpec((tk, tn), lambda i, j, kk: (kk, j)),
               o_spec=pl.BlockSpec((tm, tn), lambda i, j, kk: (i, j)),
               out_shape=jax.ShapeDtypeStruct((m, n), out_dtype), contract=TN, ride=ride)


def _rowwise(name, fn, rows, params, out_rows, out_reds, tm=256):
    rows = [r if isinstance(r, tuple) else (r, 0, r.shape[1]) for r in rows]
    t = rows[0][0].shape[0]
    tm = _tile(t, tm)
    in_specs = []
    for _, off, w in rows:
        in_specs.append(pl.BlockSpec((tm, w), functools.partial(lambda i, cb: (i, cb), cb=off // w)))
    for p in params:
        in_specs.append(pl.BlockSpec((1, p.shape[1]), lambda i: (0, 0)))
    out_specs = [pl.BlockSpec((tm, w), lambda i: (i, 0)) for w, _ in out_rows]
    out_specs += [pl.BlockSpec((1, w), lambda i: (0, 0)) for w in out_reds]
    out_shape = [jax.ShapeDtypeStruct((t, w), dt) for w, dt in out_rows]
    out_shape += [jax.ShapeDtypeStruct((1, w), F32) for w in out_reds]
    nr, npar, nor = len(rows), len(params), len(out_rows)

    def body(*refs):
        ins, outs = refs[:nr + npar], refs[nr + npar:]
        vals = [r[...].astype(F32) for r in ins[:nr]]
        vals += [jnp.broadcast_to(r[...], (tm, r.shape[1])) for r in ins[nr:]]
        res = fn(*vals)
        step = pl.program_id(0)
        for o, v in zip(outs[:nor], res[:nor]):
            o[...] = v.astype(o.dtype)
        for o, v in zip(outs[nor:], res[nor:]):
            _accumulate(o, v, step)

    res = _pcall(body, name=name, grid=(t // tm,), in_specs=in_specs, out_specs=out_specs, out_shape=out_shape,
                 compiler_params=_cparams(("arbitrary",)))(*[r[0] for r in rows], *params)
    return res


def _accumulate(o_ref, v, step):
    @pl.when(step == 0)
    def _():
        o_ref[...] = v

    @pl.when(step > 0)
    def _():
        o_ref[...] += v


def _colsum(v):
    return jnp.sum(v, axis=0, keepdims=True)


def _vjp_rowwise(name, fn, rows, params, cots, n_row_grads, tm=256, dtypes=None):
    nr, npar, nc = len(rows), len(params), len(cots)

    def bwd(*vals):
        prim, par, ct = vals[:nr], vals[nr + nc:], vals[nr:nr + nc]
        _, pull = jax.vjp(fn, *prim, *par)
        grads = pull(tuple(ct) if nc > 1 else ct[0])
        return tuple(grads[:n_row_grads]) + tuple(_colsum(g) for g in grads[nr:])

    dtypes = [F32] * n_row_grads if dtypes is None else dtypes
    widths = [(r[2] if isinstance(r, tuple) else r.shape[1], dt) for r, dt in zip(rows[:n_row_grads], dtypes)]
    return _rowwise(name, bwd, list(rows) + list(cots), params, widths, [p.shape[1] for p in params], tm=tm)


def _ln(s, g, b):
    mu = jnp.mean(s, axis=-1, keepdims=True)
    var = jnp.mean(jnp.square(s - mu), axis=-1, keepdims=True)
    return (s - mu) * lax.rsqrt(var + LN_EPS) * g + b


def _softplus(x):
    return jnp.maximum(x, 0.0) + jnp.log1p(jnp.exp(-jnp.abs(x)))


def _expm1(x):
    series = x * (1.0 + x * (1.0 / 2 + x * (1.0 / 6 + x * (1.0 / 24 + x * (1.0 / 120 + x * (1.0 / 720))))))
    return jnp.where(jnp.abs(x) < 0.25, series, jnp.exp(x) - 1.0)


def _f_resid_ln(h, branch, g, b):
    return _ln(ALPHA * h + branch, g, b)


def _f_ple(h, gp, pe, bpg, g, b):
    return _ln(ALPHA * h + jax.nn.sigmoid(gp + bpg) * pe, g, b)


def _f_merge(ga, gb, ya, yb, bm0, bm1):
    return jax.nn.sigmoid(ga + bm0) * ya + jax.nn.sigmoid(gb + bm1) * yb


def _f_rnn_out(hs, ry):
    return hs * jax.nn.gelu(ry, approximate=True)


def _f_logf(fl, bf):
    return -_softplus(-(fl + bf))


def _f_gate(xc, ra, ia, lam, ba, bx):
    r = jax.nn.sigmoid(ra + ba)
    i = jax.nn.sigmoid(ia + bx)
    log_a = -RG_C * _softplus(-lam) * r
    a = jnp.exp(log_a)
    mult = jnp.sqrt(-_expm1(2.0 * log_a))
    return a, mult * (i * xc)


def _f_act(hg, hu):
    return jax.nn.silu(hg) * hu


ATT_BLOCK = 512


def _scores(q, k, cq, ck, diagonal):
    s = lax.dot_general(q, k, (NT, ((), ())), preferred_element_type=F32) * SCALE
    s = s + cq - ck
    if diagonal:
        row = lax.broadcasted_iota(jnp.int32, s.shape, 0)
        col = lax.broadcasted_iota(jnp.int32, s.shape, 1)
        s = jnp.where(col <= row, s, NEG)
    return s


def _dscores(p, do, o, v):
    dob = do.astype(BF16)
    delta = jnp.sum(dob.astype(F32) * o, axis=1, keepdims=True)
    dp = lax.dot_general(dob, v.astype(BF16), (NT, ((), ())), preferred_element_type=F32)
    return p * (dp - delta)


def _attn_fwd(z, cq, ck, bsz, seq, ride=None):
    t = bsz * seq
    tq = _tile(seq, ATT_BLOCK)
    nq = seq // tq

    def body(q_ref, k_ref, v_ref, cq_ref, ck_ref, o_ref, ob_ref, lse_ref):
        for i in range(nq):
            rows = slice(i * tq, (i + 1) * tq)
            q = q_ref[rows, :].astype(BF16)
            cqi = cq_ref[rows, :]

            def step(j, carry, diagonal, q=q, cqi=cqi):
                m, l, acc = carry
                keys = pl.ds(pl.multiple_of(j * tq, tq), tq)
                s = _scores(q, k_ref[keys, :].astype(BF16), cqi, ck_ref[pl.ds(j, 1), :], diagonal)
                m_new = jnp.maximum(m, jnp.max(s, axis=1, keepdims=True))
                alpha = jnp.exp(m - m_new)
                p = jnp.exp(s - m_new)
                p_hi = p.astype(BF16)
                p_lo = (p - p_hi.astype(F32)).astype(BF16)
                vb = v_ref[keys, :].astype(BF16)
                pv = lax.dot_general(p_hi, vb, (NN, ((), ())), preferred_element_type=F32)
                pv = pv + lax.dot_general(p_lo, vb, (NN, ((), ())), preferred_element_type=F32)
                return m_new, alpha * l + jnp.sum(p, axis=1, keepdims=True), alpha * acc + pv

            carry = (jnp.full((tq, 1), NEG, F32), jnp.zeros((tq, 1), F32), jnp.zeros((tq, HEAD_DIM), F32))
            if i > 0:
                carry = lax.fori_loop(0, i, functools.partial(step, diagonal=False), carry)
            m, l, acc = step(i, carry, True)
            o = acc / l
            o_ref[rows, :] = o
            ob_ref[rows, :] = o.astype(BF16)
            lse_ref[rows, :] = m + jnp.log(l)

    head = (seq, HEAD_DIM)
    in_specs = [
        pl.BlockSpec(head, lambda b, h: (b, h)),
        pl.BlockSpec(head, lambda b, h: (b, N_HEADS + h)),
        pl.BlockSpec(head, lambda b, h: (b, 2 * N_HEADS + h)),
        pl.BlockSpec((None, None, seq, 1), lambda b, h: (b, h, 0, 0)),
        pl.BlockSpec((None, None, nq, tq), lambda b, h: (b, h, 0, 0)),
    ]
    out_specs = [pl.BlockSpec(head, lambda b, h: (b, h)), pl.BlockSpec(head, lambda b, h: (b, h)),
                 pl.BlockSpec((None, None, seq, 1), lambda b, h: (b, h, 0, 0))]
    out_shape = [jax.ShapeDtypeStruct((t, D_MODEL), F32), jax.ShapeDtypeStruct((t, D_MODEL), BF16),
                 jax.ShapeDtypeStruct((bsz, N_HEADS, seq, 1), F32)]
    return _pcall(body, ride=ride, name="attn_fwd", grid=(bsz, N_HEADS), in_specs=in_specs, out_specs=out_specs,
                  out_shape=out_shape, compiler_params=_cparams(("parallel", "parallel")))(
                      z, z, z, cq, ck.reshape(bsz, N_HEADS, nq, tq))


def _attn_bwd(z, att, datt, lse, cq, ck, bsz, seq, ride=None):
    t = bsz * seq
    tq = _tile(seq, ATT_BLOCK)
    nq = seq // tq

    def body(q_ref, k_ref, v_ref, o_ref, do_ref, lse_ref, cq_ref, ck_ref,
             dq_ref, dk_ref, dv_ref, dcq_ref, dck_ref, dq_sc):
        dq_sc[...] = jnp.zeros_like(dq_sc)
        dcq_ref[...] = jnp.zeros_like(dcq_ref)
        for j in range(nq):
            keys = slice(j * tq, (j + 1) * tq)
            kb = k_ref[keys, :].astype(BF16)
            vb = v_ref[keys, :].astype(BF16)
            ckj = ck_ref[j:j + 1, :]

            def step(i, carry, diagonal, kb=kb, vb=vb, ckj=ckj):
                dk, dv, dc = carry
                rows = pl.ds(pl.multiple_of(i * tq, tq), tq)
                qb = q_ref[rows, :].astype(BF16)
                do = do_ref[rows, :]
                s = _scores(qb, kb, cq_ref[rows, :], ckj, diagonal)
                p = jnp.exp(s - lse_ref[rows, :])
                ds = _dscores(p, do, o_ref[rows, :], vb)
                dsb = (ds * SCALE).astype(BF16)
                dq_sc[rows, :] += lax.dot_general(dsb, kb, (NN, ((), ())), preferred_element_type=F32)
                dcq_ref[rows, :] += jnp.sum(ds, axis=1, keepdims=True)
                dv = dv + lax.dot_general(p.astype(BF16), do.astype(BF16), (TN, ((), ())),
                                          preferred_element_type=F32)
                dk = dk + lax.dot_general(dsb, qb, (TN, ((), ())), preferred_element_type=F32)
                return dk, dv, dc - jnp.sum(ds, axis=0, keepdims=True)

            zero = jnp.zeros((tq, HEAD_DIM), F32)
            carry = step(j, (zero, zero, jnp.zeros((1, tq), F32)), True)
            if j + 1 < nq:
                carry = lax.fori_loop(j + 1, nq, functools.partial(step, diagonal=False), carry)
            dk, dv, dck_ref[j:j + 1, :] = carry
            dk_ref[keys, :] = dk.astype(BF16)
            dv_ref[keys, :] = dv.astype(BF16)
        dq_ref[...] = dq_sc[...].astype(BF16)

    head = (seq, HEAD_DIM)
    hmap = lambda b, h: (b, h)
    col = pl.BlockSpec((None, None, seq, 1), lambda b, h: (b, h, 0, 0))
    row = pl.BlockSpec((None, None, nq, tq), lambda b, h: (b, h, 0, 0))
    in_specs = [pl.BlockSpec(head, hmap),
                pl.BlockSpec(head, lambda b, h: (b, N_HEADS + h)),
                pl.BlockSpec(head, lambda b, h: (b, 2 * N_HEADS + h)),
                pl.BlockSpec(head, hmap), pl.BlockSpec(head, hmap), col, col, row]
    big = jax.ShapeDtypeStruct((t, D_MODEL), BF16)
    return _pcall(body, ride=ride, name="attn_bwd", grid=(bsz, N_HEADS), in_specs=in_specs,
                  out_specs=[pl.BlockSpec(head, hmap)] * 3 + [col, row],
                  out_shape=[big, big, big, jax.ShapeDtypeStruct((bsz, N_HEADS, seq, 1), F32),
                             jax.ShapeDtypeStruct((bsz, N_HEADS, nq, tq), F32)],
                  scratch_shapes=[pltpu.VMEM(head, F32)],
                  compiler_params=_cparams(("parallel", "parallel")))(
                      z, z, z, att, datt, lse, cq, ck.reshape(bsz, N_HEADS, nq, tq))


def _scan(name, a, u, bsz, seq, *, reverse, with_prev=False, tb=256):
    c = u.shape[1]
    tb = _tile(seq, tb)
    nb = seq // tb
    has_a = a is not None

    def body(*refs):
        if has_a:
            a_ref, u_ref = refs[0], refs[1]
            rest = refs[2:]
        else:
            u_ref = refs[0]
            rest = refs[1:]
        outs = rest[:2] if with_prev else rest[:1]
        carry_sc, afirst_sc = rest[-2], rest[-1]
        step = pl.program_id(1)

        @pl.when(step == 0)
        def _():
            carry_sc[...] = jnp.zeros_like(carry_sc)
            afirst_sc[...] = jnp.zeros_like(afirst_sc)

        row = lax.broadcasted_iota(jnp.int32, (tb, c), 0)
        uu = u_ref[...]
        if has_a:
            aa = a_ref[...]
            if reverse:
                coef = jnp.where(row < tb - 1, pltpu.roll(aa, tb - 1, 0), afirst_sc[...])
            else:
                coef = aa
        k = 1
        while k < tb:
            shift = tb - k if reverse else k
            keep = (row < tb - k) if reverse else (row >= k)
            uu_sh = jnp.where(keep, pltpu.roll(uu, shift, 0), 0.0)
            if has_a:
                uu = coef * uu_sh + uu
                coef = coef * jnp.where(keep, pltpu.roll(coef, shift, 0), 1.0)
            else:
                uu = uu + uu_sh
            k *= 2
        carry = carry_sc[...]
        h = uu + coef * carry if has_a else uu + carry
        outs[0][...] = h
        if with_prev:
            outs[1][...] = jnp.where(row >= 1, pltpu.roll(h, 1, 0), carry)
        if reverse:
            carry_sc[...] = outs[0][0:1, :]
            if has_a:
                afirst_sc[...] = a_ref[0:1, :]
        else:
            carry_sc[...] = outs[0][tb - 1:tb, :]

    if reverse:
        imap = lambda b, s: (b * nb + nb - 1 - s, 0)
    else:
        imap = lambda b, s: (b * nb + s, 0)
    spec = pl.BlockSpec((tb, c), imap)
    n_in = 2 if has_a else 1
    n_out = 2 if with_prev else 1
    res = _pcall(body, name=name, grid=(bsz, nb), in_specs=[spec] * n_in, out_specs=[spec] * n_out,
                 out_shape=[jax.ShapeDtypeStruct(u.shape, F32)] * n_out,
                 scratch_shapes=[pltpu.VMEM((1, c), F32), pltpu.VMEM((1, c), F32)],
                 compiler_params=_cparams(("parallel", "arbitrary")))(*([a, u] if has_a else [u]))
    return res if with_prev else res[0]


def _conv_fwd(z, w, b, bsz, seq, tb=256):
    c = D_MODEL
    t = bsz * seq
    tb = _tile(seq, tb)
    nb = seq // tb

    def body(x_ref, w_ref, b_ref, o_ref, tail_sc):
        step = pl.program_id(1)

        @pl.when(step == 0)
        def _():
            tail_sc[...] = jnp.zeros_like(tail_sc)

        x = x_ref[...]
        row8 = lax.broadcasted_iota(jnp.int32, (8, c), 0)
        tail = tail_sc[...]
        acc = w_ref[CONV_W - 1:CONV_W, :] * x + b_ref[...]
        for sh in range(1, CONV_W):
            xs = pltpu.roll(x, sh, 0)
            top = jnp.where(row8 < sh, pltpu.roll(tail, sh, 0), xs[0:8, :])
            xs = jnp.concatenate([top, xs[8:, :]], axis=0) if tb > 8 else top
            acc = acc + w_ref[CONV_W - 1 - sh:CONV_W - sh, :] * xs
        o_ref[...] = acc
        tail_sc[...] = x_ref[tb - 8:tb, :]

    return _pcall(body, name="conv_fwd", grid=(bsz, nb),
                  in_specs=[pl.BlockSpec((tb, c), lambda bb, s: (bb * nb + s, OFF_RX // c)),
                            pl.BlockSpec((CONV_W, c), lambda bb, s: (0, 0)),
                            pl.BlockSpec((1, c), lambda bb, s: (0, 0))],
                  out_specs=pl.BlockSpec((tb, c), lambda bb, s: (bb * nb + s, 0)),
                  out_shape=jax.ShapeDtypeStruct((t, c), F32),
                  scratch_shapes=[pltpu.VMEM((8, c), F32)],
                  compiler_params=_cparams(("parallel", "arbitrary")))(z, w, b)


def _conv_bwd(z, dxc, w, bsz, seq, tb=256):
    c = D_MODEL
    t = bsz * seq
    tb = _tile(seq, tb)
    nb = seq // tb

    def body(x_ref, g_ref, w_ref, dx_ref, dw_ref, db_ref, head_sc):
        bb, step = pl.program_id(0), pl.program_id(1)

        @pl.when(step == 0)
        def _():
            head_sc[...] = jnp.zeros_like(head_sc)

        x, g = x_ref[...], g_ref[...]
        row8 = lax.broadcasted_iota(jnp.int32, (8, c), 0)
        head = head_sc[...]
        dx = w_ref[CONV_W - 1:CONV_W, :] * g
        dws = [None] * CONV_W
        dws[CONV_W - 1] = _colsum(g * x)
        for sh in range(1, CONV_W):
            gs = pltpu.roll(g, tb - sh, 0)
            bot = jnp.where(row8 >= 8 - sh, pltpu.roll(head, 8 - sh, 0), gs[tb - 8:tb, :])
            gs = jnp.concatenate([gs[:tb - 8, :], bot], axis=0) if tb > 8 else bot
            dx = dx + w_ref[CONV_W - 1 - sh:CONV_W - sh, :] * gs
            dws[CONV_W - 1 - sh] = _colsum(gs * x)
        dx_ref[...] = dx.astype(dx_ref.dtype)
        first = (bb == 0) & (step == 0)
        dw = jnp.concatenate(dws, axis=0)
        db = _colsum(g)

        @pl.when(first)
        def _():
            dw_ref[...] = dw
            db_ref[...] = db

        @pl.when(jnp.logical_not(first))
        def _():
            dw_ref[...] += dw
            db_ref[...] += db

        head_sc[...] = g_ref[0:8, :]

    rmap = lambda bb, s: (bb * nb + nb - 1 - s, 0)
    return _pcall(body, name="conv_bwd", grid=(bsz, nb),
                  in_specs=[pl.BlockSpec((tb, c), lambda bb, s: (bb * nb + nb - 1 - s, OFF_RX // c)),
                            pl.BlockSpec((tb, c), rmap),
                            pl.BlockSpec((CONV_W, c), lambda bb, s: (0, 0))],
                  out_specs=[pl.BlockSpec((tb, c), rmap),
                             pl.BlockSpec((CONV_W, c), lambda bb, s: (0, 0)),
                             pl.BlockSpec((1, c), lambda bb, s: (0, 0))],
                  out_shape=[jax.ShapeDtypeStruct((t, c), BF16), jax.ShapeDtypeStruct((CONV_W, c), F32),
                             jax.ShapeDtypeStruct((1, c), F32)],
                  scratch_shapes=[pltpu.VMEM((8, c), F32)],
                  compiler_params=_cparams(("arbitrary", "arbitrary")))(z, dxc, w)


def _gate_fwd(xc, w_a, w_x, b_a, b_x, lam, tm=512):
    t = xc.shape[0]
    tm = _tile(t, tm)

    def body(xc_ref, wa_ref, wx_ref, ba_ref, bx_ref, lam_ref, a_ref, u_ref):
        xc_b = xc_ref[...]
        xb = xc_b.astype(BF16)
        ra = lax.dot_general(xb, wa_ref[...].astype(BF16), (NN, ((), ())), preferred_element_type=F32)
        ia = lax.dot_general(xb, wx_ref[...].astype(BF16), (NN, ((), ())), preferred_element_type=F32)
        a, u = _f_gate(xc_b, ra, ia, lam_ref[...], ba_ref[...], bx_ref[...])
        a_ref[...] = a
        u_ref[...] = u

    row = pl.BlockSpec((tm, BLK), lambda n, i: (i, n))
    wsp = pl.BlockSpec((None, BLK, BLK), lambda n, i: (n, 0, 0))
    vec = pl.BlockSpec((1, BLK), lambda n, i: (0, n))
    return _pcall(body, name="gate_fwd", grid=(N_BLK, t // tm), in_specs=[row, wsp, wsp, vec, vec, vec],
                  out_specs=[row, row], out_shape=[jax.ShapeDtypeStruct((t, D_MODEL), F32)] * 2,
                  compiler_params=_cparams(("parallel", "parallel")))(xc, w_a, w_x, b_a, b_x, lam)


def _gate_bwd(xc, w_a, w_x, b_a, b_x, lam, da, du, tm=512, ride=None):
    t = xc.shape[0]
    tm = _tile(t, tm)

    def body(xc_ref, wa_ref, wx_ref, ba_ref, bx_ref, lam_ref, da_ref, du_ref,
             dxc_ref, dwa_ref, dwx_ref, dba_ref, dbx_ref, dlam_ref):
        step = pl.program_id(1)
        xc_b = xc_ref[...]
        xb = xc_b.astype(BF16)
        wa, wx = wa_ref[...].astype(BF16), wx_ref[...].astype(BF16)
        ra = lax.dot_general(xb, wa, (NN, ((), ())), preferred_element_type=F32)
        ia = lax.dot_general(xb, wx, (NN, ((), ())), preferred_element_type=F32)
        full = lambda r: jnp.broadcast_to(r[...], (tm, BLK))
        _, pull = jax.vjp(_f_gate, xc_b, ra, ia, full(lam_ref), full(ba_ref), full(bx_ref))
        dxc, dra, dia, dlam, dba, dbx = pull((da_ref[...], du_ref[...]))
        drb, dib = dra.astype(BF16), dia.astype(BF16)
        dxc = dxc + lax.dot_general(drb, wa, (NT, ((), ())), preferred_element_type=F32)
        dxc = dxc + lax.dot_general(dib, wx, (NT, ((), ())), preferred_element_type=F32)
        dxc_ref[...] = dxc
        _accumulate(dwa_ref, lax.dot_general(xb, drb, (TN, ((), ())), preferred_element_type=F32), step)
        _accumulate(dwx_ref, lax.dot_general(xb, dib, (TN, ((), ())), preferred_element_type=F32), step)
        _accumulate(dba_ref, _colsum(dba), step)
        _accumulate(dbx_ref, _colsum(dbx), step)
        _accumulate(dlam_ref, _colsum(dlam), step)

    row = pl.BlockSpec((tm, BLK), lambda n, i: (i, n))
    wsp = pl.BlockSpec((None, BLK, BLK), lambda n, i: (n, 0, 0))
    vec = pl.BlockSpec((1, BLK), lambda n, i: (0, n))
    wshape = jax.ShapeDtypeStruct((N_BLK, BLK, BLK), F32)
    vshape = jax.ShapeDtypeStruct((1, D_MODEL), F32)
    return _pcall(body, ride=ride, name="gate_bwd", grid=(N_BLK, t // tm),
                  in_specs=[row, wsp, wsp, vec, vec, vec, row, row],
                  out_specs=[row, wsp, wsp, vec, vec, vec],
                  out_shape=[jax.ShapeDtypeStruct((t, D_MODEL), F32), wshape, wshape, vshape, vshape, vshape],
                  compiler_params=_cparams(("parallel", "arbitrary")))(xc, w_a, w_x, b_a, b_x, lam, da, du)


def _act_fwd(hgu, tm=512):
    _, t, w = hgu.shape
    tm = _tile(t, tm)

    def body(hg_ref, hu_ref, o_ref):
        o_ref[...] = _f_act(hg_ref[...], hu_ref[...]).astype(o_ref.dtype)

    spec = lambda off: pl.BlockSpec((None, tm, w), lambda s, i: (s + off, i, 0))
    return _pcall(body, name="act_fwd", grid=(N_FF, t // tm), in_specs=[spec(0), spec(N_FF)], out_specs=spec(0),
                  out_shape=jax.ShapeDtypeStruct((N_FF, t, w), BF16),
                  compiler_params=_cparams(("parallel", "parallel")))(hgu, hgu)


def _act_bwd(hgu, dact, tm=512, ride=None):
    _, t, w = hgu.shape
    tm = _tile(t, tm)

    def body(hg_ref, hu_ref, d_ref, o_ref):
        _, pull = jax.vjp(_f_act, hg_ref[...], hu_ref[...])
        dhg, dhu = pull(d_ref[...])
        o_ref[0] = dhg.astype(o_ref.dtype)
        o_ref[1] = dhu.astype(o_ref.dtype)

    spec = lambda off: pl.BlockSpec((None, tm, w), lambda s, i: (s + off, i, 0))
    res = _pcall(body, ride=ride, name="act_bwd", grid=(N_FF, t // tm), in_specs=[spec(0), spec(N_FF), spec(0)],
                 out_specs=pl.BlockSpec((2, None, tm, w), lambda s, i: (0, s, i, 0)),
                 out_shape=jax.ShapeDtypeStruct((2, N_FF, t, w), BF16),
                 compiler_params=_cparams(("parallel", "parallel")))(hgu, hgu, dact)
    return res.reshape(2 * N_FF, t, w)


def _adamw(name, parts, w, m, v, tr=128):
    ng = len(parts)
    n_src, r, c = parts[0].shape
    tr = _tile(r, tr)
    nb = r // tr
    bc1 = 1.0 - ADAM_B1 ** ADAM_STEP
    bc2 = 1.0 - ADAM_B2 ** ADAM_STEP

    def body(*refs):
        p_refs = refs[:ng]
        w_ref, m_ref, v_ref, g_ref, d_ref, nm_ref, nv_ref = refs[ng:]
        grp = pl.program_id(0)

        def update(p_ref):
            g = p_ref[0].astype(F32)
            for s in range(1, n_src):
                g = g + p_ref[s].astype(F32)
            nm = ADAM_B1 * m_ref[...] + (1.0 - ADAM_B1) * g
            nv = ADAM_B2 * v_ref[...] + (1.0 - ADAM_B2) * jnp.square(g)
            g_ref[...] = g
            nm_ref[...] = nm
            nv_ref[...] = nv
            d_ref[...] = -ADAM_LR * ((nm / bc1) / (jnp.sqrt(nv / bc2) + ADAM_EPS) + ADAM_WD * w_ref[...])

        for k in range(ng):
            pl.when(grp == k)(functools.partial(update, p_refs[k]))

    p_specs = [pl.BlockSpec((n_src, tr, c), functools.partial(lambda gi, i, k: (0, jnp.where(gi == k, i, 0), 0), k=k))
               for k in range(ng)]
    spec = pl.BlockSpec((tr, c), lambda gi, i: (gi * nb + i, 0))
    return _pcall(body, name=name, grid=(ng, nb), in_specs=p_specs + [spec, spec, spec],
                  out_specs=[spec] * 4, out_shape=[jax.ShapeDtypeStruct((ng * r, c), F32)] * 4,
                  compiler_params=_cparams(("parallel", "parallel")))(*parts, w, m, v)


def _sum_parts(name, parts, tr=256):
    _, r, c = parts.shape
    tr = _tile(r, tr)

    def body(p_ref, o_ref):
        g = p_ref[0]
        for s in range(1, parts.shape[0]):
            g = g + p_ref[s]
        o_ref[...] = g

    return _pcall(body, name=name, grid=(r // tr,),
                  in_specs=[pl.BlockSpec((parts.shape[0], tr, c), lambda i: (0, i, 0))],
                  out_specs=pl.BlockSpec((tr, c), lambda i: (i, 0)),
                  out_shape=jax.ShapeDtypeStruct((r, c), F32), compiler_params=_cparams(("parallel",)))(parts)


def _peer(k):
    x, y, c = lax.axis_index("x"), lax.axis_index("y"), lax.axis_index("c")
    return (x ^ ((k >> 2) & 1), y ^ ((k >> 1) & 1), c ^ (k & 1))


def _my_id():
    return 4 * lax.axis_index("x") + 2 * lax.axis_index("y") + lax.axis_index("c")


def _exchange(name, ride):
    n = len(ride.arrays)

    def body(*refs):
        ride.begin(refs[:n], refs[n:2 * n], refs[2 * n:])
        ride.finish(refs[:n], refs[n:2 * n], refs[2 * n:])

    hbm = pl.BlockSpec(memory_space=pltpu.HBM)
    return _pcall(body, name=name, in_specs=[hbm] * n, out_specs=[hbm] * n, out_shape=ride.out_shapes(),
                  scratch_shapes=ride.scratch())(*ride.arrays)


def _row(v):
    return v.reshape(1, -1)


def _time_major_heads(c, bsz, seq):
    return c.reshape(bsz, seq, BLK)[:, :, :N_HEADS].transpose(0, 2, 1)


def _no_ride(*_):
    return None


TWICE = [(D_MODEL, F32), (D_MODEL, BF16)]


def _both(fn):
    def run(*v):
        y = fn(*v)
        return y, y
    return run


def _layer_fwd(h, hb, p_l, w, bsz, seq, ride_of=_no_ride):
    t = bsz * seq
    zq = _mm_nn("z_proj_qkv", hb, w['w_in7'], n=QKV, out_dtype=BF16, ride=ride_of('z_proj_qkv'))
    zr = _mm_nn("z_proj_rest", hb, w['w_in7'], b_off=QKV, n=4 * D_MODEL, ride=ride_of('z_proj_rest'))
    fl = _mm_nn("f_proj", hb, w['w_inf'])
    logf, = _rowwise("logf_fwd", lambda f, b: (_f_logf(f, b),), [fl], [w['b_forget']], [(BLK, F32)], [])
    c = _scan("cumsum_fwd", None, logf, bsz, seq, reverse=False)
    ct = _time_major_heads(c, bsz, seq)
    cq, ck = ct[..., None], ct[:, :, None, :]
    att, attb, lse = _attn_fwd(zq, cq, ck, bsz, seq, ride=ride_of('attn_fwd'))
    xc = _conv_fwd(zr, w['conv_w'], w['conv_b'], bsz, seq)
    a, u = _gate_fwd(xc, w['rg_w_a'], w['rg_w_x'], w['rg_b_a'], w['rg_b_x'], w['rg_lambda'])
    hs, hprev = _scan("lru_fwd", a, u, bsz, seq, reverse=False, with_prev=True)
    rnn, = _rowwise("rnn_out_fwd", lambda s, y: (_f_rnn_out(s, y),), [hs, (zr, OFF_RY, D_MODEL)], [],
                    [(D_MODEL, BF16)], [])
    ya = _mm_nn("branch_att", attb, w['w_branch_att'])
    yb = _mm_nn("branch_rnn", rnn, w['w_branch_rnn'])
    merged, = _rowwise("merge_fwd", lambda *v: (_f_merge(*v),),
                       [(zr, OFF_GA, D_MODEL), (zr, OFF_GB, D_MODEL), ya, yb], [w['b_merge0'], w['b_merge1']],
                       [(D_MODEL, BF16)], [])
    mix = _mm_nn("mix_out", merged, w['w_out'])
    h1, h1b = _rowwise("ln_mix_fwd", _both(_f_resid_ln), [h, mix], [w['ln_mix_g'], w['ln_mix_b']], TWICE, [])
    tm = _tile(t, 1024)
    hgu = _mm("ffn_in", h1b, w['w_ffn_in'], grid=(t // tm, 2 * N_FF, 1),
              a_spec=pl.BlockSpec((tm, D_MODEL), lambda i, s, k: (i, 0)),
              b_spec=pl.BlockSpec((None, D_MODEL, FF_SH), lambda i, s, k: (s, 0, 0)),
              o_spec=pl.BlockSpec((None, tm, FF_SH), lambda i, s, k: (s, i, 0)),
              out_shape=jax.ShapeDtypeStruct((2 * N_FF, t, FF_SH), F32), contract=NN, ride=ride_of('ffn_in'))
    act = _act_fwd(hgu)
    ffn = _mm("ffn_out", act, w['w_ffn_out'], grid=(t // tm, 1, N_FF),
              a_spec=pl.BlockSpec((None, tm, FF_SH), lambda i, j, s: (s, i, 0)),
              b_spec=pl.BlockSpec((None, FF_SH, D_MODEL), lambda i, j, s: (s, 0, 0)),
              o_spec=pl.BlockSpec((tm, D_MODEL), lambda i, j, s: (i, 0)),
              out_shape=jax.ShapeDtypeStruct((t, D_MODEL), F32), contract=NN, ride=ride_of('ffn_out'))
    h2, h2b = _rowwise("ln_ffn_fwd", _both(_f_resid_ln), [h1, ffn], [w['ln_ffn_g'], w['ln_ffn_b']], TWICE, [])
    gp = _mm_nn("ple_gate", h2b, w['w_ple_gate'])
    pe = _mm_nn("ple_proj", p_l, w['w_ple'])
    h3, h3b = _rowwise("ln_ple_fwd", _both(_f_ple), [h2, gp, pe],
                       [w['b_ple_gate'], w['ln_ple_g'], w['ln_ple_b']], TWICE, [])
    saved = dict(h=h, hb=hb, zq=zq, zr=zr, fl=fl, cq=cq, ck=ck, att=att, attb=attb, lse=lse, xc=xc, a=a,
                 hprev=hprev, hs=hs, rnn=rnn, ya=ya, yb=yb, merged=merged, mix=mix, h1=h1, h1b=h1b, hgu=hgu,
                 act=act, ffn=ffn, h2=h2, h2b=h2b, gp=gp, pe=pe)
    return h3, h3b, saved


def _layer_bwd(dh3, p_l, w, s, bsz, seq, ride_of=_no_ride):
    t = bsz * seq
    g = {}
    dh2, dgp, dpe, g['b_ple_gate'], g['ln_ple_g'], g['ln_ple_b'] = _vjp_rowwise(
        "ln_ple_bwd", _f_ple, [s['h2'], s['gp'], s['pe']], [w['b_ple_gate'], w['ln_ple_g'], w['ln_ple_b']], [dh3], 3,
        dtypes=[F32, BF16, BF16])
    g['w_ple_gate'] = _mm_tn("ple_gate_dw", s['h2b'], dgp, out_dtype=BF16)
    g['w_ple'] = _mm_tn("ple_proj_dw", p_l, dpe, out_dtype=BF16)
    dh2b = _mm_nt("ple_gate_dx", dgp, w['w_ple_gate'])
    dh1, dffn, g['ln_ffn_g'], g['ln_ffn_b'] = _ln_resid_bwd(
        "ln_ffn_bwd", s['h1'], s['ffn'], w['ln_ffn_g'], w['ln_ffn_b'], dh2, dh2b)
    tm = _tile(t, 1024)
    dact = _mm("ffn_out_dx", dffn, w['w_ffn_out'], grid=(t // tm, N_FF, 1),
               a_spec=pl.BlockSpec((tm, D_MODEL), lambda i, ss, k: (i, 0)),
               b_spec=pl.BlockSpec((None, FF_SH, D_MODEL), lambda i, ss, k: (ss, 0, 0)),
               o_spec=pl.BlockSpec((None, tm, FF_SH), lambda i, ss, k: (ss, i, 0)),
               out_shape=jax.ShapeDtypeStruct((N_FF, t, FF_SH), F32), contract=NT)
    tk = _tile(t, 2048)
    g['w_ffn_out'] = _mm("ffn_out_dw", s['act'], dffn, grid=(N_FF, 1, t // tk),
                         a_spec=pl.BlockSpec((None, tk, FF_SH), lambda ss, j, k: (ss, k, 0)),
                         b_spec=pl.BlockSpec((tk, D_MODEL), lambda ss, j, k: (k, 0)),
                         o_spec=pl.BlockSpec((None, FF_SH, D_MODEL), lambda ss, j, k: (ss, 0, 0)),
                         out_shape=jax.ShapeDtypeStruct((N_FF, FF_SH, D_MODEL), BF16), contract=TN)
    dhgu = _act_bwd(s['hgu'], dact, ride=ride_of('act_bwd', g))
    g['w_ffn_in'] = _mm("ffn_in_dw", s['h1b'], dhgu, grid=(2 * N_FF, 1, t // tk),
                        a_spec=pl.BlockSpec((tk, D_MODEL), lambda ss, j, k: (k, 0)),
                        b_spec=pl.BlockSpec((None, tk, FF_SH), lambda ss, j, k: (ss, k, 0)),
                        o_spec=pl.BlockSpec((None, D_MODEL, FF_SH), lambda ss, j, k: (ss, 0, 0)),
                        out_shape=jax.ShapeDtypeStruct((2 * N_FF, D_MODEL, FF_SH), BF16), contract=TN,
                        ride=ride_of('ffn_in_dw', g))
    dh1b = _mm("ffn_in_dx", dhgu, w['w_ffn_in'], grid=(t // tm, 1, 2 * N_FF),
               a_spec=pl.BlockSpec((None, tm, FF_SH), lambda i, j, ss: (ss, i, 0)),
               b_spec=pl.BlockSpec((None, D_MODEL, FF_SH), lambda i, j, ss: (ss, 0, 0)),
               o_spec=pl.BlockSpec((tm, D_MODEL), lambda i, j, ss: (i, 0)),
               out_shape=jax.ShapeDtypeStruct((t, D_MODEL), F32), contract=NT, ride=ride_of('ffn_in_dx', g))
    dh, dmix, g['ln_mix_g'], g['ln_mix_b'] = _ln_resid_bwd(
        "ln_mix_bwd", s['h'], s['mix'], w['ln_mix_g'], w['ln_mix_b'], dh1, dh1b)
    g['w_out'] = _mm_tn("mix_out_dw", s['merged'], dmix, out_dtype=BF16)
    dmerged = _mm_nt("mix_out_dx", dmix, w['w_out'])
    z = s['zr']
    dga, dgb, dya, dyb, dbm0, dbm1 = _vjp_rowwise(
        "merge_bwd", _f_merge, [(z, OFF_GA, D_MODEL), (z, OFF_GB, D_MODEL), s['ya'], s['yb']],
        [w['b_merge0'], w['b_merge1']], [dmerged], 4, dtypes=[BF16] * 4)
    g['b_merge'] = jnp.concatenate([dbm0, dbm1], axis=0)
    g['w_branch_att'] = _mm_tn("branch_att_dw", s['attb'], dya, out_dtype=BF16)
    g['w_branch_rnn'] = _mm_tn("branch_rnn_dw", s['rnn'], dyb, out_dtype=BF16)
    datt = _mm_nt("branch_att_dx", dya, w['w_branch_att'], out_dtype=BF16)
    drnn = _mm_nt("branch_rnn_dx", dyb, w['w_branch_rnn'])
    dhs, dry = _vjp_rowwise("rnn_out_bwd", _f_rnn_out, [s['hs'], (z, OFF_RY, D_MODEL)], [], [drnn], 2,
                            dtypes=[F32, BF16])
    lam = _scan("lru_bwd", s['a'], dhs, bsz, seq, reverse=True)
    da, = _rowwise("lru_da", lambda l, hp: (l * hp,), [lam, s['hprev']], [], [(D_MODEL, F32)], [])
    dxc, g['rg_w_a'], g['rg_w_x'], g['rg_b_a'], g['rg_b_x'], g['rg_lambda'] = _gate_bwd(
        s['xc'], w['rg_w_a'], w['rg_w_x'], w['rg_b_a'], w['rg_b_x'], w['rg_lambda'], da, lam,
        ride=ride_of('gate_bwd', g))
    drx, g['conv_w'], g['conv_b'] = _conv_bwd(z, dxc, w['conv_w'], bsz, seq)
    dq, dk, dv, dcq, dck = _attn_bwd(s['zq'], s['att'], datt, s['lse'], s['cq'], s['ck'], bsz, seq,
                                     ride=ride_of('attn_bwd', g))
    dc = (dcq[:, :, :, 0] + dck.reshape(bsz, N_HEADS, seq)).transpose(0, 2, 1)
    dc = jnp.pad(dc, ((0, 0), (0, 0), (0, BLK - N_HEADS))).reshape(t, BLK)
    dlogf = _scan("cumsum_bwd", None, dc, bsz, seq, reverse=True)
    dfl, g['b_forget'] = _vjp_rowwise("logf_bwd", _f_logf, [s['fl']], [w['b_forget']], [dlogf], 1, dtypes=[BF16])
    dz = jnp.concatenate([dq, dk, dv, drx, dry, dga, dgb], axis=1)
    g['w_in7'] = _mm_tn("z_proj_dw", s['hb'], dz, out_dtype=BF16)
    g['w_inf'] = _mm_tn("f_proj_dw", s['hb'], dfl, out_dtype=BF16)
    dh = _mm_nt("z_proj_dx", dz, w['w_in7'], ride=ride_of('z_proj_dx', g), add=dh)
    dh = _mm_nt("f_proj_dx", dfl, w['w_inf'], add=dh)
    return dh, g


def _ln_resid_bwd(name, h, branch, gam, bet, d0, d1):
    def bwd(hv, bv, d0v, d1v, gv, btv):
        _, pull = jax.vjp(_f_resid_ln, hv, bv, gv, btv)
        dh, db, dg, dbt = pull(d0v + d1v)
        return dh, db, _colsum(dg), _colsum(dbt)

    return _rowwise(name, bwd, [h, branch, d0, d1], [gam, bet], [(D_MODEL, F32), (D_MODEL, BF16)],
                    [D_MODEL, D_MODEL])


class _Schedule:
    FWD = {'z_proj_qkv': ['w_ffn_out'], 'z_proj_rest': ['w_branch_att', 'w_branch_rnn', 'w_out', 'w_ple_gate'],
           'attn_fwd': ['w_in'], 'ffn_in': ['w_ffn_in'], 'ffn_out': ['w_ple', 'conv_w', 'b_merge']}
    BWD = {'act_bwd': ['w_ffn_out'], 'ffn_in_dw': ['w_ple_gate', 'w_ple'],
           'gate_bwd': ['w_out', 'w_branch_att', 'w_branch_rnn'],
           'attn_bwd': ['w_ffn_in', 'conv_w', 'b_merge']}

    def __init__(self, shards, depth):
        self.shards, self.depth = shards, depth
        self.gathered = [{} for _ in range(depth)]
        self.received = [{} for _ in range(depth)]
        self.pending = []
        self.deferred = None

    def gather_ride(self, layer, kernel_name):
        if layer + 1 >= self.depth:
            return None
        names = self.FWD[kernel_name]
        ride = _Ride([self.shards[n] for n in names], gather=True, index=layer + 1)
        self.pending.append((ride, names, self.gathered[layer + 1]))
        return ride

    def _scatter(self, arrays, names, layer):
        ride = _Ride(arrays, gather=False)
        self.pending.append((ride, names, self.received[layer]))
        return ride

    def scatter_ride(self, layer, kernel_name, grads):
        if kernel_name == 'z_proj_dx':
            whole = _by_destination('w_in', grads)
            half = whole.shape[1] // 2
            self.deferred = (whole[:, half:], layer)
            return self._scatter([whole[:, :half]], ['w_in_a'], layer)
        if kernel_name == 'ffn_in_dx':
            if self.deferred is None:
                return None
            (late, from_layer), self.deferred = self.deferred, None
            return self._scatter([late], ['w_in_b'], from_layer)
        names = self.BWD[kernel_name]
        return self._scatter([_by_destination(n, grads) for n in names], names, layer)

    def flush(self):
        late, from_layer = self.deferred
        self.deferred = None
        self.received[from_layer]['w_in_b'], = _exchange("scatter_last", _Ride([late], gather=False))

    def collect(self):
        for ride, names, dst in self.pending:
            dst.update(zip(names, ride.result))
        self.pending = []


def _local_step(x2, tgt, p3, weights_of, depth, g_in, b_in, bsz, seq, sched=None):
    h, hb = _rowwise("ln_in_fwd", _both(_ln), [x2], [g_in, b_in], TWICE, [])
    p3 = p3.astype(BF16)
    saved, layer_w = [], []
    for l in range(depth):
        layer_w.append(weights_of(l))
        ride_of = functools.partial(sched.gather_ride, l) if sched else _no_ride
        h, hb, s = _layer_fwd(h, hb, p3[l], layer_w[l], bsz, seq, ride_of)
        if sched:
            sched.collect()
        saved.append(s)

    def loss_fn(y, tv):
        err = y - tv
        return err * (1.0 / D_MODEL), _colsum(jnp.square(err))

    dh, sq = _rowwise("loss", loss_fn, [h, tgt], [], [(D_MODEL, F32)], [D_MODEL])
    grads = [None] * depth
    for l in reversed(range(depth)):
        ride_of = functools.partial(sched.scatter_ride, l) if sched else _no_ride
        dh, grads[l] = _layer_bwd(dh, p3[l], layer_w[l], saved[l], bsz, seq, ride_of)
        if sched:
            sched.collect()
    dx, dg_in, db_in = _vjp_rowwise("ln_in_bwd", _ln, [x2], [g_in, b_in], [dh], 1)
    return sq, dx, grads, dg_in, db_in


def _layer_weights(full):
    w = {}
    wt = full['w_in'].transpose(1, 0, 2).reshape(D_MODEL, N_IN)
    w['w_in7'] = jnp.concatenate([wt[:, :3 * D_MODEL], wt[:, 3 * D_MODEL + N_HEADS:]], axis=1)
    w['w_inf'] = jnp.pad(wt[:, 3 * D_MODEL:3 * D_MODEL + N_HEADS], ((0, 0), (0, BLK - N_HEADS)))
    for n in ['w_branch_att', 'w_branch_rnn', 'w_out', 'w_ple_gate']:
        w[n] = full[n].reshape(D_MODEL, D_MODEL)
    w['w_ffn_in'] = full['w_ffn_in']
    w['w_ffn_out'] = full['w_ffn_out'].reshape(N_FF, FF_SH, D_MODEL)
    w['w_ple'] = full['w_ple'].transpose(1, 0, 2).reshape(D_PLE, D_MODEL)
    w['conv_w'] = full['conv_w'].transpose(1, 0, 2).reshape(CONV_W, D_MODEL)
    bm = full['b_merge'].transpose(1, 0, 2).reshape(2, D_MODEL)
    w['b_merge0'], w['b_merge1'] = bm[0:1], bm[1:2]
    return w


def _by_destination(name, gw):
    if name == 'w_in':
        g7, gf = gw['w_in7'], gw['w_inf']
        true = jnp.concatenate([g7[:, :3 * D_MODEL], gf[:, :N_HEADS], g7[:, 3 * D_MODEL:]], axis=1)
        return true.reshape(D_MODEL, N_DEV, IN_SH).transpose(1, 0, 2)
    g = gw[name]
    if name in ('w_branch_att', 'w_branch_rnn', 'w_out', 'w_ple_gate'):
        return g.reshape(N_DEV, D_MODEL // N_DEV, D_MODEL)
    if name == 'w_ffn_in':
        return g
    if name == 'w_ffn_out':
        return g.reshape(N_DEV, N_FF * FF_SH // N_DEV, D_MODEL)
    return g.reshape(g.shape[0], N_DEV, BLK).transpose(1, 0, 2)


def kernel(x, p, ln_in_g, ln_in_b, w_in, b_forget, conv_w, conv_b, rg_w_a, rg_b_a, rg_w_x, rg_b_x, rg_lambda, w_branch_att, w_branch_rnn, b_merge, w_out, ln_mix_g, ln_mix_b, w_ffn_in, w_ffn_out, ln_ffn_g, ln_ffn_b, w_ple, w_ple_gate, b_ple_gate, ln_ple_g, ln_ple_b, loss_target, m_ln_in_g, m_ln_in_b, m_w_in, m_b_forget, m_conv_w, m_conv_b, m_rg_w_a, m_rg_b_a, m_rg_w_x, m_rg_b_x, m_rg_lambda, m_w_branch_att, m_w_branch_rnn, m_b_merge, m_w_out, m_ln_mix_g, m_ln_mix_b, m_w_ffn_in, m_w_ffn_out, m_ln_ffn_g, m_ln_ffn_b, m_w_ple, m_w_ple_gate, m_b_ple_gate, m_ln_ple_g, m_ln_ple_b, v_ln_in_g, v_ln_in_b, v_w_in, v_b_forget, v_conv_w, v_conv_b, v_rg_w_a, v_rg_b_a, v_rg_w_x, v_rg_b_x, v_rg_lambda, v_w_branch_att, v_w_branch_rnn, v_b_merge, v_w_out, v_ln_mix_g, v_ln_mix_b, v_w_ffn_in, v_w_ffn_out, v_ln_ffn_g, v_ln_ffn_b, v_w_ple, v_w_ple_gate, v_b_ple_gate, v_ln_ple_g, v_ln_ple_b):
    env = dict(locals())
    wts = {n: env[n] for n in WEIGHTS}
    mom = {n: env['m_' + n] for n in WEIGHTS}
    var = {n: env['v_' + n] for n in WEIGHTS}
    bsz, seq, _ = x.shape
    depth = w_in.shape[0]
    t = bsz * seq
    x2, tgt = x.reshape(t, D_MODEL), loss_target.reshape(t, D_MODEL)
    p3 = p.reshape(depth, t, D_PLE)

    shard_names = SHARDED_BF16 + SHARDED_F32
    shards = {n: wts[n].astype(BF16) for n in SHARDED_BF16}
    shards.update({n: wts[n] for n in SHARDED_F32})
    sched = _Schedule(shards, depth)
    first = _Ride([shards[n] for n in shard_names], gather=True, index=0)
    sched.gathered[0] = dict(zip(shard_names, _exchange("gather_layer0", first)))

    def weights_of(l):
        w = _layer_weights(sched.gathered[l])
        for n in ['conv_b', 'rg_b_a', 'rg_b_x', 'rg_lambda', 'ln_mix_g', 'ln_mix_b', 'ln_ffn_g', 'ln_ffn_b',
                  'b_ple_gate', 'ln_ple_g', 'ln_ple_b']:
            w[n] = _row(wts[n][l])
        w['b_forget'] = jnp.pad(_row(b_forget[l]), ((0, 0), (0, BLK - N_HEADS)))
        w['rg_w_a'], w['rg_w_x'] = rg_w_a[l], rg_w_x[l]
        return w

    g_in, b_in = _row(ln_in_g), _row(ln_in_b)
    sq, dx, grads, dg_in, db_in = _local_step(x2, tgt, p3, weights_of, depth, g_in, b_in, bsz, seq, sched)
    loss = lax.psum(0.5 * jnp.sum(sq) / D_MODEL, ("x", "y", "c"))
    grad_x = dx.reshape(bsz, seq, D_MODEL)

    sched.flush()
    out = {}
    for n in shard_names:
        shp = wts[n].shape
        flat = lambda a: a.reshape(-1, shp[-1])
        if n == 'w_in':
            recv = [sched.received[l][half] for l in range(depth) for half in ('w_in_a', 'w_in_b')]
        else:
            recv = [sched.received[l][n] for l in range(depth)]
        if n in SHARDED_F32:
            recv = [jnp.stack(recv, axis=1).reshape(N_DEV, -1, shp[-1])]
        res = _adamw("adamw_" + n, recv, flat(wts[n]), flat(mom[n]), flat(var[n]))
        out[n] = [r.reshape(shp) for r in res]

    def rep_grad(n):
        if n == 'ln_in_g':
            return dg_in.reshape(-1)
        if n == 'ln_in_b':
            return db_in.reshape(-1)
        return jnp.stack([grads[l][n].reshape(wts[n].shape[1:]) if n != 'b_forget'
                          else grads[l][n][0, :N_HEADS] for l in range(depth)]).reshape(-1)

    sizes = [int(wts[n].size) for n in REPLICATED]
    n_rows = [8 * (-(-sz // (8 * BLK))) for sz in sizes]
    total_rows = -(-sum(n_rows) // (N_DEV * 8)) * (N_DEV * 8)

    def as_rows(v, sz, nr):
        v = v.reshape(-1)
        return (jnp.pad(v, (0, nr * BLK - sz)) if nr * BLK != sz else v).reshape(nr, BLK)

    def pack(vals):
        parts = [as_rows(v, sz, nr) for v, sz, nr in zip(vals, sizes, n_rows)]
        parts.append(jnp.zeros((total_rows - sum(n_rows), BLK), F32))
        return jnp.concatenate(parts, axis=0)

    gp, = _exchange("scatter_small", _Ride([pack([rep_grad(n) for n in REPLICATED]).reshape(
        N_DEV, total_rows // N_DEV, BLK)], gather=False))
    g_slice = _sum_parts("sum_small", gp)
    g_all, = _exchange("gather_small", _Ride([g_slice], gather=True))
    g_rows = g_all.reshape(total_rows, BLK)
    starts = [sum(n_rows[:i]) for i in range(len(n_rows))]
    for n, r0, sz, nr in zip(REPLICATED, starts, sizes, n_rows):
        shp = wts[n].shape
        two_d = (1, sz) if len(shp) == 1 else (-1, shp[-1])
        g_n = g_rows[r0:r0 + nr]
        g_n = (g_n if nr * BLK == sz else g_n.reshape(-1)[:sz]).reshape(two_d)
        res = _adamw("adamw_" + n, [g_n[None]], *[d[n].reshape(two_d) for d in (wts, mom, var)])
        out[n] = [r.reshape(shp) for r in res]

    return (loss, grad_x, *[out[n][k] for k in range(4) for n in WEIGHTS])
```

```python
import functools
import math

import jax
import jax.numpy as jnp
from jax import lax
from jax.experimental import pallas as pl
from jax.experimental.pallas import tpu as pltpu

F32 = jnp.float32
BF16 = jnp.bfloat16

N_DEV = 8
D_MODEL = 1024
N_HEADS = 8
HEAD_DIM = 128
N_BLK = 8
BLK = 128
CONV_W = 4
D_PLE = 256
FF_SH = 704
N_FF = 4
IN_SH = 897
N_IN = 7176
DEPTH = 4
RG_C = 8.0
ALPHA = float((2 * DEPTH) ** 0.25)
LN_EPS = 1e-5
SCALE = 1.0 / math.sqrt(HEAD_DIM)
NEG = -1e30
ADAM_LR, ADAM_B1, ADAM_B2, ADAM_EPS, ADAM_WD, ADAM_STEP = 0.001, 0.9, 0.999, 1e-08, 0.01, 10
QKV = 3 * D_MODEL
OFF_RX, OFF_RY, OFF_GA, OFF_GB = (i * D_MODEL for i in range(4))
V7X_VMEM_LIMIT = 48 * 1024 * 1024

WEIGHTS = ['ln_in_g', 'ln_in_b', 'w_in', 'b_forget', 'conv_w', 'conv_b', 'rg_w_a', 'rg_b_a', 'rg_w_x', 'rg_b_x',
           'rg_lambda', 'w_branch_att', 'w_branch_rnn', 'b_merge', 'w_out', 'ln_mix_g', 'ln_mix_b', 'w_ffn_in',
           'w_ffn_out', 'ln_ffn_g', 'ln_ffn_b', 'w_ple', 'w_ple_gate', 'b_ple_gate', 'ln_ple_g', 'ln_ple_b']
SHARDED_BF16 = ['w_in', 'w_branch_att', 'w_branch_rnn', 'w_out', 'w_ffn_in', 'w_ffn_out', 'w_ple', 'w_ple_gate']
SHARDED_F32 = ['conv_w', 'b_merge']
REPLICATED = [n for n in WEIGHTS if n not in SHARDED_BF16 and n not in SHARDED_F32]

NN = ((1,), (0,))
NT = ((1,), (1,))
TN = ((0,), (0,))


class _Ride:
    def __init__(self, arrays, *, gather, index=None):
        self.arrays, self.gather, self.index = list(arrays), gather, index
        self.result = None

    def out_shapes(self):
        if not self.gather:
            return [jax.ShapeDtypeStruct(a.shape, a.dtype) for a in self.arrays]
        cut = 0 if self.index is None else 1
        return [jax.ShapeDtypeStruct((N_DEV,) + a.shape[cut:], a.dtype) for a in self.arrays]

    def scratch(self):
        n = len(self.arrays)
        return [pltpu.SemaphoreType.DMA((n * N_DEV,)), pltpu.SemaphoreType.DMA((n * N_DEV,)),
                pltpu.SemaphoreType.DMA((n,))]

    def _copy(self, a, k, src, dst, sems, to=None):
        send_sems, recv_sems, _ = sems
        return pltpu.make_async_remote_copy(
            src_ref=src, dst_ref=dst, send_sem=send_sems.at[a * N_DEV + k], recv_sem=recv_sems.at[a * N_DEV + k],
            device_id=_peer(k if to is None else to), device_id_type=pl.DeviceIdType.MESH)

    def begin(self, ins, outs, sems):
        me = _my_id()
        started = []
        for a in range(len(ins)):
            if self.gather:
                src = ins[a] if self.index is None else ins[a].at[self.index]
                started.append(pltpu.make_async_copy(src, outs[a].at[me], sems[2].at[a]))
                started += [self._copy(a, k, src, outs[a].at[me], sems) for k in (1, 2, 4, 6)]
            else:
                started.append(pltpu.make_async_copy(ins[a].at[me], outs[a].at[me], sems[2].at[a]))
                started += [self._copy(a, k, ins[a].at[me ^ k], outs[a].at[me], sems) for k in range(1, N_DEV)]
        for cp in started:
            cp.start()

    def finish(self, ins, outs, sems):
        me = _my_id()
        for a in range(len(ins)):
            if self.gather:
                src = ins[a] if self.index is None else ins[a].at[self.index]
                passed = []
                for k in (2, 4, 6):
                    block = outs[a].at[me ^ k]
                    self._copy(a, k, src, block, sems).wait_recv()
                    passed.append(self._copy(a, k + 1, block, block, sems, to=1))
                    passed[-1].start()
                for k in (1, 2, 4, 6):
                    self._copy(a, k, src, outs[a].at[me], sems).wait_send()
                self._copy(a, 1, src, outs[a].at[me ^ 1], sems).wait_recv()
                for cp in passed:
                    cp.wait()
                pltpu.make_async_copy(src, outs[a].at[me], sems[2].at[a]).wait()
            else:
                pltpu.make_async_copy(ins[a].at[me], outs[a].at[me], sems[2].at[a]).wait()
                for k in range(1, N_DEV):
                    self._copy(a, k, ins[a].at[me ^ k], outs[a].at[me], sems).wait()


def _pcall(body, ride=None, **kw):
    if ride is None:
        return pl.pallas_call(body, **kw)
    n = len(ride.arrays)
    grid = kw['grid']
    single = not isinstance(kw['out_shape'], (list, tuple))
    out_specs = [kw['out_specs']] if single else list(kw['out_specs'])
    out_shape = [kw['out_shape']] if single else list(kw['out_shape'])
    in_specs = list(kw['in_specs'])
    scratch = list(kw.get('scratch_shapes', ()))
    n_in, n_out, n_sc = len(in_specs), len(out_shape), len(scratch)
    hbm = pl.BlockSpec(memory_space=pltpu.HBM)

    def wrapped(*refs):
        ins, xin = refs[:n_in], refs[n_in:n_in + n]
        outs, xout = refs[n_in + n:n_in + n + n_out], refs[n_in + n + n_out:n_in + 2 * n + n_out]
        sc, sems = refs[n_in + 2 * n + n_out:n_in + 2 * n + n_out + n_sc], refs[-3:]
        ids = [pl.program_id(ax) for ax in range(len(grid))]
        first = functools.reduce(jnp.logical_and, [i == 0 for i in ids])
        last = functools.reduce(jnp.logical_and, [i == g - 1 for i, g in zip(ids, grid)])

        pl.when(first)(lambda: ride.begin(xin, xout, sems))
        body(*ins, *outs, *sc)
        pl.when(last)(lambda: ride.finish(xin, xout, sems))

    call = pl.pallas_call(wrapped, name=kw['name'], grid=grid, in_specs=in_specs + [hbm] * n,
                          out_specs=out_specs + [hbm] * n, out_shape=out_shape + ride.out_shapes(),
                          scratch_shapes=scratch + ride.scratch(), compiler_params=kw['compiler_params'])

    def run(*args):
        res = call(*args, *ride.arrays)
        ride.result = list(res[n_out:])
        return res[0] if single else list(res[:n_out])

    return run


def _tile(n, pref, mult=8):
    if n <= pref:
        return n
    t = (pref // mult) * mult
    while t >= mult:
        if n % t == 0:
            return t
        t -= mult
    return n


def _cparams(sem):
    return pltpu.CompilerParams(dimension_semantics=sem, vmem_limit_bytes=V7X_VMEM_LIMIT)


def _mm(name, a, b, *, grid, a_spec, b_spec, o_spec, out_shape, contract, ride=None, add=None):
    nk = grid[-1]
    in_out = out_shape.dtype == F32
    acc_shape = tuple(d for d in o_spec.block_shape if d is not None)

    def body(*refs):
        a_ref, b_ref = refs[0], refs[1]
        add_ref = refs[2] if add is not None else None
        o_ref = refs[3] if add is not None else refs[2]
        acc_ref = o_ref if (in_out or nk == 1) else refs[-1]
        k = pl.program_id(len(grid) - 1)
        part = lax.dot_general(a_ref[...].astype(BF16), b_ref[...].astype(BF16), (contract, ((), ())),
                               preferred_element_type=F32)
        if add_ref is not None:
            part = jnp.where(k == 0, part + add_ref[...], part) if nk > 1 else part + add_ref[...]
        if nk == 1:
            o_ref[...] = part.astype(o_ref.dtype)
            return

        @pl.when(k == 0)
        def _():
            acc_ref[...] = part

        @pl.when(k > 0)
        def _():
            acc_ref[...] += part

        if not in_out:
            @pl.when(k == nk - 1)
            def _():
                o_ref[...] = acc_ref[...].astype(o_ref.dtype)

    sem = ("parallel",) * (len(grid) - 1) + ("arbitrary",)
    scratch = [] if (in_out or nk == 1) else [pltpu.VMEM(acc_shape, F32)]
    in_specs, args = [a_spec, b_spec], [a, b]
    if add is not None:
        in_specs.append(o_spec)
        args.append(add)
    return _pcall(body, ride=ride, name=name, grid=grid, in_specs=in_specs, out_specs=o_spec,
                  out_shape=out_shape, scratch_shapes=scratch, compiler_params=_cparams(sem))(*args)


def _mm_nn(name, a, b, *, b_off=0, n=None, out_dtype=F32, tm=1024, tn=1024, tk=1024, ride=None):
    m, k = a.shape
    n = b.shape[1] if n is None else n
    tm, tn, tk = _tile(m, tm), _tile(n, tn, 128), _tile(k, tk, 128)
    no = b_off // tn
    return _mm(name, a, b, grid=(m // tm, n // tn, k // tk),
               a_spec=pl.BlockSpec((tm, tk), lambda i, j, kk: (i, kk)),
               b_spec=pl.BlockSpec((tk, tn), lambda i, j, kk: (kk, j + no)),
               o_spec=pl.BlockSpec((tm, tn), lambda i, j, kk: (i, j)),
               out_shape=jax.ShapeDtypeStruct((m, n), out_dtype), contract=NN, ride=ride)


def _mm_nt(name, a, b, *, out_dtype=F32, tm=1024, tn=1024, tk=1024, ride=None, add=None):
    m, k = a.shape
    n = b.shape[0]
    tm, tn, tk = _tile(m, tm), _tile(n, tn, 128), _tile(k, tk, 128)
    return _mm(name, a, b, grid=(m // tm, n // tn, k // tk),
               a_spec=pl.BlockSpec((tm, tk), lambda i, j, kk: (i, kk)),
               b_spec=pl.BlockSpec((tn, tk), lambda i, j, kk: (j, kk)),
               o_spec=pl.BlockSpec((tm, tn), lambda i, j, kk: (i, j)),
               out_shape=jax.ShapeDtypeStruct((m, n), out_dtype), contract=NT, ride=ride, add=add)


def _mm_tn(name, a, b, *, a_off=0, m=None, out_dtype=F32, tm=1024, tn=1024, tk=2048, ride=None):
    t, n = b.shape
    m = a.shape[1] if m is None else m
    tm, tn, tk = _tile(m, tm, 128), _tile(n, tn, 128), _tile(t, tk)
    mo = a_off // tm
    return _mm(name, a, b, grid=(m // tm, n // tn, t // tk),
               a_spec=pl.BlockSpec((tk, tm), lambda i, j, kk: (kk, i + mo)),
               b_spec=pl.BlockSpec((tk, tn), lambda i, j, kk: (kk, j)),
               o_spec=pl.BlockSpec((tm, tn), lambda i, j, kk: (i, j)),
               out_shape=jax.ShapeDtypeStruct((m, n), out_dtype), contract=TN, ride=ride)


def _rowwise(name, fn, rows, params, out_rows, out_reds, tm=256):
    rows = [r if isinstance(r, tuple) else (r, 0, r.shape[1]) for r in rows]
    t = rows[0][0].shape[0]
    tm = _tile(t, tm)
    in_specs = []
    for _, off, w in rows:
        in_specs.append(pl.BlockSpec((tm, w), functools.partial(lambda i, cb: (i, cb), cb=off // w)))
    for p in params:
        in_specs.append(pl.BlockSpec((1, p.shape[1]), lambda i: (0, 0)))
    out_specs = [pl.BlockSpec((tm, w), lambda i: (i, 0)) for w, _ in out_rows]
    out_specs += [pl.BlockSpec((1, w), lambda i: (0, 0)) for w in out_reds]
    out_shape = [jax.ShapeDtypeStruct((t, w), dt) for w, dt in out_rows]
    out_shape += [jax.ShapeDtypeStruct((1, w), F32) for w in out_reds]
    nr, npar, nor = len(rows), len(params), len(out_rows)

    def body(*refs):
        ins, outs = refs[:nr + npar], refs[nr + npar:]
        vals = [r[...].astype(F32) for r in ins[:nr]]
        vals += [jnp.broadcast_to(r[...], (tm, r.shape[1])) for r in ins[nr:]]
        res = fn(*vals)
        step = pl.program_id(0)
        for o, v in zip(outs[:nor], res[:nor]):
            o[...] = v.astype(o.dtype)
        for o, v in zip(outs[nor:], res[nor:]):
            _accumulate(o, v, step)

    res = _pcall(body, name=name, grid=(t // tm,), in_specs=in_specs, out_specs=out_specs, out_shape=out_shape,
                 compiler_params=_cparams(("arbitrary",)))(*[r[0] for r in rows], *params)
    return res


def _accumulate(o_ref, v, step):
    @pl.when(step == 0)
    def _():
        o_ref[...] = v

    @pl.when(step > 0)
    def _():
        o_ref[...] += v


def _colsum(v):
    return jnp.sum(v, axis=0, keepdims=True)


def _vjp_rowwise(name, fn, rows, params, cots, n_row_grads, tm=256, dtypes=None):
    nr, npar, nc = len(rows), len(params), len(cots)

    def bwd(*vals):
        prim, par, ct = vals[:nr], vals[nr + nc:], vals[nr:nr + nc]
        _, pull = jax.vjp(fn, *prim, *par)
        grads = pull(tuple(ct) if nc > 1 else ct[0])
        return tuple(grads[:n_row_grads]) + tuple(_colsum(g) for g in grads[nr:])

    dtypes = [F32] * n_row_grads if dtypes is None else dtypes
    widths = [(r[2] if isinstance(r, tuple) else r.shape[1], dt) for r, dt in zip(rows[:n_row_grads], dtypes)]
    return _rowwise(name, bwd, list(rows) + list(cots), params, widths, [p.shape[1] for p in params], tm=tm)


def _ln(s, g, b):
    mu = jnp.mean(s, axis=-1, keepdims=True)
    var = jnp.mean(jnp.square(s - mu), axis=-1, keepdims=True)
    return (s - mu) * lax.rsqrt(var + LN_EPS) * g + b


def _softplus(x):
    return jnp.maximum(x, 0.0) + jnp.log1p(jnp.exp(-jnp.abs(x)))


def _expm1(x):
    series = x * (1.0 + x * (1.0 / 2 + x * (1.0 / 6 + x * (1.0 / 24 + x * (1.0 / 120 + x * (1.0 / 720))))))
    return jnp.where(jnp.abs(x) < 0.25, series, jnp.exp(x) - 1.0)


def _f_resid_ln(h, branch, g, b):
    return _ln(ALPHA * h + branch, g, b)


def _f_ple(h, gp, pe, bpg, g, b):
    return _ln(ALPHA * h + jax.nn.sigmoid(gp + bpg) * pe, g, b)


def _f_merge(ga, gb, ya, yb, bm0, bm1):
    return jax.nn.sigmoid(ga + bm0) * ya + jax.nn.sigmoid(gb + bm1) * yb


def _f_rnn_out(hs, ry):
    return hs * jax.nn.gelu(ry, approximate=True)


def _f_logf(fl, bf):
    return -_softplus(-(fl + bf))


def _f_gate(xc, ra, ia, lam, ba, bx):
    r = jax.nn.sigmoid(ra + ba)
    i = jax.nn.sigmoid(ia + bx)
    log_a = -RG_C * _softplus(-lam) * r
    a = jnp.exp(log_a)
    mult = jnp.sqrt(-_expm1(2.0 * log_a))
    return a, mult * (i * xc)


def _f_act(hg, hu):
    return jax.nn.silu(hg) * hu


ATT_BLOCK = 512


def _scores(q, k, cq, ck, diagonal):
    s = lax.dot_general(q, k, (NT, ((), ())), preferred_element_type=F32) * SCALE
    s = s + cq - ck
    if diagonal:
        row = lax.broadcasted_iota(jnp.int32, s.shape, 0)
        col = lax.broadcasted_iota(jnp.int32, s.shape, 1)
        s = jnp.where(col <= row, s, NEG)
    return s


def _dscores(p, do, o, v):
    dob = do.astype(BF16)
    delta = jnp.sum(dob.astype(F32) * o, axis=1, keepdims=True)
    dp = lax.dot_general(dob, v.astype(BF16), (NT, ((), ())), preferred_element_type=F32)
    return p * (dp - delta)


def _attn_fwd(z, cq, ck, bsz, seq, ride=None):
    t = bsz * seq
    tq = _tile(seq, ATT_BLOCK)
    nq = seq // tq

    def body(q_ref, k_ref, v_ref, cq_ref, ck_ref, o_ref, ob_ref, lse_ref):
        for i in range(nq):
            rows = slice(i * tq, (i + 1) * tq)
            q = q_ref[rows, :].astype(BF16)
            cqi = cq_ref[rows, :]

            def step(j, carry, diagonal, q=q, cqi=cqi):
                m, l, acc = carry
                keys = pl.ds(pl.multiple_of(j * tq, tq), tq)
                s = _scores(q, k_ref[keys, :].astype(BF16), cqi, ck_ref[pl.ds(j, 1), :], diagonal)
                m_new = jnp.maximum(m, jnp.max(s, axis=1, keepdims=True))
                alpha = jnp.exp(m - m_new)
                p = jnp.exp(s - m_new)
                p_hi = p.astype(BF16)
                p_lo = (p - p_hi.astype(F32)).astype(BF16)
                vb = v_ref[keys, :].astype(BF16)
                pv = lax.dot_general(p_hi, vb, (NN, ((), ())), preferred_element_type=F32)
                pv = pv + lax.dot_general(p_lo, vb, (NN, ((), ())), preferred_element_type=F32)
                return m_new, alpha * l + jnp.sum(p, axis=1, keepdims=True), alpha * acc + pv

            carry = (jnp.full((tq, 1), NEG, F32), jnp.zeros((tq, 1), F32), jnp.zeros((tq, HEAD_DIM), F32))
            if i > 0:
                carry = lax.fori_loop(0, i, functools.partial(step, diagonal=False), carry)
            m, l, acc = step(i, carry, True)
            o = acc / l
            o_ref[rows, :] = o
            ob_ref[rows, :] = o.astype(BF16)
            lse_ref[rows, :] = m + jnp.log(l)

    head = (seq, HEAD_DIM)
    in_specs = [
        pl.BlockSpec(head, lambda b, h: (b, h)),
        pl.BlockSpec(head, lambda b, h: (b, N_HEADS + h)),
        pl.BlockSpec(head, lambda b, h: (b, 2 * N_HEADS + h)),
        pl.BlockSpec((None, None, seq, 1), lambda b, h: (b, h, 0, 0)),
        pl.BlockSpec((None, None, nq, tq), lambda b, h: (b, h, 0, 0)),
    ]
    out_specs = [pl.BlockSpec(head, lambda b, h: (b, h)), pl.BlockSpec(head, lambda b, h: (b, h)),
                 pl.BlockSpec((None, None, seq, 1), lambda b, h: (b, h, 0, 0))]
    out_shape = [jax.ShapeDtypeStruct((t, D_MODEL), F32), jax.ShapeDtypeStruct((t, D_MODEL), BF16),
                 jax.ShapeDtypeStruct((bsz, N_HEADS, seq, 1), F32)]
    return _pcall(body, ride=ride, name="attn_fwd", grid=(bsz, N_HEADS), in_specs=in_specs, out_specs=out_specs,
                  out_shape=out_shape, compiler_params=_cparams(("parallel", "parallel")))(
                      z, z, z, cq, ck.reshape(bsz, N_HEADS, nq, tq))


def _attn_bwd(z, att, datt, lse, cq, ck, bsz, seq, ride=None):
    t = bsz * seq
    tq = _tile(seq, ATT_BLOCK)
    nq = seq // tq

    def body(q_ref, k_ref, v_ref, o_ref, do_ref, lse_ref, cq_ref, ck_ref,
             dq_ref, dk_ref, dv_ref, dcq_ref, dck_ref, dq_sc):
        dq_sc[...] = jnp.zeros_like(dq_sc)
        dcq_ref[...] = jnp.zeros_like(dcq_ref)
        for j in range(nq):
            keys = slice(j * tq, (j + 1) * tq)
            kb = k_ref[keys, :].astype(BF16)
            vb = v_ref[keys, :].astype(BF16)
            ckj = ck_ref[j:j + 1, :]

            def step(i, carry, diagonal, kb=kb, vb=vb, ckj=ckj):
                dk, dv, dc = carry
                rows = pl.ds(pl.multiple_of(i * tq, tq), tq)
                qb = q_ref[rows, :].astype(BF16)
                do = do_ref[rows, :]
                s = _scores(qb, kb, cq_ref[rows, :], ckj, diagonal)
                p = jnp.exp(s - lse_ref[rows, :])
                ds = _dscores(p, do, o_ref[rows, :], vb)
                dsb = (ds * SCALE).astype(BF16)
                dq_sc[rows, :] += lax.dot_general(dsb, kb, (NN, ((), ())), preferred_element_type=F32)
                dcq_ref[rows, :] += jnp.sum(ds, axis=1, keepdims=True)
                dv = dv + lax.dot_general(p.astype(BF16), do.astype(BF16), (TN, ((), ())),
                                          preferred_element_type=F32)
                dk = dk + lax.dot_general(dsb, qb, (TN, ((), ())), preferred_element_type=F32)
                return dk, dv, dc - jnp.sum(ds, axis=0, keepdims=True)

            zero = jnp.zeros((tq, HEAD_DIM), F32)
            carry = step(j, (zero, zero, jnp.zeros((1, tq), F32)), True)
            if j + 1 < nq:
                carry = lax.fori_loop(j + 1, nq, functools.partial(step, diagonal=False), carry)
            dk, dv, dck_ref[j:j + 1, :] = carry
            dk_ref[keys, :] = dk.astype(BF16)
            dv_ref[keys, :] = dv.astype(BF16)
        dq_ref[...] = dq_sc[...].astype(BF16)

    head = (seq, HEAD_DIM)
    hmap = lambda b, h: (b, h)
    col = pl.BlockSpec((None, None, seq, 1), lambda b, h: (b, h, 0, 0))
    row = pl.BlockSpec((None, None, nq, tq), lambda b, h: (b, h, 0, 0))
    in_specs = [pl.BlockSpec(head, hmap),
                pl.BlockSpec(head, lambda b, h: (b, N_HEADS + h)),
                pl.BlockSpec(head, lambda b, h: (b, 2 * N_HEADS + h)),
                pl.BlockSpec(head, hmap), pl.BlockSpec(head, hmap), col, col, row]
    big = jax.ShapeDtypeStruct((t, D_MODEL), BF16)
    return _pcall(body, ride=ride, name="attn_bwd", grid=(bsz, N_HEADS), in_specs=in_specs,
                  out_specs=[pl.BlockSpec(head, hmap)] * 3 + [col, row],
                  out_shape=[big, big, big, jax.ShapeDtypeStruct((bsz, N_HEADS, seq, 1), F32),
                             jax.ShapeDtypeStruct((bsz, N_HEADS, nq, tq), F32)],
                  scratch_shapes=[pltpu.VMEM(head, F32)],
                  compiler_params=_cparams(("parallel", "parallel")))(
                      z, z, z, att, datt, lse, cq, ck.reshape(bsz, N_HEADS, nq, tq))


def _scan(name, a, u, bsz, seq, *, reverse, with_prev=False, tb=256):
    c = u.shape[1]
    tb = _tile(seq, tb)
    nb = seq // tb
    has_a = a is not None

    def body(*refs):
        if has_a:
            a_ref, u_ref = refs[0], refs[1]
            rest = refs[2:]
        else:
            u_ref = refs[0]
            rest = refs[1:]
        outs = rest[:2] if with_prev else rest[:1]
        carry_sc, afirst_sc = rest[-2], rest[-1]
        step = pl.program_id(1)

        @pl.when(step == 0)
        def _():
            carry_sc[...] = jnp.zeros_like(carry_sc)
            afirst_sc[...] = jnp.zeros_like(afirst_sc)

        row = lax.broadcasted_iota(jnp.int32, (tb, c), 0)
        uu = u_ref[...]
        if has_a:
            aa = a_ref[...]
            if reverse:
                coef = jnp.where(row < tb - 1, pltpu.roll(aa, tb - 1, 0), afirst_sc[...])
            else:
                coef = aa
        k = 1
        while k < tb:
            shift = tb - k if reverse else k
            keep = (row < tb - k) if reverse else (row >= k)
            uu_sh = jnp.where(keep, pltpu.roll(uu, shift, 0), 0.0)
            if has_a:
                uu = coef * uu_sh + uu
                coef = coef * jnp.where(keep, pltpu.roll(coef, shift, 0), 1.0)
            else:
                uu = uu + uu_sh
            k *= 2
        carry = carry_sc[...]
        h = uu + coef * carry if has_a else uu + carry
        outs[0][...] = h
        if with_prev:
            outs[1][...] = jnp.where(row >= 1, pltpu.roll(h, 1, 0), carry)
        if reverse:
            carry_sc[...] = outs[0][0:1, :]
            if has_a:
                afirst_sc[...] = a_ref[0:1, :]
        else:
            carry_sc[...] = outs[0][tb - 1:tb, :]

    if reverse:
        imap = lambda b, s: (b * nb + nb - 1 - s, 0)
    else:
        imap = lambda b, s: (b * nb + s, 0)
    spec = pl.BlockSpec((tb, c), imap)
    n_in = 2 if has_a else 1
    n_out = 2 if with_prev else 1
    res = _pcall(body, name=name, grid=(bsz, nb), in_specs=[spec] * n_in, out_specs=[spec] * n_out,
                 out_shape=[jax.ShapeDtypeStruct(u.shape, F32)] * n_out,
                 scratch_shapes=[pltpu.VMEM((1, c), F32), pltpu.VMEM((1, c), F32)],
                 compiler_params=_cparams(("parallel", "arbitrary")))(*([a, u] if has_a else [u]))
    return res if with_prev else res[0]


def _conv_fwd(z, w, b, bsz, seq, tb=256):
    c = D_MODEL
    t = bsz * seq
    tb = _tile(seq, tb)
    nb = seq // tb

    def body(x_ref, w_ref, b_ref, o_ref, tail_sc):
        step = pl.program_id(1)

        @pl.when(step == 0)
        def _():
            tail_sc[...] = jnp.zeros_like(tail_sc)

        x = x_ref[...]
        row8 = lax.broadcasted_iota(jnp.int32, (8, c), 0)
        tail = tail_sc[...]
        acc = w_ref[CONV_W - 1:CONV_W, :] * x + b_ref[...]
        for sh in range(1, CONV_W):
            xs = pltpu.roll(x, sh, 0)
            top = jnp.where(row8 < sh, pltpu.roll(tail, sh, 0), xs[0:8, :])
            xs = jnp.concatenate([top, xs[8:, :]], axis=0) if tb > 8 else top
            acc = acc + w_ref[CONV_W - 1 - sh:CONV_W - sh, :] * xs
        o_ref[...] = acc
        tail_sc[...] = x_ref[tb - 8:tb, :]

    return _pcall(body, name="conv_fwd", grid=(bsz, nb),
                  in_specs=[pl.BlockSpec((tb, c), lambda bb, s: (bb * nb + s, OFF_RX // c)),
                            pl.BlockSpec((CONV_W, c), lambda bb, s: (0, 0)),
                            pl.BlockSpec((1, c), lambda bb, s: (0, 0))],
                  out_specs=pl.BlockSpec((tb, c), lambda bb, s: (bb * nb + s, 0)),
                  out_shape=jax.ShapeDtypeStruct((t, c), F32),
                  scratch_shapes=[pltpu.VMEM((8, c), F32)],
                  compiler_params=_cparams(("parallel", "arbitrary")))(z, w, b)


def _conv_bwd(z, dxc, w, bsz, seq, tb=256):
    c = D_MODEL
    t = bsz * seq
    tb = _tile(seq, tb)
    nb = seq // tb

    def body(x_ref, g_ref, w_ref, dx_ref, dw_ref, db_ref, head_sc):
        bb, step = pl.program_id(0), pl.program_id(1)

        @pl.when(step == 0)
        def _():
            head_sc[...] = jnp.zeros_like(head_sc)

        x, g = x_ref[...], g_ref[...]
        row8 = lax.broadcasted_iota(jnp.int32, (8, c), 0)
        head = head_sc[...]
        dx = w_ref[CONV_W - 1:CONV_W, :] * g
        dws = [None] * CONV_W
        dws[CONV_W - 1] = _colsum(g * x)
        for sh in range(1, CONV_W):
            gs = pltpu.roll(g, tb - sh, 0)
            bot = jnp.where(row8 >= 8 - sh, pltpu.roll(head, 8 - sh, 0), gs[tb - 8:tb, :])
            gs = jnp.concatenate([gs[:tb - 8, :], bot], axis=0) if tb > 8 else bot
            dx = dx + w_ref[CONV_W - 1 - sh:CONV_W - sh, :] * gs
            dws[CONV_W - 1 - sh] = _colsum(gs * x)
        dx_ref[...] = dx.astype(dx_ref.dtype)
        first = (bb == 0) & (step == 0)
        dw = jnp.concatenate(dws, axis=0)
        db = _colsum(g)

        @pl.when(first)
        def _():
            dw_ref[...] = dw
            db_ref[...] = db

        @pl.when(jnp.logical_not(first))
        def _():
            dw_ref[...] += dw
            db_ref[...] += db

        head_sc[...] = g_ref[0:8, :]

    rmap = lambda bb, s: (bb * nb + nb - 1 - s, 0)
    return _pcall(body, name="conv_bwd", grid=(bsz, nb),
                  in_specs=[pl.BlockSpec((tb, c), lambda bb, s: (bb * nb + nb - 1 - s, OFF_RX // c)),
                            pl.BlockSpec((tb, c), rmap),
                            pl.BlockSpec((CONV_W, c), lambda bb, s: (0, 0))],
                  out_specs=[pl.BlockSpec((tb, c), rmap),
                             pl.BlockSpec((CONV_W, c), lambda bb, s: (0, 0)),
                             pl.BlockSpec((1, c), lambda bb, s: (0, 0))],
                  out_shape=[jax.ShapeDtypeStruct((t, c), BF16), jax.ShapeDtypeStruct((CONV_W, c), F32),
                             jax.ShapeDtypeStruct((1, c), F32)],
                  scratch_shapes=[pltpu.VMEM((8, c), F32)],
                  compiler_params=_cparams(("arbitrary", "arbitrary")))(z, dxc, w)


def _gate_fwd(xc, w_a, w_x, b_a, b_x, lam, tm=512):
    t = xc.shape[0]
    tm = _tile(t, tm)

    def body(xc_ref, wa_ref, wx_ref, ba_ref, bx_ref, lam_ref, a_ref, u_ref):
        xc_b = xc_ref[...]
        xb = xc_b.astype(BF16)
        ra = lax.dot_general(xb, wa_ref[...].astype(BF16), (NN, ((), ())), preferred_element_type=F32)
        ia = lax.dot_general(xb, wx_ref[...].astype(BF16), (NN, ((), ())), preferred_element_type=F32)
        a, u = _f_gate(xc_b, ra, ia, lam_ref[...], ba_ref[...], bx_ref[...])
        a_ref[...] = a
        u_ref[...] = u

    row = pl.BlockSpec((tm, BLK), lambda n, i: (i, n))
    wsp = pl.BlockSpec((None, BLK, BLK), lambda n, i: (n, 0, 0))
    vec = pl.BlockSpec((1, BLK), lambda n, i: (0, n))
    return _pcall(body, name="gate_fwd", grid=(N_BLK, t // tm), in_specs=[row, wsp, wsp, vec, vec, vec],
                  out_specs=[row, row], out_shape=[jax.ShapeDtypeStruct((t, D_MODEL), F32)] * 2,
                  compiler_params=_cparams(("parallel", "parallel")))(xc, w_a, w_x, b_a, b_x, lam)


def _gate_bwd(xc, w_a, w_x, b_a, b_x, lam, da, du, tm=512, ride=None):
    t = xc.shape[0]
    tm = _tile(t, tm)

    def body(xc_ref, wa_ref, wx_ref, ba_ref, bx_ref, lam_ref, da_ref, du_ref,
             dxc_ref, dwa_ref, dwx_ref, dba_ref, dbx_ref, dlam_ref):
        step = pl.program_id(1)
        xc_b = xc_ref[...]
        xb = xc_b.astype(BF16)
        wa, wx = wa_ref[...].astype(BF16), wx_ref[...].astype(BF16)
        ra = lax.dot_general(xb, wa, (NN, ((), ())), preferred_element_type=F32)
        ia = lax.dot_general(xb, wx, (NN, ((), ())), preferred_element_type=F32)
        full = lambda r: jnp.broadcast_to(r[...], (tm, BLK))
        _, pull = jax.vjp(_f_gate, xc_b, ra, ia, full(lam_ref), full(ba_ref), full(bx_ref))
        dxc, dra, dia, dlam, dba, dbx = pull((da_ref[...], du_ref[...]))
        drb, dib = dra.astype(BF16), dia.astype(BF16)
        dxc = dxc + lax.dot_general(drb, wa, (NT, ((), ())), preferred_element_type=F32)
        dxc = dxc + lax.dot_general(dib, wx, (NT, ((), ())), preferred_element_type=F32)
        dxc_ref[...] = dxc
        _accumulate(dwa_ref, lax.dot_general(xb, drb, (TN, ((), ())), preferred_element_type=F32), step)
        _accumulate(dwx_ref, lax.dot_general(xb, dib, (TN, ((), ())), preferred_element_type=F32), step)
        _accumulate(dba_ref, _colsum(dba), step)
        _accumulate(dbx_ref, _colsum(dbx), step)
        _accumulate(dlam_ref, _colsum(dlam), step)

    row = pl.BlockSpec((tm, BLK), lambda n, i: (i, n))
    wsp = pl.BlockSpec((None, BLK, BLK), lambda n, i: (n, 0, 0))
    vec = pl.BlockSpec((1, BLK), lambda n, i: (0, n))
    wshape = jax.ShapeDtypeStruct((N_BLK, BLK, BLK), F32)
    vshape = jax.ShapeDtypeStruct((1, D_MODEL), F32)
    return _pcall(body, ride=ride, name="gate_bwd", grid=(N_BLK, t // tm),
                  in_specs=[row, wsp, wsp, vec, vec, vec, row, row],
                  out_specs=[row, wsp, wsp, vec, vec, vec],
                  out_shape=[jax.ShapeDtypeStruct((t, D_MODEL), F32), wshape, wshape, vshape, vshape, vshape],
                  compiler_params=_cparams(("parallel", "arbitrary")))(xc, w_a, w_x, b_a, b_x, lam, da, du)


def _ffn_in_act(a, w, tm=1024, ride=None):
    t = a.shape[0]
    tm = _tile(t, tm)

    def body(a_ref, wg_ref, wu_ref, hgu_ref, act_ref):
        ab = a_ref[...].astype(BF16)
        hg = lax.dot_general(ab, wg_ref[...].astype(BF16), (NN, ((), ())), preferred_element_type=F32)
        hu = lax.dot_general(ab, wu_ref[...].astype(BF16), (NN, ((), ())), preferred_element_type=F32)
        hgu_ref[0] = hg
        hgu_ref[1] = hu
        act_ref[...] = _f_act(hg, hu).astype(act_ref.dtype)

    wspec = lambda off: pl.BlockSpec((None, D_MODEL, FF_SH), lambda i, s: (s + off, 0, 0))
    hgu, act = _pcall(body, ride=ride, name="ffn_in", grid=(t // tm, N_FF),
                      in_specs=[pl.BlockSpec((tm, D_MODEL), lambda i, s: (i, 0)), wspec(0), wspec(N_FF)],
                      out_specs=[pl.BlockSpec((2, None, tm, FF_SH), lambda i, s: (0, s, i, 0)),
                                 pl.BlockSpec((None, tm, FF_SH), lambda i, s: (s, i, 0))],
                      out_shape=[jax.ShapeDtypeStruct((2, N_FF, t, FF_SH), F32),
                                 jax.ShapeDtypeStruct((N_FF, t, FF_SH), BF16)],
                      compiler_params=_cparams(("parallel", "parallel")))(a, w, w)
    return hgu.reshape(2 * N_FF, t, FF_SH), act


def _ffn_out_dx_act(d, w, hgu, tm=512, ride=None):
    t = d.shape[0]
    tm = _tile(t, tm)

    def body(d_ref, w_ref, hg_ref, hu_ref, o_ref):
        dact = lax.dot_general(d_ref[...].astype(BF16), w_ref[...].astype(BF16), (NT, ((), ())),
                               preferred_element_type=F32)
        _, pull = jax.vjp(_f_act, hg_ref[...], hu_ref[...])
        dhg, dhu = pull(dact)
        o_ref[0] = dhg.astype(o_ref.dtype)
        o_ref[1] = dhu.astype(o_ref.dtype)

    hspec = lambda off: pl.BlockSpec((None, tm, FF_SH), lambda i, s: (s + off, i, 0))
    res = _pcall(body, ride=ride, name="ffn_out_dx", grid=(t // tm, N_FF),
                 in_specs=[pl.BlockSpec((tm, D_MODEL), lambda i, s: (i, 0)),
                           pl.BlockSpec((None, FF_SH, D_MODEL), lambda i, s: (s, 0, 0)), hspec(0), hspec(N_FF)],
                 out_specs=pl.BlockSpec((2, None, tm, FF_SH), lambda i, s: (0, s, i, 0)),
                 out_shape=jax.ShapeDtypeStruct((2, N_FF, t, FF_SH), BF16),
                 compiler_params=_cparams(("parallel", "parallel")))(d, w, hgu, hgu)
    return res.reshape(2 * N_FF, t, FF_SH)


def _adamw(name, parts, w, m, v, tr=128, ride=None):
    ng = len(parts)
    n_src, r, c = parts[0].shape
    tr = _tile(r, tr)
    nb = r // tr
    bc1 = 1.0 - ADAM_B1 ** ADAM_STEP
    bc2 = 1.0 - ADAM_B2 ** ADAM_STEP

    def body(*refs):
        p_refs = refs[:ng]
        w_ref, m_ref, v_ref, g_ref, d_ref, nm_ref, nv_ref = refs[ng:]
        grp = pl.program_id(0)

        def update(p_ref):
            g = p_ref[0].astype(F32)
            for s in range(1, n_src):
                g = g + p_ref[s].astype(F32)
            nm = ADAM_B1 * m_ref[...] + (1.0 - ADAM_B1) * g
            nv = ADAM_B2 * v_ref[...] + (1.0 - ADAM_B2) * jnp.square(g)
            g_ref[...] = g
            nm_ref[...] = nm
            nv_ref[...] = nv
            d_ref[...] = -ADAM_LR * ((nm / bc1) / (jnp.sqrt(nv / bc2) + ADAM_EPS) + ADAM_WD * w_ref[...])

        for k in range(ng):
            pl.when(grp == k)(functools.partial(update, p_refs[k]))

    p_specs = [pl.BlockSpec((n_src, tr, c), functools.partial(lambda gi, i, k: (0, jnp.where(gi == k, i, 0), 0), k=k))
               for k in range(ng)]
    spec = pl.BlockSpec((tr, c), lambda gi, i: (gi * nb + i, 0))
    return _pcall(body, ride=ride, name=name, grid=(ng, nb), in_specs=p_specs + [spec, spec, spec],
                  out_specs=[spec] * 4, out_shape=[jax.ShapeDtypeStruct((ng * r, c), F32)] * 4,
                  compiler_params=_cparams(("parallel", "parallel")))(*parts, w, m, v)


def _sum_parts(name, parts, tr=256):
    _, r, c = parts.shape
    tr = _tile(r, tr)

    def body(p_ref, o_ref):
        g = p_ref[0]
        for s in range(1, parts.shape[0]):
            g = g + p_ref[s]
        o_ref[...] = g

    return _pcall(body, name=name, grid=(r // tr,),
                  in_specs=[pl.BlockSpec((parts.shape[0], tr, c), lambda i: (0, i, 0))],
                  out_specs=pl.BlockSpec((tr, c), lambda i: (i, 0)),
                  out_shape=jax.ShapeDtypeStruct((r, c), F32), compiler_params=_cparams(("parallel",)))(parts)


def _peer(k):
    x, y, c = lax.axis_index("x"), lax.axis_index("y"), lax.axis_index("c")
    return (x ^ ((k >> 2) & 1), y ^ ((k >> 1) & 1), c ^ (k & 1))


def _my_id():
    return 4 * lax.axis_index("x") + 2 * lax.axis_index("y") + lax.axis_index("c")


def _exchange(name, ride):
    n = len(ride.arrays)

    def body(*refs):
        ride.begin(refs[:n], refs[n:2 * n], refs[2 * n:])
        ride.finish(refs[:n], refs[n:2 * n], refs[2 * n:])

    hbm = pl.BlockSpec(memory_space=pltpu.HBM)
    return _pcall(body, name=name, in_specs=[hbm] * n, out_specs=[hbm] * n, out_shape=ride.out_shapes(),
                  scratch_shapes=ride.scratch())(*ride.arrays)


def _row(v):
    return v.reshape(1, -1)


def _time_major_heads(c, bsz, seq):
    return c.reshape(bsz, seq, BLK)[:, :, :N_HEADS].transpose(0, 2, 1)


def _no_ride(*_):
    return None


TWICE = [(D_MODEL, F32), (D_MODEL, BF16)]


def _both(fn):
    def run(*v):
        y = fn(*v)
        return y, y
    return run


def _layer_fwd(h, hb, p_l, w, bsz, seq, ride_of=_no_ride):
    t = bsz * seq
    zq = _mm_nn("z_proj_qkv", hb, w['w_in7'], n=QKV, out_dtype=BF16, ride=ride_of('z_proj_qkv'))
    zr = _mm_nn("z_proj_rest", hb, w['w_in7'], b_off=QKV, n=4 * D_MODEL, ride=ride_of('z_proj_rest'))
    fl = _mm_nn("f_proj", hb, w['w_inf'])
    logf, = _rowwise("logf_fwd", lambda f, b: (_f_logf(f, b),), [fl], [w['b_forget']], [(BLK, F32)], [])
    c = _scan("cumsum_fwd", None, logf, bsz, seq, reverse=False)
    ct = _time_major_heads(c, bsz, seq)
    cq, ck = ct[..., None], ct[:, :, None, :]
    att, attb, lse = _attn_fwd(zq, cq, ck, bsz, seq, ride=ride_of('attn_fwd'))
    xc = _conv_fwd(zr, w['conv_w'], w['conv_b'], bsz, seq)
    a, u = _gate_fwd(xc, w['rg_w_a'], w['rg_w_x'], w['rg_b_a'], w['rg_b_x'], w['rg_lambda'])
    hs, hprev = _scan("lru_fwd", a, u, bsz, seq, reverse=False, with_prev=True)
    rnn, = _rowwise("rnn_out_fwd", lambda s, y: (_f_rnn_out(s, y),), [hs, (zr, OFF_RY, D_MODEL)], [],
                    [(D_MODEL, BF16)], [])
    ya = _mm_nn("branch_att", attb, w['w_branch_att'])
    yb = _mm_nn("branch_rnn", rnn, w['w_branch_rnn'])
    merged, = _rowwise("merge_fwd", lambda *v: (_f_merge(*v),),
                       [(zr, OFF_GA, D_MODEL), (zr, OFF_GB, D_MODEL), ya, yb], [w['b_merge0'], w['b_merge1']],
                       [(D_MODEL, BF16)], [])
    mix = _mm_nn("mix_out", merged, w['w_out'])
    h1, h1b = _rowwise("ln_mix_fwd", _both(_f_resid_ln), [h, mix], [w['ln_mix_g'], w['ln_mix_b']], TWICE, [])
    tm = _tile(t, 1024)
    hgu, act = _ffn_in_act(h1b, w['w_ffn_in'], ride=ride_of('ffn_in'))
    ffn = _mm("ffn_out", act, w['w_ffn_out'], grid=(t // tm, 1, N_FF),
              a_spec=pl.BlockSpec((None, tm, FF_SH), lambda i, j, s: (s, i, 0)),
              b_spec=pl.BlockSpec((None, FF_SH, D_MODEL), lambda i, j, s: (s, 0, 0)),
              o_spec=pl.BlockSpec((tm, D_MODEL), lambda i, j, s: (i, 0)),
              out_shape=jax.ShapeDtypeStruct((t, D_MODEL), F32), contract=NN, ride=ride_of('ffn_out'))
    h2, h2b = _rowwise("ln_ffn_fwd", _both(_f_resid_ln), [h1, ffn], [w['ln_ffn_g'], w['ln_ffn_b']], TWICE, [])
    gp = _mm_nn("ple_gate", h2b, w['w_ple_gate'])
    pe = _mm_nn("ple_proj", p_l, w['w_ple'])
    h3, h3b = _rowwise("ln_ple_fwd", _both(_f_ple), [h2, gp, pe],
                       [w['b_ple_gate'], w['ln_ple_g'], w['ln_ple_b']], TWICE, [])
    saved = dict(h=h, hb=hb, zq=zq, zr=zr, fl=fl, cq=cq, ck=ck, att=att, attb=attb, lse=lse, xc=xc, a=a,
                 hprev=hprev, hs=hs, rnn=rnn, ya=ya, yb=yb, merged=merged, mix=mix, h1=h1, h1b=h1b, hgu=hgu,
                 act=act, ffn=ffn, h2=h2, h2b=h2b, gp=gp, pe=pe)
    return h3, h3b, saved


def _layer_bwd(dh3, p_l, w, s, bsz, seq, ride_of=_no_ride):
    t = bsz * seq
    g = {}
    dh2, dgp, dpe, g['b_ple_gate'], g['ln_ple_g'], g['ln_ple_b'] = _vjp_rowwise(
        "ln_ple_bwd", _f_ple, [s['h2'], s['gp'], s['pe']], [w['b_ple_gate'], w['ln_ple_g'], w['ln_ple_b']], [dh3], 3,
        dtypes=[F32, BF16, BF16])
    g['w_ple_gate'] = _mm_tn("ple_gate_dw", s['h2b'], dgp, out_dtype=BF16)
    g['w_ple'] = _mm_tn("ple_proj_dw", p_l, dpe, out_dtype=BF16)
    dh2b = _mm_nt("ple_gate_dx", dgp, w['w_ple_gate'])
    dh1, dffn, g['ln_ffn_g'], g['ln_ffn_b'] = _ln_resid_bwd(
        "ln_ffn_bwd", s['h1'], s['ffn'], w['ln_ffn_g'], w['ln_ffn_b'], dh2, dh2b)
    tm = _tile(t, 1024)
    tk = _tile(t, 2048)
    g['w_ffn_out'] = _mm("ffn_out_dw", s['act'], dffn, grid=(N_FF, 1, t // tk),
                         a_spec=pl.BlockSpec((None, tk, FF_SH), lambda ss, j, k: (ss, k, 0)),
                         b_spec=pl.BlockSpec((tk, D_MODEL), lambda ss, j, k: (k, 0)),
                         o_spec=pl.BlockSpec((None, FF_SH, D_MODEL), lambda ss, j, k: (ss, 0, 0)),
                         out_shape=jax.ShapeDtypeStruct((N_FF, FF_SH, D_MODEL), BF16), contract=TN)
    dhgu = _ffn_out_dx_act(dffn, w['w_ffn_out'], s['hgu'], ride=ride_of('ffn_out_dx', g))
    g['w_ffn_in'] = _mm("ffn_in_dw", s['h1b'], dhgu, grid=(2 * N_FF, 1, t // tk),
                        a_spec=pl.BlockSpec((tk, D_MODEL), lambda ss, j, k: (k, 0)),
                        b_spec=pl.BlockSpec((None, tk, FF_SH), lambda ss, j, k: (ss, k, 0)),
                        o_spec=pl.BlockSpec((None, D_MODEL, FF_SH), lambda ss, j, k: (ss, 0, 0)),
                        out_shape=jax.ShapeDtypeStruct((2 * N_FF, D_MODEL, FF_SH), BF16), contract=TN,
                        ride=ride_of('ffn_in_dw', g))
    dh1b = _mm("ffn_in_dx", dhgu, w['w_ffn_in'], grid=(t // tm, 1, 2 * N_FF),
               a_spec=pl.BlockSpec((None, tm, FF_SH), lambda i, j, ss: (ss, i, 0)),
               b_spec=pl.BlockSpec((None, D_MODEL, FF_SH), lambda i, j, ss: (ss, 0, 0)),
               o_spec=pl.BlockSpec((tm, D_MODEL), lambda i, j, ss: (i, 0)),
               out_shape=jax.ShapeDtypeStruct((t, D_MODEL), F32), contract=NT, ride=ride_of('ffn_in_dx', g))
    dh, dmix, g['ln_mix_g'], g['ln_mix_b'] = _ln_resid_bwd(
        "ln_mix_bwd", s['h'], s['mix'], w['ln_mix_g'], w['ln_mix_b'], dh1, dh1b)
    g['w_out'] = _mm_tn("mix_out_dw", s['merged'], dmix, out_dtype=BF16)
    dmerged = _mm_nt("mix_out_dx", dmix, w['w_out'])
    z = s['zr']
    dga, dgb, dya, dyb, dbm0, dbm1 = _vjp_rowwise(
        "merge_bwd", _f_merge, [(z, OFF_GA, D_MODEL), (z, OFF_GB, D_MODEL), s['ya'], s['yb']],
        [w['b_merge0'], w['b_merge1']], [dmerged], 4, dtypes=[BF16] * 4)
    g['b_merge'] = jnp.concatenate([dbm0, dbm1], axis=0)
    g['w_branch_att'] = _mm_tn("branch_att_dw", s['attb'], dya, out_dtype=BF16)
    g['w_branch_rnn'] = _mm_tn("branch_rnn_dw", s['rnn'], dyb, out_dtype=BF16)
    datt = _mm_nt("branch_att_dx", dya, w['w_branch_att'], out_dtype=BF16)
    drnn = _mm_nt("branch_rnn_dx", dyb, w['w_branch_rnn'])
    dhs, dry = _vjp_rowwise("rnn_out_bwd", _f_rnn_out, [s['hs'], (z, OFF_RY, D_MODEL)], [], [drnn], 2,
                            dtypes=[F32, BF16])
    lam = _scan("lru_bwd", s['a'], dhs, bsz, seq, reverse=True)
    da, = _rowwise("lru_da", lambda l, hp: (l * hp,), [lam, s['hprev']], [], [(D_MODEL, F32)], [])
    dxc, g['rg_w_a'], g['rg_w_x'], g['rg_b_a'], g['rg_b_x'], g['rg_lambda'] = _gate_bwd(
        s['xc'], w['rg_w_a'], w['rg_w_x'], w['rg_b_a'], w['rg_b_x'], w['rg_lambda'], da, lam,
        ride=ride_of('gate_bwd', g))
    drx, g['conv_w'], g['conv_b'] = _conv_bwd(z, dxc, w['conv_w'], bsz, seq)
    dq, dk, dv, dcq, dck = _attn_bwd(s['zq'], s['att'], datt, s['lse'], s['cq'], s['ck'], bsz, seq,
                                     ride=ride_of('attn_bwd', g))
    dc = (dcq[:, :, :, 0] + dck.reshape(bsz, N_HEADS, seq)).transpose(0, 2, 1)
    dc = jnp.pad(dc, ((0, 0), (0, 0), (0, BLK - N_HEADS))).reshape(t, BLK)
    dlogf = _scan("cumsum_bwd", None, dc, bsz, seq, reverse=True)
    dfl, g['b_forget'] = _vjp_rowwise("logf_bwd", _f_logf, [s['fl']], [w['b_forget']], [dlogf], 1, dtypes=[BF16])
    dz = jnp.concatenate([dq, dk, dv, drx, dry, dga, dgb], axis=1)
    g['w_in7'] = _mm_tn("z_proj_dw", s['hb'], dz, out_dtype=BF16)
    g['w_inf'] = _mm_tn("f_proj_dw", s['hb'], dfl, out_dtype=BF16)
    dh = _mm_nt("z_proj_dx", dz, w['w_in7'], ride=ride_of('z_proj_dx', g), add=dh)
    dh = _mm_nt("f_proj_dx", dfl, w['w_inf'], add=dh)
    return dh, g


def _ln_resid_bwd(name, h, branch, gam, bet, d0, d1):
    def bwd(hv, bv, d0v, d1v, gv, btv):
        _, pull = jax.vjp(_f_resid_ln, hv, bv, gv, btv)
        dh, db, dg, dbt = pull(d0v + d1v)
        return dh, db, _colsum(dg), _colsum(dbt)

    return _rowwise(name, bwd, [h, branch, d0, d1], [gam, bet], [(D_MODEL, F32), (D_MODEL, BF16)],
                    [D_MODEL, D_MODEL])


class _Schedule:
    FWD = {'z_proj_qkv': ['w_ffn_out'], 'z_proj_rest': ['w_branch_att', 'w_branch_rnn', 'w_out', 'w_ple_gate'],
           'attn_fwd': ['w_in'], 'ffn_in': ['w_ffn_in'], 'ffn_out': ['w_ple', 'conv_w', 'b_merge']}
    BWD = {'ffn_out_dx': ['w_ffn_out'], 'ffn_in_dw': ['w_ple_gate', 'w_ple'],
           'gate_bwd': ['w_out', 'w_branch_att', 'w_branch_rnn'],
           'attn_bwd': ['w_ffn_in', 'conv_w', 'b_merge']}

    def __init__(self, shards, depth):
        self.shards, self.depth = shards, depth
        self.gathered = [{} for _ in range(depth)]
        self.received = [{} for _ in range(depth)]
        self.pending = []
        self.deferred = None

    def gather_ride(self, layer, kernel_name):
        if layer + 1 >= self.depth:
            return None
        names = self.FWD[kernel_name]
        ride = _Ride([self.shards[n] for n in names], gather=True, index=layer + 1)
        self.pending.append((ride, names, self.gathered[layer + 1]))
        return ride

    def _scatter(self, arrays, names, layer):
        ride = _Ride(arrays, gather=False)
        self.pending.append((ride, names, self.received[layer]))
        return ride

    def scatter_ride(self, layer, kernel_name, grads):
        if kernel_name == 'z_proj_dx':
            whole = _by_destination('w_in', grads)
            half = whole.shape[1] // 2
            self.deferred = (whole[:, half:], layer)
            return self._scatter([whole[:, :half]], ['w_in_a'], layer)
        if kernel_name == 'ffn_in_dx':
            if self.deferred is None:
                return None
            (late, from_layer), self.deferred = self.deferred, None
            return self._scatter([late], ['w_in_b'], from_layer)
        names = self.BWD[kernel_name]
        return self._scatter([_by_destination(n, grads) for n in names], names, layer)

    def collect(self):
        for ride, names, dst in self.pending:
            dst.update(zip(names, ride.result))
        self.pending = []


def _local_step(x2, tgt, p3, weights_of, depth, g_in, b_in, bsz, seq, sched=None):
    h, hb = _rowwise("ln_in_fwd", _both(_ln), [x2], [g_in, b_in], TWICE, [])
    p3 = p3.astype(BF16)
    saved, layer_w = [], []
    for l in range(depth):
        layer_w.append(weights_of(l))
        ride_of = functools.partial(sched.gather_ride, l) if sched else _no_ride
        h, hb, s = _layer_fwd(h, hb, p3[l], layer_w[l], bsz, seq, ride_of)
        if sched:
            sched.collect()
        saved.append(s)

    def loss_fn(y, tv):
        err = y - tv
        return err * (1.0 / D_MODEL), _colsum(jnp.square(err))

    dh, sq = _rowwise("loss", loss_fn, [h, tgt], [], [(D_MODEL, F32)], [D_MODEL])
    grads = [None] * depth
    for l in reversed(range(depth)):
        ride_of = functools.partial(sched.scatter_ride, l) if sched else _no_ride
        dh, grads[l] = _layer_bwd(dh, p3[l], layer_w[l], saved[l], bsz, seq, ride_of)
        if sched:
            sched.collect()
    dx, dg_in, db_in = _vjp_rowwise("ln_in_bwd", _ln, [x2], [g_in, b_in], [dh], 1)
    return sq, dx, grads, dg_in, db_in


def _layer_weights(full):
    w = {}
    wt = full['w_in'].transpose(1, 0, 2).reshape(D_MODEL, N_IN)
    w['w_in7'] = jnp.concatenate([wt[:, :3 * D_MODEL], wt[:, 3 * D_MODEL + N_HEADS:]], axis=1)
    w['w_inf'] = jnp.pad(wt[:, 3 * D_MODEL:3 * D_MODEL + N_HEADS], ((0, 0), (0, BLK - N_HEADS)))
    for n in ['w_branch_att', 'w_branch_rnn', 'w_out', 'w_ple_gate']:
        w[n] = full[n].reshape(D_MODEL, D_MODEL)
    w['w_ffn_in'] = full['w_ffn_in']
    w['w_ffn_out'] = full['w_ffn_out'].reshape(N_FF, FF_SH, D_MODEL)
    w['w_ple'] = full['w_ple'].transpose(1, 0, 2).reshape(D_PLE, D_MODEL)
    w['conv_w'] = full['conv_w'].transpose(1, 0, 2).reshape(CONV_W, D_MODEL)
    bm = full['b_merge'].transpose(1, 0, 2).reshape(2, D_MODEL)
    w['b_merge0'], w['b_merge1'] = bm[0:1], bm[1:2]
    return w


def _by_destination(name, gw):
    if name == 'w_in':
        g7, gf = gw['w_in7'], gw['w_inf']
        true = jnp.concatenate([g7[:, :3 * D_MODEL], gf[:, :N_HEADS], g7[:, 3 * D_MODEL:]], axis=1)
        return true.reshape(D_MODEL, N_DEV, IN_SH).transpose(1, 0, 2)
    g = gw[name]
    if name in ('w_branch_att', 'w_branch_rnn', 'w_out', 'w_ple_gate'):
        return g.reshape(N_DEV, D_MODEL // N_DEV, D_MODEL)
    if name == 'w_ffn_in':
        return g
    if name == 'w_ffn_out':
        return g.reshape(N_DEV, N_FF * FF_SH // N_DEV, D_MODEL)
    return g.reshape(g.shape[0], N_DEV, BLK).transpose(1, 0, 2)


def kernel(x, p, ln_in_g, ln_in_b, w_in, b_forget, conv_w, conv_b, rg_w_a, rg_b_a, rg_w_x, rg_b_x, rg_lambda, w_branch_att, w_branch_rnn, b_merge, w_out, ln_mix_g, ln_mix_b, w_ffn_in, w_ffn_out, ln_ffn_g, ln_ffn_b, w_ple, w_ple_gate, b_ple_gate, ln_ple_g, ln_ple_b, loss_target, m_ln_in_g, m_ln_in_b, m_w_in, m_b_forget, m_conv_w, m_conv_b, m_rg_w_a, m_rg_b_a, m_rg_w_x, m_rg_b_x, m_rg_lambda, m_w_branch_att, m_w_branch_rnn, m_b_merge, m_w_out, m_ln_mix_g, m_ln_mix_b, m_w_ffn_in, m_w_ffn_out, m_ln_ffn_g, m_ln_ffn_b, m_w_ple, m_w_ple_gate, m_b_ple_gate, m_ln_ple_g, m_ln_ple_b, v_ln_in_g, v_ln_in_b, v_w_in, v_b_forget, v_conv_w, v_conv_b, v_rg_w_a, v_rg_b_a, v_rg_w_x, v_rg_b_x, v_rg_lambda, v_w_branch_att, v_w_branch_rnn, v_b_merge, v_w_out, v_ln_mix_g, v_ln_mix_b, v_w_ffn_in, v_w_ffn_out, v_ln_ffn_g, v_ln_ffn_b, v_w_ple, v_w_ple_gate, v_b_ple_gate, v_ln_ple_g, v_ln_ple_b):
    env = dict(locals())
    wts = {n: env[n] for n in WEIGHTS}
    mom = {n: env['m_' + n] for n in WEIGHTS}
    var = {n: env['v_' + n] for n in WEIGHTS}
    bsz, seq, _ = x.shape
    depth = w_in.shape[0]
    t = bsz * seq
    x2, tgt = x.reshape(t, D_MODEL), loss_target.reshape(t, D_MODEL)
    p3 = p.reshape(depth, t, D_PLE)

    shard_names = SHARDED_BF16 + SHARDED_F32
    shards = {n: wts[n].astype(BF16) for n in SHARDED_BF16}
    shards.update({n: wts[n] for n in SHARDED_F32})
    sched = _Schedule(shards, depth)
    first = _Ride([shards[n] for n in shard_names], gather=True, index=0)
    sched.gathered[0] = dict(zip(shard_names, _exchange("gather_layer0", first)))

    def weights_of(l):
        w = _layer_weights(sched.gathered[l])
        for n in ['conv_b', 'rg_b_a', 'rg_b_x', 'rg_lambda', 'ln_mix_g', 'ln_mix_b', 'ln_ffn_g', 'ln_ffn_b',
                  'b_ple_gate', 'ln_ple_g', 'ln_ple_b']:
            w[n] = _row(wts[n][l])
        w['b_forget'] = jnp.pad(_row(b_forget[l]), ((0, 0), (0, BLK - N_HEADS)))
        w['rg_w_a'], w['rg_w_x'] = rg_w_a[l], rg_w_x[l]
        return w

    g_in, b_in = _row(ln_in_g), _row(ln_in_b)
    sq, dx, grads, dg_in, db_in = _local_step(x2, tgt, p3, weights_of, depth, g_in, b_in, bsz, seq, sched)
    loss = lax.psum(0.5 * jnp.sum(sq) / D_MODEL, ("x", "y", "c"))
    grad_x = dx.reshape(bsz, seq, D_MODEL)

    out = {}

    def update(n, ride=None):
        shp = wts[n].shape
        flat = lambda a: a.reshape(-1, shp[-1])
        if n == 'w_in':
            recv = [sched.received[l][half] for l in range(depth) for half in ('w_in_a', 'w_in_b')]
        else:
            recv = [sched.received[l][n] for l in range(depth)]
        if n in SHARDED_F32:
            recv = [jnp.stack(recv, axis=1).reshape(N_DEV, -1, shp[-1])]
        res = _adamw("adamw_" + n, recv, flat(wts[n]), flat(mom[n]), flat(var[n]), ride=ride)
        out[n] = [r.reshape(shp) for r in res]

    def rep_grad(n):
        if n == 'ln_in_g':
            return dg_in.reshape(-1)
        if n == 'ln_in_b':
            return db_in.reshape(-1)
        return jnp.stack([grads[l][n].reshape(wts[n].shape[1:]) if n != 'b_forget'
                          else grads[l][n][0, :N_HEADS] for l in range(depth)]).reshape(-1)

    sizes = [int(wts[n].size) for n in REPLICATED]
    n_rows = [8 * (-(-sz // (8 * BLK))) for sz in sizes]
    total_rows = -(-sum(n_rows) // (N_DEV * 8)) * (N_DEV * 8)

    def as_rows(v, sz, nr):
        v = v.reshape(-1)
        return (jnp.pad(v, (0, nr * BLK - sz)) if nr * BLK != sz else v).reshape(nr, BLK)

    def pack(vals):
        parts = [as_rows(v, sz, nr) for v, sz, nr in zip(vals, sizes, n_rows)]
        parts.append(jnp.zeros((total_rows - sum(n_rows), BLK), F32))
        return jnp.concatenate(parts, axis=0)

    late, from_layer = sched.deferred
    last_w_in = _Ride([late], gather=False)
    update('w_ffn_in', ride=last_w_in)
    sched.received[from_layer]['w_in_b'], = last_w_in.result
    scatter_small = _Ride([pack([rep_grad(n) for n in REPLICATED]).reshape(N_DEV, total_rows // N_DEV, BLK)],
                          gather=False)
    update('w_ffn_out', ride=scatter_small)
    gather_small = _Ride([_sum_parts("sum_small", scatter_small.result[0])], gather=True)
    update('w_in', ride=gather_small)
    for n in shard_names:
        if n not in out:
            update(n)
    g_rows = gather_small.result[0].reshape(total_rows, BLK)
    starts = [sum(n_rows[:i]) for i in range(len(n_rows))]
    for n, r0, sz, nr in zip(REPLICATED, starts, sizes, n_rows):
        shp = wts[n].shape
        two_d = (1, sz) if len(shp) == 1 else (-1, shp[-1])
        g_n = g_rows[r0:r0 + nr]
        g_n = (g_n if nr * BLK == sz else g_n.reshape(-1)[:sz]).reshape(two_d)
        res = _adamw("adamw_" + n, [g_n[None]], *[d[n].reshape(two_d) for d in (wts, mom, var)])
        out[n] = [r.reshape(shp) for r in res]

    return (loss, grad_x, *[out[n][k] for k in range(4) for n in WEIGHTS])
```

```python
import functools
import math

import jax
import jax.numpy as jnp
from jax import lax
from jax.experimental import pallas as pl
from jax.experimental.pallas import tpu as pltpu

F32 = jnp.float32
BF16 = jnp.bfloat16

N_DEV = 8
D_MODEL = 1024
N_HEADS = 8
HEAD_DIM = 128
N_BLK = 8
BLK = 128
CONV_W = 4
D_PLE = 256
FF_SH = 704
N_FF = 4
IN_SH = 897
N_IN = 7176
DEPTH = 4
RG_C = 8.0
ALPHA = float((2 * DEPTH) ** 0.25)
LN_EPS = 1e-5
SCALE = 1.0 / math.sqrt(HEAD_DIM)
NEG = -1e30
ADAM_LR, ADAM_B1, ADAM_B2, ADAM_EPS, ADAM_WD, ADAM_STEP = 0.001, 0.9, 0.999, 1e-08, 0.01, 10
QKV = 3 * D_MODEL
OFF_RX, OFF_RY, OFF_GA, OFF_GB = (i * D_MODEL for i in range(4))
V7X_VMEM_LIMIT = 48 * 1024 * 1024

WEIGHTS = ['ln_in_g', 'ln_in_b', 'w_in', 'b_forget', 'conv_w', 'conv_b', 'rg_w_a', 'rg_b_a', 'rg_w_x', 'rg_b_x',
           'rg_lambda', 'w_branch_att', 'w_branch_rnn', 'b_merge', 'w_out', 'ln_mix_g', 'ln_mix_b', 'w_ffn_in',
           'w_ffn_out', 'ln_ffn_g', 'ln_ffn_b', 'w_ple', 'w_ple_gate', 'b_ple_gate', 'ln_ple_g', 'ln_ple_b']
SHARDED_BF16 = ['w_in', 'w_branch_att', 'w_branch_rnn', 'w_out', 'w_ffn_in', 'w_ffn_out', 'w_ple', 'w_ple_gate']
SHARDED_F32 = ['conv_w', 'b_merge']
REPLICATED = [n for n in WEIGHTS if n not in SHARDED_BF16 and n not in SHARDED_F32]

NN = ((1,), (0,))
NT = ((1,), (1,))
TN = ((0,), (0,))


class _Ride:
    def __init__(self, arrays, *, gather, index=None):
        self.arrays, self.gather, self.index = list(arrays), gather, index
        self.result = None

    def out_shapes(self):
        if not self.gather:
            return [jax.ShapeDtypeStruct(a.shape, a.dtype) for a in self.arrays]
        cut = 0 if self.index is None else 1
        return [jax.ShapeDtypeStruct((N_DEV,) + a.shape[cut:], a.dtype) for a in self.arrays]

    def scratch(self):
        n = len(self.arrays)
        return [pltpu.SemaphoreType.DMA((n * N_DEV,)), pltpu.SemaphoreType.DMA((n * N_DEV,)),
                pltpu.SemaphoreType.DMA((n,))]

    def _copy(self, a, k, src, dst, sems, to=None):
        send_sems, recv_sems, _ = sems
        return pltpu.make_async_remote_copy(
            src_ref=src, dst_ref=dst, send_sem=send_sems.at[a * N_DEV + k], recv_sem=recv_sems.at[a * N_DEV + k],
            device_id=_peer(k if to is None else to), device_id_type=pl.DeviceIdType.MESH)

    def begin(self, ins, outs, sems):
        me = _my_id()
        started = []
        for a in range(len(ins)):
            if self.gather:
                src = ins[a] if self.index is None else ins[a].at[self.index]
                started.append(pltpu.make_async_copy(src, outs[a].at[me], sems[2].at[a]))
                started += [self._copy(a, k, src, outs[a].at[me], sems) for k in (1, 2, 4, 6)]
            else:
                started.append(pltpu.make_async_copy(ins[a].at[me], outs[a].at[me], sems[2].at[a]))
                started += [self._copy(a, k, ins[a].at[me ^ k], outs[a].at[me], sems) for k in range(1, N_DEV)]
        for cp in started:
            cp.start()

    def finish(self, ins, outs, sems):
        me = _my_id()
        for a in range(len(ins)):
            if self.gather:
                src = ins[a] if self.index is None else ins[a].at[self.index]
                passed = []
                for k in (2, 4, 6):
                    block = outs[a].at[me ^ k]
                    self._copy(a, k, src, block, sems).wait_recv()
                    passed.append(self._copy(a, k + 1, block, block, sems, to=1))
                    passed[-1].start()
                for k in (1, 2, 4, 6):
                    self._copy(a, k, src, outs[a].at[me], sems).wait_send()
                self._copy(a, 1, src, outs[a].at[me ^ 1], sems).wait_recv()
                for cp in passed:
                    cp.wait()
                pltpu.make_async_copy(src, outs[a].at[me], sems[2].at[a]).wait()
            else:
                pltpu.make_async_copy(ins[a].at[me], outs[a].at[me], sems[2].at[a]).wait()
                for k in range(1, N_DEV):
                    self._copy(a, k, ins[a].at[me ^ k], outs[a].at[me], sems).wait()


def _pcall(body, ride=None, **kw):
    if ride is None:
        return pl.pallas_call(body, **kw)
    n = len(ride.arrays)
    grid = kw['grid']
    single = not isinstance(kw['out_shape'], (list, tuple))
    out_specs = [kw['out_specs']] if single else list(kw['out_specs'])
    out_shape = [kw['out_shape']] if single else list(kw['out_shape'])
    in_specs = list(kw['in_specs'])
    scratch = list(kw.get('scratch_shapes', ()))
    n_in, n_out, n_sc = len(in_specs), len(out_shape), len(scratch)
    hbm = pl.BlockSpec(memory_space=pltpu.HBM)

    def wrapped(*refs):
        ins, xin = refs[:n_in], refs[n_in:n_in + n]
        outs, xout = refs[n_in + n:n_in + n + n_out], refs[n_in + n + n_out:n_in + 2 * n + n_out]
        sc, sems = refs[n_in + 2 * n + n_out:n_in + 2 * n + n_out + n_sc], refs[-3:]
        ids = [pl.program_id(ax) for ax in range(len(grid))]
        first = functools.reduce(jnp.logical_and, [i == 0 for i in ids])
        last = functools.reduce(jnp.logical_and, [i == g - 1 for i, g in zip(ids, grid)])

        pl.when(first)(lambda: ride.begin(xin, xout, sems))
        body(*ins, *outs, *sc)
        pl.when(last)(lambda: ride.finish(xin, xout, sems))

    call = pl.pallas_call(wrapped, name=kw['name'], grid=grid, in_specs=in_specs + [hbm] * n,
                          out_specs=out_specs + [hbm] * n, out_shape=out_shape + ride.out_shapes(),
                          scratch_shapes=scratch + ride.scratch(), compiler_params=kw['compiler_params'])

    def run(*args):
        res = call(*args, *ride.arrays)
        ride.result = list(res[n_out:])
        return res[0] if single else list(res[:n_out])

    return run


def _tile(n, pref, mult=8):
    if n <= pref:
        return n
    t = (pref // mult) * mult
    while t >= mult:
        if n % t == 0:
            return t
        t -= mult
    return n


def _cparams(sem):
    return pltpu.CompilerParams(dimension_semantics=sem, vmem_limit_bytes=V7X_VMEM_LIMIT)


def _mm(name, a, b, *, grid, a_spec, b_spec, o_spec, out_shape, contract, ride=None, add=None):
    nk = grid[-1]
    in_out = out_shape.dtype == F32
    acc_shape = tuple(d for d in o_spec.block_shape if d is not None)

    def body(*refs):
        a_ref, b_ref = refs[0], refs[1]
        add_ref = refs[2] if add is not None else None
        o_ref = refs[3] if add is not None else refs[2]
        acc_ref = o_ref if (in_out or nk == 1) else refs[-1]
        k = pl.program_id(len(grid) - 1)
        part = lax.dot_general(a_ref[...].astype(BF16), b_ref[...].astype(BF16), (contract, ((), ())),
                               preferred_element_type=F32)
        if add_ref is not None:
            part = jnp.where(k == 0, part + add_ref[...], part) if nk > 1 else part + add_ref[...]
        if nk == 1:
            o_ref[...] = part.astype(o_ref.dtype)
            return

        @pl.when(k == 0)
        def _():
            acc_ref[...] = part

        @pl.when(k > 0)
        def _():
            acc_ref[...] += part

        if not in_out:
            @pl.when(k == nk - 1)
            def _():
                o_ref[...] = acc_ref[...].astype(o_ref.dtype)

    sem = ("parallel",) * (len(grid) - 1) + ("arbitrary",)
    scratch = [] if (in_out or nk == 1) else [pltpu.VMEM(acc_shape, F32)]
    in_specs, args = [a_spec, b_spec], [a, b]
    if add is not None:
        in_specs.append(o_spec)
        args.append(add)
    return _pcall(body, ride=ride, name=name, grid=grid, in_specs=in_specs, out_specs=o_spec,
                  out_shape=out_shape, scratch_shapes=scratch, compiler_params=_cparams(sem))(*args)


def _mm_nn(name, a, b, *, b_off=0, n=None, out_dtype=F32, tm=1024, tn=1024, tk=1024, ride=None):
    m, k = a.shape
    n = b.shape[1] if n is None else n
    tm, tn, tk = _tile(m, tm), _tile(n, tn, 128), _tile(k, tk, 128)
    no = b_off // tn
    return _mm(name, a, b, grid=(m // tm, n // tn, k // tk),
               a_spec=pl.BlockSpec((tm, tk), lambda i, j, kk: (i, kk)),
               b_spec=pl.BlockSpec((tk, tn), lambda i, j, kk: (kk, j + no)),
               o_spec=pl.BlockSpec((tm, tn), lambda i, j, kk: (i, j)),
               out_shape=jax.ShapeDtypeStruct((m, n), out_dtype), contract=NN, ride=ride)


def _mm_nt(name, a, b, *, out_dtype=F32, tm=1024, tn=1024, tk=1024, ride=None, add=None):
    m, k = a.shape
    n = b.shape[0]
    tm, tn, tk = _tile(m, tm), _tile(n, tn, 128), _tile(k, tk, 128)
    return _mm(name, a, b, grid=(m // tm, n // tn, k // tk),
               a_spec=pl.BlockSpec((tm, tk), lambda i, j, kk: (i, kk)),
               b_spec=pl.BlockSpec((tn, tk), lambda i, j, kk: (j, kk)),
               o_spec=pl.BlockSpec((tm, tn), lambda i, j, kk: (i, j)),
               out_shape=jax.ShapeDtypeStruct((m, n), out_dtype), contract=NT, ride=ride, add=add)


def _mm_tn(name, a, b, *, a_off=0, m=None, out_dtype=F32, tm=1024, tn=1024, tk=2048, ride=None):
    t, n = b.shape
    m = a.shape[1] if m is None else m
    tm, tn, tk = _tile(m, tm, 128), _tile(n, tn, 128), _tile(t, tk)
    mo = a_off // tm
    return _mm(name, a, b, grid=(m // tm, n // tn, t // tk),
               a_spec=pl.BlockSpec((tk, tm), lambda i, j, kk: (kk, i + mo)),
               b_spec=pl.BlockSpec((tk, tn), lambda i, j, kk: (kk, j)),
               o_spec=pl.BlockSpec((tm, tn), lambda i, j, kk: (i, j)),
               out_shape=jax.ShapeDtypeStruct((m, n), out_dtype), contract=TN, ride=ride)


def _rowwise(name, fn, rows, params, out_rows, out_reds, tm=256):
    rows = [r if isinstance(r, tuple) else (r, 0, r.shape[1]) for r in rows]
    t = rows[0][0].shape[0]
    tm = _tile(t, tm)
    in_specs = []
    for _, off, w in rows:
        in_specs.append(pl.BlockSpec((tm, w), functools.partial(lambda i, cb: (i, cb), cb=off // w)))
    for p in params:
        in_specs.append(pl.BlockSpec((1, p.shape[1]), lambda i: (0, 0)))
    out_specs = [pl.BlockSpec((tm, w), lambda i: (i, 0)) for w, _ in out_rows]
    out_specs += [pl.BlockSpec((1, w), lambda i: (0, 0)) for w in out_reds]
    out_shape = [jax.ShapeDtypeStruct((t, w), dt) for w, dt in out_rows]
    out_shape += [jax.ShapeDtypeStruct((1, w), F32) for w in out_reds]
    nr, npar, nor = len(rows), len(params), len(out_rows)

    def body(*refs):
        ins, outs = refs[:nr + npar], refs[nr + npar:]
        vals = [r[...].astype(F32) for r in ins[:nr]]
        vals += [jnp.broadcast_to(r[...], (tm, r.shape[1])) for r in ins[nr:]]
        res = fn(*vals)
        step = pl.program_id(0)
        for o, v in zip(outs[:nor], res[:nor]):
            o[...] = v.astype(o.dtype)
        for o, v in zip(outs[nor:], res[nor:]):
            _accumulate(o, v, step)

    res = _pcall(body, name=name, grid=(t // tm,), in_specs=in_specs, out_specs=out_specs, out_shape=out_shape,
                 compiler_params=_cparams(("arbitrary",)))(*[r[0] for r in rows], *params)
    return res


def _accumulate(o_ref, v, step):
    @pl.when(step == 0)
    def _():
        o_ref[...] = v

    @pl.when(step > 0)
    def _():
        o_ref[...] += v


def _colsum(v):
    return jnp.sum(v, axis=0, keepdims=True)


def _vjp_rowwise(name, fn, rows, params, cots, n_row_grads, tm=256, dtypes=None):
    nr, npar, nc = len(rows), len(params), len(cots)

    def bwd(*vals):
        prim, par, ct = vals[:nr], vals[nr + nc:], vals[nr:nr + nc]
        _, pull = jax.vjp(fn, *prim, *par)
        grads = pull(tuple(ct) if nc > 1 else ct[0])
        return tuple(grads[:n_row_grads]) + tuple(_colsum(g) for g in grads[nr:])

    dtypes = [F32] * n_row_grads if dtypes is None else dtypes
    widths = [(r[2] if isinstance(r, tuple) else r.shape[1], dt) for r, dt in zip(rows[:n_row_grads], dtypes)]
    return _rowwise(name, bwd, list(rows) + list(cots), params, widths, [p.shape[1] for p in params], tm=tm)


def _ln(s, g, b):
    mu = jnp.mean(s, axis=-1, keepdims=True)
    var = jnp.mean(jnp.square(s - mu), axis=-1, keepdims=True)
    return (s - mu) * lax.rsqrt(var + LN_EPS) * g + b


def _softplus(x):
    return jnp.maximum(x, 0.0) + jnp.log1p(jnp.exp(-jnp.abs(x)))


def _expm1(x):
    series = x * (1.0 + x * (1.0 / 2 + x * (1.0 / 6 + x * (1.0 / 24 + x * (1.0 / 120 + x * (1.0 / 720))))))
    return jnp.where(jnp.abs(x) < 0.25, series, jnp.exp(x) - 1.0)


def _f_resid_ln(h, branch, g, b):
    return _ln(ALPHA * h + branch, g, b)


def _f_ple(h, gp, pe, bpg, g, b):
    return _ln(ALPHA * h + jax.nn.sigmoid(gp + bpg) * pe, g, b)


def _f_merge(ga, gb, ya, yb, bm0, bm1):
    return jax.nn.sigmoid(ga + bm0) * ya + jax.nn.sigmoid(gb + bm1) * yb


def _f_rnn_out(hs, ry):
    return hs * jax.nn.gelu(ry, approximate=True)


def _f_logf(fl, bf):
    return -_softplus(-(fl + bf))


def _f_decay(lam):
    return -RG_C * _softplus(-lam)


def _f_gate(xc, ra, ia, decay, ba, bx):
    r = jax.nn.sigmoid(ra + ba)
    i = jax.nn.sigmoid(ia + bx)
    log_a = decay * r
    a = jnp.exp(log_a)
    mult = jnp.sqrt(-_expm1(2.0 * log_a))
    return a, mult * (i * xc)


def _f_act(hg, hu):
    return jax.nn.silu(hg) * hu


ATT_BLOCK = 512
ATT_HEADS_PER_STEP = 2


def _scores(q, k, cq, ck, diagonal):
    s = lax.dot_general(q, k, (NT, ((), ())), preferred_element_type=F32) * SCALE
    s = s + cq - ck
    if diagonal:
        row = lax.broadcasted_iota(jnp.int32, s.shape, 0)
        col = lax.broadcasted_iota(jnp.int32, s.shape, 1)
        s = jnp.where(col <= row, s, NEG)
    return s


def _dscores(p, do, o, v):
    dob = do.astype(BF16)
    delta = jnp.sum(dob.astype(F32) * o, axis=1, keepdims=True)
    dp = lax.dot_general(dob, v.astype(BF16), (NT, ((), ())), preferred_element_type=F32)
    return p * (dp - delta)


def _attn_fwd(z, cq, ck, bsz, seq, ride=None):
    t = bsz * seq
    tq = _tile(seq, ATT_BLOCK)
    nq = seq // tq

    hp = ATT_HEADS_PER_STEP

    def body(q_ref, k_ref, v_ref, cq_ref, ck_ref, o_ref, ob_ref, lse_ref):
        for hh, i in [(hh, i) for hh in range(hp) for i in range(nq)]:
            lanes = slice(hh * HEAD_DIM, (hh + 1) * HEAD_DIM)
            rows = slice(i * tq, (i + 1) * tq)
            q = q_ref[rows, lanes].astype(BF16)
            cqi = cq_ref[hh, rows, :]

            def step(j, carry, diagonal, q=q, cqi=cqi, hh=hh, lanes=lanes):
                m, l, acc = carry
                keys = pl.ds(pl.multiple_of(j * tq, tq), tq)
                s = _scores(q, k_ref[keys, lanes].astype(BF16), cqi, ck_ref[hh, pl.ds(j, 1), :], diagonal)
                m_new = jnp.maximum(m, jnp.max(s, axis=1, keepdims=True))
                alpha = jnp.exp(m - m_new)
                p = jnp.exp(s - m_new)
                p_hi = p.astype(BF16)
                p_lo = (p - p_hi.astype(F32)).astype(BF16)
                vb = v_ref[keys, lanes].astype(BF16)
                pv = lax.dot_general(p_hi, vb, (NN, ((), ())), preferred_element_type=F32)
                pv = pv + lax.dot_general(p_lo, vb, (NN, ((), ())), preferred_element_type=F32)
                return m_new, alpha * l + jnp.sum(p, axis=1, keepdims=True), alpha * acc + pv

            carry = (jnp.full((tq, 1), NEG, F32), jnp.zeros((tq, 1), F32), jnp.zeros((tq, HEAD_DIM), F32))
            if i > 0:
                carry = lax.fori_loop(0, i, functools.partial(step, diagonal=False), carry)
            m, l, acc = step(i, carry, True)
            o = acc / l
            o_ref[rows, lanes] = o
            ob_ref[rows, lanes] = o.astype(BF16)
            lse_ref[hh, rows, :] = m + jnp.log(l)

    groups = N_HEADS // hp
    head = (seq, hp * HEAD_DIM)
    in_specs = [
        pl.BlockSpec(head, lambda b, g: (b, g)),
        pl.BlockSpec(head, lambda b, g: (b, groups + g)),
        pl.BlockSpec(head, lambda b, g: (b, 2 * groups + g)),
        pl.BlockSpec((None, hp, seq, 1), lambda b, g: (b, g, 0, 0)),
        pl.BlockSpec((None, hp, nq, tq), lambda b, g: (b, g, 0, 0)),
    ]
    out_specs = [pl.BlockSpec(head, lambda b, g: (b, g)), pl.BlockSpec(head, lambda b, g: (b, g)),
                 pl.BlockSpec((None, hp, seq, 1), lambda b, g: (b, g, 0, 0))]
    out_shape = [jax.ShapeDtypeStruct((t, D_MODEL), F32), jax.ShapeDtypeStruct((t, D_MODEL), BF16),
                 jax.ShapeDtypeStruct((bsz, N_HEADS, seq, 1), F32)]
    return _pcall(body, ride=ride, name="attn_fwd", grid=(bsz, groups), in_specs=in_specs, out_specs=out_specs,
                  out_shape=out_shape, compiler_params=_cparams(("parallel", "parallel")))(
                      z, z, z, cq, ck.reshape(bsz, N_HEADS, nq, tq))


def _attn_bwd(z, att, datt, lse, cq, ck, bsz, seq, ride=None):
    t = bsz * seq
    tq = _tile(seq, ATT_BLOCK)
    nq = seq // tq

    def body(q_ref, k_ref, v_ref, o_ref, do_ref, lse_ref, cq_ref, ck_ref,
             dq_ref, dk_ref, dv_ref, dcq_ref, dck_ref, dq_sc):
        dq_sc[...] = jnp.zeros_like(dq_sc)
        dcq_ref[...] = jnp.zeros_like(dcq_ref)
        for j in range(nq):
            keys = slice(j * tq, (j + 1) * tq)
            kb = k_ref[keys, :].astype(BF16)
            vb = v_ref[keys, :].astype(BF16)
            ckj = ck_ref[j:j + 1, :]

            def step(i, carry, diagonal, kb=kb, vb=vb, ckj=ckj):
                dk, dv, dc = carry
                rows = pl.ds(pl.multiple_of(i * tq, tq), tq)
                qb = q_ref[rows, :].astype(BF16)
                do = do_ref[rows, :]
                s = _scores(qb, kb, cq_ref[rows, :], ckj, diagonal)
                p = jnp.exp(s - lse_ref[rows, :])
                ds = _dscores(p, do, o_ref[rows, :], vb)
                dsb = (ds * SCALE).astype(BF16)
                dq_sc[rows, :] += lax.dot_general(dsb, kb, (NN, ((), ())), preferred_element_type=F32)
                dcq_ref[rows, :] += jnp.sum(ds, axis=1, keepdims=True)
                dv = dv + lax.dot_general(p.astype(BF16), do.astype(BF16), (TN, ((), ())),
                                          preferred_element_type=F32)
                dk = dk + lax.dot_general(dsb, qb, (TN, ((), ())), preferred_element_type=F32)
                return dk, dv, dc - jnp.sum(ds, axis=0, keepdims=True)

            zero = jnp.zeros((tq, HEAD_DIM), F32)
            carry = step(j, (zero, zero, jnp.zeros((1, tq), F32)), True)
            if j + 1 < nq:
                carry = lax.fori_loop(j + 1, nq, functools.partial(step, diagonal=False), carry)
            dk, dv, dck_ref[j:j + 1, :] = carry
            dk_ref[keys, :] = dk.astype(BF16)
            dv_ref[keys, :] = dv.astype(BF16)
        dq_ref[...] = dq_sc[...].astype(BF16)

    head = (seq, HEAD_DIM)
    hmap = lambda b, h: (b, h)
    col = pl.BlockSpec((None, None, seq, 1), lambda b, h: (b, h, 0, 0))
    row = pl.BlockSpec((None, None, nq, tq), lambda b, h: (b, h, 0, 0))
    in_specs = [pl.BlockSpec(head, hmap),
                pl.BlockSpec(head, lambda b, h: (b, N_HEADS + h)),
                pl.BlockSpec(head, lambda b, h: (b, 2 * N_HEADS + h)),
                pl.BlockSpec(head, hmap), pl.BlockSpec(head, hmap), col, col, row]
    big = jax.ShapeDtypeStruct((t, D_MODEL), BF16)
    return _pcall(body, ride=ride, name="attn_bwd", grid=(bsz, N_HEADS), in_specs=in_specs,
                  out_specs=[pl.BlockSpec(head, hmap)] * 3 + [col, row],
                  out_shape=[big, big, big, jax.ShapeDtypeStruct((bsz, N_HEADS, seq, 1), F32),
                             jax.ShapeDtypeStruct((bsz, N_HEADS, nq, tq), F32)],
                  scratch_shapes=[pltpu.VMEM(head, F32)],
                  compiler_params=_cparams(("parallel", "parallel")))(
                      z, z, z, att, datt, lse, cq, ck.reshape(bsz, N_HEADS, nq, tq))


def _scan(name, a, u, bsz, seq, *, reverse, with_prev=False, tb=256):
    c = u.shape[1]
    tb = _tile(seq, tb)
    nb = seq // tb
    has_a = a is not None

    def body(*refs):
        if has_a:
            a_ref, u_ref = refs[0], refs[1]
            rest = refs[2:]
        else:
            u_ref = refs[0]
            rest = refs[1:]
        outs = rest[:2] if with_prev else rest[:1]
        carry_sc, afirst_sc = rest[-2], rest[-1]
        step = pl.program_id(1)

        @pl.when(step == 0)
        def _():
            carry_sc[...] = jnp.zeros_like(carry_sc)
            afirst_sc[...] = jnp.zeros_like(afirst_sc)

        row = lax.broadcasted_iota(jnp.int32, (tb, c), 0)
        uu = u_ref[...]
        if has_a:
            aa = a_ref[...]
            if reverse:
                coef = jnp.where(row < tb - 1, pltpu.roll(aa, tb - 1, 0), afirst_sc[...])
            else:
                coef = aa
        k = 1
        while k < tb:
            shift = tb - k if reverse else k
            keep = (row < tb - k) if reverse else (row >= k)
            uu_sh = jnp.where(keep, pltpu.roll(uu, shift, 0), 0.0)
            if has_a:
                uu = coef * uu_sh + uu
                coef = coef * jnp.where(keep, pltpu.roll(coef, shift, 0), 1.0)
            else:
                uu = uu + uu_sh
            k *= 2
        carry = carry_sc[...]
        h = uu + coef * carry if has_a else uu + carry
        outs[0][...] = h
        if with_prev:
            outs[1][...] = jnp.where(row >= 1, pltpu.roll(h, 1, 0), carry)
        if reverse:
            carry_sc[...] = outs[0][0:1, :]
            if has_a:
                afirst_sc[...] = a_ref[0:1, :]
        else:
            carry_sc[...] = outs[0][tb - 1:tb, :]

    if reverse:
        imap = lambda b, s: (b * nb + nb - 1 - s, 0)
    else:
        imap = lambda b, s: (b * nb + s, 0)
    spec = pl.BlockSpec((tb, c), imap)
    n_in = 2 if has_a else 1
    n_out = 2 if with_prev else 1
    res = _pcall(body, name=name, grid=(bsz, nb), in_specs=[spec] * n_in, out_specs=[spec] * n_out,
                 out_shape=[jax.ShapeDtypeStruct(u.shape, F32)] * n_out,
                 scratch_shapes=[pltpu.VMEM((1, c), F32), pltpu.VMEM((1, c), F32)],
                 compiler_params=_cparams(("parallel", "arbitrary")))(*([a, u] if has_a else [u]))
    return res if with_prev else res[0]


def _conv_fwd(z, w, b, bsz, seq, tb=256):
    c = D_MODEL
    t = bsz * seq
    tb = _tile(seq, tb)
    nb = seq // tb

    def body(x_ref, w_ref, b_ref, o_ref, tail_sc):
        step = pl.program_id(1)

        @pl.when(step == 0)
        def _():
            tail_sc[...] = jnp.zeros_like(tail_sc)

        x = x_ref[...]
        row8 = lax.broadcasted_iota(jnp.int32, (8, c), 0)
        tail = tail_sc[...]
        acc = w_ref[CONV_W - 1:CONV_W, :] * x + b_ref[...]
        for sh in range(1, CONV_W):
            xs = pltpu.roll(x, sh, 0)
            top = jnp.where(row8 < sh, pltpu.roll(tail, sh, 0), xs[0:8, :])
            xs = jnp.concatenate([top, xs[8:, :]], axis=0) if tb > 8 else top
            acc = acc + w_ref[CONV_W - 1 - sh:CONV_W - sh, :] * xs
        o_ref[...] = acc
        tail_sc[...] = x_ref[tb - 8:tb, :]

    return _pcall(body, name="conv_fwd", grid=(bsz, nb),
                  in_specs=[pl.BlockSpec((tb, c), lambda bb, s: (bb * nb + s, OFF_RX // c)),
                            pl.BlockSpec((CONV_W, c), lambda bb, s: (0, 0)),
                            pl.BlockSpec((1, c), lambda bb, s: (0, 0))],
                  out_specs=pl.BlockSpec((tb, c), lambda bb, s: (bb * nb + s, 0)),
                  out_shape=jax.ShapeDtypeStruct((t, c), F32),
                  scratch_shapes=[pltpu.VMEM((8, c), F32)],
                  compiler_params=_cparams(("parallel", "arbitrary")))(z, w, b)


def _conv_bwd(z, dxc, w, bsz, seq, tb=256):
    c = D_MODEL
    t = bsz * seq
    tb = _tile(seq, tb)
    nb = seq // tb

    def body(x_ref, g_ref, w_ref, dx_ref, dw_ref, db_ref, head_sc):
        bb, step = pl.program_id(0), pl.program_id(1)

        @pl.when(step == 0)
        def _():
            head_sc[...] = jnp.zeros_like(head_sc)

        x, g = x_ref[...], g_ref[...]
        row8 = lax.broadcasted_iota(jnp.int32, (8, c), 0)
        head = head_sc[...]
        dx = w_ref[CONV_W - 1:CONV_W, :] * g
        dws = [None] * CONV_W
        dws[CONV_W - 1] = _colsum(g * x)
        for sh in range(1, CONV_W):
            gs = pltpu.roll(g, tb - sh, 0)
            bot = jnp.where(row8 >= 8 - sh, pltpu.roll(head, 8 - sh, 0), gs[tb - 8:tb, :])
            gs = jnp.concatenate([gs[:tb - 8, :], bot], axis=0) if tb > 8 else bot
            dx = dx + w_ref[CONV_W - 1 - sh:CONV_W - sh, :] * gs
            dws[CONV_W - 1 - sh] = _colsum(gs * x)
        dx_ref[...] = dx.astype(dx_ref.dtype)
        first = (bb == 0) & (step == 0)
        dw = jnp.concatenate(dws, axis=0)
        db = _colsum(g)

        @pl.when(first)
        def _():
            dw_ref[...] = dw
            db_ref[...] = db

        @pl.when(jnp.logical_not(first))
        def _():
            dw_ref[...] += dw
            db_ref[...] += db

        head_sc[...] = g_ref[0:8, :]

    rmap = lambda bb, s: (bb * nb + nb - 1 - s, 0)
    return _pcall(body, name="conv_bwd", grid=(bsz, nb),
                  in_specs=[pl.BlockSpec((tb, c), lambda bb, s: (bb * nb + nb - 1 - s, OFF_RX // c)),
                            pl.BlockSpec((tb, c), rmap),
                            pl.BlockSpec((CONV_W, c), lambda bb, s: (0, 0))],
                  out_specs=[pl.BlockSpec((tb, c), rmap),
                             pl.BlockSpec((CONV_W, c), lambda bb, s: (0, 0)),
                             pl.BlockSpec((1, c), lambda bb, s: (0, 0))],
                  out_shape=[jax.ShapeDtypeStruct((t, c), BF16), jax.ShapeDtypeStruct((CONV_W, c), F32),
                             jax.ShapeDtypeStruct((1, c), F32)],
                  scratch_shapes=[pltpu.VMEM((8, c), F32)],
                  compiler_params=_cparams(("arbitrary", "arbitrary")))(z, dxc, w)


def _gate_fwd(xc, w_a, w_x, b_a, b_x, lam, tm=1024):
    t = xc.shape[0]
    tm = _tile(t, tm)

    def body(xc_ref, wa_ref, wx_ref, ba_ref, bx_ref, lam_ref, a_ref, u_ref):
        xc_b = xc_ref[...]
        xb = xc_b.astype(BF16)
        ra = lax.dot_general(xb, wa_ref[...].astype(BF16), (NN, ((), ())), preferred_element_type=F32)
        ia = lax.dot_general(xb, wx_ref[...].astype(BF16), (NN, ((), ())), preferred_element_type=F32)
        a, u = _f_gate(xc_b, ra, ia, lam_ref[...], ba_ref[...], bx_ref[...])
        a_ref[...] = a
        u_ref[...] = u

    row = pl.BlockSpec((tm, BLK), lambda n, i: (i, n))
    wsp = pl.BlockSpec((None, BLK, BLK), lambda n, i: (n, 0, 0))
    vec = pl.BlockSpec((1, BLK), lambda n, i: (0, n))
    return _pcall(body, name="gate_fwd", grid=(N_BLK, t // tm), in_specs=[row, wsp, wsp, vec, vec, vec],
                  out_specs=[row, row], out_shape=[jax.ShapeDtypeStruct((t, D_MODEL), F32)] * 2,
                  compiler_params=_cparams(("parallel", "parallel")))(xc, w_a, w_x, b_a, b_x, lam)


def _gate_bwd(xc, w_a, w_x, b_a, b_x, lam, hprev, du, tm=1024, ride=None):
    t = xc.shape[0]
    tm = _tile(t, tm)

    def body(xc_ref, wa_ref, wx_ref, ba_ref, bx_ref, lam_ref, hp_ref, du_ref,
             dxc_ref, dwa_ref, dwx_ref, dba_ref, dbx_ref, dlam_ref):
        step = pl.program_id(1)
        xc_b = xc_ref[...]
        xb = xc_b.astype(BF16)
        wa, wx = wa_ref[...].astype(BF16), wx_ref[...].astype(BF16)
        ra = lax.dot_general(xb, wa, (NN, ((), ())), preferred_element_type=F32)
        ia = lax.dot_general(xb, wx, (NN, ((), ())), preferred_element_type=F32)
        full = lambda r: jnp.broadcast_to(r[...], (tm, BLK))
        _, pull = jax.vjp(_f_gate, xc_b, ra, ia, full(lam_ref), full(ba_ref), full(bx_ref))
        du_b = du_ref[...]
        dxc, dra, dia, dlam, dba, dbx = pull((du_b * hp_ref[...], du_b))
        drb, dib = dra.astype(BF16), dia.astype(BF16)
        dxc = dxc + lax.dot_general(drb, wa, (NT, ((), ())), preferred_element_type=F32)
        dxc = dxc + lax.dot_general(dib, wx, (NT, ((), ())), preferred_element_type=F32)
        dxc_ref[...] = dxc
        _accumulate(dwa_ref, lax.dot_general(xb, drb, (TN, ((), ())), preferred_element_type=F32), step)
        _accumulate(dwx_ref, lax.dot_general(xb, dib, (TN, ((), ())), preferred_element_type=F32), step)
        _accumulate(dba_ref, _colsum(dba), step)
        _accumulate(dbx_ref, _colsum(dbx), step)
        _accumulate(dlam_ref, _colsum(dlam), step)

    row = pl.BlockSpec((tm, BLK), lambda n, i: (i, n))
    wsp = pl.BlockSpec((None, BLK, BLK), lambda n, i: (n, 0, 0))
    vec = pl.BlockSpec((1, BLK), lambda n, i: (0, n))
    wshape = jax.ShapeDtypeStruct((N_BLK, BLK, BLK), F32)
    vshape = jax.ShapeDtypeStruct((1, D_MODEL), F32)
    return _pcall(body, ride=ride, name="gate_bwd", grid=(N_BLK, t // tm),
                  in_specs=[row, wsp, wsp, vec, vec, vec, row, row],
                  out_specs=[row, wsp, wsp, vec, vec, vec],
                  out_shape=[jax.ShapeDtypeStruct((t, D_MODEL), F32), wshape, wshape, vshape, vshape, vshape],
                  compiler_params=_cparams(("parallel", "arbitrary")))(xc, w_a, w_x, b_a, b_x, lam, hprev, du)


def _ffn_in_act(a, w, tm=1024, ride=None):
    t = a.shape[0]
    tm = _tile(t, tm)

    def body(a_ref, wg_ref, wu_ref, hgu_ref, act_ref):
        ab = a_ref[...].astype(BF16)
        hg = lax.dot_general(ab, wg_ref[...].astype(BF16), (NN, ((), ())), preferred_element_type=F32)
        hu = lax.dot_general(ab, wu_ref[...].astype(BF16), (NN, ((), ())), preferred_element_type=F32)
        hgu_ref[0] = hg
        hgu_ref[1] = hu
        act_ref[...] = _f_act(hg, hu).astype(act_ref.dtype)

    wspec = lambda off: pl.BlockSpec((None, D_MODEL, FF_SH), lambda i, s: (s + off, 0, 0))
    hgu, act = _pcall(body, ride=ride, name="ffn_in", grid=(t // tm, N_FF),
                      in_specs=[pl.BlockSpec((tm, D_MODEL), lambda i, s: (i, 0)), wspec(0), wspec(N_FF)],
                      out_specs=[pl.BlockSpec((2, None, tm, FF_SH), lambda i, s: (0, s, i, 0)),
                                 pl.BlockSpec((None, tm, FF_SH), lambda i, s: (s, i, 0))],
                      out_shape=[jax.ShapeDtypeStruct((2, N_FF, t, FF_SH), F32),
                                 jax.ShapeDtypeStruct((N_FF, t, FF_SH), BF16)],
                      compiler_params=_cparams(("parallel", "parallel")))(a, w, w)
    return hgu.reshape(2 * N_FF, t, FF_SH), act


def _ffn_out_dx_act(d, w, hgu, tm=512, ride=None):
    t = d.shape[0]
    tm = _tile(t, tm)

    def body(d_ref, w_ref, hg_ref, hu_ref, o_ref):
        dact = lax.dot_general(d_ref[...].astype(BF16), w_ref[...].astype(BF16), (NT, ((), ())),
                               preferred_element_type=F32)
        _, pull = jax.vjp(_f_act, hg_ref[...], hu_ref[...])
        dhg, dhu = pull(dact)
        o_ref[0] = dhg.astype(o_ref.dtype)
        o_ref[1] = dhu.astype(o_ref.dtype)

    hspec = lambda off: pl.BlockSpec((None, tm, FF_SH), lambda i, s: (s + off, i, 0))
    res = _pcall(body, ride=ride, name="ffn_out_dx", grid=(t // tm, N_FF),
                 in_specs=[pl.BlockSpec((tm, D_MODEL), lambda i, s: (i, 0)),
                           pl.BlockSpec((None, FF_SH, D_MODEL), lambda i, s: (s, 0, 0)), hspec(0), hspec(N_FF)],
                 out_specs=pl.BlockSpec((2, None, tm, FF_SH), lambda i, s: (0, s, i, 0)),
                 out_shape=jax.ShapeDtypeStruct((2, N_FF, t, FF_SH), BF16),
                 compiler_params=_cparams(("parallel", "parallel")))(d, w, hgu, hgu)
    return res.reshape(2 * N_FF, t, FF_SH)


def _adamw(name, parts, w, m, v, tr=128, ride=None):
    ng = len(parts)
    n_src, r, c = parts[0].shape
    tr = _tile(r, tr)
    nb = r // tr
    bc1 = 1.0 - ADAM_B1 ** ADAM_STEP
    bc2 = 1.0 - ADAM_B2 ** ADAM_STEP

    def body(*refs):
        p_refs = refs[:ng]
        w_ref, m_ref, v_ref, g_ref, d_ref, nm_ref, nv_ref = refs[ng:]
        grp = pl.program_id(0)

        def update(p_ref):
            g = p_ref[0].astype(F32)
            for s in range(1, n_src):
                g = g + p_ref[s].astype(F32)
            nm = ADAM_B1 * m_ref[...] + (1.0 - ADAM_B1) * g
            nv = ADAM_B2 * v_ref[...] + (1.0 - ADAM_B2) * jnp.square(g)
            g_ref[...] = g
            nm_ref[...] = nm
            nv_ref[...] = nv
            d_ref[...] = -ADAM_LR * ((nm / bc1) / (jnp.sqrt(nv / bc2) + ADAM_EPS) + ADAM_WD * w_ref[...])

        for k in range(ng):
            pl.when(grp == k)(functools.partial(update, p_refs[k]))

    p_specs = [pl.BlockSpec((n_src, tr, c), functools.partial(lambda gi, i, k: (0, jnp.where(gi == k, i, 0), 0), k=k))
               for k in range(ng)]
    spec = pl.BlockSpec((tr, c), lambda gi, i: (gi * nb + i, 0))
    return _pcall(body, ride=ride, name=name, grid=(ng, nb), in_specs=p_specs + [spec, spec, spec],
                  out_specs=[spec] * 4, out_shape=[jax.ShapeDtypeStruct((ng * r, c), F32)] * 4,
                  compiler_params=_cparams(("parallel", "parallel")))(*parts, w, m, v)


def _sum_parts(name, parts, tr=256):
    _, r, c = parts.shape
    tr = _tile(r, tr)

    def body(p_ref, o_ref):
        g = p_ref[0]
        for s in range(1, parts.shape[0]):
            g = g + p_ref[s]
        o_ref[...] = g

    return _pcall(body, name=name, grid=(r // tr,),
                  in_specs=[pl.BlockSpec((parts.shape[0], tr, c), lambda i: (0, i, 0))],
                  out_specs=pl.BlockSpec((tr, c), lambda i: (i, 0)),
                  out_shape=jax.ShapeDtypeStruct((r, c), F32), compiler_params=_cparams(("parallel",)))(parts)


def _peer(k):
    x, y, c = lax.axis_index("x"), lax.axis_index("y"), lax.axis_index("c")
    return (x ^ ((k >> 2) & 1), y ^ ((k >> 1) & 1), c ^ (k & 1))


def _my_id():
    return 4 * lax.axis_index("x") + 2 * lax.axis_index("y") + lax.axis_index("c")


def _exchange(name, ride):
    n = len(ride.arrays)

    def body(*refs):
        ride.begin(refs[:n], refs[n:2 * n], refs[2 * n:])
        ride.finish(refs[:n], refs[n:2 * n], refs[2 * n:])

    hbm = pl.BlockSpec(memory_space=pltpu.HBM)
    return _pcall(body, name=name, in_specs=[hbm] * n, out_specs=[hbm] * n, out_shape=ride.out_shapes(),
                  scratch_shapes=ride.scratch())(*ride.arrays)


def _row(v):
    return v.reshape(1, -1)


def _time_major_heads(c, bsz, seq):
    return c.reshape(bsz, seq, BLK)[:, :, :N_HEADS].transpose(0, 2, 1)


def _no_ride(*_):
    return None


TWICE = [(D_MODEL, F32), (D_MODEL, BF16)]


def _both(fn):
    def run(*v):
        y = fn(*v)
        return y, y
    return run


def _layer_fwd(h, hb, p_l, w, bsz, seq, ride_of=_no_ride):
    t = bsz * seq
    zq = _mm_nn("z_proj_qkv", hb, w['w_in7'], n=QKV, out_dtype=BF16, ride=ride_of('z_proj_qkv'))
    zr = _mm_nn("z_proj_rest", hb, w['w_in7'], b_off=QKV, n=4 * D_MODEL, ride=ride_of('z_proj_rest'))
    fl = _mm_nn("f_proj", hb, w['w_inf'])
    logf, = _rowwise("logf_fwd", lambda f, b: (_f_logf(f, b),), [fl], [w['b_forget']], [(BLK, F32)], [])
    c = _scan("cumsum_fwd", None, logf, bsz, seq, reverse=False)
    ct = _time_major_heads(c, bsz, seq)
    cq, ck = ct[..., None], ct[:, :, None, :]
    att, attb, lse = _attn_fwd(zq, cq, ck, bsz, seq, ride=ride_of('attn_fwd'))
    xc = _conv_fwd(zr, w['conv_w'], w['conv_b'], bsz, seq)
    decay, = _rowwise("decay_fwd", lambda lam: (_f_decay(lam),), [w['rg_lambda']], [], [(D_MODEL, F32)], [])
    a, u = _gate_fwd(xc, w['rg_w_a'], w['rg_w_x'], w['rg_b_a'], w['rg_b_x'], decay)
    hs, hprev = _scan("lru_fwd", a, u, bsz, seq, reverse=False, with_prev=True)
    rnn, = _rowwise("rnn_out_fwd", lambda s, y: (_f_rnn_out(s, y),), [hs, (zr, OFF_RY, D_MODEL)], [],
                    [(D_MODEL, BF16)], [])
    ya = _mm_nn("branch_att", attb, w['w_branch_att'])
    yb = _mm_nn("branch_rnn", rnn, w['w_branch_rnn'])
    merged, = _rowwise("merge_fwd", lambda *v: (_f_merge(*v),),
                       [(zr, OFF_GA, D_MODEL), (zr, OFF_GB, D_MODEL), ya, yb], [w['b_merge0'], w['b_merge1']],
                       [(D_MODEL, BF16)], [])
    mix = _mm_nn("mix_out", merged, w['w_out'])
    h1, h1b = _rowwise("ln_mix_fwd", _both(_f_resid_ln), [h, mix], [w['ln_mix_g'], w['ln_mix_b']], TWICE, [])
    tm = _tile(t, 1024)
    hgu, act = _ffn_in_act(h1b, w['w_ffn_in'], ride=ride_of('ffn_in'))
    ffn = _mm("ffn_out", act, w['w_ffn_out'], grid=(t // tm, 1, N_FF),
              a_spec=pl.BlockSpec((None, tm, FF_SH), lambda i, j, s: (s, i, 0)),
              b_spec=pl.BlockSpec((None, FF_SH, D_MODEL), lambda i, j, s: (s, 0, 0)),
              o_spec=pl.BlockSpec((tm, D_MODEL), lambda i, j, s: (i, 0)),
              out_shape=jax.ShapeDtypeStruct((t, D_MODEL), F32), contract=NN, ride=ride_of('ffn_out'))
    h2, h2b = _rowwise("ln_ffn_fwd", _both(_f_resid_ln), [h1, ffn], [w['ln_ffn_g'], w['ln_ffn_b']], TWICE, [])
    gp = _mm_nn("ple_gate", h2b, w['w_ple_gate'])
    pe = _mm_nn("ple_proj", p_l, w['w_ple'])
    h3, h3b = _rowwise("ln_ple_fwd", _both(_f_ple), [h2, gp, pe],
                       [w['b_ple_gate'], w['ln_ple_g'], w['ln_ple_b']], TWICE, [])
    saved = dict(h=h, hb=hb, zq=zq, zr=zr, fl=fl, cq=cq, ck=ck, att=att, attb=attb, lse=lse, xc=xc, a=a, decay=decay,
                 hprev=hprev, hs=hs, rnn=rnn, ya=ya, yb=yb, merged=merged, mix=mix, h1=h1, h1b=h1b, hgu=hgu,
                 act=act, ffn=ffn, h2=h2, h2b=h2b, gp=gp, pe=pe)
    return h3, h3b, saved


def _layer_bwd(dh3, p_l, w, s, bsz, seq, ride_of=_no_ride):
    t = bsz * seq
    g = {}
    dh2, dgp, dpe, g['b_ple_gate'], g['ln_ple_g'], g['ln_ple_b'] = _vjp_rowwise(
        "ln_ple_bwd", _f_ple, [s['h2'], s['gp'], s['pe']], [w['b_ple_gate'], w['ln_ple_g'], w['ln_ple_b']], [dh3], 3,
        dtypes=[F32, BF16, BF16])
    g['w_ple_gate'] = _mm_tn("ple_gate_dw", s['h2b'], dgp, out_dtype=BF16)
    g['w_ple'] = _mm_tn("ple_proj_dw", p_l, dpe, out_dtype=BF16)
    dh2b = _mm_nt("ple_gate_dx", dgp, w['w_ple_gate'])
    dh1, dffn, g['ln_ffn_g'], g['ln_ffn_b'] = _ln_resid_bwd(
        "ln_ffn_bwd", s['h1'], s['ffn'], w['ln_ffn_g'], w['ln_ffn_b'], dh2, dh2b)
    tm = _tile(t, 1024)
    tk = _tile(t, 2048)
    g['w_ffn_out'] = _mm("ffn_out_dw", s['act'], dffn, grid=(N_FF, 1, t // tk),
                         a_spec=pl.BlockSpec((None, tk, FF_SH), lambda ss, j, k: (ss, k, 0)),
                         b_spec=pl.BlockSpec((tk, D_MODEL), lambda ss, j, k: (k, 0)),
                         o_spec=pl.BlockSpec((None, FF_SH, D_MODEL), lambda ss, j, k: (ss, 0, 0)),
                         out_shape=jax.ShapeDtypeStruct((N_FF, FF_SH, D_MODEL), BF16), contract=TN)
    dhgu = _ffn_out_dx_act(dffn, w['w_ffn_out'], s['hgu'], ride=ride_of('ffn_out_dx', g))
    g['w_ffn_in'] = _mm("ffn_in_dw", s['h1b'], dhgu, grid=(2 * N_FF, 1, t // tk),
                        a_spec=pl.BlockSpec((tk, D_MODEL), lambda ss, j, k: (k, 0)),
                        b_spec=pl.BlockSpec((None, tk, FF_SH), lambda ss, j, k: (ss, k, 0)),
                        o_spec=pl.BlockSpec((None, D_MODEL, FF_SH), lambda ss, j, k: (ss, 0, 0)),
                        out_shape=jax.ShapeDtypeStruct((2 * N_FF, D_MODEL, FF_SH), BF16), contract=TN,
                        ride=ride_of('ffn_in_dw', g))
    dh1b = _mm("ffn_in_dx", dhgu, w['w_ffn_in'], grid=(t // tm, 1, 2 * N_FF),
               a_spec=pl.BlockSpec((None, tm, FF_SH), lambda i, j, ss: (ss, i, 0)),
               b_spec=pl.BlockSpec((None, D_MODEL, FF_SH), lambda i, j, ss: (ss, 0, 0)),
               o_spec=pl.BlockSpec((tm, D_MODEL), lambda i, j, ss: (i, 0)),
               out_shape=jax.ShapeDtypeStruct((t, D_MODEL), F32), contract=NT, ride=ride_of('ffn_in_dx', g))
    dh, dmix, g['ln_mix_g'], g['ln_mix_b'] = _ln_resid_bwd(
        "ln_mix_bwd", s['h'], s['mix'], w['ln_mix_g'], w['ln_mix_b'], dh1, dh1b)
    g['w_out'] = _mm_tn("mix_out_dw", s['merged'], dmix, out_dtype=BF16)
    dmerged = _mm_nt("mix_out_dx", dmix, w['w_out'])
    z = s['zr']
    dga, dgb, dya, dyb, dbm0, dbm1 = _vjp_rowwise(
        "merge_bwd", _f_merge, [(z, OFF_GA, D_MODEL), (z, OFF_GB, D_MODEL), s['ya'], s['yb']],
        [w['b_merge0'], w['b_merge1']], [dmerged], 4, dtypes=[BF16] * 4)
    g['b_merge'] = jnp.concatenate([dbm0, dbm1], axis=0)
    g['w_branch_att'] = _mm_tn("branch_att_dw", s['attb'], dya, out_dtype=BF16)
    g['w_branch_rnn'] = _mm_tn("branch_rnn_dw", s['rnn'], dyb, out_dtype=BF16)
    datt = _mm_nt("branch_att_dx", dya, w['w_branch_att'], out_dtype=BF16)
    drnn = _mm_nt("branch_rnn_dx", dyb, w['w_branch_rnn'])
    dhs, dry = _vjp_rowwise("rnn_out_bwd", _f_rnn_out, [s['hs'], (z, OFF_RY, D_MODEL)], [], [drnn], 2,
                            dtypes=[F32, BF16])
    lam = _scan("lru_bwd", s['a'], dhs, bsz, seq, reverse=True)
    dxc, g['rg_w_a'], g['rg_w_x'], g['rg_b_a'], g['rg_b_x'], ddecay = _gate_bwd(
        s['xc'], w['rg_w_a'], w['rg_w_x'], w['rg_b_a'], w['rg_b_x'], s['decay'], s['hprev'], lam,
        ride=ride_of('gate_bwd', g))
    g['rg_lambda'], = _vjp_rowwise("decay_bwd", _f_decay, [w['rg_lambda']], [], [ddecay], 1)
    drx, g['conv_w'], g['conv_b'] = _conv_bwd(z, dxc, w['conv_w'], bsz, seq)
    dq, dk, dv, dcq, dck = _attn_bwd(s['zq'], s['att'], datt, s['lse'], s['cq'], s['ck'], bsz, seq,
                                     ride=ride_of('attn_bwd', g))
    dc = (dcq[:, :, :, 0] + dck.reshape(bsz, N_HEADS, seq)).transpose(0, 2, 1)
    dc = jnp.pad(dc, ((0, 0), (0, 0), (0, BLK - N_HEADS))).reshape(t, BLK)
    dlogf = _scan("cumsum_bwd", None, dc, bsz, seq, reverse=True)
    dfl, g['b_forget'] = _vjp_rowwise("logf_bwd", _f_logf, [s['fl']], [w['b_forget']], [dlogf], 1, dtypes=[BF16])
    dz = jnp.concatenate([dq, dk, dv, drx, dry, dga, dgb], axis=1)
    g['w_in7'] = _mm_tn("z_proj_dw", s['hb'], dz, out_dtype=BF16)
    g['w_inf'] = _mm_tn("f_proj_dw", s['hb'], dfl, out_dtype=BF16)
    dh = _mm_nt("z_proj_dx", dz, w['w_in7'], ride=ride_of('z_proj_dx', g), add=dh)
    dh = _mm_nt("f_proj_dx", dfl, w['w_inf'], add=dh)
    return dh, g


def _ln_resid_bwd(name, h, branch, gam, bet, d0, d1):
    def bwd(hv, bv, d0v, d1v, gv, btv):
        _, pull = jax.vjp(_f_resid_ln, hv, bv, gv, btv)
        dh, db, dg, dbt = pull(d0v + d1v)
        return dh, db, _colsum(dg), _colsum(dbt)

    return _rowwise(name, bwd, [h, branch, d0, d1], [gam, bet], [(D_MODEL, F32), (D_MODEL, BF16)],
                    [D_MODEL, D_MODEL])


class _Schedule:
    FWD = {'z_proj_qkv': ['w_ffn_out'], 'z_proj_rest': ['w_branch_att', 'w_branch_rnn', 'w_out', 'w_ple_gate'],
           'attn_fwd': ['w_in'], 'ffn_in': ['w_ffn_in'], 'ffn_out': ['w_ple', 'conv_w', 'b_merge']}
    BWD = {'ffn_out_dx': ['w_ffn_out'], 'ffn_in_dw': ['w_ple_gate', 'w_ple'],
           'gate_bwd': ['w_out', 'w_branch_att', 'w_branch_rnn'],
           'attn_bwd': ['w_ffn_in', 'conv_w', 'b_merge']}

    def __init__(self, shards, depth):
        self.shards, self.depth = shards, depth
        self.gathered = [{} for _ in range(depth)]
        self.received = [{} for _ in range(depth)]
        self.pending = []
        self.deferred = None

    def gather_ride(self, layer, kernel_name):
        if layer + 1 >= self.depth:
            return None
        names = self.FWD[kernel_name]
        ride = _Ride([self.shards[n] for n in names], gather=True, index=layer + 1)
        self.pending.append((ride, names, self.gathered[layer + 1]))
        return ride

    def _scatter(self, arrays, names, layer):
        ride = _Ride(arrays, gather=False)
        self.pending.append((ride, names, self.received[layer]))
        return ride

    def scatter_ride(self, layer, kernel_name, grads):
        if kernel_name == 'z_proj_dx':
            whole = _by_destination('w_in', grads)
            half = whole.shape[1] // 2
            self.deferred = (whole[:, half:], layer)
            return self._scatter([whole[:, :half]], ['w_in_a'], layer)
        if kernel_name == 'ffn_in_dx':
            if self.deferred is None:
                return None
            (late, from_layer), self.deferred = self.deferred, None
            return self._scatter([late], ['w_in_b'], from_layer)
        names = self.BWD[kernel_name]
        return self._scatter([_by_destination(n, grads) for n in names], names, layer)

    def collect(self):
        for ride, names, dst in self.pending:
            dst.update(zip(names, ride.result))
        self.pending = []


def _local_step(x2, tgt, p3, weights_of, depth, g_in, b_in, bsz, seq, sched=None):
    h, hb = _rowwise("ln_in_fwd", _both(_ln), [x2], [g_in, b_in], TWICE, [])
    p3 = p3.astype(BF16)
    saved, layer_w = [], []
    for l in range(depth):
        layer_w.append(weights_of(l))
        ride_of = functools.partial(sched.gather_ride, l) if sched else _no_ride
        h, hb, s = _layer_fwd(h, hb, p3[l], layer_w[l], bsz, seq, ride_of)
        if sched:
            sched.collect()
        saved.append(s)

    def loss_fn(y, tv):
        err = y - tv
        return err * (1.0 / D_MODEL), _colsum(jnp.square(err))

    dh, sq = _rowwise("loss", loss_fn, [h, tgt], [], [(D_MODEL, F32)], [D_MODEL])
    grads = [None] * depth
    for l in reversed(range(depth)):
        ride_of = functools.partial(sched.scatter_ride, l) if sched else _no_ride
        dh, grads[l] = _layer_bwd(dh, p3[l], layer_w[l], saved[l], bsz, seq, ride_of)
        if sched:
            sched.collect()
    dx, dg_in, db_in = _vjp_rowwise("ln_in_bwd", _ln, [x2], [g_in, b_in], [dh], 1)
    return sq, dx, grads, dg_in, db_in


def _layer_weights(full):
    w = {}
    wt = full['w_in'].transpose(1, 0, 2).reshape(D_MODEL, N_IN)
    w['w_in7'] = jnp.concatenate([wt[:, :3 * D_MODEL], wt[:, 3 * D_MODEL + N_HEADS:]], axis=1)
    w['w_inf'] = jnp.pad(wt[:, 3 * D_MODEL:3 * D_MODEL + N_HEADS], ((0, 0), (0, BLK - N_HEADS)))
    for n in ['w_branch_att', 'w_branch_rnn', 'w_out', 'w_ple_gate']:
        w[n] = full[n].reshape(D_MODEL, D_MODEL)
    w['w_ffn_in'] = full['w_ffn_in']
    w['w_ffn_out'] = full['w_ffn_out'].reshape(N_FF, FF_SH, D_MODEL)
    w['w_ple'] = full['w_ple'].transpose(1, 0, 2).reshape(D_PLE, D_MODEL)
    w['conv_w'] = full['conv_w'].transpose(1, 0, 2).reshape(CONV_W, D_MODEL)
    bm = full['b_merge'].transpose(1, 0, 2).reshape(2, D_MODEL)
    w['b_merge0'], w['b_merge1'] = bm[0:1], bm[1:2]
    return w


def _by_destination(name, gw):
    if name == 'w_in':
        g7, gf = gw['w_in7'], gw['w_inf']
        true = jnp.concatenate([g7[:, :3 * D_MODEL], gf[:, :N_HEADS], g7[:, 3 * D_MODEL:]], axis=1)
        return true.reshape(D_MODEL, N_DEV, IN_SH).transpose(1, 0, 2)
    g = gw[name]
    if name in ('w_branch_att', 'w_branch_rnn', 'w_out', 'w_ple_gate'):
        return g.reshape(N_DEV, D_MODEL // N_DEV, D_MODEL)
    if name == 'w_ffn_in':
        return g
    if name == 'w_ffn_out':
        return g.reshape(N_DEV, N_FF * FF_SH // N_DEV, D_MODEL)
    return g.reshape(g.shape[0], N_DEV, BLK).transpose(1, 0, 2)


def kernel(x, p, ln_in_g, ln_in_b, w_in, b_forget, conv_w, conv_b, rg_w_a, rg_b_a, rg_w_x, rg_b_x, rg_lambda, w_branch_att, w_branch_rnn, b_merge, w_out, ln_mix_g, ln_mix_b, w_ffn_in, w_ffn_out, ln_ffn_g, ln_ffn_b, w_ple, w_ple_gate, b_ple_gate, ln_ple_g, ln_ple_b, loss_target, m_ln_in_g, m_ln_in_b, m_w_in, m_b_forget, m_conv_w, m_conv_b, m_rg_w_a, m_rg_b_a, m_rg_w_x, m_rg_b_x, m_rg_lambda, m_w_branch_att, m_w_branch_rnn, m_b_merge, m_w_out, m_ln_mix_g, m_ln_mix_b, m_w_ffn_in, m_w_ffn_out, m_ln_ffn_g, m_ln_ffn_b, m_w_ple, m_w_ple_gate, m_b_ple_gate, m_ln_ple_g, m_ln_ple_b, v_ln_in_g, v_ln_in_b, v_w_in, v_b_forget, v_conv_w, v_conv_b, v_rg_w_a, v_rg_b_a, v_rg_w_x, v_rg_b_x, v_rg_lambda, v_w_branch_att, v_w_branch_rnn, v_b_merge, v_w_out, v_ln_mix_g, v_ln_mix_b, v_w_ffn_in, v_w_ffn_out, v_ln_ffn_g, v_ln_ffn_b, v_w_ple, v_w_ple_gate, v_b_ple_gate, v_ln_ple_g, v_ln_ple_b):
    env = dict(locals())
    wts = {n: env[n] for n in WEIGHTS}
    mom = {n: env['m_' + n] for n in WEIGHTS}
    var = {n: env['v_' + n] for n in WEIGHTS}
    bsz, seq, _ = x.shape
    depth = w_in.shape[0]
    t = bsz * seq
    x2, tgt = x.reshape(t, D_MODEL), loss_target.reshape(t, D_MODEL)
    p3 = p.reshape(depth, t, D_PLE)

    shard_names = SHARDED_BF16 + SHARDED_F32
    shards = {n: wts[n].astype(BF16) for n in SHARDED_BF16}
    shards.update({n: wts[n] for n in SHARDED_F32})
    sched = _Schedule(shards, depth)
    first = _Ride([shards[n] for n in shard_names], gather=True, index=0)
    sched.gathered[0] = dict(zip(shard_names, _exchange("gather_layer0", first)))

    def weights_of(l):
        w = _layer_weights(sched.gathered[l])
        for n in ['conv_b', 'rg_b_a', 'rg_b_x', 'rg_lambda', 'ln_mix_g', 'ln_mix_b', 'ln_ffn_g', 'ln_ffn_b',
                  'b_ple_gate', 'ln_ple_g', 'ln_ple_b']:
            w[n] = _row(wts[n][l])
        w['b_forget'] = jnp.pad(_row(b_forget[l]), ((0, 0), (0, BLK - N_HEADS)))
        w['rg_w_a'], w['rg_w_x'] = rg_w_a[l], rg_w_x[l]
        return w

    g_in, b_in = _row(ln_in_g), _row(ln_in_b)
    sq, dx, grads, dg_in, db_in = _local_step(x2, tgt, p3, weights_of, depth, g_in, b_in, bsz, seq, sched)
    loss = lax.psum(0.5 * jnp.sum(sq) / D_MODEL, ("x", "y", "c"))
    grad_x = dx.reshape(bsz, seq, D_MODEL)

    out = {}

    def update(n, ride=None):
        shp = wts[n].shape
        flat = lambda a: a.reshape(-1, shp[-1])
        if n == 'w_in':
            recv = [sched.received[l][half] for l in range(depth) for half in ('w_in_a', 'w_in_b')]
        else:
            recv = [sched.received[l][n] for l in range(depth)]
        if n in SHARDED_F32:
            recv = [jnp.stack(recv, axis=1).reshape(N_DEV, -1, shp[-1])]
        res = _adamw("adamw_" + n, recv, flat(wts[n]), flat(mom[n]), flat(var[n]), ride=ride)
        out[n] = [r.reshape(shp) for r in res]

    def rep_grad(n):
        if n == 'ln_in_g':
            return dg_in.reshape(-1)
        if n == 'ln_in_b':
            return db_in.reshape(-1)
        return jnp.stack([grads[l][n].reshape(wts[n].shape[1:]) if n != 'b_forget'
                          else grads[l][n][0, :N_HEADS] for l in range(depth)]).reshape(-1)

    sizes = [int(wts[n].size) for n in REPLICATED]
    n_rows = [8 * (-(-sz // (8 * BLK))) for sz in sizes]
    total_rows = -(-sum(n_rows) // (N_DEV * 8)) * (N_DEV * 8)

    def as_rows(v, sz, nr):
        v = v.reshape(-1)
        return (jnp.pad(v, (0, nr * BLK - sz)) if nr * BLK != sz else v).reshape(nr, BLK)

    def pack(vals):
        parts = [as_rows(v, sz, nr) for v, sz, nr in zip(vals, sizes, n_rows)]
        parts.append(jnp.zeros((total_rows - sum(n_rows), BLK), F32))
        return jnp.concatenate(parts, axis=0)

    late, from_layer = sched.deferred
    last_w_in = _Ride([late], gather=False)
    update('w_ffn_in', ride=last_w_in)
    sched.received[from_layer]['w_in_b'], = last_w_in.result
    scatter_small = _Ride([pack([rep_grad(n) for n in REPLICATED]).reshape(N_DEV, total_rows // N_DEV, BLK)],
                          gather=False)
    update('w_ffn_out', ride=scatter_small)
    gather_small = _Ride([_sum_parts("sum_small", scatter_small.result[0])], gather=True)
    update('w_in', ride=gather_small)
    for n in shard_names:
        if n not in out:
            update(n)
    g_rows = gather_small.result[0].reshape(total_rows, BLK)
    starts = [sum(n_rows[:i]) for i in range(len(n_rows))]
    for n, r0, sz, nr in zip(REPLICATED, starts, sizes, n_rows):
        shp = wts[n].shape
        two_d = (1, sz) if len(shp) == 1 else (-1, shp[-1])
        g_n = g_rows[r0:r0 + nr]
        g_n = (g_n if nr * BLK == sz else g_n.reshape(-1)[:sz]).reshape(two_d)
        res = _adamw("adamw_" + n, [g_n[None]], *[d[n].reshape(two_d) for d in (wts, mom, var)])
        out[n] = [r.reshape(shp) for r in res]

    return (loss, grad_x, *[out[n][k] for k in range(4) for n in WEIGHTS])
```

```python
import functools
import math

import jax
import jax.numpy as jnp
from jax import lax
from jax.experimental import pallas as pl
from jax.experimental.pallas import tpu as pltpu

F32 = jnp.float32
BF16 = jnp.bfloat16

N_DEV = 8
D_MODEL = 1024
N_HEADS = 8
HEAD_DIM = 128
N_BLK = 8
BLK = 128
CONV_W = 4
D_PLE = 256
FF_SH = 704
N_FF = 4
IN_SH = 897
N_IN = 7176
DEPTH = 4
RG_C = 8.0
ALPHA = float((2 * DEPTH) ** 0.25)
LN_EPS = 1e-5
SCALE = 1.0 / math.sqrt(HEAD_DIM)
NEG = -1e30
ADAM_LR, ADAM_B1, ADAM_B2, ADAM_EPS, ADAM_WD, ADAM_STEP = 0.001, 0.9, 0.999, 1e-08, 0.01, 10
QKV = 3 * D_MODEL
OFF_RX, OFF_RY, OFF_GA, OFF_GB = (i * D_MODEL for i in range(4))
V7X_VMEM_LIMIT = 48 * 1024 * 1024

WEIGHTS = ['ln_in_g', 'ln_in_b', 'w_in', 'b_forget', 'conv_w', 'conv_b', 'rg_w_a', 'rg_b_a', 'rg_w_x', 'rg_b_x',
           'rg_lambda', 'w_branch_att', 'w_branch_rnn', 'b_merge', 'w_out', 'ln_mix_g', 'ln_mix_b', 'w_ffn_in',
           'w_ffn_out', 'ln_ffn_g', 'ln_ffn_b', 'w_ple', 'w_ple_gate', 'b_ple_gate', 'ln_ple_g', 'ln_ple_b']
SHARDED_BF16 = ['w_in', 'w_branch_att', 'w_branch_rnn', 'w_out', 'w_ffn_in', 'w_ffn_out', 'w_ple', 'w_ple_gate']
SHARDED_F32 = ['conv_w', 'b_merge']
REPLICATED = [n for n in WEIGHTS if n not in SHARDED_BF16 and n not in SHARDED_F32]

NN = ((1,), (0,))
NT = ((1,), (1,))
TN = ((0,), (0,))


class _Ride:
    def __init__(self, arrays, *, gather, index=None):
        self.arrays, self.gather, self.index = list(arrays), gather, index
        self.result = None

    def out_shapes(self):
        if not self.gather:
            return [jax.ShapeDtypeStruct(a.shape, a.dtype) for a in self.arrays]
        cut = 0 if self.index is None else 1
        return [jax.ShapeDtypeStruct((N_DEV,) + a.shape[cut:], a.dtype) for a in self.arrays]

    def scratch(self):
        n = len(self.arrays)
        return [pltpu.SemaphoreType.DMA((n * N_DEV,)), pltpu.SemaphoreType.DMA((n * N_DEV,)),
                pltpu.SemaphoreType.DMA((n,))]

    def _copy(self, a, k, src, dst, sems, to=None):
        send_sems, recv_sems, _ = sems
        return pltpu.make_async_remote_copy(
            src_ref=src, dst_ref=dst, send_sem=send_sems.at[a * N_DEV + k], recv_sem=recv_sems.at[a * N_DEV + k],
            device_id=_peer(k if to is None else to), device_id_type=pl.DeviceIdType.MESH)

    def begin(self, ins, outs, sems):
        me = _my_id()
        started = []
        for a in range(len(ins)):
            if self.gather:
                src = ins[a] if self.index is None else ins[a].at[self.index]
                started.append(pltpu.make_async_copy(src, outs[a].at[me], sems[2].at[a]))
                started += [self._copy(a, k, src, outs[a].at[me], sems) for k in (1, 2, 4, 6)]
            else:
                started.append(pltpu.make_async_copy(ins[a].at[me], outs[a].at[me], sems[2].at[a]))
                started += [self._copy(a, k, ins[a].at[me ^ k], outs[a].at[me], sems) for k in range(1, N_DEV)]
        for cp in started:
            cp.start()

    def finish(self, ins, outs, sems):
        me = _my_id()
        for a in range(len(ins)):
            if self.gather:
                src = ins[a] if self.index is None else ins[a].at[self.index]
                passed = []
                for k in (2, 4, 6):
                    block = outs[a].at[me ^ k]
                    self._copy(a, k, src, block, sems).wait_recv()
                    passed.append(self._copy(a, k + 1, block, block, sems, to=1))
                    passed[-1].start()
                for k in (1, 2, 4, 6):
                    self._copy(a, k, src, outs[a].at[me], sems).wait_send()
                self._copy(a, 1, src, outs[a].at[me ^ 1], sems).wait_recv()
                for cp in passed:
                    cp.wait()
                pltpu.make_async_copy(src, outs[a].at[me], sems[2].at[a]).wait()
            else:
                pltpu.make_async_copy(ins[a].at[me], outs[a].at[me], sems[2].at[a]).wait()
                for k in range(1, N_DEV):
                    self._copy(a, k, ins[a].at[me ^ k], outs[a].at[me], sems).wait()


def _pcall(body, ride=None, **kw):
    if ride is None:
        return pl.pallas_call(body, **kw)
    n = len(ride.arrays)
    grid = kw['grid']
    single = not isinstance(kw['out_shape'], (list, tuple))
    out_specs = [kw['out_specs']] if single else list(kw['out_specs'])
    out_shape = [kw['out_shape']] if single else list(kw['out_shape'])
    in_specs = list(kw['in_specs'])
    scratch = list(kw.get('scratch_shapes', ()))
    n_in, n_out, n_sc = len(in_specs), len(out_shape), len(scratch)
    hbm = pl.BlockSpec(memory_space=pltpu.HBM)

    def wrapped(*refs):
        ins, xin = refs[:n_in], refs[n_in:n_in + n]
        outs, xout = refs[n_in + n:n_in + n + n_out], refs[n_in + n + n_out:n_in + 2 * n + n_out]
        sc, sems = refs[n_in + 2 * n + n_out:n_in + 2 * n + n_out + n_sc], refs[-3:]
        ids = [pl.program_id(ax) for ax in range(len(grid))]
        first = functools.reduce(jnp.logical_and, [i == 0 for i in ids])
        last = functools.reduce(jnp.logical_and, [i == g - 1 for i, g in zip(ids, grid)])

        pl.when(first)(lambda: ride.begin(xin, xout, sems))
        body(*ins, *outs, *sc)
        pl.when(last)(lambda: ride.finish(xin, xout, sems))

    call = pl.pallas_call(wrapped, name=kw['name'], grid=grid, in_specs=in_specs + [hbm] * n,
                          out_specs=out_specs + [hbm] * n, out_shape=out_shape + ride.out_shapes(),
                          scratch_shapes=scratch + ride.scratch(), compiler_params=kw['compiler_params'])

    def run(*args):
        res = call(*args, *ride.arrays)
        ride.result = list(res[n_out:])
        return res[0] if single else list(res[:n_out])

    return run


def _tile(n, pref, mult=8):
    if n <= pref:
        return n
    t = (pref // mult) * mult
    while t >= mult:
        if n % t == 0:
            return t
        t -= mult
    return n


def _cparams(sem):
    return pltpu.CompilerParams(dimension_semantics=sem, vmem_limit_bytes=V7X_VMEM_LIMIT)


def _mm(name, a, b, *, grid, a_spec, b_spec, o_spec, out_shape, contract, ride=None, add=None):
    nk = grid[-1]
    in_out = out_shape.dtype == F32
    acc_shape = tuple(d for d in o_spec.block_shape if d is not None)

    def body(*refs):
        a_ref, b_ref = refs[0], refs[1]
        add_ref = refs[2] if add is not None else None
        o_ref = refs[3] if add is not None else refs[2]
        acc_ref = o_ref if (in_out or nk == 1) else refs[-1]
        k = pl.program_id(len(grid) - 1)
        part = lax.dot_general(a_ref[...].astype(BF16), b_ref[...].astype(BF16), (contract, ((), ())),
                               preferred_element_type=F32)
        if add_ref is not None:
            part = jnp.where(k == 0, part + add_ref[...], part) if nk > 1 else part + add_ref[...]
        if nk == 1:
            o_ref[...] = part.astype(o_ref.dtype)
            return

        @pl.when(k == 0)
        def _():
            acc_ref[...] = part

        @pl.when(k > 0)
        def _():
            acc_ref[...] += part

        if not in_out:
            @pl.when(k == nk - 1)
            def _():
                o_ref[...] = acc_ref[...].astype(o_ref.dtype)

    sem = ("parallel",) * (len(grid) - 1) + ("arbitrary",)
    scratch = [] if (in_out or nk == 1) else [pltpu.VMEM(acc_shape, F32)]
    in_specs, args = [a_spec, b_spec], [a, b]
    if add is not None:
        in_specs.append(o_spec)
        args.append(add)
    return _pcall(body, ride=ride, name=name, grid=grid, in_specs=in_specs, out_specs=o_spec,
                  out_shape=out_shape, scratch_shapes=scratch, compiler_params=_cparams(sem))(*args)


def _mm_nn(name, a, b, *, b_off=0, n=None, out_dtype=F32, tm=1024, tn=1024, tk=1024, ride=None):
    m, k = a.shape
    n = b.shape[1] if n is None else n
    tm, tn, tk = _tile(m, tm), _tile(n, tn, 128), _tile(k, tk, 128)
    no = b_off // tn
    return _mm(name, a, b, grid=(m // tm, n // tn, k // tk),
               a_spec=pl.BlockSpec((tm, tk), lambda i, j, kk: (i, kk)),
               b_spec=pl.BlockSpec((tk, tn), lambda i, j, kk: (kk, j + no)),
               o_spec=pl.BlockSpec((tm, tn), lambda i, j, kk: (i, j)),
               out_shape=jax.ShapeDtypeStruct((m, n), out_dtype), contract=NN, ride=ride)


def _mm_nt(name, a, b, *, out_dtype=F32, tm=1024, tn=1024, tk=1024, ride=None, add=None):
    m, k = a.shape
    n = b.shape[0]
    tm, tn, tk = _tile(m, tm), _tile(n, tn, 128), _tile(k, tk, 128)
    return _mm(name, a, b, grid=(m // tm, n // tn, k // tk),
               a_spec=pl.BlockSpec((tm, tk), lambda i, j, kk: (i, kk)),
               b_spec=pl.BlockSpec((tn, tk), lambda i, j, kk: (j, kk)),
               o_spec=pl.BlockSpec((tm, tn), lambda i, j, kk: (i, j)),
               out_shape=jax.ShapeDtypeStruct((m, n), out_dtype), contract=NT, ride=ride, add=add)


def _mm_tn(name, a, b, *, a_off=0, m=None, out_dtype=F32, tm=1024, tn=1024, tk=2048, ride=None):
    t, n = b.shape
    m = a.shape[1] if m is None else m
    tm, tn, tk = _tile(m, tm, 128), _tile(n, tn, 128), _tile(t, tk)
    mo = a_off // tm
    return _mm(name, a, b, grid=(m // tm, n // tn, t // tk),
               a_spec=pl.BlockSpec((tk, tm), lambda i, j, kk: (kk, i + mo)),
               b_spec=pl.BlockSpec((tk, tn), lambda i, j, kk: (kk, j)),
               o_spec=pl.BlockSpec((tm, tn), lambda i, j, kk: (i, j)),
               out_shape=jax.ShapeDtypeStruct((m, n), out_dtype), contract=TN, ride=ride)


def _rowwise(name, fn, rows, params, out_rows, out_reds, tm=256):
    rows = [r if isinstance(r, tuple) else (r, 0, r.shape[1]) for r in rows]
    t = rows[0][0].shape[0]
    tm = _tile(t, tm)
    in_specs = []
    for _, off, w in rows:
        in_specs.append(pl.BlockSpec((tm, w), functools.partial(lambda i, cb: (i, cb), cb=off // w)))
    for p in params:
        in_specs.append(pl.BlockSpec((1, p.shape[1]), lambda i: (0, 0)))
    out_specs = [pl.BlockSpec((tm, w), lambda i: (i, 0)) for w, _ in out_rows]
    out_specs += [pl.BlockSpec((1, w), lambda i: (0, 0)) for w in out_reds]
    out_shape = [jax.ShapeDtypeStruct((t, w), dt) for w, dt in out_rows]
    out_shape += [jax.ShapeDtypeStruct((1, w), F32) for w in out_reds]
    nr, npar, nor = len(rows), len(params), len(out_rows)

    def body(*refs):
        ins, outs = refs[:nr + npar], refs[nr + npar:]
        vals = [r[...].astype(F32) for r in ins[:nr]]
        vals += [jnp.broadcast_to(r[...], (tm, r.shape[1])) for r in ins[nr:]]
        res = fn(*vals)
        step = pl.program_id(0)
        for o, v in zip(outs[:nor], res[:nor]):
            o[...] = v.astype(o.dtype)
        for o, v in zip(outs[nor:], res[nor:]):
            _accumulate(o, v, step)

    res = _pcall(body, name=name, grid=(t // tm,), in_specs=in_specs, out_specs=out_specs, out_shape=out_shape,
                 compiler_params=_cparams(("arbitrary",)))(*[r[0] for r in rows], *params)
    return res


def _accumulate(o_ref, v, step):
    @pl.when(step == 0)
    def _():
        o_ref[...] = v

    @pl.when(step > 0)
    def _():
        o_ref[...] += v


def _colsum(v):
    return jnp.sum(v, axis=0, keepdims=True)


def _vjp_rowwise(name, fn, rows, params, cots, n_row_grads, tm=256, dtypes=None):
    nr, npar, nc = len(rows), len(params), len(cots)

    def bwd(*vals):
        prim, par, ct = vals[:nr], vals[nr + nc:], vals[nr:nr + nc]
        _, pull = jax.vjp(fn, *prim, *par)
        grads = pull(tuple(ct) if nc > 1 else ct[0])
        return tuple(grads[:n_row_grads]) + tuple(_colsum(g) for g in grads[nr:])

    dtypes = [F32] * n_row_grads if dtypes is None else dtypes
    widths = [(r[2] if isinstance(r, tuple) else r.shape[1], dt) for r, dt in zip(rows[:n_row_grads], dtypes)]
    return _rowwise(name, bwd, list(rows) + list(cots), params, widths, [p.shape[1] for p in params], tm=tm)


def _ln(s, g, b):
    mu = jnp.mean(s, axis=-1, keepdims=True)
    var = jnp.mean(jnp.square(s - mu), axis=-1, keepdims=True)
    return (s - mu) * lax.rsqrt(var + LN_EPS) * g + b


def _softplus(x):
    return jnp.maximum(x, 0.0) + jnp.log1p(jnp.exp(-jnp.abs(x)))


def _expm1(x):
    series = x * (1.0 + x * (1.0 / 2 + x * (1.0 / 6 + x * (1.0 / 24 + x * (1.0 / 120 + x * (1.0 / 720))))))
    return jnp.where(jnp.abs(x) < 0.25, series, jnp.exp(x) - 1.0)


def _f_resid_ln(h, branch, g, b):
    return _ln(ALPHA * h + branch, g, b)


def _f_ple(h, gp, pe, bpg, g, b):
    return _ln(ALPHA * h + jax.nn.sigmoid(gp + bpg) * pe, g, b)


def _f_merge(ga, gb, ya, yb, bm0, bm1):
    return jax.nn.sigmoid(ga + bm0) * ya + jax.nn.sigmoid(gb + bm1) * yb


def _f_rnn_out(hs, ry):
    return hs * jax.nn.gelu(ry, approximate=True)


def _f_logf(fl, bf):
    return -_softplus(-(fl + bf))


def _f_decay(lam):
    return -RG_C * _softplus(-lam)


def _f_gate(xc, ra, ia, decay, ba, bx):
    r = jax.nn.sigmoid(ra + ba)
    i = jax.nn.sigmoid(ia + bx)
    log_a = decay * r
    a = jnp.exp(log_a)
    mult = jnp.sqrt(-_expm1(2.0 * log_a))
    return a, mult * (i * xc)


def _f_act(hg, hu):
    return jax.nn.silu(hg) * hu


ATT_BLOCK = 512
ATT_HEADS_PER_STEP = 1
SCAN_ROWS = 128


def _scores(q, k, cq, ck, diagonal):
    s = lax.dot_general(q, k, (NT, ((), ())), preferred_element_type=F32) * SCALE
    s = s + cq - ck
    if diagonal:
        row = lax.broadcasted_iota(jnp.int32, s.shape, 0)
        col = lax.broadcasted_iota(jnp.int32, s.shape, 1)
        s = jnp.where(col <= row, s, NEG)
    return s


def _dscores(p, do, o, v):
    dob = do.astype(BF16)
    delta = jnp.sum(dob.astype(F32) * o, axis=1, keepdims=True)
    dp = lax.dot_general(dob, v.astype(BF16), (NT, ((), ())), preferred_element_type=F32)
    return p * (dp - delta)


def _attn_fwd(z, cq, ck, bsz, seq, ride=None):
    t = bsz * seq
    tq = _tile(seq, ATT_BLOCK)
    nq = seq // tq

    hp = ATT_HEADS_PER_STEP

    def body(q_ref, k_ref, v_ref, cq_ref, ck_ref, o_ref, ob_ref, lse_ref):
        for hh, i in [(hh, i) for hh in range(hp) for i in range(nq)]:
            lanes = slice(hh * HEAD_DIM, (hh + 1) * HEAD_DIM)
            rows = slice(i * tq, (i + 1) * tq)
            q = q_ref[rows, lanes].astype(BF16)
            cqi = cq_ref[hh, rows, :]

            def step(j, carry, diagonal, q=q, cqi=cqi, hh=hh, lanes=lanes):
                m, l, acc = carry
                keys = pl.ds(pl.multiple_of(j * tq, tq), tq)
                s = _scores(q, k_ref[keys, lanes].astype(BF16), cqi, ck_ref[hh, pl.ds(j, 1), :], diagonal)
                m_new = jnp.maximum(m, jnp.max(s, axis=1, keepdims=True))
                alpha = jnp.exp(m - m_new)
                p = jnp.exp(s - m_new)
                p_hi = p.astype(BF16)
                p_lo = (p - p_hi.astype(F32)).astype(BF16)
                vb = v_ref[keys, lanes].astype(BF16)
                pv = lax.dot_general(p_hi, vb, (NN, ((), ())), preferred_element_type=F32)
                pv = pv + lax.dot_general(p_lo, vb, (NN, ((), ())), preferred_element_type=F32)
                return m_new, alpha * l + jnp.sum(p, axis=1, keepdims=True), alpha * acc + pv

            carry = (jnp.full((tq, 1), NEG, F32), jnp.zeros((tq, 1), F32), jnp.zeros((tq, HEAD_DIM), F32))
            if i > 0:
                carry = lax.fori_loop(0, i, functools.partial(step, diagonal=False), carry)
            m, l, acc = step(i, carry, True)
            o = acc / l
            o_ref[rows, lanes] = o
            ob_ref[rows, lanes] = o.astype(BF16)
            lse_ref[hh, rows, :] = m + jnp.log(l)

    groups = N_HEADS // hp
    head = (seq, hp * HEAD_DIM)
    in_specs = [
        pl.BlockSpec(head, lambda b, g: (b, g)),
        pl.BlockSpec(head, lambda b, g: (b, groups + g)),
        pl.BlockSpec(head, lambda b, g: (b, 2 * groups + g)),
        pl.BlockSpec((None, hp, seq, 1), lambda b, g: (b, g, 0, 0)),
        pl.BlockSpec((None, hp, nq, tq), lambda b, g: (b, g, 0, 0)),
    ]
    out_specs = [pl.BlockSpec(head, lambda b, g: (b, g)), pl.BlockSpec(head, lambda b, g: (b, g)),
                 pl.BlockSpec((None, hp, seq, 1), lambda b, g: (b, g, 0, 0))]
    out_shape = [jax.ShapeDtypeStruct((t, D_MODEL), F32), jax.ShapeDtypeStruct((t, D_MODEL), BF16),
                 jax.ShapeDtypeStruct((bsz, N_HEADS, seq, 1), F32)]
    return _pcall(body, ride=ride, name="attn_fwd", grid=(bsz, groups), in_specs=in_specs, out_specs=out_specs,
                  out_shape=out_shape, compiler_params=_cparams(("parallel", "parallel")))(
                      z, z, z, cq, ck.reshape(bsz, N_HEADS, nq, tq))


def _attn_bwd(z, att, datt, lse, cq, ck, bsz, seq, ride=None):
    t = bsz * seq
    tq = _tile(seq, ATT_BLOCK)
    nq = seq // tq

    def body(q_ref, k_ref, v_ref, o_ref, do_ref, lse_ref, cq_ref, ck_ref,
             dq_ref, dk_ref, dv_ref, dcq_ref, dck_ref, dq_sc):
        dq_sc[...] = jnp.zeros_like(dq_sc)
        dcq_ref[...] = jnp.zeros_like(dcq_ref)
        for j in range(nq):
            keys = slice(j * tq, (j + 1) * tq)
            kb = k_ref[keys, :].astype(BF16)
            vb = v_ref[keys, :].astype(BF16)
            ckj = ck_ref[j:j + 1, :]

            def step(i, carry, diagonal, kb=kb, vb=vb, ckj=ckj):
                dk, dv, dc = carry
                rows = pl.ds(pl.multiple_of(i * tq, tq), tq)
                qb = q_ref[rows, :].astype(BF16)
                do = do_ref[rows, :]
                s = _scores(qb, kb, cq_ref[rows, :], ckj, diagonal)
                p = jnp.exp(s - lse_ref[rows, :])
                ds = _dscores(p, do, o_ref[rows, :], vb)
                dsb = (ds * SCALE).astype(BF16)
                dq_sc[rows, :] += lax.dot_general(dsb, kb, (NN, ((), ())), preferred_element_type=F32)
                dcq_ref[rows, :] += jnp.sum(ds, axis=1, keepdims=True)
                dv = dv + lax.dot_general(p.astype(BF16), do.astype(BF16), (TN, ((), ())),
                                          preferred_element_type=F32)
                dk = dk + lax.dot_general(dsb, qb, (TN, ((), ())), preferred_element_type=F32)
                return dk, dv, dc - jnp.sum(ds, axis=0, keepdims=True)

            zero = jnp.zeros((tq, HEAD_DIM), F32)
            carry = step(j, (zero, zero, jnp.zeros((1, tq), F32)), True)
            if j + 1 < nq:
                carry = lax.fori_loop(j + 1, nq, functools.partial(step, diagonal=False), carry)
            dk, dv, dck_ref[j:j + 1, :] = carry
            dk_ref[keys, :] = dk.astype(BF16)
            dv_ref[keys, :] = dv.astype(BF16)
        dq_ref[...] = dq_sc[...].astype(BF16)

    head = (seq, HEAD_DIM)
    hmap = lambda b, h: (b, h)
    col = pl.BlockSpec((None, None, seq, 1), lambda b, h: (b, h, 0, 0))
    row = pl.BlockSpec((None, None, nq, tq), lambda b, h: (b, h, 0, 0))
    in_specs = [pl.BlockSpec(head, hmap),
                pl.BlockSpec(head, lambda b, h: (b, N_HEADS + h)),
                pl.BlockSpec(head, lambda b, h: (b, 2 * N_HEADS + h)),
                pl.BlockSpec(head, hmap), pl.BlockSpec(head, hmap), col, col, row]
    big = jax.ShapeDtypeStruct((t, D_MODEL), BF16)
    return _pcall(body, ride=ride, name="attn_bwd", grid=(bsz, N_HEADS), in_specs=in_specs,
                  out_specs=[pl.BlockSpec(head, hmap)] * 3 + [col, row],
                  out_shape=[big, big, big, jax.ShapeDtypeStruct((bsz, N_HEADS, seq, 1), F32),
                             jax.ShapeDtypeStruct((bsz, N_HEADS, nq, tq), F32)],
                  scratch_shapes=[pltpu.VMEM(head, F32)],
                  compiler_params=_cparams(("parallel", "parallel")))(
                      z, z, z, att, datt, lse, cq, ck.reshape(bsz, N_HEADS, nq, tq))


def _scan(name, a, u, bsz, seq, *, reverse, with_prev=False, tb=512):
    c = u.shape[1]
    tb = _tile(seq, tb)
    nb = seq // tb
    rc = SCAN_ROWS if tb % SCAN_ROWS == 0 else tb
    has_a = a is not None

    def body(*refs):
        if has_a:
            a_ref, u_ref = refs[0], refs[1]
            rest = refs[2:]
        else:
            u_ref = refs[0]
            rest = refs[1:]
        outs = rest[:2] if with_prev else rest[:1]
        carry_sc, afirst_sc = rest[-2], rest[-1]
        step = pl.program_id(1)

        @pl.when(step == 0)
        def _():
            carry_sc[...] = jnp.zeros_like(carry_sc)
            afirst_sc[...] = jnp.zeros_like(afirst_sc)

        row = lax.broadcasted_iota(jnp.int32, (rc, BLK), 0)
        pieces = list(range(tb // rc))
        for ls in range(c // BLK):
            lanes = slice(ls * BLK, (ls + 1) * BLK)
            carry = carry_sc[:, lanes]
            afirst = afirst_sc[:, lanes]
            for pc in (reversed(pieces) if reverse else pieces):
                rows = slice(pc * rc, (pc + 1) * rc)
                uu = u_ref[rows, lanes]
                if has_a:
                    aa = a_ref[rows, lanes]
                    coef = jnp.where(row < rc - 1, pltpu.roll(aa, rc - 1, 0), afirst) if reverse else aa
                k = 1
                while k < rc:
                    shift = rc - k if reverse else k
                    keep = (row < rc - k) if reverse else (row >= k)
                    uu_sh = jnp.where(keep, pltpu.roll(uu, shift, 0), 0.0)
                    if has_a:
                        uu = coef * uu_sh + uu
                        coef = coef * jnp.where(keep, pltpu.roll(coef, shift, 0), 1.0)
                    else:
                        uu = uu + uu_sh
                    k *= 2
                h = uu + coef * carry if has_a else uu + carry
                outs[0][rows, lanes] = h
                if with_prev:
                    outs[1][rows, lanes] = jnp.where(row >= 1, pltpu.roll(h, 1, 0), carry)
                edge = pc * rc if reverse else (pc + 1) * rc - 1
                carry = outs[0][edge:edge + 1, lanes]
                if has_a and reverse:
                    afirst = a_ref[edge:edge + 1, lanes]
            carry_sc[:, lanes] = carry
            if has_a and reverse:
                afirst_sc[:, lanes] = afirst

    if reverse:
        imap = lambda b, s: (b * nb + nb - 1 - s, 0)
    else:
        imap = lambda b, s: (b * nb + s, 0)
    spec = pl.BlockSpec((tb, c), imap)
    n_in = 2 if has_a else 1
    n_out = 2 if with_prev else 1
    res = _pcall(body, name=name, grid=(bsz, nb), in_specs=[spec] * n_in, out_specs=[spec] * n_out,
                 out_shape=[jax.ShapeDtypeStruct(u.shape, F32)] * n_out,
                 scratch_shapes=[pltpu.VMEM((1, c), F32), pltpu.VMEM((1, c), F32)],
                 compiler_params=_cparams(("parallel", "arbitrary")))(*([a, u] if has_a else [u]))
    return res if with_prev else res[0]


def _conv_fwd(z, w, b, bsz, seq, tb=256):
    c = D_MODEL
    t = bsz * seq
    tb = _tile(seq, tb)
    nb = seq // tb

    def body(x_ref, w_ref, b_ref, o_ref, tail_sc):
        step = pl.program_id(1)

        @pl.when(step == 0)
        def _():
            tail_sc[...] = jnp.zeros_like(tail_sc)

        x = x_ref[...]
        row8 = lax.broadcasted_iota(jnp.int32, (8, c), 0)
        tail = tail_sc[...]
        acc = w_ref[CONV_W - 1:CONV_W, :] * x + b_ref[...]
        for sh in range(1, CONV_W):
            xs = pltpu.roll(x, sh, 0)
            top = jnp.where(row8 < sh, pltpu.roll(tail, sh, 0), xs[0:8, :])
            xs = jnp.concatenate([top, xs[8:, :]], axis=0) if tb > 8 else top
            acc = acc + w_ref[CONV_W - 1 - sh:CONV_W - sh, :] * xs
        o_ref[...] = acc
        tail_sc[...] = x_ref[tb - 8:tb, :]

    return _pcall(body, name="conv_fwd", grid=(bsz, nb),
                  in_specs=[pl.BlockSpec((tb, c), lambda bb, s: (bb * nb + s, OFF_RX // c)),
                            pl.BlockSpec((CONV_W, c), lambda bb, s: (0, 0)),
                            pl.BlockSpec((1, c), lambda bb, s: (0, 0))],
                  out_specs=pl.BlockSpec((tb, c), lambda bb, s: (bb * nb + s, 0)),
                  out_shape=jax.ShapeDtypeStruct((t, c), F32),
                  scratch_shapes=[pltpu.VMEM((8, c), F32)],
                  compiler_params=_cparams(("parallel", "arbitrary")))(z, w, b)


def _conv_bwd(z, dxc, w, bsz, seq, tb=256):
    c = D_MODEL
    t = bsz * seq
    tb = _tile(seq, tb)
    nb = seq // tb

    def body(x_ref, g_ref, w_ref, dx_ref, dw_ref, db_ref, head_sc):
        bb, step = pl.program_id(0), pl.program_id(1)

        @pl.when(step == 0)
        def _():
            head_sc[...] = jnp.zeros_like(head_sc)

        x, g = x_ref[...], g_ref[...]
        row8 = lax.broadcasted_iota(jnp.int32, (8, c), 0)
        head = head_sc[...]
        dx = w_ref[CONV_W - 1:CONV_W, :] * g
        dws = [None] * CONV_W
        dws[CONV_W - 1] = _colsum(g * x)
        for sh in range(1, CONV_W):
            gs = pltpu.roll(g, tb - sh, 0)
            bot = jnp.where(row8 >= 8 - sh, pltpu.roll(head, 8 - sh, 0), gs[tb - 8:tb, :])
            gs = jnp.concatenate([gs[:tb - 8, :], bot], axis=0) if tb > 8 else bot
            dx = dx + w_ref[CONV_W - 1 - sh:CONV_W - sh, :] * gs
            dws[CONV_W - 1 - sh] = _colsum(gs * x)
        dx_ref[...] = dx.astype(dx_ref.dtype)
        first = (bb == 0) & (step == 0)
        dw = jnp.concatenate(dws, axis=0)
        db = _colsum(g)

        @pl.when(first)
        def _():
            dw_ref[...] = dw
            db_ref[...] = db

        @pl.when(jnp.logical_not(first))
        def _():
            dw_ref[...] += dw
            db_ref[...] += db

        head_sc[...] = g_ref[0:8, :]

    rmap = lambda bb, s: (bb * nb + nb - 1 - s, 0)
    return _pcall(body, name="conv_bwd", grid=(bsz, nb),
                  in_specs=[pl.BlockSpec((tb, c), lambda bb, s: (bb * nb + nb - 1 - s, OFF_RX // c)),
                            pl.BlockSpec((tb, c), rmap),
                            pl.BlockSpec((CONV_W, c), lambda bb, s: (0, 0))],
                  out_specs=[pl.BlockSpec((tb, c), rmap),
                             pl.BlockSpec((CONV_W, c), lambda bb, s: (0, 0)),
                             pl.BlockSpec((1, c), lambda bb, s: (0, 0))],
                  out_shape=[jax.ShapeDtypeStruct((t, c), BF16), jax.ShapeDtypeStruct((CONV_W, c), F32),
                             jax.ShapeDtypeStruct((1, c), F32)],
                  scratch_shapes=[pltpu.VMEM((8, c), F32)],
                  compiler_params=_cparams(("arbitrary", "arbitrary")))(z, dxc, w)


def _gate_fwd(xc, w_a, w_x, b_a, b_x, lam, tm=1024):
    t = xc.shape[0]
    tm = _tile(t, tm)

    def body(xc_ref, wa_ref, wx_ref, ba_ref, bx_ref, lam_ref, a_ref, u_ref):
        xc_b = xc_ref[...]
        xb = xc_b.astype(BF16)
        ra = lax.dot_general(xb, wa_ref[...].astype(BF16), (NN, ((), ())), preferred_element_type=F32)
        ia = lax.dot_general(xb, wx_ref[...].astype(BF16), (NN, ((), ())), preferred_element_type=F32)
        a, u = _f_gate(xc_b, ra, ia, lam_ref[...], ba_ref[...], bx_ref[...])
        a_ref[...] = a
        u_ref[...] = u

    row = pl.BlockSpec((tm, BLK), lambda n, i: (i, n))
    wsp = pl.BlockSpec((None, BLK, BLK), lambda n, i: (n, 0, 0))
    vec = pl.BlockSpec((1, BLK), lambda n, i: (0, n))
    return _pcall(body, name="gate_fwd", grid=(N_BLK, t // tm), in_specs=[row, wsp, wsp, vec, vec, vec],
                  out_specs=[row, row], out_shape=[jax.ShapeDtypeStruct((t, D_MODEL), F32)] * 2,
                  compiler_params=_cparams(("parallel", "parallel")))(xc, w_a, w_x, b_a, b_x, lam)


def _gate_bwd(xc, w_a, w_x, b_a, b_x, lam, hprev, du, tm=1024, ride=None):
    t = xc.shape[0]
    tm = _tile(t, tm)

    def body(xc_ref, wa_ref, wx_ref, ba_ref, bx_ref, lam_ref, hp_ref, du_ref,
             dxc_ref, dwa_ref, dwx_ref, dba_ref, dbx_ref, dlam_ref):
        step = pl.program_id(1)
        xc_b = xc_ref[...]
        xb = xc_b.astype(BF16)
        wa, wx = wa_ref[...].astype(BF16), wx_ref[...].astype(BF16)
        ra = lax.dot_general(xb, wa, (NN, ((), ())), preferred_element_type=F32)
        ia = lax.dot_general(xb, wx, (NN, ((), ())), preferred_element_type=F32)
        full = lambda r: jnp.broadcast_to(r[...], (tm, BLK))
        _, pull = jax.vjp(_f_gate, xc_b, ra, ia, full(lam_ref), full(ba_ref), full(bx_ref))
        du_b = du_ref[...]
        dxc, dra, dia, dlam, dba, dbx = pull((du_b * hp_ref[...], du_b))
        drb, dib = dra.astype(BF16), dia.astype(BF16)
        dxc = dxc + lax.dot_general(drb, wa, (NT, ((), ())), preferred_element_type=F32)
        dxc = dxc + lax.dot_general(dib, wx, (NT, ((), ())), preferred_element_type=F32)
        dxc_ref[...] = dxc
        _accumulate(dwa_ref, lax.dot_general(xb, drb, (TN, ((), ())), preferred_element_type=F32), step)
        _accumulate(dwx_ref, lax.dot_general(xb, dib, (TN, ((), ())), preferred_element_type=F32), step)
        _accumulate(dba_ref, _colsum(dba), step)
        _accumulate(dbx_ref, _colsum(dbx), step)
        _accumulate(dlam_ref, _colsum(dlam), step)

    row = pl.BlockSpec((tm, BLK), lambda n, i: (i, n))
    wsp = pl.BlockSpec((None, BLK, BLK), lambda n, i: (n, 0, 0))
    vec = pl.BlockSpec((1, BLK), lambda n, i: (0, n))
    wshape = jax.ShapeDtypeStruct((N_BLK, BLK, BLK), F32)
    vshape = jax.ShapeDtypeStruct((1, D_MODEL), F32)
    return _pcall(body, ride=ride, name="gate_bwd", grid=(N_BLK, t // tm),
                  in_specs=[row, wsp, wsp, vec, vec, vec, row, row],
                  out_specs=[row, wsp, wsp, vec, vec, vec],
                  out_shape=[jax.ShapeDtypeStruct((t, D_MODEL), F32), wshape, wshape, vshape, vshape, vshape],
                  compiler_params=_cparams(("parallel", "arbitrary")))(xc, w_a, w_x, b_a, b_x, lam, hprev, du)


def _ffn_in_act(a, w, tm=1024, ride=None):
    t = a.shape[0]
    tm = _tile(t, tm)

    def body(a_ref, wg_ref, wu_ref, hgu_ref, act_ref):
        ab = a_ref[...].astype(BF16)
        hg = lax.dot_general(ab, wg_ref[...].astype(BF16), (NN, ((), ())), preferred_element_type=F32)
        hu = lax.dot_general(ab, wu_ref[...].astype(BF16), (NN, ((), ())), preferred_element_type=F32)
        hgu_ref[0] = hg
        hgu_ref[1] = hu
        act_ref[...] = _f_act(hg, hu).astype(act_ref.dtype)

    wspec = lambda off: pl.BlockSpec((None, D_MODEL, FF_SH), lambda i, s: (s + off, 0, 0))
    hgu, act = _pcall(body, ride=ride, name="ffn_in", grid=(t // tm, N_FF),
                      in_specs=[pl.BlockSpec((tm, D_MODEL), lambda i, s: (i, 0)), wspec(0), wspec(N_FF)],
                      out_specs=[pl.BlockSpec((2, None, tm, FF_SH), lambda i, s: (0, s, i, 0)),
                                 pl.BlockSpec((None, tm, FF_SH), lambda i, s: (s, i, 0))],
                      out_shape=[jax.ShapeDtypeStruct((2, N_FF, t, FF_SH), F32),
                                 jax.ShapeDtypeStruct((N_FF, t, FF_SH), BF16)],
                      compiler_params=_cparams(("parallel", "parallel")))(a, w, w)
    return hgu.reshape(2 * N_FF, t, FF_SH), act


def _ffn_out_dx_act(d, w, hgu, tm=512, ride=None):
    t = d.shape[0]
    tm = _tile(t, tm)

    def body(d_ref, w_ref, hg_ref, hu_ref, o_ref):
        dact = lax.dot_general(d_ref[...].astype(BF16), w_ref[...].astype(BF16), (NT, ((), ())),
                               preferred_element_type=F32)
        _, pull = jax.vjp(_f_act, hg_ref[...], hu_ref[...])
        dhg, dhu = pull(dact)
        o_ref[0] = dhg.astype(o_ref.dtype)
        o_ref[1] = dhu.astype(o_ref.dtype)

    hspec = lambda off: pl.BlockSpec((None, tm, FF_SH), lambda i, s: (s + off, i, 0))
    res = _pcall(body, ride=ride, name="ffn_out_dx", grid=(t // tm, N_FF),
                 in_specs=[pl.BlockSpec((tm, D_MODEL), lambda i, s: (i, 0)),
                           pl.BlockSpec((None, FF_SH, D_MODEL), lambda i, s: (s, 0, 0)), hspec(0), hspec(N_FF)],
                 out_specs=pl.BlockSpec((2, None, tm, FF_SH), lambda i, s: (0, s, i, 0)),
                 out_shape=jax.ShapeDtypeStruct((2, N_FF, t, FF_SH), BF16),
                 compiler_params=_cparams(("parallel", "parallel")))(d, w, hgu, hgu)
    return res.reshape(2 * N_FF, t, FF_SH)


def _adamw(name, parts, w, m, v, tr=128, ride=None):
    ng = len(parts)
    n_src, r, c = parts[0].shape
    per = w.shape[1] // r
    assert w.shape[0] * per == ng and w.shape[2] == c
    tr = _tile(r, tr)
    nb = r // tr
    bc1 = 1.0 - ADAM_B1 ** ADAM_STEP
    bc2 = 1.0 - ADAM_B2 ** ADAM_STEP

    def body(*refs):
        p_refs = refs[:ng]
        w_ref, m_ref, v_ref, g_ref, d_ref, nm_ref, nv_ref = refs[ng:]
        grp = pl.program_id(0)

        def update(p_ref):
            g = p_ref[0].astype(F32)
            for s in range(1, n_src):
                g = g + p_ref[s].astype(F32)
            nm = ADAM_B1 * m_ref[...] + (1.0 - ADAM_B1) * g
            nv = ADAM_B2 * v_ref[...] + (1.0 - ADAM_B2) * jnp.square(g)
            g_ref[...] = g
            nm_ref[...] = nm
            nv_ref[...] = nv
            d_ref[...] = -ADAM_LR * ((nm / bc1) / (jnp.sqrt(nv / bc2) + ADAM_EPS) + ADAM_WD * w_ref[...])

        for k in range(ng):
            pl.when(grp == k)(functools.partial(update, p_refs[k]))

    p_specs = [pl.BlockSpec((n_src, tr, c), functools.partial(lambda gi, i, k: (0, jnp.where(gi == k, i, 0), 0), k=k))
               for k in range(ng)]
    spec = pl.BlockSpec((None, tr, c), lambda gi, i: (gi // per, (gi % per) * nb + i, 0))
    return _pcall(body, ride=ride, name=name, grid=(ng, nb), in_specs=p_specs + [spec, spec, spec],
                  out_specs=[spec] * 4, out_shape=[jax.ShapeDtypeStruct(w.shape, F32)] * 4,
                  compiler_params=_cparams(("parallel", "parallel")))(*parts, w, m, v)


def _sum_parts(name, parts, tr=256):
    _, r, c = parts.shape
    tr = _tile(r, tr)

    def body(p_ref, o_ref):
        g = p_ref[0]
        for s in range(1, parts.shape[0]):
            g = g + p_ref[s]
        o_ref[...] = g

    return _pcall(body, name=name, grid=(r // tr,),
                  in_specs=[pl.BlockSpec((parts.shape[0], tr, c), lambda i: (0, i, 0))],
                  out_specs=pl.BlockSpec((tr, c), lambda i: (i, 0)),
                  out_shape=jax.ShapeDtypeStruct((r, c), F32), compiler_params=_cparams(("parallel",)))(parts)


def _peer(k):
    x, y, c = lax.axis_index("x"), lax.axis_index("y"), lax.axis_index("c")
    return (x ^ ((k >> 2) & 1), y ^ ((k >> 1) & 1), c ^ (k & 1))


def _my_id():
    return 4 * lax.axis_index("x") + 2 * lax.axis_index("y") + lax.axis_index("c")


def _exchange(name, ride):
    n = len(ride.arrays)

    def body(*refs):
        ride.begin(refs[:n], refs[n:2 * n], refs[2 * n:])
        ride.finish(refs[:n], refs[n:2 * n], refs[2 * n:])

    hbm = pl.BlockSpec(memory_space=pltpu.HBM)
    return _pcall(body, name=name, in_specs=[hbm] * n, out_specs=[hbm] * n, out_shape=ride.out_shapes(),
                  scratch_shapes=ride.scratch())(*ride.arrays)


def _row(v):
    return v.reshape(1, -1)


def _time_major_heads(c, bsz, seq):
    return c.reshape(bsz, seq, BLK)[:, :, :N_HEADS].transpose(0, 2, 1)


def _no_ride(*_):
    return None


TWICE = [(D_MODEL, F32), (D_MODEL, BF16)]


def _both(fn):
    def run(*v):
        y = fn(*v)
        return y, y
    return run


def _layer_fwd(h, hb, p_l, w, bsz, seq, ride_of=_no_ride):
    t = bsz * seq
    zq = _mm_nn("z_proj_qkv", hb, w['w_in7'], n=QKV, out_dtype=BF16, ride=ride_of('z_proj_qkv'))
    zr = _mm_nn("z_proj_rest", hb, w['w_in7'], b_off=QKV, n=4 * D_MODEL, ride=ride_of('z_proj_rest'))
    fl = _mm_nn("f_proj", hb, w['w_inf'])
    logf, = _rowwise("logf_fwd", lambda f, b: (_f_logf(f, b),), [fl], [w['b_forget']], [(BLK, F32)], [])
    c = _scan("cumsum_fwd", None, logf, bsz, seq, reverse=False)
    ct = _time_major_heads(c, bsz, seq)
    cq, ck = ct[..., None], ct[:, :, None, :]
    att, attb, lse = _attn_fwd(zq, cq, ck, bsz, seq, ride=ride_of('attn_fwd'))
    xc = _conv_fwd(zr, w['conv_w'], w['conv_b'], bsz, seq)
    decay, = _rowwise("decay_fwd", lambda lam: (_f_decay(lam),), [w['rg_lambda']], [], [(D_MODEL, F32)], [])
    a, u = _gate_fwd(xc, w['rg_w_a'], w['rg_w_x'], w['rg_b_a'], w['rg_b_x'], decay)
    hs, hprev = _scan("lru_fwd", a, u, bsz, seq, reverse=False, with_prev=True)
    rnn, = _rowwise("rnn_out_fwd", lambda s, y: (_f_rnn_out(s, y),), [hs, (zr, OFF_RY, D_MODEL)], [],
                    [(D_MODEL, BF16)], [])
    ya = _mm_nn("branch_att", attb, w['w_branch_att'])
    yb = _mm_nn("branch_rnn", rnn, w['w_branch_rnn'])
    merged, = _rowwise("merge_fwd", lambda *v: (_f_merge(*v),),
                       [(zr, OFF_GA, D_MODEL), (zr, OFF_GB, D_MODEL), ya, yb], [w['b_merge0'], w['b_merge1']],
                       [(D_MODEL, BF16)], [])
    mix = _mm_nn("mix_out", merged, w['w_out'])
    h1, h1b = _rowwise("ln_mix_fwd", _both(_f_resid_ln), [h, mix], [w['ln_mix_g'], w['ln_mix_b']], TWICE, [])
    tm = _tile(t, 1024)
    hgu, act = _ffn_in_act(h1b, w['w_ffn_in'], ride=ride_of('ffn_in'))
    ffn = _mm("ffn_out", act, w['w_ffn_out'], grid=(t // tm, 1, N_FF),
              a_spec=pl.BlockSpec((None, tm, FF_SH), lambda i, j, s: (s, i, 0)),
              b_spec=pl.BlockSpec((None, FF_SH, D_MODEL), lambda i, j, s: (s, 0, 0)),
              o_spec=pl.BlockSpec((tm, D_MODEL), lambda i, j, s: (i, 0)),
              out_shape=jax.ShapeDtypeStruct((t, D_MODEL), F32), contract=NN, ride=ride_of('ffn_out'))
    h2, h2b = _rowwise("ln_ffn_fwd", _both(_f_resid_ln), [h1, ffn], [w['ln_ffn_g'], w['ln_ffn_b']], TWICE, [])
    gp = _mm_nn("ple_gate", h2b, w['w_ple_gate'])
    pe = _mm_nn("ple_proj", p_l, w['w_ple'])
    h3, h3b = _rowwise("ln_ple_fwd", _both(_f_ple), [h2, gp, pe],
                       [w['b_ple_gate'], w['ln_ple_g'], w['ln_ple_b']], TWICE, [])
    saved = dict(h=h, hb=hb, zq=zq, zr=zr, fl=fl, cq=cq, ck=ck, att=att, attb=attb, lse=lse, xc=xc, a=a, decay=decay,
                 hprev=hprev, hs=hs, rnn=rnn, ya=ya, yb=yb, merged=merged, mix=mix, h1=h1, h1b=h1b, hgu=hgu,
                 act=act, ffn=ffn, h2=h2, h2b=h2b, gp=gp, pe=pe)
    return h3, h3b, saved


def _layer_bwd(dh3, p_l, w, s, bsz, seq, ride_of=_no_ride):
    t = bsz * seq
    g = {}
    dh2, dgp, dpe, g['b_ple_gate'], g['ln_ple_g'], g['ln_ple_b'] = _vjp_rowwise(
        "ln_ple_bwd", _f_ple, [s['h2'], s['gp'], s['pe']], [w['b_ple_gate'], w['ln_ple_g'], w['ln_ple_b']], [dh3], 3,
        dtypes=[F32, BF16, BF16])
    g['w_ple_gate'] = _mm_tn("ple_gate_dw", s['h2b'], dgp, out_dtype=BF16)
    g['w_ple'] = _mm_tn("ple_proj_dw", p_l, dpe, out_dtype=BF16)
    dh2b = _mm_nt("ple_gate_dx", dgp, w['w_ple_gate'])
    dh1, dffn, g['ln_ffn_g'], g['ln_ffn_b'] = _ln_resid_bwd(
        "ln_ffn_bwd", s['h1'], s['ffn'], w['ln_ffn_g'], w['ln_ffn_b'], dh2, dh2b)
    tm = _tile(t, 1024)
    tk = _tile(t, 2048)
    g['w_ffn_out'] = _mm("ffn_out_dw", s['act'], dffn, grid=(N_FF, 1, t // tk),
                         a_spec=pl.BlockSpec((None, tk, FF_SH), lambda ss, j, k: (ss, k, 0)),
                         b_spec=pl.BlockSpec((tk, D_MODEL), lambda ss, j, k: (k, 0)),
                         o_spec=pl.BlockSpec((None, FF_SH, D_MODEL), lambda ss, j, k: (ss, 0, 0)),
                         out_shape=jax.ShapeDtypeStruct((N_FF, FF_SH, D_MODEL), BF16), contract=TN)
    dhgu = _ffn_out_dx_act(dffn, w['w_ffn_out'], s['hgu'], ride=ride_of('ffn_out_dx', g))
    g['w_ffn_in'] = _mm("ffn_in_dw", s['h1b'], dhgu, grid=(2 * N_FF, 1, t // tk),
                        a_spec=pl.BlockSpec((tk, D_MODEL), lambda ss, j, k: (k, 0)),
                        b_spec=pl.BlockSpec((None, tk, FF_SH), lambda ss, j, k: (ss, k, 0)),
                        o_spec=pl.BlockSpec((None, D_MODEL, FF_SH), lambda ss, j, k: (ss, 0, 0)),
                        out_shape=jax.ShapeDtypeStruct((2 * N_FF, D_MODEL, FF_SH), BF16), contract=TN,
                        ride=ride_of('ffn_in_dw', g))
    dh1b = _mm("ffn_in_dx", dhgu, w['w_ffn_in'], grid=(t // tm, 1, 2 * N_FF),
               a_spec=pl.BlockSpec((None, tm, FF_SH), lambda i, j, ss: (ss, i, 0)),
               b_spec=pl.BlockSpec((None, D_MODEL, FF_SH), lambda i, j, ss: (ss, 0, 0)),
               o_spec=pl.BlockSpec((tm, D_MODEL), lambda i, j, ss: (i, 0)),
               out_shape=jax.ShapeDtypeStruct((t, D_MODEL), F32), contract=NT, ride=ride_of('ffn_in_dx', g))
    dh, dmix, g['ln_mix_g'], g['ln_mix_b'] = _ln_resid_bwd(
        "ln_mix_bwd", s['h'], s['mix'], w['ln_mix_g'], w['ln_mix_b'], dh1, dh1b)
    g['w_out'] = _mm_tn("mix_out_dw", s['merged'], dmix, out_dtype=BF16)
    dmerged = _mm_nt("mix_out_dx", dmix, w['w_out'])
    z = s['zr']
    dga, dgb, dya, dyb, dbm0, dbm1 = _vjp_rowwise(
        "merge_bwd", _f_merge, [(z, OFF_GA, D_MODEL), (z, OFF_GB, D_MODEL), s['ya'], s['yb']],
        [w['b_merge0'], w['b_merge1']], [dmerged], 4, dtypes=[BF16] * 4)
    g['b_merge'] = jnp.concatenate([dbm0, dbm1], axis=0)
    g['w_branch_att'] = _mm_tn("branch_att_dw", s['attb'], dya, out_dtype=BF16)
    g['w_branch_rnn'] = _mm_tn("branch_rnn_dw", s['rnn'], dyb, out_dtype=BF16)
    datt = _mm_nt("branch_att_dx", dya, w['w_branch_att'], out_dtype=BF16)
    drnn = _mm_nt("branch_rnn_dx", dyb, w['w_branch_rnn'])
    dhs, dry = _vjp_rowwise("rnn_out_bwd", _f_rnn_out, [s['hs'], (z, OFF_RY, D_MODEL)], [], [drnn], 2,
                            dtypes=[F32, BF16])
    lam = _scan("lru_bwd", s['a'], dhs, bsz, seq, reverse=True)
    dxc, g['rg_w_a'], g['rg_w_x'], g['rg_b_a'], g['rg_b_x'], ddecay = _gate_bwd(
        s['xc'], w['rg_w_a'], w['rg_w_x'], w['rg_b_a'], w['rg_b_x'], s['decay'], s['hprev'], lam,
        ride=ride_of('gate_bwd', g))
    g['rg_lambda'], = _vjp_rowwise("decay_bwd", _f_decay, [w['rg_lambda']], [], [ddecay], 1)
    drx, g['conv_w'], g['conv_b'] = _conv_bwd(z, dxc, w['conv_w'], bsz, seq)
    dq, dk, dv, dcq, dck = _attn_bwd(s['zq'], s['att'], datt, s['lse'], s['cq'], s['ck'], bsz, seq,
                                     ride=ride_of('attn_bwd', g))
    dc = (dcq[:, :, :, 0] + dck.reshape(bsz, N_HEADS, seq)).transpose(0, 2, 1)
    dc = jnp.pad(dc, ((0, 0), (0, 0), (0, BLK - N_HEADS))).reshape(t, BLK)
    dlogf = _scan("cumsum_bwd", None, dc, bsz, seq, reverse=True)
    dfl, g['b_forget'] = _vjp_rowwise("logf_bwd", _f_logf, [s['fl']], [w['b_forget']], [dlogf], 1, dtypes=[BF16])
    dz = jnp.concatenate([dq, dk, dv, drx, dry, dga, dgb], axis=1)
    g['w_in7'] = _mm_tn("z_proj_dw", s['hb'], dz, out_dtype=BF16)
    g['w_inf'] = _mm_tn("f_proj_dw", s['hb'], dfl, out_dtype=BF16)
    dh = _mm_nt("z_proj_dx", dz, w['w_in7'], ride=ride_of('z_proj_dx', g), add=dh)
    dh = _mm_nt("f_proj_dx", dfl, w['w_inf'], add=dh)
    return dh, g


def _ln_resid_bwd(name, h, branch, gam, bet, d0, d1):
    def bwd(hv, bv, d0v, d1v, gv, btv):
        _, pull = jax.vjp(_f_resid_ln, hv, bv, gv, btv)
        dh, db, dg, dbt = pull(d0v + d1v)
        return dh, db, _colsum(dg), _colsum(dbt)

    return _rowwise(name, bwd, [h, branch, d0, d1], [gam, bet], [(D_MODEL, F32), (D_MODEL, BF16)],
                    [D_MODEL, D_MODEL])


class _Schedule:
    FWD = {'z_proj_qkv': ['w_ffn_out'], 'z_proj_rest': ['w_branch_att', 'w_branch_rnn', 'w_out', 'w_ple_gate'],
           'attn_fwd': ['w_in'], 'ffn_in': ['w_ffn_in'], 'ffn_out': ['w_ple', 'conv_w', 'b_merge']}
    BWD = {'ffn_out_dx': ['w_ffn_out'], 'ffn_in_dw': ['w_ple_gate', 'w_ple'],
           'gate_bwd': ['w_out', 'w_branch_att', 'w_branch_rnn'],
           'attn_bwd': ['w_ffn_in', 'conv_w', 'b_merge']}

    def __init__(self, shards, depth):
        self.shards, self.depth = shards, depth
        self.gathered = [{} for _ in range(depth)]
        self.received = [{} for _ in range(depth)]
        self.pending = []
        self.deferred = None

    def gather_ride(self, layer, kernel_name):
        if layer + 1 >= self.depth:
            return None
        names = self.FWD[kernel_name]
        ride = _Ride([self.shards[n] for n in names], gather=True, index=layer + 1)
        self.pending.append((ride, names, self.gathered[layer + 1]))
        return ride

    def _scatter(self, arrays, names, layer):
        ride = _Ride(arrays, gather=False)
        self.pending.append((ride, names, self.received[layer]))
        return ride

    def scatter_ride(self, layer, kernel_name, grads):
        if kernel_name == 'z_proj_dx':
            whole = _by_destination('w_in', grads)
            half = whole.shape[1] // 2
            self.deferred = (whole[:, half:], layer)
            return self._scatter([whole[:, :half]], ['w_in_a'], layer)
        if kernel_name == 'ffn_in_dx':
            if self.deferred is None:
                return None
            (late, from_layer), self.deferred = self.deferred, None
            return self._scatter([late], ['w_in_b'], from_layer)
        names = self.BWD[kernel_name]
        return self._scatter([_by_destination(n, grads) for n in names], names, layer)

    def collect(self):
        for ride, names, dst in self.pending:
            dst.update(zip(names, ride.result))
        self.pending = []


def _local_step(x2, tgt, p3, weights_of, depth, g_in, b_in, bsz, seq, sched=None):
    h, hb = _rowwise("ln_in_fwd", _both(_ln), [x2], [g_in, b_in], TWICE, [])
    p3 = p3.astype(BF16)
    saved, layer_w = [], []
    for l in range(depth):
        layer_w.append(weights_of(l))
        ride_of = functools.partial(sched.gather_ride, l) if sched else _no_ride
        h, hb, s = _layer_fwd(h, hb, p3[l], layer_w[l], bsz, seq, ride_of)
        if sched:
            sched.collect()
        saved.append(s)

    def loss_fn(y, tv):
        err = y - tv
        return err * (1.0 / D_MODEL), _colsum(jnp.square(err))

    dh, sq = _rowwise("loss", loss_fn, [h, tgt], [], [(D_MODEL, F32)], [D_MODEL])
    grads = [None] * depth
    for l in reversed(range(depth)):
        ride_of = functools.partial(sched.scatter_ride, l) if sched else _no_ride
        dh, grads[l] = _layer_bwd(dh, p3[l], layer_w[l], saved[l], bsz, seq, ride_of)
        if sched:
            sched.collect()
    dx, dg_in, db_in = _vjp_rowwise("ln_in_bwd", _ln, [x2], [g_in, b_in], [dh], 1)
    return sq, dx, grads, dg_in, db_in


def _layer_weights(full):
    w = {}
    wt = full['w_in'].transpose(1, 0, 2).reshape(D_MODEL, N_IN)
    w['w_in7'] = jnp.concatenate([wt[:, :3 * D_MODEL], wt[:, 3 * D_MODEL + N_HEADS:]], axis=1)
    w['w_inf'] = jnp.pad(wt[:, 3 * D_MODEL:3 * D_MODEL + N_HEADS], ((0, 0), (0, BLK - N_HEADS)))
    for n in ['w_branch_att', 'w_branch_rnn', 'w_out', 'w_ple_gate']:
        w[n] = full[n].reshape(D_MODEL, D_MODEL)
    w['w_ffn_in'] = full['w_ffn_in']
    w['w_ffn_out'] = full['w_ffn_out'].reshape(N_FF, FF_SH, D_MODEL)
    w['w_ple'] = full['w_ple'].transpose(1, 0, 2).reshape(D_PLE, D_MODEL)
    w['conv_w'] = full['conv_w'].transpose(1, 0, 2).reshape(CONV_W, D_MODEL)
    bm = full['b_merge'].transpose(1, 0, 2).reshape(2, D_MODEL)
    w['b_merge0'], w['b_merge1'] = bm[0:1], bm[1:2]
    return w


def _by_destination(name, gw):
    if name == 'w_in':
        g7, gf = gw['w_in7'], gw['w_inf']
        true = jnp.concatenate([g7[:, :3 * D_MODEL], gf[:, :N_HEADS], g7[:, 3 * D_MODEL:]], axis=1)
        return true.reshape(D_MODEL, N_DEV, IN_SH).transpose(1, 0, 2)
    g = gw[name]
    if name in ('w_branch_att', 'w_branch_rnn', 'w_out', 'w_ple_gate'):
        return g.reshape(N_DEV, D_MODEL // N_DEV, D_MODEL)
    if name == 'w_ffn_in':
        return g
    if name == 'w_ffn_out':
        return g.reshape(N_DEV, N_FF * FF_SH // N_DEV, D_MODEL)
    return g.reshape(g.shape[0], N_DEV, BLK).transpose(1, 0, 2)


def kernel(x, p, ln_in_g, ln_in_b, w_in, b_forget, conv_w, conv_b, rg_w_a, rg_b_a, rg_w_x, rg_b_x, rg_lambda, w_branch_att, w_branch_rnn, b_merge, w_out, ln_mix_g, ln_mix_b, w_ffn_in, w_ffn_out, ln_ffn_g, ln_ffn_b, w_ple, w_ple_gate, b_ple_gate, ln_ple_g, ln_ple_b, loss_target, m_ln_in_g, m_ln_in_b, m_w_in, m_b_forget, m_conv_w, m_conv_b, m_rg_w_a, m_rg_b_a, m_rg_w_x, m_rg_b_x, m_rg_lambda, m_w_branch_att, m_w_branch_rnn, m_b_merge, m_w_out, m_ln_mix_g, m_ln_mix_b, m_w_ffn_in, m_w_ffn_out, m_ln_ffn_g, m_ln_ffn_b, m_w_ple, m_w_ple_gate, m_b_ple_gate, m_ln_ple_g, m_ln_ple_b, v_ln_in_g, v_ln_in_b, v_w_in, v_b_forget, v_conv_w, v_conv_b, v_rg_w_a, v_rg_b_a, v_rg_w_x, v_rg_b_x, v_rg_lambda, v_w_branch_att, v_w_branch_rnn, v_b_merge, v_w_out, v_ln_mix_g, v_ln_mix_b, v_w_ffn_in, v_w_ffn_out, v_ln_ffn_g, v_ln_ffn_b, v_w_ple, v_w_ple_gate, v_b_ple_gate, v_ln_ple_g, v_ln_ple_b):
    env = dict(locals())
    wts = {n: env[n] for n in WEIGHTS}
    mom = {n: env['m_' + n] for n in WEIGHTS}
    var = {n: env['v_' + n] for n in WEIGHTS}
    bsz, seq, _ = x.shape
    depth = w_in.shape[0]
    t = bsz * seq
    x2, tgt = x.reshape(t, D_MODEL), loss_target.reshape(t, D_MODEL)
    p3 = p.reshape(depth, t, D_PLE)

    shard_names = SHARDED_BF16 + SHARDED_F32
    shards = {n: wts[n].astype(BF16) for n in SHARDED_BF16}
    shards.update({n: wts[n] for n in SHARDED_F32})
    sched = _Schedule(shards, depth)
    first = _Ride([shards[n] for n in shard_names], gather=True, index=0)
    sched.gathered[0] = dict(zip(shard_names, _exchange("gather_layer0", first)))

    def weights_of(l):
        w = _layer_weights(sched.gathered[l])
        for n in ['conv_b', 'rg_b_a', 'rg_b_x', 'rg_lambda', 'ln_mix_g', 'ln_mix_b', 'ln_ffn_g', 'ln_ffn_b',
                  'b_ple_gate', 'ln_ple_g', 'ln_ple_b']:
            w[n] = _row(wts[n][l])
        w['b_forget'] = jnp.pad(_row(b_forget[l]), ((0, 0), (0, BLK - N_HEADS)))
        w['rg_w_a'], w['rg_w_x'] = rg_w_a[l], rg_w_x[l]
        return w

    g_in, b_in = _row(ln_in_g), _row(ln_in_b)
    sq, dx, grads, dg_in, db_in = _local_step(x2, tgt, p3, weights_of, depth, g_in, b_in, bsz, seq, sched)
    loss = lax.psum(0.5 * jnp.sum(sq) / D_MODEL, ("x", "y", "c"))
    grad_x = dx.reshape(bsz, seq, D_MODEL)

    out = {}

    def update(n, ride=None):
        shp = wts[n].shape
        view = lambda a: a
        if n == 'w_in':
            recv = [sched.received[l][half] for l in range(depth) for half in ('w_in_a', 'w_in_b')]
        else:
            recv = [sched.received[l][n] for l in range(depth)]
        if n in SHARDED_F32:
            recv = [jnp.stack(recv, axis=1).reshape(N_DEV, -1, shp[-1])]
            view = lambda a: a.reshape(1, -1, shp[-1])
        res = _adamw("adamw_" + n, recv, view(wts[n]), view(mom[n]), view(var[n]), ride=ride)
        out[n] = [r.reshape(shp) for r in res]

    def rep_grad(n):
        if n == 'ln_in_g':
            return dg_in.reshape(-1)
        if n == 'ln_in_b':
            return db_in.reshape(-1)
        return jnp.stack([grads[l][n].reshape(wts[n].shape[1:]) if n != 'b_forget'
                          else grads[l][n][0, :N_HEADS] for l in range(depth)]).reshape(-1)

    sizes = [int(wts[n].size) for n in REPLICATED]
    n_rows = [8 * (-(-sz // (8 * BLK))) for sz in sizes]
    total_rows = -(-sum(n_rows) // (N_DEV * 8)) * (N_DEV * 8)

    def as_rows(v, sz, nr):
        v = v.reshape(-1)
        return (jnp.pad(v, (0, nr * BLK - sz)) if nr * BLK != sz else v).reshape(nr, BLK)

    def pack(vals):
        parts = [as_rows(v, sz, nr) for v, sz, nr in zip(vals, sizes, n_rows)]
        parts.append(jnp.zeros((total_rows - sum(n_rows), BLK), F32))
        return jnp.concatenate(parts, axis=0)

    late, from_layer = sched.deferred
    last_w_in = _Ride([late], gather=False)
    update('w_ffn_in', ride=last_w_in)
    sched.received[from_layer]['w_in_b'], = last_w_in.result
    scatter_small = _Ride([pack([rep_grad(n) for n in REPLICATED]).reshape(N_DEV, total_rows // N_DEV, BLK)],
                          gather=False)
    update('w_ffn_out', ride=scatter_small)
    gather_small = _Ride([_sum_parts("sum_small", scatter_small.result[0])], gather=True)
    update('w_in', ride=gather_small)
    for n in shard_names:
        if n not in out:
            update(n)
    g_rows = gather_small.result[0].reshape(total_rows, BLK)
    starts = [sum(n_rows[:i]) for i in range(len(n_rows))]
    for n, r0, sz, nr in zip(REPLICATED, starts, sizes, n_rows):
        shp = wts[n].shape
        as_one = (1, 1, sz) if len(shp) == 1 else (1, -1, shp[-1])
        g_n = g_rows[r0:r0 + nr]
        g_n = (g_n if nr * BLK == sz else g_n.reshape(-1)[:sz]).reshape(as_one)
        res = _adamw("adamw_" + n, [g_n], *[d[n].reshape(as_one) for d in (wts, mom, var)])
        out[n] = [r.reshape(shp) for r in res]

    return (loss, grad_x, *[out[n][k] for k in range(4) for n in WEIGHTS])
```

```python
import functools
import math

import jax
import jax.numpy as jnp
from jax import lax
from jax.experimental import pallas as pl
from jax.experimental.pallas import tpu as pltpu

F32 = jnp.float32
BF16 = jnp.bfloat16

N_DEV = 8
D_MODEL = 1024
N_HEADS = 8
HEAD_DIM = 128
N_BLK = 8
BLK = 128
CONV_W = 4
D_PLE = 256
FF_SH = 704
N_FF = 4
IN_SH = 897
N_IN = 7176
DEPTH = 4
RG_C = 8.0
ALPHA = float((2 * DEPTH) ** 0.25)
LN_EPS = 1e-5
SCALE = 1.0 / math.sqrt(HEAD_DIM)
NEG = -1e30
ADAM_LR, ADAM_B1, ADAM_B2, ADAM_EPS, ADAM_WD, ADAM_STEP = 0.001, 0.9, 0.999, 1e-08, 0.01, 10
QKV = 3 * D_MODEL
OFF_RX, OFF_RY, OFF_GA, OFF_GB = (i * D_MODEL for i in range(4))
V7X_VMEM_LIMIT = 48 * 1024 * 1024

WEIGHTS = ['ln_in_g', 'ln_in_b', 'w_in', 'b_forget', 'conv_w', 'conv_b', 'rg_w_a', 'rg_b_a', 'rg_w_x', 'rg_b_x',
           'rg_lambda', 'w_branch_att', 'w_branch_rnn', 'b_merge', 'w_out', 'ln_mix_g', 'ln_mix_b', 'w_ffn_in',
           'w_ffn_out', 'ln_ffn_g', 'ln_ffn_b', 'w_ple', 'w_ple_gate', 'b_ple_gate', 'ln_ple_g', 'ln_ple_b']
SHARDED_BF16 = ['w_in', 'w_branch_att', 'w_branch_rnn', 'w_out', 'w_ffn_in', 'w_ffn_out', 'w_ple', 'w_ple_gate']
SHARDED_F32 = ['conv_w', 'b_merge']
REPLICATED = [n for n in WEIGHTS if n not in SHARDED_BF16 and n not in SHARDED_F32]

NN = ((1,), (0,))
NT = ((1,), (1,))
TN = ((0,), (0,))


class _Ride:
    def __init__(self, arrays, *, gather, index=None):
        self.arrays, self.gather, self.index = list(arrays), gather, index
        self.result = None

    def _shard(self, ins, a):
        return ins[a] if self.index is None else ins[a].at[self.index[a]]

    def out_shapes(self):
        if not self.gather:
            return [jax.ShapeDtypeStruct(a.shape, a.dtype) for a in self.arrays]
        cut = 0 if self.index is None else 1
        return [jax.ShapeDtypeStruct((N_DEV,) + a.shape[cut:], a.dtype) for a in self.arrays]

    def scratch(self):
        n = len(self.arrays)
        return [pltpu.SemaphoreType.DMA((n * N_DEV,)), pltpu.SemaphoreType.DMA((n * N_DEV,)),
                pltpu.SemaphoreType.DMA((n,))]

    def _copy(self, a, k, src, dst, sems, to=None):
        send_sems, recv_sems, _ = sems
        return pltpu.make_async_remote_copy(
            src_ref=src, dst_ref=dst, send_sem=send_sems.at[a * N_DEV + k], recv_sem=recv_sems.at[a * N_DEV + k],
            device_id=_peer(k if to is None else to), device_id_type=pl.DeviceIdType.MESH)

    def begin(self, ins, outs, sems):
        me = _my_id()
        started = []
        for a in range(len(ins)):
            if self.gather:
                src = self._shard(ins, a)
                started.append(pltpu.make_async_copy(src, outs[a].at[me], sems[2].at[a]))
                started += [self._copy(a, k, src, outs[a].at[me], sems) for k in (1, 2, 4, 6)]
            else:
                started.append(pltpu.make_async_copy(ins[a].at[me], outs[a].at[me], sems[2].at[a]))
                started += [self._copy(a, k, ins[a].at[me ^ k], outs[a].at[me], sems) for k in range(1, N_DEV)]
        for cp in started:
            cp.start()

    def finish(self, ins, outs, sems):
        me = _my_id()
        for a in range(len(ins)):
            if self.gather:
                src = self._shard(ins, a)
                passed = []
                for k in (2, 4, 6):
                    block = outs[a].at[me ^ k]
                    self._copy(a, k, src, block, sems).wait_recv()
                    passed.append(self._copy(a, k + 1, block, block, sems, to=1))
                    passed[-1].start()
                for k in (1, 2, 4, 6):
                    self._copy(a, k, src, outs[a].at[me], sems).wait_send()
                self._copy(a, 1, src, outs[a].at[me ^ 1], sems).wait_recv()
                for cp in passed:
                    cp.wait()
                pltpu.make_async_copy(src, outs[a].at[me], sems[2].at[a]).wait()
            else:
                pltpu.make_async_copy(ins[a].at[me], outs[a].at[me], sems[2].at[a]).wait()
                for k in range(1, N_DEV):
                    self._copy(a, k, ins[a].at[me ^ k], outs[a].at[me], sems).wait()


def _pcall(body, ride=None, **kw):
    if ride is None:
        return pl.pallas_call(body, **kw)
    n = len(ride.arrays)
    grid = kw['grid']
    single = not isinstance(kw['out_shape'], (list, tuple))
    out_specs = [kw['out_specs']] if single else list(kw['out_specs'])
    out_shape = [kw['out_shape']] if single else list(kw['out_shape'])
    in_specs = list(kw['in_specs'])
    scratch = list(kw.get('scratch_shapes', ()))
    n_in, n_out, n_sc = len(in_specs), len(out_shape), len(scratch)
    hbm = pl.BlockSpec(memory_space=pltpu.HBM)

    def wrapped(*refs):
        ins, xin = refs[:n_in], refs[n_in:n_in + n]
        outs, xout = refs[n_in + n:n_in + n + n_out], refs[n_in + n + n_out:n_in + 2 * n + n_out]
        sc, sems = refs[n_in + 2 * n + n_out:n_in + 2 * n + n_out + n_sc], refs[-3:]
        ids = [pl.program_id(ax) for ax in range(len(grid))]
        first = functools.reduce(jnp.logical_and, [i == 0 for i in ids])
        last = functools.reduce(jnp.logical_and, [i == g - 1 for i, g in zip(ids, grid)])

        pl.when(first)(lambda: ride.begin(xin, xout, sems))
        body(*ins, *outs, *sc)
        pl.when(last)(lambda: ride.finish(xin, xout, sems))

    call = pl.pallas_call(wrapped, name=kw['name'], grid=grid, in_specs=in_specs + [hbm] * n,
                          out_specs=out_specs + [hbm] * n, out_shape=out_shape + ride.out_shapes(),
                          scratch_shapes=scratch + ride.scratch(), compiler_params=kw['compiler_params'])

    def run(*args):
        res = call(*args, *ride.arrays)
        ride.result = list(res[n_out:])
        return res[0] if single else list(res[:n_out])

    return run


def _tile(n, pref, mult=8):
    if n <= pref:
        return n
    t = (pref // mult) * mult
    while t >= mult:
        if n % t == 0:
            return t
        t -= mult
    return n


def _cparams(sem):
    return pltpu.CompilerParams(dimension_semantics=sem, vmem_limit_bytes=V7X_VMEM_LIMIT)


def _mm(name, a, b, *, grid, a_spec, b_spec, o_spec, out_shape, contract, ride=None, add=None):
    nk = grid[-1]
    in_out = out_shape.dtype == F32
    acc_shape = tuple(d for d in o_spec.block_shape if d is not None)

    def body(*refs):
        a_ref, b_ref = refs[0], refs[1]
        add_ref = refs[2] if add is not None else None
        o_ref = refs[3] if add is not None else refs[2]
        acc_ref = o_ref if (in_out or nk == 1) else refs[-1]
        k = pl.program_id(len(grid) - 1)
        part = lax.dot_general(a_ref[...].astype(BF16), b_ref[...].astype(BF16), (contract, ((), ())),
                               preferred_element_type=F32)
        if add_ref is not None:
            part = jnp.where(k == 0, part + add_ref[...], part) if nk > 1 else part + add_ref[...]
        if nk == 1:
            o_ref[...] = part.astype(o_ref.dtype)
            return

        @pl.when(k == 0)
        def _():
            acc_ref[...] = part

        @pl.when(k > 0)
        def _():
            acc_ref[...] += part

        if not in_out:
            @pl.when(k == nk - 1)
            def _():
                o_ref[...] = acc_ref[...].astype(o_ref.dtype)

    sem = ("parallel",) * (len(grid) - 1) + ("arbitrary",)
    scratch = [] if (in_out or nk == 1) else [pltpu.VMEM(acc_shape, F32)]
    in_specs, args = [a_spec, b_spec], [a, b]
    if add is not None:
        in_specs.append(o_spec)
        args.append(add)
    return _pcall(body, ride=ride, name=name, grid=grid, in_specs=in_specs, out_specs=o_spec,
                  out_shape=out_shape, scratch_shapes=scratch, compiler_params=_cparams(sem))(*args)


def _mm_nn(name, a, b, *, b_off=0, n=None, out_dtype=F32, tm=1024, tn=1024, tk=1024, ride=None):
    m, k = a.shape
    n = b.shape[1] if n is None else n
    tm, tn, tk = _tile(m, tm), _tile(n, tn, 128), _tile(k, tk, 128)
    no = b_off // tn
    return _mm(name, a, b, grid=(m // tm, n // tn, k // tk),
               a_spec=pl.BlockSpec((tm, tk), lambda i, j, kk: (i, kk)),
               b_spec=pl.BlockSpec((tk, tn), lambda i, j, kk: (kk, j + no)),
               o_spec=pl.BlockSpec((tm, tn), lambda i, j, kk: (i, j)),
               out_shape=jax.ShapeDtypeStruct((m, n), out_dtype), contract=NN, ride=ride)


def _mm_nt(name, a, b, *, out_dtype=F32, tm=1024, tn=1024, tk=1024, ride=None, add=None):
    m, k = a.shape
    n = b.shape[0]
    tm, tn, tk = _tile(m, tm), _tile(n, tn, 128), _tile(k, tk, 128)
    return _mm(name, a, b, grid=(m // tm, n // tn, k // tk),
               a_spec=pl.BlockSpec((tm, tk), lambda i, j, kk: (i, kk)),
               b_spec=pl.BlockSpec((tn, tk), lambda i, j, kk: (j, kk)),
               o_spec=pl.BlockSpec((tm, tn), lambda i, j, kk: (i, j)),
               out_shape=jax.ShapeDtypeStruct((m, n), out_dtype), contract=NT, ride=ride, add=add)


def _mm_tn(name, a, b, *, a_off=0, m=None, out_dtype=F32, tm=1024, tn=1024, tk=2048, ride=None):
    t, n = b.shape
    m = a.shape[1] if m is None else m
    tm, tn, tk = _tile(m, tm, 128), _tile(n, tn, 128), _tile(t, tk)
    mo = a_off // tm
    return _mm(name, a, b, grid=(m // tm, n // tn, t // tk),
               a_spec=pl.BlockSpec((tk, tm), lambda i, j, kk: (kk, i + mo)),
               b_spec=pl.BlockSpec((tk, tn), lambda i, j, kk: (kk, j)),
               o_spec=pl.BlockSpec((tm, tn), lambda i, j, kk: (i, j)),
               out_shape=jax.ShapeDtypeStruct((m, n), out_dtype), contract=TN, ride=ride)


def _rowwise(name, fn, rows, params, out_rows, out_reds, tm=512):
    rows = [r if isinstance(r, tuple) else (r, 0, r.shape[1]) for r in rows]
    t = rows[0][0].shape[0]
    tm = _tile(t, tm)
    in_specs = []
    for _, off, w in rows:
        in_specs.append(pl.BlockSpec((tm, w), functools.partial(lambda i, cb: (i, cb), cb=off // w)))
    for p in params:
        in_specs.append(pl.BlockSpec((1, p.shape[1]), lambda i: (0, 0)))
    out_specs = [pl.BlockSpec((tm, w), lambda i: (i, 0)) for w, _ in out_rows]
    out_specs += [pl.BlockSpec((1, w), lambda i: (0, 0)) for w in out_reds]
    out_shape = [jax.ShapeDtypeStruct((t, w), dt) for w, dt in out_rows]
    out_shape += [jax.ShapeDtypeStruct((1, w), F32) for w in out_reds]
    nr, npar, nor = len(rows), len(params), len(out_rows)

    def body(*refs):
        ins, outs = refs[:nr + npar], refs[nr + npar:]
        vals = [r[...].astype(F32) for r in ins[:nr]]
        vals += [jnp.broadcast_to(r[...], (tm, r.shape[1])) for r in ins[nr:]]
        res = fn(*vals)
        step = pl.program_id(0)
        for o, v in zip(outs[:nor], res[:nor]):
            o[...] = v.astype(o.dtype)
        for o, v in zip(outs[nor:], res[nor:]):
            _accumulate(o, v, step)

    res = _pcall(body, name=name, grid=(t // tm,), in_specs=in_specs, out_specs=out_specs, out_shape=out_shape,
                 compiler_params=_cparams(("arbitrary",)))(*[r[0] for r in rows], *params)
    return res


def _accumulate(o_ref, v, step):
    @pl.when(step == 0)
    def _():
        o_ref[...] = v

    @pl.when(step > 0)
    def _():
        o_ref[...] += v


def _colsum(v):
    return jnp.sum(v, axis=0, keepdims=True)


def _vjp_rowwise(name, fn, rows, params, cots, n_row_grads, tm=256, dtypes=None):
    nr, npar, nc = len(rows), len(params), len(cots)

    def bwd(*vals):
        prim, par, ct = vals[:nr], vals[nr + nc:], vals[nr:nr + nc]
        _, pull = jax.vjp(fn, *prim, *par)
        grads = pull(tuple(ct) if nc > 1 else ct[0])
        return tuple(grads[:n_row_grads]) + tuple(_colsum(g) for g in grads[nr:])

    dtypes = [F32] * n_row_grads if dtypes is None else dtypes
    widths = [(r[2] if isinstance(r, tuple) else r.shape[1], dt) for r, dt in zip(rows[:n_row_grads], dtypes)]
    return _rowwise(name, bwd, list(rows) + list(cots), params, widths, [p.shape[1] for p in params], tm=tm)


def _ln(s, g, b):
    mu = jnp.mean(s, axis=-1, keepdims=True)
    var = jnp.mean(jnp.square(s - mu), axis=-1, keepdims=True)
    return (s - mu) * lax.rsqrt(var + LN_EPS) * g + b


def _softplus(x):
    return jnp.maximum(x, 0.0) + jnp.log1p(jnp.exp(-jnp.abs(x)))


def _expm1(x):
    series = x * (1.0 + x * (1.0 / 2 + x * (1.0 / 6 + x * (1.0 / 24 + x * (1.0 / 120 + x * (1.0 / 720))))))
    return jnp.where(jnp.abs(x) < 0.25, series, jnp.exp(x) - 1.0)


def _f_resid_ln(h, branch, g, b):
    return _ln(ALPHA * h + branch, g, b)


def _f_ple(h, gp, pe, bpg, g, b):
    return _ln(ALPHA * h + jax.nn.sigmoid(gp + bpg) * pe, g, b)


def _f_merge(ga, gb, ya, yb, bm0, bm1):
    return jax.nn.sigmoid(ga + bm0) * ya + jax.nn.sigmoid(gb + bm1) * yb


def _f_rnn_out(hs, ry):
    return hs * jax.nn.gelu(ry, approximate=True)


def _f_logf(fl, bf):
    return -_softplus(-(fl + bf))


def _f_decay(lam):
    return -RG_C * _softplus(-lam)


def _f_gate(xc, ra, ia, decay, ba, bx):
    r = jax.nn.sigmoid(ra + ba)
    i = jax.nn.sigmoid(ia + bx)
    log_a = decay * r
    a = jnp.exp(log_a)
    mult = jnp.sqrt(-_expm1(2.0 * log_a))
    return a, mult * (i * xc)


def _f_act(hg, hu):
    return jax.nn.silu(hg) * hu


ATT_BLOCK = 512
ATT_HEADS_PER_STEP = 1
SCAN_ROWS = 128


def _scores(q, k, cq, ck, diagonal):
    s = lax.dot_general(q, k, (NT, ((), ())), preferred_element_type=F32) * SCALE
    s = s + cq - ck
    if diagonal:
        row = lax.broadcasted_iota(jnp.int32, s.shape, 0)
        col = lax.broadcasted_iota(jnp.int32, s.shape, 1)
        s = jnp.where(col <= row, s, NEG)
    return s


def _dscores(p, do, o, v):
    dob = do.astype(BF16)
    delta = jnp.sum(dob.astype(F32) * o, axis=1, keepdims=True)
    dp = lax.dot_general(dob, v.astype(BF16), (NT, ((), ())), preferred_element_type=F32)
    return p * (dp - delta)


def _attn_fwd(z, cq, ck, bsz, seq, ride=None):
    t = bsz * seq
    tq = _tile(seq, ATT_BLOCK)
    nq = seq // tq

    hp = ATT_HEADS_PER_STEP

    def body(q_ref, k_ref, v_ref, cq_ref, ck_ref, o_ref, ob_ref, lse_ref):
        for hh, i in [(hh, i) for hh in range(hp) for i in range(nq)]:
            lanes = slice(hh * HEAD_DIM, (hh + 1) * HEAD_DIM)
            rows = slice(i * tq, (i + 1) * tq)
            q = q_ref[rows, lanes].astype(BF16)
            cqi = cq_ref[hh, rows, :]

            def step(j, carry, diagonal, q=q, cqi=cqi, hh=hh, lanes=lanes):
                m, l, acc = carry
                keys = pl.ds(pl.multiple_of(j * tq, tq), tq)
                s = _scores(q, k_ref[keys, lanes].astype(BF16), cqi, ck_ref[hh, pl.ds(j, 1), :], diagonal)
                m_new = jnp.maximum(m, jnp.max(s, axis=1, keepdims=True))
                alpha = jnp.exp(m - m_new)
                p = jnp.exp(s - m_new)
                p_hi = p.astype(BF16)
                p_lo = (p - p_hi.astype(F32)).astype(BF16)
                vb = v_ref[keys, lanes].astype(BF16)
                pv = lax.dot_general(p_hi, vb, (NN, ((), ())), preferred_element_type=F32)
                pv = pv + lax.dot_general(p_lo, vb, (NN, ((), ())), preferred_element_type=F32)
                return m_new, alpha * l + jnp.sum(p, axis=1, keepdims=True), alpha * acc + pv

            carry = (jnp.full((tq, 1), NEG, F32), jnp.zeros((tq, 1), F32), jnp.zeros((tq, HEAD_DIM), F32))
            if i > 0:
                carry = lax.fori_loop(0, i, functools.partial(step, diagonal=False), carry)
            m, l, acc = step(i, carry, True)
            o = acc / l
            o_ref[rows, lanes] = o
            ob_ref[rows, lanes] = o.astype(BF16)
            lse_ref[hh, rows, :] = m + jnp.log(l)

    groups = N_HEADS // hp
    head = (seq, hp * HEAD_DIM)
    in_specs = [
        pl.BlockSpec(head, lambda b, g: (b, g)),
        pl.BlockSpec(head, lambda b, g: (b, groups + g)),
        pl.BlockSpec(head, lambda b, g: (b, 2 * groups + g)),
        pl.BlockSpec((None, hp, seq, 1), lambda b, g: (b, g, 0, 0)),
        pl.BlockSpec((None, hp, nq, tq), lambda b, g: (b, g, 0, 0)),
    ]
    out_specs = [pl.BlockSpec(head, lambda b, g: (b, g)), pl.BlockSpec(head, lambda b, g: (b, g)),
                 pl.BlockSpec((None, hp, seq, 1), lambda b, g: (b, g, 0, 0))]
    out_shape = [jax.ShapeDtypeStruct((t, D_MODEL), F32), jax.ShapeDtypeStruct((t, D_MODEL), BF16),
                 jax.ShapeDtypeStruct((bsz, N_HEADS, seq, 1), F32)]
    return _pcall(body, ride=ride, name="attn_fwd", grid=(bsz, groups), in_specs=in_specs, out_specs=out_specs,
                  out_shape=out_shape, compiler_params=_cparams(("parallel", "parallel")))(
                      z, z, z, cq, ck.reshape(bsz, N_HEADS, nq, tq))


def _attn_bwd(z, att, datt, lse, cq, ck, bsz, seq, ride=None):
    t = bsz * seq
    tq = _tile(seq, ATT_BLOCK)
    nq = seq // tq

    def body(q_ref, k_ref, v_ref, o_ref, do_ref, lse_ref, cq_ref, ck_ref,
             dq_ref, dk_ref, dv_ref, dcq_ref, dck_ref, dq_sc):
        dq_sc[...] = jnp.zeros_like(dq_sc)
        dcq_ref[...] = jnp.zeros_like(dcq_ref)
        for j in range(nq):
            keys = slice(j * tq, (j + 1) * tq)
            kb = k_ref[keys, :].astype(BF16)
            vb = v_ref[keys, :].astype(BF16)
            ckj = ck_ref[j:j + 1, :]

            def step(i, carry, diagonal, kb=kb, vb=vb, ckj=ckj):
                dk, dv, dc = carry
                rows = pl.ds(pl.multiple_of(i * tq, tq), tq)
                qb = q_ref[rows, :].astype(BF16)
                do = do_ref[rows, :]
                s = _scores(qb, kb, cq_ref[rows, :], ckj, diagonal)
                p = jnp.exp(s - lse_ref[rows, :])
                ds = _dscores(p, do, o_ref[rows, :], vb)
                dsb = (ds * SCALE).astype(BF16)
                dq_sc[rows, :] += lax.dot_general(dsb, kb, (NN, ((), ())), preferred_element_type=F32)
                dcq_ref[rows, :] += jnp.sum(ds, axis=1, keepdims=True)
                dv = dv + lax.dot_general(p.astype(BF16), do.astype(BF16), (TN, ((), ())),
                                          preferred_element_type=F32)
                dk = dk + lax.dot_general(dsb, qb, (TN, ((), ())), preferred_element_type=F32)
                return dk, dv, dc - jnp.sum(ds, axis=0, keepdims=True)

            zero = jnp.zeros((tq, HEAD_DIM), F32)
            carry = step(j, (zero, zero, jnp.zeros((1, tq), F32)), True)
            if j + 1 < nq:
                carry = lax.fori_loop(j + 1, nq, functools.partial(step, diagonal=False), carry)
            dk, dv, dck_ref[j:j + 1, :] = carry
            dk_ref[keys, :] = dk.astype(BF16)
            dv_ref[keys, :] = dv.astype(BF16)
        dq_ref[...] = dq_sc[...].astype(BF16)

    head = (seq, HEAD_DIM)
    hmap = lambda b, h: (b, h)
    col = pl.BlockSpec((None, None, seq, 1), lambda b, h: (b, h, 0, 0))
    row = pl.BlockSpec((None, None, nq, tq), lambda b, h: (b, h, 0, 0))
    in_specs = [pl.BlockSpec(head, hmap),
                pl.BlockSpec(head, lambda b, h: (b, N_HEADS + h)),
                pl.BlockSpec(head, lambda b, h: (b, 2 * N_HEADS + h)),
                pl.BlockSpec(head, hmap), pl.BlockSpec(head, hmap), col, col, row]
    big = jax.ShapeDtypeStruct((t, D_MODEL), BF16)
    return _pcall(body, ride=ride, name="attn_bwd", grid=(bsz, N_HEADS), in_specs=in_specs,
                  out_specs=[pl.BlockSpec(head, hmap)] * 3 + [col, row],
                  out_shape=[big, big, big, jax.ShapeDtypeStruct((bsz, N_HEADS, seq, 1), F32),
                             jax.ShapeDtypeStruct((bsz, N_HEADS, nq, tq), F32)],
                  scratch_shapes=[pltpu.VMEM(head, F32)],
                  compiler_params=_cparams(("parallel", "parallel")))(
                      z, z, z, att, datt, lse, cq, ck.reshape(bsz, N_HEADS, nq, tq))


def _scan(name, a, u, bsz, seq, *, reverse, with_prev=False, tb=512, ride=None):
    c = u.shape[1]
    tb = _tile(seq, tb)
    nb = seq // tb
    rc = SCAN_ROWS if tb % SCAN_ROWS == 0 else tb
    has_a = a is not None

    def body(*refs):
        if has_a:
            a_ref, u_ref = refs[0], refs[1]
            rest = refs[2:]
        else:
            u_ref = refs[0]
            rest = refs[1:]
        outs = rest[:2] if with_prev else rest[:1]
        carry_sc, afirst_sc = rest[-2], rest[-1]
        step = pl.program_id(1)

        @pl.when(step == 0)
        def _():
            carry_sc[...] = jnp.zeros_like(carry_sc)
            afirst_sc[...] = jnp.zeros_like(afirst_sc)

        row = lax.broadcasted_iota(jnp.int32, (rc, BLK), 0)
        pieces = list(range(tb // rc))
        for ls in range(c // BLK):
            lanes = slice(ls * BLK, (ls + 1) * BLK)
            carry = carry_sc[:, lanes]
            afirst = afirst_sc[:, lanes]
            for pc in (reversed(pieces) if reverse else pieces):
                rows = slice(pc * rc, (pc + 1) * rc)
                uu = u_ref[rows, lanes]
                if has_a:
                    aa = a_ref[rows, lanes]
                    coef = jnp.where(row < rc - 1, pltpu.roll(aa, rc - 1, 0), afirst) if reverse else aa
                k = 1
                while k < rc:
                    shift = rc - k if reverse else k
                    keep = (row < rc - k) if reverse else (row >= k)
                    uu_sh = jnp.where(keep, pltpu.roll(uu, shift, 0), 0.0)
                    if has_a:
                        uu = coef * uu_sh + uu
                        coef = coef * jnp.where(keep, pltpu.roll(coef, shift, 0), 1.0)
                    else:
                        uu = uu + uu_sh
                    k *= 2
                h = uu + coef * carry if has_a else uu + carry
                outs[0][rows, lanes] = h
                if with_prev:
                    outs[1][rows, lanes] = jnp.where(row >= 1, pltpu.roll(h, 1, 0), carry)
                edge = pc * rc if reverse else (pc + 1) * rc - 1
                carry = outs[0][edge:edge + 1, lanes]
                if has_a and reverse:
                    afirst = a_ref[edge:edge + 1, lanes]
            carry_sc[:, lanes] = carry
            if has_a and reverse:
                afirst_sc[:, lanes] = afirst

    if reverse:
        imap = lambda b, s: (b * nb + nb - 1 - s, 0)
    else:
        imap = lambda b, s: (b * nb + s, 0)
    spec = pl.BlockSpec((tb, c), imap)
    n_in = 2 if has_a else 1
    n_out = 2 if with_prev else 1
    res = _pcall(body, ride=ride, name=name, grid=(bsz, nb), in_specs=[spec] * n_in, out_specs=[spec] * n_out,
                 out_shape=[jax.ShapeDtypeStruct(u.shape, F32)] * n_out,
                 scratch_shapes=[pltpu.VMEM((1, c), F32), pltpu.VMEM((1, c), F32)],
                 compiler_params=_cparams(("parallel", "arbitrary")))(*([a, u] if has_a else [u]))
    return res if with_prev else res[0]


def _conv_fwd(z, w, b, bsz, seq, tb=512):
    c = D_MODEL
    t = bsz * seq
    tb = _tile(seq, tb)
    nb = seq // tb

    def body(x_ref, w_ref, b_ref, o_ref, tail_sc):
        step = pl.program_id(1)

        @pl.when(step == 0)
        def _():
            tail_sc[...] = jnp.zeros_like(tail_sc)

        x = x_ref[...]
        row8 = lax.broadcasted_iota(jnp.int32, (8, c), 0)
        tail = tail_sc[...]
        acc = w_ref[CONV_W - 1:CONV_W, :] * x + b_ref[...]
        for sh in range(1, CONV_W):
            xs = pltpu.roll(x, sh, 0)
            top = jnp.where(row8 < sh, pltpu.roll(tail, sh, 0), xs[0:8, :])
            xs = jnp.concatenate([top, xs[8:, :]], axis=0) if tb > 8 else top
            acc = acc + w_ref[CONV_W - 1 - sh:CONV_W - sh, :] * xs
        o_ref[...] = acc
        tail_sc[...] = x_ref[tb - 8:tb, :]

    return _pcall(body, name="conv_fwd", grid=(bsz, nb),
                  in_specs=[pl.BlockSpec((tb, c), lambda bb, s: (bb * nb + s, OFF_RX // c)),
                            pl.BlockSpec((CONV_W, c), lambda bb, s: (0, 0)),
                            pl.BlockSpec((1, c), lambda bb, s: (0, 0))],
                  out_specs=pl.BlockSpec((tb, c), lambda bb, s: (bb * nb + s, 0)),
                  out_shape=jax.ShapeDtypeStruct((t, c), F32),
                  scratch_shapes=[pltpu.VMEM((8, c), F32)],
                  compiler_params=_cparams(("parallel", "arbitrary")))(z, w, b)


def _conv_bwd(z, dxc, w, bsz, seq, tb=512):
    c = D_MODEL
    t = bsz * seq
    tb = _tile(seq, tb)
    nb = seq // tb

    def body(x_ref, g_ref, w_ref, dx_ref, dw_ref, db_ref, head_sc):
        bb, step = pl.program_id(0), pl.program_id(1)

        @pl.when(step == 0)
        def _():
            head_sc[...] = jnp.zeros_like(head_sc)

        x, g = x_ref[...], g_ref[...]
        row8 = lax.broadcasted_iota(jnp.int32, (8, c), 0)
        head = head_sc[...]
        dx = w_ref[CONV_W - 1:CONV_W, :] * g
        dws = [None] * CONV_W
        dws[CONV_W - 1] = _colsum(g * x)
        for sh in range(1, CONV_W):
            gs = pltpu.roll(g, tb - sh, 0)
            bot = jnp.where(row8 >= 8 - sh, pltpu.roll(head, 8 - sh, 0), gs[tb - 8:tb, :])
            gs = jnp.concatenate([gs[:tb - 8, :], bot], axis=0) if tb > 8 else bot
            dx = dx + w_ref[CONV_W - 1 - sh:CONV_W - sh, :] * gs
            dws[CONV_W - 1 - sh] = _colsum(gs * x)
        dx_ref[...] = dx.astype(dx_ref.dtype)
        first = (bb == 0) & (step == 0)
        dw = jnp.concatenate(dws, axis=0)
        db = _colsum(g)

        @pl.when(first)
        def _():
            dw_ref[...] = dw
            db_ref[...] = db

        @pl.when(jnp.logical_not(first))
        def _():
            dw_ref[...] += dw
            db_ref[...] += db

        head_sc[...] = g_ref[0:8, :]

    rmap = lambda bb, s: (bb * nb + nb - 1 - s, 0)
    return _pcall(body, name="conv_bwd", grid=(bsz, nb),
                  in_specs=[pl.BlockSpec((tb, c), lambda bb, s: (bb * nb + nb - 1 - s, OFF_RX // c)),
                            pl.BlockSpec((tb, c), rmap),
                            pl.BlockSpec((CONV_W, c), lambda bb, s: (0, 0))],
                  out_specs=[pl.BlockSpec((tb, c), rmap),
                             pl.BlockSpec((CONV_W, c), lambda bb, s: (0, 0)),
                             pl.BlockSpec((1, c), lambda bb, s: (0, 0))],
                  out_shape=[jax.ShapeDtypeStruct((t, c), BF16), jax.ShapeDtypeStruct((CONV_W, c), F32),
                             jax.ShapeDtypeStruct((1, c), F32)],
                  scratch_shapes=[pltpu.VMEM((8, c), F32)],
                  compiler_params=_cparams(("arbitrary", "arbitrary")))(z, dxc, w)


def _gate_fwd(xc, w_a, w_x, b_a, b_x, lam, tm=1024, ride=None):
    t = xc.shape[0]
    tm = _tile(t, tm)

    def body(xc_ref, wa_ref, wx_ref, ba_ref, bx_ref, lam_ref, a_ref, u_ref):
        xc_b = xc_ref[...]
        xb = xc_b.astype(BF16)
        ra = lax.dot_general(xb, wa_ref[...].astype(BF16), (NN, ((), ())), preferred_element_type=F32)
        ia = lax.dot_general(xb, wx_ref[...].astype(BF16), (NN, ((), ())), preferred_element_type=F32)
        a, u = _f_gate(xc_b, ra, ia, lam_ref[...], ba_ref[...], bx_ref[...])
        a_ref[...] = a
        u_ref[...] = u

    row = pl.BlockSpec((tm, BLK), lambda n, i: (i, n))
    wsp = pl.BlockSpec((None, BLK, BLK), lambda n, i: (n, 0, 0))
    vec = pl.BlockSpec((1, BLK), lambda n, i: (0, n))
    return _pcall(body, ride=ride, name="gate_fwd", grid=(N_BLK, t // tm), in_specs=[row, wsp, wsp, vec, vec, vec],
                  out_specs=[row, row], out_shape=[jax.ShapeDtypeStruct((t, D_MODEL), F32)] * 2,
                  compiler_params=_cparams(("parallel", "parallel")))(xc, w_a, w_x, b_a, b_x, lam)


def _gate_bwd(xc, w_a, w_x, b_a, b_x, lam, hprev, du, tm=1024, ride=None):
    t = xc.shape[0]
    tm = _tile(t, tm)

    def body(xc_ref, wa_ref, wx_ref, ba_ref, bx_ref, lam_ref, hp_ref, du_ref,
             dxc_ref, dwa_ref, dwx_ref, dba_ref, dbx_ref, dlam_ref):
        step = pl.program_id(1)
        xc_b = xc_ref[...]
        xb = xc_b.astype(BF16)
        wa, wx = wa_ref[...].astype(BF16), wx_ref[...].astype(BF16)
        ra = lax.dot_general(xb, wa, (NN, ((), ())), preferred_element_type=F32)
        ia = lax.dot_general(xb, wx, (NN, ((), ())), preferred_element_type=F32)
        full = lambda r: jnp.broadcast_to(r[...], (tm, BLK))
        _, pull = jax.vjp(_f_gate, xc_b, ra, ia, full(lam_ref), full(ba_ref), full(bx_ref))
        du_b = du_ref[...]
        dxc, dra, dia, dlam, dba, dbx = pull((du_b * hp_ref[...], du_b))
        drb, dib = dra.astype(BF16), dia.astype(BF16)
        dxc = dxc + lax.dot_general(drb, wa, (NT, ((), ())), preferred_element_type=F32)
        dxc = dxc + lax.dot_general(dib, wx, (NT, ((), ())), preferred_element_type=F32)
        dxc_ref[...] = dxc
        _accumulate(dwa_ref, lax.dot_general(xb, drb, (TN, ((), ())), preferred_element_type=F32), step)
        _accumulate(dwx_ref, lax.dot_general(xb, dib, (TN, ((), ())), preferred_element_type=F32), step)
        _accumulate(dba_ref, _colsum(dba), step)
        _accumulate(dbx_ref, _colsum(dbx), step)
        _accumulate(dlam_ref, _colsum(dlam), step)

    row = pl.BlockSpec((tm, BLK), lambda n, i: (i, n))
    wsp = pl.BlockSpec((None, BLK, BLK), lambda n, i: (n, 0, 0))
    vec = pl.BlockSpec((1, BLK), lambda n, i: (0, n))
    wshape = jax.ShapeDtypeStruct((N_BLK, BLK, BLK), F32)
    vshape = jax.ShapeDtypeStruct((1, D_MODEL), F32)
    return _pcall(body, ride=ride, name="gate_bwd", grid=(N_BLK, t // tm),
                  in_specs=[row, wsp, wsp, vec, vec, vec, row, row],
                  out_specs=[row, wsp, wsp, vec, vec, vec],
                  out_shape=[jax.ShapeDtypeStruct((t, D_MODEL), F32), wshape, wshape, vshape, vshape, vshape],
                  compiler_params=_cparams(("parallel", "arbitrary")))(xc, w_a, w_x, b_a, b_x, lam, hprev, du)


def _ffn_in_act(a, w, tm=1024, ride=None):
    t = a.shape[0]
    tm = _tile(t, tm)

    def body(a_ref, wg_ref, wu_ref, hgu_ref, act_ref):
        ab = a_ref[...].astype(BF16)
        hg = lax.dot_general(ab, wg_ref[...].astype(BF16), (NN, ((), ())), preferred_element_type=F32)
        hu = lax.dot_general(ab, wu_ref[...].astype(BF16), (NN, ((), ())), preferred_element_type=F32)
        hgu_ref[0] = hg
        hgu_ref[1] = hu
        act_ref[...] = _f_act(hg, hu).astype(act_ref.dtype)

    wspec = lambda off: pl.BlockSpec((None, D_MODEL, FF_SH), lambda i, s: (s + off, 0, 0))
    hgu, act = _pcall(body, ride=ride, name="ffn_in", grid=(t // tm, N_FF),
                      in_specs=[pl.BlockSpec((tm, D_MODEL), lambda i, s: (i, 0)), wspec(0), wspec(N_FF)],
                      out_specs=[pl.BlockSpec((2, None, tm, FF_SH), lambda i, s: (0, s, i, 0)),
                                 pl.BlockSpec((None, tm, FF_SH), lambda i, s: (s, i, 0))],
                      out_shape=[jax.ShapeDtypeStruct((2, N_FF, t, FF_SH), F32),
                                 jax.ShapeDtypeStruct((N_FF, t, FF_SH), BF16)],
                      compiler_params=_cparams(("parallel", "parallel")))(a, w, w)
    return hgu.reshape(2 * N_FF, t, FF_SH), act


def _ffn_out_dx_act(d, w, hgu, tm=512, ride=None):
    t = d.shape[0]
    tm = _tile(t, tm)

    def body(d_ref, w_ref, hg_ref, hu_ref, o_ref):
        dact = lax.dot_general(d_ref[...].astype(BF16), w_ref[...].astype(BF16), (NT, ((), ())),
                               preferred_element_type=F32)
        _, pull = jax.vjp(_f_act, hg_ref[...], hu_ref[...])
        dhg, dhu = pull(dact)
        o_ref[0] = dhg.astype(o_ref.dtype)
        o_ref[1] = dhu.astype(o_ref.dtype)

    hspec = lambda off: pl.BlockSpec((None, tm, FF_SH), lambda i, s: (s + off, i, 0))
    res = _pcall(body, ride=ride, name="ffn_out_dx", grid=(t // tm, N_FF),
                 in_specs=[pl.BlockSpec((tm, D_MODEL), lambda i, s: (i, 0)),
                           pl.BlockSpec((None, FF_SH, D_MODEL), lambda i, s: (s, 0, 0)), hspec(0), hspec(N_FF)],
                 out_specs=pl.BlockSpec((2, None, tm, FF_SH), lambda i, s: (0, s, i, 0)),
                 out_shape=jax.ShapeDtypeStruct((2, N_FF, t, FF_SH), BF16),
                 compiler_params=_cparams(("parallel", "parallel")))(d, w, hgu, hgu)
    return res.reshape(2 * N_FF, t, FF_SH)


def _adamw(name, parts, w, m, v, tr=128, ride=None):
    ng = len(parts)
    n_src, r, c = parts[0].shape
    per = w.shape[1] // r
    assert w.shape[0] * per == ng and w.shape[2] == c
    tr = _tile(r, tr)
    nb = r // tr
    bc1 = 1.0 - ADAM_B1 ** ADAM_STEP
    bc2 = 1.0 - ADAM_B2 ** ADAM_STEP

    def body(*refs):
        p_refs = refs[:ng]
        w_ref, m_ref, v_ref, g_ref, d_ref, nm_ref, nv_ref = refs[ng:]
        grp = pl.program_id(0)

        def update(p_ref):
            g = p_ref[0].astype(F32)
            for s in range(1, n_src):
                g = g + p_ref[s].astype(F32)
            nm = ADAM_B1 * m_ref[...] + (1.0 - ADAM_B1) * g
            nv = ADAM_B2 * v_ref[...] + (1.0 - ADAM_B2) * jnp.square(g)
            g_ref[...] = g
            nm_ref[...] = nm
            nv_ref[...] = nv
            d_ref[...] = -ADAM_LR * ((nm / bc1) / (jnp.sqrt(nv / bc2) + ADAM_EPS) + ADAM_WD * w_ref[...])

        for k in range(ng):
            pl.when(grp == k)(functools.partial(update, p_refs[k]))

    p_specs = [pl.BlockSpec((n_src, tr, c), functools.partial(lambda gi, i, k: (0, jnp.where(gi == k, i, 0), 0), k=k))
               for k in range(ng)]
    spec = pl.BlockSpec((None, tr, c), lambda gi, i: (gi // per, (gi % per) * nb + i, 0))
    return _pcall(body, ride=ride, name=name, grid=(ng, nb), in_specs=p_specs + [spec, spec, spec],
                  out_specs=[spec] * 4, out_shape=[jax.ShapeDtypeStruct(w.shape, F32)] * 4,
                  compiler_params=_cparams(("parallel", "parallel")))(*parts, w, m, v)


def _sum_parts(name, parts, tr=256):
    _, r, c = parts.shape
    tr = _tile(r, tr)

    def body(p_ref, o_ref):
        g = p_ref[0]
        for s in range(1, parts.shape[0]):
            g = g + p_ref[s]
        o_ref[...] = g

    return _pcall(body, name=name, grid=(r // tr,),
                  in_specs=[pl.BlockSpec((parts.shape[0], tr, c), lambda i: (0, i, 0))],
                  out_specs=pl.BlockSpec((tr, c), lambda i: (i, 0)),
                  out_shape=jax.ShapeDtypeStruct((r, c), F32), compiler_params=_cparams(("parallel",)))(parts)


def _peer(k):
    x, y, c = lax.axis_index("x"), lax.axis_index("y"), lax.axis_index("c")
    return (x ^ ((k >> 2) & 1), y ^ ((k >> 1) & 1), c ^ (k & 1))


def _my_id():
    return 4 * lax.axis_index("x") + 2 * lax.axis_index("y") + lax.axis_index("c")


def _exchange(name, ride):
    n = len(ride.arrays)

    def body(*refs):
        ride.begin(refs[:n], refs[n:2 * n], refs[2 * n:])
        ride.finish(refs[:n], refs[n:2 * n], refs[2 * n:])

    hbm = pl.BlockSpec(memory_space=pltpu.HBM)
    return _pcall(body, name=name, in_specs=[hbm] * n, out_specs=[hbm] * n, out_shape=ride.out_shapes(),
                  scratch_shapes=ride.scratch())(*ride.arrays)


def _row(v):
    return v.reshape(1, -1)


def _time_major_heads(c, bsz, seq):
    return c.reshape(bsz, seq, BLK)[:, :, :N_HEADS].transpose(0, 2, 1)


def _no_ride(*_):
    return None


TWICE = [(D_MODEL, F32), (D_MODEL, BF16)]


def _both(fn):
    def run(*v):
        y = fn(*v)
        return y, y
    return run


def _layer_fwd(h, hb, p_l, w, bsz, seq, ride_of=_no_ride):
    t = bsz * seq
    zq = _mm_nn("z_proj_qkv", hb, w['w_in7'], n=QKV, out_dtype=BF16, ride=ride_of('z_proj_qkv'))
    zr = _mm_nn("z_proj_rest", hb, w['w_in7'], b_off=QKV, n=4 * D_MODEL, ride=ride_of('z_proj_rest'))
    fl = _mm_nn("f_proj", hb, w['w_inf'])
    logf, = _rowwise("logf_fwd", lambda f, b: (_f_logf(f, b),), [fl], [w['b_forget']], [(BLK, F32)], [])
    c = _scan("cumsum_fwd", None, logf, bsz, seq, reverse=False)
    ct = _time_major_heads(c, bsz, seq)
    cq, ck = ct[..., None], ct[:, :, None, :]
    att, attb, lse = _attn_fwd(zq, cq, ck, bsz, seq, ride=ride_of('attn_fwd'))
    xc = _conv_fwd(zr, w['conv_w'], w['conv_b'], bsz, seq)
    decay, = _rowwise("decay_fwd", lambda lam: (_f_decay(lam),), [w['rg_lambda']], [], [(D_MODEL, F32)], [])
    a, u = _gate_fwd(xc, w['rg_w_a'], w['rg_w_x'], w['rg_b_a'], w['rg_b_x'], decay, ride=ride_of('gate_fwd'))
    hs, hprev = _scan("lru_fwd", a, u, bsz, seq, reverse=False, with_prev=True, ride=ride_of('lru_fwd'))
    rnn, = _rowwise("rnn_out_fwd", lambda s, y: (_f_rnn_out(s, y),), [hs, (zr, OFF_RY, D_MODEL)], [],
                    [(D_MODEL, BF16)], [])
    ya = _mm_nn("branch_att", attb, w['w_branch_att'])
    yb = _mm_nn("branch_rnn", rnn, w['w_branch_rnn'])
    merged, = _rowwise("merge_fwd", lambda *v: (_f_merge(*v),),
                       [(zr, OFF_GA, D_MODEL), (zr, OFF_GB, D_MODEL), ya, yb], [w['b_merge0'], w['b_merge1']],
                       [(D_MODEL, BF16)], [])
    mix = _mm_nn("mix_out", merged, w['w_out'])
    h1, h1b = _rowwise("ln_mix_fwd", _both(_f_resid_ln), [h, mix], [w['ln_mix_g'], w['ln_mix_b']], TWICE, [])
    tm = _tile(t, 1024)
    hgu, act = _ffn_in_act(h1b, w['w_ffn_in'], ride=ride_of('ffn_in'))
    ffn = _mm("ffn_out", act, w['w_ffn_out'], grid=(t // tm, 1, N_FF),
              a_spec=pl.BlockSpec((None, tm, FF_SH), lambda i, j, s: (s, i, 0)),
              b_spec=pl.BlockSpec((None, FF_SH, D_MODEL), lambda i, j, s: (s, 0, 0)),
              o_spec=pl.BlockSpec((tm, D_MODEL), lambda i, j, s: (i, 0)),
              out_shape=jax.ShapeDtypeStruct((t, D_MODEL), F32), contract=NN, ride=ride_of('ffn_out'))
    h2, h2b = _rowwise("ln_ffn_fwd", _both(_f_resid_ln), [h1, ffn], [w['ln_ffn_g'], w['ln_ffn_b']], TWICE, [])
    gp = _mm_nn("ple_gate", h2b, w['w_ple_gate'])
    pe = _mm_nn("ple_proj", p_l, w['w_ple'])
    h3, h3b = _rowwise("ln_ple_fwd", _both(_f_ple), [h2, gp, pe],
                       [w['b_ple_gate'], w['ln_ple_g'], w['ln_ple_b']], TWICE, [])
    saved = dict(h=h, hb=hb, zq=zq, zr=zr, fl=fl, cq=cq, ck=ck, att=att, attb=attb, lse=lse, xc=xc, a=a, decay=decay,
                 hprev=hprev, hs=hs, rnn=rnn, ya=ya, yb=yb, merged=merged, mix=mix, h1=h1, h1b=h1b, hgu=hgu,
                 act=act, ffn=ffn, h2=h2, h2b=h2b, gp=gp, pe=pe)
    return h3, h3b, saved


def _layer_bwd(dh3, p_l, w, s, bsz, seq, ride_of=_no_ride):
    t = bsz * seq
    g = {}
    dh2, dgp, dpe, g['b_ple_gate'], g['ln_ple_g'], g['ln_ple_b'] = _vjp_rowwise(
        "ln_ple_bwd", _f_ple, [s['h2'], s['gp'], s['pe']], [w['b_ple_gate'], w['ln_ple_g'], w['ln_ple_b']], [dh3], 3,
        dtypes=[F32, BF16, BF16])
    g['w_ple_gate'] = _mm_tn("ple_gate_dw", s['h2b'], dgp, out_dtype=BF16)
    g['w_ple'] = _mm_tn("ple_proj_dw", p_l, dpe, out_dtype=BF16)
    dh2b = _mm_nt("ple_gate_dx", dgp, w['w_ple_gate'])
    dh1, dffn, g['ln_ffn_g'], g['ln_ffn_b'] = _ln_resid_bwd(
        "ln_ffn_bwd", s['h1'], s['ffn'], w['ln_ffn_g'], w['ln_ffn_b'], dh2, dh2b)
    tm = _tile(t, 1024)
    tk = _tile(t, 2048)
    g['w_ffn_out'] = _mm("ffn_out_dw", s['act'], dffn, grid=(N_FF, 1, t // tk),
                         a_spec=pl.BlockSpec((None, tk, FF_SH), lambda ss, j, k: (ss, k, 0)),
                         b_spec=pl.BlockSpec((tk, D_MODEL), lambda ss, j, k: (k, 0)),
                         o_spec=pl.BlockSpec((None, FF_SH, D_MODEL), lambda ss, j, k: (ss, 0, 0)),
                         out_shape=jax.ShapeDtypeStruct((N_FF, FF_SH, D_MODEL), BF16), contract=TN)
    dhgu = _ffn_out_dx_act(dffn, w['w_ffn_out'], s['hgu'], ride=ride_of('ffn_out_dx', g))
    g['w_ffn_in'] = _mm("ffn_in_dw", s['h1b'], dhgu, grid=(2 * N_FF, 1, t // tk),
                        a_spec=pl.BlockSpec((tk, D_MODEL), lambda ss, j, k: (k, 0)),
                        b_spec=pl.BlockSpec((None, tk, FF_SH), lambda ss, j, k: (ss, k, 0)),
                        o_spec=pl.BlockSpec((None, D_MODEL, FF_SH), lambda ss, j, k: (ss, 0, 0)),
                        out_shape=jax.ShapeDtypeStruct((2 * N_FF, D_MODEL, FF_SH), BF16), contract=TN,
                        ride=ride_of('ffn_in_dw', g))
    dh1b = _mm("ffn_in_dx", dhgu, w['w_ffn_in'], grid=(t // tm, 1, 2 * N_FF),
               a_spec=pl.BlockSpec((None, tm, FF_SH), lambda i, j, ss: (ss, i, 0)),
               b_spec=pl.BlockSpec((None, D_MODEL, FF_SH), lambda i, j, ss: (ss, 0, 0)),
               o_spec=pl.BlockSpec((tm, D_MODEL), lambda i, j, ss: (i, 0)),
               out_shape=jax.ShapeDtypeStruct((t, D_MODEL), F32), contract=NT, ride=ride_of('ffn_in_dx', g))
    dh, dmix, g['ln_mix_g'], g['ln_mix_b'] = _ln_resid_bwd(
        "ln_mix_bwd", s['h'], s['mix'], w['ln_mix_g'], w['ln_mix_b'], dh1, dh1b)
    g['w_out'] = _mm_tn("mix_out_dw", s['merged'], dmix, out_dtype=BF16)
    dmerged = _mm_nt("mix_out_dx", dmix, w['w_out'])
    z = s['zr']
    dga, dgb, dya, dyb, dbm0, dbm1 = _vjp_rowwise(
        "merge_bwd", _f_merge, [(z, OFF_GA, D_MODEL), (z, OFF_GB, D_MODEL), s['ya'], s['yb']],
        [w['b_merge0'], w['b_merge1']], [dmerged], 4, dtypes=[BF16] * 4)
    g['b_merge'] = jnp.concatenate([dbm0, dbm1], axis=0)
    g['w_branch_att'] = _mm_tn("branch_att_dw", s['attb'], dya, out_dtype=BF16)
    g['w_branch_rnn'] = _mm_tn("branch_rnn_dw", s['rnn'], dyb, out_dtype=BF16)
    datt = _mm_nt("branch_att_dx", dya, w['w_branch_att'], out_dtype=BF16)
    drnn = _mm_nt("branch_rnn_dx", dyb, w['w_branch_rnn'])
    dhs, dry = _vjp_rowwise("rnn_out_bwd", _f_rnn_out, [s['hs'], (z, OFF_RY, D_MODEL)], [], [drnn], 2,
                            dtypes=[F32, BF16])
    lam = _scan("lru_bwd", s['a'], dhs, bsz, seq, reverse=True)
    dxc, g['rg_w_a'], g['rg_w_x'], g['rg_b_a'], g['rg_b_x'], ddecay = _gate_bwd(
        s['xc'], w['rg_w_a'], w['rg_w_x'], w['rg_b_a'], w['rg_b_x'], s['decay'], s['hprev'], lam,
        ride=ride_of('gate_bwd', g))
    g['rg_lambda'], = _vjp_rowwise("decay_bwd", _f_decay, [w['rg_lambda']], [], [ddecay], 1)
    drx, g['conv_w'], g['conv_b'] = _conv_bwd(z, dxc, w['conv_w'], bsz, seq)
    dq, dk, dv, dcq, dck = _attn_bwd(s['zq'], s['att'], datt, s['lse'], s['cq'], s['ck'], bsz, seq,
                                     ride=ride_of('attn_bwd', g))
    dc = (dcq[:, :, :, 0] + dck.reshape(bsz, N_HEADS, seq)).transpose(0, 2, 1)
    dc = jnp.pad(dc, ((0, 0), (0, 0), (0, BLK - N_HEADS))).reshape(t, BLK)
    dlogf = _scan("cumsum_bwd", None, dc, bsz, seq, reverse=True)
    dfl, g['b_forget'] = _vjp_rowwise("logf_bwd", _f_logf, [s['fl']], [w['b_forget']], [dlogf], 1, dtypes=[BF16])
    dz = jnp.concatenate([dq, dk, dv, drx, dry, dga, dgb], axis=1)
    g['w_in7'] = _mm_tn("z_proj_dw", s['hb'], dz, out_dtype=BF16)
    g['w_inf'] = _mm_tn("f_proj_dw", s['hb'], dfl, out_dtype=BF16)
    dh = _mm_nt("z_proj_dx", dz, w['w_in7'], ride=ride_of('z_proj_dx', g), add=dh)
    dh = _mm_nt("f_proj_dx", dfl, w['w_inf'], add=dh)
    return dh, g


def _ln_resid_bwd(name, h, branch, gam, bet, d0, d1):
    def bwd(hv, bv, d0v, d1v, gv, btv):
        _, pull = jax.vjp(_f_resid_ln, hv, bv, gv, btv)
        dh, db, dg, dbt = pull(d0v + d1v)
        return dh, db, _colsum(dg), _colsum(dbt)

    return _rowwise(name, bwd, [h, branch, d0, d1], [gam, bet], [(D_MODEL, F32), (D_MODEL, BF16)],
                    [D_MODEL, D_MODEL], tm=256)


class _Schedule:
    FWD = {'z_proj_qkv': ['w_ffn_out'], 'z_proj_rest': ['w_branch_att', 'w_branch_rnn', 'w_out', 'w_ple_gate'],
           'attn_fwd': ['w_in'], 'ffn_in': ['w_ffn_in'], 'ffn_out': ['w_ple', 'conv_w', 'b_merge']}
    FIRST = ['w_in', 'conv_w', 'b_merge']
    OWN = {'z_proj_qkv': ['w_branch_att', 'w_branch_rnn', 'w_out'], 'z_proj_rest': ['w_ffn_in'],
           'attn_fwd': ['w_ffn_out', 'w_ple_gate', 'w_ple']}
    NEXT = {'attn_fwd': ['w_in'], 'gate_fwd': ['w_ffn_out'],
            'lru_fwd': ['w_branch_att', 'w_branch_rnn', 'w_out', 'w_ple_gate'],
            'ffn_in': ['w_ffn_in'], 'ffn_out': ['w_ple', 'conv_w', 'b_merge']}
    BWD = {'ffn_out_dx': ['w_ffn_out'], 'ffn_in_dw': ['w_ple_gate', 'w_ple'],
           'gate_bwd': ['w_out', 'w_branch_att', 'w_branch_rnn'],
           'attn_bwd': ['w_ffn_in', 'conv_w', 'b_merge']}

    def __init__(self, shards, depth):
        self.shards, self.depth = shards, depth
        self.gathered = [{} for _ in range(depth)]
        self.received = [{} for _ in range(depth)]
        self.pending = []
        self.deferred = None

    def gather_first(self):
        ride = _Ride([self.shards[n] for n in self.FIRST], gather=True, index=[0] * len(self.FIRST))
        self.gathered[0].update(zip(self.FIRST, _exchange("gather_first", ride)))

    def gather_ride(self, layer, kernel_name):
        own = self.OWN.get(kernel_name, []) if layer == 0 else []
        nxt = (self.NEXT if layer == 0 else self.FWD).get(kernel_name, []) if layer + 1 < self.depth else []
        items = [(n, 0) for n in own] + [(n, layer + 1) for n in nxt]
        if not items:
            return None
        ride = _Ride([self.shards[n] for n, _ in items], gather=True, index=[l for _, l in items])
        self.pending.append((ride, [(n, self.gathered[l]) for n, l in items]))
        return ride

    def _scatter(self, arrays, names, layer):
        ride = _Ride(arrays, gather=False)
        self.pending.append((ride, [(n, self.received[layer]) for n in names]))
        return ride

    def scatter_ride(self, layer, kernel_name, grads):
        if kernel_name == 'z_proj_dx':
            whole = _by_destination('w_in', grads)
            half = whole.shape[1] // 2
            self.deferred = (whole[:, half:], layer)
            return self._scatter([whole[:, :half]], ['w_in_a'], layer)
        if kernel_name == 'ffn_in_dx':
            if self.deferred is None:
                return None
            (late, from_layer), self.deferred = self.deferred, None
            return self._scatter([late], ['w_in_b'], from_layer)
        names = self.BWD[kernel_name]
        return self._scatter([_by_destination(n, grads) for n in names], names, layer)

    def collect(self):
        for ride, places in self.pending:
            if ride.result is not None:
                for (name, dst), res in zip(places, ride.result):
                    dst[name] = res
        self.pending = [(ride, places) for ride, places in self.pending if ride.result is None]


class _LayerWeights:
    SOURCE = {'w_in7': 'w_in', 'w_inf': 'w_in', 'b_merge0': 'b_merge', 'b_merge1': 'b_merge'}

    def __init__(self, sched, layer, replicated):
        self.sched, self.layer, self.made = sched, layer, dict(replicated)

    def __getitem__(self, key):
        if key not in self.made:
            self.sched.collect()
            src = self.SOURCE.get(key, key)
            self.made.update(_from_shards(src, self.sched.gathered[self.layer][src]))
        return self.made[key]


def _local_step(x2, tgt, p3, weights_of, depth, g_in, b_in, bsz, seq, sched=None):
    h, hb = _rowwise("ln_in_fwd", _both(_ln), [x2], [g_in, b_in], TWICE, [])
    p3 = p3.astype(BF16)
    saved, layer_w = [], []
    for l in range(depth):
        layer_w.append(weights_of(l))
        ride_of = functools.partial(sched.gather_ride, l) if sched else _no_ride
        h, hb, s = _layer_fwd(h, hb, p3[l], layer_w[l], bsz, seq, ride_of)
        if sched:
            sched.collect()
        saved.append(s)

    def loss_fn(y, tv):
        err = y - tv
        return err * (1.0 / D_MODEL), _colsum(jnp.square(err))

    dh, sq = _rowwise("loss", loss_fn, [h, tgt], [], [(D_MODEL, F32)], [D_MODEL])
    grads = [None] * depth
    for l in reversed(range(depth)):
        ride_of = functools.partial(sched.scatter_ride, l) if sched else _no_ride
        dh, grads[l] = _layer_bwd(dh, p3[l], layer_w[l], saved[l], bsz, seq, ride_of)
        if sched:
            sched.collect()
    dx, dg_in, db_in = _vjp_rowwise("ln_in_bwd", _ln, [x2], [g_in, b_in], [dh], 1)
    return sq, dx, grads, dg_in, db_in


def _from_shards(name, g):
    if name == 'w_in':
        wt = g.transpose(1, 0, 2).reshape(D_MODEL, N_IN)
        return {'w_in7': jnp.concatenate([wt[:, :3 * D_MODEL], wt[:, 3 * D_MODEL + N_HEADS:]], axis=1),
                'w_inf': jnp.pad(wt[:, 3 * D_MODEL:3 * D_MODEL + N_HEADS], ((0, 0), (0, BLK - N_HEADS)))}
    if name in ('w_branch_att', 'w_branch_rnn', 'w_out', 'w_ple_gate'):
        return {name: g.reshape(D_MODEL, D_MODEL)}
    if name == 'w_ffn_in':
        return {name: g}
    if name == 'w_ffn_out':
        return {name: g.reshape(N_FF, FF_SH, D_MODEL)}
    if name == 'b_merge':
        bm = g.transpose(1, 0, 2).reshape(2, D_MODEL)
        return {'b_merge0': bm[0:1], 'b_merge1': bm[1:2]}
    return {name: g.transpose(1, 0, 2).reshape(g.shape[1], D_MODEL)}


def _layer_weights(full):
    w = {}
    for name, g in full.items():
        w.update(_from_shards(name, g))
    return w


def _by_destination(name, gw):
    if name == 'w_in':
        g7, gf = gw['w_in7'], gw['w_inf']
        true = jnp.concatenate([g7[:, :3 * D_MODEL], gf[:, :N_HEADS], g7[:, 3 * D_MODEL:]], axis=1)
        return true.reshape(D_MODEL, N_DEV, IN_SH).transpose(1, 0, 2)
    g = gw[name]
    if name in ('w_branch_att', 'w_branch_rnn', 'w_out', 'w_ple_gate'):
        return g.reshape(N_DEV, D_MODEL // N_DEV, D_MODEL)
    if name == 'w_ffn_in':
        return g
    if name == 'w_ffn_out':
        return g.reshape(N_DEV, N_FF * FF_SH // N_DEV, D_MODEL)
    return g.reshape(g.shape[0], N_DEV, BLK).transpose(1, 0, 2)


def kernel(x, p, ln_in_g, ln_in_b, w_in, b_forget, conv_w, conv_b, rg_w_a, rg_b_a, rg_w_x, rg_b_x, rg_lambda, w_branch_att, w_branch_rnn, b_merge, w_out, ln_mix_g, ln_mix_b, w_ffn_in, w_ffn_out, ln_ffn_g, ln_ffn_b, w_ple, w_ple_gate, b_ple_gate, ln_ple_g, ln_ple_b, loss_target, m_ln_in_g, m_ln_in_b, m_w_in, m_b_forget, m_conv_w, m_conv_b, m_rg_w_a, m_rg_b_a, m_rg_w_x, m_rg_b_x, m_rg_lambda, m_w_branch_att, m_w_branch_rnn, m_b_merge, m_w_out, m_ln_mix_g, m_ln_mix_b, m_w_ffn_in, m_w_ffn_out, m_ln_ffn_g, m_ln_ffn_b, m_w_ple, m_w_ple_gate, m_b_ple_gate, m_ln_ple_g, m_ln_ple_b, v_ln_in_g, v_ln_in_b, v_w_in, v_b_forget, v_conv_w, v_conv_b, v_rg_w_a, v_rg_b_a, v_rg_w_x, v_rg_b_x, v_rg_lambda, v_w_branch_att, v_w_branch_rnn, v_b_merge, v_w_out, v_ln_mix_g, v_ln_mix_b, v_w_ffn_in, v_w_ffn_out, v_ln_ffn_g, v_ln_ffn_b, v_w_ple, v_w_ple_gate, v_b_ple_gate, v_ln_ple_g, v_ln_ple_b):
    env = dict(locals())
    wts = {n: env[n] for n in WEIGHTS}
    mom = {n: env['m_' + n] for n in WEIGHTS}
    var = {n: env['v_' + n] for n in WEIGHTS}
    bsz, seq, _ = x.shape
    depth = w_in.shape[0]
    t = bsz * seq
    x2, tgt = x.reshape(t, D_MODEL), loss_target.reshape(t, D_MODEL)
    p3 = p.reshape(depth, t, D_PLE)

    shard_names = SHARDED_BF16 + SHARDED_F32
    shards = {n: wts[n].astype(BF16) for n in SHARDED_BF16}
    shards.update({n: wts[n] for n in SHARDED_F32})
    sched = _Schedule(shards, depth)
    sched.gather_first()

    def weights_of(l):
        w = {n: _row(wts[n][l]) for n in ['conv_b', 'rg_b_a', 'rg_b_x', 'rg_lambda', 'ln_mix_g', 'ln_mix_b',
                                          'ln_ffn_g', 'ln_ffn_b', 'b_ple_gate', 'ln_ple_g', 'ln_ple_b']}
        w['b_forget'] = jnp.pad(_row(b_forget[l]), ((0, 0), (0, BLK - N_HEADS)))
        w['rg_w_a'], w['rg_w_x'] = rg_w_a[l], rg_w_x[l]
        return _LayerWeights(sched, l, w)

    g_in, b_in = _row(ln_in_g), _row(ln_in_b)
    sq, dx, grads, dg_in, db_in = _local_step(x2, tgt, p3, weights_of, depth, g_in, b_in, bsz, seq, sched)
    loss = lax.psum(0.5 * jnp.sum(sq) / D_MODEL, ("x", "y", "c"))
    grad_x = dx.reshape(bsz, seq, D_MODEL)

    out = {}

    def update(n, ride=None):
        shp = wts[n].shape
        view = lambda a: a
        if n == 'w_in':
            recv = [sched.received[l][half] for l in range(depth) for half in ('w_in_a', 'w_in_b')]
        else:
            recv = [sched.received[l][n] for l in range(depth)]
        if n in SHARDED_F32:
            recv = [jnp.stack(recv, axis=1).reshape(N_DEV, -1, shp[-1])]
            view = lambda a: a.reshape(1, -1, shp[-1])
        res = _adamw("adamw_" + n, recv, view(wts[n]), view(mom[n]), view(var[n]), ride=ride)
        out[n] = [r.reshape(shp) for r in res]

    def rep_grad(n):
        if n == 'ln_in_g':
            return dg_in.reshape(-1)
        if n == 'ln_in_b':
            return db_in.reshape(-1)
        return jnp.stack([grads[l][n].reshape(wts[n].shape[1:]) if n != 'b_forget'
                          else grads[l][n][0, :N_HEADS] for l in range(depth)]).reshape(-1)

    sizes = [int(wts[n].size) for n in REPLICATED]
    n_rows = [8 * (-(-sz // (8 * BLK))) for sz in sizes]
    total_rows = -(-sum(n_rows) // (N_DEV * 8)) * (N_DEV * 8)

    def as_rows(v, sz, nr):
        v = v.reshape(-1)
        return (jnp.pad(v, (0, nr * BLK - sz)) if nr * BLK != sz else v).reshape(nr, BLK)

    def pack(vals):
        parts = [as_rows(v, sz, nr) for v, sz, nr in zip(vals, sizes, n_rows)]
        parts.append(jnp.zeros((total_rows - sum(n_rows), BLK), F32))
        return jnp.concatenate(parts, axis=0)

    late, from_layer = sched.deferred
    last_w_in = _Ride([late], gather=False)
    update('w_ffn_in', ride=last_w_in)
    sched.received[from_layer]['w_in_b'], = last_w_in.result
    scatter_small = _Ride([pack([rep_grad(n) for n in REPLICATED]).reshape(N_DEV, total_rows // N_DEV, BLK)],
                          gather=False)
    update('w_branch_att', ride=scatter_small)
    gather_small = _Ride([_sum_parts("sum_small", scatter_small.result[0])], gather=True)
    update('w_out', ride=gather_small)
    for n in shard_names:
        if n not in out:
            update(n)
    g_rows = gather_small.result[0].reshape(total_rows, BLK)
    starts = [sum(n_rows[:i]) for i in range(len(n_rows))]
    for n, r0, sz, nr in zip(REPLICATED, starts, sizes, n_rows):
        shp = wts[n].shape
        as_one = (1, 1, sz) if len(shp) == 1 else (1, -1, shp[-1])
        g_n = g_rows[r0:r0 + nr]
        g_n = (g_n if nr * BLK == sz else g_n.reshape(-1)[:sz]).reshape(as_one)
        res = _adamw("adamw_" + n, [g_n], *[d[n].reshape(as_one) for d in (wts, mom, var)])
        out[n] = [r.reshape(shp) for r in res]

    return (loss, grad_x, *[out[n][k] for k in range(4) for n in WEIGHTS])
```

```python
import functools
import math

import jax
import jax.numpy as jnp
from jax import lax
from jax.experimental import pallas as pl
from jax.experimental.pallas import tpu as pltpu

F32 = jnp.float32
BF16 = jnp.bfloat16

N_DEV = 8
D_MODEL = 1024
N_HEADS = 8
HEAD_DIM = 128
N_BLK = 8
BLK = 128
CONV_W = 4
D_PLE = 256
FF_SH = 704
N_FF = 4
IN_SH = 897
N_IN = 7176
DEPTH = 4
RG_C = 8.0
ALPHA = float((2 * DEPTH) ** 0.25)
LN_EPS = 1e-5
SCALE = 1.0 / math.sqrt(HEAD_DIM)
NEG = -1e30
ADAM_LR, ADAM_B1, ADAM_B2, ADAM_EPS, ADAM_WD, ADAM_STEP = 0.001, 0.9, 0.999, 1e-08, 0.01, 10
QKV = 3 * D_MODEL
OFF_RX, OFF_RY, OFF_GA, OFF_GB = (i * D_MODEL for i in range(4))
V7X_VMEM_LIMIT = 48 * 1024 * 1024

WEIGHTS = ['ln_in_g', 'ln_in_b', 'w_in', 'b_forget', 'conv_w', 'conv_b', 'rg_w_a', 'rg_b_a', 'rg_w_x', 'rg_b_x',
           'rg_lambda', 'w_branch_att', 'w_branch_rnn', 'b_merge', 'w_out', 'ln_mix_g', 'ln_mix_b', 'w_ffn_in',
           'w_ffn_out', 'ln_ffn_g', 'ln_ffn_b', 'w_ple', 'w_ple_gate', 'b_ple_gate', 'ln_ple_g', 'ln_ple_b']
SHARDED_BF16 = ['w_in', 'w_branch_att', 'w_branch_rnn', 'w_out', 'w_ffn_in', 'w_ffn_out', 'w_ple', 'w_ple_gate']
SHARDED_F32 = ['conv_w', 'b_merge']
REPLICATED = [n for n in WEIGHTS if n not in SHARDED_BF16 and n not in SHARDED_F32]

NN = ((1,), (0,))
NT = ((1,), (1,))
TN = ((0,), (0,))


class _Ride:
    def __init__(self, arrays, *, gather, index=None):
        self.arrays, self.gather, self.index = list(arrays), gather, index
        self.result = None

    def _shard(self, ins, a):
        return ins[a] if self.index is None else ins[a].at[self.index[a]]

    def out_shapes(self):
        if not self.gather:
            return [jax.ShapeDtypeStruct(a.shape, a.dtype) for a in self.arrays]
        cut = 0 if self.index is None else 1
        return [jax.ShapeDtypeStruct((N_DEV,) + a.shape[cut:], a.dtype) for a in self.arrays]

    def scratch(self):
        n = len(self.arrays)
        return [pltpu.SemaphoreType.DMA((n * N_DEV,)), pltpu.SemaphoreType.DMA((n * N_DEV,)),
                pltpu.SemaphoreType.DMA((n,))]

    def _copy(self, a, k, src, dst, sems, to=None):
        send_sems, recv_sems, _ = sems
        return pltpu.make_async_remote_copy(
            src_ref=src, dst_ref=dst, send_sem=send_sems.at[a * N_DEV + k], recv_sem=recv_sems.at[a * N_DEV + k],
            device_id=_peer(k if to is None else to), device_id_type=pl.DeviceIdType.MESH)

    def begin(self, ins, outs, sems):
        me = _my_id()
        started = []
        for a in range(len(ins)):
            if self.gather:
                src = self._shard(ins, a)
                started.append(pltpu.make_async_copy(src, outs[a].at[me], sems[2].at[a]))
                started += [self._copy(a, k, src, outs[a].at[me], sems) for k in (1, 2, 4, 6)]
            else:
                started.append(pltpu.make_async_copy(ins[a].at[me], outs[a].at[me], sems[2].at[a]))
                started += [self._copy(a, k, ins[a].at[me ^ k], outs[a].at[me], sems) for k in range(1, N_DEV)]
        for cp in started:
            cp.start()

    def finish(self, ins, outs, sems):
        me = _my_id()
        for a in range(len(ins)):
            if self.gather:
                src = self._shard(ins, a)
                passed = []
                for k in (2, 4, 6):
                    block = outs[a].at[me ^ k]
                    self._copy(a, k, src, block, sems).wait_recv()
                    passed.append(self._copy(a, k + 1, block, block, sems, to=1))
                    passed[-1].start()
                for k in (1, 2, 4, 6):
                    self._copy(a, k, src, outs[a].at[me], sems).wait_send()
                self._copy(a, 1, src, outs[a].at[me ^ 1], sems).wait_recv()
                for cp in passed:
                    cp.wait()
                pltpu.make_async_copy(src, outs[a].at[me], sems[2].at[a]).wait()
            else:
                pltpu.make_async_copy(ins[a].at[me], outs[a].at[me], sems[2].at[a]).wait()
                for k in range(1, N_DEV):
                    self._copy(a, k, ins[a].at[me ^ k], outs[a].at[me], sems).wait()


def _pcall(body, ride=None, **kw):
    if ride is None:
        return pl.pallas_call(body, **kw)
    n = len(ride.arrays)
    grid = kw['grid']
    single = not isinstance(kw['out_shape'], (list, tuple))
    out_specs = [kw['out_specs']] if single else list(kw['out_specs'])
    out_shape = [kw['out_shape']] if single else list(kw['out_shape'])
    in_specs = list(kw['in_specs'])
    scratch = list(kw.get('scratch_shapes', ()))
    n_in, n_out, n_sc = len(in_specs), len(out_shape), len(scratch)
    hbm = pl.BlockSpec(memory_space=pltpu.HBM)

    def wrapped(*refs):
        ins, xin = refs[:n_in], refs[n_in:n_in + n]
        outs, xout = refs[n_in + n:n_in + n + n_out], refs[n_in + n + n_out:n_in + 2 * n + n_out]
        sc, sems = refs[n_in + 2 * n + n_out:n_in + 2 * n + n_out + n_sc], refs[-3:]
        ids = [pl.program_id(ax) for ax in range(len(grid))]
        first = functools.reduce(jnp.logical_and, [i == 0 for i in ids])
        last = functools.reduce(jnp.logical_and, [i == g - 1 for i, g in zip(ids, grid)])

        pl.when(first)(lambda: ride.begin(xin, xout, sems))
        body(*ins, *outs, *sc)
        pl.when(last)(lambda: ride.finish(xin, xout, sems))

    call = pl.pallas_call(wrapped, name=kw['name'], grid=grid, in_specs=in_specs + [hbm] * n,
                          out_specs=out_specs + [hbm] * n, out_shape=out_shape + ride.out_shapes(),
                          scratch_shapes=scratch + ride.scratch(), compiler_params=kw['compiler_params'])

    def run(*args):
        res = call(*args, *ride.arrays)
        ride.result = list(res[n_out:])
        return res[0] if single else list(res[:n_out])

    return run


def _tile(n, pref, mult=8):
    if n <= pref:
        return n
    t = (pref // mult) * mult
    while t >= mult:
        if n % t == 0:
            return t
        t -= mult
    return n


def _cparams(sem):
    return pltpu.CompilerParams(dimension_semantics=sem, vmem_limit_bytes=V7X_VMEM_LIMIT)


def _mm(name, a, b, *, grid, a_spec, b_spec, o_spec, out_shape, contract, ride=None, add=None):
    nk = grid[-1]
    in_out = out_shape.dtype == F32
    acc_shape = tuple(d for d in o_spec.block_shape if d is not None)

    def body(*refs):
        a_ref, b_ref = refs[0], refs[1]
        add_ref = refs[2] if add is not None else None
        o_ref = refs[3] if add is not None else refs[2]
        acc_ref = o_ref if (in_out or nk == 1) else refs[-1]
        k = pl.program_id(len(grid) - 1)
        part = lax.dot_general(a_ref[...].astype(BF16), b_ref[...].astype(BF16), (contract, ((), ())),
                               preferred_element_type=F32)
        if add_ref is not None:
            part = jnp.where(k == 0, part + add_ref[...], part) if nk > 1 else part + add_ref[...]
        if nk == 1:
            o_ref[...] = part.astype(o_ref.dtype)
            return

        @pl.when(k == 0)
        def _():
            acc_ref[...] = part

        @pl.when(k > 0)
        def _():
            acc_ref[...] += part

        if not in_out:
            @pl.when(k == nk - 1)
            def _():
                o_ref[...] = acc_ref[...].astype(o_ref.dtype)

    sem = ("parallel",) * (len(grid) - 1) + ("arbitrary",)
    scratch = [] if (in_out or nk == 1) else [pltpu.VMEM(acc_shape, F32)]
    in_specs, args = [a_spec, b_spec], [a, b]
    if add is not None:
        in_specs.append(o_spec)
        args.append(add)
    return _pcall(body, ride=ride, name=name, grid=grid, in_specs=in_specs, out_specs=o_spec,
                  out_shape=out_shape, scratch_shapes=scratch, compiler_params=_cparams(sem))(*args)


def _mm_nn(name, a, b, *, b_off=0, n=None, out_dtype=F32, tm=1024, tn=1024, tk=1024, ride=None):
    m, k = a.shape
    n = b.shape[1] if n is None else n
    tm, tn, tk = _tile(m, tm), _tile(n, tn, 128), _tile(k, tk, 128)
    no = b_off // tn
    return _mm(name, a, b, grid=(m // tm, n // tn, k // tk),
               a_spec=pl.BlockSpec((tm, tk), lambda i, j, kk: (i, kk)),
               b_spec=pl.BlockSpec((tk, tn), lambda i, j, kk: (kk, j + no)),
               o_spec=pl.BlockSpec((tm, tn), lambda i, j, kk: (i, j)),
               out_shape=jax.ShapeDtypeStruct((m, n), out_dtype), contract=NN, ride=ride)


def _mm_nt(name, a, b, *, out_dtype=F32, tm=1024, tn=1024, tk=1024, ride=None, add=None):
    m, k = a.shape
    n = b.shape[0]
    tm, tn, tk = _tile(m, tm), _tile(n, tn, 128), _tile(k, tk, 128)
    return _mm(name, a, b, grid=(m // tm, n // tn, k // tk),
               a_spec=pl.BlockSpec((tm, tk), lambda i, j, kk: (i, kk)),
               b_spec=pl.BlockSpec((tn, tk), lambda i, j, kk: (j, kk)),
               o_spec=pl.BlockSpec((tm, tn), lambda i, j, kk: (i, j)),
               out_shape=jax.ShapeDtypeStruct((m, n), out_dtype), contract=NT, ride=ride, add=add)


def _mm_tn(name, a, b, *, a_off=0, m=None, out_dtype=F32, tm=1024, tn=1024, tk=2048, ride=None):
    t, n = b.shape
    m = a.shape[1] if m is None else m
    tm, tn, tk = _tile(m, tm, 128), _tile(n, tn, 128), _tile(t, tk)
    mo = a_off // tm
    return _mm(name, a, b, grid=(m // tm, n // tn, t // tk),
               a_spec=pl.BlockSpec((tk, tm), lambda i, j, kk: (kk, i + mo)),
               b_spec=pl.BlockSpec((tk, tn), lambda i, j, kk: (kk, j)),
               o_spec=pl.BlockSpec((tm, tn), lambda i, j, kk: (i, j)),
               out_shape=jax.ShapeDtypeStruct((m, n), out_dtype), contract=TN, ride=ride)


def _rowwise(name, fn, rows, params, out_rows, out_reds, tm=512):
    rows = [r if isinstance(r, tuple) else (r, 0, r.shape[1]) for r in rows]
    t = rows[0][0].shape[0]
    tm = _tile(t, tm)
    in_specs = []
    for _, off, w in rows:
        in_specs.append(pl.BlockSpec((tm, w), functools.partial(lambda i, cb: (i, cb), cb=off // w)))
    for p in params:
        in_specs.append(pl.BlockSpec((1, p.shape[1]), lambda i: (0, 0)))
    out_specs = [pl.BlockSpec((tm, w), lambda i: (i, 0)) for w, _ in out_rows]
    out_specs += [pl.BlockSpec((1, w), lambda i: (0, 0)) for w in out_reds]
    out_shape = [jax.ShapeDtypeStruct((t, w), dt) for w, dt in out_rows]
    out_shape += [jax.ShapeDtypeStruct((1, w), F32) for w in out_reds]
    nr, npar, nor = len(rows), len(params), len(out_rows)

    def body(*refs):
        ins, outs = refs[:nr + npar], refs[nr + npar:]
        vals = [r[...].astype(F32) for r in ins[:nr]]
        vals += [jnp.broadcast_to(r[...], (tm, r.shape[1])) for r in ins[nr:]]
        res = fn(*vals)
        step = pl.program_id(0)
        for o, v in zip(outs[:nor], res[:nor]):
            o[...] = v.astype(o.dtype)
        for o, v in zip(outs[nor:], res[nor:]):
            _accumulate(o, v, step)

    res = _pcall(body, name=name, grid=(t // tm,), in_specs=in_specs, out_specs=out_specs, out_shape=out_shape,
                 compiler_params=_cparams(("arbitrary",)))(*[r[0] for r in rows], *params)
    return res


def _accumulate(o_ref, v, step):
    @pl.when(step == 0)
    def _():
        o_ref[...] = v

    @pl.when(step > 0)
    def _():
        o_ref[...] += v


def _colsum(v):
    return jnp.sum(v, axis=0, keepdims=True)


def _vjp_rowwise(name, fn, rows, params, cots, n_row_grads, tm=256, dtypes=None):
    nr, npar, nc = len(rows), len(params), len(cots)

    def bwd(*vals):
        prim, par, ct = vals[:nr], vals[nr + nc:], vals[nr:nr + nc]
        _, pull = jax.vjp(fn, *prim, *par)
        grads = pull(tuple(ct) if nc > 1 else ct[0])
        return tuple(grads[:n_row_grads]) + tuple(_colsum(g) for g in grads[nr:])

    dtypes = [F32] * n_row_grads if dtypes is None else dtypes
    widths = [(r[2] if isinstance(r, tuple) else r.shape[1], dt) for r, dt in zip(rows[:n_row_grads], dtypes)]
    return _rowwise(name, bwd, list(rows) + list(cots), params, widths, [p.shape[1] for p in params], tm=tm)


def _ln(s, g, b):
    mu = jnp.mean(s, axis=-1, keepdims=True)
    var = jnp.mean(jnp.square(s - mu), axis=-1, keepdims=True)
    return (s - mu) * lax.rsqrt(var + LN_EPS) * g + b


def _softplus(x):
    return jnp.maximum(x, 0.0) + jnp.log1p(jnp.exp(-jnp.abs(x)))


def _expm1(x):
    series = x * (1.0 + x * (1.0 / 2 + x * (1.0 / 6 + x * (1.0 / 24 + x * (1.0 / 120 + x * (1.0 / 720))))))
    return jnp.where(jnp.abs(x) < 0.25, series, jnp.exp(x) - 1.0)


def _f_resid_ln(h, branch, g, b):
    return _ln(ALPHA * h + branch, g, b)


def _f_ple(h, gp, pe, bpg, g, b):
    return _ln(ALPHA * h + jax.nn.sigmoid(gp + bpg) * pe, g, b)


def _f_merge(ga, gb, ya, yb, bm0, bm1):
    return jax.nn.sigmoid(ga + bm0) * ya + jax.nn.sigmoid(gb + bm1) * yb


def _f_rnn_out(hs, ry):
    return hs * jax.nn.gelu(ry, approximate=True)


def _f_logf(fl, bf):
    return -_softplus(-(fl + bf))


def _f_decay(lam):
    return -RG_C * _softplus(-lam)


def _f_gate(xc, ra, ia, decay, ba, bx):
    r = jax.nn.sigmoid(ra + ba)
    i = jax.nn.sigmoid(ia + bx)
    log_a = decay * r
    a = jnp.exp(log_a)
    mult = jnp.sqrt(-_expm1(2.0 * log_a))
    return a, mult * (i * xc)


def _f_act(hg, hu):
    return jax.nn.silu(hg) * hu


ATT_BLOCK = 512
ATT_HEADS_PER_STEP = 1
SCAN_ROWS = 128


def _scores(q, k, cq, ck, diagonal):
    s = lax.dot_general(q, k, (NT, ((), ())), preferred_element_type=F32) * SCALE
    s = s + cq - ck
    if diagonal:
        row = lax.broadcasted_iota(jnp.int32, s.shape, 0)
        col = lax.broadcasted_iota(jnp.int32, s.shape, 1)
        s = jnp.where(col <= row, s, NEG)
    return s


def _dscores(p, do, o, v):
    dob = do.astype(BF16)
    delta = jnp.sum(dob.astype(F32) * o, axis=1, keepdims=True)
    dp = lax.dot_general(dob, v.astype(BF16), (NT, ((), ())), preferred_element_type=F32)
    return p * (dp - delta)


def _attn_fwd(z, cq, ck, bsz, seq, ride=None):
    t = bsz * seq
    tq = _tile(seq, ATT_BLOCK)
    nq = seq // tq

    hp = ATT_HEADS_PER_STEP

    def body(q_ref, k_ref, v_ref, cq_ref, ck_ref, o_ref, ob_ref, lse_ref):
        for hh, i in [(hh, i) for hh in range(hp) for i in range(nq)]:
            lanes = slice(hh * HEAD_DIM, (hh + 1) * HEAD_DIM)
            rows = slice(i * tq, (i + 1) * tq)
            q = q_ref[rows, lanes].astype(BF16)
            cqi = cq_ref[hh, rows, :]

            def step(j, carry, diagonal, q=q, cqi=cqi, hh=hh, lanes=lanes):
                m, l, acc = carry
                keys = pl.ds(pl.multiple_of(j * tq, tq), tq)
                s = _scores(q, k_ref[keys, lanes].astype(BF16), cqi, ck_ref[hh, pl.ds(j, 1), :], diagonal)
                m_new = jnp.maximum(m, jnp.max(s, axis=1, keepdims=True))
                alpha = jnp.exp(m - m_new)
                p = jnp.exp(s - m_new)
                p_hi = p.astype(BF16)
                p_lo = (p - p_hi.astype(F32)).astype(BF16)
                vb = v_ref[keys, lanes].astype(BF16)
                pv = lax.dot_general(p_hi, vb, (NN, ((), ())), preferred_element_type=F32)
                pv = pv + lax.dot_general(p_lo, vb, (NN, ((), ())), preferred_element_type=F32)
                return m_new, alpha * l + jnp.sum(p, axis=1, keepdims=True), alpha * acc + pv

            carry = (jnp.full((tq, 1), NEG, F32), jnp.zeros((tq, 1), F32), jnp.zeros((tq, HEAD_DIM), F32))
            if i > 0:
                carry = lax.fori_loop(0, i, functools.partial(step, diagonal=False), carry)
            m, l, acc = step(i, carry, True)
            o = acc / l
            o_ref[rows, lanes] = o
            ob_ref[rows, lanes] = o.astype(BF16)
            lse_ref[hh, rows, :] = m + jnp.log(l)

    groups = N_HEADS // hp
    head = (seq, hp * HEAD_DIM)
    in_specs = [
        pl.BlockSpec(head, lambda b, g: (b, g)),
        pl.BlockSpec(head, lambda b, g: (b, groups + g)),
        pl.BlockSpec(head, lambda b, g: (b, 2 * groups + g)),
        pl.BlockSpec((None, hp, seq, 1), lambda b, g: (b, g, 0, 0)),
        pl.BlockSpec((None, hp, nq, tq), lambda b, g: (b, g, 0, 0)),
    ]
    out_specs = [pl.BlockSpec(head, lambda b, g: (b, g)), pl.BlockSpec(head, lambda b, g: (b, g)),
                 pl.BlockSpec((None, hp, seq, 1), lambda b, g: (b, g, 0, 0))]
    out_shape = [jax.ShapeDtypeStruct((t, D_MODEL), F32), jax.ShapeDtypeStruct((t, D_MODEL), BF16),
                 jax.ShapeDtypeStruct((bsz, N_HEADS, seq, 1), F32)]
    return _pcall(body, ride=ride, name="attn_fwd", grid=(bsz, groups), in_specs=in_specs, out_specs=out_specs,
                  out_shape=out_shape, compiler_params=_cparams(("parallel", "parallel")))(
                      z, z, z, cq, ck.reshape(bsz, N_HEADS, nq, tq))


def _attn_bwd(z, att, datt, lse, cq, ck, bsz, seq, ride=None):
    t = bsz * seq
    tq = _tile(seq, ATT_BLOCK)
    nq = seq // tq

    def body(q_ref, k_ref, v_ref, o_ref, do_ref, lse_ref, cq_ref, ck_ref,
             dq_ref, dk_ref, dv_ref, dcq_ref, dck_ref, dq_sc):
        dq_sc[...] = jnp.zeros_like(dq_sc)
        dcq_ref[...] = jnp.zeros_like(dcq_ref)
        for j in range(nq):
            keys = slice(j * tq, (j + 1) * tq)
            kb = k_ref[keys, :].astype(BF16)
            vb = v_ref[keys, :].astype(BF16)
            ckj = ck_ref[j:j + 1, :]

            def step(i, carry, diagonal, kb=kb, vb=vb, ckj=ckj):
                dk, dv, dc = carry
                rows = pl.ds(pl.multiple_of(i * tq, tq), tq)
                qb = q_ref[rows, :].astype(BF16)
                do = do_ref[rows, :]
                s = _scores(qb, kb, cq_ref[rows, :], ckj, diagonal)
                p = jnp.exp(s - lse_ref[rows, :])
                ds = _dscores(p, do, o_ref[rows, :], vb)
                dsb = (ds * SCALE).astype(BF16)
                dq_sc[rows, :] += lax.dot_general(dsb, kb, (NN, ((), ())), preferred_element_type=F32)
                dcq_ref[rows, :] += jnp.sum(ds, axis=1, keepdims=True)
                dv = dv + lax.dot_general(p.astype(BF16), do.astype(BF16), (TN, ((), ())),
                                          preferred_element_type=F32)
                dk = dk + lax.dot_general(dsb, qb, (TN, ((), ())), preferred_element_type=F32)
                return dk, dv, dc - jnp.sum(ds, axis=0, keepdims=True)

            zero = jnp.zeros((tq, HEAD_DIM), F32)
            carry = step(j, (zero, zero, jnp.zeros((1, tq), F32)), True)
            if j + 1 < nq:
                carry = lax.fori_loop(j + 1, nq, functools.partial(step, diagonal=False), carry)
            dk, dv, dck_ref[j:j + 1, :] = carry
            dk_ref[keys, :] = dk.astype(BF16)
            dv_ref[keys, :] = dv.astype(BF16)
        dq_ref[...] = dq_sc[...].astype(BF16)

    head = (seq, HEAD_DIM)
    hmap = lambda b, h: (b, h)
    col = pl.BlockSpec((None, None, seq, 1), lambda b, h: (b, h, 0, 0))
    row = pl.BlockSpec((None, None, nq, tq), lambda b, h: (b, h, 0, 0))
    in_specs = [pl.BlockSpec(head, hmap),
                pl.BlockSpec(head, lambda b, h: (b, N_HEADS + h)),
                pl.BlockSpec(head, lambda b, h: (b, 2 * N_HEADS + h)),
                pl.BlockSpec(head, hmap), pl.BlockSpec(head, hmap), col, col, row]
    big = jax.ShapeDtypeStruct((t, D_MODEL), BF16)
    return _pcall(body, ride=ride, name="attn_bwd", grid=(bsz, N_HEADS), in_specs=in_specs,
                  out_specs=[pl.BlockSpec(head, hmap)] * 3 + [col, row],
                  out_shape=[big, big, big, jax.ShapeDtypeStruct((bsz, N_HEADS, seq, 1), F32),
                             jax.ShapeDtypeStruct((bsz, N_HEADS, nq, tq), F32)],
                  scratch_shapes=[pltpu.VMEM(head, F32)],
                  compiler_params=_cparams(("parallel", "parallel")))(
                      z, z, z, att, datt, lse, cq, ck.reshape(bsz, N_HEADS, nq, tq))


def _scan(name, a, u, bsz, seq, *, reverse, with_prev=False, tb=512, ride=None):
    c = u.shape[1]
    tb = _tile(seq, tb)
    nb = seq // tb
    rc = SCAN_ROWS if tb % SCAN_ROWS == 0 else tb
    has_a = a is not None

    def body(*refs):
        if has_a:
            a_ref, u_ref = refs[0], refs[1]
            rest = refs[2:]
        else:
            u_ref = refs[0]
            rest = refs[1:]
        outs = rest[:2] if with_prev else rest[:1]
        carry_sc, afirst_sc = rest[-2], rest[-1]
        step = pl.program_id(1)

        @pl.when(step == 0)
        def _():
            carry_sc[...] = jnp.zeros_like(carry_sc)
            afirst_sc[...] = jnp.zeros_like(afirst_sc)

        row = lax.broadcasted_iota(jnp.int32, (rc, BLK), 0)
        pieces = list(range(tb // rc))
        for ls in range(c // BLK):
            lanes = slice(ls * BLK, (ls + 1) * BLK)
            carry = carry_sc[:, lanes]
            afirst = afirst_sc[:, lanes]
            for pc in (reversed(pieces) if reverse else pieces):
                rows = slice(pc * rc, (pc + 1) * rc)
                uu = u_ref[rows, lanes]
                if has_a:
                    aa = a_ref[rows, lanes]
                    coef = jnp.where(row < rc - 1, pltpu.roll(aa, rc - 1, 0), afirst) if reverse else aa
                k = 1
                while k < rc:
                    shift = rc - k if reverse else k
                    keep = (row < rc - k) if reverse else (row >= k)
                    uu_sh = jnp.where(keep, pltpu.roll(uu, shift, 0), 0.0)
                    if has_a:
                        uu = coef * uu_sh + uu
                        coef = coef * jnp.where(keep, pltpu.roll(coef, shift, 0), 1.0)
                    else:
                        uu = uu + uu_sh
                    k *= 2
                h = uu + coef * carry if has_a else uu + carry
                outs[0][rows, lanes] = h
                if with_prev:
                    outs[1][rows, lanes] = jnp.where(row >= 1, pltpu.roll(h, 1, 0), carry)
                edge = pc * rc if reverse else (pc + 1) * rc - 1
                carry = outs[0][edge:edge + 1, lanes]
                if has_a and reverse:
                    afirst = a_ref[edge:edge + 1, lanes]
            carry_sc[:, lanes] = carry
            if has_a and reverse:
                afirst_sc[:, lanes] = afirst

    if reverse:
        imap = lambda b, s: (b * nb + nb - 1 - s, 0)
    else:
        imap = lambda b, s: (b * nb + s, 0)
    spec = pl.BlockSpec((tb, c), imap)
    n_in = 2 if has_a else 1
    n_out = 2 if with_prev else 1
    res = _pcall(body, ride=ride, name=name, grid=(bsz, nb), in_specs=[spec] * n_in, out_specs=[spec] * n_out,
                 out_shape=[jax.ShapeDtypeStruct(u.shape, F32)] * n_out,
                 scratch_shapes=[pltpu.VMEM((1, c), F32), pltpu.VMEM((1, c), F32)],
                 compiler_params=_cparams(("parallel", "arbitrary")))(*([a, u] if has_a else [u]))
    return res if with_prev else res[0]


def _conv_fwd(z, w, b, bsz, seq, tb=512):
    c = D_MODEL
    t = bsz * seq
    tb = _tile(seq, tb)
    nb = seq // tb

    def body(x_ref, w_ref, b_ref, o_ref, tail_sc):
        step = pl.program_id(1)

        @pl.when(step == 0)
        def _():
            tail_sc[...] = jnp.zeros_like(tail_sc)

        x = x_ref[...]
        row8 = lax.broadcasted_iota(jnp.int32, (8, c), 0)
        tail = tail_sc[...]
        acc = w_ref[CONV_W - 1:CONV_W, :] * x + b_ref[...]
        for sh in range(1, CONV_W):
            xs = pltpu.roll(x, sh, 0)
            top = jnp.where(row8 < sh, pltpu.roll(tail, sh, 0), xs[0:8, :])
            xs = jnp.concatenate([top, xs[8:, :]], axis=0) if tb > 8 else top
            acc = acc + w_ref[CONV_W - 1 - sh:CONV_W - sh, :] * xs
        o_ref[...] = acc
        tail_sc[...] = x_ref[tb - 8:tb, :]

    return _pcall(body, name="conv_fwd", grid=(bsz, nb),
                  in_specs=[pl.BlockSpec((tb, c), lambda bb, s: (bb * nb + s, OFF_RX // c)),
                            pl.BlockSpec((CONV_W, c), lambda bb, s: (0, 0)),
                            pl.BlockSpec((1, c), lambda bb, s: (0, 0))],
                  out_specs=pl.BlockSpec((tb, c), lambda bb, s: (bb * nb + s, 0)),
                  out_shape=jax.ShapeDtypeStruct((t, c), F32),
                  scratch_shapes=[pltpu.VMEM((8, c), F32)],
                  compiler_params=_cparams(("parallel", "arbitrary")))(z, w, b)


def _conv_bwd(z, dxc, w, bsz, seq, tb=512):
    c = D_MODEL
    t = bsz * seq
    tb = _tile(seq, tb)
    nb = seq // tb

    def body(x_ref, g_ref, w_ref, dx_ref, dw_ref, db_ref, head_sc):
        bb, step = pl.program_id(0), pl.program_id(1)

        @pl.when(step == 0)
        def _():
            head_sc[...] = jnp.zeros_like(head_sc)

        x, g = x_ref[...], g_ref[...]
        row8 = lax.broadcasted_iota(jnp.int32, (8, c), 0)
        head = head_sc[...]
        dx = w_ref[CONV_W - 1:CONV_W, :] * g
        dws = [None] * CONV_W
        dws[CONV_W - 1] = _colsum(g * x)
        for sh in range(1, CONV_W):
            gs = pltpu.roll(g, tb - sh, 0)
            bot = jnp.where(row8 >= 8 - sh, pltpu.roll(head, 8 - sh, 0), gs[tb - 8:tb, :])
            gs = jnp.concatenate([gs[:tb - 8, :], bot], axis=0) if tb > 8 else bot
            dx = dx + w_ref[CONV_W - 1 - sh:CONV_W - sh, :] * gs
            dws[CONV_W - 1 - sh] = _colsum(gs * x)
        dx_ref[...] = dx.astype(dx_ref.dtype)
        first = (bb == 0) & (step == 0)
        dw = jnp.concatenate(dws, axis=0)
        db = _colsum(g)

        @pl.when(first)
        def _():
            dw_ref[...] = dw
            db_ref[...] = db

        @pl.when(jnp.logical_not(first))
        def _():
            dw_ref[...] += dw
            db_ref[...] += db

        head_sc[...] = g_ref[0:8, :]

    rmap = lambda bb, s: (bb * nb + nb - 1 - s, 0)
    return _pcall(body, name="conv_bwd", grid=(bsz, nb),
                  in_specs=[pl.BlockSpec((tb, c), lambda bb, s: (bb * nb + nb - 1 - s, OFF_RX // c)),
                            pl.BlockSpec((tb, c), rmap),
                            pl.BlockSpec((CONV_W, c), lambda bb, s: (0, 0))],
                  out_specs=[pl.BlockSpec((tb, c), rmap),
                             pl.BlockSpec((CONV_W, c), lambda bb, s: (0, 0)),
                             pl.BlockSpec((1, c), lambda bb, s: (0, 0))],
                  out_shape=[jax.ShapeDtypeStruct((t, c), BF16), jax.ShapeDtypeStruct((CONV_W, c), F32),
                             jax.ShapeDtypeStruct((1, c), F32)],
                  scratch_shapes=[pltpu.VMEM((8, c), F32)],
                  compiler_params=_cparams(("arbitrary", "arbitrary")))(z, dxc, w)


def _gate_fwd(xc, w_a, w_x, b_a, b_x, lam, tm=1024, ride=None):
    t = xc.shape[0]
    tm = _tile(t, tm)

    def body(xc_ref, wa_ref, wx_ref, ba_ref, bx_ref, lam_ref, a_ref, u_ref):
        xc_b = xc_ref[...]
        xb = xc_b.astype(BF16)
        ra = lax.dot_general(xb, wa_ref[...].astype(BF16), (NN, ((), ())), preferred_element_type=F32)
        ia = lax.dot_general(xb, wx_ref[...].astype(BF16), (NN, ((), ())), preferred_element_type=F32)
        a, u = _f_gate(xc_b, ra, ia, lam_ref[...], ba_ref[...], bx_ref[...])
        a_ref[...] = a
        u_ref[...] = u

    row = pl.BlockSpec((tm, BLK), lambda n, i: (i, n))
    wsp = pl.BlockSpec((None, BLK, BLK), lambda n, i: (n, 0, 0))
    vec = pl.BlockSpec((1, BLK), lambda n, i: (0, n))
    return _pcall(body, ride=ride, name="gate_fwd", grid=(N_BLK, t // tm), in_specs=[row, wsp, wsp, vec, vec, vec],
                  out_specs=[row, row], out_shape=[jax.ShapeDtypeStruct((t, D_MODEL), F32)] * 2,
                  compiler_params=_cparams(("parallel", "parallel")))(xc, w_a, w_x, b_a, b_x, lam)


def _gate_bwd(xc, w_a, w_x, b_a, b_x, lam, hprev, du, tm=1024, ride=None):
    t = xc.shape[0]
    tm = _tile(t, tm)

    def body(xc_ref, wa_ref, wx_ref, ba_ref, bx_ref, lam_ref, hp_ref, du_ref,
             dxc_ref, dwa_ref, dwx_ref, dba_ref, dbx_ref, dlam_ref):
        step = pl.program_id(1)
        xc_b = xc_ref[...]
        xb = xc_b.astype(BF16)
        wa, wx = wa_ref[...].astype(BF16), wx_ref[...].astype(BF16)
        ra = lax.dot_general(xb, wa, (NN, ((), ())), preferred_element_type=F32)
        ia = lax.dot_general(xb, wx, (NN, ((), ())), preferred_element_type=F32)
        full = lambda r: jnp.broadcast_to(r[...], (tm, BLK))
        _, pull = jax.vjp(_f_gate, xc_b, ra, ia, full(lam_ref), full(ba_ref), full(bx_ref))
        du_b = du_ref[...]
        dxc, dra, dia, dlam, dba, dbx = pull((du_b * hp_ref[...], du_b))
        drb, dib = dra.astype(BF16), dia.astype(BF16)
        dxc = dxc + lax.dot_general(drb, wa, (NT, ((), ())), preferred_element_type=F32)
        dxc = dxc + lax.dot_general(dib, wx, (NT, ((), ())), preferred_element_type=F32)
        dxc_ref[...] = dxc
        _accumulate(dwa_ref, lax.dot_general(xb, drb, (TN, ((), ())), preferred_element_type=F32), step)
        _accumulate(dwx_ref, lax.dot_general(xb, dib, (TN, ((), ())), preferred_element_type=F32), step)
        _accumulate(dba_ref, _colsum(dba), step)
        _accumulate(dbx_ref, _colsum(dbx), step)
        _accumulate(dlam_ref, _colsum(dlam), step)

    row = pl.BlockSpec((tm, BLK), lambda n, i: (i, n))
    wsp = pl.BlockSpec((None, BLK, BLK), lambda n, i: (n, 0, 0))
    vec = pl.BlockSpec((1, BLK), lambda n, i: (0, n))
    wshape = jax.ShapeDtypeStruct((N_BLK, BLK, BLK), F32)
    vshape = jax.ShapeDtypeStruct((1, D_MODEL), F32)
    return _pcall(body, ride=ride, name="gate_bwd", grid=(N_BLK, t // tm),
                  in_specs=[row, wsp, wsp, vec, vec, vec, row, row],
                  out_specs=[row, wsp, wsp, vec, vec, vec],
                  out_shape=[jax.ShapeDtypeStruct((t, D_MODEL), F32), wshape, wshape, vshape, vshape, vshape],
                  compiler_params=_cparams(("parallel", "arbitrary")))(xc, w_a, w_x, b_a, b_x, lam, hprev, du)


def _ffn_in_act(a, w, tm=1024, ride=None):
    t = a.shape[0]
    tm = _tile(t, tm)

    def body(a_ref, wg_ref, wu_ref, hgu_ref, act_ref):
        ab = a_ref[...].astype(BF16)
        hg = lax.dot_general(ab, wg_ref[...].astype(BF16), (NN, ((), ())), preferred_element_type=F32)
        hu = lax.dot_general(ab, wu_ref[...].astype(BF16), (NN, ((), ())), preferred_element_type=F32)
        hgu_ref[0] = hg
        hgu_ref[1] = hu
        act_ref[...] = _f_act(hg, hu).astype(act_ref.dtype)

    wspec = lambda off: pl.BlockSpec((None, D_MODEL, FF_SH), lambda i, s: (s + off, 0, 0))
    hgu, act = _pcall(body, ride=ride, name="ffn_in", grid=(t // tm, N_FF),
                      in_specs=[pl.BlockSpec((tm, D_MODEL), lambda i, s: (i, 0)), wspec(0), wspec(N_FF)],
                      out_specs=[pl.BlockSpec((2, None, tm, FF_SH), lambda i, s: (0, s, i, 0)),
                                 pl.BlockSpec((None, tm, FF_SH), lambda i, s: (s, i, 0))],
                      out_shape=[jax.ShapeDtypeStruct((2, N_FF, t, FF_SH), F32),
                                 jax.ShapeDtypeStruct((N_FF, t, FF_SH), BF16)],
                      compiler_params=_cparams(("parallel", "parallel")))(a, w, w)
    return hgu.reshape(2 * N_FF, t, FF_SH), act


def _ffn_out_dx_act(d, w, hgu, tm=1024, ride=None):
    t = d.shape[0]
    tm = _tile(t, tm)

    def body(d_ref, w_ref, hg_ref, hu_ref, o_ref):
        dact = lax.dot_general(d_ref[...].astype(BF16), w_ref[...].astype(BF16), (NT, ((), ())),
                               preferred_element_type=F32)
        _, pull = jax.vjp(_f_act, hg_ref[...], hu_ref[...])
        dhg, dhu = pull(dact)
        o_ref[0] = dhg.astype(o_ref.dtype)
        o_ref[1] = dhu.astype(o_ref.dtype)

    hspec = lambda off: pl.BlockSpec((None, tm, FF_SH), lambda i, s: (s + off, i, 0))
    res = _pcall(body, ride=ride, name="ffn_out_dx", grid=(t // tm, N_FF),
                 in_specs=[pl.BlockSpec((tm, D_MODEL), lambda i, s: (i, 0)),
                           pl.BlockSpec((None, FF_SH, D_MODEL), lambda i, s: (s, 0, 0)), hspec(0), hspec(N_FF)],
                 out_specs=pl.BlockSpec((2, None, tm, FF_SH), lambda i, s: (0, s, i, 0)),
                 out_shape=jax.ShapeDtypeStruct((2, N_FF, t, FF_SH), BF16),
                 compiler_params=_cparams(("parallel", "parallel")))(d, w, hgu, hgu)
    return res.reshape(2 * N_FF, t, FF_SH)


def _adamw(name, parts, w, m, v, tr=128, ride=None):
    ng = len(parts)
    n_src, r, c = parts[0].shape
    per = w.shape[1] // r
    assert w.shape[0] * per == ng and w.shape[2] == c
    tr = _tile(r, tr)
    nb = r // tr
    bc1 = 1.0 - ADAM_B1 ** ADAM_STEP
    bc2 = 1.0 - ADAM_B2 ** ADAM_STEP

    def body(*refs):
        p_refs = refs[:ng]
        w_ref, m_ref, v_ref, g_ref, d_ref, nm_ref, nv_ref = refs[ng:]
        grp = pl.program_id(0)

        def update(p_ref):
            g = p_ref[0].astype(F32)
            for s in range(1, n_src):
                g = g + p_ref[s].astype(F32)
            nm = ADAM_B1 * m_ref[...] + (1.0 - ADAM_B1) * g
            nv = ADAM_B2 * v_ref[...] + (1.0 - ADAM_B2) * jnp.square(g)
            g_ref[...] = g
            nm_ref[...] = nm
            nv_ref[...] = nv
            d_ref[...] = -ADAM_LR * ((nm / bc1) / (jnp.sqrt(nv / bc2) + ADAM_EPS) + ADAM_WD * w_ref[...])

        for k in range(ng):
            pl.when(grp == k)(functools.partial(update, p_refs[k]))

    p_specs = [pl.BlockSpec((n_src, tr, c), functools.partial(lambda gi, i, k: (0, jnp.where(gi == k, i, 0), 0), k=k))
               for k in range(ng)]
    spec = pl.BlockSpec((None, tr, c), lambda gi, i: (gi // per, (gi % per) * nb + i, 0))
    return _pcall(body, ride=ride, name=name, grid=(ng, nb), in_specs=p_specs + [spec, spec, spec],
                  out_specs=[spec] * 4, out_shape=[jax.ShapeDtypeStruct(w.shape, F32)] * 4,
                  compiler_params=_cparams(("parallel", "parallel")))(*parts, w, m, v)


def _sum_parts(name, parts, tr=256):
    _, r, c = parts.shape
    tr = _tile(r, tr)

    def body(p_ref, o_ref):
        g = p_ref[0]
        for s in range(1, parts.shape[0]):
            g = g + p_ref[s]
        o_ref[...] = g

    return _pcall(body, name=name, grid=(r // tr,),
                  in_specs=[pl.BlockSpec((parts.shape[0], tr, c), lambda i: (0, i, 0))],
                  out_specs=pl.BlockSpec((tr, c), lambda i: (i, 0)),
                  out_shape=jax.ShapeDtypeStruct((r, c), F32), compiler_params=_cparams(("parallel",)))(parts)


def _peer(k):
    x, y, c = lax.axis_index("x"), lax.axis_index("y"), lax.axis_index("c")
    return (x ^ ((k >> 2) & 1), y ^ ((k >> 1) & 1), c ^ (k & 1))


def _my_id():
    return 4 * lax.axis_index("x") + 2 * lax.axis_index("y") + lax.axis_index("c")


def _exchange(name, ride):
    n = len(ride.arrays)

    def body(*refs):
        ride.begin(refs[:n], refs[n:2 * n], refs[2 * n:])
        ride.finish(refs[:n], refs[n:2 * n], refs[2 * n:])

    hbm = pl.BlockSpec(memory_space=pltpu.HBM)
    return _pcall(body, name=name, in_specs=[hbm] * n, out_specs=[hbm] * n, out_shape=ride.out_shapes(),
                  scratch_shapes=ride.scratch())(*ride.arrays)


def _row(v):
    return v.reshape(1, -1)


def _time_major_heads(c, bsz, seq):
    return c.reshape(bsz, seq, BLK)[:, :, :N_HEADS].transpose(0, 2, 1)


def _no_ride(*_):
    return None


TWICE = [(D_MODEL, F32), (D_MODEL, BF16)]


def _both(fn):
    def run(*v):
        y = fn(*v)
        return y, y
    return run


def _layer_fwd(h, hb, p_l, w, bsz, seq, ride_of=_no_ride):
    t = bsz * seq
    zq = _mm_nn("z_proj_qkv", hb, w['w_in7'], n=QKV, out_dtype=BF16, ride=ride_of('z_proj_qkv'))
    zr = _mm_nn("z_proj_rest", hb, w['w_in7'], b_off=QKV, n=4 * D_MODEL, ride=ride_of('z_proj_rest'))
    fl = _mm_nn("f_proj", hb, w['w_inf'])
    logf, = _rowwise("logf_fwd", lambda f, b: (_f_logf(f, b),), [fl], [w['b_forget']], [(BLK, F32)], [])
    c = _scan("cumsum_fwd", None, logf, bsz, seq, reverse=False)
    ct = _time_major_heads(c, bsz, seq)
    cq, ck = ct[..., None], ct[:, :, None, :]
    att, attb, lse = _attn_fwd(zq, cq, ck, bsz, seq, ride=ride_of('attn_fwd'))
    xc = _conv_fwd(zr, w['conv_w'], w['conv_b'], bsz, seq)
    decay, = _rowwise("decay_fwd", lambda lam: (_f_decay(lam),), [w['rg_lambda']], [], [(D_MODEL, F32)], [])
    a, u = _gate_fwd(xc, w['rg_w_a'], w['rg_w_x'], w['rg_b_a'], w['rg_b_x'], decay, ride=ride_of('gate_fwd'))
    hs, hprev = _scan("lru_fwd", a, u, bsz, seq, reverse=False, with_prev=True, ride=ride_of('lru_fwd'))
    rnn, = _rowwise("rnn_out_fwd", lambda s, y: (_f_rnn_out(s, y),), [hs, (zr, OFF_RY, D_MODEL)], [],
                    [(D_MODEL, BF16)], [])
    ya = _mm_nn("branch_att", attb, w['w_branch_att'])
    yb = _mm_nn("branch_rnn", rnn, w['w_branch_rnn'])
    merged, = _rowwise("merge_fwd", lambda *v: (_f_merge(*v),),
                       [(zr, OFF_GA, D_MODEL), (zr, OFF_GB, D_MODEL), ya, yb], [w['b_merge0'], w['b_merge1']],
                       [(D_MODEL, BF16)], [])
    mix = _mm_nn("mix_out", merged, w['w_out'])
    h1, h1b = _rowwise("ln_mix_fwd", _both(_f_resid_ln), [h, mix], [w['ln_mix_g'], w['ln_mix_b']], TWICE, [])
    tm = _tile(t, 1024)
    hgu, act = _ffn_in_act(h1b, w['w_ffn_in'], ride=ride_of('ffn_in'))
    ffn = _mm("ffn_out", act, w['w_ffn_out'], grid=(t // tm, 1, N_FF),
              a_spec=pl.BlockSpec((None, tm, FF_SH), lambda i, j, s: (s, i, 0)),
              b_spec=pl.BlockSpec((None, FF_SH, D_MODEL), lambda i, j, s: (s, 0, 0)),
              o_spec=pl.BlockSpec((tm, D_MODEL), lambda i, j, s: (i, 0)),
              out_shape=jax.ShapeDtypeStruct((t, D_MODEL), F32), contract=NN, ride=ride_of('ffn_out'))
    h2, h2b = _rowwise("ln_ffn_fwd", _both(_f_resid_ln), [h1, ffn], [w['ln_ffn_g'], w['ln_ffn_b']], TWICE, [])
    gp = _mm_nn("ple_gate", h2b, w['w_ple_gate'])
    pe = _mm_nn("ple_proj", p_l, w['w_ple'])
    h3, h3b = _rowwise("ln_ple_fwd", _both(_f_ple), [h2, gp, pe],
                       [w['b_ple_gate'], w['ln_ple_g'], w['ln_ple_b']], TWICE, [])
    saved = dict(h=h, hb=hb, zq=zq, zr=zr, fl=fl, cq=cq, ck=ck, att=att, attb=attb, lse=lse, xc=xc, a=a, decay=decay,
                 hprev=hprev, hs=hs, rnn=rnn, ya=ya, yb=yb, merged=merged, mix=mix, h1=h1, h1b=h1b, hgu=hgu,
                 act=act, ffn=ffn, h2=h2, h2b=h2b, gp=gp, pe=pe)
    return h3, h3b, saved


def _layer_bwd(dh3, p_l, w, s, bsz, seq, ride_of=_no_ride):
    t = bsz * seq
    g = {}
    dh2, dgp, dpe, g['b_ple_gate'], g['ln_ple_g'], g['ln_ple_b'] = _vjp_rowwise(
        "ln_ple_bwd", _f_ple, [s['h2'], s['gp'], s['pe']], [w['b_ple_gate'], w['ln_ple_g'], w['ln_ple_b']], [dh3], 3,
        dtypes=[F32, BF16, BF16])
    g['w_ple_gate'] = _mm_tn("ple_gate_dw", s['h2b'], dgp, out_dtype=BF16)
    g['w_ple'] = _mm_tn("ple_proj_dw", p_l, dpe, out_dtype=BF16)
    dh2b = _mm_nt("ple_gate_dx", dgp, w['w_ple_gate'])
    dh1, dffn, g['ln_ffn_g'], g['ln_ffn_b'] = _ln_resid_bwd(
        "ln_ffn_bwd", s['h1'], s['ffn'], w['ln_ffn_g'], w['ln_ffn_b'], dh2, dh2b)
    tm = _tile(t, 1024)
    tk = _tile(t, 2048)
    g['w_ffn_out'] = _mm("ffn_out_dw", s['act'], dffn, grid=(N_FF, 1, t // tk),
                         a_spec=pl.BlockSpec((None, tk, FF_SH), lambda ss, j, k: (ss, k, 0)),
                         b_spec=pl.BlockSpec((tk, D_MODEL), lambda ss, j, k: (k, 0)),
                         o_spec=pl.BlockSpec((None, FF_SH, D_MODEL), lambda ss, j, k: (ss, 0, 0)),
                         out_shape=jax.ShapeDtypeStruct((N_FF, FF_SH, D_MODEL), BF16), contract=TN)
    dhgu = _ffn_out_dx_act(dffn, w['w_ffn_out'], s['hgu'], ride=ride_of('ffn_out_dx', g))
    g['w_ffn_in'] = _mm("ffn_in_dw", s['h1b'], dhgu, grid=(2 * N_FF, 1, t // tk),
                        a_spec=pl.BlockSpec((tk, D_MODEL), lambda ss, j, k: (k, 0)),
                        b_spec=pl.BlockSpec((None, tk, FF_SH), lambda ss, j, k: (ss, k, 0)),
                        o_spec=pl.BlockSpec((None, D_MODEL, FF_SH), lambda ss, j, k: (ss, 0, 0)),
                        out_shape=jax.ShapeDtypeStruct((2 * N_FF, D_MODEL, FF_SH), BF16), contract=TN,
                        ride=ride_of('ffn_in_dw', g))
    dh1b = _mm("ffn_in_dx", dhgu, w['w_ffn_in'], grid=(t // tm, 1, 2 * N_FF),
               a_spec=pl.BlockSpec((None, tm, FF_SH), lambda i, j, ss: (ss, i, 0)),
               b_spec=pl.BlockSpec((None, D_MODEL, FF_SH), lambda i, j, ss: (ss, 0, 0)),
               o_spec=pl.BlockSpec((tm, D_MODEL), lambda i, j, ss: (i, 0)),
               out_shape=jax.ShapeDtypeStruct((t, D_MODEL), F32), contract=NT, ride=ride_of('ffn_in_dx', g))
    dh, dmix, g['ln_mix_g'], g['ln_mix_b'] = _ln_resid_bwd(
        "ln_mix_bwd", s['h'], s['mix'], w['ln_mix_g'], w['ln_mix_b'], dh1, dh1b)
    g['w_out'] = _mm_tn("mix_out_dw", s['merged'], dmix, out_dtype=BF16)
    dmerged = _mm_nt("mix_out_dx", dmix, w['w_out'])
    z = s['zr']
    dga, dgb, dya, dyb, dbm0, dbm1 = _vjp_rowwise(
        "merge_bwd", _f_merge, [(z, OFF_GA, D_MODEL), (z, OFF_GB, D_MODEL), s['ya'], s['yb']],
        [w['b_merge0'], w['b_merge1']], [dmerged], 4, dtypes=[BF16] * 4)
    g['b_merge'] = jnp.concatenate([dbm0, dbm1], axis=0)
    g['w_branch_att'] = _mm_tn("branch_att_dw", s['attb'], dya, out_dtype=BF16)
    g['w_branch_rnn'] = _mm_tn("branch_rnn_dw", s['rnn'], dyb, out_dtype=BF16)
    datt = _mm_nt("branch_att_dx", dya, w['w_branch_att'], out_dtype=BF16)
    drnn = _mm_nt("branch_rnn_dx", dyb, w['w_branch_rnn'])
    dhs, dry = _vjp_rowwise("rnn_out_bwd", _f_rnn_out, [s['hs'], (z, OFF_RY, D_MODEL)], [], [drnn], 2,
                            dtypes=[F32, BF16])
    lam = _scan("lru_bwd", s['a'], dhs, bsz, seq, reverse=True)
    dxc, g['rg_w_a'], g['rg_w_x'], g['rg_b_a'], g['rg_b_x'], ddecay = _gate_bwd(
        s['xc'], w['rg_w_a'], w['rg_w_x'], w['rg_b_a'], w['rg_b_x'], s['decay'], s['hprev'], lam,
        ride=ride_of('gate_bwd', g))
    g['rg_lambda'], = _vjp_rowwise("decay_bwd", _f_decay, [w['rg_lambda']], [], [ddecay], 1)
    drx, g['conv_w'], g['conv_b'] = _conv_bwd(z, dxc, w['conv_w'], bsz, seq)
    dq, dk, dv, dcq, dck = _attn_bwd(s['zq'], s['att'], datt, s['lse'], s['cq'], s['ck'], bsz, seq,
                                     ride=ride_of('attn_bwd', g))
    dc = (dcq[:, :, :, 0] + dck.reshape(bsz, N_HEADS, seq)).transpose(0, 2, 1)
    dc = jnp.pad(dc, ((0, 0), (0, 0), (0, BLK - N_HEADS))).reshape(t, BLK)
    dlogf = _scan("cumsum_bwd", None, dc, bsz, seq, reverse=True)
    dfl, g['b_forget'] = _vjp_rowwise("logf_bwd", _f_logf, [s['fl']], [w['b_forget']], [dlogf], 1, dtypes=[BF16])
    dz = jnp.concatenate([dq, dk, dv, drx, dry, dga, dgb], axis=1)
    g['w_in7'] = _mm_tn("z_proj_dw", s['hb'], dz, out_dtype=BF16)
    g['w_inf'] = _mm_tn("f_proj_dw", s['hb'], dfl, out_dtype=BF16)
    dh = _mm_nt("z_proj_dx", dz, w['w_in7'], ride=ride_of('z_proj_dx', g), add=dh)
    dh = _mm_nt("f_proj_dx", dfl, w['w_inf'], add=dh)
    return dh, g


def _ln_resid_bwd(name, h, branch, gam, bet, d0, d1):
    def bwd(hv, bv, d0v, d1v, gv, btv):
        _, pull = jax.vjp(_f_resid_ln, hv, bv, gv, btv)
        dh, db, dg, dbt = pull(d0v + d1v)
        return dh, db, _colsum(dg), _colsum(dbt)

    return _rowwise(name, bwd, [h, branch, d0, d1], [gam, bet], [(D_MODEL, F32), (D_MODEL, BF16)],
                    [D_MODEL, D_MODEL], tm=256)


class _Schedule:
    FWD = {'z_proj_qkv': ['w_ffn_out'], 'z_proj_rest': ['w_branch_att', 'w_branch_rnn', 'w_out', 'w_ple_gate'],
           'attn_fwd': ['w_in'], 'ffn_in': ['w_ffn_in'], 'ffn_out': ['w_ple', 'conv_w', 'b_merge']}
    FIRST = ['w_in', 'conv_w', 'b_merge']
    OWN = {'z_proj_qkv': ['w_branch_att', 'w_branch_rnn', 'w_out'], 'z_proj_rest': ['w_ffn_in'],
           'attn_fwd': ['w_ffn_out', 'w_ple_gate', 'w_ple']}
    NEXT = {'attn_fwd': ['w_in'], 'gate_fwd': ['w_ffn_out'],
            'lru_fwd': ['w_branch_att', 'w_branch_rnn', 'w_out', 'w_ple_gate'],
            'ffn_in': ['w_ffn_in'], 'ffn_out': ['w_ple', 'conv_w', 'b_merge']}
    BWD = {'ffn_out_dx': ['w_ffn_out'], 'ffn_in_dw': ['w_ple_gate', 'w_ple'],
           'gate_bwd': ['w_out', 'w_branch_att', 'w_branch_rnn'],
           'attn_bwd': ['w_ffn_in', 'conv_w', 'b_merge']}

    def __init__(self, shards, depth):
        self.shards, self.depth = shards, depth
        self.gathered = [{} for _ in range(depth)]
        self.received = [{} for _ in range(depth)]
        self.pending = []
        self.deferred = None

    def gather_first(self):
        ride = _Ride([self.shards[n] for n in self.FIRST], gather=True, index=[0] * len(self.FIRST))
        self.gathered[0].update(zip(self.FIRST, _exchange("gather_first", ride)))

    def gather_ride(self, layer, kernel_name):
        own = self.OWN.get(kernel_name, []) if layer == 0 else []
        nxt = (self.NEXT if layer == 0 else self.FWD).get(kernel_name, []) if layer + 1 < self.depth else []
        items = [(n, 0) for n in own] + [(n, layer + 1) for n in nxt]
        if not items:
            return None
        ride = _Ride([self.shards[n] for n, _ in items], gather=True, index=[l for _, l in items])
        self.pending.append((ride, [(n, self.gathered[l]) for n, l in items]))
        return ride

    def _scatter(self, arrays, names, layer):
        ride = _Ride(arrays, gather=False)
        self.pending.append((ride, [(n, self.received[layer]) for n in names]))
        return ride

    def scatter_ride(self, layer, kernel_name, grads):
        if kernel_name == 'z_proj_dx':
            whole = _by_destination('w_in', grads)
            half = whole.shape[1] // 2
            self.deferred = (whole[:, half:], layer)
            return self._scatter([whole[:, :half]], ['w_in_a'], layer)
        if kernel_name == 'ffn_in_dx':
            if self.deferred is None:
                return None
            (late, from_layer), self.deferred = self.deferred, None
            return self._scatter([late], ['w_in_b'], from_layer)
        names = self.BWD[kernel_name]
        return self._scatter([_by_destination(n, grads) for n in names], names, layer)

    def collect(self):
        for ride, places in self.pending:
            if ride.result is not None:
                for (name, dst), res in zip(places, ride.result):
                    dst[name] = res
        self.pending = [(ride, places) for ride, places in self.pending if ride.result is None]


class _LayerWeights:
    SOURCE = {'w_in7': 'w_in', 'w_inf': 'w_in', 'b_merge0': 'b_merge', 'b_merge1': 'b_merge'}

    def __init__(self, sched, layer, replicated):
        self.sched, self.layer, self.made = sched, layer, dict(replicated)

    def __getitem__(self, key):
        if key not in self.made:
            self.sched.collect()
            src = self.SOURCE.get(key, key)
            self.made.update(_from_shards(src, self.sched.gathered[self.layer][src]))
        return self.made[key]


def _local_step(x2, tgt, p3, weights_of, depth, g_in, b_in, bsz, seq, sched=None):
    h, hb = _rowwise("ln_in_fwd", _both(_ln), [x2], [g_in, b_in], TWICE, [])
    p3 = p3.astype(BF16)
    saved, layer_w = [], []
    for l in range(depth):
        layer_w.append(weights_of(l))
        ride_of = functools.partial(sched.gather_ride, l) if sched else _no_ride
        h, hb, s = _layer_fwd(h, hb, p3[l], layer_w[l], bsz, seq, ride_of)
        if sched:
            sched.collect()
        saved.append(s)

    def loss_fn(y, tv):
        err = y - tv
        return err * (1.0 / D_MODEL), _colsum(jnp.square(err))

    dh, sq = _rowwise("loss", loss_fn, [h, tgt], [], [(D_MODEL, F32)], [D_MODEL])
    grads = [None] * depth
    for l in reversed(range(depth)):
        ride_of = functools.partial(sched.scatter_ride, l) if sched else _no_ride
        dh, grads[l] = _layer_bwd(dh, p3[l], layer_w[l], saved[l], bsz, seq, ride_of)
        if sched:
            sched.collect()
    dx, dg_in, db_in = _vjp_rowwise("ln_in_bwd", _ln, [x2], [g_in, b_in], [dh], 1)
    return sq, dx, grads, dg_in, db_in


def _from_shards(name, g):
    if name == 'w_in':
        wt = g.transpose(1, 0, 2).reshape(D_MODEL, N_IN)
        return {'w_in7': jnp.concatenate([wt[:, :3 * D_MODEL], wt[:, 3 * D_MODEL + N_HEADS:]], axis=1),
                'w_inf': jnp.pad(wt[:, 3 * D_MODEL:3 * D_MODEL + N_HEADS], ((0, 0), (0, BLK - N_HEADS)))}
    if name in ('w_branch_att', 'w_branch_rnn', 'w_out', 'w_ple_gate'):
        return {name: g.reshape(D_MODEL, D_MODEL)}
    if name == 'w_ffn_in':
        return {name: g}
    if name == 'w_ffn_out':
        return {name: g.reshape(N_FF, FF_SH, D_MODEL)}
    if name == 'b_merge':
        bm = g.transpose(1, 0, 2).reshape(2, D_MODEL)
        return {'b_merge0': bm[0:1], 'b_merge1': bm[1:2]}
    return {name: g.transpose(1, 0, 2).reshape(g.shape[1], D_MODEL)}


def _layer_weights(full):
    w = {}
    for name, g in full.items():
        w.update(_from_shards(name, g))
    return w


def _by_destination(name, gw):
    if name == 'w_in':
        g7, gf = gw['w_in7'], gw['w_inf']
        true = jnp.concatenate([g7[:, :3 * D_MODEL], gf[:, :N_HEADS], g7[:, 3 * D_MODEL:]], axis=1)
        return true.reshape(D_MODEL, N_DEV, IN_SH).transpose(1, 0, 2)
    g = gw[name]
    if name in ('w_branch_att', 'w_branch_rnn', 'w_out', 'w_ple_gate'):
        return g.reshape(N_DEV, D_MODEL // N_DEV, D_MODEL)
    if name == 'w_ffn_in':
        return g
    if name == 'w_ffn_out':
        return g.reshape(N_DEV, N_FF * FF_SH // N_DEV, D_MODEL)
    return g.reshape(g.shape[0], N_DEV, BLK).transpose(1, 0, 2)


def kernel(x, p, ln_in_g, ln_in_b, w_in, b_forget, conv_w, conv_b, rg_w_a, rg_b_a, rg_w_x, rg_b_x, rg_lambda, w_branch_att, w_branch_rnn, b_merge, w_out, ln_mix_g, ln_mix_b, w_ffn_in, w_ffn_out, ln_ffn_g, ln_ffn_b, w_ple, w_ple_gate, b_ple_gate, ln_ple_g, ln_ple_b, loss_target, m_ln_in_g, m_ln_in_b, m_w_in, m_b_forget, m_conv_w, m_conv_b, m_rg_w_a, m_rg_b_a, m_rg_w_x, m_rg_b_x, m_rg_lambda, m_w_branch_att, m_w_branch_rnn, m_b_merge, m_w_out, m_ln_mix_g, m_ln_mix_b, m_w_ffn_in, m_w_ffn_out, m_ln_ffn_g, m_ln_ffn_b, m_w_ple, m_w_ple_gate, m_b_ple_gate, m_ln_ple_g, m_ln_ple_b, v_ln_in_g, v_ln_in_b, v_w_in, v_b_forget, v_conv_w, v_conv_b, v_rg_w_a, v_rg_b_a, v_rg_w_x, v_rg_b_x, v_rg_lambda, v_w_branch_att, v_w_branch_rnn, v_b_merge, v_w_out, v_ln_mix_g, v_ln_mix_b, v_w_ffn_in, v_w_ffn_out, v_ln_ffn_g, v_ln_ffn_b, v_w_ple, v_w_ple_gate, v_b_ple_gate, v_ln_ple_g, v_ln_ple_b):
    env = dict(locals())
    wts = {n: env[n] for n in WEIGHTS}
    mom = {n: env['m_' + n] for n in WEIGHTS}
    var = {n: env['v_' + n] for n in WEIGHTS}
    bsz, seq, _ = x.shape
    depth = w_in.shape[0]
    t = bsz * seq
    x2, tgt = x.reshape(t, D_MODEL), loss_target.reshape(t, D_MODEL)
    p3 = p.reshape(depth, t, D_PLE)

    shard_names = SHARDED_BF16 + SHARDED_F32
    shards = {n: wts[n].astype(BF16) for n in SHARDED_BF16}
    shards.update({n: wts[n] for n in SHARDED_F32})
    sched = _Schedule(shards, depth)
    sched.gather_first()

    def weights_of(l):
        w = {n: _row(wts[n][l]) for n in ['conv_b', 'rg_b_a', 'rg_b_x', 'rg_lambda', 'ln_mix_g', 'ln_mix_b',
                                          'ln_ffn_g', 'ln_ffn_b', 'b_ple_gate', 'ln_ple_g', 'ln_ple_b']}
        w['b_forget'] = jnp.pad(_row(b_forget[l]), ((0, 0), (0, BLK - N_HEADS)))
        w['rg_w_a'], w['rg_w_x'] = rg_w_a[l], rg_w_x[l]
        return _LayerWeights(sched, l, w)

    g_in, b_in = _row(ln_in_g), _row(ln_in_b)
    sq, dx, grads, dg_in, db_in = _local_step(x2, tgt, p3, weights_of, depth, g_in, b_in, bsz, seq, sched)
    loss = lax.psum(0.5 * jnp.sum(sq) / D_MODEL, ("x", "y", "c"))
    grad_x = dx.reshape(bsz, seq, D_MODEL)

    out = {}

    def update(n, ride=None):
        shp = wts[n].shape
        view = lambda a: a
        if n == 'w_in':
            recv = [sched.received[l][half] for l in range(depth) for half in ('w_in_a', 'w_in_b')]
        else:
            recv = [sched.received[l][n] for l in range(depth)]
        if n in SHARDED_F32:
            recv = [jnp.stack(recv, axis=1).reshape(N_DEV, -1, shp[-1])]
            view = lambda a: a.reshape(1, -1, shp[-1])
        res = _adamw("adamw_" + n, recv, view(wts[n]), view(mom[n]), view(var[n]), ride=ride)
        out[n] = [r.reshape(shp) for r in res]

    def rep_grad(n):
        if n == 'ln_in_g':
            return dg_in.reshape(-1)
        if n == 'ln_in_b':
            return db_in.reshape(-1)
        return jnp.stack([grads[l][n].reshape(wts[n].shape[1:]) if n != 'b_forget'
                          else grads[l][n][0, :N_HEADS] for l in range(depth)]).reshape(-1)

    sizes = [int(wts[n].size) for n in REPLICATED]
    n_rows = [8 * (-(-sz // (8 * BLK))) for sz in sizes]
    total_rows = -(-sum(n_rows) // (N_DEV * 8)) * (N_DEV * 8)

    def as_rows(v, sz, nr):
        v = v.reshape(-1)
        return (jnp.pad(v, (0, nr * BLK - sz)) if nr * BLK != sz else v).reshape(nr, BLK)

    def pack(vals):
        parts = [as_rows(v, sz, nr) for v, sz, nr in zip(vals, sizes, n_rows)]
        parts.append(jnp.zeros((total_rows - sum(n_rows), BLK), F32))
        return jnp.concatenate(parts, axis=0)

    late, from_layer = sched.deferred
    last_w_in = _Ride([late], gather=False)
    update('w_ffn_in', ride=last_w_in)
    sched.received[from_layer]['w_in_b'], = last_w_in.result
    scatter_small = _Ride([pack([rep_grad(n) for n in REPLICATED]).reshape(N_DEV, total_rows // N_DEV, BLK)],
                          gather=False)
    update('w_branch_att', ride=scatter_small)
    gather_small = _Ride([_sum_parts("sum_small", scatter_small.result[0])], gather=True)
    update('w_out', ride=gather_small)
    for n in shard_names:
        if n not in out:
            update(n)
    g_rows = gather_small.result[0].reshape(total_rows, BLK)
    starts = [sum(n_rows[:i]) for i in range(len(n_rows))]
    for n, r0, sz, nr in zip(REPLICATED, starts, sizes, n_rows):
        shp = wts[n].shape
        as_one = (1, 1, sz) if len(shp) == 1 else (1, -1, shp[-1])
        g_n = g_rows[r0:r0 + nr]
        g_n = (g_n if nr * BLK == sz else g_n.reshape(-1)[:sz]).reshape(as_one)
        res = _adamw("adamw_" + n, [g_n], *[d[n].reshape(as_one) for d in (wts, mom, var)])
        out[n] = [r.reshape(shp) for r in res]

    return (loss, grad_x, *[out[n][k] for k in range(4) for n in WEIGHTS])
```

```python
import functools
import math

import jax
import jax.numpy as jnp
from jax import lax
from jax.experimental import pallas as pl
from jax.experimental.pallas import tpu as pltpu

F32 = jnp.float32
BF16 = jnp.bfloat16

N_DEV = 8
D_MODEL = 1024
N_HEADS = 8
HEAD_DIM = 128
N_BLK = 8
BLK = 128
CONV_W = 4
D_PLE = 256
FF_SH = 704
N_FF = 4
IN_SH = 897
N_IN = 7176
DEPTH = 4
RG_C = 8.0
ALPHA = float((2 * DEPTH) ** 0.25)
LN_EPS = 1e-5
SCALE = 1.0 / math.sqrt(HEAD_DIM)
NEG = -1e30
ADAM_LR, ADAM_B1, ADAM_B2, ADAM_EPS, ADAM_WD, ADAM_STEP = 0.001, 0.9, 0.999, 1e-08, 0.01, 10
QKV = 3 * D_MODEL
OFF_RX, OFF_RY, OFF_GA, OFF_GB = (i * D_MODEL for i in range(4))
V7X_VMEM_LIMIT = 48 * 1024 * 1024

WEIGHTS = ['ln_in_g', 'ln_in_b', 'w_in', 'b_forget', 'conv_w', 'conv_b', 'rg_w_a', 'rg_b_a', 'rg_w_x', 'rg_b_x',
           'rg_lambda', 'w_branch_att', 'w_branch_rnn', 'b_merge', 'w_out', 'ln_mix_g', 'ln_mix_b', 'w_ffn_in',
           'w_ffn_out', 'ln_ffn_g', 'ln_ffn_b', 'w_ple', 'w_ple_gate', 'b_ple_gate', 'ln_ple_g', 'ln_ple_b']
SHARDED_BF16 = ['w_in', 'w_branch_att', 'w_branch_rnn', 'w_out', 'w_ffn_in', 'w_ffn_out', 'w_ple', 'w_ple_gate']
SHARDED_F32 = ['conv_w', 'b_merge']
REPLICATED = [n for n in WEIGHTS if n not in SHARDED_BF16 and n not in SHARDED_F32]

NN = ((1,), (0,))
NT = ((1,), (1,))
TN = ((0,), (0,))


class _Ride:
    def __init__(self, arrays, *, gather, index=None):
        self.arrays, self.gather, self.index = list(arrays), gather, index
        self.result = None

    def _shard(self, ins, a):
        return ins[a] if self.index is None else ins[a].at[self.index[a]]

    def out_shapes(self):
        if not self.gather:
            return [jax.ShapeDtypeStruct(a.shape, a.dtype) for a in self.arrays]
        cut = 0 if self.index is None else 1
        return [jax.ShapeDtypeStruct((N_DEV,) + a.shape[cut:], a.dtype) for a in self.arrays]

    def scratch(self):
        n = len(self.arrays)
        return [pltpu.SemaphoreType.DMA((n * N_DEV,)), pltpu.SemaphoreType.DMA((n * N_DEV,)),
                pltpu.SemaphoreType.DMA((n,))]

    def _copy(self, a, k, src, dst, sems, to=None):
        send_sems, recv_sems, _ = sems
        return pltpu.make_async_remote_copy(
            src_ref=src, dst_ref=dst, send_sem=send_sems.at[a * N_DEV + k], recv_sem=recv_sems.at[a * N_DEV + k],
            device_id=_peer(k if to is None else to), device_id_type=pl.DeviceIdType.MESH)

    def begin(self, ins, outs, sems):
        me = _my_id()
        started = []
        for a in range(len(ins)):
            if self.gather:
                src = self._shard(ins, a)
                started.append(pltpu.make_async_copy(src, outs[a].at[me], sems[2].at[a]))
                started += [self._copy(a, k, src, outs[a].at[me], sems) for k in (1, 2, 4, 6)]
            else:
                started.append(pltpu.make_async_copy(ins[a].at[me], outs[a].at[me], sems[2].at[a]))
                started += [self._copy(a, k, ins[a].at[me ^ k], outs[a].at[me], sems) for k in range(1, N_DEV)]
        for cp in started:
            cp.start()

    def finish(self, ins, outs, sems):
        me = _my_id()
        for a in range(len(ins)):
            if self.gather:
                src = self._shard(ins, a)
                passed = []
                for k in (2, 4, 6):
                    block = outs[a].at[me ^ k]
                    self._copy(a, k, src, block, sems).wait_recv()
                    passed.append(self._copy(a, k + 1, block, block, sems, to=1))
                    passed[-1].start()
                for k in (1, 2, 4, 6):
                    self._copy(a, k, src, outs[a].at[me], sems).wait_send()
                self._copy(a, 1, src, outs[a].at[me ^ 1], sems).wait_recv()
                for cp in passed:
                    cp.wait()
                pltpu.make_async_copy(src, outs[a].at[me], sems[2].at[a]).wait()
            else:
                pltpu.make_async_copy(ins[a].at[me], outs[a].at[me], sems[2].at[a]).wait()
                for k in range(1, N_DEV):
                    self._copy(a, k, ins[a].at[me ^ k], outs[a].at[me], sems).wait()


def _pcall(body, ride=None, **kw):
    if ride is None:
        return pl.pallas_call(body, **kw)
    n = len(ride.arrays)
    grid = kw['grid']
    single = not isinstance(kw['out_shape'], (list, tuple))
    out_specs = [kw['out_specs']] if single else list(kw['out_specs'])
    out_shape = [kw['out_shape']] if single else list(kw['out_shape'])
    in_specs = list(kw['in_specs'])
    scratch = list(kw.get('scratch_shapes', ()))
    n_in, n_out, n_sc = len(in_specs), len(out_shape), len(scratch)
    hbm = pl.BlockSpec(memory_space=pltpu.HBM)

    def wrapped(*refs):
        ins, xin = refs[:n_in], refs[n_in:n_in + n]
        outs, xout = refs[n_in + n:n_in + n + n_out], refs[n_in + n + n_out:n_in + 2 * n + n_out]
        sc, sems = refs[n_in + 2 * n + n_out:n_in + 2 * n + n_out + n_sc], refs[-3:]
        ids = [pl.program_id(ax) for ax in range(len(grid))]
        first = functools.reduce(jnp.logical_and, [i == 0 for i in ids])
        last = functools.reduce(jnp.logical_and, [i == g - 1 for i, g in zip(ids, grid)])

        pl.when(first)(lambda: ride.begin(xin, xout, sems))
        body(*ins, *outs, *sc)
        pl.when(last)(lambda: ride.finish(xin, xout, sems))

    call = pl.pallas_call(wrapped, name=kw['name'], grid=grid, in_specs=in_specs + [hbm] * n,
                          out_specs=out_specs + [hbm] * n, out_shape=out_shape + ride.out_shapes(),
                          scratch_shapes=scratch + ride.scratch(), compiler_params=kw['compiler_params'])

    def run(*args):
        res = call(*args, *ride.arrays)
        ride.result = list(res[n_out:])
        return res[0] if single else list(res[:n_out])

    return run


def _tile(n, pref, mult=8):
    if n <= pref:
        return n
    t = (pref // mult) * mult
    while t >= mult:
        if n % t == 0:
            return t
        t -= mult
    return n


def _cparams(sem):
    return pltpu.CompilerParams(dimension_semantics=sem, vmem_limit_bytes=V7X_VMEM_LIMIT)


def _mm(name, a, b, *, grid, a_spec, b_spec, o_spec, out_shape, contract, ride=None, add=None):
    nk = grid[-1]
    in_out = out_shape.dtype == F32
    acc_shape = tuple(d for d in o_spec.block_shape if d is not None)

    def body(*refs):
        a_ref, b_ref = refs[0], refs[1]
        add_ref = refs[2] if add is not None else None
        o_ref = refs[3] if add is not None else refs[2]
        acc_ref = o_ref if (in_out or nk == 1) else refs[-1]
        k = pl.program_id(len(grid) - 1)
        part = lax.dot_general(a_ref[...].astype(BF16), b_ref[...].astype(BF16), (contract, ((), ())),
                               preferred_element_type=F32)
        if add_ref is not None:
            part = jnp.where(k == 0, part + add_ref[...], part) if nk > 1 else part + add_ref[...]
        if nk == 1:
            o_ref[...] = part.astype(o_ref.dtype)
            return

        @pl.when(k == 0)
        def _():
            acc_ref[...] = part

        @pl.when(k > 0)
        def _():
            acc_ref[...] += part

        if not in_out:
            @pl.when(k == nk - 1)
            def _():
                o_ref[...] = acc_ref[...].astype(o_ref.dtype)

    sem = ("parallel",) * (len(grid) - 1) + ("arbitrary",)
    scratch = [] if (in_out or nk == 1) else [pltpu.VMEM(acc_shape, F32)]
    in_specs, args = [a_spec, b_spec], [a, b]
    if add is not None:
        in_specs.append(o_spec)
        args.append(add)
    return _pcall(body, ride=ride, name=name, grid=grid, in_specs=in_specs, out_specs=o_spec,
                  out_shape=out_shape, scratch_shapes=scratch, compiler_params=_cparams(sem))(*args)


def _mm_nn(name, a, b, *, b_off=0, n=None, out_dtype=F32, tm=1024, tn=1024, tk=1024, ride=None):
    m, k = a.shape
    n = b.shape[1] if n is None else n
    tm, tn, tk = _tile(m, tm), _tile(n, tn, 128), _tile(k, tk, 128)
    no = b_off // tn
    return _mm(name, a, b, grid=(m // tm, n // tn, k // tk),
               a_spec=pl.BlockSpec((tm, tk), lambda i, j, kk: (i, kk)),
               b_spec=pl.BlockSpec((tk, tn), lambda i, j, kk: (kk, j + no)),
               o_spec=pl.BlockSpec((tm, tn), lambda i, j, kk: (i, j)),
               out_shape=jax.ShapeDtypeStruct((m, n), out_dtype), contract=NN, ride=ride)


def _mm_nt(name, a, b, *, out_dtype=F32, tm=1024, tn=1024, tk=1024, ride=None, add=None):
    m, k = a.shape
    n = b.shape[0]
    tm, tn, tk = _tile(m, tm), _tile(n, tn, 128), _tile(k, tk, 128)
    return _mm(name, a, b, grid=(m // tm, n // tn, k // tk),
               a_spec=pl.BlockSpec((tm, tk), lambda i, j, kk: (i, kk)),
               b_spec=pl.BlockSpec((tn, tk), lambda i, j, kk: (j, kk)),
               o_spec=pl.BlockSpec((tm, tn), lambda i, j, kk: (i, j)),
               out_shape=jax.ShapeDtypeStruct((m, n), out_dtype), contract=NT, ride=ride, add=add)


def _mm_tn(name, a, b, *, a_off=0, m=None, out_dtype=F32, tm=1024, tn=1024, tk=2048, ride=None):
    t, n = b.shape
    m = a.shape[1] if m is None else m
    tm, tn, tk = _tile(m, tm, 128), _tile(n, tn, 128), _tile(t, tk)
    mo = a_off // tm
    return _mm(name, a, b, grid=(m // tm, n // tn, t // tk),
               a_spec=pl.BlockSpec((tk, tm), lambda i, j, kk: (kk, i + mo)),
               b_spec=pl.BlockSpec((tk, tn), lambda i, j, kk: (kk, j)),
               o_spec=pl.BlockSpec((tm, tn), lambda i, j, kk: (i, j)),
               out_shape=jax.ShapeDtypeStruct((m, n), out_dtype), contract=TN, ride=ride)


def _rowwise(name, fn, rows, params, out_rows, out_reds, tm=512):
    rows = [r if isinstance(r, tuple) else (r, 0, r.shape[1]) for r in rows]
    t = rows[0][0].shape[0]
    tm = _tile(t, tm)
    in_specs = []
    for _, off, w in rows:
        in_specs.append(pl.BlockSpec((tm, w), functools.partial(lambda i, cb: (i, cb), cb=off // w)))
    for p in params:
        in_specs.append(pl.BlockSpec((1, p.shape[1]), lambda i: (0, 0)))
    out_specs = [pl.BlockSpec((tm, w), lambda i: (i, 0)) for w, _ in out_rows]
    out_specs += [pl.BlockSpec((1, w), lambda i: (0, 0)) for w in out_reds]
    out_shape = [jax.ShapeDtypeStruct((t, w), dt) for w, dt in out_rows]
    out_shape += [jax.ShapeDtypeStruct((1, w), F32) for w in out_reds]
    nr, npar, nor = len(rows), len(params), len(out_rows)

    def body(*refs):
        ins, outs = refs[:nr + npar], refs[nr + npar:]
        vals = [r[...].astype(F32) for r in ins[:nr]]
        vals += [jnp.broadcast_to(r[...], (tm, r.shape[1])) for r in ins[nr:]]
        res = fn(*vals)
        step = pl.program_id(0)
        for o, v in zip(outs[:nor], res[:nor]):
            o[...] = v.astype(o.dtype)
        for o, v in zip(outs[nor:], res[nor:]):
            _accumulate(o, v, step)

    res = _pcall(body, name=name, grid=(t // tm,), in_specs=in_specs, out_specs=out_specs, out_shape=out_shape,
                 compiler_params=_cparams(("arbitrary",)))(*[r[0] for r in rows], *params)
    return res


def _accumulate(o_ref, v, step):
    @pl.when(step == 0)
    def _():
        o_ref[...] = v

    @pl.when(step > 0)
    def _():
        o_ref[...] += v


def _colsum(v):
    return jnp.sum(v, axis=0, keepdims=True)


def _vjp_rowwise(name, fn, rows, params, cots, n_row_grads, tm=256, dtypes=None):
    nr, npar, nc = len(rows), len(params), len(cots)

    def bwd(*vals):
        prim, par, ct = vals[:nr], vals[nr + nc:], vals[nr:nr + nc]
        _, pull = jax.vjp(fn, *prim, *par)
        grads = pull(tuple(ct) if nc > 1 else ct[0])
        return tuple(grads[:n_row_grads]) + tuple(_colsum(g) for g in grads[nr:])

    dtypes = [F32] * n_row_grads if dtypes is None else dtypes
    widths = [(r[2] if isinstance(r, tuple) else r.shape[1], dt) for r, dt in zip(rows[:n_row_grads], dtypes)]
    return _rowwise(name, bwd, list(rows) + list(cots), params, widths, [p.shape[1] for p in params], tm=tm)


def _ln(s, g, b):
    mu = jnp.mean(s, axis=-1, keepdims=True)
    var = jnp.mean(jnp.square(s - mu), axis=-1, keepdims=True)
    return (s - mu) * lax.rsqrt(var + LN_EPS) * g + b


def _softplus(x):
    return jnp.maximum(x, 0.0) + jnp.log1p(jnp.exp(-jnp.abs(x)))


def _expm1(x):
    series = x * (1.0 + x * (1.0 / 2 + x * (1.0 / 6 + x * (1.0 / 24 + x * (1.0 / 120 + x * (1.0 / 720))))))
    return jnp.where(jnp.abs(x) < 0.25, series, jnp.exp(x) - 1.0)


def _f_resid_ln(h, branch, g, b):
    return _ln(ALPHA * h + branch, g, b)


def _f_ple(h, gp, pe, bpg, g, b):
    return _ln(ALPHA * h + jax.nn.sigmoid(gp + bpg) * pe, g, b)


def _f_merge(ga, gb, ya, yb, bm0, bm1):
    return jax.nn.sigmoid(ga + bm0) * ya + jax.nn.sigmoid(gb + bm1) * yb


def _f_rnn_out(hs, ry):
    return hs * jax.nn.gelu(ry, approximate=True)


def _f_logf(fl, bf):
    return -_softplus(-(fl + bf))


def _f_decay(lam):
    return -RG_C * _softplus(-lam)


def _f_gate(xc, ra, ia, decay, ba, bx):
    r = jax.nn.sigmoid(ra + ba)
    i = jax.nn.sigmoid(ia + bx)
    log_a = decay * r
    a = jnp.exp(log_a)
    mult = jnp.sqrt(-_expm1(2.0 * log_a))
    return a, mult * (i * xc)


def _f_act(hg, hu):
    return jax.nn.silu(hg) * hu


ATT_BLOCK = 512
ATT_HEADS_PER_STEP = 1
SCAN_ROWS = 128


def _scores(q, k, cq, ck, diagonal):
    s = lax.dot_general(q, k, (NT, ((), ())), preferred_element_type=F32) * SCALE
    s = s + cq - ck
    if diagonal:
        row = lax.broadcasted_iota(jnp.int32, s.shape, 0)
        col = lax.broadcasted_iota(jnp.int32, s.shape, 1)
        s = jnp.where(col <= row, s, NEG)
    return s


def _dscores(p, do, o, v):
    dob = do.astype(BF16)
    delta = jnp.sum(dob.astype(F32) * o, axis=1, keepdims=True)
    dp = lax.dot_general(dob, v.astype(BF16), (NT, ((), ())), preferred_element_type=F32)
    return p * (dp - delta)


def _attn_fwd(z, cq, ck, bsz, seq, ride=None):
    t = bsz * seq
    tq = _tile(seq, ATT_BLOCK)
    nq = seq // tq

    hp = ATT_HEADS_PER_STEP

    def body(q_ref, k_ref, v_ref, cq_ref, ck_ref, o_ref, ob_ref, lse_ref):
        for hh, i in [(hh, i) for hh in range(hp) for i in range(nq)]:
            lanes = slice(hh * HEAD_DIM, (hh + 1) * HEAD_DIM)
            rows = slice(i * tq, (i + 1) * tq)
            q = q_ref[rows, lanes].astype(BF16)
            cqi = cq_ref[hh, rows, :]

            def step(j, carry, diagonal, q=q, cqi=cqi, hh=hh, lanes=lanes):
                m, l, acc = carry
                keys = slice(j * tq, (j + 1) * tq)
                s = _scores(q, k_ref[keys, lanes].astype(BF16), cqi, ck_ref[hh, j:j + 1, :], diagonal)
                m_new = jnp.maximum(m, jnp.max(s, axis=1, keepdims=True))
                alpha = jnp.exp(m - m_new)
                p = jnp.exp(s - m_new)
                p_hi = p.astype(BF16)
                p_lo = (p - p_hi.astype(F32)).astype(BF16)
                vb = v_ref[keys, lanes].astype(BF16)
                pv = lax.dot_general(p_hi, vb, (NN, ((), ())), preferred_element_type=F32)
                pv = pv + lax.dot_general(p_lo, vb, (NN, ((), ())), preferred_element_type=F32)
                return m_new, alpha * l + jnp.sum(p, axis=1, keepdims=True), alpha * acc + pv

            carry = (jnp.full((tq, 1), NEG, F32), jnp.zeros((tq, 1), F32), jnp.zeros((tq, HEAD_DIM), F32))
            for j in range(i):
                carry = step(j, carry, False)
            m, l, acc = step(i, carry, True)
            o = acc / l
            o_ref[rows, lanes] = o
            ob_ref[rows, lanes] = o.astype(BF16)
            lse_ref[hh, rows, :] = m + jnp.log(l)

    groups = N_HEADS // hp
    head = (seq, hp * HEAD_DIM)
    in_specs = [
        pl.BlockSpec(head, lambda b, g: (b, g)),
        pl.BlockSpec(head, lambda b, g: (b, groups + g)),
        pl.BlockSpec(head, lambda b, g: (b, 2 * groups + g)),
        pl.BlockSpec((None, hp, seq, 1), lambda b, g: (b, g, 0, 0)),
        pl.BlockSpec((None, hp, nq, tq), lambda b, g: (b, g, 0, 0)),
    ]
    out_specs = [pl.BlockSpec(head, lambda b, g: (b, g)), pl.BlockSpec(head, lambda b, g: (b, g)),
                 pl.BlockSpec((None, hp, seq, 1), lambda b, g: (b, g, 0, 0))]
    out_shape = [jax.ShapeDtypeStruct((t, D_MODEL), F32), jax.ShapeDtypeStruct((t, D_MODEL), BF16),
                 jax.ShapeDtypeStruct((bsz, N_HEADS, seq, 1), F32)]
    return _pcall(body, ride=ride, name="attn_fwd", grid=(bsz, groups), in_specs=in_specs, out_specs=out_specs,
                  out_shape=out_shape, compiler_params=_cparams(("parallel", "parallel")))(
                      z, z, z, cq, ck.reshape(bsz, N_HEADS, nq, tq))


def _attn_bwd(z, att, datt, lse, cq, ck, bsz, seq, ride=None):
    t = bsz * seq
    tq = _tile(seq, ATT_BLOCK)
    nq = seq // tq

    def body(q_ref, k_ref, v_ref, o_ref, do_ref, lse_ref, cq_ref, ck_ref,
             dq_ref, dk_ref, dv_ref, dcq_ref, dck_ref, dq_sc):
        dq_sc[...] = jnp.zeros_like(dq_sc)
        dcq_ref[...] = jnp.zeros_like(dcq_ref)
        for j in range(nq):
            keys = slice(j * tq, (j + 1) * tq)
            kb = k_ref[keys, :].astype(BF16)
            vb = v_ref[keys, :].astype(BF16)
            ckj = ck_ref[j:j + 1, :]

            def step(i, carry, diagonal, kb=kb, vb=vb, ckj=ckj):
                dk, dv, dc = carry
                rows = slice(i * tq, (i + 1) * tq)
                qb = q_ref[rows, :].astype(BF16)
                do = do_ref[rows, :]
                s = _scores(qb, kb, cq_ref[rows, :], ckj, diagonal)
                p = jnp.exp(s - lse_ref[rows, :])
                ds = _dscores(p, do, o_ref[rows, :], vb)
                dsb = (ds * SCALE).astype(BF16)
                dq_sc[rows, :] += lax.dot_general(dsb, kb, (NN, ((), ())), preferred_element_type=F32)
                dcq_ref[rows, :] += jnp.sum(ds, axis=1, keepdims=True)
                dv = dv + lax.dot_general(p.astype(BF16), do.astype(BF16), (TN, ((), ())),
                                          preferred_element_type=F32)
                dk = dk + lax.dot_general(dsb, qb, (TN, ((), ())), preferred_element_type=F32)
                return dk, dv, dc - jnp.sum(ds, axis=0, keepdims=True)

            zero = jnp.zeros((tq, HEAD_DIM), F32)
            carry = step(j, (zero, zero, jnp.zeros((1, tq), F32)), True)
            for i in range(j + 1, nq):
                carry = step(i, carry, False)
            dk, dv, dck_ref[j:j + 1, :] = carry
            dk_ref[keys, :] = dk.astype(BF16)
            dv_ref[keys, :] = dv.astype(BF16)
        dq_ref[...] = dq_sc[...].astype(BF16)

    head = (seq, HEAD_DIM)
    hmap = lambda b, h: (b, h)
    col = pl.BlockSpec((None, None, seq, 1), lambda b, h: (b, h, 0, 0))
    row = pl.BlockSpec((None, None, nq, tq), lambda b, h: (b, h, 0, 0))
    in_specs = [pl.BlockSpec(head, hmap),
                pl.BlockSpec(head, lambda b, h: (b, N_HEADS + h)),
                pl.BlockSpec(head, lambda b, h: (b, 2 * N_HEADS + h)),
                pl.BlockSpec(head, hmap), pl.BlockSpec(head, hmap), col, col, row]
    big = jax.ShapeDtypeStruct((t, D_MODEL), BF16)
    return _pcall(body, ride=ride, name="attn_bwd", grid=(bsz, N_HEADS), in_specs=in_specs,
                  out_specs=[pl.BlockSpec(head, hmap)] * 3 + [col, row],
                  out_shape=[big, big, big, jax.ShapeDtypeStruct((bsz, N_HEADS, seq, 1), F32),
                             jax.ShapeDtypeStruct((bsz, N_HEADS, nq, tq), F32)],
                  scratch_shapes=[pltpu.VMEM(head, F32)],
                  compiler_params=_cparams(("parallel", "parallel")))(
                      z, z, z, att, datt, lse, cq, ck.reshape(bsz, N_HEADS, nq, tq))


def _scan(name, a, u, bsz, seq, *, reverse, with_prev=False, tb=512, ride=None):
    c = u.shape[1]
    tb = _tile(seq, tb)
    nb = seq // tb
    rc = SCAN_ROWS if tb % SCAN_ROWS == 0 else tb
    has_a = a is not None

    def body(*refs):
        if has_a:
            a_ref, u_ref = refs[0], refs[1]
            rest = refs[2:]
        else:
            u_ref = refs[0]
            rest = refs[1:]
        outs = rest[:2] if with_prev else rest[:1]
        carry_sc, afirst_sc = rest[-2], rest[-1]
        step = pl.program_id(1)

        @pl.when(step == 0)
        def _():
            carry_sc[...] = jnp.zeros_like(carry_sc)
            afirst_sc[...] = jnp.zeros_like(afirst_sc)

        row = lax.broadcasted_iota(jnp.int32, (rc, BLK), 0)
        pieces = list(range(tb // rc))
        for ls in range(c // BLK):
            lanes = slice(ls * BLK, (ls + 1) * BLK)
            carry = carry_sc[:, lanes]
            afirst = afirst_sc[:, lanes]
            for pc in (reversed(pieces) if reverse else pieces):
                rows = slice(pc * rc, (pc + 1) * rc)
                uu = u_ref[rows, lanes]
                if has_a:
                    aa = a_ref[rows, lanes]
                    coef = jnp.where(row < rc - 1, pltpu.roll(aa, rc - 1, 0), afirst) if reverse else aa
                k = 1
                while k < rc:
                    shift = rc - k if reverse else k
                    keep = (row < rc - k) if reverse else (row >= k)
                    uu_sh = jnp.where(keep, pltpu.roll(uu, shift, 0), 0.0)
                    if has_a:
                        uu = coef * uu_sh + uu
                        coef = coef * jnp.where(keep, pltpu.roll(coef, shift, 0), 1.0)
                    else:
                        uu = uu + uu_sh
                    k *= 2
                h = uu + coef * carry if has_a else uu + carry
                outs[0][rows, lanes] = h
                if with_prev:
                    outs[1][rows, lanes] = jnp.where(row >= 1, pltpu.roll(h, 1, 0), carry)
                edge = pc * rc if reverse else (pc + 1) * rc - 1
                carry = outs[0][edge:edge + 1, lanes]
                if has_a and reverse:
                    afirst = a_ref[edge:edge + 1, lanes]
            carry_sc[:, lanes] = carry
            if has_a and reverse:
                afirst_sc[:, lanes] = afirst

    if reverse:
        imap = lambda b, s: (b * nb + nb - 1 - s, 0)
    else:
        imap = lambda b, s: (b * nb + s, 0)
    spec = pl.BlockSpec((tb, c), imap)
    n_in = 2 if has_a else 1
    n_out = 2 if with_prev else 1
    res = _pcall(body, ride=ride, name=name, grid=(bsz, nb), in_specs=[spec] * n_in, out_specs=[spec] * n_out,
                 out_shape=[jax.ShapeDtypeStruct(u.shape, F32)] * n_out,
                 scratch_shapes=[pltpu.VMEM((1, c), F32), pltpu.VMEM((1, c), F32)],
                 compiler_params=_cparams(("parallel", "arbitrary")))(*([a, u] if has_a else [u]))
    return res if with_prev else res[0]


def _conv_fwd(z, w, b, bsz, seq, tb=512):
    c = D_MODEL
    t = bsz * seq
    tb = _tile(seq, tb)
    nb = seq // tb

    def body(x_ref, w_ref, b_ref, o_ref, tail_sc):
        step = pl.program_id(1)

        @pl.when(step == 0)
        def _():
            tail_sc[...] = jnp.zeros_like(tail_sc)

        x = x_ref[...]
        row8 = lax.broadcasted_iota(jnp.int32, (8, c), 0)
        tail = tail_sc[...]
        acc = w_ref[CONV_W - 1:CONV_W, :] * x + b_ref[...]
        for sh in range(1, CONV_W):
            xs = pltpu.roll(x, sh, 0)
            top = jnp.where(row8 < sh, pltpu.roll(tail, sh, 0), xs[0:8, :])
            xs = jnp.concatenate([top, xs[8:, :]], axis=0) if tb > 8 else top
            acc = acc + w_ref[CONV_W - 1 - sh:CONV_W - sh, :] * xs
        o_ref[...] = acc
        tail_sc[...] = x_ref[tb - 8:tb, :]

    return _pcall(body, name="conv_fwd", grid=(bsz, nb),
                  in_specs=[pl.BlockSpec((tb, c), lambda bb, s: (bb * nb + s, OFF_RX // c)),
                            pl.BlockSpec((CONV_W, c), lambda bb, s: (0, 0)),
                            pl.BlockSpec((1, c), lambda bb, s: (0, 0))],
                  out_specs=pl.BlockSpec((tb, c), lambda bb, s: (bb * nb + s, 0)),
                  out_shape=jax.ShapeDtypeStruct((t, c), F32),
                  scratch_shapes=[pltpu.VMEM((8, c), F32)],
                  compiler_params=_cparams(("parallel", "arbitrary")))(z, w, b)


def _conv_bwd(z, dxc, w, bsz, seq, tb=512):
    c = D_MODEL
    t = bsz * seq
    tb = _tile(seq, tb)
    nb = seq // tb

    def body(x_ref, g_ref, w_ref, dx_ref, dw_ref, db_ref, head_sc):
        bb, step = pl.program_id(0), pl.program_id(1)

        @pl.when(step == 0)
        def _():
            head_sc[...] = jnp.zeros_like(head_sc)

        x, g = x_ref[...], g_ref[...]
        row8 = lax.broadcasted_iota(jnp.int32, (8, c), 0)
        head = head_sc[...]
        dx = w_ref[CONV_W - 1:CONV_W, :] * g
        dws = [None] * CONV_W
        dws[CONV_W - 1] = _colsum(g * x)
        for sh in range(1, CONV_W):
            gs = pltpu.roll(g, tb - sh, 0)
            bot = jnp.where(row8 >= 8 - sh, pltpu.roll(head, 8 - sh, 0), gs[tb - 8:tb, :])
            gs = jnp.concatenate([gs[:tb - 8, :], bot], axis=0) if tb > 8 else bot
            dx = dx + w_ref[CONV_W - 1 - sh:CONV_W - sh, :] * gs
            dws[CONV_W - 1 - sh] = _colsum(gs * x)
        dx_ref[...] = dx.astype(dx_ref.dtype)
        first = (bb == 0) & (step == 0)
        dw = jnp.concatenate(dws, axis=0)
        db = _colsum(g)

        @pl.when(first)
        def _():
            dw_ref[...] = dw
            db_ref[...] = db

        @pl.when(jnp.logical_not(first))
        def _():
            dw_ref[...] += dw
            db_ref[...] += db

        head_sc[...] = g_ref[0:8, :]

    rmap = lambda bb, s: (bb * nb + nb - 1 - s, 0)
    return _pcall(body, name="conv_bwd", grid=(bsz, nb),
                  in_specs=[pl.BlockSpec((tb, c), lambda bb, s: (bb * nb + nb - 1 - s, OFF_RX // c)),
                            pl.BlockSpec((tb, c), rmap),
                            pl.BlockSpec((CONV_W, c), lambda bb, s: (0, 0))],
                  out_specs=[pl.BlockSpec((tb, c), rmap),
                             pl.BlockSpec((CONV_W, c), lambda bb, s: (0, 0)),
                             pl.BlockSpec((1, c), lambda bb, s: (0, 0))],
                  out_shape=[jax.ShapeDtypeStruct((t, c), BF16), jax.ShapeDtypeStruct((CONV_W, c), F32),
                             jax.ShapeDtypeStruct((1, c), F32)],
                  scratch_shapes=[pltpu.VMEM((8, c), F32)],
                  compiler_params=_cparams(("arbitrary", "arbitrary")))(z, dxc, w)


def _gate_fwd(xc, w_a, w_x, b_a, b_x, lam, tm=1024, ride=None):
    t = xc.shape[0]
    tm = _tile(t, tm)

    def body(xc_ref, wa_ref, wx_ref, ba_ref, bx_ref, lam_ref, a_ref, u_ref):
        xc_b = xc_ref[...]
        xb = xc_b.astype(BF16)
        ra = lax.dot_general(xb, wa_ref[...].astype(BF16), (NN, ((), ())), preferred_element_type=F32)
        ia = lax.dot_general(xb, wx_ref[...].astype(BF16), (NN, ((), ())), preferred_element_type=F32)
        a, u = _f_gate(xc_b, ra, ia, lam_ref[...], ba_ref[...], bx_ref[...])
        a_ref[...] = a
        u_ref[...] = u

    row = pl.BlockSpec((tm, BLK), lambda n, i: (i, n))
    wsp = pl.BlockSpec((None, BLK, BLK), lambda n, i: (n, 0, 0))
    vec = pl.BlockSpec((1, BLK), lambda n, i: (0, n))
    return _pcall(body, ride=ride, name="gate_fwd", grid=(N_BLK, t // tm), in_specs=[row, wsp, wsp, vec, vec, vec],
                  out_specs=[row, row], out_shape=[jax.ShapeDtypeStruct((t, D_MODEL), F32)] * 2,
                  compiler_params=_cparams(("parallel", "parallel")))(xc, w_a, w_x, b_a, b_x, lam)


def _gate_bwd(xc, w_a, w_x, b_a, b_x, lam, hprev, du, tm=1024, ride=None):
    t = xc.shape[0]
    tm = _tile(t, tm)

    def body(xc_ref, wa_ref, wx_ref, ba_ref, bx_ref, lam_ref, hp_ref, du_ref,
             dxc_ref, dwa_ref, dwx_ref, dba_ref, dbx_ref, dlam_ref):
        step = pl.program_id(1)
        xc_b = xc_ref[...]
        xb = xc_b.astype(BF16)
        wa, wx = wa_ref[...].astype(BF16), wx_ref[...].astype(BF16)
        ra = lax.dot_general(xb, wa, (NN, ((), ())), preferred_element_type=F32)
        ia = lax.dot_general(xb, wx, (NN, ((), ())), preferred_element_type=F32)
        full = lambda r: jnp.broadcast_to(r[...], (tm, BLK))
        _, pull = jax.vjp(_f_gate, xc_b, ra, ia, full(lam_ref), full(ba_ref), full(bx_ref))
        du_b = du_ref[...]
        dxc, dra, dia, dlam, dba, dbx = pull((du_b * hp_ref[...], du_b))
        drb, dib = dra.astype(BF16), dia.astype(BF16)
        dxc = dxc + lax.dot_general(drb, wa, (NT, ((), ())), preferred_element_type=F32)
        dxc = dxc + lax.dot_general(dib, wx, (NT, ((), ())), preferred_element_type=F32)
        dxc_ref[...] = dxc
        _accumulate(dwa_ref, lax.dot_general(xb, drb, (TN, ((), ())), preferred_element_type=F32), step)
        _accumulate(dwx_ref, lax.dot_general(xb, dib, (TN, ((), ())), preferred_element_type=F32), step)
        _accumulate(dba_ref, _colsum(dba), step)
        _accumulate(dbx_ref, _colsum(dbx), step)
        _accumulate(dlam_ref, _colsum(dlam), step)

    row = pl.BlockSpec((tm, BLK), lambda n, i: (i, n))
    wsp = pl.BlockSpec((None, BLK, BLK), lambda n, i: (n, 0, 0))
    vec = pl.BlockSpec((1, BLK), lambda n, i: (0, n))
    wshape = jax.ShapeDtypeStruct((N_BLK, BLK, BLK), F32)
    vshape = jax.ShapeDtypeStruct((1, D_MODEL), F32)
    return _pcall(body, ride=ride, name="gate_bwd", grid=(N_BLK, t // tm),
                  in_specs=[row, wsp, wsp, vec, vec, vec, row, row],
                  out_specs=[row, wsp, wsp, vec, vec, vec],
                  out_shape=[jax.ShapeDtypeStruct((t, D_MODEL), F32), wshape, wshape, vshape, vshape, vshape],
                  compiler_params=_cparams(("parallel", "arbitrary")))(xc, w_a, w_x, b_a, b_x, lam, hprev, du)


def _ffn_in_act(a, w, tm=1024, ride=None):
    t = a.shape[0]
    tm = _tile(t, tm)

    def body(a_ref, wg_ref, wu_ref, hgu_ref, act_ref):
        ab = a_ref[...].astype(BF16)
        hg = lax.dot_general(ab, wg_ref[...].astype(BF16), (NN, ((), ())), preferred_element_type=F32)
        hu = lax.dot_general(ab, wu_ref[...].astype(BF16), (NN, ((), ())), preferred_element_type=F32)
        hgu_ref[0] = hg
        hgu_ref[1] = hu
        act_ref[...] = _f_act(hg, hu).astype(act_ref.dtype)

    wspec = lambda off: pl.BlockSpec((None, D_MODEL, FF_SH), lambda i, s: (s + off, 0, 0))
    hgu, act = _pcall(body, ride=ride, name="ffn_in", grid=(t // tm, N_FF),
                      in_specs=[pl.BlockSpec((tm, D_MODEL), lambda i, s: (i, 0)), wspec(0), wspec(N_FF)],
                      out_specs=[pl.BlockSpec((2, None, tm, FF_SH), lambda i, s: (0, s, i, 0)),
                                 pl.BlockSpec((None, tm, FF_SH), lambda i, s: (s, i, 0))],
                      out_shape=[jax.ShapeDtypeStruct((2, N_FF, t, FF_SH), F32),
                                 jax.ShapeDtypeStruct((N_FF, t, FF_SH), BF16)],
                      compiler_params=_cparams(("parallel", "parallel")))(a, w, w)
    return hgu.reshape(2 * N_FF, t, FF_SH), act


def _ffn_out_dx_act(d, w, hgu, tm=1024, ride=None):
    t = d.shape[0]
    tm = _tile(t, tm)

    def body(d_ref, w_ref, hg_ref, hu_ref, o_ref):
        dact = lax.dot_general(d_ref[...].astype(BF16), w_ref[...].astype(BF16), (NT, ((), ())),
                               preferred_element_type=F32)
        _, pull = jax.vjp(_f_act, hg_ref[...], hu_ref[...])
        dhg, dhu = pull(dact)
        o_ref[0] = dhg.astype(o_ref.dtype)
        o_ref[1] = dhu.astype(o_ref.dtype)

    hspec = lambda off: pl.BlockSpec((None, tm, FF_SH), lambda i, s: (s + off, i, 0))
    res = _pcall(body, ride=ride, name="ffn_out_dx", grid=(t // tm, N_FF),
                 in_specs=[pl.BlockSpec((tm, D_MODEL), lambda i, s: (i, 0)),
                           pl.BlockSpec((None, FF_SH, D_MODEL), lambda i, s: (s, 0, 0)), hspec(0), hspec(N_FF)],
                 out_specs=pl.BlockSpec((2, None, tm, FF_SH), lambda i, s: (0, s, i, 0)),
                 out_shape=jax.ShapeDtypeStruct((2, N_FF, t, FF_SH), BF16),
                 compiler_params=_cparams(("parallel", "parallel")))(d, w, hgu, hgu)
    return res.reshape(2 * N_FF, t, FF_SH)


def _adamw(name, parts, w, m, v, tr=128, ride=None):
    ng = len(parts)
    n_src, r, c = parts[0].shape
    per = w.shape[1] // r
    assert w.shape[0] * per == ng and w.shape[2] == c
    tr = _tile(r, tr)
    nb = r // tr
    bc1 = 1.0 - ADAM_B1 ** ADAM_STEP
    bc2 = 1.0 - ADAM_B2 ** ADAM_STEP

    def body(*refs):
        p_refs = refs[:ng]
        w_ref, m_ref, v_ref, g_ref, d_ref, nm_ref, nv_ref = refs[ng:]
        grp = pl.program_id(0)

        def update(p_ref):
            g = p_ref[0].astype(F32)
            for s in range(1, n_src):
                g = g + p_ref[s].astype(F32)
            nm = ADAM_B1 * m_ref[...] + (1.0 - ADAM_B1) * g
            nv = ADAM_B2 * v_ref[...] + (1.0 - ADAM_B2) * jnp.square(g)
            g_ref[...] = g
            nm_ref[...] = nm
            nv_ref[...] = nv
            d_ref[...] = -ADAM_LR * ((nm / bc1) / (jnp.sqrt(nv / bc2) + ADAM_EPS) + ADAM_WD * w_ref[...])

        for k in range(ng):
            pl.when(grp == k)(functools.partial(update, p_refs[k]))

    p_specs = [pl.BlockSpec((n_src, tr, c), functools.partial(lambda gi, i, k: (0, jnp.where(gi == k, i, 0), 0), k=k))
               for k in range(ng)]
    spec = pl.BlockSpec((None, tr, c), lambda gi, i: (gi // per, (gi % per) * nb + i, 0))
    return _pcall(body, ride=ride, name=name, grid=(ng, nb), in_specs=p_specs + [spec, spec, spec],
                  out_specs=[spec] * 4, out_shape=[jax.ShapeDtypeStruct(w.shape, F32)] * 4,
                  compiler_params=_cparams(("parallel", "parallel")))(*parts, w, m, v)


def _sum_parts(name, parts, tr=256):
    _, r, c = parts.shape
    tr = _tile(r, tr)

    def body(p_ref, o_ref):
        g = p_ref[0]
        for s in range(1, parts.shape[0]):
            g = g + p_ref[s]
        o_ref[...] = g

    return _pcall(body, name=name, grid=(r // tr,),
                  in_specs=[pl.BlockSpec((parts.shape[0], tr, c), lambda i: (0, i, 0))],
                  out_specs=pl.BlockSpec((tr, c), lambda i: (i, 0)),
                  out_shape=jax.ShapeDtypeStruct((r, c), F32), compiler_params=_cparams(("parallel",)))(parts)


def _peer(k):
    x, y, c = lax.axis_index("x"), lax.axis_index("y"), lax.axis_index("c")
    return (x ^ ((k >> 2) & 1), y ^ ((k >> 1) & 1), c ^ (k & 1))


def _my_id():
    return 4 * lax.axis_index("x") + 2 * lax.axis_index("y") + lax.axis_index("c")


def _exchange(name, ride):
    n = len(ride.arrays)

    def body(*refs):
        ride.begin(refs[:n], refs[n:2 * n], refs[2 * n:])
        ride.finish(refs[:n], refs[n:2 * n], refs[2 * n:])

    hbm = pl.BlockSpec(memory_space=pltpu.HBM)
    return _pcall(body, name=name, in_specs=[hbm] * n, out_specs=[hbm] * n, out_shape=ride.out_shapes(),
                  scratch_shapes=ride.scratch())(*ride.arrays)


def _row(v):
    return v.reshape(1, -1)


def _time_major_heads(c, bsz, seq):
    return c.reshape(bsz, seq, BLK)[:, :, :N_HEADS].transpose(0, 2, 1)


def _no_ride(*_):
    return None


TWICE = [(D_MODEL, F32), (D_MODEL, BF16)]


def _both(fn):
    def run(*v):
        y = fn(*v)
        return y, y
    return run


def _layer_fwd(h, hb, p_l, w, bsz, seq, ride_of=_no_ride):
    t = bsz * seq
    zq = _mm_nn("z_proj_qkv", hb, w['w_in7'], n=QKV, out_dtype=BF16, ride=ride_of('z_proj_qkv'))
    zr = _mm_nn("z_proj_rest", hb, w['w_in7'], b_off=QKV, n=4 * D_MODEL, ride=ride_of('z_proj_rest'))
    fl = _mm_nn("f_proj", hb, w['w_inf'])
    logf, = _rowwise("logf_fwd", lambda f, b: (_f_logf(f, b),), [fl], [w['b_forget']], [(BLK, F32)], [])
    c = _scan("cumsum_fwd", None, logf, bsz, seq, reverse=False)
    ct = _time_major_heads(c, bsz, seq)
    cq, ck = ct[..., None], ct[:, :, None, :]
    att, attb, lse = _attn_fwd(zq, cq, ck, bsz, seq, ride=ride_of('attn_fwd'))
    xc = _conv_fwd(zr, w['conv_w'], w['conv_b'], bsz, seq)
    decay, = _rowwise("decay_fwd", lambda lam: (_f_decay(lam),), [w['rg_lambda']], [], [(D_MODEL, F32)], [])
    a, u = _gate_fwd(xc, w['rg_w_a'], w['rg_w_x'], w['rg_b_a'], w['rg_b_x'], decay, ride=ride_of('gate_fwd'))
    hs, hprev = _scan("lru_fwd", a, u, bsz, seq, reverse=False, with_prev=True, ride=ride_of('lru_fwd'))
    rnn, = _rowwise("rnn_out_fwd", lambda s, y: (_f_rnn_out(s, y),), [hs, (zr, OFF_RY, D_MODEL)], [],
                    [(D_MODEL, BF16)], [])
    ya = _mm_nn("branch_att", attb, w['w_branch_att'])
    yb = _mm_nn("branch_rnn", rnn, w['w_branch_rnn'])
    merged, = _rowwise("merge_fwd", lambda *v: (_f_merge(*v),),
                       [(zr, OFF_GA, D_MODEL), (zr, OFF_GB, D_MODEL), ya, yb], [w['b_merge0'], w['b_merge1']],
                       [(D_MODEL, BF16)], [])
    mix = _mm_nn("mix_out", merged, w['w_out'])
    h1, h1b = _rowwise("ln_mix_fwd", _both(_f_resid_ln), [h, mix], [w['ln_mix_g'], w['ln_mix_b']], TWICE, [])
    tm = _tile(t, 1024)
    hgu, act = _ffn_in_act(h1b, w['w_ffn_in'], ride=ride_of('ffn_in'))
    ffn = _mm("ffn_out", act, w['w_ffn_out'], grid=(t // tm, 1, N_FF),
              a_spec=pl.BlockSpec((None, tm, FF_SH), lambda i, j, s: (s, i, 0)),
              b_spec=pl.BlockSpec((None, FF_SH, D_MODEL), lambda i, j, s: (s, 0, 0)),
              o_spec=pl.BlockSpec((tm, D_MODEL), lambda i, j, s: (i, 0)),
              out_shape=jax.ShapeDtypeStruct((t, D_MODEL), F32), contract=NN, ride=ride_of('ffn_out'))
    h2, h2b = _rowwise("ln_ffn_fwd", _both(_f_resid_ln), [h1, ffn], [w['ln_ffn_g'], w['ln_ffn_b']], TWICE, [])
    gp = _mm_nn("ple_gate", h2b, w['w_ple_gate'])
    pe = _mm_nn("ple_proj", p_l, w['w_ple'])
    h3, h3b = _rowwise("ln_ple_fwd", _both(_f_ple), [h2, gp, pe],
                       [w['b_ple_gate'], w['ln_ple_g'], w['ln_ple_b']], TWICE, [])
    saved = dict(h=h, hb=hb, zq=zq, zr=zr, fl=fl, cq=cq, ck=ck, att=att, attb=attb, lse=lse, xc=xc, a=a, decay=decay,
                 hprev=hprev, hs=hs, rnn=rnn, ya=ya, yb=yb, merged=merged, mix=mix, h1=h1, h1b=h1b, hgu=hgu,
                 act=act, ffn=ffn, h2=h2, h2b=h2b, gp=gp, pe=pe)
    return h3, h3b, saved


def _layer_bwd(dh3, p_l, w, s, bsz, seq, ride_of=_no_ride):
    t = bsz * seq
    g = {}
    dh2, dgp, dpe, g['b_ple_gate'], g['ln_ple_g'], g['ln_ple_b'] = _vjp_rowwise(
        "ln_ple_bwd", _f_ple, [s['h2'], s['gp'], s['pe']], [w['b_ple_gate'], w['ln_ple_g'], w['ln_ple_b']], [dh3], 3,
        dtypes=[F32, BF16, BF16])
    g['w_ple_gate'] = _mm_tn("ple_gate_dw", s['h2b'], dgp, out_dtype=BF16)
    g['w_ple'] = _mm_tn("ple_proj_dw", p_l, dpe, out_dtype=BF16)
    dh2b = _mm_nt("ple_gate_dx", dgp, w['w_ple_gate'])
    dh1, dffn, g['ln_ffn_g'], g['ln_ffn_b'] = _ln_resid_bwd(
        "ln_ffn_bwd", s['h1'], s['ffn'], w['ln_ffn_g'], w['ln_ffn_b'], dh2, dh2b)
    tm = _tile(t, 1024)
    tk = _tile(t, 2048)
    g['w_ffn_out'] = _mm("ffn_out_dw", s['act'], dffn, grid=(N_FF, 1, t // tk),
                         a_spec=pl.BlockSpec((None, tk, FF_SH), lambda ss, j, k: (ss, k, 0)),
                         b_spec=pl.BlockSpec((tk, D_MODEL), lambda ss, j, k: (k, 0)),
                         o_spec=pl.BlockSpec((None, FF_SH, D_MODEL), lambda ss, j, k: (ss, 0, 0)),
                         out_shape=jax.ShapeDtypeStruct((N_FF, FF_SH, D_MODEL), BF16), contract=TN)
    dhgu = _ffn_out_dx_act(dffn, w['w_ffn_out'], s['hgu'], ride=ride_of('ffn_out_dx', g))
    g['w_ffn_in'] = _mm("ffn_in_dw", s['h1b'], dhgu, grid=(2 * N_FF, 1, t // tk),
                        a_spec=pl.BlockSpec((tk, D_MODEL), lambda ss, j, k: (k, 0)),
                        b_spec=pl.BlockSpec((None, tk, FF_SH), lambda ss, j, k: (ss, k, 0)),
                        o_spec=pl.BlockSpec((None, D_MODEL, FF_SH), lambda ss, j, k: (ss, 0, 0)),
                        out_shape=jax.ShapeDtypeStruct((2 * N_FF, D_MODEL, FF_SH), BF16), contract=TN,
                        ride=ride_of('ffn_in_dw', g))
    dh1b = _mm("ffn_in_dx", dhgu, w['w_ffn_in'], grid=(t // tm, 1, 2 * N_FF),
               a_spec=pl.BlockSpec((None, tm, FF_SH), lambda i, j, ss: (ss, i, 0)),
               b_spec=pl.BlockSpec((None, D_MODEL, FF_SH), lambda i, j, ss: (ss, 0, 0)),
               o_spec=pl.BlockSpec((tm, D_MODEL), lambda i, j, ss: (i, 0)),
               out_shape=jax.ShapeDtypeStruct((t, D_MODEL), F32), contract=NT, ride=ride_of('ffn_in_dx', g))
    dh, dmix, g['ln_mix_g'], g['ln_mix_b'] = _ln_resid_bwd(
        "ln_mix_bwd", s['h'], s['mix'], w['ln_mix_g'], w['ln_mix_b'], dh1, dh1b)
    g['w_out'] = _mm_tn("mix_out_dw", s['merged'], dmix, out_dtype=BF16)
    dmerged = _mm_nt("mix_out_dx", dmix, w['w_out'])
    z = s['zr']
    dga, dgb, dya, dyb, dbm0, dbm1 = _vjp_rowwise(
        "merge_bwd", _f_merge, [(z, OFF_GA, D_MODEL), (z, OFF_GB, D_MODEL), s['ya'], s['yb']],
        [w['b_merge0'], w['b_merge1']], [dmerged], 4, dtypes=[BF16] * 4)
    g['b_merge'] = jnp.concatenate([dbm0, dbm1], axis=0)
    g['w_branch_att'] = _mm_tn("branch_att_dw", s['attb'], dya, out_dtype=BF16)
    g['w_branch_rnn'] = _mm_tn("branch_rnn_dw", s['rnn'], dyb, out_dtype=BF16)
    datt = _mm_nt("branch_att_dx", dya, w['w_branch_att'], out_dtype=BF16)
    drnn = _mm_nt("branch_rnn_dx", dyb, w['w_branch_rnn'])
    dhs, dry = _vjp_rowwise("rnn_out_bwd", _f_rnn_out, [s['hs'], (z, OFF_RY, D_MODEL)], [], [drnn], 2,
                            dtypes=[F32, BF16])
    lam = _scan("lru_bwd", s['a'], dhs, bsz, seq, reverse=True)
    dxc, g['rg_w_a'], g['rg_w_x'], g['rg_b_a'], g['rg_b_x'], ddecay = _gate_bwd(
        s['xc'], w['rg_w_a'], w['rg_w_x'], w['rg_b_a'], w['rg_b_x'], s['decay'], s['hprev'], lam,
        ride=ride_of('gate_bwd', g))
    g['rg_lambda'], = _vjp_rowwise("decay_bwd", _f_decay, [w['rg_lambda']], [], [ddecay], 1)
    drx, g['conv_w'], g['conv_b'] = _conv_bwd(z, dxc, w['conv_w'], bsz, seq)
    dq, dk, dv, dcq, dck = _attn_bwd(s['zq'], s['att'], datt, s['lse'], s['cq'], s['ck'], bsz, seq,
                                     ride=ride_of('attn_bwd', g))
    dc = (dcq[:, :, :, 0] + dck.reshape(bsz, N_HEADS, seq)).transpose(0, 2, 1)
    dc = jnp.pad(dc, ((0, 0), (0, 0), (0, BLK - N_HEADS))).reshape(t, BLK)
    dlogf = _scan("cumsum_bwd", None, dc, bsz, seq, reverse=True)
    dfl, g['b_forget'] = _vjp_rowwise("logf_bwd", _f_logf, [s['fl']], [w['b_forget']], [dlogf], 1, dtypes=[BF16])
    dz = jnp.concatenate([dq, dk, dv, drx, dry, dga, dgb], axis=1)
    g['w_in7'] = _mm_tn("z_proj_dw", s['hb'], dz, out_dtype=BF16)
    g['w_inf'] = _mm_tn("f_proj_dw", s['hb'], dfl, out_dtype=BF16)
    dh = _mm_nt("z_proj_dx", dz, w['w_in7'], ride=ride_of('z_proj_dx', g), add=dh)
    dh = _mm_nt("f_proj_dx", dfl, w['w_inf'], add=dh)
    return dh, g


def _ln_resid_bwd(name, h, branch, gam, bet, d0, d1):
    def bwd(hv, bv, d0v, d1v, gv, btv):
        _, pull = jax.vjp(_f_resid_ln, hv, bv, gv, btv)
        dh, db, dg, dbt = pull(d0v + d1v)
        return dh, db, _colsum(dg), _colsum(dbt)

    return _rowwise(name, bwd, [h, branch, d0, d1], [gam, bet], [(D_MODEL, F32), (D_MODEL, BF16)],
                    [D_MODEL, D_MODEL], tm=256)


class _Schedule:
    FWD = {'z_proj_qkv': ['w_ffn_out'], 'z_proj_rest': ['w_branch_att', 'w_branch_rnn', 'w_out', 'w_ple_gate'],
           'attn_fwd': ['w_in'], 'ffn_in': ['w_ffn_in'], 'ffn_out': ['w_ple', 'conv_w', 'b_merge']}
    FIRST = ['w_in', 'conv_w', 'b_merge']
    OWN = {'z_proj_qkv': ['w_branch_att', 'w_branch_rnn', 'w_out'], 'z_proj_rest': ['w_ffn_in'],
           'attn_fwd': ['w_ffn_out', 'w_ple_gate', 'w_ple']}
    NEXT = {'attn_fwd': ['w_in'], 'gate_fwd': ['w_ffn_out'],
            'lru_fwd': ['w_branch_att', 'w_branch_rnn', 'w_out', 'w_ple_gate'],
            'ffn_in': ['w_ffn_in'], 'ffn_out': ['w_ple', 'conv_w', 'b_merge']}
    BWD = {'ffn_out_dx': ['w_ffn_out'], 'ffn_in_dw': ['w_ple_gate', 'w_ple'],
           'gate_bwd': ['w_out', 'w_branch_att', 'w_branch_rnn'],
           'attn_bwd': ['w_ffn_in', 'conv_w', 'b_merge']}

    def __init__(self, shards, depth):
        self.shards, self.depth = shards, depth
        self.gathered = [{} for _ in range(depth)]
        self.received = [{} for _ in range(depth)]
        self.pending = []
        self.deferred = None

    def gather_first(self):
        ride = _Ride([self.shards[n] for n in self.FIRST], gather=True, index=[0] * len(self.FIRST))
        self.gathered[0].update(zip(self.FIRST, _exchange("gather_first", ride)))

    def gather_ride(self, layer, kernel_name):
        own = self.OWN.get(kernel_name, []) if layer == 0 else []
        nxt = (self.NEXT if layer == 0 else self.FWD).get(kernel_name, []) if layer + 1 < self.depth else []
        items = [(n, 0) for n in own] + [(n, layer + 1) for n in nxt]
        if not items:
            return None
        ride = _Ride([self.shards[n] for n, _ in items], gather=True, index=[l for _, l in items])
        self.pending.append((ride, [(n, self.gathered[l]) for n, l in items]))
        return ride

    def _scatter(self, arrays, names, layer):
        ride = _Ride(arrays, gather=False)
        self.pending.append((ride, [(n, self.received[layer]) for n in names]))
        return ride

    def scatter_ride(self, layer, kernel_name, grads):
        if kernel_name == 'z_proj_dx':
            whole = _by_destination('w_in', grads)
            half = whole.shape[1] // 2
            self.deferred = (whole[:, half:], layer)
            return self._scatter([whole[:, :half]], ['w_in_a'], layer)
        if kernel_name == 'ffn_in_dx':
            if self.deferred is None:
                return None
            (late, from_layer), self.deferred = self.deferred, None
            return self._scatter([late], ['w_in_b'], from_layer)
        names = self.BWD[kernel_name]
        return self._scatter([_by_destination(n, grads) for n in names], names, layer)

    def collect(self):
        for ride, places in self.pending:
            if ride.result is not None:
                for (name, dst), res in zip(places, ride.result):
                    dst[name] = res
        self.pending = [(ride, places) for ride, places in self.pending if ride.result is None]


class _LayerWeights:
    SOURCE = {'w_in7': 'w_in', 'w_inf': 'w_in', 'b_merge0': 'b_merge', 'b_merge1': 'b_merge'}

    def __init__(self, sched, layer, replicated):
        self.sched, self.layer, self.made = sched, layer, dict(replicated)

    def __getitem__(self, key):
        if key not in self.made:
            self.sched.collect()
            src = self.SOURCE.get(key, key)
            self.made.update(_from_shards(src, self.sched.gathered[self.layer][src]))
        return self.made[key]


def _local_step(x2, tgt, p3, weights_of, depth, g_in, b_in, bsz, seq, sched=None):
    h, hb = _rowwise("ln_in_fwd", _both(_ln), [x2], [g_in, b_in], TWICE, [])
    p3 = p3.astype(BF16)
    saved, layer_w = [], []
    for l in range(depth):
        layer_w.append(weights_of(l))
        ride_of = functools.partial(sched.gather_ride, l) if sched else _no_ride
        h, hb, s = _layer_fwd(h, hb, p3[l], layer_w[l], bsz, seq, ride_of)
        if sched:
            sched.collect()
        saved.append(s)

    def loss_fn(y, tv):
        err = y - tv
        return err * (1.0 / D_MODEL), _colsum(jnp.square(err))

    dh, sq = _rowwise("loss", loss_fn, [h, tgt], [], [(D_MODEL, F32)], [D_MODEL])
    grads = [None] * depth
    for l in reversed(range(depth)):
        ride_of = functools.partial(sched.scatter_ride, l) if sched else _no_ride
        dh, grads[l] = _layer_bwd(dh, p3[l], layer_w[l], saved[l], bsz, seq, ride_of)
        if sched:
            sched.collect()
    dx, dg_in, db_in = _vjp_rowwise("ln_in_bwd", _ln, [x2], [g_in, b_in], [dh], 1)
    return sq, dx, grads, dg_in, db_in


def _from_shards(name, g):
    if name == 'w_in':
        wt = g.transpose(1, 0, 2).reshape(D_MODEL, N_IN)
        return {'w_in7': jnp.concatenate([wt[:, :3 * D_MODEL], wt[:, 3 * D_MODEL + N_HEADS:]], axis=1),
                'w_inf': jnp.pad(wt[:, 3 * D_MODEL:3 * D_MODEL + N_HEADS], ((0, 0), (0, BLK - N_HEADS)))}
    if name in ('w_branch_att', 'w_branch_rnn', 'w_out', 'w_ple_gate'):
        return {name: g.reshape(D_MODEL, D_MODEL)}
    if name == 'w_ffn_in':
        return {name: g}
    if name == 'w_ffn_out':
        return {name: g.reshape(N_FF, FF_SH, D_MODEL)}
    if name == 'b_merge':
        bm = g.transpose(1, 0, 2).reshape(2, D_MODEL)
        return {'b_merge0': bm[0:1], 'b_merge1': bm[1:2]}
    return {name: g.transpose(1, 0, 2).reshape(g.shape[1], D_MODEL)}


def _layer_weights(full):
    w = {}
    for name, g in full.items():
        w.update(_from_shards(name, g))
    return w


def _by_destination(name, gw):
    if name == 'w_in':
        g7, gf = gw['w_in7'], gw['w_inf']
        true = jnp.concatenate([g7[:, :3 * D_MODEL], gf[:, :N_HEADS], g7[:, 3 * D_MODEL:]], axis=1)
        return true.reshape(D_MODEL, N_DEV, IN_SH).transpose(1, 0, 2)
    g = gw[name]
    if name in ('w_branch_att', 'w_branch_rnn', 'w_out', 'w_ple_gate'):
        return g.reshape(N_DEV, D_MODEL // N_DEV, D_MODEL)
    if name == 'w_ffn_in':
        return g
    if name == 'w_ffn_out':
        return g.reshape(N_DEV, N_FF * FF_SH // N_DEV, D_MODEL)
    return g.reshape(g.shape[0], N_DEV, BLK).transpose(1, 0, 2)


def kernel(x, p, ln_in_g, ln_in_b, w_in, b_forget, conv_w, conv_b, rg_w_a, rg_b_a, rg_w_x, rg_b_x, rg_lambda, w_branch_att, w_branch_rnn, b_merge, w_out, ln_mix_g, ln_mix_b, w_ffn_in, w_ffn_out, ln_ffn_g, ln_ffn_b, w_ple, w_ple_gate, b_ple_gate, ln_ple_g, ln_ple_b, loss_target, m_ln_in_g, m_ln_in_b, m_w_in, m_b_forget, m_conv_w, m_conv_b, m_rg_w_a, m_rg_b_a, m_rg_w_x, m_rg_b_x, m_rg_lambda, m_w_branch_att, m_w_branch_rnn, m_b_merge, m_w_out, m_ln_mix_g, m_ln_mix_b, m_w_ffn_in, m_w_ffn_out, m_ln_ffn_g, m_ln_ffn_b, m_w_ple, m_w_ple_gate, m_b_ple_gate, m_ln_ple_g, m_ln_ple_b, v_ln_in_g, v_ln_in_b, v_w_in, v_b_forget, v_conv_w, v_conv_b, v_rg_w_a, v_rg_b_a, v_rg_w_x, v_rg_b_x, v_rg_lambda, v_w_branch_att, v_w_branch_rnn, v_b_merge, v_w_out, v_ln_mix_g, v_ln_mix_b, v_w_ffn_in, v_w_ffn_out, v_ln_ffn_g, v_ln_ffn_b, v_w_ple, v_w_ple_gate, v_b_ple_gate, v_ln_ple_g, v_ln_ple_b):
    env = dict(locals())
    wts = {n: env[n] for n in WEIGHTS}
    mom = {n: env['m_' + n] for n in WEIGHTS}
    var = {n: env['v_' + n] for n in WEIGHTS}
    bsz, seq, _ = x.shape
    depth = w_in.shape[0]
    t = bsz * seq
    x2, tgt = x.reshape(t, D_MODEL), loss_target.reshape(t, D_MODEL)
    p3 = p.reshape(depth, t, D_PLE)

    shard_names = SHARDED_BF16 + SHARDED_F32
    shards = {n: wts[n].astype(BF16) for n in SHARDED_BF16}
    shards.update({n: wts[n] for n in SHARDED_F32})
    sched = _Schedule(shards, depth)
    sched.gather_first()

    def weights_of(l):
        w = {n: _row(wts[n][l]) for n in ['conv_b', 'rg_b_a', 'rg_b_x', 'rg_lambda', 'ln_mix_g', 'ln_mix_b',
                                          'ln_ffn_g', 'ln_ffn_b', 'b_ple_gate', 'ln_ple_g', 'ln_ple_b']}
        w['b_forget'] = jnp.pad(_row(b_forget[l]), ((0, 0), (0, BLK - N_HEADS)))
        w['rg_w_a'], w['rg_w_x'] = rg_w_a[l], rg_w_x[l]
        return _LayerWeights(sched, l, w)

    g_in, b_in = _row(ln_in_g), _row(ln_in_b)
    sq, dx, grads, dg_in, db_in = _local_step(x2, tgt, p3, weights_of, depth, g_in, b_in, bsz, seq, sched)
    loss = lax.psum(0.5 * jnp.sum(sq) / D_MODEL, ("x", "y", "c"))
    grad_x = dx.reshape(bsz, seq, D_MODEL)

    out = {}

    def update(n, ride=None):
        shp = wts[n].shape
        view = lambda a: a
        if n == 'w_in':
            recv = [sched.received[l][half] for l in range(depth) for half in ('w_in_a', 'w_in_b')]
        else:
            recv = [sched.received[l][n] for l in range(depth)]
        if n in SHARDED_F32:
            recv = [jnp.stack(recv, axis=1).reshape(N_DEV, -1, shp[-1])]
            view = lambda a: a.reshape(1, -1, shp[-1])
        res = _adamw("adamw_" + n, recv, view(wts[n]), view(mom[n]), view(var[n]), ride=ride)
        out[n] = [r.reshape(shp) for r in res]

    def rep_grad(n):
        if n == 'ln_in_g':
            return dg_in.reshape(-1)
        if n == 'ln_in_b':
            return db_in.reshape(-1)
        return jnp.stack([grads[l][n].reshape(wts[n].shape[1:]) if n != 'b_forget'
                          else grads[l][n][0, :N_HEADS] for l in range(depth)]).reshape(-1)

    sizes = [int(wts[n].size) for n in REPLICATED]
    n_rows = [8 * (-(-sz // (8 * BLK))) for sz in sizes]
    total_rows = -(-sum(n_rows) // (N_DEV * 8)) * (N_DEV * 8)

    def as_rows(v, sz, nr):
        v = v.reshape(-1)
        return (jnp.pad(v, (0, nr * BLK - sz)) if nr * BLK != sz else v).reshape(nr, BLK)

    def pack(vals):
        parts = [as_rows(v, sz, nr) for v, sz, nr in zip(vals, sizes, n_rows)]
        parts.append(jnp.zeros((total_rows - sum(n_rows), BLK), F32))
        return jnp.concatenate(parts, axis=0)

    late, from_layer = sched.deferred
    last_w_in = _Ride([late], gather=False)
    update('w_ffn_in', ride=last_w_in)
    sched.received[from_layer]['w_in_b'], = last_w_in.result
    scatter_small = _Ride([pack([rep_grad(n) for n in REPLICATED]).reshape(N_DEV, total_rows // N_DEV, BLK)],
                          gather=False)
    update('w_branch_att', ride=scatter_small)
    gather_small = _Ride([_sum_parts("sum_small", scatter_small.result[0])], gather=True)
    update('w_out', ride=gather_small)
    for n in shard_names:
        if n not in out:
            update(n)
    g_rows = gather_small.result[0].reshape(total_rows, BLK)
    starts = [sum(n_rows[:i]) for i in range(len(n_rows))]
    for n, r0, sz, nr in zip(REPLICATED, starts, sizes, n_rows):
        shp = wts[n].shape
        as_one = (1, 1, sz) if len(shp) == 1 else (1, -1, shp[-1])
        g_n = g_rows[r0:r0 + nr]
        g_n = (g_n if nr * BLK == sz else g_n.reshape(-1)[:sz]).reshape(as_one)
        res = _adamw("adamw_" + n, [g_n], *[d[n].reshape(as_one) for d in (wts, mom, var)])
        out[n] = [r.reshape(shp) for r in res]

    return (loss, grad_x, *[out[n][k] for k in range(4) for n in WEIGHTS])
```

```python
import functools
import math

import jax
import jax.numpy as jnp
from jax import lax
from jax.experimental import pallas as pl
from jax.experimental.pallas import tpu as pltpu

F32 = jnp.float32
BF16 = jnp.bfloat16

N_DEV = 8
D_MODEL = 1024
N_HEADS = 8
HEAD_DIM = 128
N_BLK = 8
BLK = 128
CONV_W = 4
D_PLE = 256
FF_SH = 704
N_FF = 4
IN_SH = 897
N_IN = 7176
DEPTH = 4
RG_C = 8.0
ALPHA = float((2 * DEPTH) ** 0.25)
LN_EPS = 1e-5
SCALE = 1.0 / math.sqrt(HEAD_DIM)
NEG = -1e30
ADAM_LR, ADAM_B1, ADAM_B2, ADAM_EPS, ADAM_WD, ADAM_STEP = 0.001, 0.9, 0.999, 1e-08, 0.01, 10
QKV = 3 * D_MODEL
OFF_RX, OFF_RY, OFF_GA, OFF_GB = (i * D_MODEL for i in range(4))
V7X_VMEM_LIMIT = 48 * 1024 * 1024

WEIGHTS = ['ln_in_g', 'ln_in_b', 'w_in', 'b_forget', 'conv_w', 'conv_b', 'rg_w_a', 'rg_b_a', 'rg_w_x', 'rg_b_x',
           'rg_lambda', 'w_branch_att', 'w_branch_rnn', 'b_merge', 'w_out', 'ln_mix_g', 'ln_mix_b', 'w_ffn_in',
           'w_ffn_out', 'ln_ffn_g', 'ln_ffn_b', 'w_ple', 'w_ple_gate', 'b_ple_gate', 'ln_ple_g', 'ln_ple_b']
SHARDED_BF16 = ['w_in', 'w_branch_att', 'w_branch_rnn', 'w_out', 'w_ffn_in', 'w_ffn_out', 'w_ple', 'w_ple_gate']
SHARDED_F32 = ['conv_w', 'b_merge']
REPLICATED = [n for n in WEIGHTS if n not in SHARDED_BF16 and n not in SHARDED_F32]

NN = ((1,), (0,))
NT = ((1,), (1,))
TN = ((0,), (0,))


class _Ride:
    def __init__(self, arrays, *, gather, index=None):
        self.arrays, self.gather, self.index = list(arrays), gather, index
        self.result = None

    def _shard(self, ins, a):
        return ins[a] if self.index is None else ins[a].at[self.index[a]]

    def out_shapes(self):
        if not self.gather:
            return [jax.ShapeDtypeStruct(a.shape, a.dtype) for a in self.arrays]
        cut = 0 if self.index is None else 1
        return [jax.ShapeDtypeStruct((N_DEV,) + a.shape[cut:], a.dtype) for a in self.arrays]

    def scratch(self):
        n = len(self.arrays)
        return [pltpu.SemaphoreType.DMA((n * N_DEV,)), pltpu.SemaphoreType.DMA((n * N_DEV,)),
                pltpu.SemaphoreType.DMA((n,))]

    def _copy(self, a, k, src, dst, sems, to=None):
        send_sems, recv_sems, _ = sems
        return pltpu.make_async_remote_copy(
            src_ref=src, dst_ref=dst, send_sem=send_sems.at[a * N_DEV + k], recv_sem=recv_sems.at[a * N_DEV + k],
            device_id=_peer(k if to is None else to), device_id_type=pl.DeviceIdType.MESH)

    def begin(self, ins, outs, sems):
        me = _my_id()
        started = []
        for a in range(len(ins)):
            if self.gather:
                src = self._shard(ins, a)
                started.append(pltpu.make_async_copy(src, outs[a].at[me], sems[2].at[a]))
                started += [self._copy(a, k, src, outs[a].at[me], sems) for k in (1, 2, 4, 6)]
            else:
                started.append(pltpu.make_async_copy(ins[a].at[me], outs[a].at[me], sems[2].at[a]))
                started += [self._copy(a, k, ins[a].at[me ^ k], outs[a].at[me], sems) for k in range(1, N_DEV)]
        for cp in started:
            cp.start()

    def finish(self, ins, outs, sems):
        me = _my_id()
        for a in range(len(ins)):
            if self.gather:
                src = self._shard(ins, a)
                passed = []
                for k in (2, 4, 6):
                    block = outs[a].at[me ^ k]
                    self._copy(a, k, src, block, sems).wait_recv()
                    passed.append(self._copy(a, k + 1, block, block, sems, to=1))
                    passed[-1].start()
                for k in (1, 2, 4, 6):
                    self._copy(a, k, src, outs[a].at[me], sems).wait_send()
                self._copy(a, 1, src, outs[a].at[me ^ 1], sems).wait_recv()
                for cp in passed:
                    cp.wait()
                pltpu.make_async_copy(src, outs[a].at[me], sems[2].at[a]).wait()
            else:
                pltpu.make_async_copy(ins[a].at[me], outs[a].at[me], sems[2].at[a]).wait()
                for k in range(1, N_DEV):
                    self._copy(a, k, ins[a].at[me ^ k], outs[a].at[me], sems).wait()


def _pcall(body, ride=None, **kw):
    if ride is None:
        return pl.pallas_call(body, **kw)
    n = len(ride.arrays)
    grid = kw['grid']
    single = not isinstance(kw['out_shape'], (list, tuple))
    out_specs = [kw['out_specs']] if single else list(kw['out_specs'])
    out_shape = [kw['out_shape']] if single else list(kw['out_shape'])
    in_specs = list(kw['in_specs'])
    scratch = list(kw.get('scratch_shapes', ()))
    n_in, n_out, n_sc = len(in_specs), len(out_shape), len(scratch)
    hbm = pl.BlockSpec(memory_space=pltpu.HBM)

    def wrapped(*refs):
        ins, xin = refs[:n_in], refs[n_in:n_in + n]
        outs, xout = refs[n_in + n:n_in + n + n_out], refs[n_in + n + n_out:n_in + 2 * n + n_out]
        sc, sems = refs[n_in + 2 * n + n_out:n_in + 2 * n + n_out + n_sc], refs[-3:]
        ids = [pl.program_id(ax) for ax in range(len(grid))]
        first = functools.reduce(jnp.logical_and, [i == 0 for i in ids])
        last = functools.reduce(jnp.logical_and, [i == g - 1 for i, g in zip(ids, grid)])

        pl.when(first)(lambda: ride.begin(xin, xout, sems))
        body(*ins, *outs, *sc)
        pl.when(last)(lambda: ride.finish(xin, xout, sems))

    call = pl.pallas_call(wrapped, name=kw['name'], grid=grid, in_specs=in_specs + [hbm] * n,
                          out_specs=out_specs + [hbm] * n, out_shape=out_shape + ride.out_shapes(),
                          scratch_shapes=scratch + ride.scratch(), compiler_params=kw['compiler_params'])

    def run(*args):
        res = call(*args, *ride.arrays)
        ride.result = list(res[n_out:])
        return res[0] if single else list(res[:n_out])

    return run


def _tile(n, pref, mult=8):
    if n <= pref:
        return n
    t = (pref // mult) * mult
    while t >= mult:
        if n % t == 0:
            return t
        t -= mult
    return n


def _cparams(sem):
    return pltpu.CompilerParams(dimension_semantics=sem, vmem_limit_bytes=V7X_VMEM_LIMIT)


def _mm(name, a, b, *, grid, a_spec, b_spec, o_spec, out_shape, contract, ride=None, add=None):
    nk = grid[-1]
    in_out = out_shape.dtype == F32
    acc_shape = tuple(d for d in o_spec.block_shape if d is not None)

    def body(*refs):
        a_ref, b_ref = refs[0], refs[1]
        add_ref = refs[2] if add is not None else None
        o_ref = refs[3] if add is not None else refs[2]
        acc_ref = o_ref if (in_out or nk == 1) else refs[-1]
        k = pl.program_id(len(grid) - 1)
        part = lax.dot_general(a_ref[...].astype(BF16), b_ref[...].astype(BF16), (contract, ((), ())),
                               preferred_element_type=F32)
        if add_ref is not None:
            part = jnp.where(k == 0, part + add_ref[...], part) if nk > 1 else part + add_ref[...]
        if nk == 1:
            o_ref[...] = part.astype(o_ref.dtype)
            return

        @pl.when(k == 0)
        def _():
            acc_ref[...] = part

        @pl.when(k > 0)
        def _():
            acc_ref[...] += part

        if not in_out:
            @pl.when(k == nk - 1)
            def _():
                o_ref[...] = acc_ref[...].astype(o_ref.dtype)

    sem = ("parallel",) * (len(grid) - 1) + ("arbitrary",)
    scratch = [] if (in_out or nk == 1) else [pltpu.VMEM(acc_shape, F32)]
    in_specs, args = [a_spec, b_spec], [a, b]
    if add is not None:
        in_specs.append(o_spec)
        args.append(add)
    return _pcall(body, ride=ride, name=name, grid=grid, in_specs=in_specs, out_specs=o_spec,
                  out_shape=out_shape, scratch_shapes=scratch, compiler_params=_cparams(sem))(*args)


def _mm_nn(name, a, b, *, b_off=0, n=None, out_dtype=F32, tm=1024, tn=1024, tk=1024, ride=None):
    m, k = a.shape
    n = b.shape[1] if n is None else n
    tm, tn, tk = _tile(m, tm), _tile(n, tn, 128), _tile(k, tk, 128)
    no = b_off // tn
    return _mm(name, a, b, grid=(m // tm, n // tn, k // tk),
               a_spec=pl.BlockSpec((tm, tk), lambda i, j, kk: (i, kk)),
               b_spec=pl.BlockSpec((tk, tn), lambda i, j, kk: (kk, j + no)),
               o_spec=pl.BlockSpec((tm, tn), lambda i, j, kk: (i, j)),
               out_shape=jax.ShapeDtypeStruct((m, n), out_dtype), contract=NN, ride=ride)


def _mm_nt(name, a, b, *, out_dtype=F32, tm=1024, tn=1024, tk=1024, ride=None, add=None):
    m, k = a.shape
    n = b.shape[0]
    tm, tn, tk = _tile(m, tm), _tile(n, tn, 128), _tile(k, tk, 128)
    return _mm(name, a, b, grid=(m // tm, n // tn, k // tk),
               a_spec=pl.BlockSpec((tm, tk), lambda i, j, kk: (i, kk)),
               b_spec=pl.BlockSpec((tn, tk), lambda i, j, kk: (j, kk)),
               o_spec=pl.BlockSpec((tm, tn), lambda i, j, kk: (i, j)),
               out_shape=jax.ShapeDtypeStruct((m, n), out_dtype), contract=NT, ride=ride, add=add)


def _mm_tn(name, a, b, *, a_off=0, m=None, out_dtype=F32, tm=1024, tn=1024, tk=2048, ride=None):
    t, n = b.shape
    m = a.shape[1] if m is None else m
    tm, tn, tk = _tile(m, tm, 128), _tile(n, tn, 128), _tile(t, tk)
    mo = a_off // tm
    return _mm(name, a, b, grid=(m // tm, n // tn, t // tk),
               a_spec=pl.BlockSpec((tk, tm), lambda i, j, kk: (kk, i + mo)),
               b_spec=pl.BlockSpec((tk, tn), lambda i, j, kk: (kk, j)),
               o_spec=pl.BlockSpec((tm, tn), lambda i, j, kk: (i, j)),
               out_shape=jax.ShapeDtypeStruct((m, n), out_dtype), contract=TN, ride=ride)


def _rowwise(name, fn, rows, params, out_rows, out_reds, tm=512):
    rows = [r if isinstance(r, tuple) else (r, 0, r.shape[1]) for r in rows]
    t = rows[0][0].shape[0]
    tm = _tile(t, tm)
    in_specs = []
    for _, off, w in rows:
        in_specs.append(pl.BlockSpec((tm, w), functools.partial(lambda i, cb: (i, cb), cb=off // w)))
    for p in params:
        in_specs.append(pl.BlockSpec((1, p.shape[1]), lambda i: (0, 0)))
    out_specs = [pl.BlockSpec((tm, w), lambda i: (i, 0)) for w, _ in out_rows]
    out_specs += [pl.BlockSpec((1, w), lambda i: (0, 0)) for w in out_reds]
    out_shape = [jax.ShapeDtypeStruct((t, w), dt) for w, dt in out_rows]
    out_shape += [jax.ShapeDtypeStruct((1, w), F32) for w in out_reds]
    nr, npar, nor = len(rows), len(params), len(out_rows)

    def body(*refs):
        ins, outs = refs[:nr + npar], refs[nr + npar:]
        vals = [r[...].astype(F32) for r in ins[:nr]]
        vals += [jnp.broadcast_to(r[...], (tm, r.shape[1])) for r in ins[nr:]]
        res = fn(*vals)
        step = pl.program_id(0)
        for o, v in zip(outs[:nor], res[:nor]):
            o[...] = v.astype(o.dtype)
        for o, v in zip(outs[nor:], res[nor:]):
            _accumulate(o, v, step)

    res = _pcall(body, name=name, grid=(t // tm,), in_specs=in_specs, out_specs=out_specs, out_shape=out_shape,
                 compiler_params=_cparams(("arbitrary",)))(*[r[0] for r in rows], *params)
    return res


def _accumulate(o_ref, v, step):
    @pl.when(step == 0)
    def _():
        o_ref[...] = v

    @pl.when(step > 0)
    def _():
        o_ref[...] += v


def _colsum(v):
    return jnp.sum(v, axis=0, keepdims=True)


def _vjp_rowwise(name, fn, rows, params, cots, n_row_grads, tm=256, dtypes=None):
    nr, npar, nc = len(rows), len(params), len(cots)

    def bwd(*vals):
        prim, par, ct = vals[:nr], vals[nr + nc:], vals[nr:nr + nc]
        _, pull = jax.vjp(fn, *prim, *par)
        grads = pull(tuple(ct) if nc > 1 else ct[0])
        return tuple(grads[:n_row_grads]) + tuple(_colsum(g) for g in grads[nr:])

    dtypes = [F32] * n_row_grads if dtypes is None else dtypes
    widths = [(r[2] if isinstance(r, tuple) else r.shape[1], dt) for r, dt in zip(rows[:n_row_grads], dtypes)]
    return _rowwise(name, bwd, list(rows) + list(cots), params, widths, [p.shape[1] for p in params], tm=tm)


def _ln(s, g, b):
    mu = jnp.mean(s, axis=-1, keepdims=True)
    var = jnp.mean(jnp.square(s - mu), axis=-1, keepdims=True)
    return (s - mu) * lax.rsqrt(var + LN_EPS) * g + b


def _softplus(x):
    return jnp.maximum(x, 0.0) + jnp.log1p(jnp.exp(-jnp.abs(x)))


def _expm1(x):
    series = x * (1.0 + x * (1.0 / 2 + x * (1.0 / 6 + x * (1.0 / 24 + x * (1.0 / 120 + x * (1.0 / 720))))))
    return jnp.where(jnp.abs(x) < 0.25, series, jnp.exp(x) - 1.0)


def _f_resid_ln(h, branch, g, b):
    return _ln(ALPHA * h + branch, g, b)


def _f_ple(h, gp, pe, bpg, g, b):
    return _ln(ALPHA * h + jax.nn.sigmoid(gp + bpg) * pe, g, b)


def _f_merge(ga, gb, ya, yb, bm0, bm1):
    return jax.nn.sigmoid(ga + bm0) * ya + jax.nn.sigmoid(gb + bm1) * yb


def _f_rnn_out(hs, ry):
    return hs * jax.nn.gelu(ry, approximate=True)


def _f_logf(fl, bf):
    return -_softplus(-(fl + bf))


def _f_decay(lam):
    return -RG_C * _softplus(-lam)


def _f_gate(xc, ra, ia, decay, ba, bx):
    r = jax.nn.sigmoid(ra + ba)
    i = jax.nn.sigmoid(ia + bx)
    log_a = decay * r
    a = jnp.exp(log_a)
    mult = jnp.sqrt(-_expm1(2.0 * log_a))
    return a, mult * (i * xc)


def _f_act(hg, hu):
    return jax.nn.silu(hg) * hu


ATT_BLOCK = 512
ATT_HEADS_PER_STEP = 2
SCAN_ROWS = 128


def _scores(q, k, cq, ck, diagonal):
    s = lax.dot_general(q, k, (NT, ((), ())), preferred_element_type=F32) * SCALE
    s = s + cq - ck
    if diagonal:
        row = lax.broadcasted_iota(jnp.int32, s.shape, 0)
        col = lax.broadcasted_iota(jnp.int32, s.shape, 1)
        s = jnp.where(col <= row, s, NEG)
    return s


def _dscores(p, do, o, v):
    dob = do.astype(BF16)
    delta = jnp.sum(dob.astype(F32) * o, axis=1, keepdims=True)
    dp = lax.dot_general(dob, v.astype(BF16), (NT, ((), ())), preferred_element_type=F32)
    return p * (dp - delta)


def _attn_fwd(z, cq, ck, bsz, seq, ride=None):
    t = bsz * seq
    tq = _tile(seq, ATT_BLOCK)
    nq = seq // tq

    hp = ATT_HEADS_PER_STEP

    def body(q_ref, k_ref, v_ref, cq_ref, ck_ref, o_ref, ob_ref, lse_ref):
        for hh, i in [(hh, i) for hh in range(hp) for i in range(nq)]:
            lanes = slice(hh * HEAD_DIM, (hh + 1) * HEAD_DIM)
            rows = slice(i * tq, (i + 1) * tq)
            q = q_ref[rows, lanes].astype(BF16)
            cqi = cq_ref[hh, rows, :]

            def step(j, carry, diagonal, q=q, cqi=cqi, hh=hh, lanes=lanes):
                m, l, acc = carry
                keys = slice(j * tq, (j + 1) * tq)
                s = _scores(q, k_ref[keys, lanes].astype(BF16), cqi, ck_ref[hh, j:j + 1, :], diagonal)
                m_new = jnp.maximum(m, jnp.max(s, axis=1, keepdims=True))
                alpha = jnp.exp(m - m_new)
                p = jnp.exp(s - m_new)
                p_hi = p.astype(BF16)
                p_lo = (p - p_hi.astype(F32)).astype(BF16)
                vb = v_ref[keys, lanes].astype(BF16)
                pv = lax.dot_general(p_hi, vb, (NN, ((), ())), preferred_element_type=F32)
                pv = pv + lax.dot_general(p_lo, vb, (NN, ((), ())), preferred_element_type=F32)
                return m_new, alpha * l + jnp.sum(p, axis=1, keepdims=True), alpha * acc + pv

            carry = (jnp.full((tq, 1), NEG, F32), jnp.zeros((tq, 1), F32), jnp.zeros((tq, HEAD_DIM), F32))
            for j in range(i):
                carry = step(j, carry, False)
            m, l, acc = step(i, carry, True)
            o = acc / l
            o_ref[rows, lanes] = o
            ob_ref[rows, lanes] = o.astype(BF16)
            lse_ref[hh, rows, :] = m + jnp.log(l)

    groups = N_HEADS // hp
    head = (seq, hp * HEAD_DIM)
    in_specs = [
        pl.BlockSpec(head, lambda b, g: (b, g)),
        pl.BlockSpec(head, lambda b, g: (b, groups + g)),
        pl.BlockSpec(head, lambda b, g: (b, 2 * groups + g)),
        pl.BlockSpec((None, hp, seq, 1), lambda b, g: (b, g, 0, 0)),
        pl.BlockSpec((None, hp, nq, tq), lambda b, g: (b, g, 0, 0)),
    ]
    out_specs = [pl.BlockSpec(head, lambda b, g: (b, g)), pl.BlockSpec(head, lambda b, g: (b, g)),
                 pl.BlockSpec((None, hp, seq, 1), lambda b, g: (b, g, 0, 0))]
    out_shape = [jax.ShapeDtypeStruct((t, D_MODEL), F32), jax.ShapeDtypeStruct((t, D_MODEL), BF16),
                 jax.ShapeDtypeStruct((bsz, N_HEADS, seq, 1), F32)]
    return _pcall(body, ride=ride, name="attn_fwd", grid=(bsz, groups), in_specs=in_specs, out_specs=out_specs,
                  out_shape=out_shape, compiler_params=_cparams(("parallel", "parallel")))(
                      z, z, z, cq, ck.reshape(bsz, N_HEADS, nq, tq))


def _attn_bwd(z, att, datt, lse, cq, ck, bsz, seq, ride=None):
    t = bsz * seq
    tq = _tile(seq, ATT_BLOCK)
    nq = seq // tq

    def body(q_ref, k_ref, v_ref, o_ref, do_ref, lse_ref, cq_ref, ck_ref,
             dq_ref, dk_ref, dv_ref, dcq_ref, dck_ref, dq_sc):
        dq_sc[...] = jnp.zeros_like(dq_sc)
        dcq_ref[...] = jnp.zeros_like(dcq_ref)
        for j in range(nq):
            keys = slice(j * tq, (j + 1) * tq)
            kb = k_ref[keys, :].astype(BF16)
            vb = v_ref[keys, :].astype(BF16)
            ckj = ck_ref[j:j + 1, :]

            def step(i, carry, diagonal, kb=kb, vb=vb, ckj=ckj):
                dk, dv, dc = carry
                rows = slice(i * tq, (i + 1) * tq)
                qb = q_ref[rows, :].astype(BF16)
                do = do_ref[rows, :]
                s = _scores(qb, kb, cq_ref[rows, :], ckj, diagonal)
                p = jnp.exp(s - lse_ref[rows, :])
                ds = _dscores(p, do, o_ref[rows, :], vb)
                dsb = (ds * SCALE).astype(BF16)
                dq_sc[rows, :] += lax.dot_general(dsb, kb, (NN, ((), ())), preferred_element_type=F32)
                dcq_ref[rows, :] += jnp.sum(ds, axis=1, keepdims=True)
                dv = dv + lax.dot_general(p.astype(BF16), do.astype(BF16), (TN, ((), ())),
                                          preferred_element_type=F32)
                dk = dk + lax.dot_general(dsb, qb, (TN, ((), ())), preferred_element_type=F32)
                return dk, dv, dc - jnp.sum(ds, axis=0, keepdims=True)

            zero = jnp.zeros((tq, HEAD_DIM), F32)
            carry = step(j, (zero, zero, jnp.zeros((1, tq), F32)), True)
            for i in range(j + 1, nq):
                carry = step(i, carry, False)
            dk, dv, dck_ref[j:j + 1, :] = carry
            dk_ref[keys, :] = dk.astype(BF16)
            dv_ref[keys, :] = dv.astype(BF16)
        dq_ref[...] = dq_sc[...].astype(BF16)

    head = (seq, HEAD_DIM)
    hmap = lambda b, h: (b, h)
    col = pl.BlockSpec((None, None, seq, 1), lambda b, h: (b, h, 0, 0))
    row = pl.BlockSpec((None, None, nq, tq), lambda b, h: (b, h, 0, 0))
    in_specs = [pl.BlockSpec(head, hmap),
                pl.BlockSpec(head, lambda b, h: (b, N_HEADS + h)),
                pl.BlockSpec(head, lambda b, h: (b, 2 * N_HEADS + h)),
                pl.BlockSpec(head, hmap), pl.BlockSpec(head, hmap), col, col, row]
    big = jax.ShapeDtypeStruct((t, D_MODEL), BF16)
    return _pcall(body, ride=ride, name="attn_bwd", grid=(bsz, N_HEADS), in_specs=in_specs,
                  out_specs=[pl.BlockSpec(head, hmap)] * 3 + [col, row],
                  out_shape=[big, big, big, jax.ShapeDtypeStruct((bsz, N_HEADS, seq, 1), F32),
                             jax.ShapeDtypeStruct((bsz, N_HEADS, nq, tq), F32)],
                  scratch_shapes=[pltpu.VMEM(head, F32)],
                  compiler_params=_cparams(("parallel", "parallel")))(
                      z, z, z, att, datt, lse, cq, ck.reshape(bsz, N_HEADS, nq, tq))


def _scan(name, a, u, bsz, seq, *, reverse, with_prev=False, tb=512, ride=None):
    c = u.shape[1]
    tb = _tile(seq, tb)
    nb = seq // tb
    rc = SCAN_ROWS if tb % SCAN_ROWS == 0 else tb
    has_a = a is not None

    def body(*refs):
        if has_a:
            a_ref, u_ref = refs[0], refs[1]
            rest = refs[2:]
        else:
            u_ref = refs[0]
            rest = refs[1:]
        outs = rest[:2] if with_prev else rest[:1]
        carry_sc, afirst_sc = rest[-2], rest[-1]
        step = pl.program_id(1)

        @pl.when(step == 0)
        def _():
            carry_sc[...] = jnp.zeros_like(carry_sc)
            afirst_sc[...] = jnp.zeros_like(afirst_sc)

        row = lax.broadcasted_iota(jnp.int32, (rc, BLK), 0)
        pieces = list(range(tb // rc))
        for ls in range(c // BLK):
            lanes = slice(ls * BLK, (ls + 1) * BLK)
            carry = carry_sc[:, lanes]
            afirst = afirst_sc[:, lanes]
            for pc in (reversed(pieces) if reverse else pieces):
                rows = slice(pc * rc, (pc + 1) * rc)
                uu = u_ref[rows, lanes]
                if has_a:
                    aa = a_ref[rows, lanes]
                    coef = jnp.where(row < rc - 1, pltpu.roll(aa, rc - 1, 0), afirst) if reverse else aa
                k = 1
                while k < rc:
                    shift = rc - k if reverse else k
                    keep = (row < rc - k) if reverse else (row >= k)
                    uu_sh = jnp.where(keep, pltpu.roll(uu, shift, 0), 0.0)
                    if has_a:
                        uu = coef * uu_sh + uu
                        coef = coef * jnp.where(keep, pltpu.roll(coef, shift, 0), 1.0)
                    else:
                        uu = uu + uu_sh
                    k *= 2
                h = uu + coef * carry if has_a else uu + carry
                outs[0][rows, lanes] = h
                if with_prev:
                    outs[1][rows, lanes] = jnp.where(row >= 1, pltpu.roll(h, 1, 0), carry)
                edge = pc * rc if reverse else (pc + 1) * rc - 1
                carry = outs[0][edge:edge + 1, lanes]
                if has_a and reverse:
                    afirst = a_ref[edge:edge + 1, lanes]
            carry_sc[:, lanes] = carry
            if has_a and reverse:
                afirst_sc[:, lanes] = afirst

    if reverse:
        imap = lambda b, s: (b * nb + nb - 1 - s, 0)
    else:
        imap = lambda b, s: (b * nb + s, 0)
    spec = pl.BlockSpec((tb, c), imap)
    n_in = 2 if has_a else 1
    n_out = 2 if with_prev else 1
    res = _pcall(body, ride=ride, name=name, grid=(bsz, nb), in_specs=[spec] * n_in, out_specs=[spec] * n_out,
                 out_shape=[jax.ShapeDtypeStruct(u.shape, F32)] * n_out,
                 scratch_shapes=[pltpu.VMEM((1, c), F32), pltpu.VMEM((1, c), F32)],
                 compiler_params=_cparams(("parallel", "arbitrary")))(*([a, u] if has_a else [u]))
    return res if with_prev else res[0]


def _conv_fwd(z, w, b, bsz, seq, tb=512):
    c = D_MODEL
    t = bsz * seq
    tb = _tile(seq, tb)
    nb = seq // tb

    def body(x_ref, w_ref, b_ref, o_ref, tail_sc):
        step = pl.program_id(1)

        @pl.when(step == 0)
        def _():
            tail_sc[...] = jnp.zeros_like(tail_sc)

        x = x_ref[...]
        row8 = lax.broadcasted_iota(jnp.int32, (8, c), 0)
        tail = tail_sc[...]
        acc = w_ref[CONV_W - 1:CONV_W, :] * x + b_ref[...]
        for sh in range(1, CONV_W):
            xs = pltpu.roll(x, sh, 0)
            top = jnp.where(row8 < sh, pltpu.roll(tail, sh, 0), xs[0:8, :])
            xs = jnp.concatenate([top, xs[8:, :]], axis=0) if tb > 8 else top
            acc = acc + w_ref[CONV_W - 1 - sh:CONV_W - sh, :] * xs
        o_ref[...] = acc
        tail_sc[...] = x_ref[tb - 8:tb, :]

    return _pcall(body, name="conv_fwd", grid=(bsz, nb),
                  in_specs=[pl.BlockSpec((tb, c), lambda bb, s: (bb * nb + s, OFF_RX // c)),
                            pl.BlockSpec((CONV_W, c), lambda bb, s: (0, 0)),
                            pl.BlockSpec((1, c), lambda bb, s: (0, 0))],
                  out_specs=pl.BlockSpec((tb, c), lambda bb, s: (bb * nb + s, 0)),
                  out_shape=jax.ShapeDtypeStruct((t, c), F32),
                  scratch_shapes=[pltpu.VMEM((8, c), F32)],
                  compiler_params=_cparams(("parallel", "arbitrary")))(z, w, b)


def _conv_bwd(z, dxc, w, bsz, seq, tb=512):
    c = D_MODEL
    t = bsz * seq
    tb = _tile(seq, tb)
    nb = seq // tb

    def body(x_ref, g_ref, w_ref, dx_ref, dw_ref, db_ref, head_sc):
        bb, step = pl.program_id(0), pl.program_id(1)

        @pl.when(step == 0)
        def _():
            head_sc[...] = jnp.zeros_like(head_sc)

        x, g = x_ref[...], g_ref[...]
        row8 = lax.broadcasted_iota(jnp.int32, (8, c), 0)
        head = head_sc[...]
        dx = w_ref[CONV_W - 1:CONV_W, :] * g
        dws = [None] * CONV_W
        dws[CONV_W - 1] = _colsum(g * x)
        for sh in range(1, CONV_W):
            gs = pltpu.roll(g, tb - sh, 0)
            bot = jnp.where(row8 >= 8 - sh, pltpu.roll(head, 8 - sh, 0), gs[tb - 8:tb, :])
            gs = jnp.concatenate([gs[:tb - 8, :], bot], axis=0) if tb > 8 else bot
            dx = dx + w_ref[CONV_W - 1 - sh:CONV_W - sh, :] * gs
            dws[CONV_W - 1 - sh] = _colsum(gs * x)
        dx_ref[...] = dx.astype(dx_ref.dtype)
        first = (bb == 0) & (step == 0)
        dw = jnp.concatenate(dws, axis=0)
        db = _colsum(g)

        @pl.when(first)
        def _():
            dw_ref[...] = dw
            db_ref[...] = db

        @pl.when(jnp.logical_not(first))
        def _():
            dw_ref[...] += dw
            db_ref[...] += db

        head_sc[...] = g_ref[0:8, :]

    rmap = lambda bb, s: (bb * nb + nb - 1 - s, 0)
    return _pcall(body, name="conv_bwd", grid=(bsz, nb),
                  in_specs=[pl.BlockSpec((tb, c), lambda bb, s: (bb * nb + nb - 1 - s, OFF_RX // c)),
                            pl.BlockSpec((tb, c), rmap),
                            pl.BlockSpec((CONV_W, c), lambda bb, s: (0, 0))],
                  out_specs=[pl.BlockSpec((tb, c), rmap),
                             pl.BlockSpec((CONV_W, c), lambda bb, s: (0, 0)),
                             pl.BlockSpec((1, c), lambda bb, s: (0, 0))],
                  out_shape=[jax.ShapeDtypeStruct((t, c), BF16), jax.ShapeDtypeStruct((CONV_W, c), F32),
                             jax.ShapeDtypeStruct((1, c), F32)],
                  scratch_shapes=[pltpu.VMEM((8, c), F32)],
                  compiler_params=_cparams(("arbitrary", "arbitrary")))(z, dxc, w)


def _gate_fwd(xc, w_a, w_x, b_a, b_x, lam, tm=1024, ride=None):
    t = xc.shape[0]
    tm = _tile(t, tm)

    def body(xc_ref, wa_ref, wx_ref, ba_ref, bx_ref, lam_ref, a_ref, u_ref):
        xc_b = xc_ref[...]
        xb = xc_b.astype(BF16)
        ra = lax.dot_general(xb, wa_ref[...].astype(BF16), (NN, ((), ())), preferred_element_type=F32)
        ia = lax.dot_general(xb, wx_ref[...].astype(BF16), (NN, ((), ())), preferred_element_type=F32)
        a, u = _f_gate(xc_b, ra, ia, lam_ref[...], ba_ref[...], bx_ref[...])
        a_ref[...] = a
        u_ref[...] = u

    row = pl.BlockSpec((tm, BLK), lambda n, i: (i, n))
    wsp = pl.BlockSpec((None, BLK, BLK), lambda n, i: (n, 0, 0))
    vec = pl.BlockSpec((1, BLK), lambda n, i: (0, n))
    return _pcall(body, ride=ride, name="gate_fwd", grid=(N_BLK, t // tm), in_specs=[row, wsp, wsp, vec, vec, vec],
                  out_specs=[row, row], out_shape=[jax.ShapeDtypeStruct((t, D_MODEL), F32)] * 2,
                  compiler_params=_cparams(("parallel", "parallel")))(xc, w_a, w_x, b_a, b_x, lam)


def _gate_bwd(xc, w_a, w_x, b_a, b_x, lam, hprev, du, tm=1024, ride=None):
    t = xc.shape[0]
    tm = _tile(t, tm)

    def body(xc_ref, wa_ref, wx_ref, ba_ref, bx_ref, lam_ref, hp_ref, du_ref,
             dxc_ref, dwa_ref, dwx_ref, dba_ref, dbx_ref, dlam_ref):
        step = pl.program_id(1)
        xc_b = xc_ref[...]
        xb = xc_b.astype(BF16)
        wa, wx = wa_ref[...].astype(BF16), wx_ref[...].astype(BF16)
        ra = lax.dot_general(xb, wa, (NN, ((), ())), preferred_element_type=F32)
        ia = lax.dot_general(xb, wx, (NN, ((), ())), preferred_element_type=F32)
        full = lambda r: jnp.broadcast_to(r[...], (tm, BLK))
        _, pull = jax.vjp(_f_gate, xc_b, ra, ia, full(lam_ref), full(ba_ref), full(bx_ref))
        du_b = du_ref[...]
        dxc, dra, dia, dlam, dba, dbx = pull((du_b * hp_ref[...], du_b))
        drb, dib = dra.astype(BF16), dia.astype(BF16)
        dxc = dxc + lax.dot_general(drb, wa, (NT, ((), ())), preferred_element_type=F32)
        dxc = dxc + lax.dot_general(dib, wx, (NT, ((), ())), preferred_element_type=F32)
        dxc_ref[...] = dxc
        _accumulate(dwa_ref, lax.dot_general(xb, drb, (TN, ((), ())), preferred_element_type=F32), step)
        _accumulate(dwx_ref, lax.dot_general(xb, dib, (TN, ((), ())), preferred_element_type=F32), step)
        _accumulate(dba_ref, _colsum(dba), step)
        _accumulate(dbx_ref, _colsum(dbx), step)
        _accumulate(dlam_ref, _colsum(dlam), step)

    row = pl.BlockSpec((tm, BLK), lambda n, i: (i, n))
    wsp = pl.BlockSpec((None, BLK, BLK), lambda n, i: (n, 0, 0))
    vec = pl.BlockSpec((1, BLK), lambda n, i: (0, n))
    wshape = jax.ShapeDtypeStruct((N_BLK, BLK, BLK), F32)
    vshape = jax.ShapeDtypeStruct((1, D_MODEL), F32)
    return _pcall(body, ride=ride, name="gate_bwd", grid=(N_BLK, t // tm),
                  in_specs=[row, wsp, wsp, vec, vec, vec, row, row],
                  out_specs=[row, wsp, wsp, vec, vec, vec],
                  out_shape=[jax.ShapeDtypeStruct((t, D_MODEL), F32), wshape, wshape, vshape, vshape, vshape],
                  compiler_params=_cparams(("parallel", "arbitrary")))(xc, w_a, w_x, b_a, b_x, lam, hprev, du)


def _ffn_in_act(a, w, tm=1024, ride=None):
    t = a.shape[0]
    tm = _tile(t, tm)

    def body(a_ref, wg_ref, wu_ref, hgu_ref, act_ref):
        ab = a_ref[...].astype(BF16)
        hg = lax.dot_general(ab, wg_ref[...].astype(BF16), (NN, ((), ())), preferred_element_type=F32)
        hu = lax.dot_general(ab, wu_ref[...].astype(BF16), (NN, ((), ())), preferred_element_type=F32)
        hgu_ref[0] = hg
        hgu_ref[1] = hu
        act_ref[...] = _f_act(hg, hu).astype(act_ref.dtype)

    wspec = lambda off: pl.BlockSpec((None, D_MODEL, FF_SH), lambda i, s: (s + off, 0, 0))
    hgu, act = _pcall(body, ride=ride, name="ffn_in", grid=(t // tm, N_FF),
                      in_specs=[pl.BlockSpec((tm, D_MODEL), lambda i, s: (i, 0)), wspec(0), wspec(N_FF)],
                      out_specs=[pl.BlockSpec((2, None, tm, FF_SH), lambda i, s: (0, s, i, 0)),
                                 pl.BlockSpec((None, tm, FF_SH), lambda i, s: (s, i, 0))],
                      out_shape=[jax.ShapeDtypeStruct((2, N_FF, t, FF_SH), F32),
                                 jax.ShapeDtypeStruct((N_FF, t, FF_SH), BF16)],
                      compiler_params=_cparams(("parallel", "parallel")))(a, w, w)
    return hgu.reshape(2 * N_FF, t, FF_SH), act


def _ffn_out_dx_act(d, w, hgu, tm=1024, ride=None):
    t = d.shape[0]
    tm = _tile(t, tm)

    def body(d_ref, w_ref, hg_ref, hu_ref, o_ref):
        dact = lax.dot_general(d_ref[...].astype(BF16), w_ref[...].astype(BF16), (NT, ((), ())),
                               preferred_element_type=F32)
        _, pull = jax.vjp(_f_act, hg_ref[...], hu_ref[...])
        dhg, dhu = pull(dact)
        o_ref[0] = dhg.astype(o_ref.dtype)
        o_ref[1] = dhu.astype(o_ref.dtype)

    hspec = lambda off: pl.BlockSpec((None, tm, FF_SH), lambda i, s: (s + off, i, 0))
    res = _pcall(body, ride=ride, name="ffn_out_dx", grid=(t // tm, N_FF),
                 in_specs=[pl.BlockSpec((tm, D_MODEL), lambda i, s: (i, 0)),
                           pl.BlockSpec((None, FF_SH, D_MODEL), lambda i, s: (s, 0, 0)), hspec(0), hspec(N_FF)],
                 out_specs=pl.BlockSpec((2, None, tm, FF_SH), lambda i, s: (0, s, i, 0)),
                 out_shape=jax.ShapeDtypeStruct((2, N_FF, t, FF_SH), BF16),
                 compiler_params=_cparams(("parallel", "parallel")))(d, w, hgu, hgu)
    return res.reshape(2 * N_FF, t, FF_SH)


def _adamw(name, parts, w, m, v, tr=128, ride=None):
    ng = len(parts)
    n_src, r, c = parts[0].shape
    per = w.shape[1] // r
    assert w.shape[0] * per == ng and w.shape[2] == c
    tr = _tile(r, tr)
    nb = r // tr
    bc1 = 1.0 - ADAM_B1 ** ADAM_STEP
    bc2 = 1.0 - ADAM_B2 ** ADAM_STEP

    def body(*refs):
        p_refs = refs[:ng]
        w_ref, m_ref, v_ref, g_ref, d_ref, nm_ref, nv_ref = refs[ng:]
        grp = pl.program_id(0)

        def update(p_ref):
            g = p_ref[0].astype(F32)
            for s in range(1, n_src):
                g = g + p_ref[s].astype(F32)
            nm = ADAM_B1 * m_ref[...] + (1.0 - ADAM_B1) * g
            nv = ADAM_B2 * v_ref[...] + (1.0 - ADAM_B2) * jnp.square(g)
            g_ref[...] = g
            nm_ref[...] = nm
            nv_ref[...] = nv
            d_ref[...] = -ADAM_LR * ((nm / bc1) / (jnp.sqrt(nv / bc2) + ADAM_EPS) + ADAM_WD * w_ref[...])

        for k in range(ng):
            pl.when(grp == k)(functools.partial(update, p_refs[k]))

    p_specs = [pl.BlockSpec((n_src, tr, c), functools.partial(lambda gi, i, k: (0, jnp.where(gi == k, i, 0), 0), k=k))
               for k in range(ng)]
    spec = pl.BlockSpec((None, tr, c), lambda gi, i: (gi // per, (gi % per) * nb + i, 0))
    return _pcall(body, ride=ride, name=name, grid=(ng, nb), in_specs=p_specs + [spec, spec, spec],
                  out_specs=[spec] * 4, out_shape=[jax.ShapeDtypeStruct(w.shape, F32)] * 4,
                  compiler_params=_cparams(("parallel", "parallel")))(*parts, w, m, v)


def _sum_parts(name, parts, tr=256):
    _, r, c = parts.shape
    tr = _tile(r, tr)

    def body(p_ref, o_ref):
        g = p_ref[0]
        for s in range(1, parts.shape[0]):
            g = g + p_ref[s]
        o_ref[...] = g

    return _pcall(body, name=name, grid=(r // tr,),
                  in_specs=[pl.BlockSpec((parts.shape[0], tr, c), lambda i: (0, i, 0))],
                  out_specs=pl.BlockSpec((tr, c), lambda i: (i, 0)),
                  out_shape=jax.ShapeDtypeStruct((r, c), F32), compiler_params=_cparams(("parallel",)))(parts)


def _peer(k):
    x, y, c = lax.axis_index("x"), lax.axis_index("y"), lax.axis_index("c")
    return (x ^ ((k >> 2) & 1), y ^ ((k >> 1) & 1), c ^ (k & 1))


def _my_id():
    return 4 * lax.axis_index("x") + 2 * lax.axis_index("y") + lax.axis_index("c")


def _exchange(name, ride):
    n = len(ride.arrays)

    def body(*refs):
        ride.begin(refs[:n], refs[n:2 * n], refs[2 * n:])
        ride.finish(refs[:n], refs[n:2 * n], refs[2 * n:])

    hbm = pl.BlockSpec(memory_space=pltpu.HBM)
    return _pcall(body, name=name, in_specs=[hbm] * n, out_specs=[hbm] * n, out_shape=ride.out_shapes(),
                  scratch_shapes=ride.scratch())(*ride.arrays)


def _row(v):
    return v.reshape(1, -1)


def _time_major_heads(c, bsz, seq):
    return c.reshape(bsz, seq, BLK)[:, :, :N_HEADS].transpose(0, 2, 1)


def _no_ride(*_):
    return None


TWICE = [(D_MODEL, F32), (D_MODEL, BF16)]


def _both(fn):
    def run(*v):
        y = fn(*v)
        return y, y
    return run


def _layer_fwd(h, hb, p_l, w, bsz, seq, ride_of=_no_ride):
    t = bsz * seq
    zq = _mm_nn("z_proj_qkv", hb, w['w_in7'], n=QKV, out_dtype=BF16, ride=ride_of('z_proj_qkv'))
    zr = _mm_nn("z_proj_rest", hb, w['w_in7'], b_off=QKV, n=4 * D_MODEL, ride=ride_of('z_proj_rest'))
    fl = _mm_nn("f_proj", hb, w['w_inf'])
    logf, = _rowwise("logf_fwd", lambda f, b: (_f_logf(f, b),), [fl], [w['b_forget']], [(BLK, F32)], [])
    c = _scan("cumsum_fwd", None, logf, bsz, seq, reverse=False)
    ct = _time_major_heads(c, bsz, seq)
    cq, ck = ct[..., None], ct[:, :, None, :]
    att, attb, lse = _attn_fwd(zq, cq, ck, bsz, seq, ride=ride_of('attn_fwd'))
    xc = _conv_fwd(zr, w['conv_w'], w['conv_b'], bsz, seq)
    decay, = _rowwise("decay_fwd", lambda lam: (_f_decay(lam),), [w['rg_lambda']], [], [(D_MODEL, F32)], [])
    a, u = _gate_fwd(xc, w['rg_w_a'], w['rg_w_x'], w['rg_b_a'], w['rg_b_x'], decay, ride=ride_of('gate_fwd'))
    hs, hprev = _scan("lru_fwd", a, u, bsz, seq, reverse=False, with_prev=True, ride=ride_of('lru_fwd'))
    rnn, = _rowwise("rnn_out_fwd", lambda s, y: (_f_rnn_out(s, y),), [hs, (zr, OFF_RY, D_MODEL)], [],
                    [(D_MODEL, BF16)], [])
    ya = _mm_nn("branch_att", attb, w['w_branch_att'])
    yb = _mm_nn("branch_rnn", rnn, w['w_branch_rnn'])
    merged, = _rowwise("merge_fwd", lambda *v: (_f_merge(*v),),
                       [(zr, OFF_GA, D_MODEL), (zr, OFF_GB, D_MODEL), ya, yb], [w['b_merge0'], w['b_merge1']],
                       [(D_MODEL, BF16)], [])
    mix = _mm_nn("mix_out", merged, w['w_out'])
    h1, h1b = _rowwise("ln_mix_fwd", _both(_f_resid_ln), [h, mix], [w['ln_mix_g'], w['ln_mix_b']], TWICE, [])
    tm = _tile(t, 1024)
    hgu, act = _ffn_in_act(h1b, w['w_ffn_in'], ride=ride_of('ffn_in'))
    ffn = _mm("ffn_out", act, w['w_ffn_out'], grid=(t // tm, 1, N_FF),
              a_spec=pl.BlockSpec((None, tm, FF_SH), lambda i, j, s: (s, i, 0)),
              b_spec=pl.BlockSpec((None, FF_SH, D_MODEL), lambda i, j, s: (s, 0, 0)),
              o_spec=pl.BlockSpec((tm, D_MODEL), lambda i, j, s: (i, 0)),
              out_shape=jax.ShapeDtypeStruct((t, D_MODEL), F32), contract=NN, ride=ride_of('ffn_out'))
    h2, h2b = _rowwise("ln_ffn_fwd", _both(_f_resid_ln), [h1, ffn], [w['ln_ffn_g'], w['ln_ffn_b']], TWICE, [])
    gp = _mm_nn("ple_gate", h2b, w['w_ple_gate'])
    pe = _mm_nn("ple_proj", p_l, w['w_ple'])
    h3, h3b = _rowwise("ln_ple_fwd", _both(_f_ple), [h2, gp, pe],
                       [w['b_ple_gate'], w['ln_ple_g'], w['ln_ple_b']], TWICE, [])
    saved = dict(h=h, hb=hb, zq=zq, zr=zr, fl=fl, cq=cq, ck=ck, att=att, attb=attb, lse=lse, xc=xc, a=a, decay=decay,
                 hprev=hprev, hs=hs, rnn=rnn, ya=ya, yb=yb, merged=merged, mix=mix, h1=h1, h1b=h1b, hgu=hgu,
                 act=act, ffn=ffn, h2=h2, h2b=h2b, gp=gp, pe=pe)
    return h3, h3b, saved


def _layer_bwd(dh3, p_l, w, s, bsz, seq, ride_of=_no_ride):
    t = bsz * seq
    g = {}
    dh2, dgp, dpe, g['b_ple_gate'], g['ln_ple_g'], g['ln_ple_b'] = _vjp_rowwise(
        "ln_ple_bwd", _f_ple, [s['h2'], s['gp'], s['pe']], [w['b_ple_gate'], w['ln_ple_g'], w['ln_ple_b']], [dh3], 3,
        dtypes=[F32, BF16, BF16])
    g['w_ple_gate'] = _mm_tn("ple_gate_dw", s['h2b'], dgp, out_dtype=BF16)
    g['w_ple'] = _mm_tn("ple_proj_dw", p_l, dpe, out_dtype=BF16)
    dh2b = _mm_nt("ple_gate_dx", dgp, w['w_ple_gate'])
    dh1, dffn, g['ln_ffn_g'], g['ln_ffn_b'] = _ln_resid_bwd(
        "ln_ffn_bwd", s['h1'], s['ffn'], w['ln_ffn_g'], w['ln_ffn_b'], dh2, dh2b)
    tm = _tile(t, 1024)
    tk = _tile(t, 2048)
    g['w_ffn_out'] = _mm("ffn_out_dw", s['act'], dffn, grid=(N_FF, 1, t // tk),
                         a_spec=pl.BlockSpec((None, tk, FF_SH), lambda ss, j, k: (ss, k, 0)),
                         b_spec=pl.BlockSpec((tk, D_MODEL), lambda ss, j, k: (k, 0)),
                         o_spec=pl.BlockSpec((None, FF_SH, D_MODEL), lambda ss, j, k: (ss, 0, 0)),
                         out_shape=jax.ShapeDtypeStruct((N_FF, FF_SH, D_MODEL), BF16), contract=TN)
    dhgu = _ffn_out_dx_act(dffn, w['w_ffn_out'], s['hgu'], ride=ride_of('ffn_out_dx', g))
    g['w_ffn_in'] = _mm("ffn_in_dw", s['h1b'], dhgu, grid=(2 * N_FF, 1, t // tk),
                        a_spec=pl.BlockSpec((tk, D_MODEL), lambda ss, j, k: (k, 0)),
                        b_spec=pl.BlockSpec((None, tk, FF_SH), lambda ss, j, k: (ss, k, 0)),
                        o_spec=pl.BlockSpec((None, D_MODEL, FF_SH), lambda ss, j, k: (ss, 0, 0)),
                        out_shape=jax.ShapeDtypeStruct((2 * N_FF, D_MODEL, FF_SH), BF16), contract=TN,
                        ride=ride_of('ffn_in_dw', g))
    dh1b = _mm("ffn_in_dx", dhgu, w['w_ffn_in'], grid=(t // tm, 1, 2 * N_FF),
               a_spec=pl.BlockSpec((None, tm, FF_SH), lambda i, j, ss: (ss, i, 0)),
               b_spec=pl.BlockSpec((None, D_MODEL, FF_SH), lambda i, j, ss: (ss, 0, 0)),
               o_spec=pl.BlockSpec((tm, D_MODEL), lambda i, j, ss: (i, 0)),
               out_shape=jax.ShapeDtypeStruct((t, D_MODEL), F32), contract=NT, ride=ride_of('ffn_in_dx', g))
    dh, dmix, g['ln_mix_g'], g['ln_mix_b'] = _ln_resid_bwd(
        "ln_mix_bwd", s['h'], s['mix'], w['ln_mix_g'], w['ln_mix_b'], dh1, dh1b)
    g['w_out'] = _mm_tn("mix_out_dw", s['merged'], dmix, out_dtype=BF16)
    dmerged = _mm_nt("mix_out_dx", dmix, w['w_out'])
    z = s['zr']
    dga, dgb, dya, dyb, dbm0, dbm1 = _vjp_rowwise(
        "merge_bwd", _f_merge, [(z, OFF_GA, D_MODEL), (z, OFF_GB, D_MODEL), s['ya'], s['yb']],
        [w['b_merge0'], w['b_merge1']], [dmerged], 4, dtypes=[BF16] * 4)
    g['b_merge'] = jnp.concatenate([dbm0, dbm1], axis=0)
    g['w_branch_att'] = _mm_tn("branch_att_dw", s['attb'], dya, out_dtype=BF16)
    g['w_branch_rnn'] = _mm_tn("branch_rnn_dw", s['rnn'], dyb, out_dtype=BF16)
    datt = _mm_nt("branch_att_dx", dya, w['w_branch_att'], out_dtype=BF16)
    drnn = _mm_nt("branch_rnn_dx", dyb, w['w_branch_rnn'])
    dhs, dry = _vjp_rowwise("rnn_out_bwd", _f_rnn_out, [s['hs'], (z, OFF_RY, D_MODEL)], [], [drnn], 2,
                            dtypes=[F32, BF16])
    lam = _scan("lru_bwd", s['a'], dhs, bsz, seq, reverse=True)
    dxc, g['rg_w_a'], g['rg_w_x'], g['rg_b_a'], g['rg_b_x'], ddecay = _gate_bwd(
        s['xc'], w['rg_w_a'], w['rg_w_x'], w['rg_b_a'], w['rg_b_x'], s['decay'], s['hprev'], lam,
        ride=ride_of('gate_bwd', g))
    g['rg_lambda'], = _vjp_rowwise("decay_bwd", _f_decay, [w['rg_lambda']], [], [ddecay], 1)
    drx, g['conv_w'], g['conv_b'] = _conv_bwd(z, dxc, w['conv_w'], bsz, seq)
    dq, dk, dv, dcq, dck = _attn_bwd(s['zq'], s['att'], datt, s['lse'], s['cq'], s['ck'], bsz, seq,
                                     ride=ride_of('attn_bwd', g))
    dc = (dcq[:, :, :, 0] + dck.reshape(bsz, N_HEADS, seq)).transpose(0, 2, 1)
    dc = jnp.pad(dc, ((0, 0), (0, 0), (0, BLK - N_HEADS))).reshape(t, BLK)
    dlogf = _scan("cumsum_bwd", None, dc, bsz, seq, reverse=True)
    dfl, g['b_forget'] = _vjp_rowwise("logf_bwd", _f_logf, [s['fl']], [w['b_forget']], [dlogf], 1, dtypes=[BF16])
    dz = jnp.concatenate([dq, dk, dv, drx, dry, dga, dgb], axis=1)
    g['w_in7'] = _mm_tn("z_proj_dw", s['hb'], dz, out_dtype=BF16)
    g['w_inf'] = _mm_tn("f_proj_dw", s['hb'], dfl, out_dtype=BF16)
    dh = _mm_nt("z_proj_dx", dz, w['w_in7'], ride=ride_of('z_proj_dx', g), add=dh)
    dh = _mm_nt("f_proj_dx", dfl, w['w_inf'], add=dh)
    return dh, g


def _ln_resid_bwd(name, h, branch, gam, bet, d0, d1):
    def bwd(hv, bv, d0v, d1v, gv, btv):
        _, pull = jax.vjp(_f_resid_ln, hv, bv, gv, btv)
        dh, db, dg, dbt = pull(d0v + d1v)
        return dh, db, _colsum(dg), _colsum(dbt)

    return _rowwise(name, bwd, [h, branch, d0, d1], [gam, bet], [(D_MODEL, F32), (D_MODEL, BF16)],
                    [D_MODEL, D_MODEL], tm=256)


class _Schedule:
    FWD = {'z_proj_qkv': ['w_ffn_out'], 'z_proj_rest': ['w_branch_att', 'w_branch_rnn', 'w_out', 'w_ple_gate'],
           'attn_fwd': ['w_in'], 'ffn_in': ['w_ffn_in'], 'ffn_out': ['w_ple', 'conv_w', 'b_merge']}
    FIRST = ['w_in', 'conv_w', 'b_merge']
    OWN = {'z_proj_qkv': ['w_branch_att', 'w_branch_rnn', 'w_out'], 'z_proj_rest': ['w_ffn_in'],
           'attn_fwd': ['w_ffn_out', 'w_ple_gate', 'w_ple']}
    NEXT = {'attn_fwd': ['w_in'], 'gate_fwd': ['w_ffn_out'],
            'lru_fwd': ['w_branch_att', 'w_branch_rnn', 'w_out', 'w_ple_gate'],
            'ffn_in': ['w_ffn_in'], 'ffn_out': ['w_ple', 'conv_w', 'b_merge']}
    BWD = {'ffn_out_dx': ['w_ffn_out'], 'ffn_in_dw': ['w_ple_gate', 'w_ple'],
           'gate_bwd': ['w_out', 'w_branch_att', 'w_branch_rnn'],
           'attn_bwd': ['w_ffn_in', 'conv_w', 'b_merge']}

    def __init__(self, shards, depth):
        self.shards, self.depth = shards, depth
        self.gathered = [{} for _ in range(depth)]
        self.received = [{} for _ in range(depth)]
        self.pending = []
        self.deferred = None

    def gather_first(self):
        ride = _Ride([self.shards[n] for n in self.FIRST], gather=True, index=[0] * len(self.FIRST))
        self.gathered[0].update(zip(self.FIRST, _exchange("gather_first", ride)))

    def gather_ride(self, layer, kernel_name):
        own = self.OWN.get(kernel_name, []) if layer == 0 else []
        nxt = (self.NEXT if layer == 0 else self.FWD).get(kernel_name, []) if layer + 1 < self.depth else []
        items = [(n, 0) for n in own] + [(n, layer + 1) for n in nxt]
        if not items:
            return None
        ride = _Ride([self.shards[n] for n, _ in items], gather=True, index=[l for _, l in items])
        self.pending.append((ride, [(n, self.gathered[l]) for n, l in items]))
        return ride

    def _scatter(self, arrays, names, layer):
        ride = _Ride(arrays, gather=False)
        self.pending.append((ride, [(n, self.received[layer]) for n in names]))
        return ride

    def scatter_ride(self, layer, kernel_name, grads):
        if kernel_name == 'z_proj_dx':
            whole = _by_destination('w_in', grads)
            half = whole.shape[1] // 2
            self.deferred = (whole[:, half:], layer)
            return self._scatter([whole[:, :half]], ['w_in_a'], layer)
        if kernel_name == 'ffn_in_dx':
            if self.deferred is None:
                return None
            (late, from_layer), self.deferred = self.deferred, None
            return self._scatter([late], ['w_in_b'], from_layer)
        names = self.BWD[kernel_name]
        return self._scatter([_by_destination(n, grads) for n in names], names, layer)

    def collect(self):
        for ride, places in self.pending:
            if ride.result is not None:
                for (name, dst), res in zip(places, ride.result):
                    dst[name] = res
        self.pending = [(ride, places) for ride, places in self.pending if ride.result is None]


class _LayerWeights:
    SOURCE = {'w_in7': 'w_in', 'w_inf': 'w_in', 'b_merge0': 'b_merge', 'b_merge1': 'b_merge'}

    def __init__(self, sched, layer, replicated):
        self.sched, self.layer, self.made = sched, layer, dict(replicated)

    def __getitem__(self, key):
        if key not in self.made:
            self.sched.collect()
            src = self.SOURCE.get(key, key)
            self.made.update(_from_shards(src, self.sched.gathered[self.layer][src]))
        return self.made[key]


def _local_step(x2, tgt, p3, weights_of, depth, g_in, b_in, bsz, seq, sched=None):
    h, hb = _rowwise("ln_in_fwd", _both(_ln), [x2], [g_in, b_in], TWICE, [])
    p3 = p3.astype(BF16)
    saved, layer_w = [], []
    for l in range(depth):
        layer_w.append(weights_of(l))
        ride_of = functools.partial(sched.gather_ride, l) if sched else _no_ride
        h, hb, s = _layer_fwd(h, hb, p3[l], layer_w[l], bsz, seq, ride_of)
        if sched:
            sched.collect()
        saved.append(s)

    def loss_fn(y, tv):
        err = y - tv
        return err * (1.0 / D_MODEL), _colsum(jnp.square(err))

    dh, sq = _rowwise("loss", loss_fn, [h, tgt], [], [(D_MODEL, F32)], [D_MODEL])
    grads = [None] * depth
    for l in reversed(range(depth)):
        ride_of = functools.partial(sched.scatter_ride, l) if sched else _no_ride
        dh, grads[l] = _layer_bwd(dh, p3[l], layer_w[l], saved[l], bsz, seq, ride_of)
        if sched:
            sched.collect()
    dx, dg_in, db_in = _vjp_rowwise("ln_in_bwd", _ln, [x2], [g_in, b_in], [dh], 1)
    return sq, dx, grads, dg_in, db_in


def _from_shards(name, g):
    if name == 'w_in':
        wt = g.transpose(1, 0, 2).reshape(D_MODEL, N_IN)
        return {'w_in7': jnp.concatenate([wt[:, :3 * D_MODEL], wt[:, 3 * D_MODEL + N_HEADS:]], axis=1),
                'w_inf': jnp.pad(wt[:, 3 * D_MODEL:3 * D_MODEL + N_HEADS], ((0, 0), (0, BLK - N_HEADS)))}
    if name in ('w_branch_att', 'w_branch_rnn', 'w_out', 'w_ple_gate'):
        return {name: g.reshape(D_MODEL, D_MODEL)}
    if name == 'w_ffn_in':
        return {name: g}
    if name == 'w_ffn_out':
        return {name: g.reshape(N_FF, FF_SH, D_MODEL)}
    if name == 'b_merge':
        bm = g.transpose(1, 0, 2).reshape(2, D_MODEL)
        return {'b_merge0': bm[0:1], 'b_merge1': bm[1:2]}
    return {name: g.transpose(1, 0, 2).reshape(g.shape[1], D_MODEL)}


def _layer_weights(full):
    w = {}
    for name, g in full.items():
        w.update(_from_shards(name, g))
    return w


def _by_destination(name, gw):
    if name == 'w_in':
        g7, gf = gw['w_in7'], gw['w_inf']
        true = jnp.concatenate([g7[:, :3 * D_MODEL], gf[:, :N_HEADS], g7[:, 3 * D_MODEL:]], axis=1)
        return true.reshape(D_MODEL, N_DEV, IN_SH).transpose(1, 0, 2)
    g = gw[name]
    if name in ('w_branch_att', 'w_branch_rnn', 'w_out', 'w_ple_gate'):
        return g.reshape(N_DEV, D_MODEL // N_DEV, D_MODEL)
    if name == 'w_ffn_in':
        return g
    if name == 'w_ffn_out':
        return g.reshape(N_DEV, N_FF * FF_SH // N_DEV, D_MODEL)
    return g.reshape(g.shape[0], N_DEV, BLK).transpose(1, 0, 2)


def kernel(x, p, ln_in_g, ln_in_b, w_in, b_forget, conv_w, conv_b, rg_w_a, rg_b_a, rg_w_x, rg_b_x, rg_lambda, w_branch_att, w_branch_rnn, b_merge, w_out, ln_mix_g, ln_mix_b, w_ffn_in, w_ffn_out, ln_ffn_g, ln_ffn_b, w_ple, w_ple_gate, b_ple_gate, ln_ple_g, ln_ple_b, loss_target, m_ln_in_g, m_ln_in_b, m_w_in, m_b_forget, m_conv_w, m_conv_b, m_rg_w_a, m_rg_b_a, m_rg_w_x, m_rg_b_x, m_rg_lambda, m_w_branch_att, m_w_branch_rnn, m_b_merge, m_w_out, m_ln_mix_g, m_ln_mix_b, m_w_ffn_in, m_w_ffn_out, m_ln_ffn_g, m_ln_ffn_b, m_w_ple, m_w_ple_gate, m_b_ple_gate, m_ln_ple_g, m_ln_ple_b, v_ln_in_g, v_ln_in_b, v_w_in, v_b_forget, v_conv_w, v_conv_b, v_rg_w_a, v_rg_b_a, v_rg_w_x, v_rg_b_x, v_rg_lambda, v_w_branch_att, v_w_branch_rnn, v_b_merge, v_w_out, v_ln_mix_g, v_ln_mix_b, v_w_ffn_in, v_w_ffn_out, v_ln_ffn_g, v_ln_ffn_b, v_w_ple, v_w_ple_gate, v_b_ple_gate, v_ln_ple_g, v_ln_ple_b):
    env = dict(locals())
    wts = {n: env[n] for n in WEIGHTS}
    mom = {n: env['m_' + n] for n in WEIGHTS}
    var = {n: env['v_' + n] for n in WEIGHTS}
    bsz, seq, _ = x.shape
    depth = w_in.shape[0]
    t = bsz * seq
    x2, tgt = x.reshape(t, D_MODEL), loss_target.reshape(t, D_MODEL)
    p3 = p.reshape(depth, t, D_PLE)

    shard_names = SHARDED_BF16 + SHARDED_F32
    shards = {n: wts[n].astype(BF16) for n in SHARDED_BF16}
    shards.update({n: wts[n] for n in SHARDED_F32})
    sched = _Schedule(shards, depth)
    sched.gather_first()

    def weights_of(l):
        w = {n: _row(wts[n][l]) for n in ['conv_b', 'rg_b_a', 'rg_b_x', 'rg_lambda', 'ln_mix_g', 'ln_mix_b',
                                          'ln_ffn_g', 'ln_ffn_b', 'b_ple_gate', 'ln_ple_g', 'ln_ple_b']}
        w['b_forget'] = jnp.pad(_row(b_forget[l]), ((0, 0), (0, BLK - N_HEADS)))
        w['rg_w_a'], w['rg_w_x'] = rg_w_a[l], rg_w_x[l]
        return _LayerWeights(sched, l, w)

    g_in, b_in = _row(ln_in_g), _row(ln_in_b)
    sq, dx, grads, dg_in, db_in = _local_step(x2, tgt, p3, weights_of, depth, g_in, b_in, bsz, seq, sched)
    loss = lax.psum(0.5 * jnp.sum(sq) / D_MODEL, ("x", "y", "c"))
    grad_x = dx.reshape(bsz, seq, D_MODEL)

    out = {}

    def update(n, ride=None):
        shp = wts[n].shape
        view = lambda a: a
        if n == 'w_in':
            recv = [sched.received[l][half] for l in range(depth) for half in ('w_in_a', 'w_in_b')]
        else:
            recv = [sched.received[l][n] for l in range(depth)]
        if n in SHARDED_F32:
            recv = [jnp.stack(recv, axis=1).reshape(N_DEV, -1, shp[-1])]
            view = lambda a: a.reshape(1, -1, shp[-1])
        res = _adamw("adamw_" + n, recv, view(wts[n]), view(mom[n]), view(var[n]), ride=ride)
        out[n] = [r.reshape(shp) for r in res]

    def rep_grad(n):
        if n == 'ln_in_g':
            return dg_in.reshape(-1)
        if n == 'ln_in_b':
            return db_in.reshape(-1)
        return jnp.stack([grads[l][n].reshape(wts[n].shape[1:]) if n != 'b_forget'
                          else grads[l][n][0, :N_HEADS] for l in range(depth)]).reshape(-1)

    sizes = [int(wts[n].size) for n in REPLICATED]
    n_rows = [8 * (-(-sz // (8 * BLK))) for sz in sizes]
    total_rows = -(-sum(n_rows) // (N_DEV * 8)) * (N_DEV * 8)

    def as_rows(v, sz, nr):
        v = v.reshape(-1)
        return (jnp.pad(v, (0, nr * BLK - sz)) if nr * BLK != sz else v).reshape(nr, BLK)

    def pack(vals):
        parts = [as_rows(v, sz, nr) for v, sz, nr in zip(vals, sizes, n_rows)]
        parts.append(jnp.zeros((total_rows - sum(n_rows), BLK), F32))
        return jnp.concatenate(parts, axis=0)

    late, from_layer = sched.deferred
    last_w_in = _Ride([late], gather=False)
    update('w_ffn_in', ride=last_w_in)
    sched.received[from_layer]['w_in_b'], = last_w_in.result
    scatter_small = _Ride([pack([rep_grad(n) for n in REPLICATED]).reshape(N_DEV, total_rows // N_DEV, BLK)],
                          gather=False)
    update('w_branch_att', ride=scatter_small)
    gather_small = _Ride([_sum_parts("sum_small", scatter_small.result[0])], gather=True)
    update('w_out', ride=gather_small)
    for n in shard_names:
        if n not in out:
            update(n)
    g_rows = gather_small.result[0].reshape(total_rows, BLK)
    starts = [sum(n_rows[:i]) for i in range(len(n_rows))]
    for n, r0, sz, nr in zip(REPLICATED, starts, sizes, n_rows):
        shp = wts[n].shape
        as_one = (1, 1, sz) if len(shp) == 1 else (1, -1, shp[-1])
        g_n = g_rows[r0:r0 + nr]
        g_n = (g_n if nr * BLK == sz else g_n.reshape(-1)[:sz]).reshape(as_one)
        res = _adamw("adamw_" + n, [g_n], *[d[n].reshape(as_one) for d in (wts, mom, var)])
        out[n] = [r.reshape(shp) for r in res]

    return (loss, grad_x, *[out[n][k] for k in range(4) for n in WEIGHTS])
```

```python
import functools
import math

import jax
import jax.numpy as jnp
from jax import lax
from jax.experimental import pallas as pl
from jax.experimental.pallas import tpu as pltpu

F32 = jnp.float32
BF16 = jnp.bfloat16

N_DEV = 8
D_MODEL = 1024
N_HEADS = 8
HEAD_DIM = 128
N_BLK = 8
BLK = 128
CONV_W = 4
D_PLE = 256
FF_SH = 704
N_FF = 4
IN_SH = 897
N_IN = 7176
DEPTH = 4
RG_C = 8.0
ALPHA = float((2 * DEPTH) ** 0.25)
LN_EPS = 1e-5
SCALE = 1.0 / math.sqrt(HEAD_DIM)
NEG = -1e30
ADAM_LR, ADAM_B1, ADAM_B2, ADAM_EPS, ADAM_WD, ADAM_STEP = 0.001, 0.9, 0.999, 1e-08, 0.01, 10
QKV = 3 * D_MODEL
OFF_RX, OFF_RY, OFF_GA, OFF_GB = (i * D_MODEL for i in range(4))
V7X_VMEM_LIMIT = 48 * 1024 * 1024

WEIGHTS = ['ln_in_g', 'ln_in_b', 'w_in', 'b_forget', 'conv_w', 'conv_b', 'rg_w_a', 'rg_b_a', 'rg_w_x', 'rg_b_x',
           'rg_lambda', 'w_branch_att', 'w_branch_rnn', 'b_merge', 'w_out', 'ln_mix_g', 'ln_mix_b', 'w_ffn_in',
           'w_ffn_out', 'ln_ffn_g', 'ln_ffn_b', 'w_ple', 'w_ple_gate', 'b_ple_gate', 'ln_ple_g', 'ln_ple_b']
SHARDED_BF16 = ['w_in', 'w_branch_att', 'w_branch_rnn', 'w_out', 'w_ffn_in', 'w_ffn_out', 'w_ple', 'w_ple_gate']
SHARDED_F32 = ['conv_w', 'b_merge']
REPLICATED = [n for n in WEIGHTS if n not in SHARDED_BF16 and n not in SHARDED_F32]

NN = ((1,), (0,))
NT = ((1,), (1,))
TN = ((0,), (0,))


class _Ride:
    def __init__(self, arrays, *, gather, index=None):
        self.arrays, self.gather, self.index = list(arrays), gather, index
        self.result = None

    def _shard(self, ins, a):
        return ins[a] if self.index is None else ins[a].at[self.index[a]]

    def out_shapes(self):
        if not self.gather:
            return [jax.ShapeDtypeStruct(a.shape, a.dtype) for a in self.arrays]
        cut = 0 if self.index is None else 1
        return [jax.ShapeDtypeStruct((N_DEV,) + a.shape[cut:], a.dtype) for a in self.arrays]

    def scratch(self):
        n = len(self.arrays)
        return [pltpu.SemaphoreType.DMA((n * N_DEV,)), pltpu.SemaphoreType.DMA((n * N_DEV,)),
                pltpu.SemaphoreType.DMA((n,))]

    def _copy(self, a, k, src, dst, sems, to=None):
        send_sems, recv_sems, _ = sems
        return pltpu.make_async_remote_copy(
            src_ref=src, dst_ref=dst, send_sem=send_sems.at[a * N_DEV + k], recv_sem=recv_sems.at[a * N_DEV + k],
            device_id=_peer(k if to is None else to), device_id_type=pl.DeviceIdType.MESH)

    def begin(self, ins, outs, sems):
        me = _my_id()
        started = []
        for a in range(len(ins)):
            if self.gather:
                src = self._shard(ins, a)
                started.append(pltpu.make_async_copy(src, outs[a].at[me], sems[2].at[a]))
                started += [self._copy(a, k, src, outs[a].at[me], sems) for k in (1, 2, 4, 6)]
            else:
                started.append(pltpu.make_async_copy(ins[a].at[me], outs[a].at[me], sems[2].at[a]))
                started += [self._copy(a, k, ins[a].at[me ^ k], outs[a].at[me], sems) for k in range(1, N_DEV)]
        for cp in started:
            cp.start()

    def finish(self, ins, outs, sems):
        me = _my_id()
        for a in range(len(ins)):
            if self.gather:
                src = self._shard(ins, a)
                passed = []
                for k in (2, 4, 6):
                    block = outs[a].at[me ^ k]
                    self._copy(a, k, src, block, sems).wait_recv()
                    passed.append(self._copy(a, k + 1, block, block, sems, to=1))
                    passed[-1].start()
                for k in (1, 2, 4, 6):
                    self._copy(a, k, src, outs[a].at[me], sems).wait_send()
                self._copy(a, 1, src, outs[a].at[me ^ 1], sems).wait_recv()
                for cp in passed:
                    cp.wait()
                pltpu.make_async_copy(src, outs[a].at[me], sems[2].at[a]).wait()
            else:
                pltpu.make_async_copy(ins[a].at[me], outs[a].at[me], sems[2].at[a]).wait()
                for k in range(1, N_DEV):
                    self._copy(a, k, ins[a].at[me ^ k], outs[a].at[me], sems).wait()


def _pcall(body, ride=None, **kw):
    if ride is None:
        return pl.pallas_call(body, **kw)
    n = len(ride.arrays)
    grid = kw['grid']
    single = not isinstance(kw['out_shape'], (list, tuple))
    out_specs = [kw['out_specs']] if single else list(kw['out_specs'])
    out_shape = [kw['out_shape']] if single else list(kw['out_shape'])
    in_specs = list(kw['in_specs'])
    scratch = list(kw.get('scratch_shapes', ()))
    n_in, n_out, n_sc = len(in_specs), len(out_shape), len(scratch)
    hbm = pl.BlockSpec(memory_space=pltpu.HBM)

    def wrapped(*refs):
        ins, xin = refs[:n_in], refs[n_in:n_in + n]
        outs, xout = refs[n_in + n:n_in + n + n_out], refs[n_in + n + n_out:n_in + 2 * n + n_out]
        sc, sems = refs[n_in + 2 * n + n_out:n_in + 2 * n + n_out + n_sc], refs[-3:]
        ids = [pl.program_id(ax) for ax in range(len(grid))]
        first = functools.reduce(jnp.logical_and, [i == 0 for i in ids])
        last = functools.reduce(jnp.logical_and, [i == g - 1 for i, g in zip(ids, grid)])

        pl.when(first)(lambda: ride.begin(xin, xout, sems))
        body(*ins, *outs, *sc)
        pl.when(last)(lambda: ride.finish(xin, xout, sems))

    call = pl.pallas_call(wrapped, name=kw['name'], grid=grid, in_specs=in_specs + [hbm] * n,
                          out_specs=out_specs + [hbm] * n, out_shape=out_shape + ride.out_shapes(),
                          scratch_shapes=scratch + ride.scratch(), compiler_params=kw['compiler_params'])

    def run(*args):
        res = call(*args, *ride.arrays)
        ride.result = list(res[n_out:])
        return res[0] if single else list(res[:n_out])

    return run


def _tile(n, pref, mult=8):
    if n <= pref:
        return n
    t = (pref // mult) * mult
    while t >= mult:
        if n % t == 0:
            return t
        t -= mult
    return n


def _cparams(sem):
    return pltpu.CompilerParams(dimension_semantics=sem, vmem_limit_bytes=V7X_VMEM_LIMIT)


def _mm(name, a, b, *, grid, a_spec, b_spec, o_spec, out_shape, contract, ride=None, add=None):
    nk = grid[-1]
    in_out = out_shape.dtype == F32
    acc_shape = tuple(d for d in o_spec.block_shape if d is not None)

    def body(*refs):
        a_ref, b_ref = refs[0], refs[1]
        add_ref = refs[2] if add is not None else None
        o_ref = refs[3] if add is not None else refs[2]
        acc_ref = o_ref if (in_out or nk == 1) else refs[-1]
        k = pl.program_id(len(grid) - 1)
        part = lax.dot_general(a_ref[...].astype(BF16), b_ref[...].astype(BF16), (contract, ((), ())),
                               preferred_element_type=F32)
        if add_ref is not None:
            part = jnp.where(k == 0, part + add_ref[...], part) if nk > 1 else part + add_ref[...]
        if nk == 1:
            o_ref[...] = part.astype(o_ref.dtype)
            return

        @pl.when(k == 0)
        def _():
            acc_ref[...] = part

        @pl.when(k > 0)
        def _():
            acc_ref[...] += part

        if not in_out:
            @pl.when(k == nk - 1)
            def _():
                o_ref[...] = acc_ref[...].astype(o_ref.dtype)

    sem = ("parallel",) * (len(grid) - 1) + ("arbitrary",)
    scratch = [] if (in_out or nk == 1) else [pltpu.VMEM(acc_shape, F32)]
    in_specs, args = [a_spec, b_spec], [a, b]
    if add is not None:
        in_specs.append(o_spec)
        args.append(add)
    return _pcall(body, ride=ride, name=name, grid=grid, in_specs=in_specs, out_specs=o_spec,
                  out_shape=out_shape, scratch_shapes=scratch, compiler_params=_cparams(sem))(*args)


def _mm_nn(name, a, b, *, b_off=0, n=None, out_dtype=F32, tm=1024, tn=1024, tk=1024, ride=None):
    m, k = a.shape
    n = b.shape[1] if n is None else n
    tm, tn, tk = _tile(m, tm), _tile(n, tn, 128), _tile(k, tk, 128)
    no = b_off // tn
    return _mm(name, a, b, grid=(m // tm, n // tn, k // tk),
               a_spec=pl.BlockSpec((tm, tk), lambda i, j, kk: (i, kk)),
               b_spec=pl.BlockSpec((tk, tn), lambda i, j, kk: (kk, j + no)),
               o_spec=pl.BlockSpec((tm, tn), lambda i, j, kk: (i, j)),
               out_shape=jax.ShapeDtypeStruct((m, n), out_dtype), contract=NN, ride=ride)


def _mm_nt(name, a, b, *, out_dtype=F32, tm=1024, tn=1024, tk=1024, ride=None, add=None):
    m, k = a.shape
    n = b.shape[0]
    tm, tn, tk = _tile(m, tm), _tile(n, tn, 128), _tile(k, tk, 128)
    return _mm(name, a, b, grid=(m // tm, n // tn, k // tk),
               a_spec=pl.BlockSpec((tm, tk), lambda i, j, kk: (i, kk)),
               b_spec=pl.BlockSpec((tn, tk), lambda i, j, kk: (j, kk)),
               o_spec=pl.BlockSpec((tm, tn), lambda i, j, kk: (i, j)),
               out_shape=jax.ShapeDtypeStruct((m, n), out_dtype), contract=NT, ride=ride, add=add)


def _mm_tn(name, a, b, *, a_off=0, m=None, out_dtype=F32, tm=1024, tn=1024, tk=2048, ride=None):
    t, n = b.shape
    m = a.shape[1] if m is None else m
    tm, tn, tk = _tile(m, tm, 128), _tile(n, tn, 128), _tile(t, tk)
    mo = a_off // tm
    return _mm(name, a, b, grid=(m // tm, n // tn, t // tk),
               a_spec=pl.BlockSpec((tk, tm), lambda i, j, kk: (kk, i + mo)),
               b_spec=pl.BlockSpec((tk, tn), lambda i, j, kk: (kk, j)),
               o_spec=pl.BlockSpec((tm, tn), lambda i, j, kk: (i, j)),
               out_shape=jax.ShapeDtypeStruct((m, n), out_dtype), contract=TN, ride=ride)


def _rowwise(name, fn, rows, params, out_rows, out_reds, tm=512):
    rows = [r if isinstance(r, tuple) else (r, 0, r.shape[1]) for r in rows]
    t = rows[0][0].shape[0]
    tm = _tile(t, tm)
    in_specs = []
    for _, off, w in rows:
        in_specs.append(pl.BlockSpec((tm, w), functools.partial(lambda i, cb: (i, cb), cb=off // w)))
    for p in params:
        in_specs.append(pl.BlockSpec((1, p.shape[1]), lambda i: (0, 0)))
    out_specs = [pl.BlockSpec((tm, w), lambda i: (i, 0)) for w, _ in out_rows]
    out_specs += [pl.BlockSpec((1, w), lambda i: (0, 0)) for w in out_reds]
    out_shape = [jax.ShapeDtypeStruct((t, w), dt) for w, dt in out_rows]
    out_shape += [jax.ShapeDtypeStruct((1, w), F32) for w in out_reds]
    nr, npar, nor = len(rows), len(params), len(out_rows)

    def body(*refs):
        ins, outs = refs[:nr + npar], refs[nr + npar:]
        vals = [r[...].astype(F32) for r in ins[:nr]]
        vals += [jnp.broadcast_to(r[...], (tm, r.shape[1])) for r in ins[nr:]]
        res = fn(*vals)
        step = pl.program_id(0)
        for o, v in zip(outs[:nor], res[:nor]):
            o[...] = v.astype(o.dtype)
        for o, v in zip(outs[nor:], res[nor:]):
            _accumulate(o, v, step)

    res = _pcall(body, name=name, grid=(t // tm,), in_specs=in_specs, out_specs=out_specs, out_shape=out_shape,
                 compiler_params=_cparams(("arbitrary",)))(*[r[0] for r in rows], *params)
    return res


def _accumulate(o_ref, v, step):
    @pl.when(step == 0)
    def _():
        o_ref[...] = v

    @pl.when(step > 0)
    def _():
        o_ref[...] += v


def _colsum(v):
    return jnp.sum(v, axis=0, keepdims=True)


def _vjp_rowwise(name, fn, rows, params, cots, n_row_grads, tm=256, dtypes=None):
    nr, npar, nc = len(rows), len(params), len(cots)

    def bwd(*vals):
        prim, par, ct = vals[:nr], vals[nr + nc:], vals[nr:nr + nc]
        _, pull = jax.vjp(fn, *prim, *par)
        grads = pull(tuple(ct) if nc > 1 else ct[0])
        return tuple(grads[:n_row_grads]) + tuple(_colsum(g) for g in grads[nr:])

    dtypes = [F32] * n_row_grads if dtypes is None else dtypes
    widths = [(r[2] if isinstance(r, tuple) else r.shape[1], dt) for r, dt in zip(rows[:n_row_grads], dtypes)]
    return _rowwise(name, bwd, list(rows) + list(cots), params, widths, [p.shape[1] for p in params], tm=tm)


def _ln(s, g, b):
    mu = jnp.mean(s, axis=-1, keepdims=True)
    var = jnp.mean(jnp.square(s - mu), axis=-1, keepdims=True)
    return (s - mu) * lax.rsqrt(var + LN_EPS) * g + b


def _softplus(x):
    return jnp.maximum(x, 0.0) + jnp.log1p(jnp.exp(-jnp.abs(x)))


def _expm1(x):
    series = x * (1.0 + x * (1.0 / 2 + x * (1.0 / 6 + x * (1.0 / 24 + x * (1.0 / 120 + x * (1.0 / 720))))))
    return jnp.where(jnp.abs(x) < 0.25, series, jnp.exp(x) - 1.0)


def _f_resid_ln(h, branch, g, b):
    return _ln(ALPHA * h + branch, g, b)


def _f_ple(h, gp, pe, bpg, g, b):
    return _ln(ALPHA * h + jax.nn.sigmoid(gp + bpg) * pe, g, b)


def _f_merge(ga, gb, ya, yb, bm0, bm1):
    return jax.nn.sigmoid(ga + bm0) * ya + jax.nn.sigmoid(gb + bm1) * yb


def _f_rnn_out(hs, ry):
    return hs * jax.nn.gelu(ry, approximate=True)


def _f_logf(fl, bf):
    return -_softplus(-(fl + bf))


def _f_decay(lam):
    return -RG_C * _softplus(-lam)


def _f_gate(xc, ra, ia, decay, ba, bx):
    r = jax.nn.sigmoid(ra + ba)
    i = jax.nn.sigmoid(ia + bx)
    log_a = decay * r
    a = jnp.exp(log_a)
    mult = jnp.sqrt(-_expm1(2.0 * log_a))
    return a, mult * (i * xc)


def _f_act(hg, hu):
    return jax.nn.silu(hg) * hu


ATT_BLOCK = 256
ATT_HEADS_PER_STEP = 1
SCAN_ROWS = 128


def _scores(q, k, cq, ck, diagonal):
    s = lax.dot_general(q, k, (NT, ((), ())), preferred_element_type=F32) * SCALE
    s = s + cq - ck
    if diagonal:
        row = lax.broadcasted_iota(jnp.int32, s.shape, 0)
        col = lax.broadcasted_iota(jnp.int32, s.shape, 1)
        s = jnp.where(col <= row, s, NEG)
    return s


def _dscores(p, do, o, v):
    dob = do.astype(BF16)
    delta = jnp.sum(dob.astype(F32) * o, axis=1, keepdims=True)
    dp = lax.dot_general(dob, v.astype(BF16), (NT, ((), ())), preferred_element_type=F32)
    return p * (dp - delta)


def _attn_fwd(z, cq, ck, bsz, seq, ride=None):
    t = bsz * seq
    tq = _tile(seq, ATT_BLOCK)
    nq = seq // tq

    hp = ATT_HEADS_PER_STEP

    def body(q_ref, k_ref, v_ref, cq_ref, ck_ref, o_ref, ob_ref, lse_ref):
        for hh, i in [(hh, i) for hh in range(hp) for i in range(nq)]:
            lanes = slice(hh * HEAD_DIM, (hh + 1) * HEAD_DIM)
            rows = slice(i * tq, (i + 1) * tq)
            q = q_ref[rows, lanes].astype(BF16)
            cqi = cq_ref[hh, rows, :]

            def step(j, carry, diagonal, q=q, cqi=cqi, hh=hh, lanes=lanes):
                m, l, acc = carry
                keys = slice(j * tq, (j + 1) * tq)
                s = _scores(q, k_ref[keys, lanes].astype(BF16), cqi, ck_ref[hh, j:j + 1, :], diagonal)
                m_new = jnp.maximum(m, jnp.max(s, axis=1, keepdims=True))
                alpha = jnp.exp(m - m_new)
                p = jnp.exp(s - m_new)
                p_hi = p.astype(BF16)
                p_lo = (p - p_hi.astype(F32)).astype(BF16)
                vb = v_ref[keys, lanes].astype(BF16)
                pv = lax.dot_general(p_hi, vb, (NN, ((), ())), preferred_element_type=F32)
                pv = pv + lax.dot_general(p_lo, vb, (NN, ((), ())), preferred_element_type=F32)
                return m_new, alpha * l + jnp.sum(p, axis=1, keepdims=True), alpha * acc + pv

            carry = (jnp.full((tq, 1), NEG, F32), jnp.zeros((tq, 1), F32), jnp.zeros((tq, HEAD_DIM), F32))
            for j in range(i):
                carry = step(j, carry, False)
            m, l, acc = step(i, carry, True)
            o = acc / l
            o_ref[rows, lanes] = o
            ob_ref[rows, lanes] = o.astype(BF16)
            lse_ref[hh, rows, :] = m + jnp.log(l)

    groups = N_HEADS // hp
    head = (seq, hp * HEAD_DIM)
    in_specs = [
        pl.BlockSpec(head, lambda b, g: (b, g)),
        pl.BlockSpec(head, lambda b, g: (b, groups + g)),
        pl.BlockSpec(head, lambda b, g: (b, 2 * groups + g)),
        pl.BlockSpec((None, hp, seq, 1), lambda b, g: (b, g, 0, 0)),
        pl.BlockSpec((None, hp, nq, tq), lambda b, g: (b, g, 0, 0)),
    ]
    out_specs = [pl.BlockSpec(head, lambda b, g: (b, g)), pl.BlockSpec(head, lambda b, g: (b, g)),
                 pl.BlockSpec((None, hp, seq, 1), lambda b, g: (b, g, 0, 0))]
    out_shape = [jax.ShapeDtypeStruct((t, D_MODEL), F32), jax.ShapeDtypeStruct((t, D_MODEL), BF16),
                 jax.ShapeDtypeStruct((bsz, N_HEADS, seq, 1), F32)]
    return _pcall(body, ride=ride, name="attn_fwd", grid=(bsz, groups), in_specs=in_specs, out_specs=out_specs,
                  out_shape=out_shape, compiler_params=_cparams(("parallel", "parallel")))(
                      z, z, z, cq, ck.reshape(bsz, N_HEADS, nq, tq))


def _attn_bwd(z, att, datt, lse, cq, ck, bsz, seq, ride=None):
    t = bsz * seq
    tq = _tile(seq, ATT_BLOCK)
    nq = seq // tq

    def body(q_ref, k_ref, v_ref, o_ref, do_ref, lse_ref, cq_ref, ck_ref,
             dq_ref, dk_ref, dv_ref, dcq_ref, dck_ref, dq_sc):
        dq_sc[...] = jnp.zeros_like(dq_sc)
        dcq_ref[...] = jnp.zeros_like(dcq_ref)
        for j in range(nq):
            keys = slice(j * tq, (j + 1) * tq)
            kb = k_ref[keys, :].astype(BF16)
            vb = v_ref[keys, :].astype(BF16)
            ckj = ck_ref[j:j + 1, :]

            def step(i, carry, diagonal, kb=kb, vb=vb, ckj=ckj):
                dk, dv, dc = carry
                rows = slice(i * tq, (i + 1) * tq)
                qb = q_ref[rows, :].astype(BF16)
                do = do_ref[rows, :]
                s = _scores(qb, kb, cq_ref[rows, :], ckj, diagonal)
                p = jnp.exp(s - lse_ref[rows, :])
                ds = _dscores(p, do, o_ref[rows, :], vb)
                dsb = (ds * SCALE).astype(BF16)
                dq_sc[rows, :] += lax.dot_general(dsb, kb, (NN, ((), ())), preferred_element_type=F32)
                dcq_ref[rows, :] += jnp.sum(ds, axis=1, keepdims=True)
                dv = dv + lax.dot_general(p.astype(BF16), do.astype(BF16), (TN, ((), ())),
                                          preferred_element_type=F32)
                dk = dk + lax.dot_general(dsb, qb, (TN, ((), ())), preferred_element_type=F32)
                return dk, dv, dc - jnp.sum(ds, axis=0, keepdims=True)

            zero = jnp.zeros((tq, HEAD_DIM), F32)
            carry = step(j, (zero, zero, jnp.zeros((1, tq), F32)), True)
            for i in range(j + 1, nq):
                carry = step(i, carry, False)
            dk, dv, dck_ref[j:j + 1, :] = carry
            dk_ref[keys, :] = dk.astype(BF16)
            dv_ref[keys, :] = dv.astype(BF16)
        dq_ref[...] = dq_sc[...].astype(BF16)

    head = (seq, HEAD_DIM)
    hmap = lambda b, h: (b, h)
    col = pl.BlockSpec((None, None, seq, 1), lambda b, h: (b, h, 0, 0))
    row = pl.BlockSpec((None, None, nq, tq), lambda b, h: (b, h, 0, 0))
    in_specs = [pl.BlockSpec(head, hmap),
                pl.BlockSpec(head, lambda b, h: (b, N_HEADS + h)),
                pl.BlockSpec(head, lambda b, h: (b, 2 * N_HEADS + h)),
                pl.BlockSpec(head, hmap), pl.BlockSpec(head, hmap), col, col, row]
    big = jax.ShapeDtypeStruct((t, D_MODEL), BF16)
    return _pcall(body, ride=ride, name="attn_bwd", grid=(bsz, N_HEADS), in_specs=in_specs,
                  out_specs=[pl.BlockSpec(head, hmap)] * 3 + [col, row],
                  out_shape=[big, big, big, jax.ShapeDtypeStruct((bsz, N_HEADS, seq, 1), F32),
                             jax.ShapeDtypeStruct((bsz, N_HEADS, nq, tq), F32)],
                  scratch_shapes=[pltpu.VMEM(head, F32)],
                  compiler_params=_cparams(("parallel", "parallel")))(
                      z, z, z, att, datt, lse, cq, ck.reshape(bsz, N_HEADS, nq, tq))


def _scan(name, a, u, bsz, seq, *, reverse, with_prev=False, tb=512, ride=None):
    c = u.shape[1]
    tb = _tile(seq, tb)
    nb = seq // tb
    rc = SCAN_ROWS if tb % SCAN_ROWS == 0 else tb
    has_a = a is not None

    def body(*refs):
        if has_a:
            a_ref, u_ref = refs[0], refs[1]
            rest = refs[2:]
        else:
            u_ref = refs[0]
            rest = refs[1:]
        outs = rest[:2] if with_prev else rest[:1]
        carry_sc, afirst_sc = rest[-2], rest[-1]
        step = pl.program_id(1)

        @pl.when(step == 0)
        def _():
            carry_sc[...] = jnp.zeros_like(carry_sc)
            afirst_sc[...] = jnp.zeros_like(afirst_sc)

        row = lax.broadcasted_iota(jnp.int32, (rc, BLK), 0)
        pieces = list(range(tb // rc))
        for ls in range(c // BLK):
            lanes = slice(ls * BLK, (ls + 1) * BLK)
            carry = carry_sc[:, lanes]
            afirst = afirst_sc[:, lanes]
            for pc in (reversed(pieces) if reverse else pieces):
                rows = slice(pc * rc, (pc + 1) * rc)
                uu = u_ref[rows, lanes]
                if has_a:
                    aa = a_ref[rows, lanes]
                    coef = jnp.where(row < rc - 1, pltpu.roll(aa, rc - 1, 0), afirst) if reverse else aa
                k = 1
                while k < rc:
                    shift = rc - k if reverse else k
                    keep = (row < rc - k) if reverse else (row >= k)
                    uu_sh = jnp.where(keep, pltpu.roll(uu, shift, 0), 0.0)
                    if has_a:
                        uu = coef * uu_sh + uu
                        coef = coef * jnp.where(keep, pltpu.roll(coef, shift, 0), 1.0)
                    else:
                        uu = uu + uu_sh
                    k *= 2
                h = uu + coef * carry if has_a else uu + carry
                outs[0][rows, lanes] = h
                if with_prev:
                    outs[1][rows, lanes] = jnp.where(row >= 1, pltpu.roll(h, 1, 0), carry)
                edge = pc * rc if reverse else (pc + 1) * rc - 1
                carry = outs[0][edge:edge + 1, lanes]
                if has_a and reverse:
                    afirst = a_ref[edge:edge + 1, lanes]
            carry_sc[:, lanes] = carry
            if has_a and reverse:
                afirst_sc[:, lanes] = afirst

    if reverse:
        imap = lambda b, s: (b * nb + nb - 1 - s, 0)
    else:
        imap = lambda b, s: (b * nb + s, 0)
    spec = pl.BlockSpec((tb, c), imap)
    n_in = 2 if has_a else 1
    n_out = 2 if with_prev else 1
    res = _pcall(body, ride=ride, name=name, grid=(bsz, nb), in_specs=[spec] * n_in, out_specs=[spec] * n_out,
                 out_shape=[jax.ShapeDtypeStruct(u.shape, F32)] * n_out,
                 scratch_shapes=[pltpu.VMEM((1, c), F32), pltpu.VMEM((1, c), F32)],
                 compiler_params=_cparams(("parallel", "arbitrary")))(*([a, u] if has_a else [u]))
    return res if with_prev else res[0]


def _conv_fwd(z, w, b, bsz, seq, tb=512):
    c = D_MODEL
    t = bsz * seq
    tb = _tile(seq, tb)
    nb = seq // tb

    def body(x_ref, w_ref, b_ref, o_ref, tail_sc):
        step = pl.program_id(1)

        @pl.when(step == 0)
        def _():
            tail_sc[...] = jnp.zeros_like(tail_sc)

        x = x_ref[...]
        row8 = lax.broadcasted_iota(jnp.int32, (8, c), 0)
        tail = tail_sc[...]
        acc = w_ref[CONV_W - 1:CONV_W, :] * x + b_ref[...]
        for sh in range(1, CONV_W):
            xs = pltpu.roll(x, sh, 0)
            top = jnp.where(row8 < sh, pltpu.roll(tail, sh, 0), xs[0:8, :])
            xs = jnp.concatenate([top, xs[8:, :]], axis=0) if tb > 8 else top
            acc = acc + w_ref[CONV_W - 1 - sh:CONV_W - sh, :] * xs
        o_ref[...] = acc
        tail_sc[...] = x_ref[tb - 8:tb, :]

    return _pcall(body, name="conv_fwd", grid=(bsz, nb),
                  in_specs=[pl.BlockSpec((tb, c), lambda bb, s: (bb * nb + s, OFF_RX // c)),
                            pl.BlockSpec((CONV_W, c), lambda bb, s: (0, 0)),
                            pl.BlockSpec((1, c), lambda bb, s: (0, 0))],
                  out_specs=pl.BlockSpec((tb, c), lambda bb, s: (bb * nb + s, 0)),
                  out_shape=jax.ShapeDtypeStruct((t, c), F32),
                  scratch_shapes=[pltpu.VMEM((8, c), F32)],
                  compiler_params=_cparams(("parallel", "arbitrary")))(z, w, b)


def _conv_bwd(z, dxc, w, bsz, seq, tb=512):
    c = D_MODEL
    t = bsz * seq
    tb = _tile(seq, tb)
    nb = seq // tb

    def body(x_ref, g_ref, w_ref, dx_ref, dw_ref, db_ref, head_sc):
        bb, step = pl.program_id(0), pl.program_id(1)

        @pl.when(step == 0)
        def _():
            head_sc[...] = jnp.zeros_like(head_sc)

        x, g = x_ref[...], g_ref[...]
        row8 = lax.broadcasted_iota(jnp.int32, (8, c), 0)
        head = head_sc[...]
        dx = w_ref[CONV_W - 1:CONV_W, :] * g
        dws = [None] * CONV_W
        dws[CONV_W - 1] = _colsum(g * x)
        for sh in range(1, CONV_W):
            gs = pltpu.roll(g, tb - sh, 0)
            bot = jnp.where(row8 >= 8 - sh, pltpu.roll(head, 8 - sh, 0), gs[tb - 8:tb, :])
            gs = jnp.concatenate([gs[:tb - 8, :], bot], axis=0) if tb > 8 else bot
            dx = dx + w_ref[CONV_W - 1 - sh:CONV_W - sh, :] * gs
            dws[CONV_W - 1 - sh] = _colsum(gs * x)
        dx_ref[...] = dx.astype(dx_ref.dtype)
        first = (bb == 0) & (step == 0)
        dw = jnp.concatenate(dws, axis=0)
        db = _colsum(g)

        @pl.when(first)
        def _():
            dw_ref[...] = dw
            db_ref[...] = db

        @pl.when(jnp.logical_not(first))
        def _():
            dw_ref[...] += dw
            db_ref[...] += db

        head_sc[...] = g_ref[0:8, :]

    rmap = lambda bb, s: (bb * nb + nb - 1 - s, 0)
    return _pcall(body, name="conv_bwd", grid=(bsz, nb),
                  in_specs=[pl.BlockSpec((tb, c), lambda bb, s: (bb * nb + nb - 1 - s, OFF_RX // c)),
                            pl.BlockSpec((tb, c), rmap),
                            pl.BlockSpec((CONV_W, c), lambda bb, s: (0, 0))],
                  out_specs=[pl.BlockSpec((tb, c), rmap),
                             pl.BlockSpec((CONV_W, c), lambda bb, s: (0, 0)),
                             pl.BlockSpec((1, c), lambda bb, s: (0, 0))],
                  out_shape=[jax.ShapeDtypeStruct((t, c), BF16), jax.ShapeDtypeStruct((CONV_W, c), F32),
                             jax.ShapeDtypeStruct((1, c), F32)],
                  scratch_shapes=[pltpu.VMEM((8, c), F32)],
                  compiler_params=_cparams(("arbitrary", "arbitrary")))(z, dxc, w)


def _gate_fwd(xc, w_a, w_x, b_a, b_x, lam, tm=1024, ride=None):
    t = xc.shape[0]
    tm = _tile(t, tm)

    def body(xc_ref, wa_ref, wx_ref, ba_ref, bx_ref, lam_ref, a_ref, u_ref):
        xc_b = xc_ref[...]
        xb = xc_b.astype(BF16)
        ra = lax.dot_general(xb, wa_ref[...].astype(BF16), (NN, ((), ())), preferred_element_type=F32)
        ia = lax.dot_general(xb, wx_ref[...].astype(BF16), (NN, ((), ())), preferred_element_type=F32)
        a, u = _f_gate(xc_b, ra, ia, lam_ref[...], ba_ref[...], bx_ref[...])
        a_ref[...] = a
        u_ref[...] = u

    row = pl.BlockSpec((tm, BLK), lambda n, i: (i, n))
    wsp = pl.BlockSpec((None, BLK, BLK), lambda n, i: (n, 0, 0))
    vec = pl.BlockSpec((1, BLK), lambda n, i: (0, n))
    return _pcall(body, ride=ride, name="gate_fwd", grid=(N_BLK, t // tm), in_specs=[row, wsp, wsp, vec, vec, vec],
                  out_specs=[row, row], out_shape=[jax.ShapeDtypeStruct((t, D_MODEL), F32)] * 2,
                  compiler_params=_cparams(("parallel", "parallel")))(xc, w_a, w_x, b_a, b_x, lam)


def _gate_bwd(xc, w_a, w_x, b_a, b_x, lam, hprev, du, tm=1024, ride=None):
    t = xc.shape[0]
    tm = _tile(t, tm)

    def body(xc_ref, wa_ref, wx_ref, ba_ref, bx_ref, lam_ref, hp_ref, du_ref,
             dxc_ref, dwa_ref, dwx_ref, dba_ref, dbx_ref, dlam_ref):
        step = pl.program_id(1)
        xc_b = xc_ref[...]
        xb = xc_b.astype(BF16)
        wa, wx = wa_ref[...].astype(BF16), wx_ref[...].astype(BF16)
        ra = lax.dot_general(xb, wa, (NN, ((), ())), preferred_element_type=F32)
        ia = lax.dot_general(xb, wx, (NN, ((), ())), preferred_element_type=F32)
        full = lambda r: jnp.broadcast_to(r[...], (tm, BLK))
        _, pull = jax.vjp(_f_gate, xc_b, ra, ia, full(lam_ref), full(ba_ref), full(bx_ref))
        du_b = du_ref[...]
        dxc, dra, dia, dlam, dba, dbx = pull((du_b * hp_ref[...], du_b))
        drb, dib = dra.astype(BF16), dia.astype(BF16)
        dxc = dxc + lax.dot_general(drb, wa, (NT, ((), ())), preferred_element_type=F32)
        dxc = dxc + lax.dot_general(dib, wx, (NT, ((), ())), preferred_element_type=F32)
        dxc_ref[...] = dxc
        _accumulate(dwa_ref, lax.dot_general(xb, drb, (TN, ((), ())), preferred_element_type=F32), step)
        _accumulate(dwx_ref, lax.dot_general(xb, dib, (TN, ((), ())), preferred_element_type=F32), step)
        _accumulate(dba_ref, _colsum(dba), step)
        _accumulate(dbx_ref, _colsum(dbx), step)
        _accumulate(dlam_ref, _colsum(dlam), step)

    row = pl.BlockSpec((tm, BLK), lambda n, i: (i, n))
    wsp = pl.BlockSpec((None, BLK, BLK), lambda n, i: (n, 0, 0))
    vec = pl.BlockSpec((1, BLK), lambda n, i: (0, n))
    wshape = jax.ShapeDtypeStruct((N_BLK, BLK, BLK), F32)
    vshape = jax.ShapeDtypeStruct((1, D_MODEL), F32)
    return _pcall(body, ride=ride, name="gate_bwd", grid=(N_BLK, t // tm),
                  in_specs=[row, wsp, wsp, vec, vec, vec, row, row],
                  out_specs=[row, wsp, wsp, vec, vec, vec],
                  out_shape=[jax.ShapeDtypeStruct((t, D_MODEL), F32), wshape, wshape, vshape, vshape, vshape],
                  compiler_params=_cparams(("parallel", "arbitrary")))(xc, w_a, w_x, b_a, b_x, lam, hprev, du)


def _ffn_in_act(a, w, tm=1024, ride=None):
    t = a.shape[0]
    tm = _tile(t, tm)

    def body(a_ref, wg_ref, wu_ref, hgu_ref, act_ref):
        ab = a_ref[...].astype(BF16)
        hg = lax.dot_general(ab, wg_ref[...].astype(BF16), (NN, ((), ())), preferred_element_type=F32)
        hu = lax.dot_general(ab, wu_ref[...].astype(BF16), (NN, ((), ())), preferred_element_type=F32)
        hgu_ref[0] = hg
        hgu_ref[1] = hu
        act_ref[...] = _f_act(hg, hu).astype(act_ref.dtype)

    wspec = lambda off: pl.BlockSpec((None, D_MODEL, FF_SH), lambda i, s: (s + off, 0, 0))
    hgu, act = _pcall(body, ride=ride, name="ffn_in", grid=(t // tm, N_FF),
                      in_specs=[pl.BlockSpec((tm, D_MODEL), lambda i, s: (i, 0)), wspec(0), wspec(N_FF)],
                      out_specs=[pl.BlockSpec((2, None, tm, FF_SH), lambda i, s: (0, s, i, 0)),
                                 pl.BlockSpec((None, tm, FF_SH), lambda i, s: (s, i, 0))],
                      out_shape=[jax.ShapeDtypeStruct((2, N_FF, t, FF_SH), F32),
                                 jax.ShapeDtypeStruct((N_FF, t, FF_SH), BF16)],
                      compiler_params=_cparams(("parallel", "parallel")))(a, w, w)
    return hgu.reshape(2 * N_FF, t, FF_SH), act


def _ffn_out_dx_act(d, w, hgu, tm=1024, ride=None):
    t = d.shape[0]
    tm = _tile(t, tm)

    def body(d_ref, w_ref, hg_ref, hu_ref, o_ref):
        dact = lax.dot_general(d_ref[...].astype(BF16), w_ref[...].astype(BF16), (NT, ((), ())),
                               preferred_element_type=F32)
        _, pull = jax.vjp(_f_act, hg_ref[...], hu_ref[...])
        dhg, dhu = pull(dact)
        o_ref[0] = dhg.astype(o_ref.dtype)
        o_ref[1] = dhu.astype(o_ref.dtype)

    hspec = lambda off: pl.BlockSpec((None, tm, FF_SH), lambda i, s: (s + off, i, 0))
    res = _pcall(body, ride=ride, name="ffn_out_dx", grid=(t // tm, N_FF),
                 in_specs=[pl.BlockSpec((tm, D_MODEL), lambda i, s: (i, 0)),
                           pl.BlockSpec((None, FF_SH, D_MODEL), lambda i, s: (s, 0, 0)), hspec(0), hspec(N_FF)],
                 out_specs=pl.BlockSpec((2, None, tm, FF_SH), lambda i, s: (0, s, i, 0)),
                 out_shape=jax.ShapeDtypeStruct((2, N_FF, t, FF_SH), BF16),
                 compiler_params=_cparams(("parallel", "parallel")))(d, w, hgu, hgu)
    return res.reshape(2 * N_FF, t, FF_SH)


def _adamw(name, parts, w, m, v, tr=128, ride=None):
    ng = len(parts)
    n_src, r, c = parts[0].shape
    per = w.shape[1] // r
    assert w.shape[0] * per == ng and w.shape[2] == c
    tr = _tile(r, tr)
    nb = r // tr
    bc1 = 1.0 - ADAM_B1 ** ADAM_STEP
    bc2 = 1.0 - ADAM_B2 ** ADAM_STEP

    def body(*refs):
        p_refs = refs[:ng]
        w_ref, m_ref, v_ref, g_ref, d_ref, nm_ref, nv_ref = refs[ng:]
        grp = pl.program_id(0)

        def update(p_ref):
            g = p_ref[0].astype(F32)
            for s in range(1, n_src):
                g = g + p_ref[s].astype(F32)
            nm = ADAM_B1 * m_ref[...] + (1.0 - ADAM_B1) * g
            nv = ADAM_B2 * v_ref[...] + (1.0 - ADAM_B2) * jnp.square(g)
            g_ref[...] = g
            nm_ref[...] = nm
            nv_ref[...] = nv
            d_ref[...] = -ADAM_LR * ((nm / bc1) / (jnp.sqrt(nv / bc2) + ADAM_EPS) + ADAM_WD * w_ref[...])

        for k in range(ng):
            pl.when(grp == k)(functools.partial(update, p_refs[k]))

    p_specs = [pl.BlockSpec((n_src, tr, c), functools.partial(lambda gi, i, k: (0, jnp.where(gi == k, i, 0), 0), k=k))
               for k in range(ng)]
    spec = pl.BlockSpec((None, tr, c), lambda gi, i: (gi // per, (gi % per) * nb + i, 0))
    return _pcall(body, ride=ride, name=name, grid=(ng, nb), in_specs=p_specs + [spec, spec, spec],
                  out_specs=[spec] * 4, out_shape=[jax.ShapeDtypeStruct(w.shape, F32)] * 4,
                  compiler_params=_cparams(("parallel", "parallel")))(*parts, w, m, v)


def _sum_parts(name, parts, tr=256):
    _, r, c = parts.shape
    tr = _tile(r, tr)

    def body(p_ref, o_ref):
        g = p_ref[0]
        for s in range(1, parts.shape[0]):
            g = g + p_ref[s]
        o_ref[...] = g

    return _pcall(body, name=name, grid=(r // tr,),
                  in_specs=[pl.BlockSpec((parts.shape[0], tr, c), lambda i: (0, i, 0))],
                  out_specs=pl.BlockSpec((tr, c), lambda i: (i, 0)),
                  out_shape=jax.ShapeDtypeStruct((r, c), F32), compiler_params=_cparams(("parallel",)))(parts)


def _peer(k):
    x, y, c = lax.axis_index("x"), lax.axis_index("y"), lax.axis_index("c")
    return (x ^ ((k >> 2) & 1), y ^ ((k >> 1) & 1), c ^ (k & 1))


def _my_id():
    return 4 * lax.axis_index("x") + 2 * lax.axis_index("y") + lax.axis_index("c")


def _exchange(name, ride):
    n = len(ride.arrays)

    def body(*refs):
        ride.begin(refs[:n], refs[n:2 * n], refs[2 * n:])
        ride.finish(refs[:n], refs[n:2 * n], refs[2 * n:])

    hbm = pl.BlockSpec(memory_space=pltpu.HBM)
    return _pcall(body, name=name, in_specs=[hbm] * n, out_specs=[hbm] * n, out_shape=ride.out_shapes(),
                  scratch_shapes=ride.scratch())(*ride.arrays)


def _row(v):
    return v.reshape(1, -1)


def _time_major_heads(c, bsz, seq):
    return c.reshape(bsz, seq, BLK)[:, :, :N_HEADS].transpose(0, 2, 1)


def _no_ride(*_):
    return None


TWICE = [(D_MODEL, F32), (D_MODEL, BF16)]


def _both(fn):
    def run(*v):
        y = fn(*v)
        return y, y
    return run


def _layer_fwd(h, hb, p_l, w, bsz, seq, ride_of=_no_ride):
    t = bsz * seq
    zq = _mm_nn("z_proj_qkv", hb, w['w_in7'], n=QKV, out_dtype=BF16, ride=ride_of('z_proj_qkv'))
    zr = _mm_nn("z_proj_rest", hb, w['w_in7'], b_off=QKV, n=4 * D_MODEL, ride=ride_of('z_proj_rest'))
    fl = _mm_nn("f_proj", hb, w['w_inf'])
    logf, = _rowwise("logf_fwd", lambda f, b: (_f_logf(f, b),), [fl], [w['b_forget']], [(BLK, F32)], [])
    c = _scan("cumsum_fwd", None, logf, bsz, seq, reverse=False)
    ct = _time_major_heads(c, bsz, seq)
    cq, ck = ct[..., None], ct[:, :, None, :]
    att, attb, lse = _attn_fwd(zq, cq, ck, bsz, seq, ride=ride_of('attn_fwd'))
    xc = _conv_fwd(zr, w['conv_w'], w['conv_b'], bsz, seq)
    decay, = _rowwise("decay_fwd", lambda lam: (_f_decay(lam),), [w['rg_lambda']], [], [(D_MODEL, F32)], [])
    a, u = _gate_fwd(xc, w['rg_w_a'], w['rg_w_x'], w['rg_b_a'], w['rg_b_x'], decay, ride=ride_of('gate_fwd'))
    hs, hprev = _scan("lru_fwd", a, u, bsz, seq, reverse=False, with_prev=True, ride=ride_of('lru_fwd'))
    rnn, = _rowwise("rnn_out_fwd", lambda s, y: (_f_rnn_out(s, y),), [hs, (zr, OFF_RY, D_MODEL)], [],
                    [(D_MODEL, BF16)], [])
    ya = _mm_nn("branch_att", attb, w['w_branch_att'])
    yb = _mm_nn("branch_rnn", rnn, w['w_branch_rnn'])
    merged, = _rowwise("merge_fwd", lambda *v: (_f_merge(*v),),
                       [(zr, OFF_GA, D_MODEL), (zr, OFF_GB, D_MODEL), ya, yb], [w['b_merge0'], w['b_merge1']],
                       [(D_MODEL, BF16)], [])
    mix = _mm_nn("mix_out", merged, w['w_out'])
    h1, h1b = _rowwise("ln_mix_fwd", _both(_f_resid_ln), [h, mix], [w['ln_mix_g'], w['ln_mix_b']], TWICE, [])
    tm = _tile(t, 1024)
    hgu, act = _ffn_in_act(h1b, w['w_ffn_in'], ride=ride_of('ffn_in'))
    ffn = _mm("ffn_out", act, w['w_ffn_out'], grid=(t // tm, 1, N_FF),
              a_spec=pl.BlockSpec((None, tm, FF_SH), lambda i, j, s: (s, i, 0)),
              b_spec=pl.BlockSpec((None, FF_SH, D_MODEL), lambda i, j, s: (s, 0, 0)),
              o_spec=pl.BlockSpec((tm, D_MODEL), lambda i, j, s: (i, 0)),
              out_shape=jax.ShapeDtypeStruct((t, D_MODEL), F32), contract=NN, ride=ride_of('ffn_out'))
    h2, h2b = _rowwise("ln_ffn_fwd", _both(_f_resid_ln), [h1, ffn], [w['ln_ffn_g'], w['ln_ffn_b']], TWICE, [])
    gp = _mm_nn("ple_gate", h2b, w['w_ple_gate'])
    pe = _mm_nn("ple_proj", p_l, w['w_ple'])
    h3, h3b = _rowwise("ln_ple_fwd", _both(_f_ple), [h2, gp, pe],
                       [w['b_ple_gate'], w['ln_ple_g'], w['ln_ple_b']], TWICE, [])
    saved = dict(h=h, hb=hb, zq=zq, zr=zr, fl=fl, cq=cq, ck=ck, att=att, attb=attb, lse=lse, xc=xc, a=a, decay=decay,
                 hprev=hprev, hs=hs, rnn=rnn, ya=ya, yb=yb, merged=merged, mix=mix, h1=h1, h1b=h1b, hgu=hgu,
                 act=act, ffn=ffn, h2=h2, h2b=h2b, gp=gp, pe=pe)
    return h3, h3b, saved


def _layer_bwd(dh3, p_l, w, s, bsz, seq, ride_of=_no_ride):
    t = bsz * seq
    g = {}
    dh2, dgp, dpe, g['b_ple_gate'], g['ln_ple_g'], g['ln_ple_b'] = _vjp_rowwise(
        "ln_ple_bwd", _f_ple, [s['h2'], s['gp'], s['pe']], [w['b_ple_gate'], w['ln_ple_g'], w['ln_ple_b']], [dh3], 3,
        dtypes=[F32, BF16, BF16])
    g['w_ple_gate'] = _mm_tn("ple_gate_dw", s['h2b'], dgp, out_dtype=BF16)
    g['w_ple'] = _mm_tn("ple_proj_dw", p_l, dpe, out_dtype=BF16)
    dh2b = _mm_nt("ple_gate_dx", dgp, w['w_ple_gate'])
    dh1, dffn, g['ln_ffn_g'], g['ln_ffn_b'] = _ln_resid_bwd(
        "ln_ffn_bwd", s['h1'], s['ffn'], w['ln_ffn_g'], w['ln_ffn_b'], dh2, dh2b)
    tm = _tile(t, 1024)
    tk = _tile(t, 2048)
    g['w_ffn_out'] = _mm("ffn_out_dw", s['act'], dffn, grid=(N_FF, 1, t // tk),
                         a_spec=pl.BlockSpec((None, tk, FF_SH), lambda ss, j, k: (ss, k, 0)),
                         b_spec=pl.BlockSpec((tk, D_MODEL), lambda ss, j, k: (k, 0)),
                         o_spec=pl.BlockSpec((None, FF_SH, D_MODEL), lambda ss, j, k: (ss, 0, 0)),
                         out_shape=jax.ShapeDtypeStruct((N_FF, FF_SH, D_MODEL), BF16), contract=TN)
    dhgu = _ffn_out_dx_act(dffn, w['w_ffn_out'], s['hgu'], ride=ride_of('ffn_out_dx', g))
    g['w_ffn_in'] = _mm("ffn_in_dw", s['h1b'], dhgu, grid=(2 * N_FF, 1, t // tk),
                        a_spec=pl.BlockSpec((tk, D_MODEL), lambda ss, j, k: (k, 0)),
                        b_spec=pl.BlockSpec((None, tk, FF_SH), lambda ss, j, k: (ss, k, 0)),
                        o_spec=pl.BlockSpec((None, D_MODEL, FF_SH), lambda ss, j, k: (ss, 0, 0)),
                        out_shape=jax.ShapeDtypeStruct((2 * N_FF, D_MODEL, FF_SH), BF16), contract=TN,
                        ride=ride_of('ffn_in_dw', g))
    dh1b = _mm("ffn_in_dx", dhgu, w['w_ffn_in'], grid=(t // tm, 1, 2 * N_FF),
               a_spec=pl.BlockSpec((None, tm, FF_SH), lambda i, j, ss: (ss, i, 0)),
               b_spec=pl.BlockSpec((None, D_MODEL, FF_SH), lambda i, j, ss: (ss, 0, 0)),
               o_spec=pl.BlockSpec((tm, D_MODEL), lambda i, j, ss: (i, 0)),
               out_shape=jax.ShapeDtypeStruct((t, D_MODEL), F32), contract=NT, ride=ride_of('ffn_in_dx', g))
    dh, dmix, g['ln_mix_g'], g['ln_mix_b'] = _ln_resid_bwd(
        "ln_mix_bwd", s['h'], s['mix'], w['ln_mix_g'], w['ln_mix_b'], dh1, dh1b)
    g['w_out'] = _mm_tn("mix_out_dw", s['merged'], dmix, out_dtype=BF16)
    dmerged = _mm_nt("mix_out_dx", dmix, w['w_out'])
    z = s['zr']
    dga, dgb, dya, dyb, dbm0, dbm1 = _vjp_rowwise(
        "merge_bwd", _f_merge, [(z, OFF_GA, D_MODEL), (z, OFF_GB, D_MODEL), s['ya'], s['yb']],
        [w['b_merge0'], w['b_merge1']], [dmerged], 4, dtypes=[BF16] * 4)
    g['b_merge'] = jnp.concatenate([dbm0, dbm1], axis=0)
    g['w_branch_att'] = _mm_tn("branch_att_dw", s['attb'], dya, out_dtype=BF16)
    g['w_branch_rnn'] = _mm_tn("branch_rnn_dw", s['rnn'], dyb, out_dtype=BF16)
    datt = _mm_nt("branch_att_dx", dya, w['w_branch_att'], out_dtype=BF16)
    drnn = _mm_nt("branch_rnn_dx", dyb, w['w_branch_rnn'])
    dhs, dry = _vjp_rowwise("rnn_out_bwd", _f_rnn_out, [s['hs'], (z, OFF_RY, D_MODEL)], [], [drnn], 2,
                            dtypes=[F32, BF16])
    lam = _scan("lru_bwd", s['a'], dhs, bsz, seq, reverse=True)
    dxc, g['rg_w_a'], g['rg_w_x'], g['rg_b_a'], g['rg_b_x'], ddecay = _gate_bwd(
        s['xc'], w['rg_w_a'], w['rg_w_x'], w['rg_b_a'], w['rg_b_x'], s['decay'], s['hprev'], lam,
        ride=ride_of('gate_bwd', g))
    g['rg_lambda'], = _vjp_rowwise("decay_bwd", _f_decay, [w['rg_lambda']], [], [ddecay], 1)
    drx, g['conv_w'], g['conv_b'] = _conv_bwd(z, dxc, w['conv_w'], bsz, seq)
    dq, dk, dv, dcq, dck = _attn_bwd(s['zq'], s['att'], datt, s['lse'], s['cq'], s['ck'], bsz, seq,
                                     ride=ride_of('attn_bwd', g))
    dc = (dcq[:, :, :, 0] + dck.reshape(bsz, N_HEADS, seq)).transpose(0, 2, 1)
    dc = jnp.pad(dc, ((0, 0), (0, 0), (0, BLK - N_HEADS))).reshape(t, BLK)
    dlogf = _scan("cumsum_bwd", None, dc, bsz, seq, reverse=True)
    dfl, g['b_forget'] = _vjp_rowwise("logf_bwd", _f_logf, [s['fl']], [w['b_forget']], [dlogf], 1, dtypes=[BF16])
    dz = jnp.concatenate([dq, dk, dv, drx, dry, dga, dgb], axis=1)
    g['w_in7'] = _mm_tn("z_proj_dw", s['hb'], dz, out_dtype=BF16)
    g['w_inf'] = _mm_tn("f_proj_dw", s['hb'], dfl, out_dtype=BF16)
    dh = _mm_nt("z_proj_dx", dz, w['w_in7'], ride=ride_of('z_proj_dx', g), add=dh)
    dh = _mm_nt("f_proj_dx", dfl, w['w_inf'], add=dh)
    return dh, g


def _ln_resid_bwd(name, h, branch, gam, bet, d0, d1):
    def bwd(hv, bv, d0v, d1v, gv, btv):
        _, pull = jax.vjp(_f_resid_ln, hv, bv, gv, btv)
        dh, db, dg, dbt = pull(d0v + d1v)
        return dh, db, _colsum(dg), _colsum(dbt)

    return _rowwise(name, bwd, [h, branch, d0, d1], [gam, bet], [(D_MODEL, F32), (D_MODEL, BF16)],
                    [D_MODEL, D_MODEL], tm=256)


class _Schedule:
    FWD = {'z_proj_qkv': ['w_ffn_out'], 'z_proj_rest': ['w_branch_att', 'w_branch_rnn', 'w_out', 'w_ple_gate'],
           'attn_fwd': ['w_in'], 'ffn_in': ['w_ffn_in'], 'ffn_out': ['w_ple', 'conv_w', 'b_merge']}
    FIRST = ['w_in', 'conv_w', 'b_merge']
    OWN = {'z_proj_qkv': ['w_branch_att', 'w_branch_rnn', 'w_out'], 'z_proj_rest': ['w_ffn_in'],
           'attn_fwd': ['w_ffn_out', 'w_ple_gate', 'w_ple']}
    NEXT = {'attn_fwd': ['w_in'], 'gate_fwd': ['w_ffn_out'],
            'lru_fwd': ['w_branch_att', 'w_branch_rnn', 'w_out', 'w_ple_gate'],
            'ffn_in': ['w_ffn_in'], 'ffn_out': ['w_ple', 'conv_w', 'b_merge']}
    BWD = {'ffn_out_dx': ['w_ffn_out'], 'ffn_in_dw': ['w_ple_gate', 'w_ple'],
           'gate_bwd': ['w_out', 'w_branch_att', 'w_branch_rnn'],
           'attn_bwd': ['w_ffn_in', 'conv_w', 'b_merge']}

    def __init__(self, shards, depth):
        self.shards, self.depth = shards, depth
        self.gathered = [{} for _ in range(depth)]
        self.received = [{} for _ in range(depth)]
        self.pending = []
        self.deferred = None

    def gather_first(self):
        ride = _Ride([self.shards[n] for n in self.FIRST], gather=True, index=[0] * len(self.FIRST))
        self.gathered[0].update(zip(self.FIRST, _exchange("gather_first", ride)))

    def gather_ride(self, layer, kernel_name):
        own = self.OWN.get(kernel_name, []) if layer == 0 else []
        nxt = (self.NEXT if layer == 0 else self.FWD).get(kernel_name, []) if layer + 1 < self.depth else []
        items = [(n, 0) for n in own] + [(n, layer + 1) for n in nxt]
        if not items:
            return None
        ride = _Ride([self.shards[n] for n, _ in items], gather=True, index=[l for _, l in items])
        self.pending.append((ride, [(n, self.gathered[l]) for n, l in items]))
        return ride

    def _scatter(self, arrays, names, layer):
        ride = _Ride(arrays, gather=False)
        self.pending.append((ride, [(n, self.received[layer]) for n in names]))
        return ride

    def scatter_ride(self, layer, kernel_name, grads):
        if kernel_name == 'z_proj_dx':
            whole = _by_destination('w_in', grads)
            half = whole.shape[1] // 2
            self.deferred = (whole[:, half:], layer)
            return self._scatter([whole[:, :half]], ['w_in_a'], layer)
        if kernel_name == 'ffn_in_dx':
            if self.deferred is None:
                return None
            (late, from_layer), self.deferred = self.deferred, None
            return self._scatter([late], ['w_in_b'], from_layer)
        names = self.BWD[kernel_name]
        return self._scatter([_by_destination(n, grads) for n in names], names, layer)

    def collect(self):
        for ride, places in self.pending:
            if ride.result is not None:
                for (name, dst), res in zip(places, ride.result):
                    dst[name] = res
        self.pending = [(ride, places) for ride, places in self.pending if ride.result is None]


class _LayerWeights:
    SOURCE = {'w_in7': 'w_in', 'w_inf': 'w_in', 'b_merge0': 'b_merge', 'b_merge1': 'b_merge'}

    def __init__(self, sched, layer, replicated):
        self.sched, self.layer, self.made = sched, layer, dict(replicated)

    def __getitem__(self, key):
        if key not in self.made:
            self.sched.collect()
            src = self.SOURCE.get(key, key)
            self.made.update(_from_shards(src, self.sched.gathered[self.layer][src]))
        return self.made[key]


def _local_step(x2, tgt, p3, weights_of, depth, g_in, b_in, bsz, seq, sched=None):
    h, hb = _rowwise("ln_in_fwd", _both(_ln), [x2], [g_in, b_in], TWICE, [])
    p3 = p3.astype(BF16)
    saved, layer_w = [], []
    for l in range(depth):
        layer_w.append(weights_of(l))
        ride_of = functools.partial(sched.gather_ride, l) if sched else _no_ride
        h, hb, s = _layer_fwd(h, hb, p3[l], layer_w[l], bsz, seq, ride_of)
        if sched:
            sched.collect()
        saved.append(s)

    def loss_fn(y, tv):
        err = y - tv
        return err * (1.0 / D_MODEL), _colsum(jnp.square(err))

    dh, sq = _rowwise("loss", loss_fn, [h, tgt], [], [(D_MODEL, F32)], [D_MODEL])
    grads = [None] * depth
    for l in reversed(range(depth)):
        ride_of = functools.partial(sched.scatter_ride, l) if sched else _no_ride
        dh, grads[l] = _layer_bwd(dh, p3[l], layer_w[l], saved[l], bsz, seq, ride_of)
        if sched:
            sched.collect()
    dx, dg_in, db_in = _vjp_rowwise("ln_in_bwd", _ln, [x2], [g_in, b_in], [dh], 1)
    return sq, dx, grads, dg_in, db_in


def _from_shards(name, g):
    if name == 'w_in':
        wt = g.transpose(1, 0, 2).reshape(D_MODEL, N_IN)
        return {'w_in7': jnp.concatenate([wt[:, :3 * D_MODEL], wt[:, 3 * D_MODEL + N_HEADS:]], axis=1),
                'w_inf': jnp.pad(wt[:, 3 * D_MODEL:3 * D_MODEL + N_HEADS], ((0, 0), (0, BLK - N_HEADS)))}
    if name in ('w_branch_att', 'w_branch_rnn', 'w_out', 'w_ple_gate'):
        return {name: g.reshape(D_MODEL, D_MODEL)}
    if name == 'w_ffn_in':
        return {name: g}
    if name == 'w_ffn_out':
        return {name: g.reshape(N_FF, FF_SH, D_MODEL)}
    if name == 'b_merge':
        bm = g.transpose(1, 0, 2).reshape(2, D_MODEL)
        return {'b_merge0': bm[0:1], 'b_merge1': bm[1:2]}
    return {name: g.transpose(1, 0, 2).reshape(g.shape[1], D_MODEL)}


def _layer_weights(full):
    w = {}
    for name, g in full.items():
        w.update(_from_shards(name, g))
    return w


def _by_destination(name, gw):
    if name == 'w_in':
        g7, gf = gw['w_in7'], gw['w_inf']
        true = jnp.concatenate([g7[:, :3 * D_MODEL], gf[:, :N_HEADS], g7[:, 3 * D_MODEL:]], axis=1)
        return true.reshape(D_MODEL, N_DEV, IN_SH).transpose(1, 0, 2)
    g = gw[name]
    if name in ('w_branch_att', 'w_branch_rnn', 'w_out', 'w_ple_gate'):
        return g.reshape(N_DEV, D_MODEL // N_DEV, D_MODEL)
    if name == 'w_ffn_in':
        return g
    if name == 'w_ffn_out':
        return g.reshape(N_DEV, N_FF * FF_SH // N_DEV, D_MODEL)
    return g.reshape(g.shape[0], N_DEV, BLK).transpose(1, 0, 2)


def kernel(x, p, ln_in_g, ln_in_b, w_in, b_forget, conv_w, conv_b, rg_w_a, rg_b_a, rg_w_x, rg_b_x, rg_lambda, w_branch_att, w_branch_rnn, b_merge, w_out, ln_mix_g, ln_mix_b, w_ffn_in, w_ffn_out, ln_ffn_g, ln_ffn_b, w_ple, w_ple_gate, b_ple_gate, ln_ple_g, ln_ple_b, loss_target, m_ln_in_g, m_ln_in_b, m_w_in, m_b_forget, m_conv_w, m_conv_b, m_rg_w_a, m_rg_b_a, m_rg_w_x, m_rg_b_x, m_rg_lambda, m_w_branch_att, m_w_branch_rnn, m_b_merge, m_w_out, m_ln_mix_g, m_ln_mix_b, m_w_ffn_in, m_w_ffn_out, m_ln_ffn_g, m_ln_ffn_b, m_w_ple, m_w_ple_gate, m_b_ple_gate, m_ln_ple_g, m_ln_ple_b, v_ln_in_g, v_ln_in_b, v_w_in, v_b_forget, v_conv_w, v_conv_b, v_rg_w_a, v_rg_b_a, v_rg_w_x, v_rg_b_x, v_rg_lambda, v_w_branch_att, v_w_branch_rnn, v_b_merge, v_w_out, v_ln_mix_g, v_ln_mix_b, v_w_ffn_in, v_w_ffn_out, v_ln_ffn_g, v_ln_ffn_b, v_w_ple, v_w_ple_gate, v_b_ple_gate, v_ln_ple_g, v_ln_ple_b):
    env = dict(locals())
    wts = {n: env[n] for n in WEIGHTS}
    mom = {n: env['m_' + n] for n in WEIGHTS}
    var = {n: env['v_' + n] for n in WEIGHTS}
    bsz, seq, _ = x.shape
    depth = w_in.shape[0]
    t = bsz * seq
    x2, tgt = x.reshape(t, D_MODEL), loss_target.reshape(t, D_MODEL)
    p3 = p.reshape(depth, t, D_PLE)

    shard_names = SHARDED_BF16 + SHARDED_F32
    shards = {n: wts[n].astype(BF16) for n in SHARDED_BF16}
    shards.update({n: wts[n] for n in SHARDED_F32})
    sched = _Schedule(shards, depth)
    sched.gather_first()

    def weights_of(l):
        w = {n: _row(wts[n][l]) for n in ['conv_b', 'rg_b_a', 'rg_b_x', 'rg_lambda', 'ln_mix_g', 'ln_mix_b',
                                          'ln_ffn_g', 'ln_ffn_b', 'b_ple_gate', 'ln_ple_g', 'ln_ple_b']}
        w['b_forget'] = jnp.pad(_row(b_forget[l]), ((0, 0), (0, BLK - N_HEADS)))
        w['rg_w_a'], w['rg_w_x'] = rg_w_a[l], rg_w_x[l]
        return _LayerWeights(sched, l, w)

    g_in, b_in = _row(ln_in_g), _row(ln_in_b)
    sq, dx, grads, dg_in, db_in = _local_step(x2, tgt, p3, weights_of, depth, g_in, b_in, bsz, seq, sched)
    loss = lax.psum(0.5 * jnp.sum(sq) / D_MODEL, ("x", "y", "c"))
    grad_x = dx.reshape(bsz, seq, D_MODEL)

    out = {}

    def update(n, ride=None):
        shp = wts[n].shape
        view = lambda a: a
        if n == 'w_in':
            recv = [sched.received[l][half] for l in range(depth) for half in ('w_in_a', 'w_in_b')]
        else:
            recv = [sched.received[l][n] for l in range(depth)]
        if n in SHARDED_F32:
            recv = [jnp.stack(recv, axis=1).reshape(N_DEV, -1, shp[-1])]
            view = lambda a: a.reshape(1, -1, shp[-1])
        res = _adamw("adamw_" + n, recv, view(wts[n]), view(mom[n]), view(var[n]), ride=ride)
        out[n] = [r.reshape(shp) for r in res]

    def rep_grad(n):
        if n == 'ln_in_g':
            return dg_in.reshape(-1)
        if n == 'ln_in_b':
            return db_in.reshape(-1)
        return jnp.stack([grads[l][n].reshape(wts[n].shape[1:]) if n != 'b_forget'
                          else grads[l][n][0, :N_HEADS] for l in range(depth)]).reshape(-1)

    sizes = [int(wts[n].size) for n in REPLICATED]
    n_rows = [8 * (-(-sz // (8 * BLK))) for sz in sizes]
    total_rows = -(-sum(n_rows) // (N_DEV * 8)) * (N_DEV * 8)

    def as_rows(v, sz, nr):
        v = v.reshape(-1)
        return (jnp.pad(v, (0, nr * BLK - sz)) if nr * BLK != sz else v).reshape(nr, BLK)

    def pack(vals):
        parts = [as_rows(v, sz, nr) for v, sz, nr in zip(vals, sizes, n_rows)]
        parts.append(jnp.zeros((total_rows - sum(n_rows), BLK), F32))
        return jnp.concatenate(parts, axis=0)

    late, from_layer = sched.deferred
    last_w_in = _Ride([late], gather=False)
    update('w_ffn_in', ride=last_w_in)
    sched.received[from_layer]['w_in_b'], = last_w_in.result
    scatter_small = _Ride([pack([rep_grad(n) for n in REPLICATED]).reshape(N_DEV, total_rows // N_DEV, BLK)],
                          gather=False)
    update('w_branch_att', ride=scatter_small)
    gather_small = _Ride([_sum_parts("sum_small", scatter_small.result[0])], gather=True)
    update('w_out', ride=gather_small)
    for n in shard_names:
        if n not in out:
            update(n)
    g_rows = gather_small.result[0].reshape(total_rows, BLK)
    starts = [sum(n_rows[:i]) for i in range(len(n_rows))]
    for n, r0, sz, nr in zip(REPLICATED, starts, sizes, n_rows):
        shp = wts[n].shape
        as_one = (1, 1, sz) if len(shp) == 1 else (1, -1, shp[-1])
        g_n = g_rows[r0:r0 + nr]
        g_n = (g_n if nr * BLK == sz else g_n.reshape(-1)[:sz]).reshape(as_one)
        res = _adamw("adamw_" + n, [g_n], *[d[n].reshape(as_one) for d in (wts, mom, var)])
        out[n] = [r.reshape(shp) for r in res]

    return (loss, grad_x, *[out[n][k] for k in range(4) for n in WEIGHTS])
```

```python
import functools
import math

import jax
import jax.numpy as jnp
from jax import lax
from jax.experimental import pallas as pl
from jax.experimental.pallas import tpu as pltpu

F32 = jnp.float32
BF16 = jnp.bfloat16

N_DEV = 8
D_MODEL = 1024
N_HEADS = 8
HEAD_DIM = 128
N_BLK = 8
BLK = 128
CONV_W = 4
D_PLE = 256
FF_SH = 704
N_FF = 4
IN_SH = 897
N_IN = 7176
DEPTH = 4
RG_C = 8.0
ALPHA = float((2 * DEPTH) ** 0.25)
LN_EPS = 1e-5
SCALE = 1.0 / math.sqrt(HEAD_DIM)
NEG = -1e30
ADAM_LR, ADAM_B1, ADAM_B2, ADAM_EPS, ADAM_WD, ADAM_STEP = 0.001, 0.9, 0.999, 1e-08, 0.01, 10
QKV = 3 * D_MODEL
OFF_RX, OFF_RY, OFF_GA, OFF_GB = (i * D_MODEL for i in range(4))
V7X_VMEM_LIMIT = 48 * 1024 * 1024

WEIGHTS = ['ln_in_g', 'ln_in_b', 'w_in', 'b_forget', 'conv_w', 'conv_b', 'rg_w_a', 'rg_b_a', 'rg_w_x', 'rg_b_x',
           'rg_lambda', 'w_branch_att', 'w_branch_rnn', 'b_merge', 'w_out', 'ln_mix_g', 'ln_mix_b', 'w_ffn_in',
           'w_ffn_out', 'ln_ffn_g', 'ln_ffn_b', 'w_ple', 'w_ple_gate', 'b_ple_gate', 'ln_ple_g', 'ln_ple_b']
SHARDED_BF16 = ['w_in', 'w_branch_att', 'w_branch_rnn', 'w_out', 'w_ffn_in', 'w_ffn_out', 'w_ple', 'w_ple_gate']
SHARDED_F32 = ['conv_w', 'b_merge']
REPLICATED = [n for n in WEIGHTS if n not in SHARDED_BF16 and n not in SHARDED_F32]

NN = ((1,), (0,))
NT = ((1,), (1,))
TN = ((0,), (0,))


class _Ride:
    def __init__(self, arrays, *, gather, index=None):
        self.arrays, self.gather, self.index = list(arrays), gather, index
        self.result = None

    def _shard(self, ins, a):
        return ins[a] if self.index is None else ins[a].at[self.index[a]]

    def out_shapes(self):
        if not self.gather:
            return [jax.ShapeDtypeStruct(a.shape, a.dtype) for a in self.arrays]
        cut = 0 if self.index is None else 1
        return [jax.ShapeDtypeStruct((N_DEV,) + a.shape[cut:], a.dtype) for a in self.arrays]

    def scratch(self):
        n = len(self.arrays)
        return [pltpu.SemaphoreType.DMA((n * N_DEV,)), pltpu.SemaphoreType.DMA((n * N_DEV,)),
                pltpu.SemaphoreType.DMA((n,))]

    def _copy(self, a, k, src, dst, sems, to=None):
        send_sems, recv_sems, _ = sems
        return pltpu.make_async_remote_copy(
            src_ref=src, dst_ref=dst, send_sem=send_sems.at[a * N_DEV + k], recv_sem=recv_sems.at[a * N_DEV + k],
            device_id=_peer(k if to is None else to), device_id_type=pl.DeviceIdType.MESH)

    def begin(self, ins, outs, sems):
        me = _my_id()
        started = []
        for a in range(len(ins)):
            if self.gather:
                src = self._shard(ins, a)
                started.append(pltpu.make_async_copy(src, outs[a].at[me], sems[2].at[a]))
                started += [self._copy(a, k, src, outs[a].at[me], sems) for k in (1, 2, 4, 6)]
            else:
                started.append(pltpu.make_async_copy(ins[a].at[me], outs[a].at[me], sems[2].at[a]))
                started += [self._copy(a, k, ins[a].at[me ^ k], outs[a].at[me], sems) for k in range(1, N_DEV)]
        for cp in started:
            cp.start()

    def finish(self, ins, outs, sems):
        me = _my_id()
        for a in range(len(ins)):
            if self.gather:
                src = self._shard(ins, a)
                passed = []
                for k in (2, 4, 6):
                    block = outs[a].at[me ^ k]
                    self._copy(a, k, src, block, sems).wait_recv()
                    passed.append(self._copy(a, k + 1, block, block, sems, to=1))
                    passed[-1].start()
                for k in (1, 2, 4, 6):
                    self._copy(a, k, src, outs[a].at[me], sems).wait_send()
                self._copy(a, 1, src, outs[a].at[me ^ 1], sems).wait_recv()
                for cp in passed:
                    cp.wait()
                pltpu.make_async_copy(src, outs[a].at[me], sems[2].at[a]).wait()
            else:
                pltpu.make_async_copy(ins[a].at[me], outs[a].at[me], sems[2].at[a]).wait()
                for k in range(1, N_DEV):
                    self._copy(a, k, ins[a].at[me ^ k], outs[a].at[me], sems).wait()


def _pcall(body, ride=None, **kw):
    if ride is None:
        return pl.pallas_call(body, **kw)
    n = len(ride.arrays)
    grid = kw['grid']
    single = not isinstance(kw['out_shape'], (list, tuple))
    out_specs = [kw['out_specs']] if single else list(kw['out_specs'])
    out_shape = [kw['out_shape']] if single else list(kw['out_shape'])
    in_specs = list(kw['in_specs'])
    scratch = list(kw.get('scratch_shapes', ()))
    n_in, n_out, n_sc = len(in_specs), len(out_shape), len(scratch)
    hbm = pl.BlockSpec(memory_space=pltpu.HBM)

    def wrapped(*refs):
        ins, xin = refs[:n_in], refs[n_in:n_in + n]
        outs, xout = refs[n_in + n:n_in + n + n_out], refs[n_in + n + n_out:n_in + 2 * n + n_out]
        sc, sems = refs[n_in + 2 * n + n_out:n_in + 2 * n + n_out + n_sc], refs[-3:]
        ids = [pl.program_id(ax) for ax in range(len(grid))]
        first = functools.reduce(jnp.logical_and, [i == 0 for i in ids])
        last = functools.reduce(jnp.logical_and, [i == g - 1 for i, g in zip(ids, grid)])

        pl.when(first)(lambda: ride.begin(xin, xout, sems))
        body(*ins, *outs, *sc)
        pl.when(last)(lambda: ride.finish(xin, xout, sems))

    call = pl.pallas_call(wrapped, name=kw['name'], grid=grid, in_specs=in_specs + [hbm] * n,
                          out_specs=out_specs + [hbm] * n, out_shape=out_shape + ride.out_shapes(),
                          scratch_shapes=scratch + ride.scratch(), compiler_params=kw['compiler_params'])

    def run(*args):
        res = call(*args, *ride.arrays)
        ride.result = list(res[n_out:])
        return res[0] if single else list(res[:n_out])

    return run


def _tile(n, pref, mult=8):
    if n <= pref:
        return n
    t = (pref // mult) * mult
    while t >= mult:
        if n % t == 0:
            return t
        t -= mult
    return n


def _cparams(sem):
    return pltpu.CompilerParams(dimension_semantics=sem, vmem_limit_bytes=V7X_VMEM_LIMIT)


def _mm(name, a, b, *, grid, a_spec, b_spec, o_spec, out_shape, contract, ride=None, add=None):
    nk = grid[-1]
    in_out = out_shape.dtype == F32
    acc_shape = tuple(d for d in o_spec.block_shape if d is not None)

    def body(*refs):
        a_ref, b_ref = refs[0], refs[1]
        add_ref = refs[2] if add is not None else None
        o_ref = refs[3] if add is not None else refs[2]
        acc_ref = o_ref if (in_out or nk == 1) else refs[-1]
        k = pl.program_id(len(grid) - 1)
        part = lax.dot_general(a_ref[...].astype(BF16), b_ref[...].astype(BF16), (contract, ((), ())),
                               preferred_element_type=F32)
        if add_ref is not None:
            part = jnp.where(k == 0, part + add_ref[...], part) if nk > 1 else part + add_ref[...]
        if nk == 1:
            o_ref[...] = part.astype(o_ref.dtype)
            return

        @pl.when(k == 0)
        def _():
            acc_ref[...] = part

        @pl.when(k > 0)
        def _():
            acc_ref[...] += part

        if not in_out:
            @pl.when(k == nk - 1)
            def _():
                o_ref[...] = acc_ref[...].astype(o_ref.dtype)

    sem = ("parallel",) * (len(grid) - 1) + ("arbitrary",)
    scratch = [] if (in_out or nk == 1) else [pltpu.VMEM(acc_shape, F32)]
    in_specs, args = [a_spec, b_spec], [a, b]
    if add is not None:
        in_specs.append(o_spec)
        args.append(add)
    return _pcall(body, ride=ride, name=name, grid=grid, in_specs=in_specs, out_specs=o_spec,
                  out_shape=out_shape, scratch_shapes=scratch, compiler_params=_cparams(sem))(*args)


def _mm_nn(name, a, b, *, b_off=0, n=None, out_dtype=F32, tm=1024, tn=1024, tk=1024, ride=None):
    m, k = a.shape
    n = b.shape[1] if n is None else n
    tm, tn, tk = _tile(m, tm), _tile(n, tn, 128), _tile(k, tk, 128)
    no = b_off // tn
    return _mm(name, a, b, grid=(m // tm, n // tn, k // tk),
               a_spec=pl.BlockSpec((tm, tk), lambda i, j, kk: (i, kk)),
               b_spec=pl.BlockSpec((tk, tn), lambda i, j, kk: (kk, j + no)),
               o_spec=pl.BlockSpec((tm, tn), lambda i, j, kk: (i, j)),
               out_shape=jax.ShapeDtypeStruct((m, n), out_dtype), contract=NN, ride=ride)


def _mm_nt(name, a, b, *, out_dtype=F32, tm=1024, tn=1024, tk=1024, ride=None, add=None):
    m, k = a.shape
    n = b.shape[0]
    tm, tn, tk = _tile(m, tm), _tile(n, tn, 128), _tile(k, tk, 128)
    return _mm(name, a, b, grid=(m // tm, n // tn, k // tk),
               a_spec=pl.BlockSpec((tm, tk), lambda i, j, kk: (i, kk)),
               b_spec=pl.BlockSpec((tn, tk), lambda i, j, kk: (j, kk)),
               o_spec=pl.BlockSpec((tm, tn), lambda i, j, kk: (i, j)),
               out_shape=jax.ShapeDtypeStruct((m, n), out_dtype), contract=NT, ride=ride, add=add)


def _mm_tn(name, a, b, *, a_off=0, m=None, out_dtype=F32, tm=1024, tn=1024, tk=2048, ride=None):
    t, n = b.shape
    m = a.shape[1] if m is None else m
    tm, tn, tk = _tile(m, tm, 128), _tile(n, tn, 128), _tile(t, tk)
    mo = a_off // tm
    return _mm(name, a, b, grid=(m // tm, n // tn, t // tk),
               a_spec=pl.BlockSpec((tk, tm), lambda i, j, kk: (kk, i + mo)),
               b_spec=pl.BlockSpec((tk, tn), lambda i, j, kk: (kk, j)),
               o_spec=pl.BlockSpec((tm, tn), lambda i, j, kk: (i, j)),
               out_shape=jax.ShapeDtypeStruct((m, n), out_dtype), contract=TN, ride=ride)


def _rowwise(name, fn, rows, params, out_rows, out_reds, tm=512):
    rows = [r if isinstance(r, tuple) else (r, 0, r.shape[1]) for r in rows]
    t = rows[0][0].shape[0]
    tm = _tile(t, tm)
    in_specs = []
    for _, off, w in rows:
        in_specs.append(pl.BlockSpec((tm, w), functools.partial(lambda i, cb: (i, cb), cb=off // w)))
    for p in params:
        in_specs.append(pl.BlockSpec((1, p.shape[1]), lambda i: (0, 0)))
    out_specs = [pl.BlockSpec((tm, w), lambda i: (i, 0)) for w, _ in out_rows]
    out_specs += [pl.BlockSpec((1, w), lambda i: (0, 0)) for w in out_reds]
    out_shape = [jax.ShapeDtypeStruct((t, w), dt) for w, dt in out_rows]
    out_shape += [jax.ShapeDtypeStruct((1, w), F32) for w in out_reds]
    nr, npar, nor = len(rows), len(params), len(out_rows)

    def body(*refs):
        ins, outs = refs[:nr + npar], refs[nr + npar:]
        vals = [r[...].astype(F32) for r in ins[:nr]]
        vals += [jnp.broadcast_to(r[...], (tm, r.shape[1])) for r in ins[nr:]]
        res = fn(*vals)
        step = pl.program_id(0)
        for o, v in zip(outs[:nor], res[:nor]):
            o[...] = v.astype(o.dtype)
        for o, v in zip(outs[nor:], res[nor:]):
            _accumulate(o, v, step)

    res = _pcall(body, name=name, grid=(t // tm,), in_specs=in_specs, out_specs=out_specs, out_shape=out_shape,
                 compiler_params=_cparams(("arbitrary",)))(*[r[0] for r in rows], *params)
    return res


def _accumulate(o_ref, v, step, at=Ellipsis):
    @pl.when(step == 0)
    def _():
        o_ref[at] = v

    @pl.when(step > 0)
    def _():
        o_ref[at] += v


def _colsum(v):
    return jnp.sum(v, axis=0, keepdims=True)


def _vjp_rowwise(name, fn, rows, params, cots, n_row_grads, tm=256, dtypes=None):
    nr, npar, nc = len(rows), len(params), len(cots)

    def bwd(*vals):
        prim, par, ct = vals[:nr], vals[nr + nc:], vals[nr:nr + nc]
        _, pull = jax.vjp(fn, *prim, *par)
        grads = pull(tuple(ct) if nc > 1 else ct[0])
        return tuple(grads[:n_row_grads]) + tuple(_colsum(g) for g in grads[nr:])

    dtypes = [F32] * n_row_grads if dtypes is None else dtypes
    widths = [(r[2] if isinstance(r, tuple) else r.shape[1], dt) for r, dt in zip(rows[:n_row_grads], dtypes)]
    return _rowwise(name, bwd, list(rows) + list(cots), params, widths, [p.shape[1] for p in params], tm=tm)


def _ln(s, g, b):
    mu = jnp.mean(s, axis=-1, keepdims=True)
    var = jnp.mean(jnp.square(s - mu), axis=-1, keepdims=True)
    return (s - mu) * lax.rsqrt(var + LN_EPS) * g + b


def _softplus(x):
    return jnp.maximum(x, 0.0) + jnp.log1p(jnp.exp(-jnp.abs(x)))


def _expm1(x):
    series = x * (1.0 + x * (1.0 / 2 + x * (1.0 / 6 + x * (1.0 / 24 + x * (1.0 / 120 + x * (1.0 / 720))))))
    return jnp.where(jnp.abs(x) < 0.25, series, jnp.exp(x) - 1.0)


def _f_resid_ln(h, branch, g, b):
    return _ln(ALPHA * h + branch, g, b)


def _f_ple(h, gp, pe, bpg, g, b):
    return _ln(ALPHA * h + jax.nn.sigmoid(gp + bpg) * pe, g, b)


def _f_merge(ga, gb, ya, yb, bm0, bm1):
    return jax.nn.sigmoid(ga + bm0) * ya + jax.nn.sigmoid(gb + bm1) * yb


def _f_rnn_out(hs, ry):
    return hs * jax.nn.gelu(ry, approximate=True)


def _f_logf(fl, bf):
    return -_softplus(-(fl + bf))


def _f_decay(lam):
    return -RG_C * _softplus(-lam)


def _f_gate(xc, ra, ia, decay, ba, bx):
    r = jax.nn.sigmoid(ra + ba)
    i = jax.nn.sigmoid(ia + bx)
    log_a = decay * r
    a = jnp.exp(log_a)
    mult = jnp.sqrt(-_expm1(2.0 * log_a))
    return a, mult * (i * xc)


def _f_act(hg, hu):
    return jax.nn.silu(hg) * hu


ATT_BLOCK = 512
ATT_HEADS_PER_STEP = 2
SCAN_ROWS = 128
GATE_BLOCKS_PER_STEP = 2


def _scores(q, k, cq, ck, diagonal):
    s = lax.dot_general(q, k, (NT, ((), ())), preferred_element_type=F32) * SCALE
    s = s + cq - ck
    if diagonal:
        row = lax.broadcasted_iota(jnp.int32, s.shape, 0)
        col = lax.broadcasted_iota(jnp.int32, s.shape, 1)
        s = jnp.where(col <= row, s, NEG)
    return s


def _dscores(p, do, o, v):
    dob = do.astype(BF16)
    delta = jnp.sum(dob.astype(F32) * o, axis=1, keepdims=True)
    dp = lax.dot_general(dob, v.astype(BF16), (NT, ((), ())), preferred_element_type=F32)
    return p * (dp - delta)


def _attn_fwd(z, cq, ck, bsz, seq, ride=None):
    t = bsz * seq
    tq = _tile(seq, ATT_BLOCK)
    nq = seq // tq

    hp = ATT_HEADS_PER_STEP

    def body(q_ref, k_ref, v_ref, cq_ref, ck_ref, o_ref, ob_ref, lse_ref):
        for hh, i in [(hh, i) for hh in range(hp) for i in range(nq)]:
            lanes = slice(hh * HEAD_DIM, (hh + 1) * HEAD_DIM)
            rows = slice(i * tq, (i + 1) * tq)
            q = q_ref[rows, lanes].astype(BF16)
            cqi = cq_ref[hh, rows, :]

            def step(j, carry, diagonal, q=q, cqi=cqi, hh=hh, lanes=lanes):
                m, l, acc = carry
                keys = slice(j * tq, (j + 1) * tq)
                s = _scores(q, k_ref[keys, lanes].astype(BF16), cqi, ck_ref[hh, j:j + 1, :], diagonal)
                m_new = jnp.maximum(m, jnp.max(s, axis=1, keepdims=True))
                alpha = jnp.exp(m - m_new)
                p = jnp.exp(s - m_new)
                p_hi = p.astype(BF16)
                p_lo = (p - p_hi.astype(F32)).astype(BF16)
                vb = v_ref[keys, lanes].astype(BF16)
                pv = lax.dot_general(p_hi, vb, (NN, ((), ())), preferred_element_type=F32)
                pv = pv + lax.dot_general(p_lo, vb, (NN, ((), ())), preferred_element_type=F32)
                return m_new, alpha * l + jnp.sum(p, axis=1, keepdims=True), alpha * acc + pv

            carry = (jnp.full((tq, 1), NEG, F32), jnp.zeros((tq, 1), F32), jnp.zeros((tq, HEAD_DIM), F32))
            for j in range(i):
                carry = step(j, carry, False)
            m, l, acc = step(i, carry, True)
            o = acc / l
            o_ref[rows, lanes] = o
            ob_ref[rows, lanes] = o.astype(BF16)
            lse_ref[hh, rows, :] = m + jnp.log(l)

    groups = N_HEADS // hp
    head = (seq, hp * HEAD_DIM)
    in_specs = [
        pl.BlockSpec(head, lambda b, g: (b, g)),
        pl.BlockSpec(head, lambda b, g: (b, groups + g)),
        pl.BlockSpec(head, lambda b, g: (b, 2 * groups + g)),
        pl.BlockSpec((None, hp, seq, 1), lambda b, g: (b, g, 0, 0)),
        pl.BlockSpec((None, hp, nq, tq), lambda b, g: (b, g, 0, 0)),
    ]
    out_specs = [pl.BlockSpec(head, lambda b, g: (b, g)), pl.BlockSpec(head, lambda b, g: (b, g)),
                 pl.BlockSpec((None, hp, seq, 1), lambda b, g: (b, g, 0, 0))]
    out_shape = [jax.ShapeDtypeStruct((t, D_MODEL), F32), jax.ShapeDtypeStruct((t, D_MODEL), BF16),
                 jax.ShapeDtypeStruct((bsz, N_HEADS, seq, 1), F32)]
    return _pcall(body, ride=ride, name="attn_fwd", grid=(bsz, groups), in_specs=in_specs, out_specs=out_specs,
                  out_shape=out_shape, compiler_params=_cparams(("parallel", "parallel")))(
                      z, z, z, cq, ck.reshape(bsz, N_HEADS, nq, tq))


def _attn_bwd(z, att, datt, lse, cq, ck, bsz, seq, ride=None):
    t = bsz * seq
    tq = _tile(seq, ATT_BLOCK)
    nq = seq // tq

    def body(q_ref, k_ref, v_ref, o_ref, do_ref, lse_ref, cq_ref, ck_ref,
             dq_ref, dk_ref, dv_ref, dcq_ref, dck_ref, dq_sc):
        dq_sc[...] = jnp.zeros_like(dq_sc)
        dcq_ref[...] = jnp.zeros_like(dcq_ref)
        for j in range(nq):
            keys = slice(j * tq, (j + 1) * tq)
            kb = k_ref[keys, :].astype(BF16)
            vb = v_ref[keys, :].astype(BF16)
            ckj = ck_ref[j:j + 1, :]

            def step(i, carry, diagonal, kb=kb, vb=vb, ckj=ckj):
                dk, dv, dc = carry
                rows = slice(i * tq, (i + 1) * tq)
                qb = q_ref[rows, :].astype(BF16)
                do = do_ref[rows, :]
                s = _scores(qb, kb, cq_ref[rows, :], ckj, diagonal)
                p = jnp.exp(s - lse_ref[rows, :])
                ds = _dscores(p, do, o_ref[rows, :], vb)
                dsb = (ds * SCALE).astype(BF16)
                dq_sc[rows, :] += lax.dot_general(dsb, kb, (NN, ((), ())), preferred_element_type=F32)
                dcq_ref[rows, :] += jnp.sum(ds, axis=1, keepdims=True)
                dv = dv + lax.dot_general(p.astype(BF16), do.astype(BF16), (TN, ((), ())),
                                          preferred_element_type=F32)
                dk = dk + lax.dot_general(dsb, qb, (TN, ((), ())), preferred_element_type=F32)
                return dk, dv, dc - jnp.sum(ds, axis=0, keepdims=True)

            zero = jnp.zeros((tq, HEAD_DIM), F32)
            carry = step(j, (zero, zero, jnp.zeros((1, tq), F32)), True)
            for i in range(j + 1, nq):
                carry = step(i, carry, False)
            dk, dv, dck_ref[j:j + 1, :] = carry
            dk_ref[keys, :] = dk.astype(BF16)
            dv_ref[keys, :] = dv.astype(BF16)
        dq_ref[...] = dq_sc[...].astype(BF16)

    head = (seq, HEAD_DIM)
    hmap = lambda b, h: (b, h)
    col = pl.BlockSpec((None, None, seq, 1), lambda b, h: (b, h, 0, 0))
    row = pl.BlockSpec((None, None, nq, tq), lambda b, h: (b, h, 0, 0))
    in_specs = [pl.BlockSpec(head, hmap),
                pl.BlockSpec(head, lambda b, h: (b, N_HEADS + h)),
                pl.BlockSpec(head, lambda b, h: (b, 2 * N_HEADS + h)),
                pl.BlockSpec(head, hmap), pl.BlockSpec(head, hmap), col, col, row]
    big = jax.ShapeDtypeStruct((t, D_MODEL), BF16)
    return _pcall(body, ride=ride, name="attn_bwd", grid=(bsz, N_HEADS), in_specs=in_specs,
                  out_specs=[pl.BlockSpec(head, hmap)] * 3 + [col, row],
                  out_shape=[big, big, big, jax.ShapeDtypeStruct((bsz, N_HEADS, seq, 1), F32),
                             jax.ShapeDtypeStruct((bsz, N_HEADS, nq, tq), F32)],
                  scratch_shapes=[pltpu.VMEM(head, F32)],
                  compiler_params=_cparams(("parallel", "parallel")))(
                      z, z, z, att, datt, lse, cq, ck.reshape(bsz, N_HEADS, nq, tq))


def _scan(name, a, u, bsz, seq, *, reverse, with_prev=False, tb=512, ride=None):
    c = u.shape[1]
    tb = _tile(seq, tb)
    nb = seq // tb
    rc = SCAN_ROWS if tb % SCAN_ROWS == 0 else tb
    has_a = a is not None

    def body(*refs):
        if has_a:
            a_ref, u_ref = refs[0], refs[1]
            rest = refs[2:]
        else:
            u_ref = refs[0]
            rest = refs[1:]
        outs = rest[:2] if with_prev else rest[:1]
        carry_sc, afirst_sc = rest[-2], rest[-1]
        step = pl.program_id(1)

        @pl.when(step == 0)
        def _():
            carry_sc[...] = jnp.zeros_like(carry_sc)
            afirst_sc[...] = jnp.zeros_like(afirst_sc)

        row = lax.broadcasted_iota(jnp.int32, (rc, BLK), 0)
        pieces = list(range(tb // rc))
        for ls in range(c // BLK):
            lanes = slice(ls * BLK, (ls + 1) * BLK)
            carry = carry_sc[:, lanes]
            afirst = afirst_sc[:, lanes]
            for pc in (reversed(pieces) if reverse else pieces):
                rows = slice(pc * rc, (pc + 1) * rc)
                uu = u_ref[rows, lanes]
                if has_a:
                    aa = a_ref[rows, lanes]
                    coef = jnp.where(row < rc - 1, pltpu.roll(aa, rc - 1, 0), afirst) if reverse else aa
                k = 1
                while k < rc:
                    shift = rc - k if reverse else k
                    keep = (row < rc - k) if reverse else (row >= k)
                    uu_sh = jnp.where(keep, pltpu.roll(uu, shift, 0), 0.0)
                    if has_a:
                        uu = coef * uu_sh + uu
                        coef = coef * jnp.where(keep, pltpu.roll(coef, shift, 0), 1.0)
                    else:
                        uu = uu + uu_sh
                    k *= 2
                h = uu + coef * carry if has_a else uu + carry
                outs[0][rows, lanes] = h
                if with_prev:
                    outs[1][rows, lanes] = jnp.where(row >= 1, pltpu.roll(h, 1, 0), carry)
                edge = pc * rc if reverse else (pc + 1) * rc - 1
                carry = outs[0][edge:edge + 1, lanes]
                if has_a and reverse:
                    afirst = a_ref[edge:edge + 1, lanes]
            carry_sc[:, lanes] = carry
            if has_a and reverse:
                afirst_sc[:, lanes] = afirst

    if reverse:
        imap = lambda b, s: (b * nb + nb - 1 - s, 0)
    else:
        imap = lambda b, s: (b * nb + s, 0)
    spec = pl.BlockSpec((tb, c), imap)
    n_in = 2 if has_a else 1
    n_out = 2 if with_prev else 1
    res = _pcall(body, ride=ride, name=name, grid=(bsz, nb), in_specs=[spec] * n_in, out_specs=[spec] * n_out,
                 out_shape=[jax.ShapeDtypeStruct(u.shape, F32)] * n_out,
                 scratch_shapes=[pltpu.VMEM((1, c), F32), pltpu.VMEM((1, c), F32)],
                 compiler_params=_cparams(("parallel", "arbitrary")))(*([a, u] if has_a else [u]))
    return res if with_prev else res[0]


def _conv_fwd(z, w, b, bsz, seq, tb=512):
    c = D_MODEL
    t = bsz * seq
    tb = _tile(seq, tb)
    nb = seq // tb

    def body(x_ref, w_ref, b_ref, o_ref, tail_sc):
        step = pl.program_id(1)

        @pl.when(step == 0)
        def _():
            tail_sc[...] = jnp.zeros_like(tail_sc)

        x = x_ref[...]
        row8 = lax.broadcasted_iota(jnp.int32, (8, c), 0)
        tail = tail_sc[...]
        acc = w_ref[CONV_W - 1:CONV_W, :] * x + b_ref[...]
        for sh in range(1, CONV_W):
            xs = pltpu.roll(x, sh, 0)
            top = jnp.where(row8 < sh, pltpu.roll(tail, sh, 0), xs[0:8, :])
            xs = jnp.concatenate([top, xs[8:, :]], axis=0) if tb > 8 else top
            acc = acc + w_ref[CONV_W - 1 - sh:CONV_W - sh, :] * xs
        o_ref[...] = acc
        tail_sc[...] = x_ref[tb - 8:tb, :]

    return _pcall(body, name="conv_fwd", grid=(bsz, nb),
                  in_specs=[pl.BlockSpec((tb, c), lambda bb, s: (bb * nb + s, OFF_RX // c)),
                            pl.BlockSpec((CONV_W, c), lambda bb, s: (0, 0)),
                            pl.BlockSpec((1, c), lambda bb, s: (0, 0))],
                  out_specs=pl.BlockSpec((tb, c), lambda bb, s: (bb * nb + s, 0)),
                  out_shape=jax.ShapeDtypeStruct((t, c), F32),
                  scratch_shapes=[pltpu.VMEM((8, c), F32)],
                  compiler_params=_cparams(("parallel", "arbitrary")))(z, w, b)


def _conv_bwd(z, dxc, w, bsz, seq, tb=512):
    c = D_MODEL
    t = bsz * seq
    tb = _tile(seq, tb)
    nb = seq // tb

    def body(x_ref, g_ref, w_ref, dx_ref, dw_ref, db_ref, head_sc):
        bb, step = pl.program_id(0), pl.program_id(1)

        @pl.when(step == 0)
        def _():
            head_sc[...] = jnp.zeros_like(head_sc)

        x, g = x_ref[...], g_ref[...]
        row8 = lax.broadcasted_iota(jnp.int32, (8, c), 0)
        head = head_sc[...]
        dx = w_ref[CONV_W - 1:CONV_W, :] * g
        dws = [None] * CONV_W
        dws[CONV_W - 1] = _colsum(g * x)
        for sh in range(1, CONV_W):
            gs = pltpu.roll(g, tb - sh, 0)
            bot = jnp.where(row8 >= 8 - sh, pltpu.roll(head, 8 - sh, 0), gs[tb - 8:tb, :])
            gs = jnp.concatenate([gs[:tb - 8, :], bot], axis=0) if tb > 8 else bot
            dx = dx + w_ref[CONV_W - 1 - sh:CONV_W - sh, :] * gs
            dws[CONV_W - 1 - sh] = _colsum(gs * x)
        dx_ref[...] = dx.astype(dx_ref.dtype)
        first = (bb == 0) & (step == 0)
        dw = jnp.concatenate(dws, axis=0)
        db = _colsum(g)

        @pl.when(first)
        def _():
            dw_ref[...] = dw
            db_ref[...] = db

        @pl.when(jnp.logical_not(first))
        def _():
            dw_ref[...] += dw
            db_ref[...] += db

        head_sc[...] = g_ref[0:8, :]

    rmap = lambda bb, s: (bb * nb + nb - 1 - s, 0)
    return _pcall(body, name="conv_bwd", grid=(bsz, nb),
                  in_specs=[pl.BlockSpec((tb, c), lambda bb, s: (bb * nb + nb - 1 - s, OFF_RX // c)),
                            pl.BlockSpec((tb, c), rmap),
                            pl.BlockSpec((CONV_W, c), lambda bb, s: (0, 0))],
                  out_specs=[pl.BlockSpec((tb, c), rmap),
                             pl.BlockSpec((CONV_W, c), lambda bb, s: (0, 0)),
                             pl.BlockSpec((1, c), lambda bb, s: (0, 0))],
                  out_shape=[jax.ShapeDtypeStruct((t, c), BF16), jax.ShapeDtypeStruct((CONV_W, c), F32),
                             jax.ShapeDtypeStruct((1, c), F32)],
                  scratch_shapes=[pltpu.VMEM((8, c), F32)],
                  compiler_params=_cparams(("arbitrary", "arbitrary")))(z, dxc, w)


def _gate_fwd(xc, w_a, w_x, b_a, b_x, lam, tm=1024, ride=None):
    t = xc.shape[0]
    tm = _tile(t, tm)

    gb = GATE_BLOCKS_PER_STEP

    def body(xc_ref, wa_ref, wx_ref, ba_ref, bx_ref, lam_ref, a_ref, u_ref):
        for s in range(gb):
            lanes = slice(s * BLK, (s + 1) * BLK)
            xc_b = xc_ref[:, lanes]
            xb = xc_b.astype(BF16)
            ra = lax.dot_general(xb, wa_ref[s].astype(BF16), (NN, ((), ())), preferred_element_type=F32)
            ia = lax.dot_general(xb, wx_ref[s].astype(BF16), (NN, ((), ())), preferred_element_type=F32)
            a, u = _f_gate(xc_b, ra, ia, lam_ref[:, lanes], ba_ref[:, lanes], bx_ref[:, lanes])
            a_ref[:, lanes] = a
            u_ref[:, lanes] = u

    row = pl.BlockSpec((tm, gb * BLK), lambda n, i: (i, n))
    wsp = pl.BlockSpec((gb, BLK, BLK), lambda n, i: (n, 0, 0))
    vec = pl.BlockSpec((1, gb * BLK), lambda n, i: (0, n))
    return _pcall(body, ride=ride, name="gate_fwd", grid=(N_BLK // gb, t // tm),
                  in_specs=[row, wsp, wsp, vec, vec, vec],
                  out_specs=[row, row], out_shape=[jax.ShapeDtypeStruct((t, D_MODEL), F32)] * 2,
                  compiler_params=_cparams(("parallel", "parallel")))(xc, w_a, w_x, b_a, b_x, lam)


def _gate_bwd(xc, w_a, w_x, b_a, b_x, lam, hprev, du, tm=512, ride=None):
    t = xc.shape[0]
    tm = _tile(t, tm)
    gb = GATE_BLOCKS_PER_STEP

    def body(xc_ref, wa_ref, wx_ref, ba_ref, bx_ref, lam_ref, hp_ref, du_ref,
             dxc_ref, dwa_ref, dwx_ref, dba_ref, dbx_ref, dlam_ref):
        step = pl.program_id(1)
        for s in range(gb):
            lanes = (slice(None), slice(s * BLK, (s + 1) * BLK))
            xc_b = xc_ref[lanes]
            xb = xc_b.astype(BF16)
            wa, wx = wa_ref[s].astype(BF16), wx_ref[s].astype(BF16)
            ra = lax.dot_general(xb, wa, (NN, ((), ())), preferred_element_type=F32)
            ia = lax.dot_general(xb, wx, (NN, ((), ())), preferred_element_type=F32)
            full = lambda r: jnp.broadcast_to(r[lanes], (tm, BLK))
            _, pull = jax.vjp(_f_gate, xc_b, ra, ia, full(lam_ref), full(ba_ref), full(bx_ref))
            du_b = du_ref[lanes]
            dxc, dra, dia, dlam, dba, dbx = pull((du_b * hp_ref[lanes], du_b))
            drb, dib = dra.astype(BF16), dia.astype(BF16)
            dxc = dxc + lax.dot_general(drb, wa, (NT, ((), ())), preferred_element_type=F32)
            dxc = dxc + lax.dot_general(dib, wx, (NT, ((), ())), preferred_element_type=F32)
            dxc_ref[lanes] = dxc
            _accumulate(dwa_ref, lax.dot_general(xb, drb, (TN, ((), ())), preferred_element_type=F32), step, at=s)
            _accumulate(dwx_ref, lax.dot_general(xb, dib, (TN, ((), ())), preferred_element_type=F32), step, at=s)
            _accumulate(dba_ref, _colsum(dba), step, at=lanes)
            _accumulate(dbx_ref, _colsum(dbx), step, at=lanes)
            _accumulate(dlam_ref, _colsum(dlam), step, at=lanes)

    row = pl.BlockSpec((tm, gb * BLK), lambda n, i: (i, n))
    wsp = pl.BlockSpec((gb, BLK, BLK), lambda n, i: (n, 0, 0))
    vec = pl.BlockSpec((1, gb * BLK), lambda n, i: (0, n))
    wshape = jax.ShapeDtypeStruct((N_BLK, BLK, BLK), F32)
    vshape = jax.ShapeDtypeStruct((1, D_MODEL), F32)
    return _pcall(body, ride=ride, name="gate_bwd", grid=(N_BLK // gb, t // tm),
                  in_specs=[row, wsp, wsp, vec, vec, vec, row, row],
                  out_specs=[row, wsp, wsp, vec, vec, vec],
                  out_shape=[jax.ShapeDtypeStruct((t, D_MODEL), F32), wshape, wshape, vshape, vshape, vshape],
                  compiler_params=_cparams(("parallel", "arbitrary")))(xc, w_a, w_x, b_a, b_x, lam, hprev, du)


def _ffn_in_act(a, w, tm=1024, ride=None):
    t = a.shape[0]
    tm = _tile(t, tm)

    def body(a_ref, wg_ref, wu_ref, hgu_ref, act_ref):
        ab = a_ref[...].astype(BF16)
        hg = lax.dot_general(ab, wg_ref[...].astype(BF16), (NN, ((), ())), preferred_element_type=F32)
        hu = lax.dot_general(ab, wu_ref[...].astype(BF16), (NN, ((), ())), preferred_element_type=F32)
        hgu_ref[0] = hg
        hgu_ref[1] = hu
        act_ref[...] = _f_act(hg, hu).astype(act_ref.dtype)

    wspec = lambda off: pl.BlockSpec((None, D_MODEL, FF_SH), lambda i, s: (s + off, 0, 0))
    hgu, act = _pcall(body, ride=ride, name="ffn_in", grid=(t // tm, N_FF),
                      in_specs=[pl.BlockSpec((tm, D_MODEL), lambda i, s: (i, 0)), wspec(0), wspec(N_FF)],
                      out_specs=[pl.BlockSpec((2, None, tm, FF_SH), lambda i, s: (0, s, i, 0)),
                                 pl.BlockSpec((None, tm, FF_SH), lambda i, s: (s, i, 0))],
                      out_shape=[jax.ShapeDtypeStruct((2, N_FF, t, FF_SH), F32),
                                 jax.ShapeDtypeStruct((N_FF, t, FF_SH), BF16)],
                      compiler_params=_cparams(("parallel", "parallel")))(a, w, w)
    return hgu.reshape(2 * N_FF, t, FF_SH), act


def _ffn_out_dx_act(d, w, hgu, tm=1024, ride=None):
    t = d.shape[0]
    tm = _tile(t, tm)

    def body(d_ref, w_ref, hg_ref, hu_ref, o_ref):
        dact = lax.dot_general(d_ref[...].astype(BF16), w_ref[...].astype(BF16), (NT, ((), ())),
                               preferred_element_type=F32)
        _, pull = jax.vjp(_f_act, hg_ref[...], hu_ref[...])
        dhg, dhu = pull(dact)
        o_ref[0] = dhg.astype(o_ref.dtype)
        o_ref[1] = dhu.astype(o_ref.dtype)

    hspec = lambda off: pl.BlockSpec((None, tm, FF_SH), lambda i, s: (s + off, i, 0))
    res = _pcall(body, ride=ride, name="ffn_out_dx", grid=(t // tm, N_FF),
                 in_specs=[pl.BlockSpec((tm, D_MODEL), lambda i, s: (i, 0)),
                           pl.BlockSpec((None, FF_SH, D_MODEL), lambda i, s: (s, 0, 0)), hspec(0), hspec(N_FF)],
                 out_specs=pl.BlockSpec((2, None, tm, FF_SH), lambda i, s: (0, s, i, 0)),
                 out_shape=jax.ShapeDtypeStruct((2, N_FF, t, FF_SH), BF16),
                 compiler_params=_cparams(("parallel", "parallel")))(d, w, hgu, hgu)
    return res.reshape(2 * N_FF, t, FF_SH)


def _adamw(name, parts, w, m, v, tr=128, ride=None):
    ng = len(parts)
    n_src, r, c = parts[0].shape
    per = w.shape[1] // r
    assert w.shape[0] * per == ng and w.shape[2] == c
    tr = _tile(r, tr)
    nb = r // tr
    bc1 = 1.0 - ADAM_B1 ** ADAM_STEP
    bc2 = 1.0 - ADAM_B2 ** ADAM_STEP

    def body(*refs):
        p_refs = refs[:ng]
        w_ref, m_ref, v_ref, g_ref, d_ref, nm_ref, nv_ref = refs[ng:]
        grp = pl.program_id(0)

        def update(p_ref):
            g = p_ref[0].astype(F32)
            for s in range(1, n_src):
                g = g + p_ref[s].astype(F32)
            nm = ADAM_B1 * m_ref[...] + (1.0 - ADAM_B1) * g
            nv = ADAM_B2 * v_ref[...] + (1.0 - ADAM_B2) * jnp.square(g)
            g_ref[...] = g
            nm_ref[...] = nm
            nv_ref[...] = nv
            d_ref[...] = -ADAM_LR * ((nm / bc1) / (jnp.sqrt(nv / bc2) + ADAM_EPS) + ADAM_WD * w_ref[...])

        for k in range(ng):
            pl.when(grp == k)(functools.partial(update, p_refs[k]))

    p_specs = [pl.BlockSpec((n_src, tr, c), functools.partial(lambda gi, i, k: (0, jnp.where(gi == k, i, 0), 0), k=k))
               for k in range(ng)]
    spec = pl.BlockSpec((None, tr, c), lambda gi, i: (gi // per, (gi % per) * nb + i, 0))
    return _pcall(body, ride=ride, name=name, grid=(ng, nb), in_specs=p_specs + [spec, spec, spec],
                  out_specs=[spec] * 4, out_shape=[jax.ShapeDtypeStruct(w.shape, F32)] * 4,
                  compiler_params=_cparams(("parallel", "parallel")))(*parts, w, m, v)


def _sum_parts(name, parts, tr=256):
    _, r, c = parts.shape
    tr = _tile(r, tr)

    def body(p_ref, o_ref):
        g = p_ref[0]
        for s in range(1, parts.shape[0]):
            g = g + p_ref[s]
        o_ref[...] = g

    return _pcall(body, name=name, grid=(r // tr,),
                  in_specs=[pl.BlockSpec((parts.shape[0], tr, c), lambda i: (0, i, 0))],
                  out_specs=pl.BlockSpec((tr, c), lambda i: (i, 0)),
                  out_shape=jax.ShapeDtypeStruct((r, c), F32), compiler_params=_cparams(("parallel",)))(parts)


def _peer(k):
    x, y, c = lax.axis_index("x"), lax.axis_index("y"), lax.axis_index("c")
    return (x ^ ((k >> 2) & 1), y ^ ((k >> 1) & 1), c ^ (k & 1))


def _my_id():
    return 4 * lax.axis_index("x") + 2 * lax.axis_index("y") + lax.axis_index("c")


def _exchange(name, ride):
    n = len(ride.arrays)

    def body(*refs):
        ride.begin(refs[:n], refs[n:2 * n], refs[2 * n:])
        ride.finish(refs[:n], refs[n:2 * n], refs[2 * n:])

    hbm = pl.BlockSpec(memory_space=pltpu.HBM)
    return _pcall(body, name=name, in_specs=[hbm] * n, out_specs=[hbm] * n, out_shape=ride.out_shapes(),
                  scratch_shapes=ride.scratch())(*ride.arrays)


def _row(v):
    return v.reshape(1, -1)


def _time_major_heads(c, bsz, seq):
    return c.reshape(bsz, seq, BLK)[:, :, :N_HEADS].transpose(0, 2, 1)


def _no_ride(*_):
    return None


TWICE = [(D_MODEL, F32), (D_MODEL, BF16)]


def _both(fn):
    def run(*v):
        y = fn(*v)
        return y, y
    return run


def _layer_fwd(h, hb, p_l, w, bsz, seq, ride_of=_no_ride):
    t = bsz * seq
    zq = _mm_nn("z_proj_qkv", hb, w['w_in7'], n=QKV, out_dtype=BF16, ride=ride_of('z_proj_qkv'))
    zr = _mm_nn("z_proj_rest", hb, w['w_in7'], b_off=QKV, n=4 * D_MODEL, ride=ride_of('z_proj_rest'))
    fl = _mm_nn("f_proj", hb, w['w_inf'])
    logf, = _rowwise("logf_fwd", lambda f, b: (_f_logf(f, b),), [fl], [w['b_forget']], [(BLK, F32)], [])
    c = _scan("cumsum_fwd", None, logf, bsz, seq, reverse=False)
    ct = _time_major_heads(c, bsz, seq)
    cq, ck = ct[..., None], ct[:, :, None, :]
    att, attb, lse = _attn_fwd(zq, cq, ck, bsz, seq, ride=ride_of('attn_fwd'))
    xc = _conv_fwd(zr, w['conv_w'], w['conv_b'], bsz, seq)
    decay, = _rowwise("decay_fwd", lambda lam: (_f_decay(lam),), [w['rg_lambda']], [], [(D_MODEL, F32)], [])
    a, u = _gate_fwd(xc, w['rg_w_a'], w['rg_w_x'], w['rg_b_a'], w['rg_b_x'], decay, ride=ride_of('gate_fwd'))
    hs, hprev = _scan("lru_fwd", a, u, bsz, seq, reverse=False, with_prev=True, ride=ride_of('lru_fwd'))
    rnn, = _rowwise("rnn_out_fwd", lambda s, y: (_f_rnn_out(s, y),), [hs, (zr, OFF_RY, D_MODEL)], [],
                    [(D_MODEL, BF16)], [])
    ya = _mm_nn("branch_att", attb, w['w_branch_att'])
    yb = _mm_nn("branch_rnn", rnn, w['w_branch_rnn'])
    merged, = _rowwise("merge_fwd", lambda *v: (_f_merge(*v),),
                       [(zr, OFF_GA, D_MODEL), (zr, OFF_GB, D_MODEL), ya, yb], [w['b_merge0'], w['b_merge1']],
                       [(D_MODEL, BF16)], [])
    mix = _mm_nn("mix_out", merged, w['w_out'])
    h1, h1b = _rowwise("ln_mix_fwd", _both(_f_resid_ln), [h, mix], [w['ln_mix_g'], w['ln_mix_b']], TWICE, [])
    tm = _tile(t, 1024)
    hgu, act = _ffn_in_act(h1b, w['w_ffn_in'], ride=ride_of('ffn_in'))
    ffn = _mm("ffn_out", act, w['w_ffn_out'], grid=(t // tm, 1, N_FF),
              a_spec=pl.BlockSpec((None, tm, FF_SH), lambda i, j, s: (s, i, 0)),
              b_spec=pl.BlockSpec((None, FF_SH, D_MODEL), lambda i, j, s: (s, 0, 0)),
              o_spec=pl.BlockSpec((tm, D_MODEL), lambda i, j, s: (i, 0)),
              out_shape=jax.ShapeDtypeStruct((t, D_MODEL), F32), contract=NN, ride=ride_of('ffn_out'))
    h2, h2b = _rowwise("ln_ffn_fwd", _both(_f_resid_ln), [h1, ffn], [w['ln_ffn_g'], w['ln_ffn_b']], TWICE, [])
    gp = _mm_nn("ple_gate", h2b, w['w_ple_gate'])
    pe = _mm_nn("ple_proj", p_l, w['w_ple'])
    h3, h3b = _rowwise("ln_ple_fwd", _both(_f_ple), [h2, gp, pe],
                       [w['b_ple_gate'], w['ln_ple_g'], w['ln_ple_b']], TWICE, [])
    saved = dict(h=h, hb=hb, zq=zq, zr=zr, fl=fl, cq=cq, ck=ck, att=att, attb=attb, lse=lse, xc=xc, a=a, decay=decay,
                 hprev=hprev, hs=hs, rnn=rnn, ya=ya, yb=yb, merged=merged, mix=mix, h1=h1, h1b=h1b, hgu=hgu,
                 act=act, ffn=ffn, h2=h2, h2b=h2b, gp=gp, pe=pe)
    return h3, h3b, saved


def _layer_bwd(dh3, p_l, w, s, bsz, seq, ride_of=_no_ride):
    t = bsz * seq
    g = {}
    dh2, dgp, dpe, g['b_ple_gate'], g['ln_ple_g'], g['ln_ple_b'] = _vjp_rowwise(
        "ln_ple_bwd", _f_ple, [s['h2'], s['gp'], s['pe']], [w['b_ple_gate'], w['ln_ple_g'], w['ln_ple_b']], [dh3], 3,
        dtypes=[F32, BF16, BF16])
    g['w_ple_gate'] = _mm_tn("ple_gate_dw", s['h2b'], dgp, out_dtype=BF16)
    g['w_ple'] = _mm_tn("ple_proj_dw", p_l, dpe, out_dtype=BF16)
    dh2b = _mm_nt("ple_gate_dx", dgp, w['w_ple_gate'])
    dh1, dffn, g['ln_ffn_g'], g['ln_ffn_b'] = _ln_resid_bwd(
        "ln_ffn_bwd", s['h1'], s['ffn'], w['ln_ffn_g'], w['ln_ffn_b'], dh2, dh2b)
    tm = _tile(t, 1024)
    tk = _tile(t, 2048)
    g['w_ffn_out'] = _mm("ffn_out_dw", s['act'], dffn, grid=(N_FF, 1, t // tk),
                         a_spec=pl.BlockSpec((None, tk, FF_SH), lambda ss, j, k: (ss, k, 0)),
                         b_spec=pl.BlockSpec((tk, D_MODEL), lambda ss, j, k: (k, 0)),
                         o_spec=pl.BlockSpec((None, FF_SH, D_MODEL), lambda ss, j, k: (ss, 0, 0)),
                         out_shape=jax.ShapeDtypeStruct((N_FF, FF_SH, D_MODEL), BF16), contract=TN)
    dhgu = _ffn_out_dx_act(dffn, w['w_ffn_out'], s['hgu'], ride=ride_of('ffn_out_dx', g))
    g['w_ffn_in'] = _mm("ffn_in_dw", s['h1b'], dhgu, grid=(2 * N_FF, 1, t // tk),
                        a_spec=pl.BlockSpec((tk, D_MODEL), lambda ss, j, k: (k, 0)),
                        b_spec=pl.BlockSpec((None, tk, FF_SH), lambda ss, j, k: (ss, k, 0)),
                        o_spec=pl.BlockSpec((None, D_MODEL, FF_SH), lambda ss, j, k: (ss, 0, 0)),
                        out_shape=jax.ShapeDtypeStruct((2 * N_FF, D_MODEL, FF_SH), BF16), contract=TN,
                        ride=ride_of('ffn_in_dw', g))
    dh1b = _mm("ffn_in_dx", dhgu, w['w_ffn_in'], grid=(t // tm, 1, 2 * N_FF),
               a_spec=pl.BlockSpec((None, tm, FF_SH), lambda i, j, ss: (ss, i, 0)),
               b_spec=pl.BlockSpec((None, D_MODEL, FF_SH), lambda i, j, ss: (ss, 0, 0)),
               o_spec=pl.BlockSpec((tm, D_MODEL), lambda i, j, ss: (i, 0)),
               out_shape=jax.ShapeDtypeStruct((t, D_MODEL), F32), contract=NT, ride=ride_of('ffn_in_dx', g))
    dh, dmix, g['ln_mix_g'], g['ln_mix_b'] = _ln_resid_bwd(
        "ln_mix_bwd", s['h'], s['mix'], w['ln_mix_g'], w['ln_mix_b'], dh1, dh1b)
    g['w_out'] = _mm_tn("mix_out_dw", s['merged'], dmix, out_dtype=BF16)
    dmerged = _mm_nt("mix_out_dx", dmix, w['w_out'])
    z = s['zr']
    dga, dgb, dya, dyb, dbm0, dbm1 = _vjp_rowwise(
        "merge_bwd", _f_merge, [(z, OFF_GA, D_MODEL), (z, OFF_GB, D_MODEL), s['ya'], s['yb']],
        [w['b_merge0'], w['b_merge1']], [dmerged], 4, dtypes=[BF16] * 4)
    g['b_merge'] = jnp.concatenate([dbm0, dbm1], axis=0)
    g['w_branch_att'] = _mm_tn("branch_att_dw", s['attb'], dya, out_dtype=BF16)
    g['w_branch_rnn'] = _mm_tn("branch_rnn_dw", s['rnn'], dyb, out_dtype=BF16)
    datt = _mm_nt("branch_att_dx", dya, w['w_branch_att'], out_dtype=BF16)
    drnn = _mm_nt("branch_rnn_dx", dyb, w['w_branch_rnn'])
    dhs, dry = _vjp_rowwise("rnn_out_bwd", _f_rnn_out, [s['hs'], (z, OFF_RY, D_MODEL)], [], [drnn], 2,
                            dtypes=[F32, BF16])
    lam = _scan("lru_bwd", s['a'], dhs, bsz, seq, reverse=True)
    dxc, g['rg_w_a'], g['rg_w_x'], g['rg_b_a'], g['rg_b_x'], ddecay = _gate_bwd(
        s['xc'], w['rg_w_a'], w['rg_w_x'], w['rg_b_a'], w['rg_b_x'], s['decay'], s['hprev'], lam,
        ride=ride_of('gate_bwd', g))
    g['rg_lambda'], = _vjp_rowwise("decay_bwd", _f_decay, [w['rg_lambda']], [], [ddecay], 1)
    drx, g['conv_w'], g['conv_b'] = _conv_bwd(z, dxc, w['conv_w'], bsz, seq)
    dq, dk, dv, dcq, dck = _attn_bwd(s['zq'], s['att'], datt, s['lse'], s['cq'], s['ck'], bsz, seq,
                                     ride=ride_of('attn_bwd', g))
    dc = (dcq[:, :, :, 0] + dck.reshape(bsz, N_HEADS, seq)).transpose(0, 2, 1)
    dc = jnp.pad(dc, ((0, 0), (0, 0), (0, BLK - N_HEADS))).reshape(t, BLK)
    dlogf = _scan("cumsum_bwd", None, dc, bsz, seq, reverse=True)
    dfl, g['b_forget'] = _vjp_rowwise("logf_bwd", _f_logf, [s['fl']], [w['b_forget']], [dlogf], 1, dtypes=[BF16])
    dz = jnp.concatenate([dq, dk, dv, drx, dry, dga, dgb], axis=1)
    g['w_in7'] = _mm_tn("z_proj_dw", s['hb'], dz, out_dtype=BF16)
    g['w_inf'] = _mm_tn("f_proj_dw", s['hb'], dfl, out_dtype=BF16)
    dh = _mm_nt("z_proj_dx", dz, w['w_in7'], ride=ride_of('z_proj_dx', g), add=dh)
    dh = _mm_nt("f_proj_dx", dfl, w['w_inf'], add=dh)
    return dh, g


def _ln_resid_bwd(name, h, branch, gam, bet, d0, d1):
    def bwd(hv, bv, d0v, d1v, gv, btv):
        _, pull = jax.vjp(_f_resid_ln, hv, bv, gv, btv)
        dh, db, dg, dbt = pull(d0v + d1v)
        return dh, db, _colsum(dg), _colsum(dbt)

    return _rowwise(name, bwd, [h, branch, d0, d1], [gam, bet], [(D_MODEL, F32), (D_MODEL, BF16)],
                    [D_MODEL, D_MODEL], tm=256)


class _Schedule:
    FWD = {'z_proj_qkv': ['w_ffn_out'], 'z_proj_rest': ['w_branch_att', 'w_branch_rnn', 'w_out', 'w_ple_gate'],
           'attn_fwd': ['w_in'], 'ffn_in': ['w_ffn_in'], 'ffn_out': ['w_ple', 'conv_w', 'b_merge']}
    FIRST = ['w_in', 'conv_w', 'b_merge']
    OWN = {'z_proj_qkv': ['w_branch_att', 'w_branch_rnn', 'w_out'], 'z_proj_rest': ['w_ffn_in'],
           'attn_fwd': ['w_ffn_out', 'w_ple_gate', 'w_ple']}
    NEXT = {'attn_fwd': ['w_in'], 'gate_fwd': ['w_ffn_out'],
            'lru_fwd': ['w_branch_att', 'w_branch_rnn', 'w_out', 'w_ple_gate'],
            'ffn_in': ['w_ffn_in'], 'ffn_out': ['w_ple', 'conv_w', 'b_merge']}
    BWD = {'ffn_out_dx': ['w_ffn_out'], 'ffn_in_dw': ['w_ple_gate', 'w_ple'],
           'gate_bwd': ['w_out', 'w_branch_att', 'w_branch_rnn'],
           'attn_bwd': ['w_ffn_in', 'conv_w', 'b_merge']}

    def __init__(self, shards, depth):
        self.shards, self.depth = shards, depth
        self.gathered = [{} for _ in range(depth)]
        self.received = [{} for _ in range(depth)]
        self.pending = []
        self.deferred = None

    def gather_first(self):
        ride = _Ride([self.shards[n] for n in self.FIRST], gather=True, index=[0] * len(self.FIRST))
        self.gathered[0].update(zip(self.FIRST, _exchange("gather_first", ride)))

    def gather_ride(self, layer, kernel_name):
        own = self.OWN.get(kernel_name, []) if layer == 0 else []
        nxt = (self.NEXT if layer == 0 else self.FWD).get(kernel_name, []) if layer + 1 < self.depth else []
        items = [(n, 0) for n in own] + [(n, layer + 1) for n in nxt]
        if not items:
            return None
        ride = _Ride([self.shards[n] for n, _ in items], gather=True, index=[l for _, l in items])
        self.pending.append((ride, [(n, self.gathered[l]) for n, l in items]))
        return ride

    def _scatter(self, arrays, names, layer):
        ride = _Ride(arrays, gather=False)
        self.pending.append((ride, [(n, self.received[layer]) for n in names]))
        return ride

    def scatter_ride(self, layer, kernel_name, grads):
        if kernel_name == 'z_proj_dx':
            whole = _by_destination('w_in', grads)
            half = whole.shape[1] // 2
            self.deferred = (whole[:, half:], layer)
            return self._scatter([whole[:, :half]], ['w_in_a'], layer)
        if kernel_name == 'ffn_in_dx':
            if self.deferred is None:
                return None
            (late, from_layer), self.deferred = self.deferred, None
            return self._scatter([late], ['w_in_b'], from_layer)
        names = self.BWD[kernel_name]
        return self._scatter([_by_destination(n, grads) for n in names], names, layer)

    def collect(self):
        for ride, places in self.pending:
            if ride.result is not None:
                for (name, dst), res in zip(places, ride.result):
                    dst[name] = res
        self.pending = [(ride, places) for ride, places in self.pending if ride.result is None]


class _LayerWeights:
    SOURCE = {'w_in7': 'w_in', 'w_inf': 'w_in', 'b_merge0': 'b_merge', 'b_merge1': 'b_merge'}

    def __init__(self, sched, layer, replicated):
        self.sched, self.layer, self.made = sched, layer, dict(replicated)

    def __getitem__(self, key):
        if key not in self.made:
            self.sched.collect()
            src = self.SOURCE.get(key, key)
            self.made.update(_from_shards(src, self.sched.gathered[self.layer][src]))
        return self.made[key]


def _local_step(x2, tgt, p3, weights_of, depth, g_in, b_in, bsz, seq, sched=None):
    h, hb = _rowwise("ln_in_fwd", _both(_ln), [x2], [g_in, b_in], TWICE, [])
    p3 = p3.astype(BF16)
    saved, layer_w = [], []
    for l in range(depth):
        layer_w.append(weights_of(l))
        ride_of = functools.partial(sched.gather_ride, l) if sched else _no_ride
        h, hb, s = _layer_fwd(h, hb, p3[l], layer_w[l], bsz, seq, ride_of)
        if sched:
            sched.collect()
        saved.append(s)

    def loss_fn(y, tv):
        err = y - tv
        return err * (1.0 / D_MODEL), _colsum(jnp.square(err))

    dh, sq = _rowwise("loss", loss_fn, [h, tgt], [], [(D_MODEL, F32)], [D_MODEL])
    grads = [None] * depth
    for l in reversed(range(depth)):
        ride_of = functools.partial(sched.scatter_ride, l) if sched else _no_ride
        dh, grads[l] = _layer_bwd(dh, p3[l], layer_w[l], saved[l], bsz, seq, ride_of)
        if sched:
            sched.collect()
    dx, dg_in, db_in = _vjp_rowwise("ln_in_bwd", _ln, [x2], [g_in, b_in], [dh], 1)
    return sq, dx, grads, dg_in, db_in


def _from_shards(name, g):
    if name == 'w_in':
        wt = g.transpose(1, 0, 2).reshape(D_MODEL, N_IN)
        return {'w_in7': jnp.concatenate([wt[:, :3 * D_MODEL], wt[:, 3 * D_MODEL + N_HEADS:]], axis=1),
                'w_inf': jnp.pad(wt[:, 3 * D_MODEL:3 * D_MODEL + N_HEADS], ((0, 0), (0, BLK - N_HEADS)))}
    if name in ('w_branch_att', 'w_branch_rnn', 'w_out', 'w_ple_gate'):
        return {name: g.reshape(D_MODEL, D_MODEL)}
    if name == 'w_ffn_in':
        return {name: g}
    if name == 'w_ffn_out':
        return {name: g.reshape(N_FF, FF_SH, D_MODEL)}
    if name == 'b_merge':
        bm = g.transpose(1, 0, 2).reshape(2, D_MODEL)
        return {'b_merge0': bm[0:1], 'b_merge1': bm[1:2]}
    return {name: g.transpose(1, 0, 2).reshape(g.shape[1], D_MODEL)}


def _layer_weights(full):
    w = {}
    for name, g in full.items():
        w.update(_from_shards(name, g))
    return w


def _by_destination(name, gw):
    if name == 'w_in':
        g7, gf = gw['w_in7'], gw['w_inf']
        true = jnp.concatenate([g7[:, :3 * D_MODEL], gf[:, :N_HEADS], g7[:, 3 * D_MODEL:]], axis=1)
        return true.reshape(D_MODEL, N_DEV, IN_SH).transpose(1, 0, 2)
    g = gw[name]
    if name in ('w_branch_att', 'w_branch_rnn', 'w_out', 'w_ple_gate'):
        return g.reshape(N_DEV, D_MODEL // N_DEV, D_MODEL)
    if name == 'w_ffn_in':
        return g
    if name == 'w_ffn_out':
        return g.reshape(N_DEV, N_FF * FF_SH // N_DEV, D_MODEL)
    return g.reshape(g.shape[0], N_DEV, BLK).transpose(1, 0, 2)


def kernel(x, p, ln_in_g, ln_in_b, w_in, b_forget, conv_w, conv_b, rg_w_a, rg_b_a, rg_w_x, rg_b_x, rg_lambda, w_branch_att, w_branch_rnn, b_merge, w_out, ln_mix_g, ln_mix_b, w_ffn_in, w_ffn_out, ln_ffn_g, ln_ffn_b, w_ple, w_ple_gate, b_ple_gate, ln_ple_g, ln_ple_b, loss_target, m_ln_in_g, m_ln_in_b, m_w_in, m_b_forget, m_conv_w, m_conv_b, m_rg_w_a, m_rg_b_a, m_rg_w_x, m_rg_b_x, m_rg_lambda, m_w_branch_att, m_w_branch_rnn, m_b_merge, m_w_out, m_ln_mix_g, m_ln_mix_b, m_w_ffn_in, m_w_ffn_out, m_ln_ffn_g, m_ln_ffn_b, m_w_ple, m_w_ple_gate, m_b_ple_gate, m_ln_ple_g, m_ln_ple_b, v_ln_in_g, v_ln_in_b, v_w_in, v_b_forget, v_conv_w, v_conv_b, v_rg_w_a, v_rg_b_a, v_rg_w_x, v_rg_b_x, v_rg_lambda, v_w_branch_att, v_w_branch_rnn, v_b_merge, v_w_out, v_ln_mix_g, v_ln_mix_b, v_w_ffn_in, v_w_ffn_out, v_ln_ffn_g, v_ln_ffn_b, v_w_ple, v_w_ple_gate, v_b_ple_gate, v_ln_ple_g, v_ln_ple_b):
    env = dict(locals())
    wts = {n: env[n] for n in WEIGHTS}
    mom = {n: env['m_' + n] for n in WEIGHTS}
    var = {n: env['v_' + n] for n in WEIGHTS}
    bsz, seq, _ = x.shape
    depth = w_in.shape[0]
    t = bsz * seq
    x2, tgt = x.reshape(t, D_MODEL), loss_target.reshape(t, D_MODEL)
    p3 = p.reshape(depth, t, D_PLE)

    shard_names = SHARDED_BF16 + SHARDED_F32
    shards = {n: wts[n].astype(BF16) for n in SHARDED_BF16}
    shards.update({n: wts[n] for n in SHARDED_F32})
    sched = _Schedule(shards, depth)
    sched.gather_first()

    def weights_of(l):
        w = {n: _row(wts[n][l]) for n in ['conv_b', 'rg_b_a', 'rg_b_x', 'rg_lambda', 'ln_mix_g', 'ln_mix_b',
                                          'ln_ffn_g', 'ln_ffn_b', 'b_ple_gate', 'ln_ple_g', 'ln_ple_b']}
        w['b_forget'] = jnp.pad(_row(b_forget[l]), ((0, 0), (0, BLK - N_HEADS)))
        w['rg_w_a'], w['rg_w_x'] = rg_w_a[l], rg_w_x[l]
        return _LayerWeights(sched, l, w)

    g_in, b_in = _row(ln_in_g), _row(ln_in_b)
    sq, dx, grads, dg_in, db_in = _local_step(x2, tgt, p3, weights_of, depth, g_in, b_in, bsz, seq, sched)
    loss = lax.psum(0.5 * jnp.sum(sq) / D_MODEL, ("x", "y", "c"))
    grad_x = dx.reshape(bsz, seq, D_MODEL)

    out = {}

    def update(n, ride=None):
        shp = wts[n].shape
        view = lambda a: a
        if n == 'w_in':
            recv = [sched.received[l][half] for l in range(depth) for half in ('w_in_a', 'w_in_b')]
        else:
            recv = [sched.received[l][n] for l in range(depth)]
        if n in SHARDED_F32:
            recv = [jnp.stack(recv, axis=1).reshape(N_DEV, -1, shp[-1])]
            view = lambda a: a.reshape(1, -1, shp[-1])
        res = _adamw("adamw_" + n, recv, view(wts[n]), view(mom[n]), view(var[n]), ride=ride)
        out[n] = [r.reshape(shp) for r in res]

    def rep_grad(n):
        if n == 'ln_in_g':
            return dg_in.reshape(-1)
        if n == 'ln_in_b':
            return db_in.reshape(-1)
        return jnp.stack([grads[l][n].reshape(wts[n].shape[1:]) if n != 'b_forget'
                          else grads[l][n][0, :N_HEADS] for l in range(depth)]).reshape(-1)

    sizes = [int(wts[n].size) for n in REPLICATED]
    n_rows = [8 * (-(-sz // (8 * BLK))) for sz in sizes]
    total_rows = -(-sum(n_rows) // (N_DEV * 8)) * (N_DEV * 8)

    def as_rows(v, sz, nr):
        v = v.reshape(-1)
        return (jnp.pad(v, (0, nr * BLK - sz)) if nr * BLK != sz else v).reshape(nr, BLK)

    def pack(vals):
        parts = [as_rows(v, sz, nr) for v, sz, nr in zip(vals, sizes, n_rows)]
        parts.append(jnp.zeros((total_rows - sum(n_rows), BLK), F32))
        return jnp.concatenate(parts, axis=0)

    late, from_layer = sched.deferred
    last_w_in = _Ride([late], gather=False)
    update('w_ffn_in', ride=last_w_in)
    sched.received[from_layer]['w_in_b'], = last_w_in.result
    scatter_small = _Ride([pack([rep_grad(n) for n in REPLICATED]).reshape(N_DEV, total_rows // N_DEV, BLK)],
                          gather=False)
    update('w_branch_att', ride=scatter_small)
    gather_small = _Ride([_sum_parts("sum_small", scatter_small.result[0])], gather=True)
    update('w_out', ride=gather_small)
    for n in shard_names:
        if n not in out:
            update(n)
    g_rows = gather_small.result[0].reshape(total_rows, BLK)
    starts = [sum(n_rows[:i]) for i in range(len(n_rows))]
    for n, r0, sz, nr in zip(REPLICATED, starts, sizes, n_rows):
        shp = wts[n].shape
        as_one = (1, 1, sz) if len(shp) == 1 else (1, -1, shp[-1])
        g_n = g_rows[r0:r0 + nr]
        g_n = (g_n if nr * BLK == sz else g_n.reshape(-1)[:sz]).reshape(as_one)
        res = _adamw("adamw_" + n, [g_n], *[d[n].reshape(as_one) for d in (wts, mom, var)])
        out[n] = [r.reshape(shp) for r in res]

    return (loss, grad_x, *[out[n][k] for k in range(4) for n in WEIGHTS])
```
